```python
import jax, jax.numpy as jnp
from jax import lax
import numpy as np

D_MODEL = 1024
BATCH = 8
SEQ = 4096
DEPTH = 2

D_MIX = D_MODEL
HEAD_DIM = 64
SGU_DIM = 3 * D_MIX // 8
SGU_HEADS = SGU_DIM // HEAD_DIM
CONV_DIM = 3 * D_MIX // 8
POOL_DIM = D_MIX - SGU_DIM - CONV_DIM
POOL_WINDOWS = (2, 4, 8, 16)
POOL_GROUPS = len(POOL_WINDOWS)
POOL_GROUP_DIM = POOL_DIM // POOL_GROUPS
CHUNK = 128
CONV_WIDTH = 31
D_IN = 2 * SGU_DIM + 2 * CONV_DIM + POOL_DIM
D_FF = 2816
FFN_RESIDUAL_WEIGHT = 0.5
EPS = 1e-6

kernel_name = "hybrid_macaron_sgu_conv_pool"


def rms_norm(x, g):
    xf = x.astype(jnp.float32)
    y = xf * lax.rsqrt(jnp.mean(xf * xf, axis=-1, keepdims=True) + EPS)
    return (y * g.astype(jnp.float32)).astype(x.dtype)


def layer_norm(x, g, b):
    xf = x.astype(jnp.float32)
    mu = jnp.mean(xf, axis=-1, keepdims=True)
    xc = xf - mu
    var = jnp.mean(xc * xc, axis=-1, keepdims=True)
    y = xc * lax.rsqrt(var + EPS) * g.astype(jnp.float32) + b.astype(jnp.float32)
    return y.astype(x.dtype)


def swiglu_ffn(h, w_gate, w_up, w_down):
    return (jax.nn.silu(h @ w_gate) * (h @ w_up)) @ w_down


def spatial_gating(u, v, ln_g, ln_b, w_s, b_s):
    bsz, t_len, _ = v.shape
    v = layer_norm(v, ln_g, ln_b)
    causal = jnp.tril(jnp.ones((CHUNK, CHUNK), dtype=bool))
    w = jnp.where(causal[None], w_s, jnp.zeros_like(w_s))
    vc = v.reshape(bsz, t_len // CHUNK, CHUNK, SGU_HEADS, HEAD_DIM)
    mixed = jnp.einsum('hpq,bnqhc->bnphc', w, vc) + b_s.T[None, None, :, :, None]
    return u * mixed.reshape(bsz, t_len, SGU_DIM)


def conformer_conv(val, gate, conv_w, conv_b, ln_g, ln_b):
    h = val * jax.nn.sigmoid(gate)
    h = lax.conv_general_dilated(
        h, conv_w[:, None, :].astype(h.dtype), window_strides=(1,),
        padding=((CONV_WIDTH - 1, 0),),
        dimension_numbers=('NWC', 'WIO', 'NWC'),
        feature_group_count=CONV_DIM) + conv_b
    h = layer_norm(h, ln_g, ln_b)
    return jax.nn.silu(h)


def multiscale_pool(p, pool_w, pool_scale):
    bsz, t_len, _ = p.shape
    pg = p.reshape(bsz, t_len, POOL_GROUPS, POOL_GROUP_DIM)
    csum = jnp.cumsum(pg.astype(jnp.float32), axis=1)
    pos = jnp.arange(1, t_len + 1, dtype=jnp.float32)
    means = []
    for g, w in enumerate(POOL_WINDOWS):
        cg = csum[:, :, g]
        lagged = jnp.pad(cg, ((0, 0), (w, 0), (0, 0)))[:, :t_len]
        count = jnp.minimum(pos, jnp.float32(w))[None, :, None]
        means.append((cg - lagged) / count)
    pooled = jnp.stack(means, axis=2).astype(p.dtype) - pg
    mixed = jnp.einsum('btgc,gcd->btgd', pooled, pool_w).reshape(bsz, t_len, POOL_DIM)
    return mixed * pool_scale


def hybrid_mixer(h, w_in, sgu_ln_g, sgu_ln_b, w_spatial, b_spatial,
                 conv_w, conv_b, conv_ln_g, conv_ln_b, pool_w, pool_scale, w_out):
    z = h @ w_in
    s1 = SGU_DIM
    s2 = 2 * SGU_DIM
    s3 = s2 + CONV_DIM
    s4 = s3 + CONV_DIM
    a_u, a_v, b_val, b_gate, c_in = z[..., :s1], z[..., s1:s2], z[..., s2:s3], z[..., s3:s4], z[..., s4:]
    a = spatial_gating(jax.nn.gelu(a_u, approximate=False), jax.nn.gelu(a_v, approximate=False),
                       sgu_ln_g, sgu_ln_b, w_spatial, b_spatial)
    b = conformer_conv(b_val, b_gate, conv_w, conv_b, conv_ln_g, conv_ln_b)
    c = multiscale_pool(c_in, pool_w, pool_scale)
    return jnp.concatenate([a, b, c], axis=-1) @ w_out


def _fwd_setup_inputs(seed: int = 0) -> dict:
    key = jax.random.key(seed)
    ks = jax.random.split(key, 24)
    f32 = jnp.float32

    def nrm(k, shape, scale):
        return jax.random.normal(k, shape, f32) * scale

    L, D, F = DEPTH, D_MODEL, D_FF
    return {
        "x": jax.random.normal(ks[0], (BATCH, SEQ, D), f32),
        "ffn1_norm": 1.0 + nrm(ks[1], (L, D), 0.02),
        "ffn1_w_gate": nrm(ks[2], (L, D, F), D ** -0.5),
        "ffn1_w_up": nrm(ks[3], (L, D, F), D ** -0.5),
        "ffn1_w_down": nrm(ks[4], (L, F, D), F ** -0.5),
        "mix_norm": 1.0 + nrm(ks[5], (L, D), 0.02),
        "w_in": nrm(ks[6], (L, D, D_IN), D ** -0.5),
        "sgu_ln_g": 1.0 + nrm(ks[7], (L, SGU_DIM), 0.02),
        "sgu_ln_b": nrm(ks[8], (L, SGU_DIM), 0.02),
        "w_spatial": nrm(ks[9], (L, SGU_HEADS, CHUNK, CHUNK), CHUNK ** -0.5),
        "b_spatial": 1.0 + nrm(ks[10], (L, SGU_HEADS, CHUNK), 0.02),
        "conv_w": nrm(ks[11], (L, CONV_WIDTH, CONV_DIM), CONV_WIDTH ** -0.5),
        "conv_b": nrm(ks[12], (L, CONV_DIM), 0.02),
        "conv_ln_g": 1.0 + nrm(ks[13], (L, CONV_DIM), 0.02),
        "conv_ln_b": nrm(ks[14], (L, CONV_DIM), 0.02),
        "pool_w": nrm(ks[15], (L, POOL_GROUPS, POOL_GROUP_DIM, POOL_GROUP_DIM), POOL_GROUP_DIM ** -0.5),
        "pool_scale": 1.0 + nrm(ks[16], (L, POOL_DIM), 0.1),
        "w_out": nrm(ks[17], (L, D_MIX, D), D_MIX ** -0.5),
        "ffn2_norm": 1.0 + nrm(ks[18], (L, D), 0.02),
        "ffn2_w_gate": nrm(ks[19], (L, D, F), D ** -0.5),
        "ffn2_w_up": nrm(ks[20], (L, D, F), D ** -0.5),
        "ffn2_w_down": nrm(ks[21], (L, F, D), F ** -0.5),
        "final_norm": 1.0 + nrm(ks[22], (D,), 0.02),
    }


def _fwd_reference(x, ffn1_norm, ffn1_w_gate, ffn1_w_up, ffn1_w_down, mix_norm, w_in,
              sgu_ln_g, sgu_ln_b, w_spatial, b_spatial, conv_w, conv_b, conv_ln_g, conv_ln_b,
              pool_w, pool_scale, w_out, ffn2_norm, ffn2_w_gate, ffn2_w_up, ffn2_w_down,
              final_norm):
    for l in range(DEPTH):
        x = x + FFN_RESIDUAL_WEIGHT * swiglu_ffn(rms_norm(x, ffn1_norm[l]),
                                                 ffn1_w_gate[l], ffn1_w_up[l], ffn1_w_down[l])
        x = x + hybrid_mixer(rms_norm(x, mix_norm[l]), w_in[l],
                             sgu_ln_g[l], sgu_ln_b[l], w_spatial[l], b_spatial[l],
                             conv_w[l], conv_b[l], conv_ln_g[l], conv_ln_b[l],
                             pool_w[l], pool_scale[l], w_out[l])
        x = x + FFN_RESIDUAL_WEIGHT * swiglu_ffn(rms_norm(x, ffn2_norm[l]),
                                                 ffn2_w_gate[l], ffn2_w_up[l], ffn2_w_down[l])
    return rms_norm(x, final_norm)


import jax as _jax
import jax.numpy as _jnp

TWIN_FORMAT = 'train_step'
FWD_PARAMS = ['x', 'ffn1_norm', 'ffn1_w_gate', 'ffn1_w_up', 'ffn1_w_down', 'mix_norm', 'w_in', 'sgu_ln_g', 'sgu_ln_b', 'w_spatial', 'b_spatial', 'conv_w', 'conv_b', 'conv_ln_g', 'conv_ln_b', 'pool_w', 'pool_scale', 'w_out', 'ffn2_norm', 'ffn2_w_gate', 'ffn2_w_up', 'ffn2_w_down', 'final_norm']
TWIN_WEIGHTS = ['ffn1_norm', 'ffn1_w_gate', 'ffn1_w_up', 'ffn1_w_down', 'mix_norm', 'w_in', 'sgu_ln_g', 'sgu_ln_b', 'w_spatial', 'b_spatial', 'conv_w', 'conv_b', 'conv_ln_g', 'conv_ln_b', 'pool_w', 'pool_scale', 'w_out', 'ffn2_norm', 'ffn2_w_gate', 'ffn2_w_up', 'ffn2_w_down', 'final_norm']
TWIN_DIFF_INPUT = 'x'
TWIN_INPUTS = ['x', 'ffn1_norm', 'ffn1_w_gate', 'ffn1_w_up', 'ffn1_w_down', 'mix_norm', 'w_in', 'sgu_ln_g', 'sgu_ln_b', 'w_spatial', 'b_spatial', 'conv_w', 'conv_b', 'conv_ln_g', 'conv_ln_b', 'pool_w', 'pool_scale', 'w_out', 'ffn2_norm', 'ffn2_w_gate', 'ffn2_w_up', 'ffn2_w_down', 'final_norm', 'loss_target', 'm_ffn1_norm', 'm_ffn1_w_gate', 'm_ffn1_w_up', 'm_ffn1_w_down', 'm_mix_norm', 'm_w_in', 'm_sgu_ln_g', 'm_sgu_ln_b', 'm_w_spatial', 'm_b_spatial', 'm_conv_w', 'm_conv_b', 'm_conv_ln_g', 'm_conv_ln_b', 'm_pool_w', 'm_pool_scale', 'm_w_out', 'm_ffn2_norm', 'm_ffn2_w_gate', 'm_ffn2_w_up', 'm_ffn2_w_down', 'm_final_norm', 'v_ffn1_norm', 'v_ffn1_w_gate', 'v_ffn1_w_up', 'v_ffn1_w_down', 'v_mix_norm', 'v_w_in', 'v_sgu_ln_g', 'v_sgu_ln_b', 'v_w_spatial', 'v_b_spatial', 'v_conv_w', 'v_conv_b', 'v_conv_ln_g', 'v_conv_ln_b', 'v_pool_w', 'v_pool_scale', 'v_w_out', 'v_ffn2_norm', 'v_ffn2_w_gate', 'v_ffn2_w_up', 'v_ffn2_w_down', 'v_final_norm']
TWIN_OUTPUTS = ['loss', 'grad_x', 'grad_ffn1_norm', 'grad_ffn1_w_gate', 'grad_ffn1_w_up', 'grad_ffn1_w_down', 'grad_mix_norm', 'grad_w_in', 'grad_sgu_ln_g', 'grad_sgu_ln_b', 'grad_w_spatial', 'grad_b_spatial', 'grad_conv_w', 'grad_conv_b', 'grad_conv_ln_g', 'grad_conv_ln_b', 'grad_pool_w', 'grad_pool_scale', 'grad_w_out', 'grad_ffn2_norm', 'grad_ffn2_w_gate', 'grad_ffn2_w_up', 'grad_ffn2_w_down', 'grad_final_norm', 'delta_ffn1_norm', 'delta_ffn1_w_gate', 'delta_ffn1_w_up', 'delta_ffn1_w_down', 'delta_mix_norm', 'delta_w_in', 'delta_sgu_ln_g', 'delta_sgu_ln_b', 'delta_w_spatial', 'delta_b_spatial', 'delta_conv_w', 'delta_conv_b', 'delta_conv_ln_g', 'delta_conv_ln_b', 'delta_pool_w', 'delta_pool_scale', 'delta_w_out', 'delta_ffn2_norm', 'delta_ffn2_w_gate', 'delta_ffn2_w_up', 'delta_ffn2_w_down', 'delta_final_norm', 'new_m_ffn1_norm', 'new_m_ffn1_w_gate', 'new_m_ffn1_w_up', 'new_m_ffn1_w_down', 'new_m_mix_norm', 'new_m_w_in', 'new_m_sgu_ln_g', 'new_m_sgu_ln_b', 'new_m_w_spatial', 'new_m_b_spatial', 'new_m_conv_w', 'new_m_conv_b', 'new_m_conv_ln_g', 'new_m_conv_ln_b', 'new_m_pool_w', 'new_m_pool_scale', 'new_m_w_out', 'new_m_ffn2_norm', 'new_m_ffn2_w_gate', 'new_m_ffn2_w_up', 'new_m_ffn2_w_down', 'new_m_final_norm', 'new_v_ffn1_norm', 'new_v_ffn1_w_gate', 'new_v_ffn1_w_up', 'new_v_ffn1_w_down', 'new_v_mix_norm', 'new_v_w_in', 'new_v_sgu_ln_g', 'new_v_sgu_ln_b', 'new_v_w_spatial', 'new_v_b_spatial', 'new_v_conv_w', 'new_v_conv_b', 'new_v_conv_ln_g', 'new_v_conv_ln_b', 'new_v_pool_w', 'new_v_pool_scale', 'new_v_w_out', 'new_v_ffn2_norm', 'new_v_ffn2_w_gate', 'new_v_ffn2_w_up', 'new_v_ffn2_w_down', 'new_v_final_norm']
TWIN_LEAF_KINDS = {'loss': 'loss', 'grad_x': 'grad_x', 'grad_ffn1_norm': 'grad_w', 'grad_ffn1_w_gate': 'grad_w', 'grad_ffn1_w_up': 'grad_w', 'grad_ffn1_w_down': 'grad_w', 'grad_mix_norm': 'grad_w', 'grad_w_in': 'grad_w', 'grad_sgu_ln_g': 'grad_w', 'grad_sgu_ln_b': 'grad_w', 'grad_w_spatial': 'grad_w', 'grad_b_spatial': 'grad_w', 'grad_conv_w': 'grad_w', 'grad_conv_b': 'grad_w', 'grad_conv_ln_g': 'grad_w', 'grad_conv_ln_b': 'grad_w', 'grad_pool_w': 'grad_w', 'grad_pool_scale': 'grad_w', 'grad_w_out': 'grad_w', 'grad_ffn2_norm': 'grad_w', 'grad_ffn2_w_gate': 'grad_w', 'grad_ffn2_w_up': 'grad_w', 'grad_ffn2_w_down': 'grad_w', 'grad_final_norm': 'grad_w', 'delta_ffn1_norm': 'delta_w', 'delta_ffn1_w_gate': 'delta_w', 'delta_ffn1_w_up': 'delta_w', 'delta_ffn1_w_down': 'delta_w', 'delta_mix_norm': 'delta_w', 'delta_w_in': 'delta_w', 'delta_sgu_ln_g': 'delta_w', 'delta_sgu_ln_b': 'delta_w', 'delta_w_spatial': 'delta_w', 'delta_b_spatial': 'delta_w', 'delta_conv_w': 'delta_w', 'delta_conv_b': 'delta_w', 'delta_conv_ln_g': 'delta_w', 'delta_conv_ln_b': 'delta_w', 'delta_pool_w': 'delta_w', 'delta_pool_scale': 'delta_w', 'delta_w_out': 'delta_w', 'delta_ffn2_norm': 'delta_w', 'delta_ffn2_w_gate': 'delta_w', 'delta_ffn2_w_up': 'delta_w', 'delta_ffn2_w_down': 'delta_w', 'delta_final_norm': 'delta_w', 'new_m_ffn1_norm': 'new_m', 'new_m_ffn1_w_gate': 'new_m', 'new_m_ffn1_w_up': 'new_m', 'new_m_ffn1_w_down': 'new_m', 'new_m_mix_norm': 'new_m', 'new_m_w_in': 'new_m', 'new_m_sgu_ln_g': 'new_m', 'new_m_sgu_ln_b': 'new_m', 'new_m_w_spatial': 'new_m', 'new_m_b_spatial': 'new_m', 'new_m_conv_w': 'new_m', 'new_m_conv_b': 'new_m', 'new_m_conv_ln_g': 'new_m', 'new_m_conv_ln_b': 'new_m', 'new_m_pool_w': 'new_m', 'new_m_pool_scale': 'new_m', 'new_m_w_out': 'new_m', 'new_m_ffn2_norm': 'new_m', 'new_m_ffn2_w_gate': 'new_m', 'new_m_ffn2_w_up': 'new_m', 'new_m_ffn2_w_down': 'new_m', 'new_m_final_norm': 'new_m', 'new_v_ffn1_norm': 'new_v', 'new_v_ffn1_w_gate': 'new_v', 'new_v_ffn1_w_up': 'new_v', 'new_v_ffn1_w_down': 'new_v', 'new_v_mix_norm': 'new_v', 'new_v_w_in': 'new_v', 'new_v_sgu_ln_g': 'new_v', 'new_v_sgu_ln_b': 'new_v', 'new_v_w_spatial': 'new_v', 'new_v_b_spatial': 'new_v', 'new_v_conv_w': 'new_v', 'new_v_conv_b': 'new_v', 'new_v_conv_ln_g': 'new_v', 'new_v_conv_ln_b': 'new_v', 'new_v_pool_w': 'new_v', 'new_v_pool_scale': 'new_v', 'new_v_w_out': 'new_v', 'new_v_ffn2_norm': 'new_v', 'new_v_ffn2_w_gate': 'new_v', 'new_v_ffn2_w_up': 'new_v', 'new_v_ffn2_w_down': 'new_v', 'new_v_final_norm': 'new_v'}


def _forward(args):
    return _fwd_reference(*[args[k] for k in FWD_PARAMS])


def _output_shape():
    def fwd():
        inp = _fwd_setup_inputs(0)
        return _fwd_reference(*[inp[k] for k in FWD_PARAMS])
    out = _jax.eval_shape(fwd)
    return out.shape, out.dtype

N_MICROBATCH = 1
ADAM_LR = 0.001
ADAM_B1 = 0.9
ADAM_B2 = 0.999
ADAM_EPS = 1e-08
ADAM_WD = 0.01
ADAM_STEP = 10
PER_EXAMPLE_BATCH_AXIS = {'x': 0, 'loss_target': 0}
SHARED_INPUTS = []
_WEIGHT_DTYPES = {'ffn1_norm': _jnp.float32, 'ffn1_w_gate': _jnp.float32, 'ffn1_w_up': _jnp.float32, 'ffn1_w_down': _jnp.float32, 'mix_norm': _jnp.float32, 'w_in': _jnp.float32, 'sgu_ln_g': _jnp.float32, 'sgu_ln_b': _jnp.float32, 'w_spatial': _jnp.float32, 'b_spatial': _jnp.float32, 'conv_w': _jnp.float32, 'conv_b': _jnp.float32, 'conv_ln_g': _jnp.float32, 'conv_ln_b': _jnp.float32, 'pool_w': _jnp.float32, 'pool_scale': _jnp.float32, 'w_out': _jnp.float32, 'ffn2_norm': _jnp.float32, 'ffn2_w_gate': _jnp.float32, 'ffn2_w_up': _jnp.float32, 'ffn2_w_down': _jnp.float32, 'final_norm': _jnp.float32}
MOMENT_SCALE = {'ffn1_norm': 7.331387e-02, 'ffn1_w_gate': 3.149251e-02, 'ffn1_w_up': 3.048050e-02, 'ffn1_w_down': 5.059609e-02, 'mix_norm': 1.179409e-01, 'w_in': 8.722209e-02, 'sgu_ln_g': 6.376258e-02, 'sgu_ln_b': 7.036374e-02, 'w_spatial': 4.401963e-02, 'b_spatial': 6.268190e-02, 'conv_w': 8.477939e-02, 'conv_b': 1.846979e-01, 'conv_ln_g': 1.102286e-01, 'conv_ln_b': 1.118635e-01, 'pool_w': 1.190532e-01, 'pool_scale': 1.288555e-01, 'w_out': 1.060957e-01, 'ffn2_norm': 5.729374e-02, 'ffn2_w_gate': 2.417413e-02, 'ffn2_w_up': 2.341742e-02, 'ffn2_w_down': 3.887782e-02, 'final_norm': 3.200881e+01}


def _to_microbatches(a, axis):
    t = _jnp.moveaxis(a, axis, 0)
    t = t.reshape((N_MICROBATCH, t.shape[0] // N_MICROBATCH) + t.shape[1:])
    return _jnp.moveaxis(t, 1, axis + 1)


def setup_inputs(seed: int = 0) -> dict:
    inp = _fwd_setup_inputs(seed)
    key = _jax.random.fold_in(_jax.random.key(seed), 7919)
    shape, _ = _output_shape()
    out = dict(inp)
    out["loss_target"] = _jax.random.normal(_jax.random.fold_in(key, 0), shape, _jnp.float32)
    for i, name in enumerate(TWIN_WEIGHTS):
        w = inp[name].astype(_jnp.float32)
        if MOMENT_SCALE is None:
            s = _jnp.sqrt(_jnp.mean(_jnp.square(w)) + 1e-30)
        else:
            s = MOMENT_SCALE[name]
        km, kv = _jax.random.split(_jax.random.fold_in(key, i + 1))
        out[name] = w
        out["m_" + name] = s * _jax.random.normal(km, w.shape, _jnp.float32)
        out["v_" + name] = (s * s) * _jax.random.uniform(kv, w.shape, _jnp.float32, 0.5, 1.5)
    if N_MICROBATCH > 1:
        for name, axis in PER_EXAMPLE_BATCH_AXIS.items():
            out[name] = _to_microbatches(out[name], axis)
    return {'x': out['x'], 'ffn1_norm': out['ffn1_norm'], 'ffn1_w_gate': out['ffn1_w_gate'], 'ffn1_w_up': out['ffn1_w_up'], 'ffn1_w_down': out['ffn1_w_down'], 'mix_norm': out['mix_norm'], 'w_in': out['w_in'], 'sgu_ln_g': out['sgu_ln_g'], 'sgu_ln_b': out['sgu_ln_b'], 'w_spatial': out['w_spatial'], 'b_spatial': out['b_spatial'], 'conv_w': out['conv_w'], 'conv_b': out['conv_b'], 'conv_ln_g': out['conv_ln_g'], 'conv_ln_b': out['conv_ln_b'], 'pool_w': out['pool_w'], 'pool_scale': out['pool_scale'], 'w_out': out['w_out'], 'ffn2_norm': out['ffn2_norm'], 'ffn2_w_gate': out['ffn2_w_gate'], 'ffn2_w_up': out['ffn2_w_up'], 'ffn2_w_down': out['ffn2_w_down'], 'final_norm': out['final_norm'], 'loss_target': out['loss_target'], 'm_ffn1_norm': out['m_ffn1_norm'], 'm_ffn1_w_gate': out['m_ffn1_w_gate'], 'm_ffn1_w_up': out['m_ffn1_w_up'], 'm_ffn1_w_down': out['m_ffn1_w_down'], 'm_mix_norm': out['m_mix_norm'], 'm_w_in': out['m_w_in'], 'm_sgu_ln_g': out['m_sgu_ln_g'], 'm_sgu_ln_b': out['m_sgu_ln_b'], 'm_w_spatial': out['m_w_spatial'], 'm_b_spatial': out['m_b_spatial'], 'm_conv_w': out['m_conv_w'], 'm_conv_b': out['m_conv_b'], 'm_conv_ln_g': out['m_conv_ln_g'], 'm_conv_ln_b': out['m_conv_ln_b'], 'm_pool_w': out['m_pool_w'], 'm_pool_scale': out['m_pool_scale'], 'm_w_out': out['m_w_out'], 'm_ffn2_norm': out['m_ffn2_norm'], 'm_ffn2_w_gate': out['m_ffn2_w_gate'], 'm_ffn2_w_up': out['m_ffn2_w_up'], 'm_ffn2_w_down': out['m_ffn2_w_down'], 'm_final_norm': out['m_final_norm'], 'v_ffn1_norm': out['v_ffn1_norm'], 'v_ffn1_w_gate': out['v_ffn1_w_gate'], 'v_ffn1_w_up': out['v_ffn1_w_up'], 'v_ffn1_w_down': out['v_ffn1_w_down'], 'v_mix_norm': out['v_mix_norm'], 'v_w_in': out['v_w_in'], 'v_sgu_ln_g': out['v_sgu_ln_g'], 'v_sgu_ln_b': out['v_sgu_ln_b'], 'v_w_spatial': out['v_w_spatial'], 'v_b_spatial': out['v_b_spatial'], 'v_conv_w': out['v_conv_w'], 'v_conv_b': out['v_conv_b'], 'v_conv_ln_g': out['v_conv_ln_g'], 'v_conv_ln_b': out['v_conv_ln_b'], 'v_pool_w': out['v_pool_w'], 'v_pool_scale': out['v_pool_scale'], 'v_w_out': out['v_w_out'], 'v_ffn2_norm': out['v_ffn2_norm'], 'v_ffn2_w_gate': out['v_ffn2_w_gate'], 'v_ffn2_w_up': out['v_ffn2_w_up'], 'v_ffn2_w_down': out['v_ffn2_w_down'], 'v_final_norm': out['v_final_norm']}


def _loss(weights, diff, rest, loss_target):
    with _jax.named_scope("forward"):
        args = {**rest, TWIN_DIFF_INPUT: diff, **{k: w.astype(_WEIGHT_DTYPES[k]) for k, w in weights.items()}}
        y = _forward(args)
    with _jax.named_scope("loss_head"):
        err = _jnp.square(y.astype(_jnp.float32) - loss_target)
        return 0.5 * _jnp.sum(_jnp.mean(err, axis=-1)) if err.ndim else 0.5 * err


def _adamw(w, g, m, v):
    m = ADAM_B1 * m + (1.0 - ADAM_B1) * g
    v = ADAM_B2 * v + (1.0 - ADAM_B2) * _jnp.square(g)
    m_hat = m / (1.0 - ADAM_B1 ** ADAM_STEP)
    v_hat = v / (1.0 - ADAM_B2 ** ADAM_STEP)
    delta = -ADAM_LR * (m_hat / (_jnp.sqrt(v_hat) + ADAM_EPS) + ADAM_WD * w)
    return delta, m, v


def reference(x, ffn1_norm, ffn1_w_gate, ffn1_w_up, ffn1_w_down, mix_norm, w_in, sgu_ln_g, sgu_ln_b, w_spatial, b_spatial, conv_w, conv_b, conv_ln_g, conv_ln_b, pool_w, pool_scale, w_out, ffn2_norm, ffn2_w_gate, ffn2_w_up, ffn2_w_down, final_norm, loss_target, m_ffn1_norm, m_ffn1_w_gate, m_ffn1_w_up, m_ffn1_w_down, m_mix_norm, m_w_in, m_sgu_ln_g, m_sgu_ln_b, m_w_spatial, m_b_spatial, m_conv_w, m_conv_b, m_conv_ln_g, m_conv_ln_b, m_pool_w, m_pool_scale, m_w_out, m_ffn2_norm, m_ffn2_w_gate, m_ffn2_w_up, m_ffn2_w_down, m_final_norm, v_ffn1_norm, v_ffn1_w_gate, v_ffn1_w_up, v_ffn1_w_down, v_mix_norm, v_w_in, v_sgu_ln_g, v_sgu_ln_b, v_w_spatial, v_b_spatial, v_conv_w, v_conv_b, v_conv_ln_g, v_conv_ln_b, v_pool_w, v_pool_scale, v_w_out, v_ffn2_norm, v_ffn2_w_gate, v_ffn2_w_up, v_ffn2_w_down, v_final_norm):
    given = dict(x=x, ffn1_norm=ffn1_norm, ffn1_w_gate=ffn1_w_gate, ffn1_w_up=ffn1_w_up, ffn1_w_down=ffn1_w_down, mix_norm=mix_norm, w_in=w_in, sgu_ln_g=sgu_ln_g, sgu_ln_b=sgu_ln_b, w_spatial=w_spatial, b_spatial=b_spatial, conv_w=conv_w, conv_b=conv_b, conv_ln_g=conv_ln_g, conv_ln_b=conv_ln_b, pool_w=pool_w, pool_scale=pool_scale, w_out=w_out, ffn2_norm=ffn2_norm, ffn2_w_gate=ffn2_w_gate, ffn2_w_up=ffn2_w_up, ffn2_w_down=ffn2_w_down, final_norm=final_norm, loss_target=loss_target, m_ffn1_norm=m_ffn1_norm, m_ffn1_w_gate=m_ffn1_w_gate, m_ffn1_w_up=m_ffn1_w_up, m_ffn1_w_down=m_ffn1_w_down, m_mix_norm=m_mix_norm, m_w_in=m_w_in, m_sgu_ln_g=m_sgu_ln_g, m_sgu_ln_b=m_sgu_ln_b, m_w_spatial=m_w_spatial, m_b_spatial=m_b_spatial, m_conv_w=m_conv_w, m_conv_b=m_conv_b, m_conv_ln_g=m_conv_ln_g, m_conv_ln_b=m_conv_ln_b, m_pool_w=m_pool_w, m_pool_scale=m_pool_scale, m_w_out=m_w_out, m_ffn2_norm=m_ffn2_norm, m_ffn2_w_gate=m_ffn2_w_gate, m_ffn2_w_up=m_ffn2_w_up, m_ffn2_w_down=m_ffn2_w_down, m_final_norm=m_final_norm, v_ffn1_norm=v_ffn1_norm, v_ffn1_w_gate=v_ffn1_w_gate, v_ffn1_w_up=v_ffn1_w_up, v_ffn1_w_down=v_ffn1_w_down, v_mix_norm=v_mix_norm, v_w_in=v_w_in, v_sgu_ln_g=v_sgu_ln_g, v_sgu_ln_b=v_sgu_ln_b, v_w_spatial=v_w_spatial, v_b_spatial=v_b_spatial, v_conv_w=v_conv_w, v_conv_b=v_conv_b, v_conv_ln_g=v_conv_ln_g, v_conv_ln_b=v_conv_ln_b, v_pool_w=v_pool_w, v_pool_scale=v_pool_scale, v_w_out=v_w_out, v_ffn2_norm=v_ffn2_norm, v_ffn2_w_gate=v_ffn2_w_gate, v_ffn2_w_up=v_ffn2_w_up, v_ffn2_w_down=v_ffn2_w_down, v_final_norm=v_final_norm)
    weights = {n: given[n] for n in TWIN_WEIGHTS}
    shared = {n: given[n] for n in SHARED_INPUTS}
    per_example = {n: given[n] for n in ['x']}
    grad_fn = _jax.value_and_grad(_loss, argnums=(0, 1))

    def one_microbatch(ex, loss_target):
        ex = dict(ex)
        diff = ex.pop(TWIN_DIFF_INPUT)
        return grad_fn(weights, diff, {**shared, **ex}, loss_target)

    if N_MICROBATCH == 1:
        loss, (grad_w, grad_x) = one_microbatch(per_example, given["loss_target"])
    else:
        def body(carry, xs):
            loss_sum, grad_sum = carry
            l_k, (gw_k, gx_k) = one_microbatch(xs[0], xs[1])
            with _jax.named_scope("update"):
                return (loss_sum + l_k, _jax.tree.map(_jnp.add, grad_sum, gw_k)), gx_k

        init = (_jnp.zeros((), _jnp.float32), _jax.tree.map(_jnp.zeros_like, weights))
        (loss, grad_w), grad_x = _jax.lax.scan(body, init, (per_example, given["loss_target"]))
    with _jax.named_scope("update"):
        delta_w, new_m, new_v = {}, {}, {}
        for n in TWIN_WEIGHTS:
            delta_w[n], new_m[n], new_v[n] = _adamw(weights[n], grad_w[n], given["m_" + n], given["v_" + n])
    return (loss, grad_x, *[grad_w[n] for n in TWIN_WEIGHTS], *[delta_w[n] for n in TWIN_WEIGHTS],
            *[new_m[n] for n in TWIN_WEIGHTS], *[new_v[n] for n in TWIN_WEIGHTS])
```

```python
import functools

import jax
import jax.numpy as jnp
from jax import lax
from jax.experimental import pallas as pl
from jax.experimental.pallas import tpu as pltpu

F32 = jnp.float32
BF16 = jnp.bfloat16
EPS = 1e-6
N_DEV = 8
N_CHIP = 4
MESH = pl.DeviceIdType.MESH
ANY = pl.BlockSpec(memory_space=pl.ANY)

VMEM_LIMIT_BYTES = 56 * 1024 * 1024
LANES = 128
HALO = 32
HEAD_DIM = 64
CHUNK = 128
CONV_WIDTH = 31
POOL_WINDOWS = (2, 4, 8, 16)

ADAM_LR = 0.001
ADAM_B1 = 0.9
ADAM_B2 = 0.999
ADAM_EPS = 1e-08
ADAM_WD = 0.01
ADAM_STEP = 10


def _cparams(sem=None):
    return pltpu.CompilerParams(dimension_semantics=sem, vmem_limit_bytes=VMEM_LIMIT_BYTES)


def _dot(a, b):
    return jnp.dot(a, b, preferred_element_type=F32)


def _dot_nt(a, b):
    return lax.dot_general(a, b, (((1,), (1,)), ((), ())), preferred_element_type=F32)


def _dot_tn(a, b):
    return lax.dot_general(a, b, (((0,), (0,)), ((), ())), preferred_element_type=F32)


def _split_dot(x, e):
    hi = x.astype(BF16)
    r1 = x - hi.astype(F32)
    mid = r1.astype(BF16)
    lo = (r1 - mid.astype(F32)).astype(BF16)
    return _dot(hi, e) + _dot(mid, e) + _dot(lo, e)


def _rms(x):
    rstd = lax.rsqrt(jnp.mean(x * x, axis=-1, keepdims=True) + EPS)
    return x * rstd, rstd


def _rms_bwd(xhat, rstd, g, dh):
    dxhat = dh * g
    dx = rstd * (dxhat - xhat * jnp.mean(dxhat * xhat, axis=-1, keepdims=True))
    return dx, jnp.sum(dh * xhat, axis=0, keepdims=True)


def _ln(v):
    mu = jnp.mean(v, axis=-1, keepdims=True)
    xc = v - mu
    rstd = lax.rsqrt(jnp.mean(xc * xc, axis=-1, keepdims=True) + EPS)
    return xc * rstd, rstd


def _ln_bwd(vhat, rstd, g, dy):
    dvhat = dy * g
    dv = rstd * (dvhat - jnp.mean(dvhat, axis=-1, keepdims=True)
                 - vhat * jnp.mean(dvhat * vhat, axis=-1, keepdims=True))
    return dv, jnp.sum(dy * vhat, axis=0, keepdims=True), jnp.sum(dy, axis=0, keepdims=True)


_INV_SQRT2 = 0.7071067811865476
_INV_SQRT2PI = 0.3989422804014327


def _gelu(x):
    return 0.5 * x * (1.0 + lax.erf(x * _INV_SQRT2))


def _gelu_grad(x):
    return 0.5 * (1.0 + lax.erf(x * _INV_SQRT2)) + x * jnp.exp(-0.5 * x * x) * _INV_SQRT2PI


def _silu_grad(x):
    s = jax.nn.sigmoid(x)
    return s * (1.0 + x * (1.0 - s))


def _ffn_fwd(x, g, wa, mi, name):
    t, d = x.shape
    f = wa.shape[1]
    tm, tf = 1024, 256
    nc = f // tf

    def body(x_ref, g_ref, wg_ref, wu_ref, wd_ref, xo_ref, gate_ref, up_ref, h_scr, acc_scr):
        c = pl.program_id(1)

        @pl.when(c == 0)
        def _():
            xhat, _ = _rms(x_ref[...])
            h_scr[...] = (xhat * g_ref[...]).astype(BF16)
            acc_scr[...] = jnp.zeros_like(acc_scr)

        h = h_scr[...]
        gate = _dot_nt(h, wg_ref[...])
        up = _dot_nt(h, wu_ref[...])
        gate_ref[...] = gate.astype(BF16)
        up_ref[...] = up.astype(BF16)
        act = (gate * jax.nn.sigmoid(gate) * up).astype(BF16)
        acc_scr[...] += _dot(act, wd_ref[...])

        @pl.when(c == nc - 1)
        def _():
            xo_ref[...] = x_ref[...] + 0.5 * acc_scr[...]

    def wspec(k):
        return pl.BlockSpec((None, tf, d), lambda i, c: (mi + k, c, 0))

    return pl.pallas_call(
        body, name=name, grid=(t // tm, nc),
        in_specs=[pl.BlockSpec((tm, d), lambda i, c: (i, 0)), pl.BlockSpec((1, d), lambda i, c: (0, 0)),
                  wspec(0), wspec(1), wspec(2)],
        out_specs=[pl.BlockSpec((tm, d), lambda i, c: (i, 0)), pl.BlockSpec((tm, tf), lambda i, c: (i, c)),
                   pl.BlockSpec((tm, tf), lambda i, c: (i, c))],
        out_shape=[jax.ShapeDtypeStruct((t, d), F32), jax.ShapeDtypeStruct((t, f), BF16),
                   jax.ShapeDtypeStruct((t, f), BF16)],
        scratch_shapes=[pltpu.VMEM((tm, d), BF16), pltpu.VMEM((tm, d), F32)],
        compiler_params=_cparams(("parallel", "arbitrary")),
    )(x, g, wa, wa, wa)


def _ffn_bwd(x, g, dxo, gate, up, wa, mi, name):
    t, d = x.shape
    f = wa.shape[1]
    tm, tf = 512, 256
    nc = f // tf

    def body(x_ref, g_ref, dxo_ref, gate_ref, up_ref, wg_ref, wu_ref, wd_ref,
             dx_ref, dgate_ref, dup_ref, act_ref, h_ref, dy_ref, dg_ref, h_scr, dy_scr, acc_scr):
        i, c = pl.program_id(0), pl.program_id(1)

        @pl.when(c == 0)
        def _():
            xhat, _ = _rms(x_ref[...])
            h = (xhat * g_ref[...]).astype(BF16)
            h_scr[...] = h
            h_ref[...] = h
            dy = (0.5 * dxo_ref[...]).astype(BF16)
            dy_scr[...] = dy
            dy_ref[...] = dy
            acc_scr[...] = jnp.zeros_like(acc_scr)

        @pl.when((c == 0) & (i == 0))
        def _():
            dg_ref[...] = jnp.zeros_like(dg_ref)

        gt = gate_ref[...].astype(F32)
        u = up_ref[...].astype(F32)
        s = jax.nn.sigmoid(gt)
        silu = gt * s
        dact = _dot_nt(dy_scr[...], wd_ref[...])
        dgate = (dact * u * (s * (1.0 + gt * (1.0 - s)))).astype(BF16)
        dup = (dact * silu).astype(BF16)
        dgate_ref[...] = dgate
        dup_ref[...] = dup
        act_ref[...] = (silu * u).astype(BF16)
        acc_scr[...] += _dot(dgate, wg_ref[...]) + _dot(dup, wu_ref[...])

        @pl.when(c == nc - 1)
        def _():
            xhat, rstd = _rms(x_ref[...])
            dxn, dg = _rms_bwd(xhat, rstd, g_ref[...], acc_scr[...])
            dx_ref[...] = dxo_ref[...] + dxn
            dg_ref[...] += dg

    def wspec(k):
        return pl.BlockSpec((None, tf, d), lambda i, c: (mi + k, c, 0))

    row = pl.BlockSpec((tm, d), lambda i, c: (i, 0))
    col = pl.BlockSpec((tm, tf), lambda i, c: (i, c))
    vec = pl.BlockSpec((1, d), lambda i, c: (0, 0))
    return pl.pallas_call(
        body, name=name, grid=(t // tm, nc),
        in_specs=[row, vec, row, col, col, wspec(0), wspec(1), wspec(2)],
        out_specs=[row, col, col, col, row, row, vec],
        out_shape=[jax.ShapeDtypeStruct((t, d), F32), jax.ShapeDtypeStruct((t, f), BF16),
                   jax.ShapeDtypeStruct((t, f), BF16), jax.ShapeDtypeStruct((t, f), BF16),
                   jax.ShapeDtypeStruct((t, d), BF16), jax.ShapeDtypeStruct((t, d), BF16),
                   jax.ShapeDtypeStruct((1, d), F32)],
        scratch_shapes=[pltpu.VMEM((tm, d), BF16), pltpu.VMEM((tm, d), BF16), pltpu.VMEM((tm, d), F32)],
        compiler_params=_cparams(("arbitrary", "arbitrary")),
    )(x, g, dxo, gate, up, wa, wa, wa)


def _tn_matmul(a, b, name):
    t, m = a.shape
    n = b.shape[1]
    tk = 512
    tmm = m // 2 if (m // 2) % LANES == 0 else m
    nk = t // tk

    def body(a_ref, b_ref, o_ref, acc_scr):
        k = pl.program_id(1)

        @pl.when(k == 0)
        def _():
            acc_scr[...] = jnp.zeros_like(acc_scr)

        acc_scr[...] += _dot_tn(a_ref[...].astype(BF16), b_ref[...].astype(BF16))

        @pl.when(k == nk - 1)
        def _():
            o_ref[...] = acc_scr[...].astype(BF16)

    return pl.pallas_call(
        body, name=name, grid=(m // tmm, nk),
        in_specs=[pl.BlockSpec((tk, tmm), lambda j, k: (k, j)), pl.BlockSpec((tk, n), lambda j, k: (k, 0))],
        out_specs=pl.BlockSpec((tmm, n), lambda j, k: (j, 0)),
        out_shape=jax.ShapeDtypeStruct((m, n), BF16),
        scratch_shapes=[pltpu.VMEM((tmm, n), F32)],
        compiler_params=_cparams(("parallel", "arbitrary")),
    )(a, b)


def _proj_in_fwd(x, g, wb, li, name):
    t, d = x.shape
    n = wb.shape[1]
    tm = 512

    def body(x_ref, g_ref, w_ref, z_ref):
        xhat, _ = _rms(x_ref[...])
        z_ref[...] = _dot_nt((xhat * g_ref[...]).astype(BF16), w_ref[...])

    return pl.pallas_call(
        body, name=name, grid=(t // tm,),
        in_specs=[pl.BlockSpec((tm, d), lambda i: (i, 0)), pl.BlockSpec((1, d), lambda i: (0, 0)),
                  pl.BlockSpec((None, n, d), lambda i: (li, 0, 0))],
        out_specs=pl.BlockSpec((tm, n), lambda i: (i, 0)),
        out_shape=jax.ShapeDtypeStruct((t, n), F32),
        compiler_params=_cparams(("parallel",)),
    )(x, g, wb)


def _proj_out_fwd(x, cat, wc, li, name):
    t, d = x.shape
    tm = 512

    def body(x_ref, cat_ref, w_ref, xo_ref):
        xo_ref[...] = x_ref[...] + _dot(cat_ref[...], w_ref[...])

    return pl.pallas_call(
        body, name=name, grid=(t // tm,),
        in_specs=[pl.BlockSpec((tm, d), lambda i: (i, 0)), pl.BlockSpec((tm, d), lambda i: (i, 0)),
                  pl.BlockSpec((None, d, d), lambda i: (li, 0, 0))],
        out_specs=pl.BlockSpec((tm, d), lambda i: (i, 0)),
        out_shape=jax.ShapeDtypeStruct((t, d), F32),
        compiler_params=_cparams(("parallel",)),
    )(x, cat, wc)


def _proj_out_bwd(dxo, wc, li, name):
    t, d = dxo.shape
    tm = 512

    def body(dxo_ref, w_ref, dcat_ref):
        dcat_ref[...] = _dot_nt(dxo_ref[...].astype(BF16), w_ref[...])

    return pl.pallas_call(
        body, name=name, grid=(t // tm,),
        in_specs=[pl.BlockSpec((tm, d), lambda i: (i, 0)), pl.BlockSpec((None, d, d), lambda i: (li, 0, 0))],
        out_specs=pl.BlockSpec((tm, d), lambda i: (i, 0)),
        out_shape=jax.ShapeDtypeStruct((t, d), F32),
        compiler_params=_cparams(("parallel",)),
    )(dxo, wc)


def _proj_in_bwd(x, g, dxo, dz, wb, li, name):
    t, d = x.shape
    n = wb.shape[1]
    tm = 512

    def body(x_ref, g_ref, dxo_ref, dz_ref, w_ref, dx_ref, h_ref, dg_ref):
        i = pl.program_id(0)

        @pl.when(i == 0)
        def _():
            dg_ref[...] = jnp.zeros_like(dg_ref)

        xhat, rstd = _rms(x_ref[...])
        h_ref[...] = (xhat * g_ref[...]).astype(BF16)
        dh = _dot(dz_ref[...], w_ref[...])
        dxn, dg = _rms_bwd(xhat, rstd, g_ref[...], dh)
        dx_ref[...] = dxo_ref[...] + dxn
        dg_ref[...] += dg

    row = pl.BlockSpec((tm, d), lambda i: (i, 0))
    vec = pl.BlockSpec((1, d), lambda i: (0, 0))
    return pl.pallas_call(
        body, name=name, grid=(t // tm,),
        in_specs=[row, vec, row, pl.BlockSpec((tm, n), lambda i: (i, 0)),
                  pl.BlockSpec((None, n, d), lambda i: (li, 0, 0))],
        out_specs=[row, row, vec],
        out_shape=[jax.ShapeDtypeStruct((t, d), F32), jax.ShapeDtypeStruct((t, d), BF16),
                   jax.ShapeDtypeStruct((1, d), F32)],
        compiler_params=_cparams(("arbitrary",)),
    )(x, g, dxo, dz, wb)


def _lane_ids(shape):
    return lax.broadcasted_iota(jnp.int32, shape, 1)


def _tril(w):
    r = lax.broadcasted_iota(jnp.int32, w.shape, 0)
    c = lax.broadcasted_iota(jnp.int32, w.shape, 1)
    return jnp.where(r >= c, w, 0.0)


def _shift_down(x, k):
    return x if k == 0 else pltpu.roll(x, k, 0)


def _shift_up(x, k):
    return x if k == 0 else pltpu.roll(x, x.shape[0] - k, 0)


def _causal_conv(ext, w, n_out):
    acc = None
    for b in range(8):
        rolled = _shift_down(ext, b)
        for a in range((CONV_WIDTH - 1 - b) // 8 + 1):
            j = 8 * a + b
            term = rolled[HALO - 8 * a:HALO - 8 * a + n_out] * w[CONV_WIDTH - 1 - j:CONV_WIDTH - j]
            acc = term if acc is None else acc + term
    return acc


def _conv_wgrad(ext, dhc, n_out):
    rows = [None] * CONV_WIDTH
    for b in range(8):
        rolled = _shift_down(ext, b)
        for a in range((CONV_WIDTH - 1 - b) // 8 + 1):
            j = 8 * a + b
            rows[CONV_WIDTH - 1 - j] = jnp.sum(
                rolled[HALO - 8 * a:HALO - 8 * a + n_out] * dhc, axis=0, keepdims=True)
    return rows


def _anticausal_conv(ext, w, n_out):
    acc = None
    for b in range(8):
        rolled = _shift_up(ext, b)
        for a in range((CONV_WIDTH - 1 - b) // 8 + 1):
            j = 8 * a + b
            term = rolled[8 * a:8 * a + n_out] * w[CONV_WIDTH - 1 - j:CONV_WIDTH - j]
            acc = term if acc is None else acc + term
    return acc


def _window_sums(ext, shift):
    s2 = ext + shift(ext, 1)
    s4 = s2 + shift(s2, 2)
    s8 = s4 + shift(s4, 4)
    s16 = s8 + shift(s8, 8)
    grp = _lane_ids(ext.shape) // HEAD_DIM
    return jnp.where(grp == 0, s2, jnp.where(grp == 1, s4, jnp.where(grp == 2, s8, s16)))


def _pool_count(t0, n, width):
    pos = (lax.broadcasted_iota(jnp.int32, (n, width), 0) + (t0 + 1)).astype(F32)
    grp = _lane_ids((n, width)) // HEAD_DIM
    win = jnp.where(grp == 0, 2.0, jnp.where(grp == 1, 4.0, jnp.where(grp == 2, 8.0, 16.0)))
    return jnp.minimum(pos, win)


def _block_diag(pw):
    gn, cg, _ = pw.shape
    rows = []
    for gi in range(gn):
        parts = [pw[gi] if gj == gi else jnp.zeros((cg, cg), pw.dtype) for gj in range(gn)]
        rows.append(jnp.concatenate(parts, axis=1))
    return jnp.concatenate(rows, axis=0)


def _head_pair_mix(w_even, w_odd, v):
    lo = _lane_ids((CHUNK, LANES)) < HEAD_DIM
    return jnp.where(lo, _dot(w_even, v), _dot(w_odd, v))


def _mix_fwd(z, p, name):
    t, d_in = z.shape
    sgu = p["sgu_ln_g"].shape[1]
    pool = p["pool_scale"].shape[1]
    d_mix = 2 * sgu + pool
    tm = 512
    n_i = t // tm
    hb = tm // HALO

    def body(z_ref, zp_ref, lng_ref, lnb_ref, ws_ref, bs_ref, cw_ref, cb_ref, clg_ref, clb_ref,
             bd_ref, ps_ref, cat_ref):
        i = pl.program_id(0)
        first = i == 0
        z_main = z_ref[...]
        z_prev = jnp.where(first, 0.0, zp_ref[...])

        lng, lnb = lng_ref[...], lnb_ref[...]
        wt = [_tril(ws_ref[h]).astype(BF16) for h in range(sgu // HEAD_DIM)]
        for n in range(tm // CHUNK):
            rows = slice(n * CHUNK, (n + 1) * CHUNK)
            u = _gelu(z_main[rows, 0:sgu])
            vhat, _ = _ln(_gelu(z_main[rows, sgu:2 * sgu]))
            vn = (vhat * lng + lnb).astype(BF16)
            for gp in range(sgu // LANES):
                ls = slice(gp * LANES, (gp + 1) * LANES)
                mixed = _head_pair_mix(wt[2 * gp], wt[2 * gp + 1], vn[:, ls]) + bs_ref[:, ls]
                cat_ref[rows, ls] = (u[:, ls] * mixed).astype(BF16)

        def glu(zz):
            return zz[:, 2 * sgu:3 * sgu] * jax.nn.sigmoid(zz[:, 3 * sgu:4 * sgu])

        ext = jnp.concatenate([glu(z_prev), glu(z_main)], axis=0)
        hc = _causal_conv(ext, cw_ref[...], tm) + cb_ref[...]
        hhat, _ = _ln(hc)
        bn = hhat * clg_ref[...] + clb_ref[...]
        cat_ref[:, sgu:2 * sgu] = (bn * jax.nn.sigmoid(bn)).astype(BF16)

        pext = jnp.concatenate([z_prev[:, 4 * sgu:], z_main[:, 4 * sgu:]], axis=0)
        sums = _window_sums(pext, _shift_down)[HALO:]
        pooled = sums / _pool_count(i * tm, tm, pool) - z_main[:, 4 * sgu:]
        mixed_c = _dot(pooled.astype(BF16), bd_ref[...].astype(BF16))
        cat_ref[:, 2 * sgu:] = (mixed_c * ps_ref[...]).astype(BF16)

    def vec(n):
        return pl.BlockSpec((1, n), lambda i: (0, 0))

    return pl.pallas_call(
        body, name=name, grid=(n_i,),
        in_specs=[pl.BlockSpec((tm, d_in), lambda i: (i, 0)),
                  pl.BlockSpec((HALO, d_in), lambda i: (jnp.maximum(i * hb - 1, 0), 0)),
                  vec(sgu), vec(sgu),
                  pl.BlockSpec(p["w_spatial"].shape, lambda i: (0, 0, 0)),
                  pl.BlockSpec((CHUNK, sgu), lambda i: (0, 0)),
                  pl.BlockSpec((CONV_WIDTH, sgu), lambda i: (0, 0)),
                  vec(sgu), vec(sgu), vec(sgu),
                  pl.BlockSpec((pool, pool), lambda i: (0, 0)), vec(pool)],
        out_specs=pl.BlockSpec((tm, d_mix), lambda i: (i, 0)),
        out_shape=jax.ShapeDtypeStruct((t, d_mix), BF16),
        compiler_params=_cparams(("parallel",)),
    )(z, z, p["sgu_ln_g"], p["sgu_ln_b"], p["w_spatial"], p["bs_full"], p["conv_w"], p["conv_b"],
      p["conv_ln_g"], p["conv_ln_b"], p["bd"], p["pool_scale"])


_R_SGU_G, _R_SGU_B, _R_CONV_B, _R_CLN_G, _R_CLN_B, _R_CONV_W = 0, 1, 2, 3, 4, 8
_R384_ROWS = 40


def _mix_bwd(z, dcat, p, name):
    t, d_in = z.shape
    sgu = p["sgu_ln_g"].shape[1]
    pool = p["pool_scale"].shape[1]
    d_mix = 2 * sgu + pool
    n_head = sgu // HEAD_DIM
    tm = 512
    n_i = t // tm
    hb = tm // HALO

    def body(z_ref, zp_ref, zn_ref, dc_ref, dcn_ref, lng_ref, lnb_ref, ws_ref, bs_ref, cw_ref, cb_ref,
             clg_ref, clb_ref, bd_ref, ps_ref, dz_ref, g384_ref, gws_ref, gpool_ref, dbs_scr):
        i = pl.program_id(0)
        first, last = i == 0, i == n_i - 1

        @pl.when(first)
        def _():
            g384_ref[...] = jnp.zeros_like(g384_ref)
            gws_ref[...] = jnp.zeros_like(gws_ref)
            gpool_ref[...] = jnp.zeros_like(gpool_ref)
            dbs_scr[...] = jnp.zeros_like(dbs_scr)

        z_main = z_ref[...]
        z_prev = jnp.where(first, 0.0, zp_ref[...])
        z_next = jnp.where(last, 0.0, zn_ref[...])
        dc_main = dc_ref[...]
        dc_next = jnp.where(last, 0.0, dcn_ref[...])

        lng, lnb = lng_ref[...], lnb_ref[...]
        wt = [_tril(ws_ref[h]) for h in range(n_head)]
        wt_b = [w.astype(BF16) for w in wt]
        wtt_b = [w.T.astype(BF16) for w in wt]
        lo = _lane_ids((CHUNK, LANES)) < HEAD_DIM
        d_lng = jnp.zeros((1, sgu), F32)
        d_lnb = jnp.zeros((1, sgu), F32)
        dws = [jnp.zeros((CHUNK, CHUNK), F32) for _ in range(n_head)]
        for n in range(tm // CHUNK):
            rows = slice(n * CHUNK, (n + 1) * CHUNK)
            au, av = z_main[rows, 0:sgu], z_main[rows, sgu:2 * sgu]
            u = _gelu(au)
            vhat, vrstd = _ln(_gelu(av))
            vn = (vhat * lng + lnb).astype(BF16)
            da = dc_main[rows, 0:sgu]
            dmixed = da * u
            dbs_scr[...] += dmixed
            dvn_parts, du_parts = [], []
            for gp in range(sgu // LANES):
                ls = slice(gp * LANES, (gp + 1) * LANES)
                vn_g = vn[:, ls]
                mixed = _head_pair_mix(wt_b[2 * gp], wt_b[2 * gp + 1], vn_g) + bs_ref[:, ls]
                du_parts.append(da[:, ls] * mixed)
                dm_g = dmixed[:, ls]
                dm_b = dm_g.astype(BF16)
                dvn_parts.append(jnp.where(lo, _dot(wtt_b[2 * gp], dm_b), _dot(wtt_b[2 * gp + 1], dm_b)))
                dws[2 * gp] = dws[2 * gp] + _dot_nt(jnp.where(lo, dm_g, 0.0).astype(BF16), vn_g)
                dws[2 * gp + 1] = dws[2 * gp + 1] + _dot_nt(jnp.where(lo, 0.0, dm_g).astype(BF16), vn_g)
            dvn = jnp.concatenate(dvn_parts, axis=1)
            du = jnp.concatenate(du_parts, axis=1)
            dv, dg_n, db_n = _ln_bwd(vhat, vrstd, lng, dvn)
            d_lng = d_lng + dg_n
            d_lnb = d_lnb + db_n
            dz_ref[rows, 0:sgu] = (du * _gelu_grad(au)).astype(BF16)
            dz_ref[rows, sgu:2 * sgu] = (dv * _gelu_grad(av)).astype(BF16)
        for h in range(n_head):
            gws_ref[h] += _tril(dws[h])
        g384_ref[_R_SGU_G:_R_SGU_G + 1, :] += d_lng
        g384_ref[_R_SGU_B:_R_SGU_B + 1, :] += d_lnb

        @pl.when(last)
        def _():
            r = lax.broadcasted_iota(jnp.int32, (sgu, LANES), 0)
            c = lax.broadcasted_iota(jnp.int32, (sgu, LANES), 1)
            sel = (r // HEAD_DIM == c).astype(BF16)
            gws_ref[n_head] = _split_dot(dbs_scr[...], sel)

        cw, clg = cw_ref[...], clg_ref[...]
        bcols = slice(2 * sgu, 4 * sgu)
        zb = jnp.concatenate([z_prev[:, bcols], z_main[:, bcols], z_next[:, bcols]], axis=0)
        bval, bgate = zb[:, 0:sgu], zb[:, sgu:2 * sgu]
        sg = jax.nn.sigmoid(bgate)
        hglu = bval * sg
        n_out = tm + HALO
        hc = _causal_conv(hglu, cw, n_out) + cb_ref[...]
        hhat, hrstd = _ln(hc)
        bn = hhat * clg + clb_ref[...]
        db = jnp.concatenate([dc_main[:, sgu:2 * sgu], dc_next[:, sgu:2 * sgu]], axis=0)
        dbn = db * _silu_grad(bn)
        dhc_all, _, _ = _ln_bwd(hhat, hrstd, clg, dbn)
        dbn_m, hhat_m, dhc = dbn[:tm], hhat[:tm], dhc_all[:tm]
        g384_ref[_R_CLN_G:_R_CLN_G + 1, :] += jnp.sum(dbn_m * hhat_m, axis=0, keepdims=True)
        g384_ref[_R_CLN_B:_R_CLN_B + 1, :] += jnp.sum(dbn_m, axis=0, keepdims=True)
        g384_ref[_R_CONV_B:_R_CONV_B + 1, :] += jnp.sum(dhc, axis=0, keepdims=True)
        wrows = _conv_wgrad(hglu[:HALO + tm], dhc, tm)
        for k in range(CONV_WIDTH):
            g384_ref[_R_CONV_W + k:_R_CONV_W + k + 1, :] += wrows[k]
        dhglu = _anticausal_conv(dhc_all, cw, tm)
        bval_m, sg_m = bval[HALO:HALO + tm], sg[HALO:HALO + tm]
        dz_ref[:, 2 * sgu:3 * sgu] = (dhglu * sg_m).astype(BF16)
        dz_ref[:, 3 * sgu:4 * sgu] = (dhglu * bval_m * sg_m * (1.0 - sg_m)).astype(BF16)

        bd_b = bd_ref[...].astype(BF16)
        ps = ps_ref[...]
        p_main = z_main[:, 4 * sgu:]
        pext = jnp.concatenate([z_prev[:, 4 * sgu:], p_main], axis=0)
        cnt = _pool_count(i * tm, n_out, pool)
        pooled = _window_sums(pext, _shift_down)[HALO:] / cnt[:tm] - p_main
        pooled_b = pooled.astype(BF16)
        dcc = jnp.concatenate([dc_main[:, 2 * sgu:], dc_next[:, 2 * sgu:]], axis=0)
        dmix_c = dcc * ps
        mixed_c = _dot(pooled_b, bd_b)
        grp_r = lax.broadcasted_iota(jnp.int32, (pool, pool), 0) // HEAD_DIM
        grp_c = lax.broadcasted_iota(jnp.int32, (pool, pool), 1) // HEAD_DIM
        gpool_ref[0:pool, :] += jnp.where(grp_r == grp_c, _dot_tn(pooled_b, dmix_c[:tm].astype(BF16)), 0.0)
        gpool_ref[pool:pool + 1, :] += jnp.sum(dcc[:tm] * mixed_c, axis=0, keepdims=True)
        dpooled = _dot_nt(dmix_c.astype(BF16), bd_b)
        q = dpooled / cnt
        dp = _window_sums(q, _shift_up)[:tm] - dpooled[:tm]
        dz_ref[:, 4 * sgu:] = dp.astype(BF16)

    def vec(n):
        return pl.BlockSpec((1, n), lambda i: (0, 0))

    def prev_map(i):
        return (jnp.maximum(i * hb - 1, 0), 0)

    def next_map(i):
        return (jnp.minimum((i + 1) * hb, n_i * hb - 1), 0)

    return pl.pallas_call(
        body, name=name, grid=(n_i,),
        in_specs=[pl.BlockSpec((tm, d_in), lambda i: (i, 0)),
                  pl.BlockSpec((HALO, d_in), prev_map), pl.BlockSpec((HALO, d_in), next_map),
                  pl.BlockSpec((tm, d_mix), lambda i: (i, 0)), pl.BlockSpec((HALO, d_mix), next_map),
                  vec(sgu), vec(sgu),
                  pl.BlockSpec(p["w_spatial"].shape, lambda i: (0, 0, 0)),
                  pl.BlockSpec((CHUNK, sgu), lambda i: (0, 0)),
                  pl.BlockSpec((CONV_WIDTH, sgu), lambda i: (0, 0)),
                  vec(sgu), vec(sgu), vec(sgu),
                  pl.BlockSpec((pool, pool), lambda i: (0, 0)), vec(pool)],
        out_specs=[pl.BlockSpec((tm, d_in), lambda i: (i, 0)),
                   pl.BlockSpec((_R384_ROWS, sgu), lambda i: (0, 0)),
                   pl.BlockSpec((n_head + 1, CHUNK, CHUNK), lambda i: (0, 0, 0)),
                   pl.BlockSpec((pool + 8, pool), lambda i: (0, 0))],
        out_shape=[jax.ShapeDtypeStruct((t, d_in), BF16),
                   jax.ShapeDtypeStruct((_R384_ROWS, sgu), F32),
                   jax.ShapeDtypeStruct((n_head + 1, CHUNK, CHUNK), F32),
                   jax.ShapeDtypeStruct((pool + 8, pool), F32)],
        scratch_shapes=[pltpu.VMEM((CHUNK, sgu), F32)],
        compiler_params=_cparams(("arbitrary",)),
    )(z, z, z, dcat, dcat, p["sgu_ln_g"], p["sgu_ln_b"], p["w_spatial"], p["bs_full"], p["conv_w"],
      p["conv_b"], p["conv_ln_g"], p["conv_ln_b"], p["bd"], p["pool_scale"])


def _loss_head(x, g, target, name):
    t, d = x.shape
    tm = 512

    def body(x_ref, g_ref, tgt_ref, dx_ref, dg_ref, loss_ref):
        i = pl.program_id(0)

        @pl.when(i == 0)
        def _():
            dg_ref[...] = jnp.zeros_like(dg_ref)
            loss_ref[...] = jnp.zeros_like(loss_ref)

        gv = g_ref[...]
        xhat, rstd = _rms(x_ref[...])
        err = xhat * gv - tgt_ref[...]
        loss_ref[...] += jnp.zeros_like(loss_ref) + 0.5 * jnp.sum(jnp.mean(err * err, axis=-1, keepdims=True))
        dxn, dg = _rms_bwd(xhat, rstd, gv, err * (1.0 / d))
        dx_ref[...] = dxn
        dg_ref[...] += dg

    row = pl.BlockSpec((tm, d), lambda i: (i, 0))
    vec = pl.BlockSpec((1, d), lambda i: (0, 0))
    return pl.pallas_call(
        body, name=name, grid=(t // tm,),
        in_specs=[row, vec, row],
        out_specs=[row, vec, pl.BlockSpec((1, LANES), lambda i: (0, 0))],
        out_shape=[jax.ShapeDtypeStruct((t, d), F32), jax.ShapeDtypeStruct((1, d), F32),
                   jax.ShapeDtypeStruct((1, LANES), F32)],
        compiler_params=_cparams(("arbitrary",)),
    )(x, g, target)


def _position():
    return lax.axis_index("x"), lax.axis_index("y"), lax.axis_index("c")


def _all_gather(arrs, name):
    n = len(arrs)

    def body(*refs):
        ins, outs = refs[:n], refs[n:2 * n]
        send_sems, recv_sems, local_sems = refs[2 * n:]
        x, y, c = _position()
        me, sibling = (x, y, c), (x, y, 1 - c)
        chips = [(1 - x, y), (x, 1 - y), (1 - x, 1 - y)]

        def rows(a, px, py, pc):
            r = ins[a].shape[1]
            return outs[a].at[:, pl.ds((4 * px + 2 * py + pc) * r, r), :]

        def copy(a, k, block, to, src=None):
            dst = rows(a, *block)
            return pltpu.make_async_remote_copy(
                src_ref=dst if src is None else src, dst_ref=dst,
                send_sem=send_sems.at[a * 7 + k], recv_sem=recv_sems.at[a * 7 + k],
                device_id=to, device_id_type=MESH)

        mine = [pltpu.make_async_copy(ins[a], rows(a, *me), local_sems.at[a]) for a in range(n)]
        for cp in mine:
            cp.start()
        first = []
        for a in range(n):
            first.append(copy(a, 0, me, sibling, src=ins[a]))
            first += [copy(a, 1 + j, me, (*chip, c), src=ins[a]) for j, chip in enumerate(chips)]
        for cp in first:
            cp.start()
        passed = []
        for j, chip in enumerate(chips):
            for a in range(n):
                copy(a, 1 + j, (*chip, c), me).wait_recv()
                fwd = copy(a, 4 + j, (*chip, c), sibling)
                fwd.start()
                passed.append(fwd)
        for a in range(n):
            copy(a, 0, sibling, me).wait_recv()
            for j, chip in enumerate(chips):
                copy(a, 4 + j, (*chip, 1 - c), me).wait_recv()
        for cp in first + passed:
            cp.wait_send()
        for cp in mine:
            cp.wait()

    return pl.pallas_call(
        body, name=name,
        in_specs=[ANY] * n, out_specs=[ANY] * n,
        out_shape=[jax.ShapeDtypeStruct((a.shape[0], N_DEV * a.shape[1], a.shape[2]), a.dtype) for a in arrs],
        scratch_shapes=[pltpu.SemaphoreType.DMA((7 * n,)), pltpu.SemaphoreType.DMA((7 * n,)),
                        pltpu.SemaphoreType.DMA((n,))],
    )(*arrs)


def _pair_exchange(grads, name):
    n = len(grads)

    def body(*refs):
        ins, outs = refs[:n], refs[n:2 * n]
        send_sems, recv_sems = refs[2 * n:]
        x, y, c = _position()
        copies = []
        for a in range(n):
            r = ins[a].shape[0] // N_DEV
            for q in range(N_CHIP):
                copies.append(pltpu.make_async_remote_copy(
                    src_ref=ins[a].at[pl.ds((2 * q + 1 - c) * r, r), :],
                    dst_ref=outs[a].at[pl.ds(q * r, r), :],
                    send_sem=send_sems.at[a * N_CHIP + q], recv_sem=recv_sems.at[a * N_CHIP + q],
                    device_id=(x, y, 1 - c), device_id_type=MESH))
        for cp in copies:
            cp.start()
        for cp in copies:
            cp.wait()

    return pl.pallas_call(
        body, name=name,
        in_specs=[ANY] * n, out_specs=[ANY] * n,
        out_shape=[jax.ShapeDtypeStruct((g.shape[0] // 2, g.shape[1]), g.dtype) for g in grads],
        scratch_shapes=[pltpu.SemaphoreType.DMA((N_CHIP * n,)), pltpu.SemaphoreType.DMA((N_CHIP * n,))],
    )(*grads)


def _pair_sum(grad, recv, cidx, name):
    r = grad.shape[0] // N_DEV
    cols = grad.shape[1]

    def body(c_ref, g_ref, r_ref, o_ref):
        o_ref[...] = (g_ref[...].astype(F32) + r_ref[...].astype(F32)).astype(BF16)

    return pl.pallas_call(
        body, name=name,
        grid_spec=pltpu.PrefetchScalarGridSpec(
            num_scalar_prefetch=1, grid=(N_CHIP,),
            in_specs=[pl.BlockSpec((r, cols), lambda q, c: (2 * q + c[0], 0)),
                      pl.BlockSpec((r, cols), lambda q, c: (q, 0))],
            out_specs=pl.BlockSpec((r, cols), lambda q, c: (q, 0))),
        out_shape=jax.ShapeDtypeStruct((N_CHIP * r, cols), BF16),
        compiler_params=_cparams(("parallel",)),
    )(cidx, grad, recv)


def _chip_exchange(sums, name):
    n = len(sums)

    def body(*refs):
        ins, outs = refs[:n], refs[n:2 * n]
        send_sems, recv_sems, local_sems = refs[2 * n:]
        x, y, c = _position()
        my_chip = 2 * x + y
        flips = [(1, 0), (0, 1), (1, 1)]
        local, remote = [], []
        for a in range(n):
            r = ins[a].shape[0] // N_CHIP
            local.append(pltpu.make_async_copy(ins[a].at[pl.ds(my_chip * r, r), :],
                                               outs[a].at[pl.ds(my_chip * r, r), :], local_sems.at[a]))
            for k, (fx, fy) in enumerate(flips):
                px, py = x ^ fx, y ^ fy
                remote.append(pltpu.make_async_remote_copy(
                    src_ref=ins[a].at[pl.ds((2 * px + py) * r, r), :],
                    dst_ref=outs[a].at[pl.ds(my_chip * r, r), :],
                    send_sem=send_sems.at[a * 3 + k], recv_sem=recv_sems.at[a * 3 + k],
                    device_id=(px, py, c), device_id_type=MESH))
        for cp in local + remote:
            cp.start()
        for a in range(n):
            r = ins[a].shape[0] // N_CHIP
            for k, (fx, fy) in enumerate(flips):
                src_chip = 2 * (x ^ fx) + (y ^ fy)
                landing = outs[a].at[pl.ds(src_chip * r, r), :]
                pltpu.make_async_remote_copy(
                    src_ref=landing, dst_ref=landing,
                    send_sem=send_sems.at[a * 3 + k], recv_sem=recv_sems.at[a * 3 + k],
                    device_id=(x ^ fx, y ^ fy, c), device_id_type=MESH).wait_recv()
        for cp in remote:
            cp.wait_send()
        for cp in local:
            cp.wait()

    return pl.pallas_call(
        body, name=name,
        in_specs=[ANY] * n, out_specs=[ANY] * n,
        out_shape=[jax.ShapeDtypeStruct(s.shape, s.dtype) for s in sums],
        scratch_shapes=[pltpu.SemaphoreType.DMA((3 * n,)), pltpu.SemaphoreType.DMA((3 * n,)),
                        pltpu.SemaphoreType.DMA((n,))],
    )(*sums)


def _sum_blocks(parts, nblk, name):
    r = parts.shape[0] // nblk
    cols = parts.shape[1]

    def body(p_ref, o_ref):
        acc = p_ref[0:r, :].astype(F32)
        for q in range(1, nblk):
            acc = acc + p_ref[q * r:(q + 1) * r, :].astype(F32)
        o_ref[...] = acc

    return pl.pallas_call(
        body, name=name,
        out_shape=jax.ShapeDtypeStruct((r, cols), F32),
        compiler_params=_cparams(),
    )(parts)


def _adamw_math(w, g, m, v):
    m = ADAM_B1 * m + (1.0 - ADAM_B1) * g
    v = ADAM_B2 * v + (1.0 - ADAM_B2) * (g * g)
    m_hat = m / (1.0 - ADAM_B1 ** ADAM_STEP)
    v_hat = v / (1.0 - ADAM_B2 ** ADAM_STEP)
    delta = -ADAM_LR * (m_hat / (jnp.sqrt(v_hat) + ADAM_EPS) + ADAM_WD * w)
    return delta, m, v


def _adamw_big(w, g, m, v, name):
    rows, cols = w.shape
    tr = max(k for k in range(8, 513, 8) if rows % k == 0)

    def body(w_ref, g_ref, m_ref, v_ref, d_ref, mo_ref, vo_ref):
        d_ref[...], mo_ref[...], vo_ref[...] = _adamw_math(w_ref[...], g_ref[...], m_ref[...], v_ref[...])

    spec = pl.BlockSpec((tr, cols), lambda i: (i, 0))
    return pl.pallas_call(
        body, name=name, grid=(rows // tr,),
        in_specs=[spec] * 4, out_specs=[spec] * 3,
        out_shape=[jax.ShapeDtypeStruct(w.shape, F32)] * 3,
        compiler_params=_cparams(("parallel",)),
    )(w, g, m, v)


def _adamw_small(ws, gs, ms, vs, name):
    n = len(ws)

    def body(*refs):
        for k in range(n):
            w_ref, g_ref, m_ref, v_ref = (refs[j * n + k] for j in range(4))
            d_ref, mo_ref, vo_ref = (refs[(4 + j) * n + k] for j in range(3))
            d_ref[...], mo_ref[...], vo_ref[...] = _adamw_math(w_ref[...], g_ref[...], m_ref[...], v_ref[...])

    shapes = [jax.ShapeDtypeStruct(w.shape, F32) for w in ws]
    return pl.pallas_call(
        body, name=name, out_shape=shapes * 3, compiler_params=_cparams(),
    )(*ws, *gs, *ms, *vs)


def kernel(x, ffn1_norm, ffn1_w_gate, ffn1_w_up, ffn1_w_down, mix_norm, w_in, sgu_ln_g, sgu_ln_b, w_spatial, b_spatial, conv_w, conv_b, conv_ln_g, conv_ln_b, pool_w, pool_scale, w_out, ffn2_norm, ffn2_w_gate, ffn2_w_up, ffn2_w_down, final_norm, loss_target, m_ffn1_norm, m_ffn1_w_gate, m_ffn1_w_up, m_ffn1_w_down, m_mix_norm, m_w_in, m_sgu_ln_g, m_sgu_ln_b, m_w_spatial, m_b_spatial, m_conv_w, m_conv_b, m_conv_ln_g, m_conv_ln_b, m_pool_w, m_pool_scale, m_w_out, m_ffn2_norm, m_ffn2_w_gate, m_ffn2_w_up, m_ffn2_w_down, m_final_norm, v_ffn1_norm, v_ffn1_w_gate, v_ffn1_w_up, v_ffn1_w_down, v_mix_norm, v_w_in, v_sgu_ln_g, v_sgu_ln_b, v_w_spatial, v_b_spatial, v_conv_w, v_conv_b, v_conv_ln_g, v_conv_ln_b, v_pool_w, v_pool_scale, v_w_out, v_ffn2_norm, v_ffn2_w_gate, v_ffn2_w_up, v_ffn2_w_down, v_final_norm):
    names = ["ffn1_norm", "ffn1_w_gate", "ffn1_w_up", "ffn1_w_down", "mix_norm", "w_in", "sgu_ln_g", "sgu_ln_b",
             "w_spatial", "b_spatial", "conv_w", "conv_b", "conv_ln_g", "conv_ln_b", "pool_w", "pool_scale",
             "w_out", "ffn2_norm", "ffn2_w_gate", "ffn2_w_up", "ffn2_w_down", "final_norm"]
    W = dict(zip(names, [ffn1_norm, ffn1_w_gate, ffn1_w_up, ffn1_w_down, mix_norm, w_in, sgu_ln_g, sgu_ln_b,
                         w_spatial, b_spatial, conv_w, conv_b, conv_ln_g, conv_ln_b, pool_w, pool_scale, w_out,
                         ffn2_norm, ffn2_w_gate, ffn2_w_up, ffn2_w_down, final_norm]))
    M = dict(zip(names, [m_ffn1_norm, m_ffn1_w_gate, m_ffn1_w_up, m_ffn1_w_down, m_mix_norm, m_w_in, m_sgu_ln_g,
                         m_sgu_ln_b, m_w_spatial, m_b_spatial, m_conv_w, m_conv_b, m_conv_ln_g, m_conv_ln_b,
                         m_pool_w, m_pool_scale, m_w_out, m_ffn2_norm, m_ffn2_w_gate, m_ffn2_w_up, m_ffn2_w_down,
                         m_final_norm]))
    V = dict(zip(names, [v_ffn1_norm, v_ffn1_w_gate, v_ffn1_w_up, v_ffn1_w_down, v_mix_norm, v_w_in, v_sgu_ln_g,
                         v_sgu_ln_b, v_w_spatial, v_b_spatial, v_conv_w, v_conv_b, v_conv_ln_g, v_conv_ln_b,
                         v_pool_w, v_pool_scale, v_w_out, v_ffn2_norm, v_ffn2_w_gate, v_ffn2_w_up, v_ffn2_w_down,
                         v_final_norm]))

    depth, d = ffn1_norm.shape
    t = x.shape[1]
    sgu = sgu_ln_g.shape[1]
    pool = pool_scale.shape[1]
    n_head = sgu // HEAD_DIM
    cw_shard = conv_w.shape[2]
    xs = x.reshape(t, d)
    target = loss_target.reshape(t, d)

    def tr(w):
        return jnp.swapaxes(w, 1, 2).astype(BF16)

    per_layer = jnp.stack([tr(ffn1_w_gate), tr(ffn1_w_up), ffn1_w_down.astype(BF16),
                           tr(ffn2_w_gate), tr(ffn2_w_up), ffn2_w_down.astype(BF16)], axis=1)
    wa_shard = per_layer.reshape(depth * 6, per_layer.shape[2], d)
    cw_rows = depth * CONV_WIDTH
    cw_pad = -cw_rows % 8
    cw_send = jnp.pad(conv_w.reshape(cw_rows, cw_shard), ((0, cw_pad), (0, 0)))[None]
    wa, wb, wc, cwg = _all_gather([wa_shard, tr(w_in), w_out.astype(BF16), cw_send], "ag_weights")
    conv_w_full = cwg.reshape(N_DEV, cw_rows + cw_pad, cw_shard)[:, :cw_rows].reshape(
        N_DEV, depth, CONV_WIDTH, cw_shard).transpose(1, 2, 0, 3).reshape(depth, CONV_WIDTH, N_DEV * cw_shard)

    def mixer_params(l):
        return dict(
            sgu_ln_g=sgu_ln_g[l:l + 1], sgu_ln_b=sgu_ln_b[l:l + 1], w_spatial=w_spatial[l],
            bs_full=jnp.repeat(b_spatial[l].T, HEAD_DIM, axis=1),
            conv_w=conv_w_full[l], conv_b=conv_b[l:l + 1], conv_ln_g=conv_ln_g[l:l + 1],
            conv_ln_b=conv_ln_b[l:l + 1], bd=_block_diag(pool_w[l]), pool_scale=pool_scale[l:l + 1])

    saved = []
    cur = xs
    for l in range(depth):
        p = mixer_params(l)
        x0 = cur
        x1, gate1, up1 = _ffn_fwd(x0, ffn1_norm[l:l + 1], wa, 6 * l, f"ffn1_fwd_{l}")
        z = _proj_in_fwd(x1, mix_norm[l:l + 1], wb, l, f"proj_in_fwd_{l}")
        cat = _mix_fwd(z, p, f"mix_fwd_{l}")
        x2 = _proj_out_fwd(x1, cat, wc, l, f"proj_out_fwd_{l}")
        x3, gate2, up2 = _ffn_fwd(x2, ffn2_norm[l:l + 1], wa, 6 * l + 3, f"ffn2_fwd_{l}")
        saved.append((p, x0, gate1, up1, x1, z, cat, x2, gate2, up2))
        cur = x3

    dx, d_final, loss_part = _loss_head(cur, final_norm.reshape(1, d), target, "loss_head")

    big = {}
    small = []
    for l in reversed(range(depth)):
        p, x0, gate1, up1, x1, z, cat, x2, gate2, up2 = saved[l]
        dx, dgate, dup, act, h, dy, dg_ffn2 = _ffn_bwd(x2, ffn2_norm[l:l + 1], dx, gate2, up2, wa, 6 * l + 3,
                                                      f"ffn2_bwd_{l}")
        big[("ffn2_w_gate", l)] = _tn_matmul(dgate, h, f"dw_ffn2_gate_{l}")
        big[("ffn2_w_up", l)] = _tn_matmul(dup, h, f"dw_ffn2_up_{l}")
        big[("ffn2_w_down", l)] = _tn_matmul(act, dy, f"dw_ffn2_down_{l}")
        big[("w_out", l)] = _tn_matmul(cat, dx, f"dw_out_{l}")
        dcat = _proj_out_bwd(dx, wc, l, f"proj_out_bwd_{l}")
        dz, g384, gws, gpool = _mix_bwd(z, dcat, p, f"mix_bwd_{l}")
        dx, hm, dg_mix = _proj_in_bwd(x1, mix_norm[l:l + 1], dx, dz, wb, l, f"proj_in_bwd_{l}")
        big[("w_in", l)] = _tn_matmul(dz, hm, f"dw_in_{l}")
        dx, dgate, dup, act, h, dy, dg_ffn1 = _ffn_bwd(x0, ffn1_norm[l:l + 1], dx, gate1, up1, wa, 6 * l,
                                                      f"ffn1_bwd_{l}")
        big[("ffn1_w_gate", l)] = _tn_matmul(dgate, h, f"dw_ffn1_gate_{l}")
        big[("ffn1_w_up", l)] = _tn_matmul(dup, h, f"dw_ffn1_up_{l}")
        big[("ffn1_w_down", l)] = _tn_matmul(act, dy, f"dw_ffn1_down_{l}")
        small.append((l, g384, gws, gpool, dg_ffn1, dg_mix, dg_ffn2))
    grad_x = dx.reshape(x.shape)

    big_names = ["ffn1_w_gate", "ffn1_w_up", "ffn1_w_down", "w_in", "w_out", "ffn2_w_gate", "ffn2_w_up",
                 "ffn2_w_down"]
    transposed = {"ffn1_w_gate", "ffn1_w_up", "w_in", "ffn2_w_gate", "ffn2_w_up"}
    keys = [(nm, l) for nm in big_names for l in range(depth)]
    full = [big[k] for k in keys]
    cidx = lax.axis_index("c").astype(jnp.int32).reshape(1)
    from_sibling = _pair_exchange(full, "rs_pair_exchange")
    pair_sums = [_pair_sum(g, r, cidx, f"rs_pair_sum_{nm}_{l}") for (nm, l), g, r in zip(keys, full, from_sibling)]
    from_chips = _chip_exchange(pair_sums, "rs_chip_exchange")
    grads = {}
    for (nm, l), parts in zip(keys, from_chips):
        g = _sum_blocks(parts, N_CHIP, f"rs_sum_{nm}_{l}")
        grads.setdefault(nm, [None] * depth)[l] = g.T if nm in transposed else g
    grads = {nm: jnp.stack(gl, axis=0) for nm, gl in grads.items()}

    small.sort(key=lambda s: s[0])
    norm_rows = []
    for (_, _, _, _, dg1, dgm, dg2) in small:
        norm_rows += [dg1, dgm, dg2]
    norm_rows += [d_final, jnp.pad(loss_part, ((0, 0), (0, d - LANES)))]
    n_norm = len(norm_rows)
    norm_pack = jnp.concatenate(norm_rows + [jnp.zeros((8 - n_norm % 8, d), F32)] * (n_norm % 8 != 0), axis=0)
    parts = [norm_pack]
    for (_, g384, gws, gpool, _, _, _) in small:
        parts += [g384, gws.reshape((n_head + 1) * CHUNK, CHUNK), gpool]
    gathered = _all_gather([a[None] for a in parts], "ag_small_grads")
    summed = [_sum_blocks(g[0], N_DEV, f"sum_small_{k}") for k, g in enumerate(gathered)]
    norm_sum = summed[0]
    loss = norm_sum[3 * depth + 1, 0]
    cpos = lax.axis_index("x") * 4 + lax.axis_index("y") * 2 + lax.axis_index("c")
    sg = {nm: [] for nm in names}
    for l in range(depth):
        g384, gws, gpool = summed[1 + 3 * l], summed[2 + 3 * l].reshape(n_head + 1, CHUNK, CHUNK), summed[3 + 3 * l]
        sg["ffn1_norm"].append(norm_sum[3 * l])
        sg["mix_norm"].append(norm_sum[3 * l + 1])
        sg["ffn2_norm"].append(norm_sum[3 * l + 2])
        sg["sgu_ln_g"].append(g384[_R_SGU_G])
        sg["sgu_ln_b"].append(g384[_R_SGU_B])
        sg["conv_b"].append(g384[_R_CONV_B])
        sg["conv_ln_g"].append(g384[_R_CLN_G])
        sg["conv_ln_b"].append(g384[_R_CLN_B])
        sg["conv_w"].append(lax.dynamic_slice_in_dim(g384[_R_CONV_W:_R_CONV_W + CONV_WIDTH], cpos * cw_shard,
                                                     cw_shard, axis=1))
        sg["w_spatial"].append(gws[:n_head])
        sg["b_spatial"].append(gws[n_head][:, :n_head].T)
        sg["pool_w"].append(jnp.stack([gpool[k * HEAD_DIM:(k + 1) * HEAD_DIM, k * HEAD_DIM:(k + 1) * HEAD_DIM]
                                       for k in range(pool // HEAD_DIM)], axis=0))
        sg["pool_scale"].append(gpool[pool])
    small_names = ["ffn1_norm", "mix_norm", "sgu_ln_g", "sgu_ln_b", "w_spatial", "b_spatial", "conv_w", "conv_b",
                   "conv_ln_g", "conv_ln_b", "pool_w", "pool_scale", "ffn2_norm"]
    for nm in small_names:
        grads[nm] = jnp.stack(sg[nm], axis=0)
    grads["final_norm"] = norm_sum[3 * depth]

    delta, new_m, new_v = {}, {}, {}
    for nm in big_names:
        shp = W[nm].shape
        two_d = (shp[0] * shp[1], shp[2])
        dl, mo, vo = _adamw_big(W[nm].reshape(two_d), grads[nm].reshape(two_d), M[nm].reshape(two_d),
                                V[nm].reshape(two_d), f"adamw_{nm}")
        delta[nm], new_m[nm], new_v[nm] = dl.reshape(shp), mo.reshape(shp), vo.reshape(shp)
    snames = small_names + ["final_norm"]

    def flat2(a):
        return a.reshape(-1, a.shape[-1])

    outs = _adamw_small([flat2(W[nm]) for nm in snames], [flat2(grads[nm]) for nm in snames],
                        [flat2(M[nm]) for nm in snames], [flat2(V[nm]) for nm in snames], "adamw_small")
    ns = len(snames)
    for k, nm in enumerate(snames):
        shp = W[nm].shape
        delta[nm], new_m[nm], new_v[nm] = (outs[k].reshape(shp), outs[ns + k].reshape(shp),
                                           outs[2 * ns + k].reshape(shp))

    return (loss, grad_x, *[grads[nm] for nm in names], *[delta[nm] for nm in names],
            *[new_m[nm] for nm in names], *[new_v[nm] for nm in names])
```

```python
import functools

import jax
import jax.numpy as jnp
from jax import lax
from jax.experimental import pallas as pl
from jax.experimental.pallas import tpu as pltpu

F32 = jnp.float32
BF16 = jnp.bfloat16
EPS = 1e-6
N_DEV = 8
N_CHIP = 4
MESH = pl.DeviceIdType.MESH
ANY = pl.BlockSpec(memory_space=pl.ANY)

VMEM_LIMIT_BYTES = 56 * 1024 * 1024
LANES = 128
HALO = 32
HEAD_DIM = 64
CHUNK = 128
CONV_WIDTH = 31
POOL_WINDOWS = (2, 4, 8, 16)

ADAM_LR = 0.001
ADAM_B1 = 0.9
ADAM_B2 = 0.999
ADAM_EPS = 1e-08
ADAM_WD = 0.01
ADAM_STEP = 10


def _cparams(sem=None):
    return pltpu.CompilerParams(dimension_semantics=sem, vmem_limit_bytes=VMEM_LIMIT_BYTES)


def _position():
    return lax.axis_index("x"), lax.axis_index("y"), lax.axis_index("c")


class _Copies:
    def __init__(self):
        self.local, self.sends, self.recvs = [], [], []

    def extend(self, other):
        self.local += other.local
        self.sends += other.sends
        self.recvs += other.recvs

    def start(self):
        for cp in self.local + self.sends:
            cp.start()

    def wait(self):
        for cp in self.recvs:
            cp.wait_recv()
        for cp in self.sends:
            cp.wait_send()
        for cp in self.local:
            cp.wait()


def _remote(src, dst, send_sems, recv_sems, k, to):
    return pltpu.make_async_remote_copy(src_ref=src, dst_ref=dst, send_sem=send_sems.at[k], recv_sem=recv_sems.at[k],
                                        device_id=to, device_id_type=MESH)


class _Payload:
    ins, out_shapes, aliases, n_remote, n_local = (), (), {}, 0, 0

    def sem_shapes(self):
        return [pltpu.SemaphoreType.DMA((max(self.n_remote, 1),)), pltpu.SemaphoreType.DMA((max(self.n_remote, 1),)),
                pltpu.SemaphoreType.DMA((max(self.n_local, 1),))]


class _GatherIci(_Payload):
    def __init__(self, shards):
        self.ins = list(shards)
        self.out_shapes = [jax.ShapeDtypeStruct((s.shape[0], N_DEV * s.shape[1], s.shape[2]), s.dtype) for s in shards]
        self.n_remote, self.n_local = 4 * len(shards), len(shards)

    def build(self, ins, outs, send_sems, recv_sems, local_sems, k0=0, l0=0):
        x, y, c = _position()
        peers = [(x, y, 1 - c), (1 - x, y, c), (x, 1 - y, c), (1 - x, 1 - y, c)]
        cps = _Copies()
        for a, (src, out) in enumerate(zip(ins, outs)):
            r = src.shape[1]

            def rows(px, py, pc, out=out, r=r):
                return out.at[:, pl.ds((4 * px + 2 * py + pc) * r, r), :]

            cps.local.append(pltpu.make_async_copy(src, rows(x, y, c), local_sems.at[l0 + a]))
            for k, peer in enumerate(peers):
                cps.sends.append(_remote(src, rows(x, y, c), send_sems, recv_sems, k0 + 4 * a + k, peer))
                cps.recvs.append(_remote(rows(*peer), rows(*peer), send_sems, recv_sems, k0 + 4 * a + k, peer))
        return cps


class _GatherForward(_Payload):
    def __init__(self, partials):
        self.ins = list(partials)
        self.out_shapes = [jax.ShapeDtypeStruct(p.shape, p.dtype) for p in partials]
        self.aliases = {a: a for a in range(len(partials))}
        self.n_remote = 3 * len(partials)

    def build(self, ins, outs, send_sems, recv_sems, local_sems, k0=0, l0=0):
        x, y, c = _position()
        chips = [(1 - x, y), (x, 1 - y), (1 - x, 1 - y)]
        cps = _Copies()
        for a, out in enumerate(outs):
            r = out.shape[1] // N_DEV
            for k, (px, py) in enumerate(chips):
                mine = out.at[:, pl.ds((4 * px + 2 * py + c) * r, r), :]
                theirs = out.at[:, pl.ds((4 * px + 2 * py + 1 - c) * r, r), :]
                cps.sends.append(_remote(mine, mine, send_sems, recv_sems, k0 + 3 * a + k, (x, y, 1 - c)))
                cps.recvs.append(_remote(theirs, theirs, send_sems, recv_sems, k0 + 3 * a + k, (x, y, 1 - c)))
        return cps


class _PairExchange(_Payload):
    def __init__(self, grads):
        self.ins = list(grads)
        self.out_shapes = [jax.ShapeDtypeStruct((g.shape[0] // 2, g.shape[1]), g.dtype) for g in grads]
        self.n_remote = N_CHIP * len(grads)

    def build(self, ins, outs, send_sems, recv_sems, local_sems, k0=0, l0=0):
        x, y, c = _position()
        cps = _Copies()
        for a, (src, out) in enumerate(zip(ins, outs)):
            r = src.shape[0] // N_DEV
            for q in range(N_CHIP):
                land = out.at[pl.ds(q * r, r), :]
                cps.sends.append(_remote(src.at[pl.ds((2 * q + 1 - c) * r, r), :], land, send_sems, recv_sems,
                                         k0 + N_CHIP * a + q, (x, y, 1 - c)))
                cps.recvs.append(_remote(land, land, send_sems, recv_sems, k0 + N_CHIP * a + q, (x, y, 1 - c)))
        return cps


class _ChipExchange(_Payload):
    def __init__(self, sums):
        self.ins = list(sums)
        self.out_shapes = [jax.ShapeDtypeStruct(s.shape, s.dtype) for s in sums]
        self.n_remote, self.n_local = 3 * len(sums), len(sums)

    def build(self, ins, outs, send_sems, recv_sems, local_sems, k0=0, l0=0):
        x, y, c = _position()
        my_chip = 2 * x + y
        chips = [(1 - x, y), (x, 1 - y), (1 - x, 1 - y)]
        cps = _Copies()
        for a, (src, out) in enumerate(zip(ins, outs)):
            r = src.shape[0] // N_CHIP
            mine = out.at[pl.ds(my_chip * r, r), :]
            cps.local.append(pltpu.make_async_copy(src.at[pl.ds(my_chip * r, r), :], mine, local_sems.at[l0 + a]))
            for k, (px, py) in enumerate(chips):
                land = out.at[pl.ds((2 * px + py) * r, r), :]
                cps.sends.append(_remote(src.at[pl.ds((2 * px + py) * r, r), :], mine, send_sems, recv_sems,
                                         k0 + 3 * a + k, (px, py, c)))
                cps.recvs.append(_remote(land, land, send_sems, recv_sems, k0 + 3 * a + k, (px, py, c)))
        return cps


class _Merged(_Payload):
    def __init__(self, parts):
        self.parts = list(parts)
        self.ins = [a for p in parts for a in p.ins]
        self.out_shapes = [s for p in parts for s in p.out_shapes]
        self.aliases, self.offsets = {}, []
        i0 = o0 = k0 = l0 = 0
        for p in parts:
            self.offsets.append((i0, o0, k0, l0))
            self.aliases.update({i0 + i: o0 + o for i, o in p.aliases.items()})
            i0, o0, k0, l0 = i0 + len(p.ins), o0 + len(p.out_shapes), k0 + p.n_remote, l0 + p.n_local
        self.n_remote, self.n_local = k0, l0

    def build(self, ins, outs, send_sems, recv_sems, local_sems):
        cps = _Copies()
        for p, (i0, o0, k0, l0) in zip(self.parts, self.offsets):
            cps.extend(p.build(ins[i0:i0 + len(p.ins)], outs[o0:o0 + len(p.out_shapes)], send_sems, recv_sems,
                               local_sems, k0, l0))
        return cps


def _call(body, name, grid, in_specs, out_specs, out_shape, scratch_shapes, semantics, args, payload=None):
    if payload is None:
        outs = pl.pallas_call(body, name=name, grid=grid, in_specs=in_specs, out_specs=out_specs,
                              out_shape=out_shape, scratch_shapes=scratch_shapes,
                              compiler_params=_cparams(semantics))(*args)
        return list(outs), []
    n_in, n_out, n_scr = len(in_specs), len(out_specs), len(scratch_shapes)
    p_in, p_out = len(payload.ins), len(payload.out_shapes)

    def carried(*refs):
        ins, p_ins = refs[:n_in], refs[n_in:n_in + p_in]
        o0 = n_in + p_in
        outs, p_outs = refs[o0:o0 + n_out], refs[o0 + n_out:o0 + n_out + p_out]
        s0 = o0 + n_out + p_out
        scr, sems = refs[s0:s0 + n_scr], refs[s0 + n_scr:]
        ids = [pl.program_id(k) for k in range(len(grid))]
        at_first = functools.reduce(jnp.logical_and, [i == 0 for i in ids])
        at_last = functools.reduce(jnp.logical_and, [i == g - 1 for i, g in zip(ids, grid)])

        @pl.when(at_first)
        def _():
            payload.build(p_ins, p_outs, *sems).start()

        body(*ins, *outs, *scr)

        @pl.when(at_last)
        def _():
            payload.build(p_ins, p_outs, *sems).wait()

    outs = pl.pallas_call(
        carried, name=name, grid=grid, in_specs=list(in_specs) + [ANY] * p_in,
        out_specs=list(out_specs) + [ANY] * p_out, out_shape=list(out_shape) + list(payload.out_shapes),
        scratch_shapes=list(scratch_shapes) + payload.sem_shapes(),
        input_output_aliases={n_in + i: n_out + o for i, o in payload.aliases.items()},
        compiler_params=_cparams(("arbitrary",) * len(grid)))(*args, *payload.ins)
    return list(outs[:n_out]), list(outs[n_out:])


def _comm(payload, name):
    def body(*refs):
        p_in, p_out = len(payload.ins), len(payload.out_shapes)
        cps = payload.build(refs[:p_in], refs[p_in:p_in + p_out], *refs[p_in + p_out:])
        cps.start()
        cps.wait()

    return list(pl.pallas_call(
        body, name=name, in_specs=[ANY] * len(payload.ins), out_specs=[ANY] * len(payload.out_shapes),
        out_shape=list(payload.out_shapes), scratch_shapes=payload.sem_shapes(),
        input_output_aliases=dict(payload.aliases))(*payload.ins))


def _dot(a, b):
    return jnp.dot(a, b, preferred_element_type=F32)


def _dot_nt(a, b):
    return lax.dot_general(a, b, (((1,), (1,)), ((), ())), preferred_element_type=F32)


def _dot_tn(a, b):
    return lax.dot_general(a, b, (((0,), (0,)), ((), ())), preferred_element_type=F32)


def _split_dot(x, e):
    hi = x.astype(BF16)
    r1 = x - hi.astype(F32)
    mid = r1.astype(BF16)
    lo = (r1 - mid.astype(F32)).astype(BF16)
    return _dot(hi, e) + _dot(mid, e) + _dot(lo, e)


def _rms(x):
    rstd = lax.rsqrt(jnp.mean(x * x, axis=-1, keepdims=True) + EPS)
    return x * rstd, rstd


def _rms_bwd(xhat, rstd, g, dh):
    dxhat = dh * g
    dx = rstd * (dxhat - xhat * jnp.mean(dxhat * xhat, axis=-1, keepdims=True))
    return dx, jnp.sum(dh * xhat, axis=0, keepdims=True)


def _ln(v):
    mu = jnp.mean(v, axis=-1, keepdims=True)
    xc = v - mu
    rstd = lax.rsqrt(jnp.mean(xc * xc, axis=-1, keepdims=True) + EPS)
    return xc * rstd, rstd


def _ln_bwd(vhat, rstd, g, dy):
    dvhat = dy * g
    dv = rstd * (dvhat - jnp.mean(dvhat, axis=-1, keepdims=True)
                 - vhat * jnp.mean(dvhat * vhat, axis=-1, keepdims=True))
    return dv, jnp.sum(dy * vhat, axis=0, keepdims=True), jnp.sum(dy, axis=0, keepdims=True)


_INV_SQRT2 = 0.7071067811865476
_INV_SQRT2PI = 0.3989422804014327


def _gelu(x):
    return 0.5 * x * (1.0 + lax.erf(x * _INV_SQRT2))


def _gelu_grad(x):
    return 0.5 * (1.0 + lax.erf(x * _INV_SQRT2)) + x * jnp.exp(-0.5 * x * x) * _INV_SQRT2PI


def _silu_grad(x):
    s = jax.nn.sigmoid(x)
    return s * (1.0 + x * (1.0 - s))


def _ffn_fwd(x, g, wa, mi, name, payload=None):
    t, d = x.shape
    f = wa.shape[1]
    tm, tf = 1024, 256
    nc = f // tf

    def body(x_ref, g_ref, wg_ref, wu_ref, wd_ref, xo_ref, gate_ref, up_ref, h_scr, acc_scr):
        c = pl.program_id(1)

        @pl.when(c == 0)
        def _():
            xhat, _ = _rms(x_ref[...])
            h_scr[...] = (xhat * g_ref[...]).astype(BF16)
            acc_scr[...] = jnp.zeros_like(acc_scr)

        h = h_scr[...]
        gate = _dot_nt(h, wg_ref[...])
        up = _dot_nt(h, wu_ref[...])
        gate_ref[...] = gate.astype(BF16)
        up_ref[...] = up.astype(BF16)
        act = (gate * jax.nn.sigmoid(gate) * up).astype(BF16)
        acc_scr[...] += _dot(act, wd_ref[...])

        @pl.when(c == nc - 1)
        def _():
            xo_ref[...] = x_ref[...] + 0.5 * acc_scr[...]

    def wspec(k):
        return pl.BlockSpec((None, tf, d), lambda i, c: (mi + k, c, 0))

    return _call(
        body, name, (t // tm, nc),
        [pl.BlockSpec((tm, d), lambda i, c: (i, 0)), pl.BlockSpec((1, d), lambda i, c: (0, 0)),
         wspec(0), wspec(1), wspec(2)],
        [pl.BlockSpec((tm, d), lambda i, c: (i, 0)), pl.BlockSpec((tm, tf), lambda i, c: (i, c)),
         pl.BlockSpec((tm, tf), lambda i, c: (i, c))],
        [jax.ShapeDtypeStruct((t, d), F32), jax.ShapeDtypeStruct((t, f), BF16), jax.ShapeDtypeStruct((t, f), BF16)],
        [pltpu.VMEM((tm, d), BF16), pltpu.VMEM((tm, d), F32)],
        ("parallel", "arbitrary"), (x, g, wa, wa, wa), payload)


def _ffn_bwd(x, g, dxo, gate, up, wa, mi, name, payload=None):
    t, d = x.shape
    f = wa.shape[1]
    tm, tf = 512, 256
    nc = f // tf

    def body(x_ref, g_ref, dxo_ref, gate_ref, up_ref, wg_ref, wu_ref, wd_ref,
             dx_ref, dgate_ref, dup_ref, act_ref, h_ref, dy_ref, dg_ref, h_scr, dy_scr, acc_scr):
        i, c = pl.program_id(0), pl.program_id(1)

        @pl.when(c == 0)
        def _():
            xhat, _ = _rms(x_ref[...])
            h = (xhat * g_ref[...]).astype(BF16)
            h_scr[...] = h
            h_ref[...] = h
            dy = (0.5 * dxo_ref[...]).astype(BF16)
            dy_scr[...] = dy
            dy_ref[...] = dy
            acc_scr[...] = jnp.zeros_like(acc_scr)

        @pl.when((c == 0) & (i == 0))
        def _():
            dg_ref[...] = jnp.zeros_like(dg_ref)

        gt = gate_ref[...].astype(F32)
        u = up_ref[...].astype(F32)
        s = jax.nn.sigmoid(gt)
        silu = gt * s
        dact = _dot_nt(dy_scr[...], wd_ref[...])
        dgate = (dact * u * (s * (1.0 + gt * (1.0 - s)))).astype(BF16)
        dup = (dact * silu).astype(BF16)
        dgate_ref[...] = dgate
        dup_ref[...] = dup
        act_ref[...] = (silu * u).astype(BF16)
        acc_scr[...] += _dot(dgate, wg_ref[...]) + _dot(dup, wu_ref[...])

        @pl.when(c == nc - 1)
        def _():
            xhat, rstd = _rms(x_ref[...])
            dxn, dg = _rms_bwd(xhat, rstd, g_ref[...], acc_scr[...])
            dx_ref[...] = dxo_ref[...] + dxn
            dg_ref[...] += dg

    def wspec(k):
        return pl.BlockSpec((None, tf, d), lambda i, c: (mi + k, c, 0))

    row = pl.BlockSpec((tm, d), lambda i, c: (i, 0))
    col = pl.BlockSpec((tm, tf), lambda i, c: (i, c))
    vec = pl.BlockSpec((1, d), lambda i, c: (0, 0))
    return _call(
        body, name, (t // tm, nc),
        [row, vec, row, col, col, wspec(0), wspec(1), wspec(2)],
        [row, col, col, col, row, row, vec],
        [jax.ShapeDtypeStruct((t, d), F32), jax.ShapeDtypeStruct((t, f), BF16),
         jax.ShapeDtypeStruct((t, f), BF16), jax.ShapeDtypeStruct((t, f), BF16),
         jax.ShapeDtypeStruct((t, d), BF16), jax.ShapeDtypeStruct((t, d), BF16),
         jax.ShapeDtypeStruct((1, d), F32)],
        [pltpu.VMEM((tm, d), BF16), pltpu.VMEM((tm, d), BF16), pltpu.VMEM((tm, d), F32)],
        ("arbitrary", "arbitrary"), (x, g, dxo, gate, up, wa, wa, wa), payload)


def _tn_matmul(a, b, name, payload=None):
    t, m = a.shape
    n = b.shape[1]
    tk = 512
    tmm = m // 2 if (m // 2) % LANES == 0 else m
    nk = t // tk

    def body(a_ref, b_ref, o_ref, acc_scr):
        k = pl.program_id(1)

        @pl.when(k == 0)
        def _():
            acc_scr[...] = jnp.zeros_like(acc_scr)

        acc_scr[...] += _dot_tn(a_ref[...].astype(BF16), b_ref[...].astype(BF16))

        @pl.when(k == nk - 1)
        def _():
            o_ref[...] = acc_scr[...].astype(BF16)

    (out,), p_outs = _call(
        body, name, (m // tmm, nk),
        [pl.BlockSpec((tk, tmm), lambda j, k: (k, j)), pl.BlockSpec((tk, n), lambda j, k: (k, 0))],
        [pl.BlockSpec((tmm, n), lambda j, k: (j, 0))],
        [jax.ShapeDtypeStruct((m, n), BF16)],
        [pltpu.VMEM((tmm, n), F32)],
        ("parallel", "arbitrary"), (a, b), payload)
    return out, p_outs


def _proj_in_fwd(x, g, wb, li, name, payload=None):
    t, d = x.shape
    n = wb.shape[1]
    tm = 512

    def body(x_ref, g_ref, w_ref, z_ref):
        xhat, _ = _rms(x_ref[...])
        z_ref[...] = _dot_nt((xhat * g_ref[...]).astype(BF16), w_ref[...])

    (z,), p_outs = _call(
        body, name, (t // tm,),
        [pl.BlockSpec((tm, d), lambda i: (i, 0)), pl.BlockSpec((1, d), lambda i: (0, 0)),
         pl.BlockSpec((None, n, d), lambda i: (li, 0, 0))],
        [pl.BlockSpec((tm, n), lambda i: (i, 0))],
        [jax.ShapeDtypeStruct((t, n), F32)], [], ("parallel",), (x, g, wb), payload)
    return z, p_outs


def _proj_out_fwd(x, cat, wc, li, name, payload=None):
    t, d = x.shape
    tm = 512

    def body(x_ref, cat_ref, w_ref, xo_ref):
        xo_ref[...] = x_ref[...] + _dot(cat_ref[...], w_ref[...])

    (xo,), p_outs = _call(
        body, name, (t // tm,),
        [pl.BlockSpec((tm, d), lambda i: (i, 0)), pl.BlockSpec((tm, d), lambda i: (i, 0)),
         pl.BlockSpec((None, d, d), lambda i: (li, 0, 0))],
        [pl.BlockSpec((tm, d), lambda i: (i, 0))],
        [jax.ShapeDtypeStruct((t, d), F32)], [], ("parallel",), (x, cat, wc), payload)
    return xo, p_outs


def _proj_out_bwd(dxo, wc, li, name):
    t, d = dxo.shape
    tm = 512

    def body(dxo_ref, w_ref, dcat_ref):
        dcat_ref[...] = _dot_nt(dxo_ref[...].astype(BF16), w_ref[...])

    return pl.pallas_call(
        body, name=name, grid=(t // tm,),
        in_specs=[pl.BlockSpec((tm, d), lambda i: (i, 0)), pl.BlockSpec((None, d, d), lambda i: (li, 0, 0))],
        out_specs=pl.BlockSpec((tm, d), lambda i: (i, 0)),
        out_shape=jax.ShapeDtypeStruct((t, d), F32),
        compiler_params=_cparams(("parallel",)),
    )(dxo, wc)


def _proj_in_bwd(x, g, dxo, dz, wb, li, name):
    t, d = x.shape
    n = wb.shape[1]
    tm = 512

    def body(x_ref, g_ref, dxo_ref, dz_ref, w_ref, dx_ref, h_ref, dg_ref):
        i = pl.program_id(0)

        @pl.when(i == 0)
        def _():
            dg_ref[...] = jnp.zeros_like(dg_ref)

        xhat, rstd = _rms(x_ref[...])
        h_ref[...] = (xhat * g_ref[...]).astype(BF16)
        dh = _dot(dz_ref[...], w_ref[...])
        dxn, dg = _rms_bwd(xhat, rstd, g_ref[...], dh)
        dx_ref[...] = dxo_ref[...] + dxn
        dg_ref[...] += dg

    row = pl.BlockSpec((tm, d), lambda i: (i, 0))
    vec = pl.BlockSpec((1, d), lambda i: (0, 0))
    return pl.pallas_call(
        body, name=name, grid=(t // tm,),
        in_specs=[row, vec, row, pl.BlockSpec((tm, n), lambda i: (i, 0)),
                  pl.BlockSpec((None, n, d), lambda i: (li, 0, 0))],
        out_specs=[row, row, vec],
        out_shape=[jax.ShapeDtypeStruct((t, d), F32), jax.ShapeDtypeStruct((t, d), BF16),
                   jax.ShapeDtypeStruct((1, d), F32)],
        compiler_params=_cparams(("arbitrary",)),
    )(x, g, dxo, dz, wb)


def _lane_ids(shape):
    return lax.broadcasted_iota(jnp.int32, shape, 1)


def _tril(w):
    r = lax.broadcasted_iota(jnp.int32, w.shape, 0)
    c = lax.broadcasted_iota(jnp.int32, w.shape, 1)
    return jnp.where(r >= c, w, 0.0)


def _shift_down(x, k):
    return x if k == 0 else pltpu.roll(x, k, 0)


def _shift_up(x, k):
    return x if k == 0 else pltpu.roll(x, x.shape[0] - k, 0)


def _causal_conv(ext, w, n_out):
    acc = None
    for b in range(8):
        rolled = _shift_down(ext, b)
        for a in range((CONV_WIDTH - 1 - b) // 8 + 1):
            j = 8 * a + b
            term = rolled[HALO - 8 * a:HALO - 8 * a + n_out] * w[CONV_WIDTH - 1 - j:CONV_WIDTH - j]
            acc = term if acc is None else acc + term
    return acc


def _conv_wgrad(ext, dhc, n_out):
    rows = [None] * CONV_WIDTH
    for b in range(8):
        rolled = _shift_down(ext, b)
        for a in range((CONV_WIDTH - 1 - b) // 8 + 1):
            j = 8 * a + b
            rows[CONV_WIDTH - 1 - j] = jnp.sum(
                rolled[HALO - 8 * a:HALO - 8 * a + n_out] * dhc, axis=0, keepdims=True)
    return rows


def _anticausal_conv(ext, w, n_out):
    acc = None
    for b in range(8):
        rolled = _shift_up(ext, b)
        for a in range((CONV_WIDTH - 1 - b) // 8 + 1):
            j = 8 * a + b
            term = rolled[8 * a:8 * a + n_out] * w[CONV_WIDTH - 1 - j:CONV_WIDTH - j]
            acc = term if acc is None else acc + term
    return acc


def _window_sums(ext, shift):
    s2 = ext + shift(ext, 1)
    s4 = s2 + shift(s2, 2)
    s8 = s4 + shift(s4, 4)
    s16 = s8 + shift(s8, 8)
    grp = _lane_ids(ext.shape) // HEAD_DIM
    return jnp.where(grp == 0, s2, jnp.where(grp == 1, s4, jnp.where(grp == 2, s8, s16)))


def _pool_count(t0, n, width):
    pos = (lax.broadcasted_iota(jnp.int32, (n, width), 0) + (t0 + 1)).astype(F32)
    grp = _lane_ids((n, width)) // HEAD_DIM
    win = jnp.where(grp == 0, 2.0, jnp.where(grp == 1, 4.0, jnp.where(grp == 2, 8.0, 16.0)))
    return jnp.minimum(pos, win)


def _block_diag(pw):
    gn, cg, _ = pw.shape
    rows = []
    for gi in range(gn):
        parts = [pw[gi] if gj == gi else jnp.zeros((cg, cg), pw.dtype) for gj in range(gn)]
        rows.append(jnp.concatenate(parts, axis=1))
    return jnp.concatenate(rows, axis=0)


def _head_pair_mix(w_even, w_odd, v):
    lo = _lane_ids((CHUNK, LANES)) < HEAD_DIM
    return jnp.where(lo, _dot(w_even, v), _dot(w_odd, v))


def _mix_fwd(z, p, name, payload=None):
    t, d_in = z.shape
    sgu = p["sgu_ln_g"].shape[1]
    pool = p["pool_scale"].shape[1]
    d_mix = 2 * sgu + pool
    tm = 512
    n_i = t // tm
    hb = tm // HALO

    def body(z_ref, zp_ref, lng_ref, lnb_ref, ws_ref, bs_ref, cw_ref, cb_ref, clg_ref, clb_ref,
             bd_ref, ps_ref, cat_ref):
        i = pl.program_id(0)
        first = i == 0
        z_main = z_ref[...]
        z_prev = jnp.where(first, 0.0, zp_ref[...])

        lng, lnb = lng_ref[...], lnb_ref[...]
        wt = [_tril(ws_ref[h]).astype(BF16) for h in range(sgu // HEAD_DIM)]
        for n in range(tm // CHUNK):
            rows = slice(n * CHUNK, (n + 1) * CHUNK)
            u = _gelu(z_main[rows, 0:sgu])
            vhat, _ = _ln(_gelu(z_main[rows, sgu:2 * sgu]))
            vn = (vhat * lng + lnb).astype(BF16)
            for gp in range(sgu // LANES):
                ls = slice(gp * LANES, (gp + 1) * LANES)
                mixed = _head_pair_mix(wt[2 * gp], wt[2 * gp + 1], vn[:, ls]) + bs_ref[:, ls]
                cat_ref[rows, ls] = (u[:, ls] * mixed).astype(BF16)

        def glu(zz):
            return zz[:, 2 * sgu:3 * sgu] * jax.nn.sigmoid(zz[:, 3 * sgu:4 * sgu])

        ext = jnp.concatenate([glu(z_prev), glu(z_main)], axis=0)
        hc = _causal_conv(ext, cw_ref[...], tm) + cb_ref[...]
        hhat, _ = _ln(hc)
        bn = hhat * clg_ref[...] + clb_ref[...]
        cat_ref[:, sgu:2 * sgu] = (bn * jax.nn.sigmoid(bn)).astype(BF16)

        pext = jnp.concatenate([z_prev[:, 4 * sgu:], z_main[:, 4 * sgu:]], axis=0)
        sums = _window_sums(pext, _shift_down)[HALO:]
        pooled = sums / _pool_count(i * tm, tm, pool) - z_main[:, 4 * sgu:]
        mixed_c = _dot(pooled.astype(BF16), bd_ref[...].astype(BF16))
        cat_ref[:, 2 * sgu:] = (mixed_c * ps_ref[...]).astype(BF16)

    def vec(n):
        return pl.BlockSpec((1, n), lambda i: (0, 0))

    (cat,), p_outs = _call(
        body, name, (n_i,),
        [pl.BlockSpec((tm, d_in), lambda i: (i, 0)),
         pl.BlockSpec((HALO, d_in), lambda i: (jnp.maximum(i * hb - 1, 0), 0)),
         vec(sgu), vec(sgu),
         pl.BlockSpec(p["w_spatial"].shape, lambda i: (0, 0, 0)),
         pl.BlockSpec((CHUNK, sgu), lambda i: (0, 0)),
         pl.BlockSpec((CONV_WIDTH, sgu), lambda i: (0, 0)),
         vec(sgu), vec(sgu), vec(sgu),
         pl.BlockSpec((pool, pool), lambda i: (0, 0)), vec(pool)],
        [pl.BlockSpec((tm, d_mix), lambda i: (i, 0))],
        [jax.ShapeDtypeStruct((t, d_mix), BF16)], [], ("parallel",),
        (z, z, p["sgu_ln_g"], p["sgu_ln_b"], p["w_spatial"], p["bs_full"], p["conv_w"], p["conv_b"],
         p["conv_ln_g"], p["conv_ln_b"], p["bd"], p["pool_scale"]), payload)
    return cat, p_outs


_R_SGU_G, _R_SGU_B, _R_CONV_B, _R_CLN_G, _R_CLN_B, _R_CONV_W = 0, 1, 2, 3, 4, 8
_R384_ROWS = 40


def _mix_bwd(z, dcat, p, name, payload=None):
    t, d_in = z.shape
    sgu = p["sgu_ln_g"].shape[1]
    pool = p["pool_scale"].shape[1]
    d_mix = 2 * sgu + pool
    n_head = sgu // HEAD_DIM
    tm = 512
    n_i = t // tm
    hb = tm // HALO

    def body(z_ref, zp_ref, zn_ref, dc_ref, dcn_ref, lng_ref, lnb_ref, ws_ref, bs_ref, cw_ref, cb_ref,
             clg_ref, clb_ref, bd_ref, ps_ref, dz_ref, g384_ref, gws_ref, gpool_ref, dbs_scr):
        i = pl.program_id(0)
        first, last = i == 0, i == n_i - 1

        @pl.when(first)
        def _():
            g384_ref[...] = jnp.zeros_like(g384_ref)
            gws_ref[...] = jnp.zeros_like(gws_ref)
            gpool_ref[...] = jnp.zeros_like(gpool_ref)
            dbs_scr[...] = jnp.zeros_like(dbs_scr)

        z_main = z_ref[...]
        z_prev = jnp.where(first, 0.0, zp_ref[...])
        z_next = jnp.where(last, 0.0, zn_ref[...])
        dc_main = dc_ref[...]
        dc_next = jnp.where(last, 0.0, dcn_ref[...])

        lng, lnb = lng_ref[...], lnb_ref[...]
        wt = [_tril(ws_ref[h]) for h in range(n_head)]
        wt_b = [w.astype(BF16) for w in wt]
        wtt_b = [w.T.astype(BF16) for w in wt]
        lo = _lane_ids((CHUNK, LANES)) < HEAD_DIM
        d_lng = jnp.zeros((1, sgu), F32)
        d_lnb = jnp.zeros((1, sgu), F32)
        dws = [jnp.zeros((CHUNK, CHUNK), F32) for _ in range(n_head)]
        for n in range(tm // CHUNK):
            rows = slice(n * CHUNK, (n + 1) * CHUNK)
            au, av = z_main[rows, 0:sgu], z_main[rows, sgu:2 * sgu]
            u = _gelu(au)
            vhat, vrstd = _ln(_gelu(av))
            vn = (vhat * lng + lnb).astype(BF16)
            da = dc_main[rows, 0:sgu]
            dmixed = da * u
            dbs_scr[...] += dmixed
            dvn_parts, du_parts = [], []
            for gp in range(sgu // LANES):
                ls = slice(gp * LANES, (gp + 1) * LANES)
                vn_g = vn[:, ls]
                mixed = _head_pair_mix(wt_b[2 * gp], wt_b[2 * gp + 1], vn_g) + bs_ref[:, ls]
                du_parts.append(da[:, ls] * mixed)
                dm_g = dmixed[:, ls]
                dm_b = dm_g.astype(BF16)
                dvn_parts.append(jnp.where(lo, _dot(wtt_b[2 * gp], dm_b), _dot(wtt_b[2 * gp + 1], dm_b)))
                dws[2 * gp] = dws[2 * gp] + _dot_nt(jnp.where(lo, dm_g, 0.0).astype(BF16), vn_g)
                dws[2 * gp + 1] = dws[2 * gp + 1] + _dot_nt(jnp.where(lo, 0.0, dm_g).astype(BF16), vn_g)
            dvn = jnp.concatenate(dvn_parts, axis=1)
            du = jnp.concatenate(du_parts, axis=1)
            dv, dg_n, db_n = _ln_bwd(vhat, vrstd, lng, dvn)
            d_lng = d_lng + dg_n
            d_lnb = d_lnb + db_n
            dz_ref[rows, 0:sgu] = (du * _gelu_grad(au)).astype(BF16)
            dz_ref[rows, sgu:2 * sgu] = (dv * _gelu_grad(av)).astype(BF16)
        for h in range(n_head):
            gws_ref[h] += _tril(dws[h])
        g384_ref[_R_SGU_G:_R_SGU_G + 1, :] += d_lng
        g384_ref[_R_SGU_B:_R_SGU_B + 1, :] += d_lnb

        @pl.when(last)
        def _():
            r = lax.broadcasted_iota(jnp.int32, (sgu, LANES), 0)
            c = lax.broadcasted_iota(jnp.int32, (sgu, LANES), 1)
            sel = (r // HEAD_DIM == c).astype(BF16)
            gws_ref[n_head] = _split_dot(dbs_scr[...], sel)

        cw, clg = cw_ref[...], clg_ref[...]
        bcols = slice(2 * sgu, 4 * sgu)
        zb = jnp.concatenate([z_prev[:, bcols], z_main[:, bcols], z_next[:, bcols]], axis=0)
        bval, bgate = zb[:, 0:sgu], zb[:, sgu:2 * sgu]
        sg = jax.nn.sigmoid(bgate)
        hglu = bval * sg
        n_out = tm + HALO
        hc = _causal_conv(hglu, cw, n_out) + cb_ref[...]
        hhat, hrstd = _ln(hc)
        bn = hhat * clg + clb_ref[...]
        db = jnp.concatenate([dc_main[:, sgu:2 * sgu], dc_next[:, sgu:2 * sgu]], axis=0)
        dbn = db * _silu_grad(bn)
        dhc_all, _, _ = _ln_bwd(hhat, hrstd, clg, dbn)
        dbn_m, hhat_m, dhc = dbn[:tm], hhat[:tm], dhc_all[:tm]
        g384_ref[_R_CLN_G:_R_CLN_G + 1, :] += jnp.sum(dbn_m * hhat_m, axis=0, keepdims=True)
        g384_ref[_R_CLN_B:_R_CLN_B + 1, :] += jnp.sum(dbn_m, axis=0, keepdims=True)
        g384_ref[_R_CONV_B:_R_CONV_B + 1, :] += jnp.sum(dhc, axis=0, keepdims=True)
        wrows = _conv_wgrad(hglu[:HALO + tm], dhc, tm)
        for k in range(CONV_WIDTH):
            g384_ref[_R_CONV_W + k:_R_CONV_W + k + 1, :] += wrows[k]
        dhglu = _anticausal_conv(dhc_all, cw, tm)
        bval_m, sg_m = bval[HALO:HALO + tm], sg[HALO:HALO + tm]
        dz_ref[:, 2 * sgu:3 * sgu] = (dhglu * sg_m).astype(BF16)
        dz_ref[:, 3 * sgu:4 * sgu] = (dhglu * bval_m * sg_m * (1.0 - sg_m)).astype(BF16)

        bd_b = bd_ref[...].astype(BF16)
        ps = ps_ref[...]
        p_main = z_main[:, 4 * sgu:]
        pext = jnp.concatenate([z_prev[:, 4 * sgu:], p_main], axis=0)
        cnt = _pool_count(i * tm, n_out, pool)
        pooled = _window_sums(pext, _shift_down)[HALO:] / cnt[:tm] - p_main
        pooled_b = pooled.astype(BF16)
        dcc = jnp.concatenate([dc_main[:, 2 * sgu:], dc_next[:, 2 * sgu:]], axis=0)
        dmix_c = dcc * ps
        mixed_c = _dot(pooled_b, bd_b)
        grp_r = lax.broadcasted_iota(jnp.int32, (pool, pool), 0) // HEAD_DIM
        grp_c = lax.broadcasted_iota(jnp.int32, (pool, pool), 1) // HEAD_DIM
        gpool_ref[0:pool, :] += jnp.where(grp_r == grp_c, _dot_tn(pooled_b, dmix_c[:tm].astype(BF16)), 0.0)
        gpool_ref[pool:pool + 1, :] += jnp.sum(dcc[:tm] * mixed_c, axis=0, keepdims=True)
        dpooled = _dot_nt(dmix_c.astype(BF16), bd_b)
        q = dpooled / cnt
        dp = _window_sums(q, _shift_up)[:tm] - dpooled[:tm]
        dz_ref[:, 4 * sgu:] = dp.astype(BF16)

    def vec(n):
        return pl.BlockSpec((1, n), lambda i: (0, 0))

    def prev_map(i):
        return (jnp.maximum(i * hb - 1, 0), 0)

    def next_map(i):
        return (jnp.minimum((i + 1) * hb, n_i * hb - 1), 0)

    return _call(
        body, name, (n_i,),
        [pl.BlockSpec((tm, d_in), lambda i: (i, 0)),
         pl.BlockSpec((HALO, d_in), prev_map), pl.BlockSpec((HALO, d_in), next_map),
         pl.BlockSpec((tm, d_mix), lambda i: (i, 0)), pl.BlockSpec((HALO, d_mix), next_map),
         vec(sgu), vec(sgu),
         pl.BlockSpec(p["w_spatial"].shape, lambda i: (0, 0, 0)),
         pl.BlockSpec((CHUNK, sgu), lambda i: (0, 0)),
         pl.BlockSpec((CONV_WIDTH, sgu), lambda i: (0, 0)),
         vec(sgu), vec(sgu), vec(sgu),
         pl.BlockSpec((pool, pool), lambda i: (0, 0)), vec(pool)],
        [pl.BlockSpec((tm, d_in), lambda i: (i, 0)),
         pl.BlockSpec((_R384_ROWS, sgu), lambda i: (0, 0)),
         pl.BlockSpec((n_head + 1, CHUNK, CHUNK), lambda i: (0, 0, 0)),
         pl.BlockSpec((pool + 8, pool), lambda i: (0, 0))],
        [jax.ShapeDtypeStruct((t, d_in), BF16),
         jax.ShapeDtypeStruct((_R384_ROWS, sgu), F32),
         jax.ShapeDtypeStruct((n_head + 1, CHUNK, CHUNK), F32),
         jax.ShapeDtypeStruct((pool + 8, pool), F32)],
        [pltpu.VMEM((CHUNK, sgu), F32)], ("arbitrary",),
        (z, z, z, dcat, dcat, p["sgu_ln_g"], p["sgu_ln_b"], p["w_spatial"], p["bs_full"], p["conv_w"],
         p["conv_b"], p["conv_ln_g"], p["conv_ln_b"], p["bd"], p["pool_scale"]), payload)


def _loss_head(x, g, target, name):
    t, d = x.shape
    tm = 512

    def body(x_ref, g_ref, tgt_ref, dx_ref, dg_ref, loss_ref):
        i = pl.program_id(0)

        @pl.when(i == 0)
        def _():
            dg_ref[...] = jnp.zeros_like(dg_ref)
            loss_ref[...] = jnp.zeros_like(loss_ref)

        gv = g_ref[...]
        xhat, rstd = _rms(x_ref[...])
        err = xhat * gv - tgt_ref[...]
        loss_ref[...] += jnp.zeros_like(loss_ref) + 0.5 * jnp.sum(jnp.mean(err * err, axis=-1, keepdims=True))
        dxn, dg = _rms_bwd(xhat, rstd, gv, err * (1.0 / d))
        dx_ref[...] = dxn
        dg_ref[...] += dg

    row = pl.BlockSpec((tm, d), lambda i: (i, 0))
    vec = pl.BlockSpec((1, d), lambda i: (0, 0))
    return pl.pallas_call(
        body, name=name, grid=(t // tm,),
        in_specs=[row, vec, row],
        out_specs=[row, vec, pl.BlockSpec((1, LANES), lambda i: (0, 0))],
        out_shape=[jax.ShapeDtypeStruct((t, d), F32), jax.ShapeDtypeStruct((1, d), F32),
                   jax.ShapeDtypeStruct((1, LANES), F32)],
        compiler_params=_cparams(("arbitrary",)),
    )(x, g, target)


def _all_gather(arrs, name, extra=None):
    gather = _GatherIci(arrs)
    n = len(arrs)
    forward = _GatherForward([jax.ShapeDtypeStruct(s.shape, s.dtype) for s in gather.out_shapes])
    x_in = len(extra.ins) if extra else 0
    x_out = len(extra.out_shapes) if extra else 0

    def body(*refs):
        ins, x_ins = refs[:n], refs[n:n + x_in]
        outs, x_outs = refs[n + x_in:2 * n + x_in], refs[2 * n + x_in:2 * n + x_in + x_out]
        sems = refs[2 * n + x_in + x_out:]
        first = gather.build(ins, outs, *sems[0:3])
        first.start()
        if extra:
            beside = extra.build(x_ins, x_outs, *sems[6:9])
            beside.start()
        first.wait()
        second = forward.build(outs, outs, *sems[3:6])
        second.start()
        second.wait()
        if extra:
            beside.wait()

    outs = pl.pallas_call(
        body, name=name,
        in_specs=[ANY] * (n + x_in), out_specs=[ANY] * (n + x_out),
        out_shape=list(gather.out_shapes) + (list(extra.out_shapes) if extra else []),
        scratch_shapes=gather.sem_shapes() + forward.sem_shapes() + (extra.sem_shapes() if extra else []),
    )(*arrs, *(extra.ins if extra else []))
    return list(outs[:n]), list(outs[n:])


def _pair_sum(grad, recv, cidx, name):
    r = grad.shape[0] // N_DEV
    cols = grad.shape[1]

    def body(c_ref, g_ref, r_ref, o_ref):
        o_ref[...] = (g_ref[...].astype(F32) + r_ref[...].astype(F32)).astype(BF16)

    return pl.pallas_call(
        body, name=name,
        grid_spec=pltpu.PrefetchScalarGridSpec(
            num_scalar_prefetch=1, grid=(N_CHIP,),
            in_specs=[pl.BlockSpec((r, cols), lambda q, c: (2 * q + c[0], 0)),
                      pl.BlockSpec((r, cols), lambda q, c: (q, 0))],
            out_specs=pl.BlockSpec((r, cols), lambda q, c: (q, 0))),
        out_shape=jax.ShapeDtypeStruct((N_CHIP * r, cols), BF16),
        compiler_params=_cparams(("parallel",)),
    )(cidx, grad, recv)


def _sum_blocks(parts, nblk, name):
    r = parts.shape[0] // nblk
    cols = parts.shape[1]

    def body(p_ref, o_ref):
        acc = p_ref[0:r, :].astype(F32)
        for q in range(1, nblk):
            acc = acc + p_ref[q * r:(q + 1) * r, :].astype(F32)
        o_ref[...] = acc

    return pl.pallas_call(
        body, name=name,
        out_shape=jax.ShapeDtypeStruct((r, cols), F32),
        compiler_params=_cparams(),
    )(parts)


def _adamw_math(w, g, m, v):
    m = ADAM_B1 * m + (1.0 - ADAM_B1) * g
    v = ADAM_B2 * v + (1.0 - ADAM_B2) * (g * g)
    m_hat = m / (1.0 - ADAM_B1 ** ADAM_STEP)
    v_hat = v / (1.0 - ADAM_B2 ** ADAM_STEP)
    delta = -ADAM_LR * (m_hat / (jnp.sqrt(v_hat) + ADAM_EPS) + ADAM_WD * w)
    return delta, m, v


def _adamw_big(w, g, m, v, name):
    rows, cols = w.shape
    tr = max(k for k in range(8, 513, 8) if rows % k == 0)

    def body(w_ref, g_ref, m_ref, v_ref, d_ref, mo_ref, vo_ref):
        d_ref[...], mo_ref[...], vo_ref[...] = _adamw_math(w_ref[...], g_ref[...], m_ref[...], v_ref[...])

    spec = pl.BlockSpec((tr, cols), lambda i: (i, 0))
    return pl.pallas_call(
        body, name=name, grid=(rows // tr,),
        in_specs=[spec] * 4, out_specs=[spec] * 3,
        out_shape=[jax.ShapeDtypeStruct(w.shape, F32)] * 3,
        compiler_params=_cparams(("parallel",)),
    )(w, g, m, v)


def _adamw_small(ws, gs, ms, vs, name):
    n = len(ws)

    def body(*refs):
        for k in range(n):
            w_ref, g_ref, m_ref, v_ref = (refs[j * n + k] for j in range(4))
            d_ref, mo_ref, vo_ref = (refs[(4 + j) * n + k] for j in range(3))
            d_ref[...], mo_ref[...], vo_ref[...] = _adamw_math(w_ref[...], g_ref[...], m_ref[...], v_ref[...])

    shapes = [jax.ShapeDtypeStruct(w.shape, F32) for w in ws]
    return pl.pallas_call(
        body, name=name, out_shape=shapes * 3, compiler_params=_cparams(),
    )(*ws, *gs, *ms, *vs)


def kernel(x, ffn1_norm, ffn1_w_gate, ffn1_w_up, ffn1_w_down, mix_norm, w_in, sgu_ln_g, sgu_ln_b, w_spatial, b_spatial, conv_w, conv_b, conv_ln_g, conv_ln_b, pool_w, pool_scale, w_out, ffn2_norm, ffn2_w_gate, ffn2_w_up, ffn2_w_down, final_norm, loss_target, m_ffn1_norm, m_ffn1_w_gate, m_ffn1_w_up, m_ffn1_w_down, m_mix_norm, m_w_in, m_sgu_ln_g, m_sgu_ln_b, m_w_spatial, m_b_spatial, m_conv_w, m_conv_b, m_conv_ln_g, m_conv_ln_b, m_pool_w, m_pool_scale, m_w_out, m_ffn2_norm, m_ffn2_w_gate, m_ffn2_w_up, m_ffn2_w_down, m_final_norm, v_ffn1_norm, v_ffn1_w_gate, v_ffn1_w_up, v_ffn1_w_down, v_mix_norm, v_w_in, v_sgu_ln_g, v_sgu_ln_b, v_w_spatial, v_b_spatial, v_conv_w, v_conv_b, v_conv_ln_g, v_conv_ln_b, v_pool_w, v_pool_scale, v_w_out, v_ffn2_norm, v_ffn2_w_gate, v_ffn2_w_up, v_ffn2_w_down, v_final_norm):
    names = ["ffn1_norm", "ffn1_w_gate", "ffn1_w_up", "ffn1_w_down", "mix_norm", "w_in", "sgu_ln_g", "sgu_ln_b",
             "w_spatial", "b_spatial", "conv_w", "conv_b", "conv_ln_g", "conv_ln_b", "pool_w", "pool_scale",
             "w_out", "ffn2_norm", "ffn2_w_gate", "ffn2_w_up", "ffn2_w_down", "final_norm"]
    W = dict(zip(names, [ffn1_norm, ffn1_w_gate, ffn1_w_up, ffn1_w_down, mix_norm, w_in, sgu_ln_g, sgu_ln_b,
                         w_spatial, b_spatial, conv_w, conv_b, conv_ln_g, conv_ln_b, pool_w, pool_scale, w_out,
                         ffn2_norm, ffn2_w_gate, ffn2_w_up, ffn2_w_down, final_norm]))
    M = dict(zip(names, [m_ffn1_norm, m_ffn1_w_gate, m_ffn1_w_up, m_ffn1_w_down, m_mix_norm, m_w_in, m_sgu_ln_g,
                         m_sgu_ln_b, m_w_spatial, m_b_spatial, m_conv_w, m_conv_b, m_conv_ln_g, m_conv_ln_b,
                         m_pool_w, m_pool_scale, m_w_out, m_ffn2_norm, m_ffn2_w_gate, m_ffn2_w_up, m_ffn2_w_down,
                         m_final_norm]))
    V = dict(zip(names, [v_ffn1_norm, v_ffn1_w_gate, v_ffn1_w_up, v_ffn1_w_down, v_mix_norm, v_w_in, v_sgu_ln_g,
                         v_sgu_ln_b, v_w_spatial, v_b_spatial, v_conv_w, v_conv_b, v_conv_ln_g, v_conv_ln_b,
                         v_pool_w, v_pool_scale, v_w_out, v_ffn2_norm, v_ffn2_w_gate, v_ffn2_w_up, v_ffn2_w_down,
                         v_final_norm]))

    depth, d = ffn1_norm.shape
    t = x.shape[1]
    sgu = sgu_ln_g.shape[1]
    pool = pool_scale.shape[1]
    n_head = sgu // HEAD_DIM
    cw_shard = conv_w.shape[2]
    xs = x.reshape(t, d)
    target = loss_target.reshape(t, d)

    def tr(w):
        return jnp.swapaxes(w, 1, 2).astype(BF16)

    ffn_shards = [[jnp.stack([tr(ffn1_w_gate)[l], tr(ffn1_w_up)[l], ffn1_w_down[l].astype(BF16)]),
                   jnp.stack([tr(ffn2_w_gate)[l], tr(ffn2_w_up)[l], ffn2_w_down[l].astype(BF16)])]
                  for l in range(depth)]
    win_shards = [tr(w_in)[l:l + 1] for l in range(depth)]
    wout_shards = [w_out[l:l + 1].astype(BF16) for l in range(depth)]
    cw_rows = depth * CONV_WIDTH
    cw_pad = -cw_rows % 8
    cw_send = jnp.pad(conv_w.reshape(cw_rows, cw_shard), ((0, cw_pad), (0, 0)))[None]
    wffn, wb, wc = {}, {}, {}
    (wffn[(0, 0)], wb[0], wc[0], cwg), _ = _all_gather(
        [ffn_shards[0][0], win_shards[0], wout_shards[0], cw_send], "ag_first")
    conv_w_full = cwg.reshape(N_DEV, cw_rows + cw_pad, cw_shard)[:, :cw_rows].reshape(
        N_DEV, depth, CONV_WIDTH, cw_shard).transpose(1, 2, 0, 3).reshape(depth, CONV_WIDTH, N_DEV * cw_shard)

    def mixer_params(l):
        return dict(
            sgu_ln_g=sgu_ln_g[l:l + 1], sgu_ln_b=sgu_ln_b[l:l + 1], w_spatial=w_spatial[l],
            bs_full=jnp.repeat(b_spatial[l].T, HEAD_DIM, axis=1),
            conv_w=conv_w_full[l], conv_b=conv_b[l:l + 1], conv_ln_g=conv_ln_g[l:l + 1],
            conv_ln_b=conv_ln_b[l:l + 1], bd=_block_diag(pool_w[l]), pool_scale=pool_scale[l:l + 1])

    saved = []
    cur = xs
    for l in range(depth):
        p = mixer_params(l)
        x0 = cur
        more = l + 1 < depth
        (x1, gate1, up1), part = _ffn_fwd(x0, ffn1_norm[l:l + 1], wffn[(l, 0)], 0, f"ffn1_fwd_{l}",
                                          _GatherIci([ffn_shards[l][1]]))
        z, (wffn[(l, 1)],) = _proj_in_fwd(x1, mix_norm[l:l + 1], wb[l], 0, f"proj_in_fwd_{l}", _GatherForward(part))
        cat, part = _mix_fwd(z, p, f"mix_fwd_{l}",
                             _GatherIci([win_shards[l + 1], wout_shards[l + 1]]) if more else None)
        x2, part = _proj_out_fwd(x1, cat, wc[l], 0, f"proj_out_fwd_{l}", _GatherForward(part) if more else None)
        if more:
            wb[l + 1], wc[l + 1] = part
        (x3, gate2, up2), part = _ffn_fwd(x2, ffn2_norm[l:l + 1], wffn[(l, 1)], 0, f"ffn2_fwd_{l}",
                                          _GatherIci([ffn_shards[l + 1][0]]) if more else None)
        if more:
            (wffn[(l + 1, 0)],) = _comm(_GatherForward(part), f"ag_forward_{l + 1}")
        saved.append((p, x0, gate1, up1, x1, z, cat, x2, gate2, up2))
        cur = x3

    dx, d_final, loss_part = _loss_head(cur, final_norm.reshape(1, d), target, "loss_head")

    cidx = lax.axis_index("c").astype(jnp.int32).reshape(1)
    from_chips = {}
    to_pair, to_chip = [], []
    small = []

    def pair_payload():
        return _PairExchange([g for _, g in to_pair]) if to_pair else None

    def pair_done(received):
        for ((nm, l), g), r in zip(to_pair, received):
            to_chip.append(((nm, l), _pair_sum(g, r, cidx, f"rs_pair_sum_{nm}_{l}")))
        to_pair.clear()

    def take_chip():
        items = list(to_chip)
        to_chip.clear()
        return items

    def chip_payload(items):
        return _ChipExchange([s for _, s in items]) if items else None

    def chip_done(items, landed):
        for (key, _), o in zip(items, landed):
            from_chips[key] = o

    def ffn_weight_grads(prefix, l, dgate, dup, act, h, dy):
        items = take_chip()
        for k, (nm, a, b) in enumerate([("w_gate", dgate, h), ("w_up", dup, h), ("w_down", act, dy)]):
            mine = items[k::3]
            g, landed = _tn_matmul(a, b, f"dw_{prefix}_{nm}_{l}", chip_payload(mine))
            chip_done(mine, landed)
            to_pair.append(((f"{prefix}_{nm}", l), g))

    for l in reversed(range(depth)):
        p, x0, gate1, up1, x1, z, cat, x2, gate2, up2 = saved[l]
        (dx, dgate, dup, act, h, dy, dg_ffn2), received = _ffn_bwd(
            x2, ffn2_norm[l:l + 1], dx, gate2, up2, wffn[(l, 1)], 0, f"ffn2_bwd_{l}", pair_payload())
        pair_done(received)
        ffn_weight_grads("ffn2", l, dgate, dup, act, h, dy)
        g_out, received = _tn_matmul(cat, dx, f"dw_w_out_{l}", pair_payload())
        pair_done(received)
        dcat = _proj_out_bwd(dx, wc[l], 0, f"proj_out_bwd_{l}")
        items = take_chip()
        (dz, g384, gws, gpool), landed = _mix_bwd(z, dcat, p, f"mix_bwd_{l}", chip_payload(items))
        chip_done(items, landed)
        dx, hm, dg_mix = _proj_in_bwd(x1, mix_norm[l:l + 1], dx, dz, wb[l], 0, f"proj_in_bwd_{l}")
        g_in, _ = _tn_matmul(dz, hm, f"dw_w_in_{l}")
        to_pair.extend([(("w_out", l), g_out), (("w_in", l), g_in)])
        (dx, dgate, dup, act, h, dy, dg_ffn1), received = _ffn_bwd(
            x0, ffn1_norm[l:l + 1], dx, gate1, up1, wffn[(l, 0)], 0, f"ffn1_bwd_{l}", pair_payload())
        pair_done(received)
        ffn_weight_grads("ffn1", l, dgate, dup, act, h, dy)
        small.append((l, g384, gws, gpool, dg_ffn1, dg_mix, dg_ffn2))
    grad_x = dx.reshape(x.shape)
    pair_done(_comm(pair_payload(), "rs_pair_exchange_last"))
    last_items = take_chip()

    small.sort(key=lambda s: s[0])
    norm_rows = []
    for (_, _, _, _, dg1, dgm, dg2) in small:
        norm_rows += [dg1, dgm, dg2]
    norm_rows += [d_final, jnp.pad(loss_part, ((0, 0), (0, d - LANES)))]
    n_norm = len(norm_rows)
    norm_pack = jnp.concatenate(norm_rows + [jnp.zeros((8 - n_norm % 8, d), F32)] * (n_norm % 8 != 0), axis=0)
    parts = [norm_pack]
    for (_, g384, gws, gpool, _, _, _) in small:
        parts += [g384, gws.reshape((n_head + 1) * CHUNK, CHUNK), gpool]
    gathered, landed = _all_gather([a[None] for a in parts], "ag_small_grads", chip_payload(last_items))
    chip_done(last_items, landed)

    big_names = ["ffn1_w_gate", "ffn1_w_up", "ffn1_w_down", "w_in", "w_out", "ffn2_w_gate", "ffn2_w_up",
                 "ffn2_w_down"]
    transposed = {"ffn1_w_gate", "ffn1_w_up", "w_in", "ffn2_w_gate", "ffn2_w_up"}
    grads = {}
    for nm in big_names:
        per_layer = []
        for l in range(depth):
            g = _sum_blocks(from_chips[(nm, l)], N_CHIP, f"rs_sum_{nm}_{l}")
            per_layer.append(g.T if nm in transposed else g)
        grads[nm] = jnp.stack(per_layer, axis=0)

    summed = [_sum_blocks(g[0], N_DEV, f"sum_small_{k}") for k, g in enumerate(gathered)]
    norm_sum = summed[0]
    loss = norm_sum[3 * depth + 1, 0]
    cpos = lax.axis_index("x") * 4 + lax.axis_index("y") * 2 + lax.axis_index("c")
    sg = {nm: [] for nm in names}
    for l in range(depth):
        g384, gws, gpool = summed[1 + 3 * l], summed[2 + 3 * l].reshape(n_head + 1, CHUNK, CHUNK), summed[3 + 3 * l]
        sg["ffn1_norm"].append(norm_sum[3 * l])
        sg["mix_norm"].append(norm_sum[3 * l + 1])
        sg["ffn2_norm"].append(norm_sum[3 * l + 2])
        sg["sgu_ln_g"].append(g384[_R_SGU_G])
        sg["sgu_ln_b"].append(g384[_R_SGU_B])
        sg["conv_b"].append(g384[_R_CONV_B])
        sg["conv_ln_g"].append(g384[_R_CLN_G])
        sg["conv_ln_b"].append(g384[_R_CLN_B])
        sg["conv_w"].append(lax.dynamic_slice_in_dim(g384[_R_CONV_W:_R_CONV_W + CONV_WIDTH], cpos * cw_shard,
                                                     cw_shard, axis=1))
        sg["w_spatial"].append(gws[:n_head])
        sg["b_spatial"].append(gws[n_head][:, :n_head].T)
        sg["pool_w"].append(jnp.stack([gpool[k * HEAD_DIM:(k + 1) * HEAD_DIM, k * HEAD_DIM:(k + 1) * HEAD_DIM]
                                       for k in range(pool // HEAD_DIM)], axis=0))
        sg["pool_scale"].append(gpool[pool])
    small_names = ["ffn1_norm", "mix_norm", "sgu_ln_g", "sgu_ln_b", "w_spatial", "b_spatial", "conv_w", "conv_b",
                   "conv_ln_g", "conv_ln_b", "pool_w", "pool_scale", "ffn2_norm"]
    for nm in small_names:
        grads[nm] = jnp.stack(sg[nm], axis=0)
    grads["final_norm"] = norm_sum[3 * depth]

    delta, new_m, new_v = {}, {}, {}
    for nm in big_names:
        shp = W[nm].shape
        two_d = (shp[0] * shp[1], shp[2])
        dl, mo, vo = _adamw_big(W[nm].reshape(two_d), grads[nm].reshape(two_d), M[nm].reshape(two_d),
                                V[nm].reshape(two_d), f"adamw_{nm}")
        delta[nm], new_m[nm], new_v[nm] = dl.reshape(shp), mo.reshape(shp), vo.reshape(shp)
    snames = small_names + ["final_norm"]

    def flat2(a):
        return a.reshape(-1, a.shape[-1])

    outs = _adamw_small([flat2(W[nm]) for nm in snames], [flat2(grads[nm]) for nm in snames],
                        [flat2(M[nm]) for nm in snames], [flat2(V[nm]) for nm in snames], "adamw_small")
    ns = len(snames)
    for k, nm in enumerate(snames):
        shp = W[nm].shape
        delta[nm], new_m[nm], new_v[nm] = (outs[k].reshape(shp), outs[ns + k].reshape(shp),
                                           outs[2 * ns + k].reshape(shp))

    return (loss, grad_x, *[grads[nm] for nm in names], *[delta[nm] for nm in names],
            *[new_m[nm] for nm in names], *[new_v[nm] for nm in names])
```

```python
import functools

import jax
import jax.numpy as jnp
from jax import lax
from jax.experimental import pallas as pl
from jax.experimental.pallas import tpu as pltpu

F32 = jnp.float32
BF16 = jnp.bfloat16
EPS = 1e-6
N_DEV = 8
N_CHIP = 4
MESH = pl.DeviceIdType.MESH
ANY = pl.BlockSpec(memory_space=pl.ANY)

VMEM_LIMIT_BYTES = 56 * 1024 * 1024
LANES = 128
HALO = 32
HEAD_DIM = 64
CHUNK = 128
CONV_WIDTH = 31
POOL_WINDOWS = (2, 4, 8, 16)

ADAM_LR = 0.001
ADAM_B1 = 0.9
ADAM_B2 = 0.999
ADAM_EPS = 1e-08
ADAM_WD = 0.01
ADAM_STEP = 10


def _cparams(sem=None):
    return pltpu.CompilerParams(dimension_semantics=sem, vmem_limit_bytes=VMEM_LIMIT_BYTES)


def _position():
    return lax.axis_index("x"), lax.axis_index("y"), lax.axis_index("c")


class _Copies:
    def __init__(self):
        self.local, self.sends, self.recvs = [], [], []

    def extend(self, other):
        self.local += other.local
        self.sends += other.sends
        self.recvs += other.recvs

    def start(self):
        for cp in self.local + self.sends:
            cp.start()

    def wait(self):
        for land, send_sems, recv_sems, k, peer in self.recvs:
            _remote(land, land, send_sems, recv_sems, k, peer).wait_recv()
        for cp in self.sends:
            cp.wait_send()
        for cp in self.local:
            cp.wait()


def _remote(src, dst, send_sems, recv_sems, k, to):
    return pltpu.make_async_remote_copy(src_ref=src, dst_ref=dst, send_sem=send_sems.at[k], recv_sem=recv_sems.at[k],
                                        device_id=to, device_id_type=MESH)


class _Payload:
    ins, out_shapes, aliases, n_remote, n_local = (), (), {}, 0, 0

    def sem_shapes(self):
        return [pltpu.SemaphoreType.DMA((max(self.n_remote, 1),)), pltpu.SemaphoreType.DMA((max(self.n_remote, 1),)),
                pltpu.SemaphoreType.DMA((max(self.n_local, 1),))]


class _GatherIci(_Payload):
    def __init__(self, shards):
        self.ins = list(shards)
        self.out_shapes = [jax.ShapeDtypeStruct((s.shape[0], N_DEV * s.shape[1], s.shape[2]), s.dtype) for s in shards]
        self.n_remote, self.n_local = 4 * len(shards), len(shards)

    def build(self, ins, outs, send_sems, recv_sems, local_sems, k0=0, l0=0):
        x, y, c = _position()
        peers = [(x, y, 1 - c), (1 - x, y, c), (x, 1 - y, c), (1 - x, 1 - y, c)]
        cps = _Copies()
        for a, (src, out) in enumerate(zip(ins, outs)):
            r = src.shape[1]

            def rows(px, py, pc, out=out, r=r):
                return out.at[:, pl.ds((4 * px + 2 * py + pc) * r, r), :]

            cps.local.append(pltpu.make_async_copy(src, rows(x, y, c), local_sems.at[l0 + a]))
            for k, peer in enumerate(peers):
                cps.sends.append(_remote(src, rows(x, y, c), send_sems, recv_sems, k0 + 4 * a + k, peer))
                cps.recvs.append((rows(*peer), send_sems, recv_sems, k0 + 4 * a + k, peer))
        return cps


class _GatherForward(_Payload):
    def __init__(self, partials):
        self.ins = list(partials)
        self.out_shapes = [jax.ShapeDtypeStruct(p.shape, p.dtype) for p in partials]
        self.aliases = {a: a for a in range(len(partials))}
        self.n_remote = 3 * len(partials)

    def build(self, ins, outs, send_sems, recv_sems, local_sems, k0=0, l0=0):
        x, y, c = _position()
        chips = [(1 - x, y), (x, 1 - y), (1 - x, 1 - y)]
        cps = _Copies()
        for a, out in enumerate(outs):
            r = out.shape[1] // N_DEV
            for k, (px, py) in enumerate(chips):
                mine = out.at[:, pl.ds((4 * px + 2 * py + c) * r, r), :]
                theirs = out.at[:, pl.ds((4 * px + 2 * py + 1 - c) * r, r), :]
                cps.sends.append(_remote(mine, mine, send_sems, recv_sems, k0 + 3 * a + k, (x, y, 1 - c)))
                cps.recvs.append((theirs, send_sems, recv_sems, k0 + 3 * a + k, (x, y, 1 - c)))
        return cps


class _PairExchange(_Payload):
    def __init__(self, grads):
        self.ins = list(grads)
        self.out_shapes = [jax.ShapeDtypeStruct((g.shape[0] // 2, g.shape[1]), g.dtype) for g in grads]
        self.n_remote = N_CHIP * len(grads)

    def build(self, ins, outs, send_sems, recv_sems, local_sems, k0=0, l0=0):
        x, y, c = _position()
        cps = _Copies()
        for a, (src, out) in enumerate(zip(ins, outs)):
            r = src.shape[0] // N_DEV
            for q in range(N_CHIP):
                land = out.at[pl.ds(q * r, r), :]
                cps.sends.append(_remote(src.at[pl.ds((2 * q + 1 - c) * r, r), :], land, send_sems, recv_sems,
                                         k0 + N_CHIP * a + q, (x, y, 1 - c)))
                cps.recvs.append((land, send_sems, recv_sems, k0 + N_CHIP * a + q, (x, y, 1 - c)))
        return cps


class _ChipExchange(_Payload):
    def __init__(self, sums):
        self.ins = list(sums)
        self.out_shapes = [jax.ShapeDtypeStruct(s.shape, s.dtype) for s in sums]
        self.n_remote, self.n_local = 3 * len(sums), len(sums)

    def build(self, ins, outs, send_sems, recv_sems, local_sems, k0=0, l0=0):
        x, y, c = _position()
        my_chip = 2 * x + y
        chips = [(1 - x, y), (x, 1 - y), (1 - x, 1 - y)]
        cps = _Copies()
        for a, (src, out) in enumerate(zip(ins, outs)):
            r = src.shape[0] // N_CHIP
            mine = out.at[pl.ds(my_chip * r, r), :]
            cps.local.append(pltpu.make_async_copy(src.at[pl.ds(my_chip * r, r), :], mine, local_sems.at[l0 + a]))
            for k, (px, py) in enumerate(chips):
                land = out.at[pl.ds((2 * px + py) * r, r), :]
                cps.sends.append(_remote(src.at[pl.ds((2 * px + py) * r, r), :], mine, send_sems, recv_sems,
                                         k0 + 3 * a + k, (px, py, c)))
                cps.recvs.append((land, send_sems, recv_sems, k0 + 3 * a + k, (px, py, c)))
        return cps


class _Merged(_Payload):
    def __init__(self, parts):
        self.parts = list(parts)
        self.ins = [a for p in parts for a in p.ins]
        self.out_shapes = [s for p in parts for s in p.out_shapes]
        self.aliases, self.offsets = {}, []
        i0 = o0 = k0 = l0 = 0
        for p in parts:
            self.offsets.append((i0, o0, k0, l0))
            self.aliases.update({i0 + i: o0 + o for i, o in p.aliases.items()})
            i0, o0, k0, l0 = i0 + len(p.ins), o0 + len(p.out_shapes), k0 + p.n_remote, l0 + p.n_local
        self.n_remote, self.n_local = k0, l0

    def build(self, ins, outs, send_sems, recv_sems, local_sems):
        cps = _Copies()
        for p, (i0, o0, k0, l0) in zip(self.parts, self.offsets):
            cps.extend(p.build(ins[i0:i0 + len(p.ins)], outs[o0:o0 + len(p.out_shapes)], send_sems, recv_sems,
                               local_sems, k0, l0))
        return cps


def _call(body, name, grid, in_specs, out_specs, out_shape, scratch_shapes, semantics, args, payload=None):
    if payload is None:
        outs = pl.pallas_call(body, name=name, grid=grid, in_specs=in_specs, out_specs=out_specs,
                              out_shape=out_shape, scratch_shapes=scratch_shapes,
                              compiler_params=_cparams(semantics))(*args)
        return list(outs), []
    n_in, n_out, n_scr = len(in_specs), len(out_specs), len(scratch_shapes)
    p_in, p_out = len(payload.ins), len(payload.out_shapes)

    def carried(*refs):
        ins, p_ins = refs[:n_in], refs[n_in:n_in + p_in]
        o0 = n_in + p_in
        outs, p_outs = refs[o0:o0 + n_out], refs[o0 + n_out:o0 + n_out + p_out]
        s0 = o0 + n_out + p_out
        scr, sems = refs[s0:s0 + n_scr], refs[s0 + n_scr:]
        ids = [pl.program_id(k) for k in range(len(grid))]
        at_first = functools.reduce(jnp.logical_and, [i == 0 for i in ids])
        at_last = functools.reduce(jnp.logical_and, [i == g - 1 for i, g in zip(ids, grid)])

        @pl.when(at_first)
        def _():
            payload.build(p_ins, p_outs, *sems).start()

        body(*ins, *outs, *scr)

        @pl.when(at_last)
        def _():
            payload.build(p_ins, p_outs, *sems).wait()

    outs = pl.pallas_call(
        carried, name=name, grid=grid, in_specs=list(in_specs) + [ANY] * p_in,
        out_specs=list(out_specs) + [ANY] * p_out, out_shape=list(out_shape) + list(payload.out_shapes),
        scratch_shapes=list(scratch_shapes) + payload.sem_shapes(),
        input_output_aliases={n_in + i: n_out + o for i, o in payload.aliases.items()},
        compiler_params=_cparams(("arbitrary",) * len(grid)))(*args, *payload.ins)
    return list(outs[:n_out]), list(outs[n_out:])


def _comm(payload, name):
    def body(*refs):
        p_in, p_out = len(payload.ins), len(payload.out_shapes)
        cps = payload.build(refs[:p_in], refs[p_in:p_in + p_out], *refs[p_in + p_out:])
        cps.start()
        cps.wait()

    return list(pl.pallas_call(
        body, name=name, in_specs=[ANY] * len(payload.ins), out_specs=[ANY] * len(payload.out_shapes),
        out_shape=list(payload.out_shapes), scratch_shapes=payload.sem_shapes(),
        input_output_aliases=dict(payload.aliases))(*payload.ins))


def _dot(a, b):
    return jnp.dot(a, b, preferred_element_type=F32)


def _dot_nt(a, b):
    return lax.dot_general(a, b, (((1,), (1,)), ((), ())), preferred_element_type=F32)


def _dot_tn(a, b):
    return lax.dot_general(a, b, (((0,), (0,)), ((), ())), preferred_element_type=F32)


def _split_dot(x, e):
    hi = x.astype(BF16)
    r1 = x - hi.astype(F32)
    mid = r1.astype(BF16)
    lo = (r1 - mid.astype(F32)).astype(BF16)
    return _dot(hi, e) + _dot(mid, e) + _dot(lo, e)


def _rms(x):
    rstd = lax.rsqrt(jnp.mean(x * x, axis=-1, keepdims=True) + EPS)
    return x * rstd, rstd


def _rms_bwd(xhat, rstd, g, dh):
    dxhat = dh * g
    dx = rstd * (dxhat - xhat * jnp.mean(dxhat * xhat, axis=-1, keepdims=True))
    return dx, jnp.sum(dh * xhat, axis=0, keepdims=True)


def _ln(v):
    mu = jnp.mean(v, axis=-1, keepdims=True)
    xc = v - mu
    rstd = lax.rsqrt(jnp.mean(xc * xc, axis=-1, keepdims=True) + EPS)
    return xc * rstd, rstd


def _ln_bwd(vhat, rstd, g, dy):
    dvhat = dy * g
    dv = rstd * (dvhat - jnp.mean(dvhat, axis=-1, keepdims=True)
                 - vhat * jnp.mean(dvhat * vhat, axis=-1, keepdims=True))
    return dv, jnp.sum(dy * vhat, axis=0, keepdims=True), jnp.sum(dy, axis=0, keepdims=True)


_INV_SQRT2 = 0.7071067811865476
_INV_SQRT2PI = 0.3989422804014327


def _gelu(x):
    return 0.5 * x * (1.0 + lax.erf(x * _INV_SQRT2))


def _gelu_grad(x):
    return 0.5 * (1.0 + lax.erf(x * _INV_SQRT2)) + x * jnp.exp(-0.5 * x * x) * _INV_SQRT2PI


def _silu_grad(x):
    s = jax.nn.sigmoid(x)
    return s * (1.0 + x * (1.0 - s))


def _ffn_fwd(x, g, wa, mi, name, payload=None):
    t, d = x.shape
    f = wa.shape[1]
    tm, tf = 1024, 256
    nc = f // tf
    groups = [slice(k * (tm // 2), (k + 1) * (tm // 2)) for k in range(2)]

    def body(x_ref, g_ref, wg_ref, wu_ref, wd_ref, xo_ref, gate_ref, up_ref, h_scr, acc_scr):
        c = pl.program_id(1)

        @pl.when(c == 0)
        def _():
            xhat, _ = _rms(x_ref[...])
            h_scr[...] = (xhat * g_ref[...]).astype(BF16)
            acc_scr[...] = jnp.zeros_like(acc_scr)

        wg, wu, wd = wg_ref[...], wu_ref[...], wd_ref[...]
        for rows in groups:
            h = h_scr[rows, :]
            gate = _dot_nt(h, wg)
            up = _dot_nt(h, wu)
            gate_ref[rows, :] = gate.astype(BF16)
            up_ref[rows, :] = up.astype(BF16)
            act = (gate * jax.nn.sigmoid(gate) * up).astype(BF16)
            acc_scr[rows, :] += _dot(act, wd)

        @pl.when(c == nc - 1)
        def _():
            xo_ref[...] = x_ref[...] + 0.5 * acc_scr[...]

    def wspec(k):
        return pl.BlockSpec((None, tf, d), lambda i, c: (mi + k, c, 0))

    return _call(
        body, name, (t // tm, nc),
        [pl.BlockSpec((tm, d), lambda i, c: (i, 0)), pl.BlockSpec((1, d), lambda i, c: (0, 0)),
         wspec(0), wspec(1), wspec(2)],
        [pl.BlockSpec((tm, d), lambda i, c: (i, 0)), pl.BlockSpec((tm, tf), lambda i, c: (i, c)),
         pl.BlockSpec((tm, tf), lambda i, c: (i, c))],
        [jax.ShapeDtypeStruct((t, d), F32), jax.ShapeDtypeStruct((t, f), BF16), jax.ShapeDtypeStruct((t, f), BF16)],
        [pltpu.VMEM((tm, d), BF16), pltpu.VMEM((tm, d), F32)],
        ("parallel", "arbitrary"), (x, g, wa, wa, wa), payload)


def _ffn_bwd(x, g, dxo, gate, up, wa, mi, name, payload=None):
    t, d = x.shape
    f = wa.shape[1]
    tm, tf = 1024, 256
    nc = f // tf
    groups = [slice(k * (tm // 2), (k + 1) * (tm // 2)) for k in range(2)]

    def body(x_ref, g_ref, dxo_ref, gate_ref, up_ref, wg_ref, wu_ref, wd_ref,
             dx_ref, dgate_ref, dup_ref, act_ref, h_ref, dy_ref, dg_ref, acc_scr):
        i, c = pl.program_id(0), pl.program_id(1)

        @pl.when(c == 0)
        def _():
            xhat, _ = _rms(x_ref[...])
            h_ref[...] = (xhat * g_ref[...]).astype(BF16)
            dy_ref[...] = (0.5 * dxo_ref[...]).astype(BF16)
            acc_scr[...] = jnp.zeros_like(acc_scr)

        @pl.when((c == 0) & (i == 0))
        def _():
            dg_ref[...] = jnp.zeros_like(dg_ref)

        wg, wu, wd = wg_ref[...], wu_ref[...], wd_ref[...]
        for rows in groups:
            gt = gate_ref[rows, :].astype(F32)
            u = up_ref[rows, :].astype(F32)
            s = jax.nn.sigmoid(gt)
            silu = gt * s
            dact = _dot_nt(dy_ref[rows, :], wd)
            dgate = (dact * u * (s * (1.0 + gt * (1.0 - s)))).astype(BF16)
            dup = (dact * silu).astype(BF16)
            dgate_ref[rows, :] = dgate
            dup_ref[rows, :] = dup
            act_ref[rows, :] = (silu * u).astype(BF16)
            acc_scr[rows, :] += _dot(dgate, wg) + _dot(dup, wu)

        @pl.when(c == nc - 1)
        def _():
            xhat, rstd = _rms(x_ref[...])
            dxn, dg = _rms_bwd(xhat, rstd, g_ref[...], acc_scr[...])
            dx_ref[...] = dxo_ref[...] + dxn
            dg_ref[...] += dg

    def wspec(k):
        return pl.BlockSpec((None, tf, d), lambda i, c: (mi + k, c, 0))

    row = pl.BlockSpec((tm, d), lambda i, c: (i, 0))
    col = pl.BlockSpec((tm, tf), lambda i, c: (i, c))
    vec = pl.BlockSpec((1, d), lambda i, c: (0, 0))
    return _call(
        body, name, (t // tm, nc),
        [row, vec, row, col, col, wspec(0), wspec(1), wspec(2)],
        [row, col, col, col, row, row, vec],
        [jax.ShapeDtypeStruct((t, d), F32), jax.ShapeDtypeStruct((t, f), BF16),
         jax.ShapeDtypeStruct((t, f), BF16), jax.ShapeDtypeStruct((t, f), BF16),
         jax.ShapeDtypeStruct((t, d), BF16), jax.ShapeDtypeStruct((t, d), BF16),
         jax.ShapeDtypeStruct((1, d), F32)],
        [pltpu.VMEM((tm, d), F32)],
        ("arbitrary", "arbitrary"), (x, g, dxo, gate, up, wa, wa, wa), payload)


def _tn_matmul(a, b, name, payload=None):
    t, m = a.shape
    n = b.shape[1]
    tk = 512
    tmm = m // 2 if (m // 2) % LANES == 0 else m
    nk = t // tk

    def body(a_ref, b_ref, o_ref, acc_scr):
        k = pl.program_id(1)

        @pl.when(k == 0)
        def _():
            acc_scr[...] = jnp.zeros_like(acc_scr)

        acc_scr[...] += _dot_tn(a_ref[...].astype(BF16), b_ref[...].astype(BF16))

        @pl.when(k == nk - 1)
        def _():
            o_ref[...] = acc_scr[...].astype(BF16)

    (out,), p_outs = _call(
        body, name, (m // tmm, nk),
        [pl.BlockSpec((tk, tmm), lambda j, k: (k, j)), pl.BlockSpec((tk, n), lambda j, k: (k, 0))],
        [pl.BlockSpec((tmm, n), lambda j, k: (j, 0))],
        [jax.ShapeDtypeStruct((m, n), BF16)],
        [pltpu.VMEM((tmm, n), F32)],
        ("parallel", "arbitrary"), (a, b), payload)
    return out, p_outs


def _proj_in_fwd(x, g, wb, li, name, payload=None):
    t, d = x.shape
    n = wb.shape[1]
    tm = 512

    def body(x_ref, g_ref, w_ref, z_ref):
        xhat, _ = _rms(x_ref[...])
        z_ref[...] = _dot_nt((xhat * g_ref[...]).astype(BF16), w_ref[...])

    (z,), p_outs = _call(
        body, name, (t // tm,),
        [pl.BlockSpec((tm, d), lambda i: (i, 0)), pl.BlockSpec((1, d), lambda i: (0, 0)),
         pl.BlockSpec((None, n, d), lambda i: (li, 0, 0))],
        [pl.BlockSpec((tm, n), lambda i: (i, 0))],
        [jax.ShapeDtypeStruct((t, n), F32)], [], ("parallel",), (x, g, wb), payload)
    return z, p_outs


def _proj_out_fwd(x, cat, wc, li, name, payload=None):
    t, d = x.shape
    tm = 512

    def body(x_ref, cat_ref, w_ref, xo_ref):
        xo_ref[...] = x_ref[...] + _dot(cat_ref[...], w_ref[...])

    (xo,), p_outs = _call(
        body, name, (t // tm,),
        [pl.BlockSpec((tm, d), lambda i: (i, 0)), pl.BlockSpec((tm, d), lambda i: (i, 0)),
         pl.BlockSpec((None, d, d), lambda i: (li, 0, 0))],
        [pl.BlockSpec((tm, d), lambda i: (i, 0))],
        [jax.ShapeDtypeStruct((t, d), F32)], [], ("parallel",), (x, cat, wc), payload)
    return xo, p_outs


def _proj_out_bwd(dxo, wc, li, name):
    t, d = dxo.shape
    tm = 512

    def body(dxo_ref, w_ref, dcat_ref):
        dcat_ref[...] = _dot_nt(dxo_ref[...].astype(BF16), w_ref[...])

    return pl.pallas_call(
        body, name=name, grid=(t // tm,),
        in_specs=[pl.BlockSpec((tm, d), lambda i: (i, 0)), pl.BlockSpec((None, d, d), lambda i: (li, 0, 0))],
        out_specs=pl.BlockSpec((tm, d), lambda i: (i, 0)),
        out_shape=jax.ShapeDtypeStruct((t, d), F32),
        compiler_params=_cparams(("parallel",)),
    )(dxo, wc)


def _proj_in_bwd(x, g, dxo, dz, wb, li, name):
    t, d = x.shape
    n = wb.shape[1]
    tm = 512

    def body(x_ref, g_ref, dxo_ref, dz_ref, w_ref, dx_ref, h_ref, dg_ref):
        i = pl.program_id(0)

        @pl.when(i == 0)
        def _():
            dg_ref[...] = jnp.zeros_like(dg_ref)

        xhat, rstd = _rms(x_ref[...])
        h_ref[...] = (xhat * g_ref[...]).astype(BF16)
        dh = _dot(dz_ref[...], w_ref[...])
        dxn, dg = _rms_bwd(xhat, rstd, g_ref[...], dh)
        dx_ref[...] = dxo_ref[...] + dxn
        dg_ref[...] += dg

    row = pl.BlockSpec((tm, d), lambda i: (i, 0))
    vec = pl.BlockSpec((1, d), lambda i: (0, 0))
    return pl.pallas_call(
        body, name=name, grid=(t // tm,),
        in_specs=[row, vec, row, pl.BlockSpec((tm, n), lambda i: (i, 0)),
                  pl.BlockSpec((None, n, d), lambda i: (li, 0, 0))],
        out_specs=[row, row, vec],
        out_shape=[jax.ShapeDtypeStruct((t, d), F32), jax.ShapeDtypeStruct((t, d), BF16),
                   jax.ShapeDtypeStruct((1, d), F32)],
        compiler_params=_cparams(("arbitrary",)),
    )(x, g, dxo, dz, wb)


def _lane_ids(shape):
    return lax.broadcasted_iota(jnp.int32, shape, 1)


def _tril(w):
    r = lax.broadcasted_iota(jnp.int32, w.shape, 0)
    c = lax.broadcasted_iota(jnp.int32, w.shape, 1)
    return jnp.where(r >= c, w, 0.0)


def _shift_down(x, k):
    return x if k == 0 else pltpu.roll(x, k, 0)


def _shift_up(x, k):
    return x if k == 0 else pltpu.roll(x, x.shape[0] - k, 0)


def _causal_conv(ext, w, n_out):
    acc = None
    for b in range(8):
        rolled = _shift_down(ext, b)
        for a in range((CONV_WIDTH - 1 - b) // 8 + 1):
            j = 8 * a + b
            term = rolled[HALO - 8 * a:HALO - 8 * a + n_out] * w[CONV_WIDTH - 1 - j:CONV_WIDTH - j]
            acc = term if acc is None else acc + term
    return acc


def _conv_wgrad(ext, dhc, n_out):
    rows = [None] * CONV_WIDTH
    for b in range(8):
        rolled = _shift_down(ext, b)
        for a in range((CONV_WIDTH - 1 - b) // 8 + 1):
            j = 8 * a + b
            rows[CONV_WIDTH - 1 - j] = jnp.sum(
                rolled[HALO - 8 * a:HALO - 8 * a + n_out] * dhc, axis=0, keepdims=True)
    return rows


def _anticausal_conv(ext, w, n_out):
    acc = None
    for b in range(8):
        rolled = _shift_up(ext, b)
        for a in range((CONV_WIDTH - 1 - b) // 8 + 1):
            j = 8 * a + b
            term = rolled[8 * a:8 * a + n_out] * w[CONV_WIDTH - 1 - j:CONV_WIDTH - j]
            acc = term if acc is None else acc + term
    return acc


def _window_sums(ext, shift):
    s2 = ext + shift(ext, 1)
    s4 = s2 + shift(s2, 2)
    s8 = s4 + shift(s4, 4)
    s16 = s8 + shift(s8, 8)
    grp = _lane_ids(ext.shape) // HEAD_DIM
    return jnp.where(grp == 0, s2, jnp.where(grp == 1, s4, jnp.where(grp == 2, s8, s16)))


def _pool_count(t0, n, width):
    pos = (lax.broadcasted_iota(jnp.int32, (n, width), 0) + (t0 + 1)).astype(F32)
    grp = _lane_ids((n, width)) // HEAD_DIM
    win = jnp.where(grp == 0, 2.0, jnp.where(grp == 1, 4.0, jnp.where(grp == 2, 8.0, 16.0)))
    return jnp.minimum(pos, win)


def _block_diag(pw):
    gn, cg, _ = pw.shape
    rows = []
    for gi in range(gn):
        parts = [pw[gi] if gj == gi else jnp.zeros((cg, cg), pw.dtype) for gj in range(gn)]
        rows.append(jnp.concatenate(parts, axis=1))
    return jnp.concatenate(rows, axis=0)


def _head_pair_mix(w_even, w_odd, v):
    lo = _lane_ids((CHUNK, LANES)) < HEAD_DIM
    return jnp.where(lo, _dot(w_even, v), _dot(w_odd, v))


def _mix_fwd(z, p, name, payload=None):
    t, d_in = z.shape
    sgu = p["sgu_ln_g"].shape[1]
    pool = p["pool_scale"].shape[1]
    d_mix = 2 * sgu + pool
    tm = 512
    n_i = t // tm
    hb = tm // HALO

    def body(z_ref, zp_ref, lng_ref, lnb_ref, ws_ref, bs_ref, cw_ref, cb_ref, clg_ref, clb_ref,
             bd_ref, ps_ref, cat_ref):
        i = pl.program_id(0)
        first = i == 0
        z_main = z_ref[...]
        z_prev = jnp.where(first, 0.0, zp_ref[...])

        lng, lnb = lng_ref[...], lnb_ref[...]
        wt = [_tril(ws_ref[h]).astype(BF16) for h in range(sgu // HEAD_DIM)]
        for n in range(tm // CHUNK):
            rows = slice(n * CHUNK, (n + 1) * CHUNK)
            u = _gelu(z_main[rows, 0:sgu])
            vhat, _ = _ln(_gelu(z_main[rows, sgu:2 * sgu]))
            vn = (vhat * lng + lnb).astype(BF16)
            for gp in range(sgu // LANES):
                ls = slice(gp * LANES, (gp + 1) * LANES)
                mixed = _head_pair_mix(wt[2 * gp], wt[2 * gp + 1], vn[:, ls]) + bs_ref[:, ls]
                cat_ref[rows, ls] = (u[:, ls] * mixed).astype(BF16)

        def glu(zz):
            return zz[:, 2 * sgu:3 * sgu] * jax.nn.sigmoid(zz[:, 3 * sgu:4 * sgu])

        ext = jnp.concatenate([glu(z_prev), glu(z_main)], axis=0)
        hc = _causal_conv(ext, cw_ref[...], tm) + cb_ref[...]
        hhat, _ = _ln(hc)
        bn = hhat * clg_ref[...] + clb_ref[...]
        cat_ref[:, sgu:2 * sgu] = (bn * jax.nn.sigmoid(bn)).astype(BF16)

        pext = jnp.concatenate([z_prev[:, 4 * sgu:], z_main[:, 4 * sgu:]], axis=0)
        sums = _window_sums(pext, _shift_down)[HALO:]
        pooled = sums / _pool_count(i * tm, tm, pool) - z_main[:, 4 * sgu:]
        mixed_c = _dot(pooled.astype(BF16), bd_ref[...].astype(BF16))
        cat_ref[:, 2 * sgu:] = (mixed_c * ps_ref[...]).astype(BF16)

    def vec(n):
        return pl.BlockSpec((1, n), lambda i: (0, 0))

    (cat,), p_outs = _call(
        body, name, (n_i,),
        [pl.BlockSpec((tm, d_in), lambda i: (i, 0)),
         pl.BlockSpec((HALO, d_in), lambda i: (jnp.maximum(i * hb - 1, 0), 0)),
         vec(sgu), vec(sgu),
         pl.BlockSpec(p["w_spatial"].shape, lambda i: (0, 0, 0)),
         pl.BlockSpec((CHUNK, sgu), lambda i: (0, 0)),
         pl.BlockSpec((CONV_WIDTH, sgu), lambda i: (0, 0)),
         vec(sgu), vec(sgu), vec(sgu),
         pl.BlockSpec((pool, pool), lambda i: (0, 0)), vec(pool)],
        [pl.BlockSpec((tm, d_mix), lambda i: (i, 0))],
        [jax.ShapeDtypeStruct((t, d_mix), BF16)], [], ("parallel",),
        (z, z, p["sgu_ln_g"], p["sgu_ln_b"], p["w_spatial"], p["bs_full"], p["conv_w"], p["conv_b"],
         p["conv_ln_g"], p["conv_ln_b"], p["bd"], p["pool_scale"]), payload)
    return cat, p_outs


_R_SGU_G, _R_SGU_B, _R_CONV_B, _R_CLN_G, _R_CLN_B, _R_CONV_W = 0, 1, 2, 3, 4, 8
_R384_ROWS = 40


def _mix_bwd(z, dcat, p, name, payload=None):
    t, d_in = z.shape
    sgu = p["sgu_ln_g"].shape[1]
    pool = p["pool_scale"].shape[1]
    d_mix = 2 * sgu + pool
    n_head = sgu // HEAD_DIM
    tm = 512
    n_i = t // tm
    hb = tm // HALO

    def body(z_ref, zp_ref, zn_ref, dc_ref, dcn_ref, lng_ref, lnb_ref, ws_ref, bs_ref, cw_ref, cb_ref,
             clg_ref, clb_ref, bd_ref, ps_ref, dz_ref, g384_ref, gws_ref, gpool_ref, dbs_scr):
        i = pl.program_id(0)
        first, last = i == 0, i == n_i - 1

        @pl.when(first)
        def _():
            g384_ref[...] = jnp.zeros_like(g384_ref)
            gws_ref[...] = jnp.zeros_like(gws_ref)
            gpool_ref[...] = jnp.zeros_like(gpool_ref)
            dbs_scr[...] = jnp.zeros_like(dbs_scr)

        z_main = z_ref[...]
        z_prev = jnp.where(first, 0.0, zp_ref[...])
        z_next = jnp.where(last, 0.0, zn_ref[...])
        dc_main = dc_ref[...]
        dc_next = jnp.where(last, 0.0, dcn_ref[...])

        lng, lnb = lng_ref[...], lnb_ref[...]
        wt = [_tril(ws_ref[h]) for h in range(n_head)]
        wt_b = [w.astype(BF16) for w in wt]
        wtt_b = [w.T.astype(BF16) for w in wt]
        lo = _lane_ids((CHUNK, LANES)) < HEAD_DIM
        d_lng = jnp.zeros((1, sgu), F32)
        d_lnb = jnp.zeros((1, sgu), F32)
        dws = [jnp.zeros((CHUNK, CHUNK), F32) for _ in range(n_head)]
        for n in range(tm // CHUNK):
            rows = slice(n * CHUNK, (n + 1) * CHUNK)
            au, av = z_main[rows, 0:sgu], z_main[rows, sgu:2 * sgu]
            u = _gelu(au)
            vhat, vrstd = _ln(_gelu(av))
            vn = (vhat * lng + lnb).astype(BF16)
            da = dc_main[rows, 0:sgu]
            dmixed = da * u
            dbs_scr[...] += dmixed
            dvn_parts, du_parts = [], []
            for gp in range(sgu // LANES):
                ls = slice(gp * LANES, (gp + 1) * LANES)
                vn_g = vn[:, ls]
                mixed = _head_pair_mix(wt_b[2 * gp], wt_b[2 * gp + 1], vn_g) + bs_ref[:, ls]
                du_parts.append(da[:, ls] * mixed)
                dm_g = dmixed[:, ls]
                dm_b = dm_g.astype(BF16)
                dvn_parts.append(jnp.where(lo, _dot(wtt_b[2 * gp], dm_b), _dot(wtt_b[2 * gp + 1], dm_b)))
                dws[2 * gp] = dws[2 * gp] + _dot_nt(jnp.where(lo, dm_g, 0.0).astype(BF16), vn_g)
                dws[2 * gp + 1] = dws[2 * gp + 1] + _dot_nt(jnp.where(lo, 0.0, dm_g).astype(BF16), vn_g)
            dvn = jnp.concatenate(dvn_parts, axis=1)
            du = jnp.concatenate(du_parts, axis=1)
            dv, dg_n, db_n = _ln_bwd(vhat, vrstd, lng, dvn)
            d_lng = d_lng + dg_n
            d_lnb = d_lnb + db_n
            dz_ref[rows, 0:sgu] = (du * _gelu_grad(au)).astype(BF16)
            dz_ref[rows, sgu:2 * sgu] = (dv * _gelu_grad(av)).astype(BF16)
        for h in range(n_head):
            gws_ref[h] += _tril(dws[h])
        g384_ref[_R_SGU_G:_R_SGU_G + 1, :] += d_lng
        g384_ref[_R_SGU_B:_R_SGU_B + 1, :] += d_lnb

        @pl.when(last)
        def _():
            r = lax.broadcasted_iota(jnp.int32, (sgu, LANES), 0)
            c = lax.broadcasted_iota(jnp.int32, (sgu, LANES), 1)
            sel = (r // HEAD_DIM == c).astype(BF16)
            gws_ref[n_head] = _split_dot(dbs_scr[...], sel)

        cw, clg = cw_ref[...], clg_ref[...]
        bcols = slice(2 * sgu, 4 * sgu)
        zb = jnp.concatenate([z_prev[:, bcols], z_main[:, bcols], z_next[:, bcols]], axis=0)
        bval, bgate = zb[:, 0:sgu], zb[:, sgu:2 * sgu]
        sg = jax.nn.sigmoid(bgate)
        hglu = bval * sg
        n_out = tm + HALO
        hc = _causal_conv(hglu, cw, n_out) + cb_ref[...]
        hhat, hrstd = _ln(hc)
        bn = hhat * clg + clb_ref[...]
        db = jnp.concatenate([dc_main[:, sgu:2 * sgu], dc_next[:, sgu:2 * sgu]], axis=0)
        dbn = db * _silu_grad(bn)
        dhc_all, _, _ = _ln_bwd(hhat, hrstd, clg, dbn)
        dbn_m, hhat_m, dhc = dbn[:tm], hhat[:tm], dhc_all[:tm]
        g384_ref[_R_CLN_G:_R_CLN_G + 1, :] += jnp.sum(dbn_m * hhat_m, axis=0, keepdims=True)
        g384_ref[_R_CLN_B:_R_CLN_B + 1, :] += jnp.sum(dbn_m, axis=0, keepdims=True)
        g384_ref[_R_CONV_B:_R_CONV_B + 1, :] += jnp.sum(dhc, axis=0, keepdims=True)
        wrows = _conv_wgrad(hglu[:HALO + tm], dhc, tm)
        for k in range(CONV_WIDTH):
            g384_ref[_R_CONV_W + k:_R_CONV_W + k + 1, :] += wrows[k]
        dhglu = _anticausal_conv(dhc_all, cw, tm)
        bval_m, sg_m = bval[HALO:HALO + tm], sg[HALO:HALO + tm]
        dz_ref[:, 2 * sgu:3 * sgu] = (dhglu * sg_m).astype(BF16)
        dz_ref[:, 3 * sgu:4 * sgu] = (dhglu * bval_m * sg_m * (1.0 - sg_m)).astype(BF16)

        bd_b = bd_ref[...].astype(BF16)
        ps = ps_ref[...]
        p_main = z_main[:, 4 * sgu:]
        pext = jnp.concatenate([z_prev[:, 4 * sgu:], p_main], axis=0)
        cnt = _pool_count(i * tm, n_out, pool)
        pooled = _window_sums(pext, _shift_down)[HALO:] / cnt[:tm] - p_main
        pooled_b = pooled.astype(BF16)
        dcc = jnp.concatenate([dc_main[:, 2 * sgu:], dc_next[:, 2 * sgu:]], axis=0)
        dmix_c = dcc * ps
        mixed_c = _dot(pooled_b, bd_b)
        grp_r = lax.broadcasted_iota(jnp.int32, (pool, pool), 0) // HEAD_DIM
        grp_c = lax.broadcasted_iota(jnp.int32, (pool, pool), 1) // HEAD_DIM
        gpool_ref[0:pool, :] += jnp.where(grp_r == grp_c, _dot_tn(pooled_b, dmix_c[:tm].astype(BF16)), 0.0)
        gpool_ref[pool:pool + 1, :] += jnp.sum(dcc[:tm] * mixed_c, axis=0, keepdims=True)
        dpooled = _dot_nt(dmix_c.astype(BF16), bd_b)
        q = dpooled / cnt
        dp = _window_sums(q, _shift_up)[:tm] - dpooled[:tm]
        dz_ref[:, 4 * sgu:] = dp.astype(BF16)

    def vec(n):
        return pl.BlockSpec((1, n), lambda i: (0, 0))

    def prev_map(i):
        return (jnp.maximum(i * hb - 1, 0), 0)

    def next_map(i):
        return (jnp.minimum((i + 1) * hb, n_i * hb - 1), 0)

    return _call(
        body, name, (n_i,),
        [pl.BlockSpec((tm, d_in), lambda i: (i, 0)),
         pl.BlockSpec((HALO, d_in), prev_map), pl.BlockSpec((HALO, d_in), next_map),
         pl.BlockSpec((tm, d_mix), lambda i: (i, 0)), pl.BlockSpec((HALO, d_mix), next_map),
         vec(sgu), vec(sgu),
         pl.BlockSpec(p["w_spatial"].shape, lambda i: (0, 0, 0)),
         pl.BlockSpec((CHUNK, sgu), lambda i: (0, 0)),
         pl.BlockSpec((CONV_WIDTH, sgu), lambda i: (0, 0)),
         vec(sgu), vec(sgu), vec(sgu),
         pl.BlockSpec((pool, pool), lambda i: (0, 0)), vec(pool)],
        [pl.BlockSpec((tm, d_in), lambda i: (i, 0)),
         pl.BlockSpec((_R384_ROWS, sgu), lambda i: (0, 0)),
         pl.BlockSpec((n_head + 1, CHUNK, CHUNK), lambda i: (0, 0, 0)),
         pl.BlockSpec((pool + 8, pool), lambda i: (0, 0))],
        [jax.ShapeDtypeStruct((t, d_in), BF16),
         jax.ShapeDtypeStruct((_R384_ROWS, sgu), F32),
         jax.ShapeDtypeStruct((n_head + 1, CHUNK, CHUNK), F32),
         jax.ShapeDtypeStruct((pool + 8, pool), F32)],
        [pltpu.VMEM((CHUNK, sgu), F32)], ("arbitrary",),
        (z, z, z, dcat, dcat, p["sgu_ln_g"], p["sgu_ln_b"], p["w_spatial"], p["bs_full"], p["conv_w"],
         p["conv_b"], p["conv_ln_g"], p["conv_ln_b"], p["bd"], p["pool_scale"]), payload)


def _loss_head(x, g, target, name):
    t, d = x.shape
    tm = 512

    def body(x_ref, g_ref, tgt_ref, dx_ref, dg_ref, loss_ref):
        i = pl.program_id(0)

        @pl.when(i == 0)
        def _():
            dg_ref[...] = jnp.zeros_like(dg_ref)
            loss_ref[...] = jnp.zeros_like(loss_ref)

        gv = g_ref[...]
        xhat, rstd = _rms(x_ref[...])
        err = xhat * gv - tgt_ref[...]
        loss_ref[...] += jnp.zeros_like(loss_ref) + 0.5 * jnp.sum(jnp.mean(err * err, axis=-1, keepdims=True))
        dxn, dg = _rms_bwd(xhat, rstd, gv, err * (1.0 / d))
        dx_ref[...] = dxn
        dg_ref[...] += dg

    row = pl.BlockSpec((tm, d), lambda i: (i, 0))
    vec = pl.BlockSpec((1, d), lambda i: (0, 0))
    return pl.pallas_call(
        body, name=name, grid=(t // tm,),
        in_specs=[row, vec, row],
        out_specs=[row, vec, pl.BlockSpec((1, LANES), lambda i: (0, 0))],
        out_shape=[jax.ShapeDtypeStruct((t, d), F32), jax.ShapeDtypeStruct((1, d), F32),
                   jax.ShapeDtypeStruct((1, LANES), F32)],
        compiler_params=_cparams(("arbitrary",)),
    )(x, g, target)


def _all_gather(arrs, name, extra=None):
    gather = _GatherIci(arrs)
    n = len(arrs)
    forward = _GatherForward([jax.ShapeDtypeStruct(s.shape, s.dtype) for s in gather.out_shapes])
    x_in = len(extra.ins) if extra else 0
    x_out = len(extra.out_shapes) if extra else 0

    def body(*refs):
        ins, x_ins = refs[:n], refs[n:n + x_in]
        outs, x_outs = refs[n + x_in:2 * n + x_in], refs[2 * n + x_in:2 * n + x_in + x_out]
        sems = refs[2 * n + x_in + x_out:]
        first = gather.build(ins, outs, *sems[0:3])
        first.start()
        if extra:
            beside = extra.build(x_ins, x_outs, *sems[6:9])
            beside.start()
        first.wait()
        second = forward.build(outs, outs, *sems[3:6])
        second.start()
        second.wait()
        if extra:
            beside.wait()

    outs = pl.pallas_call(
        body, name=name,
        in_specs=[ANY] * (n + x_in), out_specs=[ANY] * (n + x_out),
        out_shape=list(gather.out_shapes) + (list(extra.out_shapes) if extra else []),
        scratch_shapes=gather.sem_shapes() + forward.sem_shapes() + (extra.sem_shapes() if extra else []),
    )(*arrs, *(extra.ins if extra else []))
    return list(outs[:n]), list(outs[n:])


def _pair_sum(grad, recv, cidx, name):
    r = grad.shape[0] // N_DEV
    cols = grad.shape[1]

    def body(c_ref, g_ref, r_ref, o_ref):
        o_ref[...] = (g_ref[...].astype(F32) + r_ref[...].astype(F32)).astype(BF16)

    return pl.pallas_call(
        body, name=name,
        grid_spec=pltpu.PrefetchScalarGridSpec(
            num_scalar_prefetch=1, grid=(N_CHIP,),
            in_specs=[pl.BlockSpec((r, cols), lambda q, c: (2 * q + c[0], 0)),
                      pl.BlockSpec((r, cols), lambda q, c: (q, 0))],
            out_specs=pl.BlockSpec((r, cols), lambda q, c: (q, 0))),
        out_shape=jax.ShapeDtypeStruct((N_CHIP * r, cols), BF16),
        compiler_params=_cparams(("parallel",)),
    )(cidx, grad, recv)


def _sum_blocks(parts, nblk, name):
    r = parts.shape[0] // nblk
    cols = parts.shape[1]

    def body(p_ref, o_ref):
        acc = p_ref[0:r, :].astype(F32)
        for q in range(1, nblk):
            acc = acc + p_ref[q * r:(q + 1) * r, :].astype(F32)
        o_ref[...] = acc

    return pl.pallas_call(
        body, name=name,
        out_shape=jax.ShapeDtypeStruct((r, cols), F32),
        compiler_params=_cparams(),
    )(parts)


def _adamw_math(w, g, m, v):
    m = ADAM_B1 * m + (1.0 - ADAM_B1) * g
    v = ADAM_B2 * v + (1.0 - ADAM_B2) * (g * g)
    m_hat = m / (1.0 - ADAM_B1 ** ADAM_STEP)
    v_hat = v / (1.0 - ADAM_B2 ** ADAM_STEP)
    delta = -ADAM_LR * (m_hat / (jnp.sqrt(v_hat) + ADAM_EPS) + ADAM_WD * w)
    return delta, m, v


def _adamw_big(w, g, m, v, name):
    rows, cols = w.shape
    tr = max(k for k in range(8, 513, 8) if rows % k == 0)

    def body(w_ref, g_ref, m_ref, v_ref, d_ref, mo_ref, vo_ref):
        d_ref[...], mo_ref[...], vo_ref[...] = _adamw_math(w_ref[...], g_ref[...], m_ref[...], v_ref[...])

    spec = pl.BlockSpec((tr, cols), lambda i: (i, 0))
    return pl.pallas_call(
        body, name=name, grid=(rows // tr,),
        in_specs=[spec] * 4, out_specs=[spec] * 3,
        out_shape=[jax.ShapeDtypeStruct(w.shape, F32)] * 3,
        compiler_params=_cparams(("parallel",)),
    )(w, g, m, v)


def _adamw_small(ws, gs, ms, vs, name):
    n = len(ws)

    def body(*refs):
        for k in range(n):
            w_ref, g_ref, m_ref, v_ref = (refs[j * n + k] for j in range(4))
            d_ref, mo_ref, vo_ref = (refs[(4 + j) * n + k] for j in range(3))
            d_ref[...], mo_ref[...], vo_ref[...] = _adamw_math(w_ref[...], g_ref[...], m_ref[...], v_ref[...])

    shapes = [jax.ShapeDtypeStruct(w.shape, F32) for w in ws]
    return pl.pallas_call(
        body, name=name, out_shape=shapes * 3, compiler_params=_cparams(),
    )(*ws, *gs, *ms, *vs)


def kernel(x, ffn1_norm, ffn1_w_gate, ffn1_w_up, ffn1_w_down, mix_norm, w_in, sgu_ln_g, sgu_ln_b, w_spatial, b_spatial, conv_w, conv_b, conv_ln_g, conv_ln_b, pool_w, pool_scale, w_out, ffn2_norm, ffn2_w_gate, ffn2_w_up, ffn2_w_down, final_norm, loss_target, m_ffn1_norm, m_ffn1_w_gate, m_ffn1_w_up, m_ffn1_w_down, m_mix_norm, m_w_in, m_sgu_ln_g, m_sgu_ln_b, m_w_spatial, m_b_spatial, m_conv_w, m_conv_b, m_conv_ln_g, m_conv_ln_b, m_pool_w, m_pool_scale, m_w_out, m_ffn2_norm, m_ffn2_w_gate, m_ffn2_w_up, m_ffn2_w_down, m_final_norm, v_ffn1_norm, v_ffn1_w_gate, v_ffn1_w_up, v_ffn1_w_down, v_mix_norm, v_w_in, v_sgu_ln_g, v_sgu_ln_b, v_w_spatial, v_b_spatial, v_conv_w, v_conv_b, v_conv_ln_g, v_conv_ln_b, v_pool_w, v_pool_scale, v_w_out, v_ffn2_norm, v_ffn2_w_gate, v_ffn2_w_up, v_ffn2_w_down, v_final_norm):
    names = ["ffn1_norm", "ffn1_w_gate", "ffn1_w_up", "ffn1_w_down", "mix_norm", "w_in", "sgu_ln_g", "sgu_ln_b",
             "w_spatial", "b_spatial", "conv_w", "conv_b", "conv_ln_g", "conv_ln_b", "pool_w", "pool_scale",
             "w_out", "ffn2_norm", "ffn2_w_gate", "ffn2_w_up", "ffn2_w_down", "final_norm"]
    W = dict(zip(names, [ffn1_norm, ffn1_w_gate, ffn1_w_up, ffn1_w_down, mix_norm, w_in, sgu_ln_g, sgu_ln_b,
                         w_spatial, b_spatial, conv_w, conv_b, conv_ln_g, conv_ln_b, pool_w, pool_scale, w_out,
                         ffn2_norm, ffn2_w_gate, ffn2_w_up, ffn2_w_down, final_norm]))
    M = dict(zip(names, [m_ffn1_norm, m_ffn1_w_gate, m_ffn1_w_up, m_ffn1_w_down, m_mix_norm, m_w_in, m_sgu_ln_g,
                         m_sgu_ln_b, m_w_spatial, m_b_spatial, m_conv_w, m_conv_b, m_conv_ln_g, m_conv_ln_b,
                         m_pool_w, m_pool_scale, m_w_out, m_ffn2_norm, m_ffn2_w_gate, m_ffn2_w_up, m_ffn2_w_down,
                         m_final_norm]))
    V = dict(zip(names, [v_ffn1_norm, v_ffn1_w_gate, v_ffn1_w_up, v_ffn1_w_down, v_mix_norm, v_w_in, v_sgu_ln_g,
                         v_sgu_ln_b, v_w_spatial, v_b_spatial, v_conv_w, v_conv_b, v_conv_ln_g, v_conv_ln_b,
                         v_pool_w, v_pool_scale, v_w_out, v_ffn2_norm, v_ffn2_w_gate, v_ffn2_w_up, v_ffn2_w_down,
                         v_final_norm]))

    depth, d = ffn1_norm.shape
    t = x.shape[1]
    sgu = sgu_ln_g.shape[1]
    pool = pool_scale.shape[1]
    n_head = sgu // HEAD_DIM
    cw_shard = conv_w.shape[2]
    xs = x.reshape(t, d)
    target = loss_target.reshape(t, d)

    def tr(w):
        return jnp.swapaxes(w, 1, 2).astype(BF16)

    ffn_shards = [[jnp.stack([tr(ffn1_w_gate)[l], tr(ffn1_w_up)[l], ffn1_w_down[l].astype(BF16)]),
                   jnp.stack([tr(ffn2_w_gate)[l], tr(ffn2_w_up)[l], ffn2_w_down[l].astype(BF16)])]
                  for l in range(depth)]
    win_shards = [tr(w_in)[l:l + 1] for l in range(depth)]
    wout_shards = [w_out[l:l + 1].astype(BF16) for l in range(depth)]
    cw_rows = depth * CONV_WIDTH
    cw_pad = -cw_rows % 8
    cw_send = jnp.pad(conv_w.reshape(cw_rows, cw_shard), ((0, cw_pad), (0, 0)))[None]
    wffn, wb, wc = {}, {}, {}
    (wffn[(0, 0)], wb[0], wc[0], cwg), _ = _all_gather(
        [ffn_shards[0][0], win_shards[0], wout_shards[0], cw_send], "ag_first")
    conv_w_full = cwg.reshape(N_DEV, cw_rows + cw_pad, cw_shard)[:, :cw_rows].reshape(
        N_DEV, depth, CONV_WIDTH, cw_shard).transpose(1, 2, 0, 3).reshape(depth, CONV_WIDTH, N_DEV * cw_shard)

    def mixer_params(l):
        return dict(
            sgu_ln_g=sgu_ln_g[l:l + 1], sgu_ln_b=sgu_ln_b[l:l + 1], w_spatial=w_spatial[l],
            bs_full=jnp.repeat(b_spatial[l].T, HEAD_DIM, axis=1),
            conv_w=conv_w_full[l], conv_b=conv_b[l:l + 1], conv_ln_g=conv_ln_g[l:l + 1],
            conv_ln_b=conv_ln_b[l:l + 1], bd=_block_diag(pool_w[l]), pool_scale=pool_scale[l:l + 1])

    saved = []
    cur = xs
    for l in range(depth):
        p = mixer_params(l)
        x0 = cur
        more = l + 1 < depth
        (x1, gate1, up1), part = _ffn_fwd(x0, ffn1_norm[l:l + 1], wffn[(l, 0)], 0, f"ffn1_fwd_{l}",
                                          _GatherIci([ffn_shards[l][1]]))
        z, (wffn[(l, 1)],) = _proj_in_fwd(x1, mix_norm[l:l + 1], wb[l], 0, f"proj_in_fwd_{l}", _GatherForward(part))
        cat, part = _mix_fwd(z, p, f"mix_fwd_{l}",
                             _GatherIci([win_shards[l + 1], wout_shards[l + 1]]) if more else None)
        x2, part = _proj_out_fwd(x1, cat, wc[l], 0, f"proj_out_fwd_{l}", _GatherForward(part) if more else None)
        if more:
            wb[l + 1], wc[l + 1] = part
        (x3, gate2, up2), part = _ffn_fwd(x2, ffn2_norm[l:l + 1], wffn[(l, 1)], 0, f"ffn2_fwd_{l}",
                                          _GatherIci([ffn_shards[l + 1][0]]) if more else None)
        if more:
            (wffn[(l + 1, 0)],) = _comm(_GatherForward(part), f"ag_forward_{l + 1}")
        saved.append((p, x0, gate1, up1, x1, z, cat, x2, gate2, up2))
        cur = x3

    dx, d_final, loss_part = _loss_head(cur, final_norm.reshape(1, d), target, "loss_head")

    cidx = lax.axis_index("c").astype(jnp.int32).reshape(1)
    from_chips = {}
    to_pair, to_chip = [], []
    small = []

    def pair_payload():
        return _PairExchange([g for _, g in to_pair]) if to_pair else None

    def pair_done(received):
        for ((nm, l), g), r in zip(to_pair, received):
            to_chip.append(((nm, l), _pair_sum(g, r, cidx, f"rs_pair_sum_{nm}_{l}")))
        to_pair.clear()

    def take_chip():
        items = list(to_chip)
        to_chip.clear()
        return items

    def chip_payload(items):
        return _ChipExchange([s for _, s in items]) if items else None

    def chip_done(items, landed):
        for (key, _), o in zip(items, landed):
            from_chips[key] = o

    def ffn_weight_grads(prefix, l, dgate, dup, act, h, dy):
        items = take_chip()
        for k, (nm, a, b) in enumerate([("w_gate", dgate, h), ("w_up", dup, h), ("w_down", act, dy)]):
            mine = items[k::3]
            g, landed = _tn_matmul(a, b, f"dw_{prefix}_{nm}_{l}", chip_payload(mine))
            chip_done(mine, landed)
            to_pair.append(((f"{prefix}_{nm}", l), g))

    for l in reversed(range(depth)):
        p, x0, gate1, up1, x1, z, cat, x2, gate2, up2 = saved[l]
        (dx, dgate, dup, act, h, dy, dg_ffn2), received = _ffn_bwd(
            x2, ffn2_norm[l:l + 1], dx, gate2, up2, wffn[(l, 1)], 0, f"ffn2_bwd_{l}", pair_payload())
        pair_done(received)
        ffn_weight_grads("ffn2", l, dgate, dup, act, h, dy)
        g_out, received = _tn_matmul(cat, dx, f"dw_w_out_{l}", pair_payload())
        pair_done(received)
        dcat = _proj_out_bwd(dx, wc[l], 0, f"proj_out_bwd_{l}")
        items = take_chip()
        (dz, g384, gws, gpool), landed = _mix_bwd(z, dcat, p, f"mix_bwd_{l}", chip_payload(items))
        chip_done(items, landed)
        dx, hm, dg_mix = _proj_in_bwd(x1, mix_norm[l:l + 1], dx, dz, wb[l], 0, f"proj_in_bwd_{l}")
        g_in, _ = _tn_matmul(dz, hm, f"dw_w_in_{l}")
        to_pair.extend([(("w_out", l), g_out), (("w_in", l), g_in)])
        (dx, dgate, dup, act, h, dy, dg_ffn1), received = _ffn_bwd(
            x0, ffn1_norm[l:l + 1], dx, gate1, up1, wffn[(l, 0)], 0, f"ffn1_bwd_{l}", pair_payload())
        pair_done(received)
        ffn_weight_grads("ffn1", l, dgate, dup, act, h, dy)
        small.append((l, g384, gws, gpool, dg_ffn1, dg_mix, dg_ffn2))
    grad_x = dx.reshape(x.shape)
    pair_done(_comm(pair_payload(), "rs_pair_exchange_last"))
    last_items = take_chip()

    small.sort(key=lambda s: s[0])
    norm_rows = []
    for (_, _, _, _, dg1, dgm, dg2) in small:
        norm_rows += [dg1, dgm, dg2]
    norm_rows += [d_final, jnp.pad(loss_part, ((0, 0), (0, d - LANES)))]
    n_norm = len(norm_rows)
    norm_pack = jnp.concatenate(norm_rows + [jnp.zeros((8 - n_norm % 8, d), F32)] * (n_norm % 8 != 0), axis=0)
    parts = [norm_pack]
    for (_, g384, gws, gpool, _, _, _) in small:
        parts += [g384, gws.reshape((n_head + 1) * CHUNK, CHUNK), gpool]
    gathered, landed = _all_gather([a[None] for a in parts], "ag_small_grads", chip_payload(last_items))
    chip_done(last_items, landed)

    big_names = ["ffn1_w_gate", "ffn1_w_up", "ffn1_w_down", "w_in", "w_out", "ffn2_w_gate", "ffn2_w_up",
                 "ffn2_w_down"]
    transposed = {"ffn1_w_gate", "ffn1_w_up", "w_in", "ffn2_w_gate", "ffn2_w_up"}
    grads = {}
    for nm in big_names:
        per_layer = []
        for l in range(depth):
            g = _sum_blocks(from_chips[(nm, l)], N_CHIP, f"rs_sum_{nm}_{l}")
            per_layer.append(g.T if nm in transposed else g)
        grads[nm] = jnp.stack(per_layer, axis=0)

    summed = [_sum_blocks(g[0], N_DEV, f"sum_small_{k}") for k, g in enumerate(gathered)]
    norm_sum = summed[0]
    loss = norm_sum[3 * depth + 1, 0]
    cpos = lax.axis_index("x") * 4 + lax.axis_index("y") * 2 + lax.axis_index("c")
    sg = {nm: [] for nm in names}
    for l in range(depth):
        g384, gws, gpool = summed[1 + 3 * l], summed[2 + 3 * l].reshape(n_head + 1, CHUNK, CHUNK), summed[3 + 3 * l]
        sg["ffn1_norm"].append(norm_sum[3 * l])
        sg["mix_norm"].append(norm_sum[3 * l + 1])
        sg["ffn2_norm"].append(norm_sum[3 * l + 2])
        sg["sgu_ln_g"].append(g384[_R_SGU_G])
        sg["sgu_ln_b"].append(g384[_R_SGU_B])
        sg["conv_b"].append(g384[_R_CONV_B])
        sg["conv_ln_g"].append(g384[_R_CLN_G])
        sg["conv_ln_b"].append(g384[_R_CLN_B])
        sg["conv_w"].append(lax.dynamic_slice_in_dim(g384[_R_CONV_W:_R_CONV_W + CONV_WIDTH], cpos * cw_shard,
                                                     cw_shard, axis=1))
        sg["w_spatial"].append(gws[:n_head])
        sg["b_spatial"].append(gws[n_head][:, :n_head].T)
        sg["pool_w"].append(jnp.stack([gpool[k * HEAD_DIM:(k + 1) * HEAD_DIM, k * HEAD_DIM:(k + 1) * HEAD_DIM]
                                       for k in range(pool // HEAD_DIM)], axis=0))
        sg["pool_scale"].append(gpool[pool])
    small_names = ["ffn1_norm", "mix_norm", "sgu_ln_g", "sgu_ln_b", "w_spatial", "b_spatial", "conv_w", "conv_b",
                   "conv_ln_g", "conv_ln_b", "pool_w", "pool_scale", "ffn2_norm"]
    for nm in small_names:
        grads[nm] = jnp.stack(sg[nm], axis=0)
    grads["final_norm"] = norm_sum[3 * depth]

    delta, new_m, new_v = {}, {}, {}
    for nm in big_names:
        shp = W[nm].shape
        two_d = (shp[0] * shp[1], shp[2])
        dl, mo, vo = _adamw_big(W[nm].reshape(two_d), grads[nm].reshape(two_d), M[nm].reshape(two_d),
                                V[nm].reshape(two_d), f"adamw_{nm}")
        delta[nm], new_m[nm], new_v[nm] = dl.reshape(shp), mo.reshape(shp), vo.reshape(shp)
    snames = small_names + ["final_norm"]

    def flat2(a):
        return a.reshape(-1, a.shape[-1])

    outs = _adamw_small([flat2(W[nm]) for nm in snames], [flat2(grads[nm]) for nm in snames],
                        [flat2(M[nm]) for nm in snames], [flat2(V[nm]) for nm in snames], "adamw_small")
    ns = len(snames)
    for k, nm in enumerate(snames):
        shp = W[nm].shape
        delta[nm], new_m[nm], new_v[nm] = (outs[k].reshape(shp), outs[ns + k].reshape(shp),
                                           outs[2 * ns + k].reshape(shp))

    return (loss, grad_x, *[grads[nm] for nm in names], *[delta[nm] for nm in names],
            *[new_m[nm] for nm in names], *[new_v[nm] for nm in names])
```

```python
import functools

import jax
import jax.numpy as jnp
from jax import lax
from jax.experimental import pallas as pl
from jax.experimental.pallas import tpu as pltpu

F32 = jnp.float32
BF16 = jnp.bfloat16
EPS = 1e-6
N_DEV = 8
N_CHIP = 4
MESH = pl.DeviceIdType.MESH
ANY = pl.BlockSpec(memory_space=pl.ANY)

VMEM_LIMIT_BYTES = 56 * 1024 * 1024
LANES = 128
HALO = 32
HEAD_DIM = 64
CHUNK = 128
CONV_WIDTH = 31
POOL_WINDOWS = (2, 4, 8, 16)

ADAM_LR = 0.001
ADAM_B1 = 0.9
ADAM_B2 = 0.999
ADAM_EPS = 1e-08
ADAM_WD = 0.01
ADAM_STEP = 10


def _cparams(sem=None):
    return pltpu.CompilerParams(dimension_semantics=sem, vmem_limit_bytes=VMEM_LIMIT_BYTES)


def _position():
    return lax.axis_index("x"), lax.axis_index("y"), lax.axis_index("c")


class _Copies:
    def __init__(self):
        self.local, self.sends, self.recvs = [], [], []

    def extend(self, other):
        self.local += other.local
        self.sends += other.sends
        self.recvs += other.recvs

    def start(self):
        for cp in self.local + self.sends:
            cp.start()

    def wait(self):
        for land, send_sems, recv_sems, k, peer in self.recvs:
            _remote(land, land, send_sems, recv_sems, k, peer).wait_recv()
        for cp in self.sends:
            cp.wait_send()
        for cp in self.local:
            cp.wait()


def _remote(src, dst, send_sems, recv_sems, k, to):
    return pltpu.make_async_remote_copy(src_ref=src, dst_ref=dst, send_sem=send_sems.at[k], recv_sem=recv_sems.at[k],
                                        device_id=to, device_id_type=MESH)


class _Payload:
    ins, out_shapes, aliases, n_remote, n_local = (), (), {}, 0, 0

    def sem_shapes(self):
        return [pltpu.SemaphoreType.DMA((max(self.n_remote, 1),)), pltpu.SemaphoreType.DMA((max(self.n_remote, 1),)),
                pltpu.SemaphoreType.DMA((max(self.n_local, 1),))]


class _GatherIci(_Payload):
    def __init__(self, shards):
        self.ins = list(shards)
        self.out_shapes = [jax.ShapeDtypeStruct((s.shape[0], N_DEV * s.shape[1], s.shape[2]), s.dtype) for s in shards]
        self.n_remote, self.n_local = 4 * len(shards), len(shards)

    def build(self, ins, outs, send_sems, recv_sems, local_sems, k0=0, l0=0):
        x, y, c = _position()
        peers = [(x, y, 1 - c), (1 - x, y, c), (x, 1 - y, c), (1 - x, 1 - y, c)]
        cps = _Copies()
        for a, (src, out) in enumerate(zip(ins, outs)):
            r = src.shape[1]

            def rows(px, py, pc, out=out, r=r):
                return out.at[:, pl.ds((4 * px + 2 * py + pc) * r, r), :]

            cps.local.append(pltpu.make_async_copy(src, rows(x, y, c), local_sems.at[l0 + a]))
            for k, peer in enumerate(peers):
                cps.sends.append(_remote(src, rows(x, y, c), send_sems, recv_sems, k0 + 4 * a + k, peer))
                cps.recvs.append((rows(*peer), send_sems, recv_sems, k0 + 4 * a + k, peer))
        return cps


class _GatherForward(_Payload):
    def __init__(self, partials):
        self.ins = list(partials)
        self.out_shapes = [jax.ShapeDtypeStruct(p.shape, p.dtype) for p in partials]
        self.aliases = {a: a for a in range(len(partials))}
        self.n_remote = 3 * len(partials)

    def build(self, ins, outs, send_sems, recv_sems, local_sems, k0=0, l0=0):
        x, y, c = _position()
        chips = [(1 - x, y), (x, 1 - y), (1 - x, 1 - y)]
        cps = _Copies()
        for a, out in enumerate(outs):
            r = out.shape[1] // N_DEV
            for k, (px, py) in enumerate(chips):
                mine = out.at[:, pl.ds((4 * px + 2 * py + c) * r, r), :]
                theirs = out.at[:, pl.ds((4 * px + 2 * py + 1 - c) * r, r), :]
                cps.sends.append(_remote(mine, mine, send_sems, recv_sems, k0 + 3 * a + k, (x, y, 1 - c)))
                cps.recvs.append((theirs, send_sems, recv_sems, k0 + 3 * a + k, (x, y, 1 - c)))
        return cps


class _PairExchange(_Payload):
    def __init__(self, grads):
        self.ins = list(grads)
        self.out_shapes = [jax.ShapeDtypeStruct((g.shape[0] // 2, g.shape[1]), g.dtype) for g in grads]
        self.n_remote = N_CHIP * len(grads)

    def build(self, ins, outs, send_sems, recv_sems, local_sems, k0=0, l0=0):
        x, y, c = _position()
        cps = _Copies()
        for a, (src, out) in enumerate(zip(ins, outs)):
            r = src.shape[0] // N_DEV
            for q in range(N_CHIP):
                land = out.at[pl.ds(q * r, r), :]
                cps.sends.append(_remote(src.at[pl.ds((2 * q + 1 - c) * r, r), :], land, send_sems, recv_sems,
                                         k0 + N_CHIP * a + q, (x, y, 1 - c)))
                cps.recvs.append((land, send_sems, recv_sems, k0 + N_CHIP * a + q, (x, y, 1 - c)))
        return cps


class _ChipExchange(_Payload):
    def __init__(self, sums):
        self.ins = list(sums)
        self.out_shapes = [jax.ShapeDtypeStruct(s.shape, s.dtype) for s in sums]
        self.n_remote, self.n_local = 3 * len(sums), len(sums)

    def build(self, ins, outs, send_sems, recv_sems, local_sems, k0=0, l0=0):
        x, y, c = _position()
        my_chip = 2 * x + y
        chips = [(1 - x, y), (x, 1 - y), (1 - x, 1 - y)]
        cps = _Copies()
        for a, (src, out) in enumerate(zip(ins, outs)):
            r = src.shape[0] // N_CHIP
            mine = out.at[pl.ds(my_chip * r, r), :]
            cps.local.append(pltpu.make_async_copy(src.at[pl.ds(my_chip * r, r), :], mine, local_sems.at[l0 + a]))
            for k, (px, py) in enumerate(chips):
                land = out.at[pl.ds((2 * px + py) * r, r), :]
                cps.sends.append(_remote(src.at[pl.ds((2 * px + py) * r, r), :], mine, send_sems, recv_sems,
                                         k0 + 3 * a + k, (px, py, c)))
                cps.recvs.append((land, send_sems, recv_sems, k0 + 3 * a + k, (px, py, c)))
        return cps


class _Merged(_Payload):
    def __init__(self, parts):
        self.parts = list(parts)
        self.ins = [a for p in parts for a in p.ins]
        self.out_shapes = [s for p in parts for s in p.out_shapes]
        self.aliases, self.offsets = {}, []
        i0 = o0 = k0 = l0 = 0
        for p in parts:
            self.offsets.append((i0, o0, k0, l0))
            self.aliases.update({i0 + i: o0 + o for i, o in p.aliases.items()})
            i0, o0, k0, l0 = i0 + len(p.ins), o0 + len(p.out_shapes), k0 + p.n_remote, l0 + p.n_local
        self.n_remote, self.n_local = k0, l0

    def build(self, ins, outs, send_sems, recv_sems, local_sems):
        cps = _Copies()
        for p, (i0, o0, k0, l0) in zip(self.parts, self.offsets):
            cps.extend(p.build(ins[i0:i0 + len(p.ins)], outs[o0:o0 + len(p.out_shapes)], send_sems, recv_sems,
                               local_sems, k0, l0))
        return cps


def _call(body, name, grid, in_specs, out_specs, out_shape, scratch_shapes, semantics, args, payload=None):
    if payload is None:
        outs = pl.pallas_call(body, name=name, grid=grid, in_specs=in_specs, out_specs=out_specs,
                              out_shape=out_shape, scratch_shapes=scratch_shapes,
                              compiler_params=_cparams(semantics))(*args)
        return list(outs), []
    n_in, n_out, n_scr = len(in_specs), len(out_specs), len(scratch_shapes)
    p_in, p_out = len(payload.ins), len(payload.out_shapes)

    def carried(*refs):
        ins, p_ins = refs[:n_in], refs[n_in:n_in + p_in]
        o0 = n_in + p_in
        outs, p_outs = refs[o0:o0 + n_out], refs[o0 + n_out:o0 + n_out + p_out]
        s0 = o0 + n_out + p_out
        scr, sems = refs[s0:s0 + n_scr], refs[s0 + n_scr:]
        ids = [pl.program_id(k) for k in range(len(grid))]
        at_first = functools.reduce(jnp.logical_and, [i == 0 for i in ids])
        at_last = functools.reduce(jnp.logical_and, [i == g - 1 for i, g in zip(ids, grid)])

        @pl.when(at_first)
        def _():
            payload.build(p_ins, p_outs, *sems).start()

        body(*ins, *outs, *scr)

        @pl.when(at_last)
        def _():
            payload.build(p_ins, p_outs, *sems).wait()

    outs = pl.pallas_call(
        carried, name=name, grid=grid, in_specs=list(in_specs) + [ANY] * p_in,
        out_specs=list(out_specs) + [ANY] * p_out, out_shape=list(out_shape) + list(payload.out_shapes),
        scratch_shapes=list(scratch_shapes) + payload.sem_shapes(),
        input_output_aliases={n_in + i: n_out + o for i, o in payload.aliases.items()},
        compiler_params=_cparams(("arbitrary",) * len(grid)))(*args, *payload.ins)
    return list(outs[:n_out]), list(outs[n_out:])


def _comm(payload, name):
    def body(*refs):
        p_in, p_out = len(payload.ins), len(payload.out_shapes)
        cps = payload.build(refs[:p_in], refs[p_in:p_in + p_out], *refs[p_in + p_out:])
        cps.start()
        cps.wait()

    return list(pl.pallas_call(
        body, name=name, in_specs=[ANY] * len(payload.ins), out_specs=[ANY] * len(payload.out_shapes),
        out_shape=list(payload.out_shapes), scratch_shapes=payload.sem_shapes(),
        input_output_aliases=dict(payload.aliases))(*payload.ins))


def _dot(a, b):
    return jnp.dot(a, b, preferred_element_type=F32)


def _dot_nt(a, b):
    return lax.dot_general(a, b, (((1,), (1,)), ((), ())), preferred_element_type=F32)


def _dot_tn(a, b):
    return lax.dot_general(a, b, (((0,), (0,)), ((), ())), preferred_element_type=F32)


def _split_dot(x, e):
    hi = x.astype(BF16)
    r1 = x - hi.astype(F32)
    mid = r1.astype(BF16)
    lo = (r1 - mid.astype(F32)).astype(BF16)
    return _dot(hi, e) + _dot(mid, e) + _dot(lo, e)


def _rms(x):
    rstd = lax.rsqrt(jnp.mean(x * x, axis=-1, keepdims=True) + EPS)
    return x * rstd, rstd


def _rms_bwd(xhat, rstd, g, dh):
    dxhat = dh * g
    dx = rstd * (dxhat - xhat * jnp.mean(dxhat * xhat, axis=-1, keepdims=True))
    return dx, jnp.sum(dh * xhat, axis=0, keepdims=True)


def _ln(v):
    mu = jnp.mean(v, axis=-1, keepdims=True)
    xc = v - mu
    rstd = lax.rsqrt(jnp.mean(xc * xc, axis=-1, keepdims=True) + EPS)
    return xc * rstd, rstd


def _ln_bwd(vhat, rstd, g, dy):
    dvhat = dy * g
    dv = rstd * (dvhat - jnp.mean(dvhat, axis=-1, keepdims=True)
                 - vhat * jnp.mean(dvhat * vhat, axis=-1, keepdims=True))
    return dv, jnp.sum(dy * vhat, axis=0, keepdims=True), jnp.sum(dy, axis=0, keepdims=True)


_INV_SQRT2 = 0.7071067811865476
_INV_SQRT2PI = 0.3989422804014327


def _gelu(x):
    return 0.5 * x * (1.0 + lax.erf(x * _INV_SQRT2))


def _gelu_grad(x):
    return 0.5 * (1.0 + lax.erf(x * _INV_SQRT2)) + x * jnp.exp(-0.5 * x * x) * _INV_SQRT2PI


def _silu_grad(x):
    s = jax.nn.sigmoid(x)
    return s * (1.0 + x * (1.0 - s))


def _ffn_fwd(x, g, wa, mi, name, payload=None):
    t, d = x.shape
    f = wa.shape[1]
    tm, tf = 1024, 256
    nc = f // tf
    groups = [slice(k * (tm // 2), (k + 1) * (tm // 2)) for k in range(2)]

    def body(x_ref, g_ref, wg_ref, wu_ref, wd_ref, xo_ref, gate_ref, up_ref, h_scr, acc_scr):
        c = pl.program_id(1)

        @pl.when(c == 0)
        def _():
            xhat, _ = _rms(x_ref[...])
            h_scr[...] = (xhat * g_ref[...]).astype(BF16)
            acc_scr[...] = jnp.zeros_like(acc_scr)

        wg, wu, wd = wg_ref[...], wu_ref[...], wd_ref[...]
        for rows in groups:
            h = h_scr[rows, :]
            gate = _dot_nt(h, wg)
            up = _dot_nt(h, wu)
            gate_ref[rows, :] = gate.astype(BF16)
            up_ref[rows, :] = up.astype(BF16)
            act = (gate * jax.nn.sigmoid(gate) * up).astype(BF16)
            acc_scr[rows, :] += _dot(act, wd)

        @pl.when(c == nc - 1)
        def _():
            xo_ref[...] = x_ref[...] + 0.5 * acc_scr[...]

    def wspec(k):
        return pl.BlockSpec((None, tf, d), lambda i, c: (mi + k, c, 0))

    return _call(
        body, name, (t // tm, nc),
        [pl.BlockSpec((tm, d), lambda i, c: (i, 0)), pl.BlockSpec((1, d), lambda i, c: (0, 0)),
         wspec(0), wspec(1), wspec(2)],
        [pl.BlockSpec((tm, d), lambda i, c: (i, 0)), pl.BlockSpec((tm, tf), lambda i, c: (i, c)),
         pl.BlockSpec((tm, tf), lambda i, c: (i, c))],
        [jax.ShapeDtypeStruct((t, d), F32), jax.ShapeDtypeStruct((t, f), BF16), jax.ShapeDtypeStruct((t, f), BF16)],
        [pltpu.VMEM((tm, d), BF16), pltpu.VMEM((tm, d), F32)],
        ("parallel", "arbitrary"), (x, g, wa, wa, wa), payload)


def _ffn_bwd(x, g, dxo, gate, up, wa, mi, name, payload=None):
    t, d = x.shape
    f = wa.shape[1]
    tm, tf = 1024, 256
    nc = f // tf
    groups = [slice(k * (tm // 2), (k + 1) * (tm // 2)) for k in range(2)]

    def body(x_ref, g_ref, dxo_ref, gate_ref, up_ref, wg_ref, wu_ref, wd_ref,
             dx_ref, dgate_ref, dup_ref, act_ref, h_ref, dy_ref, dg_ref, acc_scr):
        i, c = pl.program_id(0), pl.program_id(1)

        @pl.when(c == 0)
        def _():
            xhat, _ = _rms(x_ref[...])
            h_ref[...] = (xhat * g_ref[...]).astype(BF16)
            dy_ref[...] = (0.5 * dxo_ref[...]).astype(BF16)
            acc_scr[...] = jnp.zeros_like(acc_scr)

        @pl.when((c == 0) & (i == 0))
        def _():
            dg_ref[...] = jnp.zeros_like(dg_ref)

        wg, wu, wd = wg_ref[...], wu_ref[...], wd_ref[...]
        for rows in groups:
            gt = gate_ref[rows, :].astype(F32)
            u = up_ref[rows, :].astype(F32)
            s = jax.nn.sigmoid(gt)
            silu = gt * s
            dact = _dot_nt(dy_ref[rows, :], wd)
            dgate = (dact * u * (s * (1.0 + gt * (1.0 - s)))).astype(BF16)
            dup = (dact * silu).astype(BF16)
            dgate_ref[rows, :] = dgate
            dup_ref[rows, :] = dup
            act_ref[rows, :] = (silu * u).astype(BF16)
            acc_scr[rows, :] += _dot(dgate, wg) + _dot(dup, wu)

        @pl.when(c == nc - 1)
        def _():
            xhat, rstd = _rms(x_ref[...])
            dxn, dg = _rms_bwd(xhat, rstd, g_ref[...], acc_scr[...])
            dx_ref[...] = dxo_ref[...] + dxn
            dg_ref[...] += dg

    def wspec(k):
        return pl.BlockSpec((None, tf, d), lambda i, c: (mi + k, c, 0))

    row = pl.BlockSpec((tm, d), lambda i, c: (i, 0))
    col = pl.BlockSpec((tm, tf), lambda i, c: (i, c))
    vec = pl.BlockSpec((1, d), lambda i, c: (0, 0))
    return _call(
        body, name, (t // tm, nc),
        [row, vec, row, col, col, wspec(0), wspec(1), wspec(2)],
        [row, col, col, col, row, row, vec],
        [jax.ShapeDtypeStruct((t, d), F32), jax.ShapeDtypeStruct((t, f), BF16),
         jax.ShapeDtypeStruct((t, f), BF16), jax.ShapeDtypeStruct((t, f), BF16),
         jax.ShapeDtypeStruct((t, d), BF16), jax.ShapeDtypeStruct((t, d), BF16),
         jax.ShapeDtypeStruct((1, d), F32)],
        [pltpu.VMEM((tm, d), F32)],
        ("arbitrary", "arbitrary"), (x, g, dxo, gate, up, wa, wa, wa), payload)


def _tn_matmul(a, b, name, payload=None):
    t, m = a.shape
    n = b.shape[1]
    tk = 1024
    tmm = m // 2 if (m // 2) % LANES == 0 else m
    nk = t // tk

    def body(a_ref, b_ref, o_ref, acc_scr):
        k = pl.program_id(1)

        @pl.when(k == 0)
        def _():
            acc_scr[...] = jnp.zeros_like(acc_scr)

        acc_scr[...] += _dot_tn(a_ref[...].astype(BF16), b_ref[...].astype(BF16))

        @pl.when(k == nk - 1)
        def _():
            o_ref[...] = acc_scr[...].astype(BF16)

    (out,), p_outs = _call(
        body, name, (m // tmm, nk),
        [pl.BlockSpec((tk, tmm), lambda j, k: (k, j)), pl.BlockSpec((tk, n), lambda j, k: (k, 0))],
        [pl.BlockSpec((tmm, n), lambda j, k: (j, 0))],
        [jax.ShapeDtypeStruct((m, n), BF16)],
        [pltpu.VMEM((tmm, n), F32)],
        ("parallel", "arbitrary"), (a, b), payload)
    return out, p_outs


def _proj_in_fwd(x, g, wb, li, name, payload=None):
    t, d = x.shape
    n = wb.shape[1]
    tm = 512

    def body(x_ref, g_ref, w_ref, z_ref):
        xhat, _ = _rms(x_ref[...])
        z_ref[...] = _dot_nt((xhat * g_ref[...]).astype(BF16), w_ref[...])

    (z,), p_outs = _call(
        body, name, (t // tm,),
        [pl.BlockSpec((tm, d), lambda i: (i, 0)), pl.BlockSpec((1, d), lambda i: (0, 0)),
         pl.BlockSpec((None, n, d), lambda i: (li, 0, 0))],
        [pl.BlockSpec((tm, n), lambda i: (i, 0))],
        [jax.ShapeDtypeStruct((t, n), F32)], [], ("parallel",), (x, g, wb), payload)
    return z, p_outs


def _proj_out_fwd(x, cat, wc, li, name, payload=None):
    t, d = x.shape
    tm = 512

    def body(x_ref, cat_ref, w_ref, xo_ref):
        xo_ref[...] = x_ref[...] + _dot(cat_ref[...], w_ref[...])

    (xo,), p_outs = _call(
        body, name, (t // tm,),
        [pl.BlockSpec((tm, d), lambda i: (i, 0)), pl.BlockSpec((tm, d), lambda i: (i, 0)),
         pl.BlockSpec((None, d, d), lambda i: (li, 0, 0))],
        [pl.BlockSpec((tm, d), lambda i: (i, 0))],
        [jax.ShapeDtypeStruct((t, d), F32)], [], ("parallel",), (x, cat, wc), payload)
    return xo, p_outs


def _proj_out_bwd(dxo, wc, li, name):
    t, d = dxo.shape
    tm = 512

    def body(dxo_ref, w_ref, dcat_ref):
        dcat_ref[...] = _dot_nt(dxo_ref[...].astype(BF16), w_ref[...])

    return pl.pallas_call(
        body, name=name, grid=(t // tm,),
        in_specs=[pl.BlockSpec((tm, d), lambda i: (i, 0)), pl.BlockSpec((None, d, d), lambda i: (li, 0, 0))],
        out_specs=pl.BlockSpec((tm, d), lambda i: (i, 0)),
        out_shape=jax.ShapeDtypeStruct((t, d), F32),
        compiler_params=_cparams(("parallel",)),
    )(dxo, wc)


def _proj_in_bwd(x, g, dxo, dz, wb, li, name):
    t, d = x.shape
    n = wb.shape[1]
    tm = 512

    def body(x_ref, g_ref, dxo_ref, dz_ref, w_ref, dx_ref, h_ref, dg_ref):
        i = pl.program_id(0)

        @pl.when(i == 0)
        def _():
            dg_ref[...] = jnp.zeros_like(dg_ref)

        xhat, rstd = _rms(x_ref[...])
        h_ref[...] = (xhat * g_ref[...]).astype(BF16)
        dh = _dot(dz_ref[...], w_ref[...])
        dxn, dg = _rms_bwd(xhat, rstd, g_ref[...], dh)
        dx_ref[...] = dxo_ref[...] + dxn
        dg_ref[...] += dg

    row = pl.BlockSpec((tm, d), lambda i: (i, 0))
    vec = pl.BlockSpec((1, d), lambda i: (0, 0))
    return pl.pallas_call(
        body, name=name, grid=(t // tm,),
        in_specs=[row, vec, row, pl.BlockSpec((tm, n), lambda i: (i, 0)),
                  pl.BlockSpec((None, n, d), lambda i: (li, 0, 0))],
        out_specs=[row, row, vec],
        out_shape=[jax.ShapeDtypeStruct((t, d), F32), jax.ShapeDtypeStruct((t, d), BF16),
                   jax.ShapeDtypeStruct((1, d), F32)],
        compiler_params=_cparams(("arbitrary",)),
    )(x, g, dxo, dz, wb)


def _lane_ids(shape):
    return lax.broadcasted_iota(jnp.int32, shape, 1)


def _tril(w):
    r = lax.broadcasted_iota(jnp.int32, w.shape, 0)
    c = lax.broadcasted_iota(jnp.int32, w.shape, 1)
    return jnp.where(r >= c, w, 0.0)


def _shift_down(x, k):
    return x if k == 0 else pltpu.roll(x, k, 0)


def _shift_up(x, k):
    return x if k == 0 else pltpu.roll(x, x.shape[0] - k, 0)


def _causal_conv(ext, w, n_out):
    acc = None
    for b in range(8):
        rolled = _shift_down(ext, b)
        for a in range((CONV_WIDTH - 1 - b) // 8 + 1):
            j = 8 * a + b
            term = rolled[HALO - 8 * a:HALO - 8 * a + n_out] * w[CONV_WIDTH - 1 - j:CONV_WIDTH - j]
            acc = term if acc is None else acc + term
    return acc


def _conv_wgrad(ext, dhc, n_out):
    rows = [None] * CONV_WIDTH
    for b in range(8):
        rolled = _shift_down(ext, b)
        for a in range((CONV_WIDTH - 1 - b) // 8 + 1):
            j = 8 * a + b
            rows[CONV_WIDTH - 1 - j] = jnp.sum(
                rolled[HALO - 8 * a:HALO - 8 * a + n_out] * dhc, axis=0, keepdims=True)
    return rows


def _anticausal_conv(ext, w, n_out):
    acc = None
    for b in range(8):
        rolled = _shift_up(ext, b)
        for a in range((CONV_WIDTH - 1 - b) // 8 + 1):
            j = 8 * a + b
            term = rolled[8 * a:8 * a + n_out] * w[CONV_WIDTH - 1 - j:CONV_WIDTH - j]
            acc = term if acc is None else acc + term
    return acc


def _window_sums(ext, shift):
    s2 = ext + shift(ext, 1)
    s4 = s2 + shift(s2, 2)
    s8 = s4 + shift(s4, 4)
    s16 = s8 + shift(s8, 8)
    grp = _lane_ids(ext.shape) // HEAD_DIM
    return jnp.where(grp == 0, s2, jnp.where(grp == 1, s4, jnp.where(grp == 2, s8, s16)))


def _pool_count(t0, n, width):
    pos = (lax.broadcasted_iota(jnp.int32, (n, width), 0) + (t0 + 1)).astype(F32)
    grp = _lane_ids((n, width)) // HEAD_DIM
    win = jnp.where(grp == 0, 2.0, jnp.where(grp == 1, 4.0, jnp.where(grp == 2, 8.0, 16.0)))
    return jnp.minimum(pos, win)


def _block_diag(pw):
    gn, cg, _ = pw.shape
    rows = []
    for gi in range(gn):
        parts = [pw[gi] if gj == gi else jnp.zeros((cg, cg), pw.dtype) for gj in range(gn)]
        rows.append(jnp.concatenate(parts, axis=1))
    return jnp.concatenate(rows, axis=0)


def _head_pair_mix(w_even, w_odd, v):
    lo = _lane_ids((CHUNK, LANES)) < HEAD_DIM
    return jnp.where(lo, _dot(w_even, v), _dot(w_odd, v))


def _mix_fwd(z, p, name, payload=None):
    t, d_in = z.shape
    sgu = p["sgu_ln_g"].shape[1]
    pool = p["pool_scale"].shape[1]
    d_mix = 2 * sgu + pool
    tm = 512
    n_i = t // tm
    hb = tm // HALO

    def body(z_ref, zp_ref, lng_ref, lnb_ref, ws_ref, bs_ref, cw_ref, cb_ref, clg_ref, clb_ref,
             bd_ref, ps_ref, cat_ref):
        i = pl.program_id(0)
        first = i == 0
        z_main = z_ref[...]
        z_prev = jnp.where(first, 0.0, zp_ref[...])

        lng, lnb = lng_ref[...], lnb_ref[...]
        wt = [_tril(ws_ref[h]).astype(BF16) for h in range(sgu // HEAD_DIM)]
        for n in range(tm // CHUNK):
            rows = slice(n * CHUNK, (n + 1) * CHUNK)
            u = _gelu(z_main[rows, 0:sgu])
            vhat, _ = _ln(_gelu(z_main[rows, sgu:2 * sgu]))
            vn = (vhat * lng + lnb).astype(BF16)
            for gp in range(sgu // LANES):
                ls = slice(gp * LANES, (gp + 1) * LANES)
                mixed = _head_pair_mix(wt[2 * gp], wt[2 * gp + 1], vn[:, ls]) + bs_ref[:, ls]
                cat_ref[rows, ls] = (u[:, ls] * mixed).astype(BF16)

        def glu(zz):
            return zz[:, 2 * sgu:3 * sgu] * jax.nn.sigmoid(zz[:, 3 * sgu:4 * sgu])

        ext = jnp.concatenate([glu(z_prev), glu(z_main)], axis=0)
        hc = _causal_conv(ext, cw_ref[...], tm) + cb_ref[...]
        hhat, _ = _ln(hc)
        bn = hhat * clg_ref[...] + clb_ref[...]
        cat_ref[:, sgu:2 * sgu] = (bn * jax.nn.sigmoid(bn)).astype(BF16)

        pext = jnp.concatenate([z_prev[:, 4 * sgu:], z_main[:, 4 * sgu:]], axis=0)
        sums = _window_sums(pext, _shift_down)[HALO:]
        pooled = sums / _pool_count(i * tm, tm, pool) - z_main[:, 4 * sgu:]
        mixed_c = _dot(pooled.astype(BF16), bd_ref[...].astype(BF16))
        cat_ref[:, 2 * sgu:] = (mixed_c * ps_ref[...]).astype(BF16)

    def vec(n):
        return pl.BlockSpec((1, n), lambda i: (0, 0))

    (cat,), p_outs = _call(
        body, name, (n_i,),
        [pl.BlockSpec((tm, d_in), lambda i: (i, 0)),
         pl.BlockSpec((HALO, d_in), lambda i: (jnp.maximum(i * hb - 1, 0), 0)),
         vec(sgu), vec(sgu),
         pl.BlockSpec(p["w_spatial"].shape, lambda i: (0, 0, 0)),
         pl.BlockSpec((CHUNK, sgu), lambda i: (0, 0)),
         pl.BlockSpec((CONV_WIDTH, sgu), lambda i: (0, 0)),
         vec(sgu), vec(sgu), vec(sgu),
         pl.BlockSpec((pool, pool), lambda i: (0, 0)), vec(pool)],
        [pl.BlockSpec((tm, d_mix), lambda i: (i, 0))],
        [jax.ShapeDtypeStruct((t, d_mix), BF16)], [], ("parallel",),
        (z, z, p["sgu_ln_g"], p["sgu_ln_b"], p["w_spatial"], p["bs_full"], p["conv_w"], p["conv_b"],
         p["conv_ln_g"], p["conv_ln_b"], p["bd"], p["pool_scale"]), payload)
    return cat, p_outs


_R_SGU_G, _R_SGU_B, _R_CONV_B, _R_CLN_G, _R_CLN_B, _R_CONV_W = 0, 1, 2, 3, 4, 8
_R384_ROWS = 40


def _mix_bwd(z, dcat, p, name, payload=None):
    t, d_in = z.shape
    sgu = p["sgu_ln_g"].shape[1]
    pool = p["pool_scale"].shape[1]
    d_mix = 2 * sgu + pool
    n_head = sgu // HEAD_DIM
    tm = 512
    n_i = t // tm
    hb = tm // HALO

    def body(z_ref, zp_ref, zn_ref, dc_ref, dcn_ref, lng_ref, lnb_ref, ws_ref, bs_ref, cw_ref, cb_ref,
             clg_ref, clb_ref, bd_ref, ps_ref, dz_ref, g384_ref, gws_ref, gpool_ref, dbs_scr):
        i = pl.program_id(0)
        first, last = i == 0, i == n_i - 1

        @pl.when(first)
        def _():
            g384_ref[...] = jnp.zeros_like(g384_ref)
            gws_ref[...] = jnp.zeros_like(gws_ref)
            gpool_ref[...] = jnp.zeros_like(gpool_ref)
            dbs_scr[...] = jnp.zeros_like(dbs_scr)

        z_main = z_ref[...]
        z_prev = jnp.where(first, 0.0, zp_ref[...])
        z_next = jnp.where(last, 0.0, zn_ref[...])
        dc_main = dc_ref[...]
        dc_next = jnp.where(last, 0.0, dcn_ref[...])

        lng, lnb = lng_ref[...], lnb_ref[...]
        wt = [_tril(ws_ref[h]) for h in range(n_head)]
        wt_b = [w.astype(BF16) for w in wt]
        wtt_b = [w.T.astype(BF16) for w in wt]
        lo = _lane_ids((CHUNK, LANES)) < HEAD_DIM
        d_lng = jnp.zeros((1, sgu), F32)
        d_lnb = jnp.zeros((1, sgu), F32)
        dws = [jnp.zeros((CHUNK, CHUNK), F32) for _ in range(n_head)]
        for n in range(tm // CHUNK):
            rows = slice(n * CHUNK, (n + 1) * CHUNK)
            au, av = z_main[rows, 0:sgu], z_main[rows, sgu:2 * sgu]
            u = _gelu(au)
            vhat, vrstd = _ln(_gelu(av))
            vn = (vhat * lng + lnb).astype(BF16)
            da = dc_main[rows, 0:sgu]
            dmixed = da * u
            dbs_scr[...] += dmixed
            dvn_parts, du_parts = [], []
            for gp in range(sgu // LANES):
                ls = slice(gp * LANES, (gp + 1) * LANES)
                vn_g = vn[:, ls]
                mixed = _head_pair_mix(wt_b[2 * gp], wt_b[2 * gp + 1], vn_g) + bs_ref[:, ls]
                du_parts.append(da[:, ls] * mixed)
                dm_g = dmixed[:, ls]
                dm_b = dm_g.astype(BF16)
                dvn_parts.append(jnp.where(lo, _dot(wtt_b[2 * gp], dm_b), _dot(wtt_b[2 * gp + 1], dm_b)))
                dws[2 * gp] = dws[2 * gp] + _dot_nt(jnp.where(lo, dm_g, 0.0).astype(BF16), vn_g)
                dws[2 * gp + 1] = dws[2 * gp + 1] + _dot_nt(jnp.where(lo, 0.0, dm_g).astype(BF16), vn_g)
            dvn = jnp.concatenate(dvn_parts, axis=1)
            du = jnp.concatenate(du_parts, axis=1)
            dv, dg_n, db_n = _ln_bwd(vhat, vrstd, lng, dvn)
            d_lng = d_lng + dg_n
            d_lnb = d_lnb + db_n
            dz_ref[rows, 0:sgu] = (du * _gelu_grad(au)).astype(BF16)
            dz_ref[rows, sgu:2 * sgu] = (dv * _gelu_grad(av)).astype(BF16)
        for h in range(n_head):
            gws_ref[h] += _tril(dws[h])
        g384_ref[_R_SGU_G:_R_SGU_G + 1, :] += d_lng
        g384_ref[_R_SGU_B:_R_SGU_B + 1, :] += d_lnb

        @pl.when(last)
        def _():
            r = lax.broadcasted_iota(jnp.int32, (sgu, LANES), 0)
            c = lax.broadcasted_iota(jnp.int32, (sgu, LANES), 1)
            sel = (r // HEAD_DIM == c).astype(BF16)
            gws_ref[n_head] = _split_dot(dbs_scr[...], sel)

        cw, clg = cw_ref[...], clg_ref[...]
        bcols = slice(2 * sgu, 4 * sgu)
        zb = jnp.concatenate([z_prev[:, bcols], z_main[:, bcols], z_next[:, bcols]], axis=0)
        bval, bgate = zb[:, 0:sgu], zb[:, sgu:2 * sgu]
        sg = jax.nn.sigmoid(bgate)
        hglu = bval * sg
        n_out = tm + HALO
        hc = _causal_conv(hglu, cw, n_out) + cb_ref[...]
        hhat, hrstd = _ln(hc)
        bn = hhat * clg + clb_ref[...]
        db = jnp.concatenate([dc_main[:, sgu:2 * sgu], dc_next[:, sgu:2 * sgu]], axis=0)
        dbn = db * _silu_grad(bn)
        dhc_all, _, _ = _ln_bwd(hhat, hrstd, clg, dbn)
        dbn_m, hhat_m, dhc = dbn[:tm], hhat[:tm], dhc_all[:tm]
        g384_ref[_R_CLN_G:_R_CLN_G + 1, :] += jnp.sum(dbn_m * hhat_m, axis=0, keepdims=True)
        g384_ref[_R_CLN_B:_R_CLN_B + 1, :] += jnp.sum(dbn_m, axis=0, keepdims=True)
        g384_ref[_R_CONV_B:_R_CONV_B + 1, :] += jnp.sum(dhc, axis=0, keepdims=True)
        wrows = _conv_wgrad(hglu[:HALO + tm], dhc, tm)
        for k in range(CONV_WIDTH):
            g384_ref[_R_CONV_W + k:_R_CONV_W + k + 1, :] += wrows[k]
        dhglu = _anticausal_conv(dhc_all, cw, tm)
        bval_m, sg_m = bval[HALO:HALO + tm], sg[HALO:HALO + tm]
        dz_ref[:, 2 * sgu:3 * sgu] = (dhglu * sg_m).astype(BF16)
        dz_ref[:, 3 * sgu:4 * sgu] = (dhglu * bval_m * sg_m * (1.0 - sg_m)).astype(BF16)

        bd_b = bd_ref[...].astype(BF16)
        ps = ps_ref[...]
        p_main = z_main[:, 4 * sgu:]
        pext = jnp.concatenate([z_prev[:, 4 * sgu:], p_main], axis=0)
        cnt = _pool_count(i * tm, n_out, pool)
        pooled = _window_sums(pext, _shift_down)[HALO:] / cnt[:tm] - p_main
        pooled_b = pooled.astype(BF16)
        dcc = jnp.concatenate([dc_main[:, 2 * sgu:], dc_next[:, 2 * sgu:]], axis=0)
        dmix_c = dcc * ps
        mixed_c = _dot(pooled_b, bd_b)
        grp_r = lax.broadcasted_iota(jnp.int32, (pool, pool), 0) // HEAD_DIM
        grp_c = lax.broadcasted_iota(jnp.int32, (pool, pool), 1) // HEAD_DIM
        gpool_ref[0:pool, :] += jnp.where(grp_r == grp_c, _dot_tn(pooled_b, dmix_c[:tm].astype(BF16)), 0.0)
        gpool_ref[pool:pool + 1, :] += jnp.sum(dcc[:tm] * mixed_c, axis=0, keepdims=True)
        dpooled = _dot_nt(dmix_c.astype(BF16), bd_b)
        q = dpooled / cnt
        dp = _window_sums(q, _shift_up)[:tm] - dpooled[:tm]
        dz_ref[:, 4 * sgu:] = dp.astype(BF16)

    def vec(n):
        return pl.BlockSpec((1, n), lambda i: (0, 0))

    def prev_map(i):
        return (jnp.maximum(i * hb - 1, 0), 0)

    def next_map(i):
        return (jnp.minimum((i + 1) * hb, n_i * hb - 1), 0)

    return _call(
        body, name, (n_i,),
        [pl.BlockSpec((tm, d_in), lambda i: (i, 0)),
         pl.BlockSpec((HALO, d_in), prev_map), pl.BlockSpec((HALO, d_in), next_map),
         pl.BlockSpec((tm, d_mix), lambda i: (i, 0)), pl.BlockSpec((HALO, d_mix), next_map),
         vec(sgu), vec(sgu),
         pl.BlockSpec(p["w_spatial"].shape, lambda i: (0, 0, 0)),
         pl.BlockSpec((CHUNK, sgu), lambda i: (0, 0)),
         pl.BlockSpec((CONV_WIDTH, sgu), lambda i: (0, 0)),
         vec(sgu), vec(sgu), vec(sgu),
         pl.BlockSpec((pool, pool), lambda i: (0, 0)), vec(pool)],
        [pl.BlockSpec((tm, d_in), lambda i: (i, 0)),
         pl.BlockSpec((_R384_ROWS, sgu), lambda i: (0, 0)),
         pl.BlockSpec((n_head + 1, CHUNK, CHUNK), lambda i: (0, 0, 0)),
         pl.BlockSpec((pool + 8, pool), lambda i: (0, 0))],
        [jax.ShapeDtypeStruct((t, d_in), BF16),
         jax.ShapeDtypeStruct((_R384_ROWS, sgu), F32),
         jax.ShapeDtypeStruct((n_head + 1, CHUNK, CHUNK), F32),
         jax.ShapeDtypeStruct((pool + 8, pool), F32)],
        [pltpu.VMEM((CHUNK, sgu), F32)], ("arbitrary",),
        (z, z, z, dcat, dcat, p["sgu_ln_g"], p["sgu_ln_b"], p["w_spatial"], p["bs_full"], p["conv_w"],
         p["conv_b"], p["conv_ln_g"], p["conv_ln_b"], p["bd"], p["pool_scale"]), payload)


def _loss_head(x, g, target, name):
    t, d = x.shape
    tm = 512

    def body(x_ref, g_ref, tgt_ref, dx_ref, dg_ref, loss_ref):
        i = pl.program_id(0)

        @pl.when(i == 0)
        def _():
            dg_ref[...] = jnp.zeros_like(dg_ref)
            loss_ref[...] = jnp.zeros_like(loss_ref)

        gv = g_ref[...]
        xhat, rstd = _rms(x_ref[...])
        err = xhat * gv - tgt_ref[...]
        loss_ref[...] += jnp.zeros_like(loss_ref) + 0.5 * jnp.sum(jnp.mean(err * err, axis=-1, keepdims=True))
        dxn, dg = _rms_bwd(xhat, rstd, gv, err * (1.0 / d))
        dx_ref[...] = dxn
        dg_ref[...] += dg

    row = pl.BlockSpec((tm, d), lambda i: (i, 0))
    vec = pl.BlockSpec((1, d), lambda i: (0, 0))
    return pl.pallas_call(
        body, name=name, grid=(t // tm,),
        in_specs=[row, vec, row],
        out_specs=[row, vec, pl.BlockSpec((1, LANES), lambda i: (0, 0))],
        out_shape=[jax.ShapeDtypeStruct((t, d), F32), jax.ShapeDtypeStruct((1, d), F32),
                   jax.ShapeDtypeStruct((1, LANES), F32)],
        compiler_params=_cparams(("arbitrary",)),
    )(x, g, target)


def _all_gather(arrs, name, extra=None):
    gather = _GatherIci(arrs)
    n = len(arrs)
    forward = _GatherForward([jax.ShapeDtypeStruct(s.shape, s.dtype) for s in gather.out_shapes])
    x_in = len(extra.ins) if extra else 0
    x_out = len(extra.out_shapes) if extra else 0

    def body(*refs):
        ins, x_ins = refs[:n], refs[n:n + x_in]
        outs, x_outs = refs[n + x_in:2 * n + x_in], refs[2 * n + x_in:2 * n + x_in + x_out]
        sems = refs[2 * n + x_in + x_out:]
        first = gather.build(ins, outs, *sems[0:3])
        first.start()
        if extra:
            beside = extra.build(x_ins, x_outs, *sems[6:9])
            beside.start()
        first.wait()
        second = forward.build(outs, outs, *sems[3:6])
        second.start()
        second.wait()
        if extra:
            beside.wait()

    outs = pl.pallas_call(
        body, name=name,
        in_specs=[ANY] * (n + x_in), out_specs=[ANY] * (n + x_out),
        out_shape=list(gather.out_shapes) + (list(extra.out_shapes) if extra else []),
        scratch_shapes=gather.sem_shapes() + forward.sem_shapes() + (extra.sem_shapes() if extra else []),
    )(*arrs, *(extra.ins if extra else []))
    return list(outs[:n]), list(outs[n:])


def _pair_sum(grad, recv, cidx, name):
    r = grad.shape[0] // N_DEV
    cols = grad.shape[1]

    def body(c_ref, g_ref, r_ref, o_ref):
        o_ref[...] = (g_ref[...].astype(F32) + r_ref[...].astype(F32)).astype(BF16)

    return pl.pallas_call(
        body, name=name,
        grid_spec=pltpu.PrefetchScalarGridSpec(
            num_scalar_prefetch=1, grid=(N_CHIP,),
            in_specs=[pl.BlockSpec((r, cols), lambda q, c: (2 * q + c[0], 0)),
                      pl.BlockSpec((r, cols), lambda q, c: (q, 0))],
            out_specs=pl.BlockSpec((r, cols), lambda q, c: (q, 0))),
        out_shape=jax.ShapeDtypeStruct((N_CHIP * r, cols), BF16),
        compiler_params=_cparams(("parallel",)),
    )(cidx, grad, recv)


def _sum_blocks(parts, nblk, name):
    r = parts.shape[0] // nblk
    cols = parts.shape[1]

    def body(p_ref, o_ref):
        acc = p_ref[0:r, :].astype(F32)
        for q in range(1, nblk):
            acc = acc + p_ref[q * r:(q + 1) * r, :].astype(F32)
        o_ref[...] = acc

    return pl.pallas_call(
        body, name=name,
        out_shape=jax.ShapeDtypeStruct((r, cols), F32),
        compiler_params=_cparams(),
    )(parts)


def _adamw_math(w, g, m, v):
    m = ADAM_B1 * m + (1.0 - ADAM_B1) * g
    v = ADAM_B2 * v + (1.0 - ADAM_B2) * (g * g)
    m_hat = m / (1.0 - ADAM_B1 ** ADAM_STEP)
    v_hat = v / (1.0 - ADAM_B2 ** ADAM_STEP)
    delta = -ADAM_LR * (m_hat / (jnp.sqrt(v_hat) + ADAM_EPS) + ADAM_WD * w)
    return delta, m, v


def _finish_sharded(parts, w, m, v, transposed, name):
    depth, rr, cw = w.shape

    def body(*refs):
        p_refs = refs[:depth]
        w_ref, m_ref, v_ref, g_ref, d_ref, mo_ref, vo_ref = refs[depth:]
        l = pl.program_id(0)
        for k in range(depth):
            @pl.when(l == k)
            def _(p_ref=p_refs[k]):
                r = p_ref.shape[0] // N_CHIP
                acc = p_ref[0:r, :].astype(F32)
                for q in range(1, N_CHIP):
                    acc = acc + p_ref[q * r:(q + 1) * r, :].astype(F32)
                g = acc.T if transposed else acc
                g_ref[...] = g
                d_ref[...], mo_ref[...], vo_ref[...] = _adamw_math(w_ref[...], g, m_ref[...], v_ref[...])

    blk = pl.BlockSpec((None, rr, cw), lambda l: (l, 0, 0))
    return pl.pallas_call(
        body, name=name, grid=(depth,),
        in_specs=[pl.BlockSpec(p.shape, lambda l: (0, 0)) for p in parts] + [blk] * 3, out_specs=[blk] * 4,
        out_shape=[jax.ShapeDtypeStruct(w.shape, F32)] * 4,
        compiler_params=_cparams(("arbitrary",)),
    )(*parts, w, m, v)


def _adamw_small(ws, gs, ms, vs, name):
    n = len(ws)

    def body(*refs):
        for k in range(n):
            w_ref, g_ref, m_ref, v_ref = (refs[j * n + k] for j in range(4))
            d_ref, mo_ref, vo_ref = (refs[(4 + j) * n + k] for j in range(3))
            d_ref[...], mo_ref[...], vo_ref[...] = _adamw_math(w_ref[...], g_ref[...], m_ref[...], v_ref[...])

    shapes = [jax.ShapeDtypeStruct(w.shape, F32) for w in ws]
    return pl.pallas_call(
        body, name=name, out_shape=shapes * 3, compiler_params=_cparams(),
    )(*ws, *gs, *ms, *vs)


def kernel(x, ffn1_norm, ffn1_w_gate, ffn1_w_up, ffn1_w_down, mix_norm, w_in, sgu_ln_g, sgu_ln_b, w_spatial, b_spatial, conv_w, conv_b, conv_ln_g, conv_ln_b, pool_w, pool_scale, w_out, ffn2_norm, ffn2_w_gate, ffn2_w_up, ffn2_w_down, final_norm, loss_target, m_ffn1_norm, m_ffn1_w_gate, m_ffn1_w_up, m_ffn1_w_down, m_mix_norm, m_w_in, m_sgu_ln_g, m_sgu_ln_b, m_w_spatial, m_b_spatial, m_conv_w, m_conv_b, m_conv_ln_g, m_conv_ln_b, m_pool_w, m_pool_scale, m_w_out, m_ffn2_norm, m_ffn2_w_gate, m_ffn2_w_up, m_ffn2_w_down, m_final_norm, v_ffn1_norm, v_ffn1_w_gate, v_ffn1_w_up, v_ffn1_w_down, v_mix_norm, v_w_in, v_sgu_ln_g, v_sgu_ln_b, v_w_spatial, v_b_spatial, v_conv_w, v_conv_b, v_conv_ln_g, v_conv_ln_b, v_pool_w, v_pool_scale, v_w_out, v_ffn2_norm, v_ffn2_w_gate, v_ffn2_w_up, v_ffn2_w_down, v_final_norm):
    names = ["ffn1_norm", "ffn1_w_gate", "ffn1_w_up", "ffn1_w_down", "mix_norm", "w_in", "sgu_ln_g", "sgu_ln_b",
             "w_spatial", "b_spatial", "conv_w", "conv_b", "conv_ln_g", "conv_ln_b", "pool_w", "pool_scale",
             "w_out", "ffn2_norm", "ffn2_w_gate", "ffn2_w_up", "ffn2_w_down", "final_norm"]
    W = dict(zip(names, [ffn1_norm, ffn1_w_gate, ffn1_w_up, ffn1_w_down, mix_norm, w_in, sgu_ln_g, sgu_ln_b,
                         w_spatial, b_spatial, conv_w, conv_b, conv_ln_g, conv_ln_b, pool_w, pool_scale, w_out,
                         ffn2_norm, ffn2_w_gate, ffn2_w_up, ffn2_w_down, final_norm]))
    M = dict(zip(names, [m_ffn1_norm, m_ffn1_w_gate, m_ffn1_w_up, m_ffn1_w_down, m_mix_norm, m_w_in, m_sgu_ln_g,
                         m_sgu_ln_b, m_w_spatial, m_b_spatial, m_conv_w, m_conv_b, m_conv_ln_g, m_conv_ln_b,
                         m_pool_w, m_pool_scale, m_w_out, m_ffn2_norm, m_ffn2_w_gate, m_ffn2_w_up, m_ffn2_w_down,
                         m_final_norm]))
    V = dict(zip(names, [v_ffn1_norm, v_ffn1_w_gate, v_ffn1_w_up, v_ffn1_w_down, v_mix_norm, v_w_in, v_sgu_ln_g,
                         v_sgu_ln_b, v_w_spatial, v_b_spatial, v_conv_w, v_conv_b, v_conv_ln_g, v_conv_ln_b,
                         v_pool_w, v_pool_scale, v_w_out, v_ffn2_norm, v_ffn2_w_gate, v_ffn2_w_up, v_ffn2_w_down,
                         v_final_norm]))

    depth, d = ffn1_norm.shape
    t = x.shape[1]
    sgu = sgu_ln_g.shape[1]
    pool = pool_scale.shape[1]
    n_head = sgu // HEAD_DIM
    cw_shard = conv_w.shape[2]
    xs = x.reshape(t, d)
    target = loss_target.reshape(t, d)

    def tr(w):
        return jnp.swapaxes(w, 1, 2).astype(BF16)

    ffn_shards = [[jnp.stack([tr(ffn1_w_gate)[l], tr(ffn1_w_up)[l], ffn1_w_down[l].astype(BF16)]),
                   jnp.stack([tr(ffn2_w_gate)[l], tr(ffn2_w_up)[l], ffn2_w_down[l].astype(BF16)])]
                  for l in range(depth)]
    win_shards = [tr(w_in)[l:l + 1] for l in range(depth)]
    wout_shards = [w_out[l:l + 1].astype(BF16) for l in range(depth)]
    cw_rows = depth * CONV_WIDTH
    cw_pad = -cw_rows % 8
    cw_send = jnp.pad(conv_w.reshape(cw_rows, cw_shard), ((0, cw_pad), (0, 0)))[None]
    wffn, wb, wc = {}, {}, {}
    (wffn[(0, 0)], wb[0], wc[0], cwg), _ = _all_gather(
        [ffn_shards[0][0], win_shards[0], wout_shards[0], cw_send], "ag_first")
    conv_w_full = cwg.reshape(N_DEV, cw_rows + cw_pad, cw_shard)[:, :cw_rows].reshape(
        N_DEV, depth, CONV_WIDTH, cw_shard).transpose(1, 2, 0, 3).reshape(depth, CONV_WIDTH, N_DEV * cw_shard)

    def mixer_params(l):
        return dict(
            sgu_ln_g=sgu_ln_g[l:l + 1], sgu_ln_b=sgu_ln_b[l:l + 1], w_spatial=w_spatial[l],
            bs_full=jnp.repeat(b_spatial[l].T, HEAD_DIM, axis=1),
            conv_w=conv_w_full[l], conv_b=conv_b[l:l + 1], conv_ln_g=conv_ln_g[l:l + 1],
            conv_ln_b=conv_ln_b[l:l + 1], bd=_block_diag(pool_w[l]), pool_scale=pool_scale[l:l + 1])

    saved = []
    cur = xs
    for l in range(depth):
        p = mixer_params(l)
        x0 = cur
        more = l + 1 < depth
        (x1, gate1, up1), part = _ffn_fwd(x0, ffn1_norm[l:l + 1], wffn[(l, 0)], 0, f"ffn1_fwd_{l}",
                                          _GatherIci([ffn_shards[l][1]]))
        z, (wffn[(l, 1)],) = _proj_in_fwd(x1, mix_norm[l:l + 1], wb[l], 0, f"proj_in_fwd_{l}", _GatherForward(part))
        cat, part = _mix_fwd(z, p, f"mix_fwd_{l}",
                             _GatherIci([win_shards[l + 1], wout_shards[l + 1]]) if more else None)
        x2, part = _proj_out_fwd(x1, cat, wc[l], 0, f"proj_out_fwd_{l}", _GatherForward(part) if more else None)
        if more:
            wb[l + 1], wc[l + 1] = part
        (x3, gate2, up2), part = _ffn_fwd(x2, ffn2_norm[l:l + 1], wffn[(l, 1)], 0, f"ffn2_fwd_{l}",
                                          _GatherIci([ffn_shards[l + 1][0]]) if more else None)
        if more:
            (wffn[(l + 1, 0)],) = _comm(_GatherForward(part), f"ag_forward_{l + 1}")
        saved.append((p, x0, gate1, up1, x1, z, cat, x2, gate2, up2))
        cur = x3

    dx, d_final, loss_part = _loss_head(cur, final_norm.reshape(1, d), target, "loss_head")

    cidx = lax.axis_index("c").astype(jnp.int32).reshape(1)
    from_chips = {}
    to_pair, to_chip = [], []
    small = []

    def pair_payload():
        return _PairExchange([g for _, g in to_pair]) if to_pair else None

    def pair_done(received):
        for ((nm, l), g), r in zip(to_pair, received):
            to_chip.append(((nm, l), _pair_sum(g, r, cidx, f"rs_pair_sum_{nm}_{l}")))
        to_pair.clear()

    def take_chip():
        items = list(to_chip)
        to_chip.clear()
        return items

    def chip_payload(items):
        return _ChipExchange([s for _, s in items]) if items else None

    def chip_done(items, landed):
        for (key, _), o in zip(items, landed):
            from_chips[key] = o

    def ffn_weight_grads(prefix, l, dgate, dup, act, h, dy):
        items = take_chip()
        for k, (nm, a, b) in enumerate([("w_gate", dgate, h), ("w_up", dup, h), ("w_down", act, dy)]):
            mine = items[k::3]
            g, landed = _tn_matmul(a, b, f"dw_{prefix}_{nm}_{l}", chip_payload(mine))
            chip_done(mine, landed)
            to_pair.append(((f"{prefix}_{nm}", l), g))

    for l in reversed(range(depth)):
        p, x0, gate1, up1, x1, z, cat, x2, gate2, up2 = saved[l]
        (dx, dgate, dup, act, h, dy, dg_ffn2), received = _ffn_bwd(
            x2, ffn2_norm[l:l + 1], dx, gate2, up2, wffn[(l, 1)], 0, f"ffn2_bwd_{l}", pair_payload())
        pair_done(received)
        ffn_weight_grads("ffn2", l, dgate, dup, act, h, dy)
        g_out, received = _tn_matmul(cat, dx, f"dw_w_out_{l}", pair_payload())
        pair_done(received)
        dcat = _proj_out_bwd(dx, wc[l], 0, f"proj_out_bwd_{l}")
        items = take_chip()
        (dz, g384, gws, gpool), landed = _mix_bwd(z, dcat, p, f"mix_bwd_{l}", chip_payload(items))
        chip_done(items, landed)
        dx, hm, dg_mix = _proj_in_bwd(x1, mix_norm[l:l + 1], dx, dz, wb[l], 0, f"proj_in_bwd_{l}")
        g_in, _ = _tn_matmul(dz, hm, f"dw_w_in_{l}")
        to_pair.extend([(("w_out", l), g_out), (("w_in", l), g_in)])
        if l > 0:
            (dx, dgate, dup, act, h, dy, dg_ffn1), received = _ffn_bwd(
                x0, ffn1_norm[l:l + 1], dx, gate1, up1, wffn[(l, 0)], 0, f"ffn1_bwd_{l}", pair_payload())
            pair_done(received)
            ffn_weight_grads("ffn1", l, dgate, dup, act, h, dy)
            small.append((l, g384, gws, gpool, dg_ffn1, dg_mix, dg_ffn2))
            continue

        small.append((0, g384, gws, gpool, None, dg_mix, dg_ffn2))
        small.sort(key=lambda s: s[0])
        norm_rows = []
        for (sl, _, _, _, dg1, dgm, dg2) in small:
            norm_rows += [jnp.zeros((1, d), F32) if dg1 is None else dg1, dgm, dg2]
        norm_rows += [d_final, jnp.pad(loss_part, ((0, 0), (0, d - LANES)))]
        n_norm = len(norm_rows)
        norm_pack = jnp.concatenate(norm_rows + [jnp.zeros((8 - n_norm % 8, d), F32)] * (n_norm % 8 != 0), axis=0)
        parts = [norm_pack]
        for (_, s384, sws, spool, _, _, _) in small:
            parts += [s384, sws.reshape((n_head + 1) * CHUNK, CHUNK), spool]
        n_pair = len(to_pair)
        (dx, dgate, dup, act, h, dy, dg_ffn1), landed = _ffn_bwd(
            x0, ffn1_norm[l:l + 1], dx, gate1, up1, wffn[(l, 0)], 0, f"ffn1_bwd_{l}",
            _Merged([pair_payload(), _GatherIci([a[None] for a in parts])]))
        pair_done(landed[:n_pair])
        items = take_chip()
        g_gate, landed = _tn_matmul(dgate, h, f"dw_ffn1_w_gate_{l}",
                                    _Merged([chip_payload(items), _GatherForward(landed[n_pair:])]))
        chip_done(items, landed[:len(items)])
        gathered = landed[len(items):]
        to_pair.append((("ffn1_w_gate", l), g_gate))
        g_up, received = _tn_matmul(dup, h, f"dw_ffn1_w_up_{l}", pair_payload())
        pair_done(received)
        to_pair.append((("ffn1_w_up", l), g_up))
        items = take_chip()
        g_down, landed = _tn_matmul(act, dy, f"dw_ffn1_w_down_{l}", _Merged([chip_payload(items), pair_payload()]))
        chip_done(items, landed[:len(items)])
        pair_done(landed[len(items):])
        to_pair.append((("ffn1_w_down", l), g_down))
    grad_x = dx.reshape(x.shape)
    pair_done(_comm(pair_payload(), "rs_pair_exchange_last"))
    items = take_chip()
    (late_norm,), landed = _all_gather([jnp.pad(dg_ffn1, ((0, 7), (0, 0)))[None]], "ag_tail", chip_payload(items))
    chip_done(items, landed)

    summed = [_sum_blocks(g[0], N_DEV, f"sum_small_{k}") for k, g in enumerate(gathered)]
    late_sum = _sum_blocks(late_norm[0], N_DEV, "sum_small_late")
    norm_sum = summed[0]
    loss = norm_sum[3 * depth + 1, 0]
    cpos = lax.axis_index("x") * 4 + lax.axis_index("y") * 2 + lax.axis_index("c")
    sg = {nm: [] for nm in names}
    for l in range(depth):
        g384, gws, gpool = summed[1 + 3 * l], summed[2 + 3 * l].reshape(n_head + 1, CHUNK, CHUNK), summed[3 + 3 * l]
        sg["ffn1_norm"].append(norm_sum[3 * l] if l > 0 else late_sum[0])
        sg["mix_norm"].append(norm_sum[3 * l + 1])
        sg["ffn2_norm"].append(norm_sum[3 * l + 2])
        sg["sgu_ln_g"].append(g384[_R_SGU_G])
        sg["sgu_ln_b"].append(g384[_R_SGU_B])
        sg["conv_b"].append(g384[_R_CONV_B])
        sg["conv_ln_g"].append(g384[_R_CLN_G])
        sg["conv_ln_b"].append(g384[_R_CLN_B])
        sg["conv_w"].append(lax.dynamic_slice_in_dim(g384[_R_CONV_W:_R_CONV_W + CONV_WIDTH], cpos * cw_shard,
                                                     cw_shard, axis=1))
        sg["w_spatial"].append(gws[:n_head])
        sg["b_spatial"].append(gws[n_head][:, :n_head].T)
        sg["pool_w"].append(jnp.stack([gpool[k * HEAD_DIM:(k + 1) * HEAD_DIM, k * HEAD_DIM:(k + 1) * HEAD_DIM]
                                       for k in range(pool // HEAD_DIM)], axis=0))
        sg["pool_scale"].append(gpool[pool])
    small_names = ["ffn1_norm", "mix_norm", "sgu_ln_g", "sgu_ln_b", "w_spatial", "b_spatial", "conv_w", "conv_b",
                   "conv_ln_g", "conv_ln_b", "pool_w", "pool_scale", "ffn2_norm"]
    grads = {nm: jnp.stack(sg[nm], axis=0) for nm in small_names}
    grads["final_norm"] = norm_sum[3 * depth]

    delta, new_m, new_v = {}, {}, {}
    big_names = ["ffn1_w_gate", "ffn1_w_up", "ffn1_w_down", "w_in", "w_out", "ffn2_w_gate", "ffn2_w_up",
                 "ffn2_w_down"]
    transposed = {"ffn1_w_gate", "ffn1_w_up", "w_in", "ffn2_w_gate", "ffn2_w_up"}
    for nm in big_names:
        grads[nm], delta[nm], new_m[nm], new_v[nm] = _finish_sharded(
            [from_chips[(nm, l)] for l in range(depth)], W[nm], M[nm], V[nm], nm in transposed, f"adamw_{nm}")
    snames = small_names + ["final_norm"]

    def flat2(a):
        return a.reshape(-1, a.shape[-1])

    outs = _adamw_small([flat2(W[nm]) for nm in snames], [flat2(grads[nm]) for nm in snames],
                        [flat2(M[nm]) for nm in snames], [flat2(V[nm]) for nm in snames], "adamw_small")
    ns = len(snames)
    for k, nm in enumerate(snames):
        shp = W[nm].shape
        delta[nm], new_m[nm], new_v[nm] = (outs[k].reshape(shp), outs[ns + k].reshape(shp),
                                           outs[2 * ns + k].reshape(shp))

    return (loss, grad_x, *[grads[nm] for nm in names], *[delta[nm] for nm in names],
            *[new_m[nm] for nm in names], *[new_v[nm] for nm in names])
```

```python
import functools

import jax
import jax.numpy as jnp
from jax import lax
from jax.experimental import pallas as pl
from jax.experimental.pallas import tpu as pltpu

F32 = jnp.float32
BF16 = jnp.bfloat16
EPS = 1e-6
N_DEV = 8
N_CHIP = 4
MESH = pl.DeviceIdType.MESH
ANY = pl.BlockSpec(memory_space=pl.ANY)

VMEM_LIMIT_BYTES = 56 * 1024 * 1024
LANES = 128
HALO = 32
HEAD_DIM = 64
CHUNK = 128
CONV_WIDTH = 31
POOL_WINDOWS = (2, 4, 8, 16)

ADAM_LR = 0.001
ADAM_B1 = 0.9
ADAM_B2 = 0.999
ADAM_EPS = 1e-08
ADAM_WD = 0.01
ADAM_STEP = 10


def _cparams(sem=None):
    return pltpu.CompilerParams(dimension_semantics=sem, vmem_limit_bytes=VMEM_LIMIT_BYTES)


def _position():
    return lax.axis_index("x"), lax.axis_index("y"), lax.axis_index("c")


class _Copies:
    def __init__(self):
        self.local, self.sends, self.recvs = [], [], []

    def extend(self, other):
        self.local += other.local
        self.sends += other.sends
        self.recvs += other.recvs

    def start(self):
        for cp in self.local + self.sends:
            cp.start()

    def wait(self):
        for land, send_sems, recv_sems, k, peer in self.recvs:
            _remote(land, land, send_sems, recv_sems, k, peer).wait_recv()
        for cp in self.sends:
            cp.wait_send()
        for cp in self.local:
            cp.wait()


def _remote(src, dst, send_sems, recv_sems, k, to):
    return pltpu.make_async_remote_copy(src_ref=src, dst_ref=dst, send_sem=send_sems.at[k], recv_sem=recv_sems.at[k],
                                        device_id=to, device_id_type=MESH)


class _Payload:
    ins, out_shapes, aliases, n_remote, n_local = (), (), {}, 0, 0

    def sem_shapes(self):
        return [pltpu.SemaphoreType.DMA((max(self.n_remote, 1),)), pltpu.SemaphoreType.DMA((max(self.n_remote, 1),)),
                pltpu.SemaphoreType.DMA((max(self.n_local, 1),))]


class _GatherIci(_Payload):
    def __init__(self, shards):
        self.ins = list(shards)
        self.out_shapes = [jax.ShapeDtypeStruct((s.shape[0], N_DEV * s.shape[1], s.shape[2]), s.dtype) for s in shards]
        self.n_remote, self.n_local = 4 * len(shards), len(shards)

    def build(self, ins, outs, send_sems, recv_sems, local_sems, k0=0, l0=0):
        x, y, c = _position()
        peers = [(x, y, 1 - c), (1 - x, y, c), (x, 1 - y, c), (1 - x, 1 - y, c)]
        cps = _Copies()
        for a, (src, out) in enumerate(zip(ins, outs)):
            r = src.shape[1]

            def rows(px, py, pc, out=out, r=r):
                return out.at[:, pl.ds((4 * px + 2 * py + pc) * r, r), :]

            cps.local.append(pltpu.make_async_copy(src, rows(x, y, c), local_sems.at[l0 + a]))
            for k, peer in enumerate(peers):
                cps.sends.append(_remote(src, rows(x, y, c), send_sems, recv_sems, k0 + 4 * a + k, peer))
                cps.recvs.append((rows(*peer), send_sems, recv_sems, k0 + 4 * a + k, peer))
        return cps


class _GatherForward(_Payload):
    def __init__(self, partials):
        self.ins = list(partials)
        self.out_shapes = [jax.ShapeDtypeStruct(p.shape, p.dtype) for p in partials]
        self.aliases = {a: a for a in range(len(partials))}
        self.n_remote = 3 * len(partials)

    def build(self, ins, outs, send_sems, recv_sems, local_sems, k0=0, l0=0):
        x, y, c = _position()
        chips = [(1 - x, y), (x, 1 - y), (1 - x, 1 - y)]
        cps = _Copies()
        for a, out in enumerate(outs):
            r = out.shape[1] // N_DEV
            for k, (px, py) in enumerate(chips):
                mine = out.at[:, pl.ds((4 * px + 2 * py + c) * r, r), :]
                theirs = out.at[:, pl.ds((4 * px + 2 * py + 1 - c) * r, r), :]
                cps.sends.append(_remote(mine, mine, send_sems, recv_sems, k0 + 3 * a + k, (x, y, 1 - c)))
                cps.recvs.append((theirs, send_sems, recv_sems, k0 + 3 * a + k, (x, y, 1 - c)))
        return cps


class _PairExchange(_Payload):
    def __init__(self, grads):
        self.ins = list(grads)
        self.out_shapes = [jax.ShapeDtypeStruct((g.shape[0] // 2, g.shape[1]), g.dtype) for g in grads]
        self.n_remote = N_CHIP * len(grads)

    def build(self, ins, outs, send_sems, recv_sems, local_sems, k0=0, l0=0):
        x, y, c = _position()
        cps = _Copies()
        for a, (src, out) in enumerate(zip(ins, outs)):
            r = src.shape[0] // N_DEV
            for q in range(N_CHIP):
                land = out.at[pl.ds(q * r, r), :]
                cps.sends.append(_remote(src.at[pl.ds((2 * q + 1 - c) * r, r), :], land, send_sems, recv_sems,
                                         k0 + N_CHIP * a + q, (x, y, 1 - c)))
                cps.recvs.append((land, send_sems, recv_sems, k0 + N_CHIP * a + q, (x, y, 1 - c)))
        return cps


class _ChipExchange(_Payload):
    def __init__(self, sums):
        self.ins = list(sums)
        self.out_shapes = [jax.ShapeDtypeStruct(s.shape, s.dtype) for s in sums]
        self.n_remote, self.n_local = 3 * len(sums), len(sums)

    def build(self, ins, outs, send_sems, recv_sems, local_sems, k0=0, l0=0):
        x, y, c = _position()
        my_chip = 2 * x + y
        chips = [(1 - x, y), (x, 1 - y), (1 - x, 1 - y)]
        cps = _Copies()
        for a, (src, out) in enumerate(zip(ins, outs)):
            r = src.shape[0] // N_CHIP
            mine = out.at[pl.ds(my_chip * r, r), :]
            cps.local.append(pltpu.make_async_copy(src.at[pl.ds(my_chip * r, r), :], mine, local_sems.at[l0 + a]))
            for k, (px, py) in enumerate(chips):
                land = out.at[pl.ds((2 * px + py) * r, r), :]
                cps.sends.append(_remote(src.at[pl.ds((2 * px + py) * r, r), :], mine, send_sems, recv_sems,
                                         k0 + 3 * a + k, (px, py, c)))
                cps.recvs.append((land, send_sems, recv_sems, k0 + 3 * a + k, (px, py, c)))
        return cps


class _Merged(_Payload):
    def __init__(self, parts):
        self.parts = list(parts)
        self.ins = [a for p in parts for a in p.ins]
        self.out_shapes = [s for p in parts for s in p.out_shapes]
        self.aliases, self.offsets = {}, []
        i0 = o0 = k0 = l0 = 0
        for p in parts:
            self.offsets.append((i0, o0, k0, l0))
            self.aliases.update({i0 + i: o0 + o for i, o in p.aliases.items()})
            i0, o0, k0, l0 = i0 + len(p.ins), o0 + len(p.out_shapes), k0 + p.n_remote, l0 + p.n_local
        self.n_remote, self.n_local = k0, l0

    def build(self, ins, outs, send_sems, recv_sems, local_sems):
        cps = _Copies()
        for p, (i0, o0, k0, l0) in zip(self.parts, self.offsets):
            cps.extend(p.build(ins[i0:i0 + len(p.ins)], outs[o0:o0 + len(p.out_shapes)], send_sems, recv_sems,
                               local_sems, k0, l0))
        return cps


def _call(body, name, grid, in_specs, out_specs, out_shape, scratch_shapes, semantics, args, payload=None):
    if payload is None:
        outs = pl.pallas_call(body, name=name, grid=grid, in_specs=in_specs, out_specs=out_specs,
                              out_shape=out_shape, scratch_shapes=scratch_shapes,
                              compiler_params=_cparams(semantics))(*args)
        return list(outs), []
    n_in, n_out, n_scr = len(in_specs), len(out_specs), len(scratch_shapes)
    p_in, p_out = len(payload.ins), len(payload.out_shapes)

    def carried(*refs):
        ins, p_ins = refs[:n_in], refs[n_in:n_in + p_in]
        o0 = n_in + p_in
        outs, p_outs = refs[o0:o0 + n_out], refs[o0 + n_out:o0 + n_out + p_out]
        s0 = o0 + n_out + p_out
        scr, sems = refs[s0:s0 + n_scr], refs[s0 + n_scr:]
        ids = [pl.program_id(k) for k in range(len(grid))]
        at_first = functools.reduce(jnp.logical_and, [i == 0 for i in ids])
        at_last = functools.reduce(jnp.logical_and, [i == g - 1 for i, g in zip(ids, grid)])

        @pl.when(at_first)
        def _():
            payload.build(p_ins, p_outs, *sems).start()

        body(*ins, *outs, *scr)

        @pl.when(at_last)
        def _():
            payload.build(p_ins, p_outs, *sems).wait()

    outs = pl.pallas_call(
        carried, name=name, grid=grid, in_specs=list(in_specs) + [ANY] * p_in,
        out_specs=list(out_specs) + [ANY] * p_out, out_shape=list(out_shape) + list(payload.out_shapes),
        scratch_shapes=list(scratch_shapes) + payload.sem_shapes(),
        input_output_aliases={n_in + i: n_out + o for i, o in payload.aliases.items()},
        compiler_params=_cparams(("arbitrary",) * len(grid)))(*args, *payload.ins)
    return list(outs[:n_out]), list(outs[n_out:])


def _comm(payload, name):
    def body(*refs):
        p_in, p_out = len(payload.ins), len(payload.out_shapes)
        cps = payload.build(refs[:p_in], refs[p_in:p_in + p_out], *refs[p_in + p_out:])
        cps.start()
        cps.wait()

    return list(pl.pallas_call(
        body, name=name, in_specs=[ANY] * len(payload.ins), out_specs=[ANY] * len(payload.out_shapes),
        out_shape=list(payload.out_shapes), scratch_shapes=payload.sem_shapes(),
        input_output_aliases=dict(payload.aliases))(*payload.ins))


def _dot(a, b):
    return jnp.dot(a, b, preferred_element_type=F32)


def _dot_nt(a, b):
    return lax.dot_general(a, b, (((1,), (1,)), ((), ())), preferred_element_type=F32)


def _dot_tn(a, b):
    return lax.dot_general(a, b, (((0,), (0,)), ((), ())), preferred_element_type=F32)


def _split_dot(x, e):
    hi = x.astype(BF16)
    r1 = x - hi.astype(F32)
    mid = r1.astype(BF16)
    lo = (r1 - mid.astype(F32)).astype(BF16)
    return _dot(hi, e) + _dot(mid, e) + _dot(lo, e)


def _rms(x):
    rstd = lax.rsqrt(jnp.mean(x * x, axis=-1, keepdims=True) + EPS)
    return x * rstd, rstd


def _rms_bwd(xhat, rstd, g, dh):
    dxhat = dh * g
    dx = rstd * (dxhat - xhat * jnp.mean(dxhat * xhat, axis=-1, keepdims=True))
    return dx, jnp.sum(dh * xhat, axis=0, keepdims=True)


def _ln(v):
    mu = jnp.mean(v, axis=-1, keepdims=True)
    xc = v - mu
    rstd = lax.rsqrt(jnp.mean(xc * xc, axis=-1, keepdims=True) + EPS)
    return xc * rstd, rstd


def _ln_bwd(vhat, rstd, g, dy):
    dvhat = dy * g
    dv = rstd * (dvhat - jnp.mean(dvhat, axis=-1, keepdims=True)
                 - vhat * jnp.mean(dvhat * vhat, axis=-1, keepdims=True))
    return dv, jnp.sum(dy * vhat, axis=0, keepdims=True), jnp.sum(dy, axis=0, keepdims=True)


_INV_SQRT2 = 0.7071067811865476
_INV_SQRT2PI = 0.3989422804014327


def _gelu(x):
    return 0.5 * x * (1.0 + lax.erf(x * _INV_SQRT2))


def _gelu_grad(x):
    return 0.5 * (1.0 + lax.erf(x * _INV_SQRT2)) + x * jnp.exp(-0.5 * x * x) * _INV_SQRT2PI


def _silu_grad(x):
    s = jax.nn.sigmoid(x)
    return s * (1.0 + x * (1.0 - s))


def _ffn_fwd(x, g, wa, mi, name, payload=None):
    t, d = x.shape
    f = wa.shape[1]
    tm, tf = 1024, 256
    nc = f // tf
    groups = [slice(k * (tm // 2), (k + 1) * (tm // 2)) for k in range(2)]

    def body(x_ref, g_ref, wg_ref, wu_ref, wd_ref, xo_ref, gate_ref, up_ref, h_scr, acc_scr):
        c = pl.program_id(1)

        @pl.when(c == 0)
        def _():
            xhat, _ = _rms(x_ref[...])
            h_scr[...] = (xhat * g_ref[...]).astype(BF16)
            acc_scr[...] = jnp.zeros_like(acc_scr)

        wg, wu, wd = wg_ref[...], wu_ref[...], wd_ref[...]
        for rows in groups:
            h = h_scr[rows, :]
            gate = _dot_nt(h, wg)
            up = _dot_nt(h, wu)
            gate_ref[rows, :] = gate.astype(BF16)
            up_ref[rows, :] = up.astype(BF16)
            act = (gate * jax.nn.sigmoid(gate) * up).astype(BF16)
            acc_scr[rows, :] += _dot(act, wd)

        @pl.when(c == nc - 1)
        def _():
            xo_ref[...] = x_ref[...] + 0.5 * acc_scr[...]

    def wspec(k):
        return pl.BlockSpec((None, tf, d), lambda i, c: (mi + k, c, 0))

    return _call(
        body, name, (t // tm, nc),
        [pl.BlockSpec((tm, d), lambda i, c: (i, 0)), pl.BlockSpec((1, d), lambda i, c: (0, 0)),
         wspec(0), wspec(1), wspec(2)],
        [pl.BlockSpec((tm, d), lambda i, c: (i, 0)), pl.BlockSpec((tm, tf), lambda i, c: (i, c)),
         pl.BlockSpec((tm, tf), lambda i, c: (i, c))],
        [jax.ShapeDtypeStruct((t, d), F32), jax.ShapeDtypeStruct((t, f), BF16), jax.ShapeDtypeStruct((t, f), BF16)],
        [pltpu.VMEM((tm, d), BF16), pltpu.VMEM((tm, d), F32)],
        ("parallel", "arbitrary"), (x, g, wa, wa, wa), payload)


def _ffn_bwd(x, g, dxo, gate, up, wa, mi, name, payload=None):
    t, d = x.shape
    f = wa.shape[1]
    tm, tf = 1024, 256
    nc = f // tf
    groups = [slice(k * (tm // 2), (k + 1) * (tm // 2)) for k in range(2)]

    def body(x_ref, g_ref, dxo_ref, gate_ref, up_ref, wg_ref, wu_ref, wd_ref,
             dx_ref, dgate_ref, dup_ref, act_ref, h_ref, dy_ref, dg_ref, acc_scr):
        i, c = pl.program_id(0), pl.program_id(1)

        @pl.when(c == 0)
        def _():
            xhat, _ = _rms(x_ref[...])
            h_ref[...] = (xhat * g_ref[...]).astype(BF16)
            dy_ref[...] = (0.5 * dxo_ref[...]).astype(BF16)
            acc_scr[...] = jnp.zeros_like(acc_scr)

        @pl.when((c == 0) & (i == 0))
        def _():
            dg_ref[...] = jnp.zeros_like(dg_ref)

        wg, wu, wd = wg_ref[...], wu_ref[...], wd_ref[...]
        for rows in groups:
            gt = gate_ref[rows, :].astype(F32)
            u = up_ref[rows, :].astype(F32)
            s = jax.nn.sigmoid(gt)
            silu = gt * s
            dact = _dot_nt(dy_ref[rows, :], wd)
            dgate = (dact * u * (s * (1.0 + gt * (1.0 - s)))).astype(BF16)
            dup = (dact * silu).astype(BF16)
            dgate_ref[rows, :] = dgate
            dup_ref[rows, :] = dup
            act_ref[rows, :] = (silu * u).astype(BF16)
            acc_scr[rows, :] += _dot(dgate, wg) + _dot(dup, wu)

        @pl.when(c == nc - 1)
        def _():
            xhat, rstd = _rms(x_ref[...])
            dxn, dg = _rms_bwd(xhat, rstd, g_ref[...], acc_scr[...])
            dx_ref[...] = dxo_ref[...] + dxn
            dg_ref[...] += dg

    def wspec(k):
        return pl.BlockSpec((None, tf, d), lambda i, c: (mi + k, c, 0))

    row = pl.BlockSpec((tm, d), lambda i, c: (i, 0))
    col = pl.BlockSpec((tm, tf), lambda i, c: (i, c))
    vec = pl.BlockSpec((1, d), lambda i, c: (0, 0))
    return _call(
        body, name, (t // tm, nc),
        [row, vec, row, col, col, wspec(0), wspec(1), wspec(2)],
        [row, col, col, col, row, row, vec],
        [jax.ShapeDtypeStruct((t, d), F32), jax.ShapeDtypeStruct((t, f), BF16),
         jax.ShapeDtypeStruct((t, f), BF16), jax.ShapeDtypeStruct((t, f), BF16),
         jax.ShapeDtypeStruct((t, d), BF16), jax.ShapeDtypeStruct((t, d), BF16),
         jax.ShapeDtypeStruct((1, d), F32)],
        [pltpu.VMEM((tm, d), F32)],
        ("arbitrary", "arbitrary"), (x, g, dxo, gate, up, wa, wa, wa), payload)


def _tn_matmul(a, b, name, payload=None):
    t, m = a.shape
    n = b.shape[1]
    tk = 1024
    tmm = m // 2 if (m // 2) % LANES == 0 else m
    nk = t // tk

    def body(a_ref, b_ref, o_ref, acc_scr):
        k = pl.program_id(1)

        @pl.when(k == 0)
        def _():
            acc_scr[...] = jnp.zeros_like(acc_scr)

        acc_scr[...] += _dot_tn(a_ref[...].astype(BF16), b_ref[...].astype(BF16))

        @pl.when(k == nk - 1)
        def _():
            o_ref[...] = acc_scr[...].astype(BF16)

    (out,), p_outs = _call(
        body, name, (m // tmm, nk),
        [pl.BlockSpec((tk, tmm), lambda j, k: (k, j)), pl.BlockSpec((tk, n), lambda j, k: (k, 0))],
        [pl.BlockSpec((tmm, n), lambda j, k: (j, 0))],
        [jax.ShapeDtypeStruct((m, n), BF16)],
        [pltpu.VMEM((tmm, n), F32)],
        ("parallel", "arbitrary"), (a, b), payload)
    return out, p_outs


def _proj_in_fwd(x, g, wb, li, name, payload=None):
    t, d = x.shape
    n = wb.shape[1]
    tm = 512

    def body(x_ref, g_ref, w_ref, z_ref):
        xhat, _ = _rms(x_ref[...])
        z_ref[...] = _dot_nt((xhat * g_ref[...]).astype(BF16), w_ref[...])

    (z,), p_outs = _call(
        body, name, (t // tm,),
        [pl.BlockSpec((tm, d), lambda i: (i, 0)), pl.BlockSpec((1, d), lambda i: (0, 0)),
         pl.BlockSpec((None, n, d), lambda i: (li, 0, 0))],
        [pl.BlockSpec((tm, n), lambda i: (i, 0))],
        [jax.ShapeDtypeStruct((t, n), F32)], [], ("parallel",), (x, g, wb), payload)
    return z, p_outs


def _proj_out_fwd(x, cat, wc, li, name, payload=None):
    t, d = x.shape
    tm = 512

    def body(x_ref, cat_ref, w_ref, xo_ref):
        xo_ref[...] = x_ref[...] + _dot(cat_ref[...], w_ref[...])

    (xo,), p_outs = _call(
        body, name, (t // tm,),
        [pl.BlockSpec((tm, d), lambda i: (i, 0)), pl.BlockSpec((tm, d), lambda i: (i, 0)),
         pl.BlockSpec((None, d, d), lambda i: (li, 0, 0))],
        [pl.BlockSpec((tm, d), lambda i: (i, 0))],
        [jax.ShapeDtypeStruct((t, d), F32)], [], ("parallel",), (x, cat, wc), payload)
    return xo, p_outs


def _proj_out_bwd(dxo, wc, li, name):
    t, d = dxo.shape
    tm = 512

    def body(dxo_ref, w_ref, dcat_ref):
        dcat_ref[...] = _dot_nt(dxo_ref[...].astype(BF16), w_ref[...])

    return pl.pallas_call(
        body, name=name, grid=(t // tm,),
        in_specs=[pl.BlockSpec((tm, d), lambda i: (i, 0)), pl.BlockSpec((None, d, d), lambda i: (li, 0, 0))],
        out_specs=pl.BlockSpec((tm, d), lambda i: (i, 0)),
        out_shape=jax.ShapeDtypeStruct((t, d), F32),
        compiler_params=_cparams(("parallel",)),
    )(dxo, wc)


def _proj_in_bwd(x, g, dxo, dz, wb, li, name):
    t, d = x.shape
    n = wb.shape[1]
    tm = 512

    def body(x_ref, g_ref, dxo_ref, dz_ref, w_ref, dx_ref, h_ref, dg_ref):
        i = pl.program_id(0)

        @pl.when(i == 0)
        def _():
            dg_ref[...] = jnp.zeros_like(dg_ref)

        xhat, rstd = _rms(x_ref[...])
        h_ref[...] = (xhat * g_ref[...]).astype(BF16)
        dh = _dot(dz_ref[...], w_ref[...])
        dxn, dg = _rms_bwd(xhat, rstd, g_ref[...], dh)
        dx_ref[...] = dxo_ref[...] + dxn
        dg_ref[...] += dg

    row = pl.BlockSpec((tm, d), lambda i: (i, 0))
    vec = pl.BlockSpec((1, d), lambda i: (0, 0))
    return pl.pallas_call(
        body, name=name, grid=(t // tm,),
        in_specs=[row, vec, row, pl.BlockSpec((tm, n), lambda i: (i, 0)),
                  pl.BlockSpec((None, n, d), lambda i: (li, 0, 0))],
        out_specs=[row, row, vec],
        out_shape=[jax.ShapeDtypeStruct((t, d), F32), jax.ShapeDtypeStruct((t, d), BF16),
                   jax.ShapeDtypeStruct((1, d), F32)],
        compiler_params=_cparams(("arbitrary",)),
    )(x, g, dxo, dz, wb)


def _lane_ids(shape):
    return lax.broadcasted_iota(jnp.int32, shape, 1)


def _tril(w):
    r = lax.broadcasted_iota(jnp.int32, w.shape, 0)
    c = lax.broadcasted_iota(jnp.int32, w.shape, 1)
    return jnp.where(r >= c, w, 0.0)


def _shift_down(x, k):
    return x if k == 0 else pltpu.roll(x, k, 0)


def _shift_up(x, k):
    return x if k == 0 else pltpu.roll(x, x.shape[0] - k, 0)


def _causal_conv(ext, w, n_out):
    acc = None
    for b in range(8):
        rolled = _shift_down(ext, b)
        for a in range((CONV_WIDTH - 1 - b) // 8 + 1):
            j = 8 * a + b
            term = rolled[HALO - 8 * a:HALO - 8 * a + n_out] * w[CONV_WIDTH - 1 - j:CONV_WIDTH - j]
            acc = term if acc is None else acc + term
    return acc


def _conv_wgrad(ext, dhc, n_out):
    rows = [None] * CONV_WIDTH
    for b in range(8):
        rolled = _shift_down(ext, b)
        for a in range((CONV_WIDTH - 1 - b) // 8 + 1):
            j = 8 * a + b
            rows[CONV_WIDTH - 1 - j] = jnp.sum(
                rolled[HALO - 8 * a:HALO - 8 * a + n_out] * dhc, axis=0, keepdims=True)
    return rows


def _anticausal_conv(ext, w, n_out):
    acc = None
    for b in range(8):
        rolled = _shift_up(ext, b)
        for a in range((CONV_WIDTH - 1 - b) // 8 + 1):
            j = 8 * a + b
            term = rolled[8 * a:8 * a + n_out] * w[CONV_WIDTH - 1 - j:CONV_WIDTH - j]
            acc = term if acc is None else acc + term
    return acc


def _window_sums(ext, shift):
    s2 = ext + shift(ext, 1)
    s4 = s2 + shift(s2, 2)
    s8 = s4 + shift(s4, 4)
    s16 = s8 + shift(s8, 8)
    grp = _lane_ids(ext.shape) // HEAD_DIM
    return jnp.where(grp == 0, s2, jnp.where(grp == 1, s4, jnp.where(grp == 2, s8, s16)))


def _pool_count(t0, n, width):
    pos = (lax.broadcasted_iota(jnp.int32, (n, width), 0) + (t0 + 1)).astype(F32)
    grp = _lane_ids((n, width)) // HEAD_DIM
    win = jnp.where(grp == 0, 2.0, jnp.where(grp == 1, 4.0, jnp.where(grp == 2, 8.0, 16.0)))
    return jnp.minimum(pos, win)


def _block_diag(pw):
    gn, cg, _ = pw.shape
    rows = []
    for gi in range(gn):
        parts = [pw[gi] if gj == gi else jnp.zeros((cg, cg), pw.dtype) for gj in range(gn)]
        rows.append(jnp.concatenate(parts, axis=1))
    return jnp.concatenate(rows, axis=0)


def _head_pair_mix(w_even, w_odd, v):
    lo = _lane_ids((CHUNK, LANES)) < HEAD_DIM
    return jnp.where(lo, _dot(w_even, v), _dot(w_odd, v))


def _mix_fwd(z, p, name, payload=None):
    t, d_in = z.shape
    sgu = p["sgu_ln_g"].shape[1]
    pool = p["pool_scale"].shape[1]
    d_mix = 2 * sgu + pool
    tm = 512
    n_i = t // tm
    hb = tm // HALO

    def body(z_ref, zp_ref, lng_ref, lnb_ref, ws_ref, bs_ref, cw_ref, cb_ref, clg_ref, clb_ref,
             bd_ref, ps_ref, cat_ref):
        i = pl.program_id(0)
        first = i == 0
        z_main = z_ref[...]
        z_prev = jnp.where(first, 0.0, zp_ref[...])

        lng, lnb = lng_ref[...], lnb_ref[...]
        wt = [_tril(ws_ref[h]).astype(BF16) for h in range(sgu // HEAD_DIM)]
        for n in range(tm // CHUNK):
            rows = slice(n * CHUNK, (n + 1) * CHUNK)
            u = _gelu(z_main[rows, 0:sgu])
            vhat, _ = _ln(_gelu(z_main[rows, sgu:2 * sgu]))
            vn = (vhat * lng + lnb).astype(BF16)
            for gp in range(sgu // LANES):
                ls = slice(gp * LANES, (gp + 1) * LANES)
                mixed = _head_pair_mix(wt[2 * gp], wt[2 * gp + 1], vn[:, ls]) + bs_ref[:, ls]
                cat_ref[rows, ls] = (u[:, ls] * mixed).astype(BF16)

        def glu(zz):
            return zz[:, 2 * sgu:3 * sgu] * jax.nn.sigmoid(zz[:, 3 * sgu:4 * sgu])

        ext = jnp.concatenate([glu(z_prev), glu(z_main)], axis=0)
        hc = _causal_conv(ext, cw_ref[...], tm) + cb_ref[...]
        hhat, _ = _ln(hc)
        bn = hhat * clg_ref[...] + clb_ref[...]
        cat_ref[:, sgu:2 * sgu] = (bn * jax.nn.sigmoid(bn)).astype(BF16)

        pext = jnp.concatenate([z_prev[:, 4 * sgu:], z_main[:, 4 * sgu:]], axis=0)
        sums = _window_sums(pext, _shift_down)[HALO:]
        pooled = sums / _pool_count(i * tm, tm, pool) - z_main[:, 4 * sgu:]
        mixed_c = _dot(pooled.astype(BF16), bd_ref[...].astype(BF16))
        cat_ref[:, 2 * sgu:] = (mixed_c * ps_ref[...]).astype(BF16)

    def vec(n):
        return pl.BlockSpec((1, n), lambda i: (0, 0))

    (cat,), p_outs = _call(
        body, name, (n_i,),
        [pl.BlockSpec((tm, d_in), lambda i: (i, 0)),
         pl.BlockSpec((HALO, d_in), lambda i: (jnp.maximum(i * hb - 1, 0), 0)),
         vec(sgu), vec(sgu),
         pl.BlockSpec(p["w_spatial"].shape, lambda i: (0, 0, 0)),
         pl.BlockSpec((CHUNK, sgu), lambda i: (0, 0)),
         pl.BlockSpec((CONV_WIDTH, sgu), lambda i: (0, 0)),
         vec(sgu), vec(sgu), vec(sgu),
         pl.BlockSpec((pool, pool), lambda i: (0, 0)), vec(pool)],
        [pl.BlockSpec((tm, d_mix), lambda i: (i, 0))],
        [jax.ShapeDtypeStruct((t, d_mix), BF16)], [], ("parallel",),
        (z, z, p["sgu_ln_g"], p["sgu_ln_b"], p["w_spatial"], p["bs_full"], p["conv_w"], p["conv_b"],
         p["conv_ln_g"], p["conv_ln_b"], p["bd"], p["pool_scale"]), payload)
    return cat, p_outs


_R_SGU_G, _R_SGU_B, _R_CONV_B, _R_CLN_G, _R_CLN_B, _R_CONV_W = 0, 1, 2, 3, 4, 8
_R384_ROWS = 40


def _mix_bwd(z, dcat, p, name, payload=None):
    t, d_in = z.shape
    sgu = p["sgu_ln_g"].shape[1]
    pool = p["pool_scale"].shape[1]
    d_mix = 2 * sgu + pool
    n_head = sgu // HEAD_DIM
    tm = 512
    n_i = t // tm
    hb = tm // HALO

    def body(z_ref, zp_ref, zn_ref, dc_ref, dcn_ref, lng_ref, lnb_ref, ws_ref, bs_ref, cw_ref, cb_ref,
             clg_ref, clb_ref, bd_ref, ps_ref, dz_ref, g384_ref, gws_ref, gpool_ref, dbs_scr):
        i = pl.program_id(0)
        first, last = i == 0, i == n_i - 1

        @pl.when(first)
        def _():
            g384_ref[...] = jnp.zeros_like(g384_ref)
            gws_ref[...] = jnp.zeros_like(gws_ref)
            gpool_ref[...] = jnp.zeros_like(gpool_ref)
            dbs_scr[...] = jnp.zeros_like(dbs_scr)

        z_main = z_ref[...]
        z_prev = jnp.where(first, 0.0, zp_ref[...])
        z_next = jnp.where(last, 0.0, zn_ref[...])
        dc_main = dc_ref[...]
        dc_next = jnp.where(last, 0.0, dcn_ref[...])

        lng, lnb = lng_ref[...], lnb_ref[...]
        wt = [_tril(ws_ref[h]) for h in range(n_head)]
        wt_b = [w.astype(BF16) for w in wt]
        wtt_b = [w.T.astype(BF16) for w in wt]
        lo = _lane_ids((CHUNK, LANES)) < HEAD_DIM
        d_lng = jnp.zeros((1, sgu), F32)
        d_lnb = jnp.zeros((1, sgu), F32)
        dws = [jnp.zeros((CHUNK, CHUNK), F32) for _ in range(n_head)]
        for n in range(tm // CHUNK):
            rows = slice(n * CHUNK, (n + 1) * CHUNK)
            au, av = z_main[rows, 0:sgu], z_main[rows, sgu:2 * sgu]
            u = _gelu(au)
            vhat, vrstd = _ln(_gelu(av))
            vn = (vhat * lng + lnb).astype(BF16)
            da = dc_main[rows, 0:sgu]
            dmixed = da * u
            dbs_scr[...] += dmixed
            dvn_parts, du_parts = [], []
            for gp in range(sgu // LANES):
                ls = slice(gp * LANES, (gp + 1) * LANES)
                vn_g = vn[:, ls]
                mixed = _head_pair_mix(wt_b[2 * gp], wt_b[2 * gp + 1], vn_g) + bs_ref[:, ls]
                du_parts.append(da[:, ls] * mixed)
                dm_g = dmixed[:, ls]
                dm_b = dm_g.astype(BF16)
                dvn_parts.append(jnp.where(lo, _dot(wtt_b[2 * gp], dm_b), _dot(wtt_b[2 * gp + 1], dm_b)))
                dws[2 * gp] = dws[2 * gp] + _dot_nt(jnp.where(lo, dm_g, 0.0).astype(BF16), vn_g)
                dws[2 * gp + 1] = dws[2 * gp + 1] + _dot_nt(jnp.where(lo, 0.0, dm_g).astype(BF16), vn_g)
            dvn = jnp.concatenate(dvn_parts, axis=1)
            du = jnp.concatenate(du_parts, axis=1)
            dv, dg_n, db_n = _ln_bwd(vhat, vrstd, lng, dvn)
            d_lng = d_lng + dg_n
            d_lnb = d_lnb + db_n
            dz_ref[rows, 0:sgu] = (du * _gelu_grad(au)).astype(BF16)
            dz_ref[rows, sgu:2 * sgu] = (dv * _gelu_grad(av)).astype(BF16)
        for h in range(n_head):
            gws_ref[h] += _tril(dws[h])
        g384_ref[_R_SGU_G:_R_SGU_G + 1, :] += d_lng
        g384_ref[_R_SGU_B:_R_SGU_B + 1, :] += d_lnb

        @pl.when(last)
        def _():
            r = lax.broadcasted_iota(jnp.int32, (sgu, LANES), 0)
            c = lax.broadcasted_iota(jnp.int32, (sgu, LANES), 1)
            sel = (r // HEAD_DIM == c).astype(BF16)
            gws_ref[n_head] = _split_dot(dbs_scr[...], sel)

        cw, clg = cw_ref[...], clg_ref[...]
        bcols = slice(2 * sgu, 4 * sgu)
        zb = jnp.concatenate([z_prev[:, bcols], z_main[:, bcols], z_next[:, bcols]], axis=0)
        bval, bgate = zb[:, 0:sgu], zb[:, sgu:2 * sgu]
        sg = jax.nn.sigmoid(bgate)
        hglu = bval * sg
        n_out = tm + HALO
        hc = _causal_conv(hglu, cw, n_out) + cb_ref[...]
        hhat, hrstd = _ln(hc)
        bn = hhat * clg + clb_ref[...]
        db = jnp.concatenate([dc_main[:, sgu:2 * sgu], dc_next[:, sgu:2 * sgu]], axis=0)
        dbn = db * _silu_grad(bn)
        dhc_all, _, _ = _ln_bwd(hhat, hrstd, clg, dbn)
        dbn_m, hhat_m, dhc = dbn[:tm], hhat[:tm], dhc_all[:tm]
        g384_ref[_R_CLN_G:_R_CLN_G + 1, :] += jnp.sum(dbn_m * hhat_m, axis=0, keepdims=True)
        g384_ref[_R_CLN_B:_R_CLN_B + 1, :] += jnp.sum(dbn_m, axis=0, keepdims=True)
        g384_ref[_R_CONV_B:_R_CONV_B + 1, :] += jnp.sum(dhc, axis=0, keepdims=True)
        wrows = _conv_wgrad(hglu[:HALO + tm], dhc, tm)
        for k in range(CONV_WIDTH):
            g384_ref[_R_CONV_W + k:_R_CONV_W + k + 1, :] += wrows[k]
        dhglu = _anticausal_conv(dhc_all, cw, tm)
        bval_m, sg_m = bval[HALO:HALO + tm], sg[HALO:HALO + tm]
        dz_ref[:, 2 * sgu:3 * sgu] = (dhglu * sg_m).astype(BF16)
        dz_ref[:, 3 * sgu:4 * sgu] = (dhglu * bval_m * sg_m * (1.0 - sg_m)).astype(BF16)

        bd_b = bd_ref[...].astype(BF16)
        ps = ps_ref[...]
        p_main = z_main[:, 4 * sgu:]
        pext = jnp.concatenate([z_prev[:, 4 * sgu:], p_main], axis=0)
        cnt = _pool_count(i * tm, n_out, pool)
        pooled = _window_sums(pext, _shift_down)[HALO:] / cnt[:tm] - p_main
        pooled_b = pooled.astype(BF16)
        dcc = jnp.concatenate([dc_main[:, 2 * sgu:], dc_next[:, 2 * sgu:]], axis=0)
        dmix_c = dcc * ps
        mixed_c = _dot(pooled_b, bd_b)
        grp_r = lax.broadcasted_iota(jnp.int32, (pool, pool), 0) // HEAD_DIM
        grp_c = lax.broadcasted_iota(jnp.int32, (pool, pool), 1) // HEAD_DIM
        gpool_ref[0:pool, :] += jnp.where(grp_r == grp_c, _dot_tn(pooled_b, dmix_c[:tm].astype(BF16)), 0.0)
        gpool_ref[pool:pool + 1, :] += jnp.sum(dcc[:tm] * mixed_c, axis=0, keepdims=True)
        dpooled = _dot_nt(dmix_c.astype(BF16), bd_b)
        q = dpooled / cnt
        dp = _window_sums(q, _shift_up)[:tm] - dpooled[:tm]
        dz_ref[:, 4 * sgu:] = dp.astype(BF16)

    def vec(n):
        return pl.BlockSpec((1, n), lambda i: (0, 0))

    def prev_map(i):
        return (jnp.maximum(i * hb - 1, 0), 0)

    def next_map(i):
        return (jnp.minimum((i + 1) * hb, n_i * hb - 1), 0)

    return _call(
        body, name, (n_i,),
        [pl.BlockSpec((tm, d_in), lambda i: (i, 0)),
         pl.BlockSpec((HALO, d_in), prev_map), pl.BlockSpec((HALO, d_in), next_map),
         pl.BlockSpec((tm, d_mix), lambda i: (i, 0)), pl.BlockSpec((HALO, d_mix), next_map),
         vec(sgu), vec(sgu),
         pl.BlockSpec(p["w_spatial"].shape, lambda i: (0, 0, 0)),
         pl.BlockSpec((CHUNK, sgu), lambda i: (0, 0)),
         pl.BlockSpec((CONV_WIDTH, sgu), lambda i: (0, 0)),
         vec(sgu), vec(sgu), vec(sgu),
         pl.BlockSpec((pool, pool), lambda i: (0, 0)), vec(pool)],
        [pl.BlockSpec((tm, d_in), lambda i: (i, 0)),
         pl.BlockSpec((_R384_ROWS, sgu), lambda i: (0, 0)),
         pl.BlockSpec((n_head + 1, CHUNK, CHUNK), lambda i: (0, 0, 0)),
         pl.BlockSpec((pool + 8, pool), lambda i: (0, 0))],
        [jax.ShapeDtypeStruct((t, d_in), BF16),
         jax.ShapeDtypeStruct((_R384_ROWS, sgu), F32),
         jax.ShapeDtypeStruct((n_head + 1, CHUNK, CHUNK), F32),
         jax.ShapeDtypeStruct((pool + 8, pool), F32)],
        [pltpu.VMEM((CHUNK, sgu), F32)], ("arbitrary",),
        (z, z, z, dcat, dcat, p["sgu_ln_g"], p["sgu_ln_b"], p["w_spatial"], p["bs_full"], p["conv_w"],
         p["conv_b"], p["conv_ln_g"], p["conv_ln_b"], p["bd"], p["pool_scale"]), payload)


def _loss_head(x, g, target, name):
    t, d = x.shape
    tm = 512

    def body(x_ref, g_ref, tgt_ref, dx_ref, dg_ref, loss_ref):
        i = pl.program_id(0)

        @pl.when(i == 0)
        def _():
            dg_ref[...] = jnp.zeros_like(dg_ref)
            loss_ref[...] = jnp.zeros_like(loss_ref)

        gv = g_ref[...]
        xhat, rstd = _rms(x_ref[...])
        err = xhat * gv - tgt_ref[...]
        loss_ref[...] += jnp.zeros_like(loss_ref) + 0.5 * jnp.sum(jnp.mean(err * err, axis=-1, keepdims=True))
        dxn, dg = _rms_bwd(xhat, rstd, gv, err * (1.0 / d))
        dx_ref[...] = dxn
        dg_ref[...] += dg

    row = pl.BlockSpec((tm, d), lambda i: (i, 0))
    vec = pl.BlockSpec((1, d), lambda i: (0, 0))
    return pl.pallas_call(
        body, name=name, grid=(t // tm,),
        in_specs=[row, vec, row],
        out_specs=[row, vec, pl.BlockSpec((1, LANES), lambda i: (0, 0))],
        out_shape=[jax.ShapeDtypeStruct((t, d), F32), jax.ShapeDtypeStruct((1, d), F32),
                   jax.ShapeDtypeStruct((1, LANES), F32)],
        compiler_params=_cparams(("arbitrary",)),
    )(x, g, target)


def _all_gather(arrs, name, extra=None):
    gather = _GatherIci(arrs)
    n = len(arrs)
    forward = _GatherForward([jax.ShapeDtypeStruct(s.shape, s.dtype) for s in gather.out_shapes])
    x_in = len(extra.ins) if extra else 0
    x_out = len(extra.out_shapes) if extra else 0

    def body(*refs):
        ins, x_ins = refs[:n], refs[n:n + x_in]
        outs, x_outs = refs[n + x_in:2 * n + x_in], refs[2 * n + x_in:2 * n + x_in + x_out]
        sems = refs[2 * n + x_in + x_out:]
        first = gather.build(ins, outs, *sems[0:3])
        first.start()
        if extra:
            beside = extra.build(x_ins, x_outs, *sems[6:9])
            beside.start()
        first.wait()
        second = forward.build(outs, outs, *sems[3:6])
        second.start()
        second.wait()
        if extra:
            beside.wait()

    outs = pl.pallas_call(
        body, name=name,
        in_specs=[ANY] * (n + x_in), out_specs=[ANY] * (n + x_out),
        out_shape=list(gather.out_shapes) + (list(extra.out_shapes) if extra else []),
        scratch_shapes=gather.sem_shapes() + forward.sem_shapes() + (extra.sem_shapes() if extra else []),
    )(*arrs, *(extra.ins if extra else []))
    return list(outs[:n]), list(outs[n:])


def _pair_sums(grads, recvs, cidx, name):
    n = len(grads)

    def body(c_ref, *refs):
        for g_ref, r_ref, o_ref in zip(refs[:n], refs[n:2 * n], refs[2 * n:]):
            o_ref[...] = (g_ref[...].astype(F32) + r_ref[...].astype(F32)).astype(BF16)

    shapes = [(g.shape[0] // N_DEV, g.shape[1]) for g in grads]
    return list(pl.pallas_call(
        body, name=name,
        grid_spec=pltpu.PrefetchScalarGridSpec(
            num_scalar_prefetch=1, grid=(N_CHIP,),
            in_specs=[pl.BlockSpec(s, lambda q, c: (2 * q + c[0], 0)) for s in shapes]
            + [pl.BlockSpec(s, lambda q, c: (q, 0)) for s in shapes],
            out_specs=[pl.BlockSpec(s, lambda q, c: (q, 0)) for s in shapes]),
        out_shape=[jax.ShapeDtypeStruct((N_CHIP * r, cols), BF16) for r, cols in shapes],
        compiler_params=_cparams(("parallel",)),
    )(cidx, *grads, *recvs))


def _sum_blocks(parts, nblk, name):
    r = parts.shape[0] // nblk
    cols = parts.shape[1]

    def body(p_ref, o_ref):
        acc = p_ref[0:r, :].astype(F32)
        for q in range(1, nblk):
            acc = acc + p_ref[q * r:(q + 1) * r, :].astype(F32)
        o_ref[...] = acc

    return pl.pallas_call(
        body, name=name,
        out_shape=jax.ShapeDtypeStruct((r, cols), F32),
        compiler_params=_cparams(),
    )(parts)


def _adamw_math(w, g, m, v):
    m = ADAM_B1 * m + (1.0 - ADAM_B1) * g
    v = ADAM_B2 * v + (1.0 - ADAM_B2) * (g * g)
    m_hat = m / (1.0 - ADAM_B1 ** ADAM_STEP)
    v_hat = v / (1.0 - ADAM_B2 ** ADAM_STEP)
    delta = -ADAM_LR * (m_hat / (jnp.sqrt(v_hat) + ADAM_EPS) + ADAM_WD * w)
    return delta, m, v


def _finish_sharded(parts, w, m, v, name):
    depth, rr, cw = w.shape

    def body(*refs):
        p_refs = refs[:depth]
        w_ref, m_ref, v_ref, g_ref, d_ref, mo_ref, vo_ref = refs[depth:]
        l = pl.program_id(0)
        for k in range(depth):
            @pl.when(l == k)
            def _(p_ref=p_refs[k]):
                r = p_ref.shape[0] // N_CHIP
                acc = p_ref[0:r, :].astype(F32)
                for q in range(1, N_CHIP):
                    acc = acc + p_ref[q * r:(q + 1) * r, :].astype(F32)
                g_ref[...] = acc
                d_ref[...], mo_ref[...], vo_ref[...] = _adamw_math(w_ref[...], acc, m_ref[...], v_ref[...])

    blk = pl.BlockSpec((None, rr, cw), lambda l: (l, 0, 0))
    return pl.pallas_call(
        body, name=name, grid=(depth,),
        in_specs=[pl.BlockSpec(p.shape, lambda l: (0, 0)) for p in parts] + [blk] * 3, out_specs=[blk] * 4,
        out_shape=[jax.ShapeDtypeStruct(w.shape, F32)] * 4,
        compiler_params=_cparams(("arbitrary",)),
    )(*parts, w, m, v)


def _adamw_small(ws, gs, ms, vs, name):
    n = len(ws)

    def body(*refs):
        for k in range(n):
            w_ref, g_ref, m_ref, v_ref = (refs[j * n + k] for j in range(4))
            d_ref, mo_ref, vo_ref = (refs[(4 + j) * n + k] for j in range(3))
            d_ref[...], mo_ref[...], vo_ref[...] = _adamw_math(w_ref[...], g_ref[...], m_ref[...], v_ref[...])

    shapes = [jax.ShapeDtypeStruct(w.shape, F32) for w in ws]
    return pl.pallas_call(
        body, name=name, out_shape=shapes * 3, compiler_params=_cparams(),
    )(*ws, *gs, *ms, *vs)


def kernel(x, ffn1_norm, ffn1_w_gate, ffn1_w_up, ffn1_w_down, mix_norm, w_in, sgu_ln_g, sgu_ln_b, w_spatial, b_spatial, conv_w, conv_b, conv_ln_g, conv_ln_b, pool_w, pool_scale, w_out, ffn2_norm, ffn2_w_gate, ffn2_w_up, ffn2_w_down, final_norm, loss_target, m_ffn1_norm, m_ffn1_w_gate, m_ffn1_w_up, m_ffn1_w_down, m_mix_norm, m_w_in, m_sgu_ln_g, m_sgu_ln_b, m_w_spatial, m_b_spatial, m_conv_w, m_conv_b, m_conv_ln_g, m_conv_ln_b, m_pool_w, m_pool_scale, m_w_out, m_ffn2_norm, m_ffn2_w_gate, m_ffn2_w_up, m_ffn2_w_down, m_final_norm, v_ffn1_norm, v_ffn1_w_gate, v_ffn1_w_up, v_ffn1_w_down, v_mix_norm, v_w_in, v_sgu_ln_g, v_sgu_ln_b, v_w_spatial, v_b_spatial, v_conv_w, v_conv_b, v_conv_ln_g, v_conv_ln_b, v_pool_w, v_pool_scale, v_w_out, v_ffn2_norm, v_ffn2_w_gate, v_ffn2_w_up, v_ffn2_w_down, v_final_norm):
    names = ["ffn1_norm", "ffn1_w_gate", "ffn1_w_up", "ffn1_w_down", "mix_norm", "w_in", "sgu_ln_g", "sgu_ln_b",
             "w_spatial", "b_spatial", "conv_w", "conv_b", "conv_ln_g", "conv_ln_b", "pool_w", "pool_scale",
             "w_out", "ffn2_norm", "ffn2_w_gate", "ffn2_w_up", "ffn2_w_down", "final_norm"]
    W = dict(zip(names, [ffn1_norm, ffn1_w_gate, ffn1_w_up, ffn1_w_down, mix_norm, w_in, sgu_ln_g, sgu_ln_b,
                         w_spatial, b_spatial, conv_w, conv_b, conv_ln_g, conv_ln_b, pool_w, pool_scale, w_out,
                         ffn2_norm, ffn2_w_gate, ffn2_w_up, ffn2_w_down, final_norm]))
    M = dict(zip(names, [m_ffn1_norm, m_ffn1_w_gate, m_ffn1_w_up, m_ffn1_w_down, m_mix_norm, m_w_in, m_sgu_ln_g,
                         m_sgu_ln_b, m_w_spatial, m_b_spatial, m_conv_w, m_conv_b, m_conv_ln_g, m_conv_ln_b,
                         m_pool_w, m_pool_scale, m_w_out, m_ffn2_norm, m_ffn2_w_gate, m_ffn2_w_up, m_ffn2_w_down,
                         m_final_norm]))
    V = dict(zip(names, [v_ffn1_norm, v_ffn1_w_gate, v_ffn1_w_up, v_ffn1_w_down, v_mix_norm, v_w_in, v_sgu_ln_g,
                         v_sgu_ln_b, v_w_spatial, v_b_spatial, v_conv_w, v_conv_b, v_conv_ln_g, v_conv_ln_b,
                         v_pool_w, v_pool_scale, v_w_out, v_ffn2_norm, v_ffn2_w_gate, v_ffn2_w_up, v_ffn2_w_down,
                         v_final_norm]))

    depth, d = ffn1_norm.shape
    t = x.shape[1]
    sgu = sgu_ln_g.shape[1]
    pool = pool_scale.shape[1]
    n_head = sgu // HEAD_DIM
    cw_shard = conv_w.shape[2]
    xs = x.reshape(t, d)
    target = loss_target.reshape(t, d)

    def tr(w):
        return jnp.swapaxes(w, 1, 2).astype(BF16)

    ffn_shards = [[jnp.stack([tr(ffn1_w_gate)[l], tr(ffn1_w_up)[l], ffn1_w_down[l].astype(BF16)]),
                   jnp.stack([tr(ffn2_w_gate)[l], tr(ffn2_w_up)[l], ffn2_w_down[l].astype(BF16)])]
                  for l in range(depth)]
    win_shards = [tr(w_in)[l:l + 1] for l in range(depth)]
    wout_shards = [w_out[l:l + 1].astype(BF16) for l in range(depth)]
    cw_rows = depth * CONV_WIDTH
    cw_pad = -cw_rows % 8
    cw_send = jnp.pad(conv_w.reshape(cw_rows, cw_shard), ((0, cw_pad), (0, 0)))[None]
    wffn, wb, wc = {}, {}, {}
    (wffn[(0, 0)], wb[0], wc[0], cwg), _ = _all_gather(
        [ffn_shards[0][0], win_shards[0], wout_shards[0], cw_send], "ag_first")
    conv_w_full = cwg.reshape(N_DEV, cw_rows + cw_pad, cw_shard)[:, :cw_rows].reshape(
        N_DEV, depth, CONV_WIDTH, cw_shard).transpose(1, 2, 0, 3).reshape(depth, CONV_WIDTH, N_DEV * cw_shard)

    def mixer_params(l):
        return dict(
            sgu_ln_g=sgu_ln_g[l:l + 1], sgu_ln_b=sgu_ln_b[l:l + 1], w_spatial=w_spatial[l],
            bs_full=jnp.repeat(b_spatial[l].T, HEAD_DIM, axis=1),
            conv_w=conv_w_full[l], conv_b=conv_b[l:l + 1], conv_ln_g=conv_ln_g[l:l + 1],
            conv_ln_b=conv_ln_b[l:l + 1], bd=_block_diag(pool_w[l]), pool_scale=pool_scale[l:l + 1])

    saved = []
    cur = xs
    for l in range(depth):
        p = mixer_params(l)
        x0 = cur
        more = l + 1 < depth
        (x1, gate1, up1), part = _ffn_fwd(x0, ffn1_norm[l:l + 1], wffn[(l, 0)], 0, f"ffn1_fwd_{l}",
                                          _GatherIci([ffn_shards[l][1]]))
        z, (wffn[(l, 1)],) = _proj_in_fwd(x1, mix_norm[l:l + 1], wb[l], 0, f"proj_in_fwd_{l}", _GatherForward(part))
        cat, part = _mix_fwd(z, p, f"mix_fwd_{l}",
                             _GatherIci([win_shards[l + 1], wout_shards[l + 1]]) if more else None)
        x2, part = _proj_out_fwd(x1, cat, wc[l], 0, f"proj_out_fwd_{l}", _GatherForward(part) if more else None)
        if more:
            wb[l + 1], wc[l + 1] = part
        (x3, gate2, up2), part = _ffn_fwd(x2, ffn2_norm[l:l + 1], wffn[(l, 1)], 0, f"ffn2_fwd_{l}",
                                          _GatherIci([ffn_shards[l + 1][0]]) if more else None)
        if more:
            (wffn[(l + 1, 0)],) = _comm(_GatherForward(part), f"ag_forward_{l + 1}")
        saved.append((p, x0, gate1, up1, x1, z, cat, x2, gate2, up2))
        cur = x3

    dx, d_final, loss_part = _loss_head(cur, final_norm.reshape(1, d), target, "loss_head")

    cidx = lax.axis_index("c").astype(jnp.int32).reshape(1)
    from_chips = {}
    to_pair, to_chip = [], []
    small = []

    def pair_payload():
        return _PairExchange([g for _, g in to_pair]) if to_pair else None

    def pair_done(received):
        if to_pair:
            (nm, l), _ = to_pair[0]
            sums = _pair_sums([g for _, g in to_pair], list(received), cidx, f"rs_pair_sum_{nm}_{l}")
            to_chip.extend((key, s) for (key, _), s in zip(to_pair, sums))
        to_pair.clear()

    def take_chip():
        items = list(to_chip)
        to_chip.clear()
        return items

    def chip_payload(items):
        return _ChipExchange([s for _, s in items]) if items else None

    def chip_done(items, landed):
        for (key, _), o in zip(items, landed):
            from_chips[key] = o

    def ffn_weight_grads(prefix, l, dgate, dup, act, h, dy):
        items = take_chip()
        for k, (nm, a, b) in enumerate([("w_gate", dgate, h), ("w_up", dup, h), ("w_down", act, dy)]):
            mine = items[k::3]
            g, landed = _tn_matmul(a, b, f"dw_{prefix}_{nm}_{l}", chip_payload(mine))
            chip_done(mine, landed)
            to_pair.append(((f"{prefix}_{nm}", l), g))

    for l in reversed(range(depth)):
        p, x0, gate1, up1, x1, z, cat, x2, gate2, up2 = saved[l]
        (dx, dgate, dup, act, h, dy, dg_ffn2), received = _ffn_bwd(
            x2, ffn2_norm[l:l + 1], dx, gate2, up2, wffn[(l, 1)], 0, f"ffn2_bwd_{l}", pair_payload())
        pair_done(received)
        ffn_weight_grads("ffn2", l, dgate, dup, act, h, dy)
        g_out, received = _tn_matmul(cat, dx, f"dw_w_out_{l}", pair_payload())
        pair_done(received)
        dcat = _proj_out_bwd(dx, wc[l], 0, f"proj_out_bwd_{l}")
        items = take_chip()
        (dz, g384, gws, gpool), landed = _mix_bwd(z, dcat, p, f"mix_bwd_{l}", chip_payload(items))
        chip_done(items, landed)
        dx, hm, dg_mix = _proj_in_bwd(x1, mix_norm[l:l + 1], dx, dz, wb[l], 0, f"proj_in_bwd_{l}")
        g_in, _ = _tn_matmul(dz, hm, f"dw_w_in_{l}")
        to_pair.extend([(("w_out", l), g_out), (("w_in", l), g_in)])
        if l > 0:
            (dx, dgate, dup, act, h, dy, dg_ffn1), received = _ffn_bwd(
                x0, ffn1_norm[l:l + 1], dx, gate1, up1, wffn[(l, 0)], 0, f"ffn1_bwd_{l}", pair_payload())
            pair_done(received)
            ffn_weight_grads("ffn1", l, dgate, dup, act, h, dy)
            small.append((l, g384, gws, gpool, dg_ffn1, dg_mix, dg_ffn2))
            continue

        small.append((0, g384, gws, gpool, None, dg_mix, dg_ffn2))
        small.sort(key=lambda s: s[0])
        norm_rows = []
        for (sl, _, _, _, dg1, dgm, dg2) in small:
            norm_rows += [jnp.zeros((1, d), F32) if dg1 is None else dg1, dgm, dg2]
        norm_rows += [d_final, jnp.pad(loss_part, ((0, 0), (0, d - LANES)))]
        n_norm = len(norm_rows)
        norm_pack = jnp.concatenate(norm_rows + [jnp.zeros((8 - n_norm % 8, d), F32)] * (n_norm % 8 != 0), axis=0)
        parts = [norm_pack]
        for (_, s384, sws, spool, _, _, _) in small:
            parts += [s384, sws.reshape((n_head + 1) * CHUNK, CHUNK), spool]
        n_pair = len(to_pair)
        (dx, dgate, dup, act, h, dy, dg_ffn1), landed = _ffn_bwd(
            x0, ffn1_norm[l:l + 1], dx, gate1, up1, wffn[(l, 0)], 0, f"ffn1_bwd_{l}",
            _Merged([pair_payload(), _GatherIci([a[None] for a in parts])]))
        pair_done(landed[:n_pair])
        items = take_chip()
        g_gate, landed = _tn_matmul(dgate, h, f"dw_ffn1_w_gate_{l}",
                                    _Merged([chip_payload(items), _GatherForward(landed[n_pair:])]))
        chip_done(items, landed[:len(items)])
        gathered = landed[len(items):]
        to_pair.append((("ffn1_w_gate", l), g_gate))
        g_up, received = _tn_matmul(dup, h, f"dw_ffn1_w_up_{l}", pair_payload())
        pair_done(received)
        to_pair.append((("ffn1_w_up", l), g_up))
        items = take_chip()
        g_down, landed = _tn_matmul(act, dy, f"dw_ffn1_w_down_{l}", _Merged([chip_payload(items), pair_payload()]))
        chip_done(items, landed[:len(items)])
        pair_done(landed[len(items):])
        to_pair.append((("ffn1_w_down", l), g_down))
    grad_x = dx.reshape(x.shape)
    pair_done(_comm(pair_payload(), "rs_pair_exchange_last"))
    items = take_chip()
    (late_norm,), landed = _all_gather([jnp.pad(dg_ffn1, ((0, 7), (0, 0)))[None]], "ag_tail", chip_payload(items))
    chip_done(items, landed)

    summed = [_sum_blocks(g[0], N_DEV, f"sum_small_{k}") for k, g in enumerate(gathered)]
    late_sum = _sum_blocks(late_norm[0], N_DEV, "sum_small_late")
    norm_sum = summed[0]
    loss = norm_sum[3 * depth + 1, 0]
    cpos = lax.axis_index("x") * 4 + lax.axis_index("y") * 2 + lax.axis_index("c")
    sg = {nm: [] for nm in names}
    for l in range(depth):
        g384, gws, gpool = summed[1 + 3 * l], summed[2 + 3 * l].reshape(n_head + 1, CHUNK, CHUNK), summed[3 + 3 * l]
        sg["ffn1_norm"].append(norm_sum[3 * l] if l > 0 else late_sum[0])
        sg["mix_norm"].append(norm_sum[3 * l + 1])
        sg["ffn2_norm"].append(norm_sum[3 * l + 2])
        sg["sgu_ln_g"].append(g384[_R_SGU_G])
        sg["sgu_ln_b"].append(g384[_R_SGU_B])
        sg["conv_b"].append(g384[_R_CONV_B])
        sg["conv_ln_g"].append(g384[_R_CLN_G])
        sg["conv_ln_b"].append(g384[_R_CLN_B])
        sg["conv_w"].append(lax.dynamic_slice_in_dim(g384[_R_CONV_W:_R_CONV_W + CONV_WIDTH], cpos * cw_shard,
                                                     cw_shard, axis=1))
        sg["w_spatial"].append(gws[:n_head])
        sg["b_spatial"].append(gws[n_head][:, :n_head].T)
        sg["pool_w"].append(jnp.stack([gpool[k * HEAD_DIM:(k + 1) * HEAD_DIM, k * HEAD_DIM:(k + 1) * HEAD_DIM]
                                       for k in range(pool // HEAD_DIM)], axis=0))
        sg["pool_scale"].append(gpool[pool])
    small_names = ["ffn1_norm", "mix_norm", "sgu_ln_g", "sgu_ln_b", "w_spatial", "b_spatial", "conv_w", "conv_b",
                   "conv_ln_g", "conv_ln_b", "pool_w", "pool_scale", "ffn2_norm"]
    grads = {nm: jnp.stack(sg[nm], axis=0) for nm in small_names}
    grads["final_norm"] = norm_sum[3 * depth]

    delta, new_m, new_v = {}, {}, {}
    big_names = ["ffn1_w_gate", "ffn1_w_up", "ffn1_w_down", "w_in", "w_out", "ffn2_w_gate", "ffn2_w_up",
                 "ffn2_w_down"]
    transposed = {"ffn1_w_gate", "ffn1_w_up", "w_in", "ffn2_w_gate", "ffn2_w_up"}
    for nm in big_names:
        view = (lambda a: jnp.swapaxes(a, 1, 2)) if nm in transposed else (lambda a: a)
        outs = _finish_sharded([from_chips[(nm, l)] for l in range(depth)], view(W[nm]), view(M[nm]), view(V[nm]),
                               f"adamw_{nm}")
        grads[nm], delta[nm], new_m[nm], new_v[nm] = (view(o) for o in outs)
    snames = small_names + ["final_norm"]

    def flat2(a):
        return a.reshape(-1, a.shape[-1])

    outs = _adamw_small([flat2(W[nm]) for nm in snames], [flat2(grads[nm]) for nm in snames],
                        [flat2(M[nm]) for nm in snames], [flat2(V[nm]) for nm in snames], "adamw_small")
    ns = len(snames)
    for k, nm in enumerate(snames):
        shp = W[nm].shape
        delta[nm], new_m[nm], new_v[nm] = (outs[k].reshape(shp), outs[ns + k].reshape(shp),
                                           outs[2 * ns + k].reshape(shp))

    return (loss, grad_x, *[grads[nm] for nm in names], *[delta[nm] for nm in names],
            *[new_m[nm] for nm in names], *[new_v[nm] for nm in names])
```

```python
import functools

import jax
import jax.numpy as jnp
from jax import lax
from jax.experimental import pallas as pl
from jax.experimental.pallas import tpu as pltpu

F32 = jnp.float32
BF16 = jnp.bfloat16
EPS = 1e-6
N_DEV = 8
N_CHIP = 4
MESH = pl.DeviceIdType.MESH
ANY = pl.BlockSpec(memory_space=pl.ANY)

VMEM_LIMIT_BYTES = 56 * 1024 * 1024
LANES = 128
HALO = 32
HEAD_DIM = 64
CHUNK = 128
CONV_WIDTH = 31
POOL_WINDOWS = (2, 4, 8, 16)

ADAM_LR = 0.001
ADAM_B1 = 0.9
ADAM_B2 = 0.999
ADAM_EPS = 1e-08
ADAM_WD = 0.01
ADAM_STEP = 10


def _cparams(sem=None):
    return pltpu.CompilerParams(dimension_semantics=sem, vmem_limit_bytes=VMEM_LIMIT_BYTES)


def _position():
    return lax.axis_index("x"), lax.axis_index("y"), lax.axis_index("c")


class _Copies:
    def __init__(self):
        self.local, self.sends, self.recvs = [], [], []

    def extend(self, other):
        self.local += other.local
        self.sends += other.sends
        self.recvs += other.recvs

    def start(self):
        for cp in self.local + self.sends:
            cp.start()

    def wait(self):
        for land, send_sems, recv_sems, k, peer in self.recvs:
            _remote(land, land, send_sems, recv_sems, k, peer).wait_recv()
        for cp in self.sends:
            cp.wait_send()
        for cp in self.local:
            cp.wait()


def _remote(src, dst, send_sems, recv_sems, k, to):
    return pltpu.make_async_remote_copy(src_ref=src, dst_ref=dst, send_sem=send_sems.at[k], recv_sem=recv_sems.at[k],
                                        device_id=to, device_id_type=MESH)


class _Payload:
    ins, out_shapes, aliases, n_remote, n_local = (), (), {}, 0, 0

    def sem_shapes(self):
        return [pltpu.SemaphoreType.DMA((max(self.n_remote, 1),)), pltpu.SemaphoreType.DMA((max(self.n_remote, 1),)),
                pltpu.SemaphoreType.DMA((max(self.n_local, 1),))]


class _GatherIci(_Payload):
    def __init__(self, shards):
        self.ins = list(shards)
        self.out_shapes = [jax.ShapeDtypeStruct((s.shape[0], N_DEV * s.shape[1], s.shape[2]), s.dtype) for s in shards]
        self.n_remote, self.n_local = 4 * len(shards), len(shards)

    def build(self, ins, outs, send_sems, recv_sems, local_sems, k0=0, l0=0):
        x, y, c = _position()
        peers = [(x, y, 1 - c), (1 - x, y, c), (x, 1 - y, c), (1 - x, 1 - y, c)]
        cps = _Copies()
        for a, (src, out) in enumerate(zip(ins, outs)):
            r = src.shape[1]

            def rows(px, py, pc, out=out, r=r):
                return out.at[:, pl.ds((4 * px + 2 * py + pc) * r, r), :]

            cps.local.append(pltpu.make_async_copy(src, rows(x, y, c), local_sems.at[l0 + a]))
            for k, peer in enumerate(peers):
                cps.sends.append(_remote(src, rows(x, y, c), send_sems, recv_sems, k0 + 4 * a + k, peer))
                cps.recvs.append((rows(*peer), send_sems, recv_sems, k0 + 4 * a + k, peer))
        return cps


class _GatherForward(_Payload):
    def __init__(self, partials):
        self.ins = list(partials)
        self.out_shapes = [jax.ShapeDtypeStruct(p.shape, p.dtype) for p in partials]
        self.aliases = {a: a for a in range(len(partials))}
        self.n_remote = 3 * len(partials)

    def build(self, ins, outs, send_sems, recv_sems, local_sems, k0=0, l0=0):
        x, y, c = _position()
        chips = [(1 - x, y), (x, 1 - y), (1 - x, 1 - y)]
        cps = _Copies()
        for a, out in enumerate(outs):
            r = out.shape[1] // N_DEV
            for k, (px, py) in enumerate(chips):
                mine = out.at[:, pl.ds((4 * px + 2 * py + c) * r, r), :]
                theirs = out.at[:, pl.ds((4 * px + 2 * py + 1 - c) * r, r), :]
                cps.sends.append(_remote(mine, mine, send_sems, recv_sems, k0 + 3 * a + k, (x, y, 1 - c)))
                cps.recvs.append((theirs, send_sems, recv_sems, k0 + 3 * a + k, (x, y, 1 - c)))
        return cps


class _PairExchange(_Payload):
    def __init__(self, grads):
        self.ins = list(grads)
        self.out_shapes = [jax.ShapeDtypeStruct((g.shape[0] // 2, g.shape[1]), g.dtype) for g in grads]
        self.n_remote = N_CHIP * len(grads)

    def build(self, ins, outs, send_sems, recv_sems, local_sems, k0=0, l0=0):
        x, y, c = _position()
        cps = _Copies()
        for a, (src, out) in enumerate(zip(ins, outs)):
            r = src.shape[0] // N_DEV
            for q in range(N_CHIP):
                land = out.at[pl.ds(q * r, r), :]
                cps.sends.append(_remote(src.at[pl.ds((2 * q + 1 - c) * r, r), :], land, send_sems, recv_sems,
                                         k0 + N_CHIP * a + q, (x, y, 1 - c)))
                cps.recvs.append((land, send_sems, recv_sems, k0 + N_CHIP * a + q, (x, y, 1 - c)))
        return cps


class _ChipExchange(_Payload):
    def __init__(self, sums):
        self.ins = list(sums)
        self.out_shapes = [jax.ShapeDtypeStruct(s.shape, s.dtype) for s in sums]
        self.n_remote, self.n_local = 3 * len(sums), len(sums)

    def build(self, ins, outs, send_sems, recv_sems, local_sems, k0=0, l0=0):
        x, y, c = _position()
        my_chip = 2 * x + y
        chips = [(1 - x, y), (x, 1 - y), (1 - x, 1 - y)]
        cps = _Copies()
        for a, (src, out) in enumerate(zip(ins, outs)):
            r = src.shape[0] // N_CHIP
            mine = out.at[pl.ds(my_chip * r, r), :]
            cps.local.append(pltpu.make_async_copy(src.at[pl.ds(my_chip * r, r), :], mine, local_sems.at[l0 + a]))
            for k, (px, py) in enumerate(chips):
                land = out.at[pl.ds((2 * px + py) * r, r), :]
                cps.sends.append(_remote(src.at[pl.ds((2 * px + py) * r, r), :], mine, send_sems, recv_sems,
                                         k0 + 3 * a + k, (px, py, c)))
                cps.recvs.append((land, send_sems, recv_sems, k0 + 3 * a + k, (px, py, c)))
        return cps


class _Merged(_Payload):
    def __init__(self, parts):
        self.parts = list(parts)
        self.ins = [a for p in parts for a in p.ins]
        self.out_shapes = [s for p in parts for s in p.out_shapes]
        self.aliases, self.offsets = {}, []
        i0 = o0 = k0 = l0 = 0
        for p in parts:
            self.offsets.append((i0, o0, k0, l0))
            self.aliases.update({i0 + i: o0 + o for i, o in p.aliases.items()})
            i0, o0, k0, l0 = i0 + len(p.ins), o0 + len(p.out_shapes), k0 + p.n_remote, l0 + p.n_local
        self.n_remote, self.n_local = k0, l0

    def build(self, ins, outs, send_sems, recv_sems, local_sems):
        cps = _Copies()
        for p, (i0, o0, k0, l0) in zip(self.parts, self.offsets):
            cps.extend(p.build(ins[i0:i0 + len(p.ins)], outs[o0:o0 + len(p.out_shapes)], send_sems, recv_sems,
                               local_sems, k0, l0))
        return cps


def _call(body, name, grid, in_specs, out_specs, out_shape, scratch_shapes, semantics, args, payload=None):
    if payload is None:
        outs = pl.pallas_call(body, name=name, grid=grid, in_specs=in_specs, out_specs=out_specs,
                              out_shape=out_shape, scratch_shapes=scratch_shapes,
                              compiler_params=_cparams(semantics))(*args)
        return list(outs), []
    n_in, n_out, n_scr = len(in_specs), len(out_specs), len(scratch_shapes)
    p_in, p_out = len(payload.ins), len(payload.out_shapes)

    def carried(*refs):
        ins, p_ins = refs[:n_in], refs[n_in:n_in + p_in]
        o0 = n_in + p_in
        outs, p_outs = refs[o0:o0 + n_out], refs[o0 + n_out:o0 + n_out + p_out]
        s0 = o0 + n_out + p_out
        scr, sems = refs[s0:s0 + n_scr], refs[s0 + n_scr:]
        ids = [pl.program_id(k) for k in range(len(grid))]
        at_first = functools.reduce(jnp.logical_and, [i == 0 for i in ids])
        at_last = functools.reduce(jnp.logical_and, [i == g - 1 for i, g in zip(ids, grid)])

        @pl.when(at_first)
        def _():
            payload.build(p_ins, p_outs, *sems).start()

        body(*ins, *outs, *scr)

        @pl.when(at_last)
        def _():
            payload.build(p_ins, p_outs, *sems).wait()

    outs = pl.pallas_call(
        carried, name=name, grid=grid, in_specs=list(in_specs) + [ANY] * p_in,
        out_specs=list(out_specs) + [ANY] * p_out, out_shape=list(out_shape) + list(payload.out_shapes),
        scratch_shapes=list(scratch_shapes) + payload.sem_shapes(),
        input_output_aliases={n_in + i: n_out + o for i, o in payload.aliases.items()},
        compiler_params=_cparams(("arbitrary",) * len(grid)))(*args, *payload.ins)
    return list(outs[:n_out]), list(outs[n_out:])


def _comm(payload, name):
    def body(*refs):
        p_in, p_out = len(payload.ins), len(payload.out_shapes)
        cps = payload.build(refs[:p_in], refs[p_in:p_in + p_out], *refs[p_in + p_out:])
        cps.start()
        cps.wait()

    return list(pl.pallas_call(
        body, name=name, in_specs=[ANY] * len(payload.ins), out_specs=[ANY] * len(payload.out_shapes),
        out_shape=list(payload.out_shapes), scratch_shapes=payload.sem_shapes(),
        input_output_aliases=dict(payload.aliases))(*payload.ins))


def _dot(a, b):
    return jnp.dot(a, b, preferred_element_type=F32)


def _dot_nt(a, b):
    return lax.dot_general(a, b, (((1,), (1,)), ((), ())), preferred_element_type=F32)


def _dot_tn(a, b):
    return lax.dot_general(a, b, (((0,), (0,)), ((), ())), preferred_element_type=F32)


def _split_dot(x, e):
    hi = x.astype(BF16)
    r1 = x - hi.astype(F32)
    mid = r1.astype(BF16)
    lo = (r1 - mid.astype(F32)).astype(BF16)
    return _dot(hi, e) + _dot(mid, e) + _dot(lo, e)


def _rms(x):
    rstd = lax.rsqrt(jnp.mean(x * x, axis=-1, keepdims=True) + EPS)
    return x * rstd, rstd


def _rms_bwd(xhat, rstd, g, dh):
    dxhat = dh * g
    dx = rstd * (dxhat - xhat * jnp.mean(dxhat * xhat, axis=-1, keepdims=True))
    return dx, jnp.sum(dh * xhat, axis=0, keepdims=True)


def _ln(v):
    mu = jnp.mean(v, axis=-1, keepdims=True)
    xc = v - mu
    rstd = lax.rsqrt(jnp.mean(xc * xc, axis=-1, keepdims=True) + EPS)
    return xc * rstd, rstd


def _ln_bwd(vhat, rstd, g, dy):
    dvhat = dy * g
    dv = rstd * (dvhat - jnp.mean(dvhat, axis=-1, keepdims=True)
                 - vhat * jnp.mean(dvhat * vhat, axis=-1, keepdims=True))
    return dv, jnp.sum(dy * vhat, axis=0, keepdims=True), jnp.sum(dy, axis=0, keepdims=True)


_INV_SQRT2 = 0.7071067811865476
_INV_SQRT2PI = 0.3989422804014327


def _gelu(x):
    return 0.5 * x * (1.0 + lax.erf(x * _INV_SQRT2))


def _gelu_grad(x):
    return 0.5 * (1.0 + lax.erf(x * _INV_SQRT2)) + x * jnp.exp(-0.5 * x * x) * _INV_SQRT2PI


def _silu_grad(x):
    s = jax.nn.sigmoid(x)
    return s * (1.0 + x * (1.0 - s))


def _ffn_fwd(x, g, wa, mi, name, payload=None):
    t, d = x.shape
    f = wa.shape[1]
    tm, tf = 1024, 256
    nc = f // tf
    groups = [slice(k * (tm // 2), (k + 1) * (tm // 2)) for k in range(2)]

    def body(x_ref, g_ref, wgu_ref, wd_ref, xo_ref, gate_ref, up_ref, h_scr, acc_scr):
        c = pl.program_id(1)

        @pl.when(c == 0)
        def _():
            xhat, _ = _rms(x_ref[...])
            h_scr[...] = (xhat * g_ref[...]).astype(BF16)
            acc_scr[...] = jnp.zeros_like(acc_scr)

        wgu, wd = wgu_ref[...].reshape(2 * tf, d), wd_ref[...]
        for rows in groups:
            gu = _dot_nt(h_scr[rows, :], wgu)
            gate, up = gu[:, :tf], gu[:, tf:]
            gate_ref[rows, :] = gate.astype(BF16)
            up_ref[rows, :] = up.astype(BF16)
            act = (gate * jax.nn.sigmoid(gate) * up).astype(BF16)
            acc_scr[rows, :] += _dot(act, wd)

        @pl.when(c == nc - 1)
        def _():
            xo_ref[...] = x_ref[...] + 0.5 * acc_scr[...]

    assert mi % 2 == 0
    return _call(
        body, name, (t // tm, nc),
        [pl.BlockSpec((tm, d), lambda i, c: (i, 0)), pl.BlockSpec((1, d), lambda i, c: (0, 0)),
         pl.BlockSpec((2, tf, d), lambda i, c: (mi // 2, c, 0)),
         pl.BlockSpec((None, tf, d), lambda i, c: (mi + 2, c, 0))],
        [pl.BlockSpec((tm, d), lambda i, c: (i, 0)), pl.BlockSpec((tm, tf), lambda i, c: (i, c)),
         pl.BlockSpec((tm, tf), lambda i, c: (i, c))],
        [jax.ShapeDtypeStruct((t, d), F32), jax.ShapeDtypeStruct((t, f), BF16), jax.ShapeDtypeStruct((t, f), BF16)],
        [pltpu.VMEM((tm, d), BF16), pltpu.VMEM((tm, d), F32)],
        ("parallel", "arbitrary"), (x, g, wa, wa), payload)


def _ffn_bwd(x, g, dxo, gate, up, wa, mi, name, payload=None):
    t, d = x.shape
    f = wa.shape[1]
    tm, tf = 1024, 256
    nc = f // tf
    groups = [slice(k * (tm // 2), (k + 1) * (tm // 2)) for k in range(2)]

    def body(x_ref, g_ref, dxo_ref, gate_ref, up_ref, wgu_ref, wd_ref,
             dx_ref, dgate_ref, dup_ref, act_ref, h_ref, dy_ref, dg_ref, acc_scr):
        i, c = pl.program_id(0), pl.program_id(1)

        @pl.when(c == 0)
        def _():
            xhat, _ = _rms(x_ref[...])
            h_ref[...] = (xhat * g_ref[...]).astype(BF16)
            dy_ref[...] = (0.5 * dxo_ref[...]).astype(BF16)
            acc_scr[...] = jnp.zeros_like(acc_scr)

        @pl.when((c == 0) & (i == 0))
        def _():
            dg_ref[...] = jnp.zeros_like(dg_ref)

        wg, wu, wd = wgu_ref[0], wgu_ref[1], wd_ref[...]
        for rows in groups:
            gt = gate_ref[rows, :].astype(F32)
            u = up_ref[rows, :].astype(F32)
            s = jax.nn.sigmoid(gt)
            silu = gt * s
            dact = _dot_nt(dy_ref[rows, :], wd)
            dgate = (dact * u * (s * (1.0 + gt * (1.0 - s)))).astype(BF16)
            dup = (dact * silu).astype(BF16)
            dgate_ref[rows, :] = dgate
            dup_ref[rows, :] = dup
            act_ref[rows, :] = (silu * u).astype(BF16)
            acc_scr[rows, :] += _dot(dgate, wg) + _dot(dup, wu)

        @pl.when(c == nc - 1)
        def _():
            xhat, rstd = _rms(x_ref[...])
            dxn, dg = _rms_bwd(xhat, rstd, g_ref[...], acc_scr[...])
            dx_ref[...] = dxo_ref[...] + dxn
            dg_ref[...] += dg

    assert mi % 2 == 0
    row = pl.BlockSpec((tm, d), lambda i, c: (i, 0))
    col = pl.BlockSpec((tm, tf), lambda i, c: (i, c))
    vec = pl.BlockSpec((1, d), lambda i, c: (0, 0))
    return _call(
        body, name, (t // tm, nc),
        [row, vec, row, col, col, pl.BlockSpec((2, tf, d), lambda i, c: (mi // 2, c, 0)),
         pl.BlockSpec((None, tf, d), lambda i, c: (mi + 2, c, 0))],
        [row, col, col, col, row, row, vec],
        [jax.ShapeDtypeStruct((t, d), F32), jax.ShapeDtypeStruct((t, f), BF16),
         jax.ShapeDtypeStruct((t, f), BF16), jax.ShapeDtypeStruct((t, f), BF16),
         jax.ShapeDtypeStruct((t, d), BF16), jax.ShapeDtypeStruct((t, d), BF16),
         jax.ShapeDtypeStruct((1, d), F32)],
        [pltpu.VMEM((tm, d), F32)],
        ("arbitrary", "arbitrary"), (x, g, dxo, gate, up, wa, wa), payload)


def _ffn_dw(dgate, dup, act, h, dy, name, payload=None):
    t, f = dgate.shape
    d = h.shape[1]
    tk = 256
    tmm = f // 2
    nk = t // tk

    def body(dg_ref, du_ref, a_ref, h_ref, dy_ref, og_ref, ou_ref, od_ref, acc_g, acc_u, acc_d):
        k = pl.program_id(1)

        @pl.when(k == 0)
        def _():
            acc_g[...] = jnp.zeros_like(acc_g)
            acc_u[...] = jnp.zeros_like(acc_u)
            acc_d[...] = jnp.zeros_like(acc_d)

        hv = h_ref[...]
        acc_g[...] += _dot_tn(dg_ref[...], hv)
        acc_u[...] += _dot_tn(du_ref[...], hv)
        acc_d[...] += _dot_tn(a_ref[...], dy_ref[...])

        @pl.when(k == nk - 1)
        def _():
            og_ref[...] = acc_g[...].astype(BF16)
            ou_ref[...] = acc_u[...].astype(BF16)
            od_ref[...] = acc_d[...].astype(BF16)

    col = pl.BlockSpec((tk, tmm), lambda j, k: (k, j))
    row = pl.BlockSpec((tk, d), lambda j, k: (k, 0))
    out = pl.BlockSpec((tmm, d), lambda j, k: (j, 0))
    return _call(
        body, name, (f // tmm, nk), [col, col, col, row, row], [out, out, out],
        [jax.ShapeDtypeStruct((f, d), BF16)] * 3, [pltpu.VMEM((tmm, d), F32)] * 3,
        ("parallel", "arbitrary"), (dgate, dup, act, h, dy), payload)


def _tn_matmul(a, b, name, payload=None):
    t, m = a.shape
    n = b.shape[1]
    tk = 1024
    tmm = m // 2 if (m // 2) % LANES == 0 else m
    nk = t // tk

    def body(a_ref, b_ref, o_ref, acc_scr):
        k = pl.program_id(1)

        @pl.when(k == 0)
        def _():
            acc_scr[...] = jnp.zeros_like(acc_scr)

        acc_scr[...] += _dot_tn(a_ref[...].astype(BF16), b_ref[...].astype(BF16))

        @pl.when(k == nk - 1)
        def _():
            o_ref[...] = acc_scr[...].astype(BF16)

    (out,), p_outs = _call(
        body, name, (m // tmm, nk),
        [pl.BlockSpec((tk, tmm), lambda j, k: (k, j)), pl.BlockSpec((tk, n), lambda j, k: (k, 0))],
        [pl.BlockSpec((tmm, n), lambda j, k: (j, 0))],
        [jax.ShapeDtypeStruct((m, n), BF16)],
        [pltpu.VMEM((tmm, n), F32)],
        ("parallel", "arbitrary"), (a, b), payload)
    return out, p_outs


def _proj_in_fwd(x, g, wb, li, name, payload=None):
    t, d = x.shape
    n = wb.shape[1]
    tm = 512

    def body(x_ref, g_ref, w_ref, z_ref):
        xhat, _ = _rms(x_ref[...])
        z_ref[...] = _dot_nt((xhat * g_ref[...]).astype(BF16), w_ref[...])

    (z,), p_outs = _call(
        body, name, (t // tm,),
        [pl.BlockSpec((tm, d), lambda i: (i, 0)), pl.BlockSpec((1, d), lambda i: (0, 0)),
         pl.BlockSpec((None, n, d), lambda i: (li, 0, 0))],
        [pl.BlockSpec((tm, n), lambda i: (i, 0))],
        [jax.ShapeDtypeStruct((t, n), F32)], [], ("parallel",), (x, g, wb), payload)
    return z, p_outs


def _proj_out_fwd(x, cat, wc, li, name, payload=None):
    t, d = x.shape
    tm = 512

    def body(x_ref, cat_ref, w_ref, xo_ref):
        xo_ref[...] = x_ref[...] + _dot(cat_ref[...], w_ref[...])

    (xo,), p_outs = _call(
        body, name, (t // tm,),
        [pl.BlockSpec((tm, d), lambda i: (i, 0)), pl.BlockSpec((tm, d), lambda i: (i, 0)),
         pl.BlockSpec((None, d, d), lambda i: (li, 0, 0))],
        [pl.BlockSpec((tm, d), lambda i: (i, 0))],
        [jax.ShapeDtypeStruct((t, d), F32)], [], ("parallel",), (x, cat, wc), payload)
    return xo, p_outs


def _proj_out_bwd(dxo, wc, li, name):
    t, d = dxo.shape
    tm = 512

    def body(dxo_ref, w_ref, dcat_ref):
        dcat_ref[...] = _dot_nt(dxo_ref[...].astype(BF16), w_ref[...])

    return pl.pallas_call(
        body, name=name, grid=(t // tm,),
        in_specs=[pl.BlockSpec((tm, d), lambda i: (i, 0)), pl.BlockSpec((None, d, d), lambda i: (li, 0, 0))],
        out_specs=pl.BlockSpec((tm, d), lambda i: (i, 0)),
        out_shape=jax.ShapeDtypeStruct((t, d), F32),
        compiler_params=_cparams(("parallel",)),
    )(dxo, wc)


def _proj_in_bwd(x, g, dxo, dz, wb, li, name):
    t, d = x.shape
    n = wb.shape[1]
    tm = 512

    def body(x_ref, g_ref, dxo_ref, dz_ref, w_ref, dx_ref, h_ref, dg_ref):
        i = pl.program_id(0)

        @pl.when(i == 0)
        def _():
            dg_ref[...] = jnp.zeros_like(dg_ref)

        xhat, rstd = _rms(x_ref[...])
        h_ref[...] = (xhat * g_ref[...]).astype(BF16)
        dh = _dot(dz_ref[...], w_ref[...])
        dxn, dg = _rms_bwd(xhat, rstd, g_ref[...], dh)
        dx_ref[...] = dxo_ref[...] + dxn
        dg_ref[...] += dg

    row = pl.BlockSpec((tm, d), lambda i: (i, 0))
    vec = pl.BlockSpec((1, d), lambda i: (0, 0))
    return pl.pallas_call(
        body, name=name, grid=(t // tm,),
        in_specs=[row, vec, row, pl.BlockSpec((tm, n), lambda i: (i, 0)),
                  pl.BlockSpec((None, n, d), lambda i: (li, 0, 0))],
        out_specs=[row, row, vec],
        out_shape=[jax.ShapeDtypeStruct((t, d), F32), jax.ShapeDtypeStruct((t, d), BF16),
                   jax.ShapeDtypeStruct((1, d), F32)],
        compiler_params=_cparams(("arbitrary",)),
    )(x, g, dxo, dz, wb)


def _lane_ids(shape):
    return lax.broadcasted_iota(jnp.int32, shape, 1)


def _tril(w):
    r = lax.broadcasted_iota(jnp.int32, w.shape, 0)
    c = lax.broadcasted_iota(jnp.int32, w.shape, 1)
    return jnp.where(r >= c, w, 0.0)


def _shift_down(x, k):
    return x if k == 0 else pltpu.roll(x, k, 0)


def _shift_up(x, k):
    return x if k == 0 else pltpu.roll(x, x.shape[0] - k, 0)


def _sub_tile_shifts(ext, shift):
    return [shift(ext, b) for b in range(8)]


def _tap(shifted, j, n_out, down):
    a, b = divmod(j, 8)
    r0 = HALO - 8 * a if down else 8 * a
    return shifted[b][r0:r0 + n_out]


def _depthwise(shifted, w, n_out, down):
    acc = None
    for j in range(CONV_WIDTH):
        term = _tap(shifted, j, n_out, down) * w[CONV_WIDTH - 1 - j:CONV_WIDTH - j]
        acc = term if acc is None else acc + term
    return acc


def _conv_wgrad(shifted, dhc, n_out):
    return [jnp.sum(_tap(shifted, CONV_WIDTH - 1 - k, n_out, True) * dhc, axis=0, keepdims=True)
            for k in range(CONV_WIDTH)]


def _window_sums(ext, shift):
    s2 = ext + shift(ext, 1)
    s4 = s2 + shift(s2, 2)
    s8 = s4 + shift(s4, 4)
    s16 = s8 + shift(s8, 8)
    grp = _lane_ids(ext.shape) // HEAD_DIM
    return jnp.where(grp == 0, s2, jnp.where(grp == 1, s4, jnp.where(grp == 2, s8, s16)))


def _pool_count(t0, n, width):
    pos = (lax.broadcasted_iota(jnp.int32, (n, width), 0) + (t0 + 1)).astype(F32)
    grp = _lane_ids((n, width)) // HEAD_DIM
    win = jnp.where(grp == 0, 2.0, jnp.where(grp == 1, 4.0, jnp.where(grp == 2, 8.0, 16.0)))
    return jnp.minimum(pos, win)


def _block_diag(pw):
    gn, cg, _ = pw.shape
    rows = []
    for gi in range(gn):
        parts = [pw[gi] if gj == gi else jnp.zeros((cg, cg), pw.dtype) for gj in range(gn)]
        rows.append(jnp.concatenate(parts, axis=1))
    return jnp.concatenate(rows, axis=0)


def _head_pair_mix(w_even, w_odd, v):
    lo = _lane_ids((CHUNK, LANES)) < HEAD_DIM
    return jnp.where(lo, _dot(w_even, v), _dot(w_odd, v))


def _mix_fwd(z, p, name, payload=None):
    t, d_in = z.shape
    sgu = p["sgu_ln_g"].shape[1]
    pool = p["pool_scale"].shape[1]
    d_mix = 2 * sgu + pool
    tm = 512
    n_i = t // tm
    hb = tm // HALO

    def body(z_ref, zp_ref, lng_ref, lnb_ref, ws_ref, bs_ref, cw_ref, cb_ref, clg_ref, clb_ref,
             bd_ref, ps_ref, cat_ref):
        i = pl.program_id(0)
        first = i == 0
        z_main = z_ref[...]
        z_prev = jnp.where(first, 0.0, zp_ref[...])

        lng, lnb = lng_ref[...], lnb_ref[...]
        wt = [_tril(ws_ref[h]).astype(BF16) for h in range(sgu // HEAD_DIM)]
        for n in range(tm // CHUNK):
            rows = slice(n * CHUNK, (n + 1) * CHUNK)
            u = _gelu(z_main[rows, 0:sgu])
            vhat, _ = _ln(_gelu(z_main[rows, sgu:2 * sgu]))
            vn = (vhat * lng + lnb).astype(BF16)
            for gp in range(sgu // LANES):
                ls = slice(gp * LANES, (gp + 1) * LANES)
                mixed = _head_pair_mix(wt[2 * gp], wt[2 * gp + 1], vn[:, ls]) + bs_ref[:, ls]
                cat_ref[rows, ls] = (u[:, ls] * mixed).astype(BF16)

        def glu(zz):
            return zz[:, 2 * sgu:3 * sgu] * jax.nn.sigmoid(zz[:, 3 * sgu:4 * sgu])

        ext = jnp.concatenate([glu(z_prev), glu(z_main)], axis=0)
        hc = _depthwise(_sub_tile_shifts(ext, _shift_down), cw_ref[...], tm, True) + cb_ref[...]
        hhat, _ = _ln(hc)
        bn = hhat * clg_ref[...] + clb_ref[...]
        cat_ref[:, sgu:2 * sgu] = (bn * jax.nn.sigmoid(bn)).astype(BF16)

        pext = jnp.concatenate([z_prev[:, 4 * sgu:], z_main[:, 4 * sgu:]], axis=0)
        sums = _window_sums(pext, _shift_down)[HALO:]
        pooled = sums / _pool_count(i * tm, tm, pool) - z_main[:, 4 * sgu:]
        mixed_c = _dot(pooled.astype(BF16), bd_ref[...].astype(BF16))
        cat_ref[:, 2 * sgu:] = (mixed_c * ps_ref[...]).astype(BF16)

    def vec(n):
        return pl.BlockSpec((1, n), lambda i: (0, 0))

    (cat,), p_outs = _call(
        body, name, (n_i,),
        [pl.BlockSpec((tm, d_in), lambda i: (i, 0)),
         pl.BlockSpec((HALO, d_in), lambda i: (jnp.maximum(i * hb - 1, 0), 0)),
         vec(sgu), vec(sgu),
         pl.BlockSpec(p["w_spatial"].shape, lambda i: (0, 0, 0)),
         pl.BlockSpec((CHUNK, sgu), lambda i: (0, 0)),
         pl.BlockSpec((CONV_WIDTH, sgu), lambda i: (0, 0)),
         vec(sgu), vec(sgu), vec(sgu),
         pl.BlockSpec((pool, pool), lambda i: (0, 0)), vec(pool)],
        [pl.BlockSpec((tm, d_mix), lambda i: (i, 0))],
        [jax.ShapeDtypeStruct((t, d_mix), BF16)], [], ("parallel",),
        (z, z, p["sgu_ln_g"], p["sgu_ln_b"], p["w_spatial"], p["bs_full"], p["conv_w"], p["conv_b"],
         p["conv_ln_g"], p["conv_ln_b"], p["bd"], p["pool_scale"]), payload)
    return cat, p_outs


_R_SGU_G, _R_SGU_B, _R_CONV_B, _R_CLN_G, _R_CLN_B, _R_CONV_W = 0, 1, 2, 3, 4, 8
_R384_ROWS = 40


def _mix_bwd(z, dcat, p, name, payload=None):
    t, d_in = z.shape
    sgu = p["sgu_ln_g"].shape[1]
    pool = p["pool_scale"].shape[1]
    d_mix = 2 * sgu + pool
    n_head = sgu // HEAD_DIM
    tm = 512
    n_i = t // tm
    hb = tm // HALO

    def body(z_ref, zp_ref, zn_ref, dc_ref, dcn_ref, lng_ref, lnb_ref, ws_ref, bs_ref, cw_ref, cb_ref,
             clg_ref, clb_ref, bd_ref, ps_ref, dz_ref, g384_ref, gws_ref, gpool_ref, dbs_scr):
        i = pl.program_id(0)
        first, last = i == 0, i == n_i - 1

        @pl.when(first)
        def _():
            g384_ref[...] = jnp.zeros_like(g384_ref)
            gws_ref[...] = jnp.zeros_like(gws_ref)
            gpool_ref[...] = jnp.zeros_like(gpool_ref)
            dbs_scr[...] = jnp.zeros_like(dbs_scr)

        z_main = z_ref[...]
        z_prev = jnp.where(first, 0.0, zp_ref[...])
        z_next = jnp.where(last, 0.0, zn_ref[...])
        dc_main = dc_ref[...]
        dc_next = jnp.where(last, 0.0, dcn_ref[...])

        lng, lnb = lng_ref[...], lnb_ref[...]
        wt = [_tril(ws_ref[h]) for h in range(n_head)]
        wt_b = [w.astype(BF16) for w in wt]
        wtt_b = [w.T.astype(BF16) for w in wt]
        lo = _lane_ids((CHUNK, LANES)) < HEAD_DIM
        d_lng = jnp.zeros((1, sgu), F32)
        d_lnb = jnp.zeros((1, sgu), F32)
        dws = [jnp.zeros((CHUNK, CHUNK), F32) for _ in range(n_head)]
        for n in range(tm // CHUNK):
            rows = slice(n * CHUNK, (n + 1) * CHUNK)
            au, av = z_main[rows, 0:sgu], z_main[rows, sgu:2 * sgu]
            u = _gelu(au)
            vhat, vrstd = _ln(_gelu(av))
            vn = (vhat * lng + lnb).astype(BF16)
            da = dc_main[rows, 0:sgu]
            dmixed = da * u
            dbs_scr[...] += dmixed
            dvn_parts, du_parts = [], []
            for gp in range(sgu // LANES):
                ls = slice(gp * LANES, (gp + 1) * LANES)
                vn_g = vn[:, ls]
                mixed = _head_pair_mix(wt_b[2 * gp], wt_b[2 * gp + 1], vn_g) + bs_ref[:, ls]
                du_parts.append(da[:, ls] * mixed)
                dm_g = dmixed[:, ls]
                dm_b = dm_g.astype(BF16)
                dvn_parts.append(jnp.where(lo, _dot(wtt_b[2 * gp], dm_b), _dot(wtt_b[2 * gp + 1], dm_b)))
                dws[2 * gp] = dws[2 * gp] + _dot_nt(jnp.where(lo, dm_g, 0.0).astype(BF16), vn_g)
                dws[2 * gp + 1] = dws[2 * gp + 1] + _dot_nt(jnp.where(lo, 0.0, dm_g).astype(BF16), vn_g)
            dvn = jnp.concatenate(dvn_parts, axis=1)
            du = jnp.concatenate(du_parts, axis=1)
            dv, dg_n, db_n = _ln_bwd(vhat, vrstd, lng, dvn)
            d_lng = d_lng + dg_n
            d_lnb = d_lnb + db_n
            dz_ref[rows, 0:sgu] = (du * _gelu_grad(au)).astype(BF16)
            dz_ref[rows, sgu:2 * sgu] = (dv * _gelu_grad(av)).astype(BF16)
        for h in range(n_head):
            gws_ref[h] += _tril(dws[h])
        g384_ref[_R_SGU_G:_R_SGU_G + 1, :] += d_lng
        g384_ref[_R_SGU_B:_R_SGU_B + 1, :] += d_lnb

        clg = clg_ref[...]
        bcols = slice(2 * sgu, 4 * sgu)
        zb = jnp.concatenate([z_prev[:, bcols], z_main[:, bcols], z_next[:, bcols]], axis=0)
        bval, bgate = zb[:, 0:sgu], zb[:, sgu:2 * sgu]
        sg = jax.nn.sigmoid(bgate)
        hglu = bval * sg
        n_out = tm + HALO
        hglu_shifts = _sub_tile_shifts(hglu, _shift_down)
        cw = cw_ref[...]
        hc = _depthwise(hglu_shifts, cw, n_out, True) + cb_ref[...]
        hhat, hrstd = _ln(hc)
        bn = hhat * clg + clb_ref[...]
        db = jnp.concatenate([dc_main[:, sgu:2 * sgu], dc_next[:, sgu:2 * sgu]], axis=0)
        dbn = db * _silu_grad(bn)
        dhc_all, _, _ = _ln_bwd(hhat, hrstd, clg, dbn)
        dbn_m, hhat_m, dhc = dbn[:tm], hhat[:tm], dhc_all[:tm]
        g384_ref[_R_CLN_G:_R_CLN_G + 1, :] += jnp.sum(dbn_m * hhat_m, axis=0, keepdims=True)
        g384_ref[_R_CLN_B:_R_CLN_B + 1, :] += jnp.sum(dbn_m, axis=0, keepdims=True)
        g384_ref[_R_CONV_B:_R_CONV_B + 1, :] += jnp.sum(dhc, axis=0, keepdims=True)
        wrows = _conv_wgrad(hglu_shifts, dhc, tm)
        for k in range(CONV_WIDTH):
            g384_ref[_R_CONV_W + k:_R_CONV_W + k + 1, :] += wrows[k]
        dhglu = _depthwise(_sub_tile_shifts(dhc_all, _shift_up), cw, tm, False)
        bval_m, sg_m = bval[HALO:HALO + tm], sg[HALO:HALO + tm]
        dz_ref[:, 2 * sgu:3 * sgu] = (dhglu * sg_m).astype(BF16)
        dz_ref[:, 3 * sgu:4 * sgu] = (dhglu * bval_m * sg_m * (1.0 - sg_m)).astype(BF16)

        bd_b = bd_ref[...].astype(BF16)
        ps = ps_ref[...]
        p_main = z_main[:, 4 * sgu:]
        pext = jnp.concatenate([z_prev[:, 4 * sgu:], p_main], axis=0)
        cnt = _pool_count(i * tm, n_out, pool)
        pooled = _window_sums(pext, _shift_down)[HALO:] / cnt[:tm] - p_main
        pooled_b = pooled.astype(BF16)
        dcc = jnp.concatenate([dc_main[:, 2 * sgu:], dc_next[:, 2 * sgu:]], axis=0)
        dmix_c = dcc * ps
        mixed_c = _dot(pooled_b, bd_b)
        grp_r = lax.broadcasted_iota(jnp.int32, (pool, pool), 0) // HEAD_DIM
        grp_c = lax.broadcasted_iota(jnp.int32, (pool, pool), 1) // HEAD_DIM
        gpool_ref[0:pool, :] += jnp.where(grp_r == grp_c, _dot_tn(pooled_b, dmix_c[:tm].astype(BF16)), 0.0)
        gpool_ref[pool:pool + 1, :] += jnp.sum(dcc[:tm] * mixed_c, axis=0, keepdims=True)
        dpooled = _dot_nt(dmix_c.astype(BF16), bd_b)
        q = dpooled / cnt
        dp = _window_sums(q, _shift_up)[:tm] - dpooled[:tm]
        dz_ref[:, 4 * sgu:] = dp.astype(BF16)

        @pl.when(last)
        def _():
            r = lax.broadcasted_iota(jnp.int32, (sgu, LANES), 0)
            c = lax.broadcasted_iota(jnp.int32, (sgu, LANES), 1)
            sel = (r // HEAD_DIM == c).astype(BF16)
            gws_ref[n_head] = _split_dot(dbs_scr[...], sel)

    def vec(n):
        return pl.BlockSpec((1, n), lambda i: (0, 0))

    def prev_map(i):
        return (jnp.maximum(i * hb - 1, 0), 0)

    def next_map(i):
        return (jnp.minimum((i + 1) * hb, n_i * hb - 1), 0)

    return _call(
        body, name, (n_i,),
        [pl.BlockSpec((tm, d_in), lambda i: (i, 0)),
         pl.BlockSpec((HALO, d_in), prev_map), pl.BlockSpec((HALO, d_in), next_map),
         pl.BlockSpec((tm, d_mix), lambda i: (i, 0)), pl.BlockSpec((HALO, d_mix), next_map),
         vec(sgu), vec(sgu),
         pl.BlockSpec(p["w_spatial"].shape, lambda i: (0, 0, 0)),
         pl.BlockSpec((CHUNK, sgu), lambda i: (0, 0)),
         pl.BlockSpec((CONV_WIDTH, sgu), lambda i: (0, 0)),
         vec(sgu), vec(sgu), vec(sgu),
         pl.BlockSpec((pool, pool), lambda i: (0, 0)), vec(pool)],
        [pl.BlockSpec((tm, d_in), lambda i: (i, 0)),
         pl.BlockSpec((_R384_ROWS, sgu), lambda i: (0, 0)),
         pl.BlockSpec((n_head + 1, CHUNK, CHUNK), lambda i: (0, 0, 0)),
         pl.BlockSpec((pool + 8, pool), lambda i: (0, 0))],
        [jax.ShapeDtypeStruct((t, d_in), BF16),
         jax.ShapeDtypeStruct((_R384_ROWS, sgu), F32),
         jax.ShapeDtypeStruct((n_head + 1, CHUNK, CHUNK), F32),
         jax.ShapeDtypeStruct((pool + 8, pool), F32)],
        [pltpu.VMEM((CHUNK, sgu), F32)], ("arbitrary",),
        (z, z, z, dcat, dcat, p["sgu_ln_g"], p["sgu_ln_b"], p["w_spatial"], p["bs_full"], p["conv_w"],
         p["conv_b"], p["conv_ln_g"], p["conv_ln_b"], p["bd"], p["pool_scale"]), payload)


def _loss_head(x, g, target, name):
    t, d = x.shape
    tm = 512

    def body(x_ref, g_ref, tgt_ref, dx_ref, dg_ref, loss_ref):
        i = pl.program_id(0)

        @pl.when(i == 0)
        def _():
            dg_ref[...] = jnp.zeros_like(dg_ref)
            loss_ref[...] = jnp.zeros_like(loss_ref)

        gv = g_ref[...]
        xhat, rstd = _rms(x_ref[...])
        err = xhat * gv - tgt_ref[...]
        loss_ref[...] += jnp.zeros_like(loss_ref) + 0.5 * jnp.sum(jnp.mean(err * err, axis=-1, keepdims=True))
        dxn, dg = _rms_bwd(xhat, rstd, gv, err * (1.0 / d))
        dx_ref[...] = dxn
        dg_ref[...] += dg

    row = pl.BlockSpec((tm, d), lambda i: (i, 0))
    vec = pl.BlockSpec((1, d), lambda i: (0, 0))
    return pl.pallas_call(
        body, name=name, grid=(t // tm,),
        in_specs=[row, vec, row],
        out_specs=[row, vec, pl.BlockSpec((1, LANES), lambda i: (0, 0))],
        out_shape=[jax.ShapeDtypeStruct((t, d), F32), jax.ShapeDtypeStruct((1, d), F32),
                   jax.ShapeDtypeStruct((1, LANES), F32)],
        compiler_params=_cparams(("arbitrary",)),
    )(x, g, target)


def _all_gather(arrs, name, extra=None):
    gather = _GatherIci(arrs)
    n = len(arrs)
    forward = _GatherForward([jax.ShapeDtypeStruct(s.shape, s.dtype) for s in gather.out_shapes])
    x_in = len(extra.ins) if extra else 0
    x_out = len(extra.out_shapes) if extra else 0

    def body(*refs):
        ins, x_ins = refs[:n], refs[n:n + x_in]
        outs, x_outs = refs[n + x_in:2 * n + x_in], refs[2 * n + x_in:2 * n + x_in + x_out]
        sems = refs[2 * n + x_in + x_out:]
        first = gather.build(ins, outs, *sems[0:3])
        first.start()
        if extra:
            beside = extra.build(x_ins, x_outs, *sems[6:9])
            beside.start()
        first.wait()
        second = forward.build(outs, outs, *sems[3:6])
        second.start()
        second.wait()
        if extra:
            beside.wait()

    outs = pl.pallas_call(
        body, name=name,
        in_specs=[ANY] * (n + x_in), out_specs=[ANY] * (n + x_out),
        out_shape=list(gather.out_shapes) + (list(extra.out_shapes) if extra else []),
        scratch_shapes=gather.sem_shapes() + forward.sem_shapes() + (extra.sem_shapes() if extra else []),
    )(*arrs, *(extra.ins if extra else []))
    return list(outs[:n]), list(outs[n:])


def _pair_sums(grads, recvs, cidx, name):
    n = len(grads)

    def body(c_ref, *refs):
        for g_ref, r_ref, o_ref in zip(refs[:n], refs[n:2 * n], refs[2 * n:]):
            o_ref[...] = (g_ref[...].astype(F32) + r_ref[...].astype(F32)).astype(BF16)

    shapes = [(g.shape[0] // N_DEV, g.shape[1]) for g in grads]
    return list(pl.pallas_call(
        body, name=name,
        grid_spec=pltpu.PrefetchScalarGridSpec(
            num_scalar_prefetch=1, grid=(N_CHIP,),
            in_specs=[pl.BlockSpec(s, lambda q, c: (2 * q + c[0], 0)) for s in shapes]
            + [pl.BlockSpec(s, lambda q, c: (q, 0)) for s in shapes],
            out_specs=[pl.BlockSpec(s, lambda q, c: (q, 0)) for s in shapes]),
        out_shape=[jax.ShapeDtypeStruct((N_CHIP * r, cols), BF16) for r, cols in shapes],
        compiler_params=_cparams(("parallel",)),
    )(cidx, *grads, *recvs))


def _sum_blocks(parts, nblk, name):
    r = parts.shape[0] // nblk
    cols = parts.shape[1]

    def body(p_ref, o_ref):
        acc = p_ref[0:r, :].astype(F32)
        for q in range(1, nblk):
            acc = acc + p_ref[q * r:(q + 1) * r, :].astype(F32)
        o_ref[...] = acc

    return pl.pallas_call(
        body, name=name,
        out_shape=jax.ShapeDtypeStruct((r, cols), F32),
        compiler_params=_cparams(),
    )(parts)


def _adamw_math(w, g, m, v):
    m = ADAM_B1 * m + (1.0 - ADAM_B1) * g
    v = ADAM_B2 * v + (1.0 - ADAM_B2) * (g * g)
    m_hat = m / (1.0 - ADAM_B1 ** ADAM_STEP)
    v_hat = v / (1.0 - ADAM_B2 ** ADAM_STEP)
    delta = -ADAM_LR * (m_hat / (jnp.sqrt(v_hat) + ADAM_EPS) + ADAM_WD * w)
    return delta, m, v


def _finish_sharded(parts, w, m, v, name):
    depth, rr, cw = w.shape

    def body(*refs):
        p_refs = refs[:depth]
        w_ref, m_ref, v_ref, g_ref, d_ref, mo_ref, vo_ref = refs[depth:]
        l = pl.program_id(0)
        for k in range(depth):
            @pl.when(l == k)
            def _(p_ref=p_refs[k]):
                r = p_ref.shape[0] // N_CHIP
                acc = p_ref[0:r, :].astype(F32)
                for q in range(1, N_CHIP):
                    acc = acc + p_ref[q * r:(q + 1) * r, :].astype(F32)
                g_ref[...] = acc
                d_ref[...], mo_ref[...], vo_ref[...] = _adamw_math(w_ref[...], acc, m_ref[...], v_ref[...])

    blk = pl.BlockSpec((None, rr, cw), lambda l: (l, 0, 0))
    return pl.pallas_call(
        body, name=name, grid=(depth,),
        in_specs=[pl.BlockSpec(p.shape, lambda l: (0, 0)) for p in parts] + [blk] * 3, out_specs=[blk] * 4,
        out_shape=[jax.ShapeDtypeStruct(w.shape, F32)] * 4,
        compiler_params=_cparams(("arbitrary",)),
    )(*parts, w, m, v)


def _adamw_small(ws, gs, ms, vs, name):
    n = len(ws)

    def body(*refs):
        for k in range(n):
            w_ref, g_ref, m_ref, v_ref = (refs[j * n + k] for j in range(4))
            d_ref, mo_ref, vo_ref = (refs[(4 + j) * n + k] for j in range(3))
            d_ref[...], mo_ref[...], vo_ref[...] = _adamw_math(w_ref[...], g_ref[...], m_ref[...], v_ref[...])

    shapes = [jax.ShapeDtypeStruct(w.shape, F32) for w in ws]
    return pl.pallas_call(
        body, name=name, out_shape=shapes * 3, compiler_params=_cparams(),
    )(*ws, *gs, *ms, *vs)


def kernel(x, ffn1_norm, ffn1_w_gate, ffn1_w_up, ffn1_w_down, mix_norm, w_in, sgu_ln_g, sgu_ln_b, w_spatial, b_spatial, conv_w, conv_b, conv_ln_g, conv_ln_b, pool_w, pool_scale, w_out, ffn2_norm, ffn2_w_gate, ffn2_w_up, ffn2_w_down, final_norm, loss_target, m_ffn1_norm, m_ffn1_w_gate, m_ffn1_w_up, m_ffn1_w_down, m_mix_norm, m_w_in, m_sgu_ln_g, m_sgu_ln_b, m_w_spatial, m_b_spatial, m_conv_w, m_conv_b, m_conv_ln_g, m_conv_ln_b, m_pool_w, m_pool_scale, m_w_out, m_ffn2_norm, m_ffn2_w_gate, m_ffn2_w_up, m_ffn2_w_down, m_final_norm, v_ffn1_norm, v_ffn1_w_gate, v_ffn1_w_up, v_ffn1_w_down, v_mix_norm, v_w_in, v_sgu_ln_g, v_sgu_ln_b, v_w_spatial, v_b_spatial, v_conv_w, v_conv_b, v_conv_ln_g, v_conv_ln_b, v_pool_w, v_pool_scale, v_w_out, v_ffn2_norm, v_ffn2_w_gate, v_ffn2_w_up, v_ffn2_w_down, v_final_norm):
    names = ["ffn1_norm", "ffn1_w_gate", "ffn1_w_up", "ffn1_w_down", "mix_norm", "w_in", "sgu_ln_g", "sgu_ln_b",
             "w_spatial", "b_spatial", "conv_w", "conv_b", "conv_ln_g", "conv_ln_b", "pool_w", "pool_scale",
             "w_out", "ffn2_norm", "ffn2_w_gate", "ffn2_w_up", "ffn2_w_down", "final_norm"]
    W = dict(zip(names, [ffn1_norm, ffn1_w_gate, ffn1_w_up, ffn1_w_down, mix_norm, w_in, sgu_ln_g, sgu_ln_b,
                         w_spatial, b_spatial, conv_w, conv_b, conv_ln_g, conv_ln_b, pool_w, pool_scale, w_out,
                         ffn2_norm, ffn2_w_gate, ffn2_w_up, ffn2_w_down, final_norm]))
    M = dict(zip(names, [m_ffn1_norm, m_ffn1_w_gate, m_ffn1_w_up, m_ffn1_w_down, m_mix_norm, m_w_in, m_sgu_ln_g,
                         m_sgu_ln_b, m_w_spatial, m_b_spatial, m_conv_w, m_conv_b, m_conv_ln_g, m_conv_ln_b,
                         m_pool_w, m_pool_scale, m_w_out, m_ffn2_norm, m_ffn2_w_gate, m_ffn2_w_up, m_ffn2_w_down,
                         m_final_norm]))
    V = dict(zip(names, [v_ffn1_norm, v_ffn1_w_gate, v_ffn1_w_up, v_ffn1_w_down, v_mix_norm, v_w_in, v_sgu_ln_g,
                         v_sgu_ln_b, v_w_spatial, v_b_spatial, v_conv_w, v_conv_b, v_conv_ln_g, v_conv_ln_b,
                         v_pool_w, v_pool_scale, v_w_out, v_ffn2_norm, v_ffn2_w_gate, v_ffn2_w_up, v_ffn2_w_down,
                         v_final_norm]))

    depth, d = ffn1_norm.shape
    t = x.shape[1]
    sgu = sgu_ln_g.shape[1]
    pool = pool_scale.shape[1]
    n_head = sgu // HEAD_DIM
    cw_shard = conv_w.shape[2]
    xs = x.reshape(t, d)
    target = loss_target.reshape(t, d)

    def tr(w):
        return jnp.swapaxes(w, 1, 2).astype(BF16)

    ffn_shards = [[jnp.stack([tr(ffn1_w_gate)[l], tr(ffn1_w_up)[l], ffn1_w_down[l].astype(BF16)]),
                   jnp.stack([tr(ffn2_w_gate)[l], tr(ffn2_w_up)[l], ffn2_w_down[l].astype(BF16)])]
                  for l in range(depth)]
    win_shards = [tr(w_in)[l:l + 1] for l in range(depth)]
    wout_shards = [w_out[l:l + 1].astype(BF16) for l in range(depth)]
    cw_rows = depth * CONV_WIDTH
    cw_pad = -cw_rows % 8
    cw_send = jnp.pad(conv_w.reshape(cw_rows, cw_shard), ((0, cw_pad), (0, 0)))[None]
    wffn, wb, wc = {}, {}, {}
    (wffn[(0, 0)], wb[0], wc[0], cwg), _ = _all_gather(
        [ffn_shards[0][0], win_shards[0], wout_shards[0], cw_send], "ag_first")
    conv_w_full = cwg.reshape(N_DEV, cw_rows + cw_pad, cw_shard)[:, :cw_rows].reshape(
        N_DEV, depth, CONV_WIDTH, cw_shard).transpose(1, 2, 0, 3).reshape(depth, CONV_WIDTH, N_DEV * cw_shard)

    def mixer_params(l):
        return dict(
            sgu_ln_g=sgu_ln_g[l:l + 1], sgu_ln_b=sgu_ln_b[l:l + 1], w_spatial=w_spatial[l],
            bs_full=jnp.repeat(b_spatial[l].T, HEAD_DIM, axis=1),
            conv_w=conv_w_full[l], conv_b=conv_b[l:l + 1], conv_ln_g=conv_ln_g[l:l + 1],
            conv_ln_b=conv_ln_b[l:l + 1], bd=_block_diag(pool_w[l]), pool_scale=pool_scale[l:l + 1])

    saved = []
    cur = xs
    for l in range(depth):
        p = mixer_params(l)
        x0 = cur
        more = l + 1 < depth
        (x1, gate1, up1), part = _ffn_fwd(x0, ffn1_norm[l:l + 1], wffn[(l, 0)], 0, f"ffn1_fwd_{l}",
                                          _GatherIci([ffn_shards[l][1]]))
        z, (wffn[(l, 1)],) = _proj_in_fwd(x1, mix_norm[l:l + 1], wb[l], 0, f"proj_in_fwd_{l}", _GatherForward(part))
        cat, part = _mix_fwd(z, p, f"mix_fwd_{l}",
                             _GatherIci([win_shards[l + 1], wout_shards[l + 1]]) if more else None)
        x2, part = _proj_out_fwd(x1, cat, wc[l], 0, f"proj_out_fwd_{l}", _GatherForward(part) if more else None)
        if more:
            wb[l + 1], wc[l + 1] = part
        (x3, gate2, up2), part = _ffn_fwd(x2, ffn2_norm[l:l + 1], wffn[(l, 1)], 0, f"ffn2_fwd_{l}",
                                          _GatherIci([ffn_shards[l + 1][0]]) if more else None)
        if more:
            (wffn[(l + 1, 0)],) = _comm(_GatherForward(part), f"ag_forward_{l + 1}")
        saved.append((p, x0, gate1, up1, x1, z, cat, x2, gate2, up2))
        cur = x3

    dx, d_final, loss_part = _loss_head(cur, final_norm.reshape(1, d), target, "loss_head")

    cidx = lax.axis_index("c").astype(jnp.int32).reshape(1)
    from_chips = {}
    to_pair, to_chip = [], []
    small = []

    def pair_payload():
        return _PairExchange([g for _, g in to_pair]) if to_pair else None

    def pair_done(received):
        if to_pair:
            (nm, l), _ = to_pair[0]
            sums = _pair_sums([g for _, g in to_pair], list(received), cidx, f"rs_pair_sum_{nm}_{l}")
            to_chip.extend((key, s) for (key, _), s in zip(to_pair, sums))
        to_pair.clear()

    def take_chip():
        items = list(to_chip)
        to_chip.clear()
        return items

    def chip_payload(items):
        return _ChipExchange([s for _, s in items]) if items else None

    def chip_done(items, landed):
        for (key, _), o in zip(items, landed):
            from_chips[key] = o

    def ffn_weight_grads(prefix, l, dgate, dup, act, h, dy):
        items = take_chip()
        grads3, landed = _ffn_dw(dgate, dup, act, h, dy, f"dw_{prefix}_{l}", chip_payload(items))
        chip_done(items, landed)
        to_pair.extend(((f"{prefix}_{nm}", l), g) for nm, g in zip(("w_gate", "w_up", "w_down"), grads3))

    for l in reversed(range(depth)):
        p, x0, gate1, up1, x1, z, cat, x2, gate2, up2 = saved[l]
        (dx, dgate, dup, act, h, dy, dg_ffn2), received = _ffn_bwd(
            x2, ffn2_norm[l:l + 1], dx, gate2, up2, wffn[(l, 1)], 0, f"ffn2_bwd_{l}", pair_payload())
        pair_done(received)
        ffn_weight_grads("ffn2", l, dgate, dup, act, h, dy)
        g_out, received = _tn_matmul(cat, dx, f"dw_w_out_{l}", pair_payload())
        pair_done(received)
        dcat = _proj_out_bwd(dx, wc[l], 0, f"proj_out_bwd_{l}")
        items = take_chip()
        (dz, g384, gws, gpool), landed = _mix_bwd(z, dcat, p, f"mix_bwd_{l}", chip_payload(items))
        chip_done(items, landed)
        dx, hm, dg_mix = _proj_in_bwd(x1, mix_norm[l:l + 1], dx, dz, wb[l], 0, f"proj_in_bwd_{l}")
        g_in, _ = _tn_matmul(dz, hm, f"dw_w_in_{l}")
        to_pair.extend([(("w_out", l), g_out), (("w_in", l), g_in)])
        if l > 0:
            (dx, dgate, dup, act, h, dy, dg_ffn1), received = _ffn_bwd(
                x0, ffn1_norm[l:l + 1], dx, gate1, up1, wffn[(l, 0)], 0, f"ffn1_bwd_{l}", pair_payload())
            pair_done(received)
            ffn_weight_grads("ffn1", l, dgate, dup, act, h, dy)
            small.append((l, g384, gws, gpool, dg_ffn1, dg_mix, dg_ffn2))
            continue

        small.append((0, g384, gws, gpool, None, dg_mix, dg_ffn2))
        small.sort(key=lambda s: s[0])
        norm_rows = []
        for (sl, _, _, _, dg1, dgm, dg2) in small:
            norm_rows += [jnp.zeros((1, d), F32) if dg1 is None else dg1, dgm, dg2]
        norm_rows += [d_final, jnp.pad(loss_part, ((0, 0), (0, d - LANES)))]
        n_norm = len(norm_rows)
        norm_pack = jnp.concatenate(norm_rows + [jnp.zeros((8 - n_norm % 8, d), F32)] * (n_norm % 8 != 0), axis=0)
        parts = [norm_pack]
        for (_, s384, sws, spool, _, _, _) in small:
            parts += [s384, sws.reshape((n_head + 1) * CHUNK, CHUNK), spool]
        n_pair = len(to_pair)
        (dx, dgate, dup, act, h, dy, dg_ffn1), landed = _ffn_bwd(
            x0, ffn1_norm[l:l + 1], dx, gate1, up1, wffn[(l, 0)], 0, f"ffn1_bwd_{l}",
            _Merged([pair_payload(), _GatherIci([a[None] for a in parts])]))
        pair_done(landed[:n_pair])
        items = take_chip()
        g_gate, landed = _tn_matmul(dgate, h, f"dw_ffn1_w_gate_{l}",
                                    _Merged([chip_payload(items), _GatherForward(landed[n_pair:])]))
        chip_done(items, landed[:len(items)])
        gathered = landed[len(items):]
        to_pair.append((("ffn1_w_gate", l), g_gate))
        g_up, received = _tn_matmul(dup, h, f"dw_ffn1_w_up_{l}", pair_payload())
        pair_done(received)
        to_pair.append((("ffn1_w_up", l), g_up))
        items = take_chip()
        g_down, landed = _tn_matmul(act, dy, f"dw_ffn1_w_down_{l}", _Merged([chip_payload(items), pair_payload()]))
        chip_done(items, landed[:len(items)])
        pair_done(landed[len(items):])
        to_pair.append((("ffn1_w_down", l), g_down))
    grad_x = dx.reshape(x.shape)
    pair_done(_comm(pair_payload(), "rs_pair_exchange_last"))
    items = take_chip()
    (late_norm,), landed = _all_gather([jnp.pad(dg_ffn1, ((0, 7), (0, 0)))[None]], "ag_tail", chip_payload(items))
    chip_done(items, landed)

    summed = [_sum_blocks(g[0], N_DEV, f"sum_small_{k}") for k, g in enumerate(gathered)]
    late_sum = _sum_blocks(late_norm[0], N_DEV, "sum_small_late")
    norm_sum = summed[0]
    loss = norm_sum[3 * depth + 1, 0]
    cpos = lax.axis_index("x") * 4 + lax.axis_index("y") * 2 + lax.axis_index("c")
    sg = {nm: [] for nm in names}
    for l in range(depth):
        g384, gws, gpool = summed[1 + 3 * l], summed[2 + 3 * l].reshape(n_head + 1, CHUNK, CHUNK), summed[3 + 3 * l]
        sg["ffn1_norm"].append(norm_sum[3 * l] if l > 0 else late_sum[0])
        sg["mix_norm"].append(norm_sum[3 * l + 1])
        sg["ffn2_norm"].append(norm_sum[3 * l + 2])
        sg["sgu_ln_g"].append(g384[_R_SGU_G])
        sg["sgu_ln_b"].append(g384[_R_SGU_B])
        sg["conv_b"].append(g384[_R_CONV_B])
        sg["conv_ln_g"].append(g384[_R_CLN_G])
        sg["conv_ln_b"].append(g384[_R_CLN_B])
        sg["conv_w"].append(lax.dynamic_slice_in_dim(g384[_R_CONV_W:_R_CONV_W + CONV_WIDTH], cpos * cw_shard,
                                                     cw_shard, axis=1))
        sg["w_spatial"].append(gws[:n_head])
        sg["b_spatial"].append(gws[n_head][:, :n_head].T)
        sg["pool_w"].append(jnp.stack([gpool[k * HEAD_DIM:(k + 1) * HEAD_DIM, k * HEAD_DIM:(k + 1) * HEAD_DIM]
                                       for k in range(pool // HEAD_DIM)], axis=0))
        sg["pool_scale"].append(gpool[pool])
    small_names = ["ffn1_norm", "mix_norm", "sgu_ln_g", "sgu_ln_b", "w_spatial", "b_spatial", "conv_w", "conv_b",
                   "conv_ln_g", "conv_ln_b", "pool_w", "pool_scale", "ffn2_norm"]
    grads = {nm: jnp.stack(sg[nm], axis=0) for nm in small_names}
    grads["final_norm"] = norm_sum[3 * depth]

    delta, new_m, new_v = {}, {}, {}
    big_names = ["ffn1_w_gate", "ffn1_w_up", "ffn1_w_down", "w_in", "w_out", "ffn2_w_gate", "ffn2_w_up",
                 "ffn2_w_down"]
    transposed = {"ffn1_w_gate", "ffn1_w_up", "w_in", "ffn2_w_gate", "ffn2_w_up"}
    for nm in big_names:
        view = (lambda a: jnp.swapaxes(a, 1, 2)) if nm in transposed else (lambda a: a)
        outs = _finish_sharded([from_chips[(nm, l)] for l in range(depth)], view(W[nm]), view(M[nm]), view(V[nm]),
                               f"adamw_{nm}")
        grads[nm], delta[nm], new_m[nm], new_v[nm] = (view(o) for o in outs)
    snames = small_names + ["final_norm"]

    def flat2(a):
        return a.reshape(-1, a.shape[-1])

    outs = _adamw_small([flat2(W[nm]) for nm in snames], [flat2(grads[nm]) for nm in snames],
                        [flat2(M[nm]) for nm in snames], [flat2(V[nm]) for nm in snames], "adamw_small")
    ns = len(snames)
    for k, nm in enumerate(snames):
        shp = W[nm].shape
        delta[nm], new_m[nm], new_v[nm] = (outs[k].reshape(shp), outs[ns + k].reshape(shp),
                                           outs[2 * ns + k].reshape(shp))

    return (loss, grad_x, *[grads[nm] for nm in names], *[delta[nm] for nm in names],
            *[new_m[nm] for nm in names], *[new_v[nm] for nm in names])
```

```python
import functools

import jax
import jax.numpy as jnp
from jax import lax
from jax.experimental import pallas as pl
from jax.experimental.pallas import tpu as pltpu

F32 = jnp.float32
BF16 = jnp.bfloat16
EPS = 1e-6
N_DEV = 8
N_CHIP = 4
MESH = pl.DeviceIdType.MESH
ANY = pl.BlockSpec(memory_space=pl.ANY)

VMEM_LIMIT_BYTES = 56 * 1024 * 1024
LANES = 128
HALO = 32
HEAD_DIM = 64
CHUNK = 128
CONV_WIDTH = 31
POOL_WINDOWS = (2, 4, 8, 16)

ADAM_LR = 0.001
ADAM_B1 = 0.9
ADAM_B2 = 0.999
ADAM_EPS = 1e-08
ADAM_WD = 0.01
ADAM_STEP = 10


def _cparams(sem=None):
    return pltpu.CompilerParams(dimension_semantics=sem, vmem_limit_bytes=VMEM_LIMIT_BYTES)


def _position():
    return lax.axis_index("x"), lax.axis_index("y"), lax.axis_index("c")


class _Copies:
    def __init__(self):
        self.local, self.sends, self.recvs = [], [], []

    def extend(self, other):
        self.local += other.local
        self.sends += other.sends
        self.recvs += other.recvs

    def start(self):
        for cp in self.local + self.sends:
            cp.start()

    def wait(self):
        for land, send_sems, recv_sems, k, peer in self.recvs:
            _remote(land, land, send_sems, recv_sems, k, peer).wait_recv()
        for cp in self.sends:
            cp.wait_send()
        for cp in self.local:
            cp.wait()


def _remote(src, dst, send_sems, recv_sems, k, to):
    return pltpu.make_async_remote_copy(src_ref=src, dst_ref=dst, send_sem=send_sems.at[k], recv_sem=recv_sems.at[k],
                                        device_id=to, device_id_type=MESH)


class _Payload:
    ins, out_shapes, aliases, n_remote, n_local = (), (), {}, 0, 0

    def sem_shapes(self):
        return [pltpu.SemaphoreType.DMA((max(self.n_remote, 1),)), pltpu.SemaphoreType.DMA((max(self.n_remote, 1),)),
                pltpu.SemaphoreType.DMA((max(self.n_local, 1),))]


class _GatherIci(_Payload):
    def __init__(self, shards):
        self.ins = list(shards)
        self.out_shapes = [jax.ShapeDtypeStruct((s.shape[0], N_DEV * s.shape[1], s.shape[2]), s.dtype) for s in shards]
        self.n_remote, self.n_local = 4 * len(shards), len(shards)

    def build(self, ins, outs, send_sems, recv_sems, local_sems, k0=0, l0=0):
        x, y, c = _position()
        peers = [(x, y, 1 - c), (1 - x, y, c), (x, 1 - y, c), (1 - x, 1 - y, c)]
        cps = _Copies()
        for a, (src, out) in enumerate(zip(ins, outs)):
            r = src.shape[1]

            def rows(px, py, pc, out=out, r=r):
                return out.at[:, pl.ds((4 * px + 2 * py + pc) * r, r), :]

            cps.local.append(pltpu.make_async_copy(src, rows(x, y, c), local_sems.at[l0 + a]))
            for k, peer in enumerate(peers):
                cps.sends.append(_remote(src, rows(x, y, c), send_sems, recv_sems, k0 + 4 * a + k, peer))
                cps.recvs.append((rows(*peer), send_sems, recv_sems, k0 + 4 * a + k, peer))
        return cps


class _GatherForward(_Payload):
    def __init__(self, partials):
        self.ins = list(partials)
        self.out_shapes = [jax.ShapeDtypeStruct(p.shape, p.dtype) for p in partials]
        self.aliases = {a: a for a in range(len(partials))}
        self.n_remote = 3 * len(partials)

    def build(self, ins, outs, send_sems, recv_sems, local_sems, k0=0, l0=0):
        x, y, c = _position()
        chips = [(1 - x, y), (x, 1 - y), (1 - x, 1 - y)]
        cps = _Copies()
        for a, out in enumerate(outs):
            r = out.shape[1] // N_DEV
            for k, (px, py) in enumerate(chips):
                mine = out.at[:, pl.ds((4 * px + 2 * py + c) * r, r), :]
                theirs = out.at[:, pl.ds((4 * px + 2 * py + 1 - c) * r, r), :]
                cps.sends.append(_remote(mine, mine, send_sems, recv_sems, k0 + 3 * a + k, (x, y, 1 - c)))
                cps.recvs.append((theirs, send_sems, recv_sems, k0 + 3 * a + k, (x, y, 1 - c)))
        return cps


class _PairExchange(_Payload):
    def __init__(self, grads):
        self.ins = list(grads)
        self.out_shapes = [jax.ShapeDtypeStruct((g.shape[0] // 2, g.shape[1]), g.dtype) for g in grads]
        self.n_remote = N_CHIP * len(grads)

    def build(self, ins, outs, send_sems, recv_sems, local_sems, k0=0, l0=0):
        x, y, c = _position()
        cps = _Copies()
        for a, (src, out) in enumerate(zip(ins, outs)):
            r = src.shape[0] // N_DEV
            for q in range(N_CHIP):
                land = out.at[pl.ds(q * r, r), :]
                cps.sends.append(_remote(src.at[pl.ds((2 * q + 1 - c) * r, r), :], land, send_sems, recv_sems,
                                         k0 + N_CHIP * a + q, (x, y, 1 - c)))
                cps.recvs.append((land, send_sems, recv_sems, k0 + N_CHIP * a + q, (x, y, 1 - c)))
        return cps


class _ChipExchange(_Payload):
    def __init__(self, sums):
        self.ins = list(sums)
        self.out_shapes = [jax.ShapeDtypeStruct(s.shape, s.dtype) for s in sums]
        self.n_remote, self.n_local = 3 * len(sums), len(sums)

    def build(self, ins, outs, send_sems, recv_sems, local_sems, k0=0, l0=0):
        x, y, c = _position()
        my_chip = 2 * x + y
        chips = [(1 - x, y), (x, 1 - y), (1 - x, 1 - y)]
        cps = _Copies()
        for a, (src, out) in enumerate(zip(ins, outs)):
            r = src.shape[0] // N_CHIP
            mine = out.at[pl.ds(my_chip * r, r), :]
            cps.local.append(pltpu.make_async_copy(src.at[pl.ds(my_chip * r, r), :], mine, local_sems.at[l0 + a]))
            for k, (px, py) in enumerate(chips):
                land = out.at[pl.ds((2 * px + py) * r, r), :]
                cps.sends.append(_remote(src.at[pl.ds((2 * px + py) * r, r), :], mine, send_sems, recv_sems,
                                         k0 + 3 * a + k, (px, py, c)))
                cps.recvs.append((land, send_sems, recv_sems, k0 + 3 * a + k, (px, py, c)))
        return cps


class _Merged(_Payload):
    def __init__(self, parts):
        self.parts = list(parts)
        self.ins = [a for p in parts for a in p.ins]
        self.out_shapes = [s for p in parts for s in p.out_shapes]
        self.aliases, self.offsets = {}, []
        i0 = o0 = k0 = l0 = 0
        for p in parts:
            self.offsets.append((i0, o0, k0, l0))
            self.aliases.update({i0 + i: o0 + o for i, o in p.aliases.items()})
            i0, o0, k0, l0 = i0 + len(p.ins), o0 + len(p.out_shapes), k0 + p.n_remote, l0 + p.n_local
        self.n_remote, self.n_local = k0, l0

    def build(self, ins, outs, send_sems, recv_sems, local_sems):
        cps = _Copies()
        for p, (i0, o0, k0, l0) in zip(self.parts, self.offsets):
            cps.extend(p.build(ins[i0:i0 + len(p.ins)], outs[o0:o0 + len(p.out_shapes)], send_sems, recv_sems,
                               local_sems, k0, l0))
        return cps


def _call(body, name, grid, in_specs, out_specs, out_shape, scratch_shapes, semantics, args, payload=None):
    if payload is None:
        outs = pl.pallas_call(body, name=name, grid=grid, in_specs=in_specs, out_specs=out_specs,
                              out_shape=out_shape, scratch_shapes=scratch_shapes,
                              compiler_params=_cparams(semantics))(*args)
        return list(outs), []
    n_in, n_out, n_scr = len(in_specs), len(out_specs), len(scratch_shapes)
    p_in, p_out = len(payload.ins), len(payload.out_shapes)

    def carried(*refs):
        ins, p_ins = refs[:n_in], refs[n_in:n_in + p_in]
        o0 = n_in + p_in
        outs, p_outs = refs[o0:o0 + n_out], refs[o0 + n_out:o0 + n_out + p_out]
        s0 = o0 + n_out + p_out
        scr, sems = refs[s0:s0 + n_scr], refs[s0 + n_scr:]
        ids = [pl.program_id(k) for k in range(len(grid))]
        at_first = functools.reduce(jnp.logical_and, [i == 0 for i in ids])
        at_last = functools.reduce(jnp.logical_and, [i == g - 1 for i, g in zip(ids, grid)])

        @pl.when(at_first)
        def _():
            payload.build(p_ins, p_outs, *sems).start()

        body(*ins, *outs, *scr)

        @pl.when(at_last)
        def _():
            payload.build(p_ins, p_outs, *sems).wait()

    outs = pl.pallas_call(
        carried, name=name, grid=grid, in_specs=list(in_specs) + [ANY] * p_in,
        out_specs=list(out_specs) + [ANY] * p_out, out_shape=list(out_shape) + list(payload.out_shapes),
        scratch_shapes=list(scratch_shapes) + payload.sem_shapes(),
        input_output_aliases={n_in + i: n_out + o for i, o in payload.aliases.items()},
        compiler_params=_cparams(("arbitrary",) * len(grid)))(*args, *payload.ins)
    return list(outs[:n_out]), list(outs[n_out:])


def _comm(payload, name):
    def body(*refs):
        p_in, p_out = len(payload.ins), len(payload.out_shapes)
        cps = payload.build(refs[:p_in], refs[p_in:p_in + p_out], *refs[p_in + p_out:])
        cps.start()
        cps.wait()

    return list(pl.pallas_call(
        body, name=name, in_specs=[ANY] * len(payload.ins), out_specs=[ANY] * len(payload.out_shapes),
        out_shape=list(payload.out_shapes), scratch_shapes=payload.sem_shapes(),
        input_output_aliases=dict(payload.aliases))(*payload.ins))


def _dot(a, b):
    return jnp.dot(a, b, preferred_element_type=F32)


def _dot_nt(a, b):
    return lax.dot_general(a, b, (((1,), (1,)), ((), ())), preferred_element_type=F32)


def _dot_tn(a, b):
    return lax.dot_general(a, b, (((0,), (0,)), ((), ())), preferred_element_type=F32)


def _split_dot(x, e):
    hi = x.astype(BF16)
    r1 = x - hi.astype(F32)
    mid = r1.astype(BF16)
    lo = (r1 - mid.astype(F32)).astype(BF16)
    return _dot(hi, e) + _dot(mid, e) + _dot(lo, e)


def _rms(x):
    rstd = lax.rsqrt(jnp.mean(x * x, axis=-1, keepdims=True) + EPS)
    return x * rstd, rstd


def _rms_bwd(xhat, rstd, g, dh):
    dxhat = dh * g
    dx = rstd * (dxhat - xhat * jnp.mean(dxhat * xhat, axis=-1, keepdims=True))
    return dx, jnp.sum(dh * xhat, axis=0, keepdims=True)


def _ln(v):
    mu = jnp.mean(v, axis=-1, keepdims=True)
    xc = v - mu
    rstd = lax.rsqrt(jnp.mean(xc * xc, axis=-1, keepdims=True) + EPS)
    return xc * rstd, rstd


def _ln_bwd(vhat, rstd, g, dy):
    dvhat = dy * g
    dv = rstd * (dvhat - jnp.mean(dvhat, axis=-1, keepdims=True)
                 - vhat * jnp.mean(dvhat * vhat, axis=-1, keepdims=True))
    return dv, jnp.sum(dy * vhat, axis=0, keepdims=True), jnp.sum(dy, axis=0, keepdims=True)


_INV_SQRT2 = 0.7071067811865476
_INV_SQRT2PI = 0.3989422804014327


def _gelu(x):
    return 0.5 * x * (1.0 + lax.erf(x * _INV_SQRT2))


def _gelu_grad(x):
    return 0.5 * (1.0 + lax.erf(x * _INV_SQRT2)) + x * jnp.exp(-0.5 * x * x) * _INV_SQRT2PI


def _silu_grad(x):
    s = jax.nn.sigmoid(x)
    return s * (1.0 + x * (1.0 - s))


def _ffn_fwd(x, g, wa, mi, name, payload=None):
    t, d = x.shape
    f = wa.shape[1]
    tm, tf = 1024, 256
    nc = f // tf
    groups = [slice(k * (tm // 2), (k + 1) * (tm // 2)) for k in range(2)]

    def body(x_ref, g_ref, wgu_ref, wd_ref, xo_ref, gate_ref, up_ref, h_scr, acc_scr):
        c = pl.program_id(1)

        @pl.when(c == 0)
        def _():
            xhat, _ = _rms(x_ref[...])
            h_scr[...] = (xhat * g_ref[...]).astype(BF16)
            acc_scr[...] = jnp.zeros_like(acc_scr)

        wgu, wd = wgu_ref[...].reshape(2 * tf, d), wd_ref[...]
        for rows in groups:
            gu = _dot_nt(h_scr[rows, :], wgu)
            gate, up = gu[:, :tf], gu[:, tf:]
            gate_ref[rows, :] = gate.astype(BF16)
            up_ref[rows, :] = up.astype(BF16)
            act = (gate * jax.nn.sigmoid(gate) * up).astype(BF16)
            acc_scr[rows, :] += _dot(act, wd)

        @pl.when(c == nc - 1)
        def _():
            xo_ref[...] = x_ref[...] + 0.5 * acc_scr[...]

    assert mi % 2 == 0
    return _call(
        body, name, (t // tm, nc),
        [pl.BlockSpec((tm, d), lambda i, c: (i, 0)), pl.BlockSpec((1, d), lambda i, c: (0, 0)),
         pl.BlockSpec((2, tf, d), lambda i, c: (mi // 2, c, 0)),
         pl.BlockSpec((None, tf, d), lambda i, c: (mi + 2, c, 0))],
        [pl.BlockSpec((tm, d), lambda i, c: (i, 0)), pl.BlockSpec((tm, tf), lambda i, c: (i, c)),
         pl.BlockSpec((tm, tf), lambda i, c: (i, c))],
        [jax.ShapeDtypeStruct((t, d), F32), jax.ShapeDtypeStruct((t, f), BF16), jax.ShapeDtypeStruct((t, f), BF16)],
        [pltpu.VMEM((tm, d), BF16), pltpu.VMEM((tm, d), F32)],
        ("parallel", "arbitrary"), (x, g, wa, wa), payload)


def _ffn_bwd(x, g, dxo, gate, up, wa, mi, name, payload=None):
    t, d = x.shape
    f = wa.shape[1]
    tm, tf = 1024, 256
    nc = f // tf
    groups = [slice(k * (tm // 2), (k + 1) * (tm // 2)) for k in range(2)]

    def body(x_ref, g_ref, dxo_ref, gate_ref, up_ref, wgu_ref, wd_ref,
             dx_ref, dgate_ref, dup_ref, act_ref, h_ref, dy_ref, dg_ref, acc_scr):
        i, c = pl.program_id(0), pl.program_id(1)

        @pl.when(c == 0)
        def _():
            xhat, _ = _rms(x_ref[...])
            h_ref[...] = (xhat * g_ref[...]).astype(BF16)
            dy_ref[...] = (0.5 * dxo_ref[...]).astype(BF16)
            acc_scr[...] = jnp.zeros_like(acc_scr)

        @pl.when((c == 0) & (i == 0))
        def _():
            dg_ref[...] = jnp.zeros_like(dg_ref)

        wg, wu, wd = wgu_ref[0], wgu_ref[1], wd_ref[...]
        for rows in groups:
            gt = gate_ref[rows, :].astype(F32)
            u = up_ref[rows, :].astype(F32)
            s = jax.nn.sigmoid(gt)
            silu = gt * s
            dact = _dot_nt(dy_ref[rows, :], wd)
            dgate = (dact * u * (s * (1.0 + gt * (1.0 - s)))).astype(BF16)
            dup = (dact * silu).astype(BF16)
            dgate_ref[rows, :] = dgate
            dup_ref[rows, :] = dup
            act_ref[rows, :] = (silu * u).astype(BF16)
            acc_scr[rows, :] += _dot(dgate, wg) + _dot(dup, wu)

        @pl.when(c == nc - 1)
        def _():
            xhat, rstd = _rms(x_ref[...])
            dxn, dg = _rms_bwd(xhat, rstd, g_ref[...], acc_scr[...])
            dx_ref[...] = dxo_ref[...] + dxn
            dg_ref[...] += dg

    assert mi % 2 == 0
    row = pl.BlockSpec((tm, d), lambda i, c: (i, 0))
    col = pl.BlockSpec((tm, tf), lambda i, c: (i, c))
    vec = pl.BlockSpec((1, d), lambda i, c: (0, 0))
    return _call(
        body, name, (t // tm, nc),
        [row, vec, row, col, col, pl.BlockSpec((2, tf, d), lambda i, c: (mi // 2, c, 0)),
         pl.BlockSpec((None, tf, d), lambda i, c: (mi + 2, c, 0))],
        [row, col, col, col, row, row, vec],
        [jax.ShapeDtypeStruct((t, d), F32), jax.ShapeDtypeStruct((t, f), BF16),
         jax.ShapeDtypeStruct((t, f), BF16), jax.ShapeDtypeStruct((t, f), BF16),
         jax.ShapeDtypeStruct((t, d), BF16), jax.ShapeDtypeStruct((t, d), BF16),
         jax.ShapeDtypeStruct((1, d), F32)],
        [pltpu.VMEM((tm, d), F32)],
        ("arbitrary", "arbitrary"), (x, g, dxo, gate, up, wa, wa), payload)


def _ffn_dw(dgate, dup, act, h, dy, name, payload=None):
    t, f = dgate.shape
    d = h.shape[1]
    tk = 512
    tmm = f // 2
    nk = t // tk

    def body(dg_ref, du_ref, a_ref, h_ref, dy_ref, og_ref, ou_ref, od_ref, acc_g, acc_u, acc_d):
        k = pl.program_id(1)

        @pl.when(k == 0)
        def _():
            acc_g[...] = jnp.zeros_like(acc_g)
            acc_u[...] = jnp.zeros_like(acc_u)
            acc_d[...] = jnp.zeros_like(acc_d)

        hv = h_ref[...]
        acc_g[...] += _dot_tn(dg_ref[...], hv)
        acc_u[...] += _dot_tn(du_ref[...], hv)
        acc_d[...] += _dot_tn(a_ref[...], dy_ref[...])

        @pl.when(k == nk - 1)
        def _():
            og_ref[...] = acc_g[...].astype(BF16)
            ou_ref[...] = acc_u[...].astype(BF16)
            od_ref[...] = acc_d[...].astype(BF16)

    col = pl.BlockSpec((tk, tmm), lambda j, k: (k, j))
    row = pl.BlockSpec((tk, d), lambda j, k: (k, 0))
    out = pl.BlockSpec((tmm, d), lambda j, k: (j, 0))
    return _call(
        body, name, (f // tmm, nk), [col, col, col, row, row], [out, out, out],
        [jax.ShapeDtypeStruct((f, d), BF16)] * 3, [pltpu.VMEM((tmm, d), F32)] * 3,
        ("parallel", "arbitrary"), (dgate, dup, act, h, dy), payload)


def _tn_matmul(a, b, name, payload=None):
    t, m = a.shape
    n = b.shape[1]
    tk = 1024
    tmm = m // 2 if (m // 2) % LANES == 0 else m
    nk = t // tk

    def body(a_ref, b_ref, o_ref, acc_scr):
        k = pl.program_id(1)

        @pl.when(k == 0)
        def _():
            acc_scr[...] = jnp.zeros_like(acc_scr)

        acc_scr[...] += _dot_tn(a_ref[...].astype(BF16), b_ref[...].astype(BF16))

        @pl.when(k == nk - 1)
        def _():
            o_ref[...] = acc_scr[...].astype(BF16)

    (out,), p_outs = _call(
        body, name, (m // tmm, nk),
        [pl.BlockSpec((tk, tmm), lambda j, k: (k, j)), pl.BlockSpec((tk, n), lambda j, k: (k, 0))],
        [pl.BlockSpec((tmm, n), lambda j, k: (j, 0))],
        [jax.ShapeDtypeStruct((m, n), BF16)],
        [pltpu.VMEM((tmm, n), F32)],
        ("parallel", "arbitrary"), (a, b), payload)
    return out, p_outs


def _proj_in_fwd(x, g, wb, li, name, payload=None):
    t, d = x.shape
    n = wb.shape[1]
    tm = 512

    def body(x_ref, g_ref, w_ref, z_ref):
        xhat, _ = _rms(x_ref[...])
        z_ref[...] = _dot_nt((xhat * g_ref[...]).astype(BF16), w_ref[...])

    (z,), p_outs = _call(
        body, name, (t // tm,),
        [pl.BlockSpec((tm, d), lambda i: (i, 0)), pl.BlockSpec((1, d), lambda i: (0, 0)),
         pl.BlockSpec((None, n, d), lambda i: (li, 0, 0))],
        [pl.BlockSpec((tm, n), lambda i: (i, 0))],
        [jax.ShapeDtypeStruct((t, n), F32)], [], ("parallel",), (x, g, wb), payload)
    return z, p_outs


def _proj_out_fwd(x, cat, wc, li, name, payload=None):
    t, d = x.shape
    tm = 512

    def body(x_ref, cat_ref, w_ref, xo_ref):
        xo_ref[...] = x_ref[...] + _dot(cat_ref[...], w_ref[...])

    (xo,), p_outs = _call(
        body, name, (t // tm,),
        [pl.BlockSpec((tm, d), lambda i: (i, 0)), pl.BlockSpec((tm, d), lambda i: (i, 0)),
         pl.BlockSpec((None, d, d), lambda i: (li, 0, 0))],
        [pl.BlockSpec((tm, d), lambda i: (i, 0))],
        [jax.ShapeDtypeStruct((t, d), F32)], [], ("parallel",), (x, cat, wc), payload)
    return xo, p_outs


def _proj_out_bwd(dxo, wc, li, name):
    t, d = dxo.shape
    tm = 512

    def body(dxo_ref, w_ref, dcat_ref):
        dcat_ref[...] = _dot_nt(dxo_ref[...].astype(BF16), w_ref[...])

    return pl.pallas_call(
        body, name=name, grid=(t // tm,),
        in_specs=[pl.BlockSpec((tm, d), lambda i: (i, 0)), pl.BlockSpec((None, d, d), lambda i: (li, 0, 0))],
        out_specs=pl.BlockSpec((tm, d), lambda i: (i, 0)),
        out_shape=jax.ShapeDtypeStruct((t, d), F32),
        compiler_params=_cparams(("parallel",)),
    )(dxo, wc)


def _proj_in_bwd(x, g, dxo, dz, wb, li, name):
    t, d = x.shape
    n = wb.shape[1]
    tm = 512

    def body(x_ref, g_ref, dxo_ref, dz_ref, w_ref, dx_ref, h_ref, dg_ref):
        i = pl.program_id(0)

        @pl.when(i == 0)
        def _():
            dg_ref[...] = jnp.zeros_like(dg_ref)

        xhat, rstd = _rms(x_ref[...])
        h_ref[...] = (xhat * g_ref[...]).astype(BF16)
        dh = _dot(dz_ref[...], w_ref[...])
        dxn, dg = _rms_bwd(xhat, rstd, g_ref[...], dh)
        dx_ref[...] = dxo_ref[...] + dxn
        dg_ref[...] += dg

    row = pl.BlockSpec((tm, d), lambda i: (i, 0))
    vec = pl.BlockSpec((1, d), lambda i: (0, 0))
    return pl.pallas_call(
        body, name=name, grid=(t // tm,),
        in_specs=[row, vec, row, pl.BlockSpec((tm, n), lambda i: (i, 0)),
                  pl.BlockSpec((None, n, d), lambda i: (li, 0, 0))],
        out_specs=[row, row, vec],
        out_shape=[jax.ShapeDtypeStruct((t, d), F32), jax.ShapeDtypeStruct((t, d), BF16),
                   jax.ShapeDtypeStruct((1, d), F32)],
        compiler_params=_cparams(("arbitrary",)),
    )(x, g, dxo, dz, wb)


def _lane_ids(shape):
    return lax.broadcasted_iota(jnp.int32, shape, 1)


def _tril(w):
    r = lax.broadcasted_iota(jnp.int32, w.shape, 0)
    c = lax.broadcasted_iota(jnp.int32, w.shape, 1)
    return jnp.where(r >= c, w, 0.0)


def _shift_down(x, k):
    return x if k == 0 else pltpu.roll(x, k, 0)


def _shift_up(x, k):
    return x if k == 0 else pltpu.roll(x, x.shape[0] - k, 0)


def _sub_tile_shifts(ext, shift):
    return [shift(ext, b) for b in range(8)]


def _tap(shifted, j, n_out, down):
    a, b = divmod(j, 8)
    r0 = HALO - 8 * a if down else 8 * a
    return shifted[b][r0:r0 + n_out]


def _depthwise(shifted, w, n_out, down):
    acc = None
    for j in range(CONV_WIDTH):
        term = _tap(shifted, j, n_out, down) * w[CONV_WIDTH - 1 - j:CONV_WIDTH - j]
        acc = term if acc is None else acc + term
    return acc


def _conv_wgrad(shifted, dhc, n_out):
    return [jnp.sum(_tap(shifted, CONV_WIDTH - 1 - k, n_out, True) * dhc, axis=0, keepdims=True)
            for k in range(CONV_WIDTH)]


def _window_sums(ext, shift):
    s2 = ext + shift(ext, 1)
    s4 = s2 + shift(s2, 2)
    s8 = s4 + shift(s4, 4)
    s16 = s8 + shift(s8, 8)
    grp = _lane_ids(ext.shape) // HEAD_DIM
    return jnp.where(grp == 0, s2, jnp.where(grp == 1, s4, jnp.where(grp == 2, s8, s16)))


def _pool_count(t0, n, width):
    pos = (lax.broadcasted_iota(jnp.int32, (n, width), 0) + (t0 + 1)).astype(F32)
    grp = _lane_ids((n, width)) // HEAD_DIM
    win = jnp.where(grp == 0, 2.0, jnp.where(grp == 1, 4.0, jnp.where(grp == 2, 8.0, 16.0)))
    return jnp.minimum(pos, win)


def _block_diag(pw):
    gn, cg, _ = pw.shape
    rows = []
    for gi in range(gn):
        parts = [pw[gi] if gj == gi else jnp.zeros((cg, cg), pw.dtype) for gj in range(gn)]
        rows.append(jnp.concatenate(parts, axis=1))
    return jnp.concatenate(rows, axis=0)


def _head_pair_mix(w_even, w_odd, v):
    lo = _lane_ids((CHUNK, LANES)) < HEAD_DIM
    return jnp.where(lo, _dot(w_even, v), _dot(w_odd, v))


def _mix_fwd(z, p, name, payload=None):
    t, d_in = z.shape
    sgu = p["sgu_ln_g"].shape[1]
    pool = p["pool_scale"].shape[1]
    d_mix = 2 * sgu + pool
    tm = 512
    n_i = t // tm
    hb = tm // HALO

    def body(z_ref, zp_ref, lng_ref, lnb_ref, ws_ref, bs_ref, cw_ref, cb_ref, clg_ref, clb_ref,
             bd_ref, ps_ref, cat_ref):
        i = pl.program_id(0)
        first = i == 0
        z_main = z_ref[...]
        z_prev = jnp.where(first, 0.0, zp_ref[...])

        lng, lnb = lng_ref[...], lnb_ref[...]
        wt = [_tril(ws_ref[h]).astype(BF16) for h in range(sgu // HEAD_DIM)]
        for n in range(tm // CHUNK):
            rows = slice(n * CHUNK, (n + 1) * CHUNK)
            u = _gelu(z_main[rows, 0:sgu])
            vhat, _ = _ln(_gelu(z_main[rows, sgu:2 * sgu]))
            vn = (vhat * lng + lnb).astype(BF16)
            for gp in range(sgu // LANES):
                ls = slice(gp * LANES, (gp + 1) * LANES)
                mixed = _head_pair_mix(wt[2 * gp], wt[2 * gp + 1], vn[:, ls]) + bs_ref[:, ls]
                cat_ref[rows, ls] = (u[:, ls] * mixed).astype(BF16)

        def glu(zz):
            return zz[:, 2 * sgu:3 * sgu] * jax.nn.sigmoid(zz[:, 3 * sgu:4 * sgu])

        ext = jnp.concatenate([glu(z_prev), glu(z_main)], axis=0)
        hc = _depthwise(_sub_tile_shifts(ext, _shift_down), cw_ref[...], tm, True) + cb_ref[...]
        hhat, _ = _ln(hc)
        bn = hhat * clg_ref[...] + clb_ref[...]
        cat_ref[:, sgu:2 * sgu] = (bn * jax.nn.sigmoid(bn)).astype(BF16)

        pext = jnp.concatenate([z_prev[:, 4 * sgu:], z_main[:, 4 * sgu:]], axis=0)
        sums = _window_sums(pext, _shift_down)[HALO:]
        pooled = sums / _pool_count(i * tm, tm, pool) - z_main[:, 4 * sgu:]
        mixed_c = _dot(pooled.astype(BF16), bd_ref[...].astype(BF16))
        cat_ref[:, 2 * sgu:] = (mixed_c * ps_ref[...]).astype(BF16)

    def vec(n):
        return pl.BlockSpec((1, n), lambda i: (0, 0))

    (cat,), p_outs = _call(
        body, name, (n_i,),
        [pl.BlockSpec((tm, d_in), lambda i: (i, 0)),
         pl.BlockSpec((HALO, d_in), lambda i: (jnp.maximum(i * hb - 1, 0), 0)),
         vec(sgu), vec(sgu),
         pl.BlockSpec(p["w_spatial"].shape, lambda i: (0, 0, 0)),
         pl.BlockSpec((CHUNK, sgu), lambda i: (0, 0)),
         pl.BlockSpec((CONV_WIDTH, sgu), lambda i: (0, 0)),
         vec(sgu), vec(sgu), vec(sgu),
         pl.BlockSpec((pool, pool), lambda i: (0, 0)), vec(pool)],
        [pl.BlockSpec((tm, d_mix), lambda i: (i, 0))],
        [jax.ShapeDtypeStruct((t, d_mix), BF16)], [], ("parallel",),
        (z, z, p["sgu_ln_g"], p["sgu_ln_b"], p["w_spatial"], p["bs_full"], p["conv_w"], p["conv_b"],
         p["conv_ln_g"], p["conv_ln_b"], p["bd"], p["pool_scale"]), payload)
    return cat, p_outs


_R_SGU_G, _R_SGU_B, _R_CONV_B, _R_CLN_G, _R_CLN_B, _R_CONV_W = 0, 1, 2, 3, 4, 8
_R384_ROWS = 40


def _mix_bwd(z, dcat, p, name, payload=None):
    t, d_in = z.shape
    sgu = p["sgu_ln_g"].shape[1]
    pool = p["pool_scale"].shape[1]
    d_mix = 2 * sgu + pool
    n_head = sgu // HEAD_DIM
    tm = 512
    n_i = t // tm
    hb = tm // HALO

    def body(z_ref, zp_ref, zn_ref, dc_ref, dcn_ref, lng_ref, lnb_ref, ws_ref, bs_ref, cw_ref, cb_ref,
             clg_ref, clb_ref, bd_ref, ps_ref, dz_ref, g384_ref, gws_ref, gpool_ref, dbs_scr):
        i = pl.program_id(0)
        first, last = i == 0, i == n_i - 1

        @pl.when(first)
        def _():
            g384_ref[...] = jnp.zeros_like(g384_ref)
            gws_ref[...] = jnp.zeros_like(gws_ref)
            gpool_ref[...] = jnp.zeros_like(gpool_ref)
            dbs_scr[...] = jnp.zeros_like(dbs_scr)

        z_main = z_ref[...]
        z_prev = jnp.where(first, 0.0, zp_ref[...])
        z_next = jnp.where(last, 0.0, zn_ref[...])
        dc_main = dc_ref[...]
        dc_next = jnp.where(last, 0.0, dcn_ref[...])

        lng, lnb = lng_ref[...], lnb_ref[...]
        wt = [_tril(ws_ref[h]) for h in range(n_head)]
        wt_b = [w.astype(BF16) for w in wt]
        wtt_b = [w.T.astype(BF16) for w in wt]
        lo = _lane_ids((CHUNK, LANES)) < HEAD_DIM
        d_lng = jnp.zeros((1, sgu), F32)
        d_lnb = jnp.zeros((1, sgu), F32)
        dws = [jnp.zeros((CHUNK, CHUNK), F32) for _ in range(n_head)]
        for n in range(tm // CHUNK):
            rows = slice(n * CHUNK, (n + 1) * CHUNK)
            au, av = z_main[rows, 0:sgu], z_main[rows, sgu:2 * sgu]
            u = _gelu(au)
            vhat, vrstd = _ln(_gelu(av))
            vn = (vhat * lng + lnb).astype(BF16)
            da = dc_main[rows, 0:sgu]
            dmixed = da * u
            dbs_scr[...] += dmixed
            dvn_parts, du_parts = [], []
            for gp in range(sgu // LANES):
                ls = slice(gp * LANES, (gp + 1) * LANES)
                vn_g = vn[:, ls]
                mixed = _head_pair_mix(wt_b[2 * gp], wt_b[2 * gp + 1], vn_g) + bs_ref[:, ls]
                du_parts.append(da[:, ls] * mixed)
                dm_g = dmixed[:, ls]
                dm_b = dm_g.astype(BF16)
                dvn_parts.append(jnp.where(lo, _dot(wtt_b[2 * gp], dm_b), _dot(wtt_b[2 * gp + 1], dm_b)))
                dws[2 * gp] = dws[2 * gp] + _dot_nt(jnp.where(lo, dm_g, 0.0).astype(BF16), vn_g)
                dws[2 * gp + 1] = dws[2 * gp + 1] + _dot_nt(jnp.where(lo, 0.0, dm_g).astype(BF16), vn_g)
            dvn = jnp.concatenate(dvn_parts, axis=1)
            du = jnp.concatenate(du_parts, axis=1)
            dv, dg_n, db_n = _ln_bwd(vhat, vrstd, lng, dvn)
            d_lng = d_lng + dg_n
            d_lnb = d_lnb + db_n
            dz_ref[rows, 0:sgu] = (du * _gelu_grad(au)).astype(BF16)
            dz_ref[rows, sgu:2 * sgu] = (dv * _gelu_grad(av)).astype(BF16)
        for h in range(n_head):
            gws_ref[h] += _tril(dws[h])
        g384_ref[_R_SGU_G:_R_SGU_G + 1, :] += d_lng
        g384_ref[_R_SGU_B:_R_SGU_B + 1, :] += d_lnb

        clg = clg_ref[...]
        bcols = slice(2 * sgu, 4 * sgu)
        zb = jnp.concatenate([z_prev[:, bcols], z_main[:, bcols], z_next[:, bcols]], axis=0)
        bval, bgate = zb[:, 0:sgu], zb[:, sgu:2 * sgu]
        sg = jax.nn.sigmoid(bgate)
        hglu = bval * sg
        n_out = tm + HALO
        hglu_shifts = _sub_tile_shifts(hglu, _shift_down)
        cw = cw_ref[...]
        hc = _depthwise(hglu_shifts, cw, n_out, True) + cb_ref[...]
        hhat, hrstd = _ln(hc)
        bn = hhat * clg + clb_ref[...]
        db = jnp.concatenate([dc_main[:, sgu:2 * sgu], dc_next[:, sgu:2 * sgu]], axis=0)
        dbn = db * _silu_grad(bn)
        dhc_all, _, _ = _ln_bwd(hhat, hrstd, clg, dbn)
        dbn_m, hhat_m, dhc = dbn[:tm], hhat[:tm], dhc_all[:tm]
        g384_ref[_R_CLN_G:_R_CLN_G + 1, :] += jnp.sum(dbn_m * hhat_m, axis=0, keepdims=True)
        g384_ref[_R_CLN_B:_R_CLN_B + 1, :] += jnp.sum(dbn_m, axis=0, keepdims=True)
        g384_ref[_R_CONV_B:_R_CONV_B + 1, :] += jnp.sum(dhc, axis=0, keepdims=True)
        wrows = _conv_wgrad(hglu_shifts, dhc, tm)
        for k in range(CONV_WIDTH):
            g384_ref[_R_CONV_W + k:_R_CONV_W + k + 1, :] += wrows[k]
        dhglu = _depthwise(_sub_tile_shifts(dhc_all, _shift_up), cw, tm, False)
        bval_m, sg_m = bval[HALO:HALO + tm], sg[HALO:HALO + tm]
        dz_ref[:, 2 * sgu:3 * sgu] = (dhglu * sg_m).astype(BF16)
        dz_ref[:, 3 * sgu:4 * sgu] = (dhglu * bval_m * sg_m * (1.0 - sg_m)).astype(BF16)

        bd_b = bd_ref[...].astype(BF16)
        ps = ps_ref[...]
        p_main = z_main[:, 4 * sgu:]
        pext = jnp.concatenate([z_prev[:, 4 * sgu:], p_main], axis=0)
        cnt = _pool_count(i * tm, n_out, pool)
        pooled = _window_sums(pext, _shift_down)[HALO:] / cnt[:tm] - p_main
        pooled_b = pooled.astype(BF16)
        dcc = jnp.concatenate([dc_main[:, 2 * sgu:], dc_next[:, 2 * sgu:]], axis=0)
        dmix_c = dcc * ps
        mixed_c = _dot(pooled_b, bd_b)
        grp_r = lax.broadcasted_iota(jnp.int32, (pool, pool), 0) // HEAD_DIM
        grp_c = lax.broadcasted_iota(jnp.int32, (pool, pool), 1) // HEAD_DIM
        gpool_ref[0:pool, :] += jnp.where(grp_r == grp_c, _dot_tn(pooled_b, dmix_c[:tm].astype(BF16)), 0.0)
        gpool_ref[pool:pool + 1, :] += jnp.sum(dcc[:tm] * mixed_c, axis=0, keepdims=True)
        dpooled = _dot_nt(dmix_c.astype(BF16), bd_b)
        q = dpooled / cnt
        dp = _window_sums(q, _shift_up)[:tm] - dpooled[:tm]
        dz_ref[:, 4 * sgu:] = dp.astype(BF16)

        @pl.when(last)
        def _():
            r = lax.broadcasted_iota(jnp.int32, (sgu, LANES), 0)
            c = lax.broadcasted_iota(jnp.int32, (sgu, LANES), 1)
            sel = (r // HEAD_DIM == c).astype(BF16)
            gws_ref[n_head] = _split_dot(dbs_scr[...], sel)

    def vec(n):
        return pl.BlockSpec((1, n), lambda i: (0, 0))

    def prev_map(i):
        return (jnp.maximum(i * hb - 1, 0), 0)

    def next_map(i):
        return (jnp.minimum((i + 1) * hb, n_i * hb - 1), 0)

    return _call(
        body, name, (n_i,),
        [pl.BlockSpec((tm, d_in), lambda i: (i, 0)),
         pl.BlockSpec((HALO, d_in), prev_map), pl.BlockSpec((HALO, d_in), next_map),
         pl.BlockSpec((tm, d_mix), lambda i: (i, 0)), pl.BlockSpec((HALO, d_mix), next_map),
         vec(sgu), vec(sgu),
         pl.BlockSpec(p["w_spatial"].shape, lambda i: (0, 0, 0)),
         pl.BlockSpec((CHUNK, sgu), lambda i: (0, 0)),
         pl.BlockSpec((CONV_WIDTH, sgu), lambda i: (0, 0)),
         vec(sgu), vec(sgu), vec(sgu),
         pl.BlockSpec((pool, pool), lambda i: (0, 0)), vec(pool)],
        [pl.BlockSpec((tm, d_in), lambda i: (i, 0)),
         pl.BlockSpec((_R384_ROWS, sgu), lambda i: (0, 0)),
         pl.BlockSpec((n_head + 1, CHUNK, CHUNK), lambda i: (0, 0, 0)),
         pl.BlockSpec((pool + 8, pool), lambda i: (0, 0))],
        [jax.ShapeDtypeStruct((t, d_in), BF16),
         jax.ShapeDtypeStruct((_R384_ROWS, sgu), F32),
         jax.ShapeDtypeStruct((n_head + 1, CHUNK, CHUNK), F32),
         jax.ShapeDtypeStruct((pool + 8, pool), F32)],
        [pltpu.VMEM((CHUNK, sgu), F32)], ("arbitrary",),
        (z, z, z, dcat, dcat, p["sgu_ln_g"], p["sgu_ln_b"], p["w_spatial"], p["bs_full"], p["conv_w"],
         p["conv_b"], p["conv_ln_g"], p["conv_ln_b"], p["bd"], p["pool_scale"]), payload)


def _loss_head(x, g, target, name):
    t, d = x.shape
    tm = 512

    def body(x_ref, g_ref, tgt_ref, dx_ref, dg_ref, loss_ref):
        i = pl.program_id(0)

        @pl.when(i == 0)
        def _():
            dg_ref[...] = jnp.zeros_like(dg_ref)
            loss_ref[...] = jnp.zeros_like(loss_ref)

        gv = g_ref[...]
        xhat, rstd = _rms(x_ref[...])
        err = xhat * gv - tgt_ref[...]
        loss_ref[...] += jnp.zeros_like(loss_ref) + 0.5 * jnp.sum(jnp.mean(err * err, axis=-1, keepdims=True))
        dxn, dg = _rms_bwd(xhat, rstd, gv, err * (1.0 / d))
        dx_ref[...] = dxn
        dg_ref[...] += dg

    row = pl.BlockSpec((tm, d), lambda i: (i, 0))
    vec = pl.BlockSpec((1, d), lambda i: (0, 0))
    return pl.pallas_call(
        body, name=name, grid=(t // tm,),
        in_specs=[row, vec, row],
        out_specs=[row, vec, pl.BlockSpec((1, LANES), lambda i: (0, 0))],
        out_shape=[jax.ShapeDtypeStruct((t, d), F32), jax.ShapeDtypeStruct((1, d), F32),
                   jax.ShapeDtypeStruct((1, LANES), F32)],
        compiler_params=_cparams(("arbitrary",)),
    )(x, g, target)


def _all_gather(arrs, name, extra=None):
    gather = _GatherIci(arrs)
    n = len(arrs)
    forward = _GatherForward([jax.ShapeDtypeStruct(s.shape, s.dtype) for s in gather.out_shapes])
    x_in = len(extra.ins) if extra else 0
    x_out = len(extra.out_shapes) if extra else 0

    def body(*refs):
        ins, x_ins = refs[:n], refs[n:n + x_in]
        outs, x_outs = refs[n + x_in:2 * n + x_in], refs[2 * n + x_in:2 * n + x_in + x_out]
        sems = refs[2 * n + x_in + x_out:]
        first = gather.build(ins, outs, *sems[0:3])
        first.start()
        if extra:
            beside = extra.build(x_ins, x_outs, *sems[6:9])
            beside.start()
        first.wait()
        second = forward.build(outs, outs, *sems[3:6])
        second.start()
        second.wait()
        if extra:
            beside.wait()

    outs = pl.pallas_call(
        body, name=name,
        in_specs=[ANY] * (n + x_in), out_specs=[ANY] * (n + x_out),
        out_shape=list(gather.out_shapes) + (list(extra.out_shapes) if extra else []),
        scratch_shapes=gather.sem_shapes() + forward.sem_shapes() + (extra.sem_shapes() if extra else []),
    )(*arrs, *(extra.ins if extra else []))
    return list(outs[:n]), list(outs[n:])


def _all_gather_relayed(arrs, name):
    n = len(arrs)
    n_pairs = 8

    def body(*refs):
        ins, outs = refs[:n], refs[n:2 * n]
        send_sems, recv_sems, local_sems = refs[2 * n:]
        x, y, c = _position()
        sib, xn, yn = (x, y, 1 - c), (1 - x, y, c), (x, 1 - y, c)

        def rows(a, px, py, pc, half=None):
            r = ins[a].shape[1]
            base = (4 * px + 2 * py + pc) * r
            if half is None:
                return outs[a].at[:, pl.ds(base, r), :]
            return outs[a].at[:, pl.ds(base + half * (r // 2), r // 2), :]

        def send(a, k, src, dst, to):
            return _remote(src, dst, send_sems, recv_sems, a * n_pairs + k, to)

        def arrived(a, k, land, sender):
            _remote(land, land, send_sems, recv_sems, a * n_pairs + k, sender).wait_recv()

        own = [pltpu.make_async_copy(ins[a], rows(a, x, y, c), local_sems.at[a]) for a in range(n)]
        first = [send(a, k, ins[a], rows(a, x, y, c), to) for a in range(n) for k, to in enumerate((sib, xn, yn))]
        for cp in own + first:
            cp.start()
        for a in range(n):
            arrived(a, 1, rows(a, *xn), xn)
            arrived(a, 2, rows(a, *yn), yn)
        second = []
        for a in range(n):
            second += [send(a, 3, rows(a, *xn, half=0), rows(a, *xn, half=0), yn),
                       send(a, 4, rows(a, *yn, half=1), rows(a, *yn, half=1), xn),
                       send(a, 5, rows(a, *xn), rows(a, *xn), sib),
                       send(a, 6, rows(a, *yn), rows(a, *yn), sib)]
        for cp in second:
            cp.start()
        for a in range(n):
            arrived(a, 3, rows(a, 1 - x, 1 - y, c, half=0), yn)
            arrived(a, 4, rows(a, 1 - x, 1 - y, c, half=1), xn)
        third = [send(a, 7, rows(a, 1 - x, 1 - y, c), rows(a, 1 - x, 1 - y, c), sib) for a in range(n)]
        for cp in third:
            cp.start()
        for a in range(n):
            arrived(a, 0, rows(a, *sib), sib)
            arrived(a, 5, rows(a, 1 - x, y, 1 - c), sib)
            arrived(a, 6, rows(a, x, 1 - y, 1 - c), sib)
            arrived(a, 7, rows(a, 1 - x, 1 - y, 1 - c), sib)
        for cp in first + second + third:
            cp.wait_send()
        for cp in own:
            cp.wait()

    return list(pl.pallas_call(
        body, name=name, in_specs=[ANY] * n, out_specs=[ANY] * n,
        out_shape=[jax.ShapeDtypeStruct((a.shape[0], N_DEV * a.shape[1], a.shape[2]), a.dtype) for a in arrs],
        scratch_shapes=[pltpu.SemaphoreType.DMA((n_pairs * n,)), pltpu.SemaphoreType.DMA((n_pairs * n,)),
                        pltpu.SemaphoreType.DMA((n,))],
    )(*arrs))


def _pair_sums(grads, recvs, cidx, name):
    n = len(grads)

    def body(c_ref, *refs):
        for g_ref, r_ref, o_ref in zip(refs[:n], refs[n:2 * n], refs[2 * n:]):
            o_ref[...] = (g_ref[...].astype(F32) + r_ref[...].astype(F32)).astype(BF16)

    shapes = [(g.shape[0] // N_DEV, g.shape[1]) for g in grads]
    return list(pl.pallas_call(
        body, name=name,
        grid_spec=pltpu.PrefetchScalarGridSpec(
            num_scalar_prefetch=1, grid=(N_CHIP,),
            in_specs=[pl.BlockSpec(s, lambda q, c: (2 * q + c[0], 0)) for s in shapes]
            + [pl.BlockSpec(s, lambda q, c: (q, 0)) for s in shapes],
            out_specs=[pl.BlockSpec(s, lambda q, c: (q, 0)) for s in shapes]),
        out_shape=[jax.ShapeDtypeStruct((N_CHIP * r, cols), BF16) for r, cols in shapes],
        compiler_params=_cparams(("parallel",)),
    )(cidx, *grads, *recvs))


def _sum_blocks(parts, nblk, name):
    r = parts.shape[0] // nblk
    cols = parts.shape[1]

    def body(p_ref, o_ref):
        acc = p_ref[0:r, :].astype(F32)
        for q in range(1, nblk):
            acc = acc + p_ref[q * r:(q + 1) * r, :].astype(F32)
        o_ref[...] = acc

    return pl.pallas_call(
        body, name=name,
        out_shape=jax.ShapeDtypeStruct((r, cols), F32),
        compiler_params=_cparams(),
    )(parts)


def _adamw_math(w, g, m, v):
    m = ADAM_B1 * m + (1.0 - ADAM_B1) * g
    v = ADAM_B2 * v + (1.0 - ADAM_B2) * (g * g)
    m_hat = m / (1.0 - ADAM_B1 ** ADAM_STEP)
    v_hat = v / (1.0 - ADAM_B2 ** ADAM_STEP)
    delta = -ADAM_LR * (m_hat / (jnp.sqrt(v_hat) + ADAM_EPS) + ADAM_WD * w)
    return delta, m, v


def _finish_sharded(parts, w, m, v, name):
    depth, rr, cw = w.shape

    def body(*refs):
        p_refs = refs[:depth]
        w_ref, m_ref, v_ref, g_ref, d_ref, mo_ref, vo_ref = refs[depth:]
        l = pl.program_id(0)
        for k in range(depth):
            @pl.when(l == k)
            def _(p_ref=p_refs[k]):
                r = p_ref.shape[0] // N_CHIP
                acc = p_ref[0:r, :].astype(F32)
                for q in range(1, N_CHIP):
                    acc = acc + p_ref[q * r:(q + 1) * r, :].astype(F32)
                g_ref[...] = acc
                d_ref[...], mo_ref[...], vo_ref[...] = _adamw_math(w_ref[...], acc, m_ref[...], v_ref[...])

    blk = pl.BlockSpec((None, rr, cw), lambda l: (l, 0, 0))
    return pl.pallas_call(
        body, name=name, grid=(depth,),
        in_specs=[pl.BlockSpec(p.shape, lambda l: (0, 0)) for p in parts] + [blk] * 3, out_specs=[blk] * 4,
        out_shape=[jax.ShapeDtypeStruct(w.shape, F32)] * 4,
        compiler_params=_cparams(("arbitrary",)),
    )(*parts, w, m, v)


def _adamw_small(ws, gs, ms, vs, name):
    n = len(ws)

    def body(*refs):
        for k in range(n):
            w_ref, g_ref, m_ref, v_ref = (refs[j * n + k] for j in range(4))
            d_ref, mo_ref, vo_ref = (refs[(4 + j) * n + k] for j in range(3))
            d_ref[...], mo_ref[...], vo_ref[...] = _adamw_math(w_ref[...], g_ref[...], m_ref[...], v_ref[...])

    shapes = [jax.ShapeDtypeStruct(w.shape, F32) for w in ws]
    return pl.pallas_call(
        body, name=name, out_shape=shapes * 3, compiler_params=_cparams(),
    )(*ws, *gs, *ms, *vs)


def kernel(x, ffn1_norm, ffn1_w_gate, ffn1_w_up, ffn1_w_down, mix_norm, w_in, sgu_ln_g, sgu_ln_b, w_spatial, b_spatial, conv_w, conv_b, conv_ln_g, conv_ln_b, pool_w, pool_scale, w_out, ffn2_norm, ffn2_w_gate, ffn2_w_up, ffn2_w_down, final_norm, loss_target, m_ffn1_norm, m_ffn1_w_gate, m_ffn1_w_up, m_ffn1_w_down, m_mix_norm, m_w_in, m_sgu_ln_g, m_sgu_ln_b, m_w_spatial, m_b_spatial, m_conv_w, m_conv_b, m_conv_ln_g, m_conv_ln_b, m_pool_w, m_pool_scale, m_w_out, m_ffn2_norm, m_ffn2_w_gate, m_ffn2_w_up, m_ffn2_w_down, m_final_norm, v_ffn1_norm, v_ffn1_w_gate, v_ffn1_w_up, v_ffn1_w_down, v_mix_norm, v_w_in, v_sgu_ln_g, v_sgu_ln_b, v_w_spatial, v_b_spatial, v_conv_w, v_conv_b, v_conv_ln_g, v_conv_ln_b, v_pool_w, v_pool_scale, v_w_out, v_ffn2_norm, v_ffn2_w_gate, v_ffn2_w_up, v_ffn2_w_down, v_final_norm):
    names = ["ffn1_norm", "ffn1_w_gate", "ffn1_w_up", "ffn1_w_down", "mix_norm", "w_in", "sgu_ln_g", "sgu_ln_b",
             "w_spatial", "b_spatial", "conv_w", "conv_b", "conv_ln_g", "conv_ln_b", "pool_w", "pool_scale",
             "w_out", "ffn2_norm", "ffn2_w_gate", "ffn2_w_up", "ffn2_w_down", "final_norm"]
    W = dict(zip(names, [ffn1_norm, ffn1_w_gate, ffn1_w_up, ffn1_w_down, mix_norm, w_in, sgu_ln_g, sgu_ln_b,
                         w_spatial, b_spatial, conv_w, conv_b, conv_ln_g, conv_ln_b, pool_w, pool_scale, w_out,
                         ffn2_norm, ffn2_w_gate, ffn2_w_up, ffn2_w_down, final_norm]))
    M = dict(zip(names, [m_ffn1_norm, m_ffn1_w_gate, m_ffn1_w_up, m_ffn1_w_down, m_mix_norm, m_w_in, m_sgu_ln_g,
                         m_sgu_ln_b, m_w_spatial, m_b_spatial, m_conv_w, m_conv_b, m_conv_ln_g, m_conv_ln_b,
                         m_pool_w, m_pool_scale, m_w_out, m_ffn2_norm, m_ffn2_w_gate, m_ffn2_w_up, m_ffn2_w_down,
                         m_final_norm]))
    V = dict(zip(names, [v_ffn1_norm, v_ffn1_w_gate, v_ffn1_w_up, v_ffn1_w_down, v_mix_norm, v_w_in, v_sgu_ln_g,
                         v_sgu_ln_b, v_w_spatial, v_b_spatial, v_conv_w, v_conv_b, v_conv_ln_g, v_conv_ln_b,
                         v_pool_w, v_pool_scale, v_w_out, v_ffn2_norm, v_ffn2_w_gate, v_ffn2_w_up, v_ffn2_w_down,
                         v_final_norm]))

    depth, d = ffn1_norm.shape
    t = x.shape[1]
    sgu = sgu_ln_g.shape[1]
    pool = pool_scale.shape[1]
    n_head = sgu // HEAD_DIM
    cw_shard = conv_w.shape[2]
    xs = x.reshape(t, d)
    target = loss_target.reshape(t, d)

    def tr(w):
        return jnp.swapaxes(w, 1, 2).astype(BF16)

    ffn_shards = [[jnp.stack([tr(ffn1_w_gate)[l], tr(ffn1_w_up)[l], ffn1_w_down[l].astype(BF16)]),
                   jnp.stack([tr(ffn2_w_gate)[l], tr(ffn2_w_up)[l], ffn2_w_down[l].astype(BF16)])]
                  for l in range(depth)]
    win_shards = [tr(w_in)[l:l + 1] for l in range(depth)]
    wout_shards = [w_out[l:l + 1].astype(BF16) for l in range(depth)]
    cw_rows = depth * CONV_WIDTH
    cw_pad = -cw_rows % 8
    cw_send = jnp.pad(conv_w.reshape(cw_rows, cw_shard), ((0, cw_pad), (0, 0)))[None]
    wffn, wb, wc = {}, {}, {}
    wffn[(0, 0)], wb[0], wc[0], cwg = _all_gather_relayed(
        [ffn_shards[0][0], win_shards[0], wout_shards[0], cw_send], "ag_first")
    conv_w_full = cwg.reshape(N_DEV, cw_rows + cw_pad, cw_shard)[:, :cw_rows].reshape(
        N_DEV, depth, CONV_WIDTH, cw_shard).transpose(1, 2, 0, 3).reshape(depth, CONV_WIDTH, N_DEV * cw_shard)

    def mixer_params(l):
        return dict(
            sgu_ln_g=sgu_ln_g[l:l + 1], sgu_ln_b=sgu_ln_b[l:l + 1], w_spatial=w_spatial[l],
            bs_full=jnp.repeat(b_spatial[l].T, HEAD_DIM, axis=1),
            conv_w=conv_w_full[l], conv_b=conv_b[l:l + 1], conv_ln_g=conv_ln_g[l:l + 1],
            conv_ln_b=conv_ln_b[l:l + 1], bd=_block_diag(pool_w[l]), pool_scale=pool_scale[l:l + 1])

    saved = []
    cur = xs
    for l in range(depth):
        p = mixer_params(l)
        x0 = cur
        more = l + 1 < depth
        (x1, gate1, up1), part = _ffn_fwd(x0, ffn1_norm[l:l + 1], wffn[(l, 0)], 0, f"ffn1_fwd_{l}",
                                          _GatherIci([ffn_shards[l][1]]))
        z, (wffn[(l, 1)],) = _proj_in_fwd(x1, mix_norm[l:l + 1], wb[l], 0, f"proj_in_fwd_{l}", _GatherForward(part))
        cat, part = _mix_fwd(z, p, f"mix_fwd_{l}",
                             _GatherIci([win_shards[l + 1], wout_shards[l + 1]]) if more else None)
        x2, part = _proj_out_fwd(x1, cat, wc[l], 0, f"proj_out_fwd_{l}", _GatherForward(part) if more else None)
        if more:
            wb[l + 1], wc[l + 1] = part
        (x3, gate2, up2), part = _ffn_fwd(x2, ffn2_norm[l:l + 1], wffn[(l, 1)], 0, f"ffn2_fwd_{l}",
                                          _GatherIci([ffn_shards[l + 1][0]]) if more else None)
        if more:
            (wffn[(l + 1, 0)],) = _comm(_GatherForward(part), f"ag_forward_{l + 1}")
        saved.append((p, x0, gate1, up1, x1, z, cat, x2, gate2, up2))
        cur = x3

    dx, d_final, loss_part = _loss_head(cur, final_norm.reshape(1, d), target, "loss_head")

    cidx = lax.axis_index("c").astype(jnp.int32).reshape(1)
    from_chips = {}
    to_pair, to_chip = [], []
    small = []

    def pair_payload():
        return _PairExchange([g for _, g in to_pair]) if to_pair else None

    def pair_done(received):
        if to_pair:
            (nm, l), _ = to_pair[0]
            sums = _pair_sums([g for _, g in to_pair], list(received), cidx, f"rs_pair_sum_{nm}_{l}")
            to_chip.extend((key, s) for (key, _), s in zip(to_pair, sums))
        to_pair.clear()

    def take_chip():
        items = list(to_chip)
        to_chip.clear()
        return items

    def chip_payload(items):
        return _ChipExchange([s for _, s in items]) if items else None

    def chip_done(items, landed):
        for (key, _), o in zip(items, landed):
            from_chips[key] = o

    def ffn_weight_grads(prefix, l, dgate, dup, act, h, dy):
        items = take_chip()
        grads3, landed = _ffn_dw(dgate, dup, act, h, dy, f"dw_{prefix}_{l}", chip_payload(items))
        chip_done(items, landed)
        to_pair.extend(((f"{prefix}_{nm}", l), g) for nm, g in zip(("w_gate", "w_up", "w_down"), grads3))

    for l in reversed(range(depth)):
        p, x0, gate1, up1, x1, z, cat, x2, gate2, up2 = saved[l]
        (dx, dgate, dup, act, h, dy, dg_ffn2), received = _ffn_bwd(
            x2, ffn2_norm[l:l + 1], dx, gate2, up2, wffn[(l, 1)], 0, f"ffn2_bwd_{l}", pair_payload())
        pair_done(received)
        ffn_weight_grads("ffn2", l, dgate, dup, act, h, dy)
        g_out, received = _tn_matmul(cat, dx, f"dw_w_out_{l}", pair_payload())
        pair_done(received)
        dcat = _proj_out_bwd(dx, wc[l], 0, f"proj_out_bwd_{l}")
        items = take_chip()
        (dz, g384, gws, gpool), landed = _mix_bwd(z, dcat, p, f"mix_bwd_{l}", chip_payload(items))
        chip_done(items, landed)
        dx, hm, dg_mix = _proj_in_bwd(x1, mix_norm[l:l + 1], dx, dz, wb[l], 0, f"proj_in_bwd_{l}")
        g_in, _ = _tn_matmul(dz, hm, f"dw_w_in_{l}")
        to_pair.extend([(("w_out", l), g_out), (("w_in", l), g_in)])
        if l > 0:
            (dx, dgate, dup, act, h, dy, dg_ffn1), received = _ffn_bwd(
                x0, ffn1_norm[l:l + 1], dx, gate1, up1, wffn[(l, 0)], 0, f"ffn1_bwd_{l}", pair_payload())
            pair_done(received)
            ffn_weight_grads("ffn1", l, dgate, dup, act, h, dy)
            small.append((l, g384, gws, gpool, dg_ffn1, dg_mix, dg_ffn2))
            continue

        small.append((0, g384, gws, gpool, None, dg_mix, dg_ffn2))
        small.sort(key=lambda s: s[0])
        norm_rows = []
        for (sl, _, _, _, dg1, dgm, dg2) in small:
            norm_rows += [jnp.zeros((1, d), F32) if dg1 is None else dg1, dgm, dg2]
        norm_rows += [d_final, jnp.pad(loss_part, ((0, 0), (0, d - LANES)))]
        n_norm = len(norm_rows)
        norm_pack = jnp.concatenate(norm_rows + [jnp.zeros((8 - n_norm % 8, d), F32)] * (n_norm % 8 != 0), axis=0)
        parts = [norm_pack]
        for (_, s384, sws, spool, _, _, _) in small:
            parts += [s384, sws.reshape((n_head + 1) * CHUNK, CHUNK), spool]
        n_pair = len(to_pair)
        (dx, dgate, dup, act, h, dy, dg_ffn1), landed = _ffn_bwd(
            x0, ffn1_norm[l:l + 1], dx, gate1, up1, wffn[(l, 0)], 0, f"ffn1_bwd_{l}",
            _Merged([pair_payload(), _GatherIci([a[None] for a in parts])]))
        pair_done(landed[:n_pair])
        items = take_chip()
        g_gate, landed = _tn_matmul(dgate, h, f"dw_ffn1_w_gate_{l}",
                                    _Merged([chip_payload(items), _GatherForward(landed[n_pair:])]))
        chip_done(items, landed[:len(items)])
        gathered = landed[len(items):]
        to_pair.append((("ffn1_w_gate", l), g_gate))
        g_up, received = _tn_matmul(dup, h, f"dw_ffn1_w_up_{l}", pair_payload())
        pair_done(received)
        to_pair.append((("ffn1_w_up", l), g_up))
        items = take_chip()
        g_down, landed = _tn_matmul(act, dy, f"dw_ffn1_w_down_{l}", _Merged([chip_payload(items), pair_payload()]))
        chip_done(items, landed[:len(items)])
        pair_done(landed[len(items):])
        to_pair.append((("ffn1_w_down", l), g_down))
    grad_x = dx.reshape(x.shape)
    pair_done(_comm(pair_payload(), "rs_pair_exchange_last"))
    items = take_chip()
    (late_norm,), landed = _all_gather([jnp.pad(dg_ffn1, ((0, 7), (0, 0)))[None]], "ag_tail", chip_payload(items))
    chip_done(items, landed)

    summed = [_sum_blocks(g[0], N_DEV, f"sum_small_{k}") for k, g in enumerate(gathered)]
    late_sum = _sum_blocks(late_norm[0], N_DEV, "sum_small_late")
    norm_sum = summed[0]
    loss = norm_sum[3 * depth + 1, 0]
    cpos = lax.axis_index("x") * 4 + lax.axis_index("y") * 2 + lax.axis_index("c")
    sg = {nm: [] for nm in names}
    for l in range(depth):
        g384, gws, gpool = summed[1 + 3 * l], summed[2 + 3 * l].reshape(n_head + 1, CHUNK, CHUNK), summed[3 + 3 * l]
        sg["ffn1_norm"].append(norm_sum[3 * l] if l > 0 else late_sum[0])
        sg["mix_norm"].append(norm_sum[3 * l + 1])
        sg["ffn2_norm"].append(norm_sum[3 * l + 2])
        sg["sgu_ln_g"].append(g384[_R_SGU_G])
        sg["sgu_ln_b"].append(g384[_R_SGU_B])
        sg["conv_b"].append(g384[_R_CONV_B])
        sg["conv_ln_g"].append(g384[_R_CLN_G])
        sg["conv_ln_b"].append(g384[_R_CLN_B])
        sg["conv_w"].append(lax.dynamic_slice_in_dim(g384[_R_CONV_W:_R_CONV_W + CONV_WIDTH], cpos * cw_shard,
                                                     cw_shard, axis=1))
        sg["w_spatial"].append(gws[:n_head])
        sg["b_spatial"].append(gws[n_head][:, :n_head].T)
        sg["pool_w"].append(jnp.stack([gpool[k * HEAD_DIM:(k + 1) * HEAD_DIM, k * HEAD_DIM:(k + 1) * HEAD_DIM]
                                       for k in range(pool // HEAD_DIM)], axis=0))
        sg["pool_scale"].append(gpool[pool])
    small_names = ["ffn1_norm", "mix_norm", "sgu_ln_g", "sgu_ln_b", "w_spatial", "b_spatial", "conv_w", "conv_b",
                   "conv_ln_g", "conv_ln_b", "pool_w", "pool_scale", "ffn2_norm"]
    grads = {nm: jnp.stack(sg[nm], axis=0) for nm in small_names}
    grads["final_norm"] = norm_sum[3 * depth]

    delta, new_m, new_v = {}, {}, {}
    big_names = ["ffn1_w_gate", "ffn1_w_up", "ffn1_w_down", "w_in", "w_out", "ffn2_w_gate", "ffn2_w_up",
                 "ffn2_w_down"]
    transposed = {"ffn1_w_gate", "ffn1_w_up", "w_in", "ffn2_w_gate", "ffn2_w_up"}
    for nm in big_names:
        view = (lambda a: jnp.swapaxes(a, 1, 2)) if nm in transposed else (lambda a: a)
        outs = _finish_sharded([from_chips[(nm, l)] for l in range(depth)], view(W[nm]), view(M[nm]), view(V[nm]),
                               f"adamw_{nm}")
        grads[nm], delta[nm], new_m[nm], new_v[nm] = (view(o) for o in outs)
    snames = small_names + ["final_norm"]

    def flat2(a):
        return a.reshape(-1, a.shape[-1])

    outs = _adamw_small([flat2(W[nm]) for nm in snames], [flat2(grads[nm]) for nm in snames],
                        [flat2(M[nm]) for nm in snames], [flat2(V[nm]) for nm in snames], "adamw_small")
    ns = len(snames)
    for k, nm in enumerate(snames):
        shp = W[nm].shape
        delta[nm], new_m[nm], new_v[nm] = (outs[k].reshape(shp), outs[ns + k].reshape(shp),
                                           outs[2 * ns + k].reshape(shp))

    return (loss, grad_x, *[grads[nm] for nm in names], *[delta[nm] for nm in names],
            *[new_m[nm] for nm in names], *[new_v[nm] for nm in names])
```

```python
import functools

import jax
import jax.numpy as jnp
from jax import lax
from jax.experimental import pallas as pl
from jax.experimental.pallas import tpu as pltpu

F32 = jnp.float32
BF16 = jnp.bfloat16
EPS = 1e-6
N_DEV = 8
N_CHIP = 4
MESH = pl.DeviceIdType.MESH
ANY = pl.BlockSpec(memory_space=pl.ANY)

VMEM_LIMIT_BYTES = 56 * 1024 * 1024
LANES = 128
HALO = 32
HEAD_DIM = 64
CHUNK = 128
CONV_WIDTH = 31
POOL_WINDOWS = (2, 4, 8, 16)

ADAM_LR = 0.001
ADAM_B1 = 0.9
ADAM_B2 = 0.999
ADAM_EPS = 1e-08
ADAM_WD = 0.01
ADAM_STEP = 10


def _cparams(sem=None):
    return pltpu.CompilerParams(dimension_semantics=sem, vmem_limit_bytes=VMEM_LIMIT_BYTES)


def _position():
    return lax.axis_index("x"), lax.axis_index("y"), lax.axis_index("c")


class _Copies:
    def __init__(self):
        self.local, self.sends, self.recvs = [], [], []

    def extend(self, other):
        self.local += other.local
        self.sends += other.sends
        self.recvs += other.recvs

    def start(self):
        for cp in self.local + self.sends:
            cp.start()

    def wait(self):
        for land, send_sems, recv_sems, k, peer in self.recvs:
            _remote(land, land, send_sems, recv_sems, k, peer).wait_recv()
        for cp in self.sends:
            cp.wait_send()
        for cp in self.local:
            cp.wait()


def _remote(src, dst, send_sems, recv_sems, k, to):
    return pltpu.make_async_remote_copy(src_ref=src, dst_ref=dst, send_sem=send_sems.at[k], recv_sem=recv_sems.at[k],
                                        device_id=to, device_id_type=MESH)


class _Payload:
    ins, out_shapes, aliases, n_remote, n_local = (), (), {}, 0, 0

    def sem_shapes(self):
        return [pltpu.SemaphoreType.DMA((max(self.n_remote, 1),)), pltpu.SemaphoreType.DMA((max(self.n_remote, 1),)),
                pltpu.SemaphoreType.DMA((max(self.n_local, 1),))]


class _GatherIci(_Payload):
    def __init__(self, shards):
        self.ins = list(shards)
        self.out_shapes = [jax.ShapeDtypeStruct((s.shape[0], N_DEV * s.shape[1], s.shape[2]), s.dtype) for s in shards]
        self.n_remote, self.n_local = 4 * len(shards), len(shards)

    def build(self, ins, outs, send_sems, recv_sems, local_sems, k0=0, l0=0):
        x, y, c = _position()
        peers = [(x, y, 1 - c), (1 - x, y, c), (x, 1 - y, c), (1 - x, 1 - y, c)]
        cps = _Copies()
        for a, (src, out) in enumerate(zip(ins, outs)):
            r = src.shape[1]

            def rows(px, py, pc, out=out, r=r):
                return out.at[:, pl.ds((4 * px + 2 * py + pc) * r, r), :]

            cps.local.append(pltpu.make_async_copy(src, rows(x, y, c), local_sems.at[l0 + a]))
            for k, peer in enumerate(peers):
                cps.sends.append(_remote(src, rows(x, y, c), send_sems, recv_sems, k0 + 4 * a + k, peer))
                cps.recvs.append((rows(*peer), send_sems, recv_sems, k0 + 4 * a + k, peer))
        return cps


class _GatherForward(_Payload):
    def __init__(self, partials):
        self.ins = list(partials)
        self.out_shapes = [jax.ShapeDtypeStruct(p.shape, p.dtype) for p in partials]
        self.aliases = {a: a for a in range(len(partials))}
        self.n_remote = 3 * len(partials)

    def build(self, ins, outs, send_sems, recv_sems, local_sems, k0=0, l0=0):
        x, y, c = _position()
        chips = [(1 - x, y), (x, 1 - y), (1 - x, 1 - y)]
        cps = _Copies()
        for a, out in enumerate(outs):
            r = out.shape[1] // N_DEV
            for k, (px, py) in enumerate(chips):
                mine = out.at[:, pl.ds((4 * px + 2 * py + c) * r, r), :]
                theirs = out.at[:, pl.ds((4 * px + 2 * py + 1 - c) * r, r), :]
                cps.sends.append(_remote(mine, mine, send_sems, recv_sems, k0 + 3 * a + k, (x, y, 1 - c)))
                cps.recvs.append((theirs, send_sems, recv_sems, k0 + 3 * a + k, (x, y, 1 - c)))
        return cps


class _PairExchange(_Payload):
    def __init__(self, grads):
        self.ins = list(grads)
        self.out_shapes = [jax.ShapeDtypeStruct((g.shape[0] // 2, g.shape[1]), g.dtype) for g in grads]
        self.n_remote = N_CHIP * len(grads)

    def build(self, ins, outs, send_sems, recv_sems, local_sems, k0=0, l0=0):
        x, y, c = _position()
        cps = _Copies()
        for a, (src, out) in enumerate(zip(ins, outs)):
            r = src.shape[0] // N_DEV
            for q in range(N_CHIP):
                land = out.at[pl.ds(q * r, r), :]
                cps.sends.append(_remote(src.at[pl.ds((2 * q + 1 - c) * r, r), :], land, send_sems, recv_sems,
                                         k0 + N_CHIP * a + q, (x, y, 1 - c)))
                cps.recvs.append((land, send_sems, recv_sems, k0 + N_CHIP * a + q, (x, y, 1 - c)))
        return cps


class _ChipExchange(_Payload):
    def __init__(self, sums):
        self.ins = list(sums)
        self.out_shapes = [jax.ShapeDtypeStruct(s.shape, s.dtype) for s in sums]
        self.n_remote, self.n_local = 3 * len(sums), len(sums)

    def build(self, ins, outs, send_sems, recv_sems, local_sems, k0=0, l0=0):
        x, y, c = _position()
        my_chip = 2 * x + y
        chips = [(1 - x, y), (x, 1 - y), (1 - x, 1 - y)]
        cps = _Copies()
        for a, (src, out) in enumerate(zip(ins, outs)):
            r = src.shape[0] // N_CHIP
            mine = out.at[pl.ds(my_chip * r, r), :]
            cps.local.append(pltpu.make_async_copy(src.at[pl.ds(my_chip * r, r), :], mine, local_sems.at[l0 + a]))
            for k, (px, py) in enumerate(chips):
                land = out.at[pl.ds((2 * px + py) * r, r), :]
                cps.sends.append(_remote(src.at[pl.ds((2 * px + py) * r, r), :], mine, send_sems, recv_sems,
                                         k0 + 3 * a + k, (px, py, c)))
                cps.recvs.append((land, send_sems, recv_sems, k0 + 3 * a + k, (px, py, c)))
        return cps


class _Merged(_Payload):
    def __init__(self, parts):
        self.parts = list(parts)
        self.ins = [a for p in parts for a in p.ins]
        self.out_shapes = [s for p in parts for s in p.out_shapes]
        self.aliases, self.offsets = {}, []
        i0 = o0 = k0 = l0 = 0
        for p in parts:
            self.offsets.append((i0, o0, k0, l0))
            self.aliases.update({i0 + i: o0 + o for i, o in p.aliases.items()})
            i0, o0, k0, l0 = i0 + len(p.ins), o0 + len(p.out_shapes), k0 + p.n_remote, l0 + p.n_local
        self.n_remote, self.n_local = k0, l0

    def build(self, ins, outs, send_sems, recv_sems, local_sems):
        cps = _Copies()
        for p, (i0, o0, k0, l0) in zip(self.parts, self.offsets):
            cps.extend(p.build(ins[i0:i0 + len(p.ins)], outs[o0:o0 + len(p.out_shapes)], send_sems, recv_sems,
                               local_sems, k0, l0))
        return cps


def _call(body, name, grid, in_specs, out_specs, out_shape, scratch_shapes, semantics, args, payload=None):
    if payload is None:
        outs = pl.pallas_call(body, name=name, grid=grid, in_specs=in_specs, out_specs=out_specs,
                              out_shape=out_shape, scratch_shapes=scratch_shapes,
                              compiler_params=_cparams(semantics))(*args)
        return list(outs), []
    n_in, n_out, n_scr = len(in_specs), len(out_specs), len(scratch_shapes)
    p_in, p_out = len(payload.ins), len(payload.out_shapes)

    def carried(*refs):
        ins, p_ins = refs[:n_in], refs[n_in:n_in + p_in]
        o0 = n_in + p_in
        outs, p_outs = refs[o0:o0 + n_out], refs[o0 + n_out:o0 + n_out + p_out]
        s0 = o0 + n_out + p_out
        scr, sems = refs[s0:s0 + n_scr], refs[s0 + n_scr:]
        ids = [pl.program_id(k) for k in range(len(grid))]
        at_first = functools.reduce(jnp.logical_and, [i == 0 for i in ids])
        at_last = functools.reduce(jnp.logical_and, [i == g - 1 for i, g in zip(ids, grid)])

        @pl.when(at_first)
        def _():
            payload.build(p_ins, p_outs, *sems).start()

        body(*ins, *outs, *scr)

        @pl.when(at_last)
        def _():
            payload.build(p_ins, p_outs, *sems).wait()

    outs = pl.pallas_call(
        carried, name=name, grid=grid, in_specs=list(in_specs) + [ANY] * p_in,
        out_specs=list(out_specs) + [ANY] * p_out, out_shape=list(out_shape) + list(payload.out_shapes),
        scratch_shapes=list(scratch_shapes) + payload.sem_shapes(),
        input_output_aliases={n_in + i: n_out + o for i, o in payload.aliases.items()},
        compiler_params=_cparams(("arbitrary",) * len(grid)))(*args, *payload.ins)
    return list(outs[:n_out]), list(outs[n_out:])


def _comm(payload, name):
    def body(*refs):
        p_in, p_out = len(payload.ins), len(payload.out_shapes)
        cps = payload.build(refs[:p_in], refs[p_in:p_in + p_out], *refs[p_in + p_out:])
        cps.start()
        cps.wait()

    return list(pl.pallas_call(
        body, name=name, in_specs=[ANY] * len(payload.ins), out_specs=[ANY] * len(payload.out_shapes),
        out_shape=list(payload.out_shapes), scratch_shapes=payload.sem_shapes(),
        input_output_aliases=dict(payload.aliases))(*payload.ins))


def _dot(a, b):
    return jnp.dot(a, b, preferred_element_type=F32)


def _dot_nt(a, b):
    return lax.dot_general(a, b, (((1,), (1,)), ((), ())), preferred_element_type=F32)


def _dot_tn(a, b):
    return lax.dot_general(a, b, (((0,), (0,)), ((), ())), preferred_element_type=F32)


def _split_dot(x, e):
    hi = x.astype(BF16)
    r1 = x - hi.astype(F32)
    mid = r1.astype(BF16)
    lo = (r1 - mid.astype(F32)).astype(BF16)
    return _dot(hi, e) + _dot(mid, e) + _dot(lo, e)


def _rms(x):
    rstd = lax.rsqrt(jnp.mean(x * x, axis=-1, keepdims=True) + EPS)
    return x * rstd, rstd


def _rms_bwd(xhat, rstd, g, dh):
    dxhat = dh * g
    dx = rstd * (dxhat - xhat * jnp.mean(dxhat * xhat, axis=-1, keepdims=True))
    return dx, jnp.sum(dh * xhat, axis=0, keepdims=True)


def _ln(v):
    mu = jnp.mean(v, axis=-1, keepdims=True)
    xc = v - mu
    rstd = lax.rsqrt(jnp.mean(xc * xc, axis=-1, keepdims=True) + EPS)
    return xc * rstd, rstd


def _ln_bwd(vhat, rstd, g, dy):
    dvhat = dy * g
    dv = rstd * (dvhat - jnp.mean(dvhat, axis=-1, keepdims=True)
                 - vhat * jnp.mean(dvhat * vhat, axis=-1, keepdims=True))
    return dv, jnp.sum(dy * vhat, axis=0, keepdims=True), jnp.sum(dy, axis=0, keepdims=True)


_INV_SQRT2 = 0.7071067811865476
_INV_SQRT2PI = 0.3989422804014327


def _gelu(x):
    return 0.5 * x * (1.0 + lax.erf(x * _INV_SQRT2))


def _gelu_grad(x):
    return 0.5 * (1.0 + lax.erf(x * _INV_SQRT2)) + x * jnp.exp(-0.5 * x * x) * _INV_SQRT2PI


def _silu_grad(x):
    s = jax.nn.sigmoid(x)
    return s * (1.0 + x * (1.0 - s))


def _ffn_fwd(x, g, wa, mi, name, payload=None):
    t, d = x.shape
    f = wa.shape[1]
    tm, tf = 1024, 256
    nc = f // tf
    groups = [slice(k * (tm // 2), (k + 1) * (tm // 2)) for k in range(2)]

    def body(x_ref, g_ref, wgu_ref, wd_ref, xo_ref, gate_ref, up_ref, h_scr, acc_scr):
        c = pl.program_id(1)

        @pl.when(c == 0)
        def _():
            xhat, _ = _rms(x_ref[...])
            h_scr[...] = (xhat * g_ref[...]).astype(BF16)
            acc_scr[...] = jnp.zeros_like(acc_scr)

        wgu, wd = wgu_ref[...].reshape(2 * tf, d), wd_ref[...]
        for rows in groups:
            gu = _dot_nt(h_scr[rows, :], wgu)
            gate, up = gu[:, :tf], gu[:, tf:]
            gate_ref[rows, :] = gate.astype(BF16)
            up_ref[rows, :] = up.astype(BF16)
            act = (gate * jax.nn.sigmoid(gate) * up).astype(BF16)
            acc_scr[rows, :] += _dot(act, wd)

        @pl.when(c == nc - 1)
        def _():
            xo_ref[...] = x_ref[...] + 0.5 * acc_scr[...]

    assert mi % 2 == 0
    return _call(
        body, name, (t // tm, nc),
        [pl.BlockSpec((tm, d), lambda i, c: (i, 0)), pl.BlockSpec((1, d), lambda i, c: (0, 0)),
         pl.BlockSpec((2, tf, d), lambda i, c: (mi // 2, c, 0)),
         pl.BlockSpec((None, tf, d), lambda i, c: (mi + 2, c, 0))],
        [pl.BlockSpec((tm, d), lambda i, c: (i, 0)), pl.BlockSpec((tm, tf), lambda i, c: (i, c)),
         pl.BlockSpec((tm, tf), lambda i, c: (i, c))],
        [jax.ShapeDtypeStruct((t, d), F32), jax.ShapeDtypeStruct((t, f), BF16), jax.ShapeDtypeStruct((t, f), BF16)],
        [pltpu.VMEM((tm, d), BF16), pltpu.VMEM((tm, d), F32)],
        ("parallel", "arbitrary"), (x, g, wa, wa), payload)


def _ffn_bwd(x, g, dxo, gate, up, wa, mi, name, payload=None):
    t, d = x.shape
    f = wa.shape[1]
    tm, tf = 1024, 256
    nc = f // tf
    groups = [slice(k * (tm // 2), (k + 1) * (tm // 2)) for k in range(2)]

    def body(x_ref, g_ref, dxo_ref, gate_ref, up_ref, wgu_ref, wd_ref,
             dx_ref, dgate_ref, dup_ref, act_ref, h_ref, dy_ref, dg_ref, acc_scr):
        i, c = pl.program_id(0), pl.program_id(1)

        @pl.when(c == 0)
        def _():
            xhat, _ = _rms(x_ref[...])
            h_ref[...] = (xhat * g_ref[...]).astype(BF16)
            dy_ref[...] = (0.5 * dxo_ref[...]).astype(BF16)
            acc_scr[...] = jnp.zeros_like(acc_scr)

        @pl.when((c == 0) & (i == 0))
        def _():
            dg_ref[...] = jnp.zeros_like(dg_ref)

        wg, wu, wd = wgu_ref[0], wgu_ref[1], wd_ref[...]
        for rows in groups:
            gt = gate_ref[rows, :].astype(F32)
            u = up_ref[rows, :].astype(F32)
            s = jax.nn.sigmoid(gt)
            silu = gt * s
            dact = _dot_nt(dy_ref[rows, :], wd)
            dgate = (dact * u * (s * (1.0 + gt * (1.0 - s)))).astype(BF16)
            dup = (dact * silu).astype(BF16)
            dgate_ref[rows, :] = dgate
            dup_ref[rows, :] = dup
            act_ref[rows, :] = (silu * u).astype(BF16)
            acc_scr[rows, :] += _dot(dgate, wg) + _dot(dup, wu)

        @pl.when(c == nc - 1)
        def _():
            xhat, rstd = _rms(x_ref[...])
            dxn, dg = _rms_bwd(xhat, rstd, g_ref[...], acc_scr[...])
            dx_ref[...] = dxo_ref[...] + dxn
            dg_ref[...] += dg

    assert mi % 2 == 0
    row = pl.BlockSpec((tm, d), lambda i, c: (i, 0))
    col = pl.BlockSpec((tm, tf), lambda i, c: (i, c))
    vec = pl.BlockSpec((1, d), lambda i, c: (0, 0))
    return _call(
        body, name, (t // tm, nc),
        [row, vec, row, col, col, pl.BlockSpec((2, tf, d), lambda i, c: (mi // 2, c, 0)),
         pl.BlockSpec((None, tf, d), lambda i, c: (mi + 2, c, 0))],
        [row, col, col, col, row, row, vec],
        [jax.ShapeDtypeStruct((t, d), F32), jax.ShapeDtypeStruct((t, f), BF16),
         jax.ShapeDtypeStruct((t, f), BF16), jax.ShapeDtypeStruct((t, f), BF16),
         jax.ShapeDtypeStruct((t, d), BF16), jax.ShapeDtypeStruct((t, d), BF16),
         jax.ShapeDtypeStruct((1, d), F32)],
        [pltpu.VMEM((tm, d), F32)],
        ("arbitrary", "arbitrary"), (x, g, dxo, gate, up, wa, wa), payload)


def _ffn_dw(dgate, dup, act, h, dy, name, payload=None):
    t, f = dgate.shape
    d = h.shape[1]
    tk = 512
    tmm = f // 2
    nk = t // tk

    def body(dg_ref, du_ref, a_ref, h_ref, dy_ref, og_ref, ou_ref, od_ref, acc_g, acc_u, acc_d):
        k = pl.program_id(1)

        @pl.when(k == 0)
        def _():
            acc_g[...] = jnp.zeros_like(acc_g)
            acc_u[...] = jnp.zeros_like(acc_u)
            acc_d[...] = jnp.zeros_like(acc_d)

        hv = h_ref[...]
        acc_g[...] += _dot_tn(dg_ref[...], hv)
        acc_u[...] += _dot_tn(du_ref[...], hv)
        acc_d[...] += _dot_tn(a_ref[...], dy_ref[...])

        @pl.when(k == nk - 1)
        def _():
            og_ref[...] = acc_g[...].astype(BF16)
            ou_ref[...] = acc_u[...].astype(BF16)
            od_ref[...] = acc_d[...].astype(BF16)

    col = pl.BlockSpec((tk, tmm), lambda j, k: (k, j))
    row = pl.BlockSpec((tk, d), lambda j, k: (k, 0))
    out = pl.BlockSpec((tmm, d), lambda j, k: (j, 0))
    return _call(
        body, name, (f // tmm, nk), [col, col, col, row, row], [out, out, out],
        [jax.ShapeDtypeStruct((f, d), BF16)] * 3, [pltpu.VMEM((tmm, d), F32)] * 3,
        ("parallel", "arbitrary"), (dgate, dup, act, h, dy), payload)


def _tn_matmul(a, b, name, payload=None):
    t, m = a.shape
    n = b.shape[1]
    tk = 1024
    tmm = m // 2 if (m // 2) % LANES == 0 else m
    nk = t // tk

    def body(a_ref, b_ref, o_ref, acc_scr):
        k = pl.program_id(1)

        @pl.when(k == 0)
        def _():
            acc_scr[...] = jnp.zeros_like(acc_scr)

        acc_scr[...] += _dot_tn(a_ref[...].astype(BF16), b_ref[...].astype(BF16))

        @pl.when(k == nk - 1)
        def _():
            o_ref[...] = acc_scr[...].astype(BF16)

    (out,), p_outs = _call(
        body, name, (m // tmm, nk),
        [pl.BlockSpec((tk, tmm), lambda j, k: (k, j)), pl.BlockSpec((tk, n), lambda j, k: (k, 0))],
        [pl.BlockSpec((tmm, n), lambda j, k: (j, 0))],
        [jax.ShapeDtypeStruct((m, n), BF16)],
        [pltpu.VMEM((tmm, n), F32)],
        ("parallel", "arbitrary"), (a, b), payload)
    return out, p_outs


def _proj_in_fwd(x, g, wb, li, name, payload=None):
    t, d = x.shape
    n = wb.shape[1]
    tm = 512

    def body(x_ref, g_ref, w_ref, z_ref):
        xhat, _ = _rms(x_ref[...])
        z_ref[...] = _dot_nt((xhat * g_ref[...]).astype(BF16), w_ref[...])

    (z,), p_outs = _call(
        body, name, (t // tm,),
        [pl.BlockSpec((tm, d), lambda i: (i, 0)), pl.BlockSpec((1, d), lambda i: (0, 0)),
         pl.BlockSpec((None, n, d), lambda i: (li, 0, 0))],
        [pl.BlockSpec((tm, n), lambda i: (i, 0))],
        [jax.ShapeDtypeStruct((t, n), F32)], [], ("parallel",), (x, g, wb), payload)
    return z, p_outs


def _proj_out_fwd(x, cat, wc, li, name, payload=None):
    t, d = x.shape
    tm = 512

    def body(x_ref, cat_ref, w_ref, xo_ref):
        xo_ref[...] = x_ref[...] + _dot(cat_ref[...], w_ref[...])

    (xo,), p_outs = _call(
        body, name, (t // tm,),
        [pl.BlockSpec((tm, d), lambda i: (i, 0)), pl.BlockSpec((tm, d), lambda i: (i, 0)),
         pl.BlockSpec((None, d, d), lambda i: (li, 0, 0))],
        [pl.BlockSpec((tm, d), lambda i: (i, 0))],
        [jax.ShapeDtypeStruct((t, d), F32)], [], ("parallel",), (x, cat, wc), payload)
    return xo, p_outs


def _proj_out_bwd(dxo, wc, li, name):
    t, d = dxo.shape
    tm = 512

    def body(dxo_ref, w_ref, dcat_ref):
        dcat_ref[...] = _dot_nt(dxo_ref[...].astype(BF16), w_ref[...])

    return pl.pallas_call(
        body, name=name, grid=(t // tm,),
        in_specs=[pl.BlockSpec((tm, d), lambda i: (i, 0)), pl.BlockSpec((None, d, d), lambda i: (li, 0, 0))],
        out_specs=pl.BlockSpec((tm, d), lambda i: (i, 0)),
        out_shape=jax.ShapeDtypeStruct((t, d), F32),
        compiler_params=_cparams(("parallel",)),
    )(dxo, wc)


def _proj_in_bwd(x, g, dxo, dz, wb, li, name):
    t, d = x.shape
    n = wb.shape[1]
    tm = 512

    def body(x_ref, g_ref, dxo_ref, dz_ref, w_ref, dx_ref, h_ref, dg_ref):
        i = pl.program_id(0)

        @pl.when(i == 0)
        def _():
            dg_ref[...] = jnp.zeros_like(dg_ref)

        xhat, rstd = _rms(x_ref[...])
        h_ref[...] = (xhat * g_ref[...]).astype(BF16)
        dh = _dot(dz_ref[...], w_ref[...])
        dxn, dg = _rms_bwd(xhat, rstd, g_ref[...], dh)
        dx_ref[...] = dxo_ref[...] + dxn
        dg_ref[...] += dg

    row = pl.BlockSpec((tm, d), lambda i: (i, 0))
    vec = pl.BlockSpec((1, d), lambda i: (0, 0))
    return pl.pallas_call(
        body, name=name, grid=(t // tm,),
        in_specs=[row, vec, row, pl.BlockSpec((tm, n), lambda i: (i, 0)),
                  pl.BlockSpec((None, n, d), lambda i: (li, 0, 0))],
        out_specs=[row, row, vec],
        out_shape=[jax.ShapeDtypeStruct((t, d), F32), jax.ShapeDtypeStruct((t, d), BF16),
                   jax.ShapeDtypeStruct((1, d), F32)],
        compiler_params=_cparams(("arbitrary",)),
    )(x, g, dxo, dz, wb)


def _lane_ids(shape):
    return lax.broadcasted_iota(jnp.int32, shape, 1)


def _tril(w):
    r = lax.broadcasted_iota(jnp.int32, w.shape, 0)
    c = lax.broadcasted_iota(jnp.int32, w.shape, 1)
    return jnp.where(r >= c, w, 0.0)


def _shift_down(x, k):
    return x if k == 0 else pltpu.roll(x, k, 0)


def _shift_up(x, k):
    return x if k == 0 else pltpu.roll(x, x.shape[0] - k, 0)


def _sub_tile_shifts(ext, shift):
    return [shift(ext, b) for b in range(8)]


def _tap(shifted, j, n_out, down):
    a, b = divmod(j, 8)
    r0 = HALO - 8 * a if down else 8 * a
    return shifted[b][r0:r0 + n_out]


def _depthwise(shifted, w, n_out, down):
    acc = None
    for j in range(CONV_WIDTH):
        term = _tap(shifted, j, n_out, down) * w[CONV_WIDTH - 1 - j:CONV_WIDTH - j]
        acc = term if acc is None else acc + term
    return acc


def _conv_wgrad(shifted, dhc, n_out):
    return [jnp.sum(_tap(shifted, CONV_WIDTH - 1 - k, n_out, True) * dhc, axis=0, keepdims=True)
            for k in range(CONV_WIDTH)]


def _window_sums(ext, shift):
    s2 = ext + shift(ext, 1)
    s4 = s2 + shift(s2, 2)
    s8 = s4 + shift(s4, 4)
    s16 = s8 + shift(s8, 8)
    grp = _lane_ids(ext.shape) // HEAD_DIM
    return jnp.where(grp == 0, s2, jnp.where(grp == 1, s4, jnp.where(grp == 2, s8, s16)))


def _pool_count(t0, n, width):
    pos = (lax.broadcasted_iota(jnp.int32, (n, width), 0) + (t0 + 1)).astype(F32)
    grp = _lane_ids((n, width)) // HEAD_DIM
    win = jnp.where(grp == 0, 2.0, jnp.where(grp == 1, 4.0, jnp.where(grp == 2, 8.0, 16.0)))
    return jnp.minimum(pos, win)


def _block_diag(pw):
    gn, cg, _ = pw.shape
    rows = []
    for gi in range(gn):
        parts = [pw[gi] if gj == gi else jnp.zeros((cg, cg), pw.dtype) for gj in range(gn)]
        rows.append(jnp.concatenate(parts, axis=1))
    return jnp.concatenate(rows, axis=0)


def _head_pair_mix(w_even, w_odd, v):
    lo = _lane_ids((CHUNK, LANES)) < HEAD_DIM
    return jnp.where(lo, _dot(w_even, v), _dot(w_odd, v))


def _mix_fwd(z, p, name, payload=None):
    t, d_in = z.shape
    sgu = p["sgu_ln_g"].shape[1]
    pool = p["pool_scale"].shape[1]
    d_mix = 2 * sgu + pool
    tm = 512
    n_i = t // tm
    hb = tm // HALO

    def body(z_ref, zp_ref, lng_ref, lnb_ref, ws_ref, bs_ref, cw_ref, cb_ref, clg_ref, clb_ref,
             bd_ref, ps_ref, cat_ref, hc_ref):
        i = pl.program_id(0)
        first = i == 0
        z_main = z_ref[...]
        z_prev = jnp.where(first, 0.0, zp_ref[...])

        lng, lnb = lng_ref[...], lnb_ref[...]
        wt = [_tril(ws_ref[h]).astype(BF16) for h in range(sgu // HEAD_DIM)]
        for n in range(tm // CHUNK):
            rows = slice(n * CHUNK, (n + 1) * CHUNK)
            u = _gelu(z_main[rows, 0:sgu])
            vhat, _ = _ln(_gelu(z_main[rows, sgu:2 * sgu]))
            vn = (vhat * lng + lnb).astype(BF16)
            for gp in range(sgu // LANES):
                ls = slice(gp * LANES, (gp + 1) * LANES)
                mixed = _head_pair_mix(wt[2 * gp], wt[2 * gp + 1], vn[:, ls]) + bs_ref[:, ls]
                cat_ref[rows, ls] = (u[:, ls] * mixed).astype(BF16)

        def glu(zz):
            return zz[:, 2 * sgu:3 * sgu] * jax.nn.sigmoid(zz[:, 3 * sgu:4 * sgu])

        ext = jnp.concatenate([glu(z_prev), glu(z_main)], axis=0)
        hc = _depthwise(_sub_tile_shifts(ext, _shift_down), cw_ref[...], tm, True) + cb_ref[...]
        hc_ref[...] = hc
        hhat, _ = _ln(hc)
        bn = hhat * clg_ref[...] + clb_ref[...]
        cat_ref[:, sgu:2 * sgu] = (bn * jax.nn.sigmoid(bn)).astype(BF16)

        pext = jnp.concatenate([z_prev[:, 4 * sgu:], z_main[:, 4 * sgu:]], axis=0)
        sums = _window_sums(pext, _shift_down)[HALO:]
        pooled = sums / _pool_count(i * tm, tm, pool) - z_main[:, 4 * sgu:]
        mixed_c = _dot(pooled.astype(BF16), bd_ref[...].astype(BF16))
        cat_ref[:, 2 * sgu:] = (mixed_c * ps_ref[...]).astype(BF16)

    def vec(n):
        return pl.BlockSpec((1, n), lambda i: (0, 0))

    (cat, hc), p_outs = _call(
        body, name, (n_i,),
        [pl.BlockSpec((tm, d_in), lambda i: (i, 0)),
         pl.BlockSpec((HALO, d_in), lambda i: (jnp.maximum(i * hb - 1, 0), 0)),
         vec(sgu), vec(sgu),
         pl.BlockSpec(p["w_spatial"].shape, lambda i: (0, 0, 0)),
         pl.BlockSpec((CHUNK, sgu), lambda i: (0, 0)),
         pl.BlockSpec((CONV_WIDTH, sgu), lambda i: (0, 0)),
         vec(sgu), vec(sgu), vec(sgu),
         pl.BlockSpec((pool, pool), lambda i: (0, 0)), vec(pool)],
        [pl.BlockSpec((tm, d_mix), lambda i: (i, 0)), pl.BlockSpec((tm, sgu), lambda i: (i, 0))],
        [jax.ShapeDtypeStruct((t, d_mix), BF16), jax.ShapeDtypeStruct((t, sgu), F32)], [], ("parallel",),
        (z, z, p["sgu_ln_g"], p["sgu_ln_b"], p["w_spatial"], p["bs_full"], p["conv_w"], p["conv_b"],
         p["conv_ln_g"], p["conv_ln_b"], p["bd"], p["pool_scale"]), payload)
    return cat, hc, p_outs


_R_SGU_G, _R_SGU_B, _R_CONV_B, _R_CLN_G, _R_CLN_B, _R_CONV_W = 0, 1, 2, 3, 4, 8
_R384_ROWS = 40


def _mix_bwd(z, hc_saved, dcat, p, name, payload=None):
    t, d_in = z.shape
    sgu = p["sgu_ln_g"].shape[1]
    pool = p["pool_scale"].shape[1]
    d_mix = 2 * sgu + pool
    n_head = sgu // HEAD_DIM
    tm = 512
    n_i = t // tm
    hb = tm // HALO

    def body(z_ref, zp_ref, zn_ref, dc_ref, dcn_ref, hc_ref, hcn_ref, lng_ref, lnb_ref, ws_ref, bs_ref, cw_ref,
             clg_ref, clb_ref, bd_ref, ps_ref, dz_ref, g384_ref, gws_ref, gpool_ref, dbs_scr):
        i = pl.program_id(0)
        first, last = i == 0, i == n_i - 1

        @pl.when(first)
        def _():
            g384_ref[...] = jnp.zeros_like(g384_ref)
            gws_ref[...] = jnp.zeros_like(gws_ref)
            gpool_ref[...] = jnp.zeros_like(gpool_ref)
            dbs_scr[...] = jnp.zeros_like(dbs_scr)

        z_main = z_ref[...]
        z_prev = jnp.where(first, 0.0, zp_ref[...])
        z_next = jnp.where(last, 0.0, zn_ref[...])
        dc_main = dc_ref[...]
        dc_next = jnp.where(last, 0.0, dcn_ref[...])

        lng, lnb = lng_ref[...], lnb_ref[...]
        wt = [_tril(ws_ref[h]) for h in range(n_head)]
        wt_b = [w.astype(BF16) for w in wt]
        wtt_b = [w.T.astype(BF16) for w in wt]
        lo = _lane_ids((CHUNK, LANES)) < HEAD_DIM
        d_lng = jnp.zeros((1, sgu), F32)
        d_lnb = jnp.zeros((1, sgu), F32)
        dws = [jnp.zeros((CHUNK, CHUNK), F32) for _ in range(n_head)]
        for n in range(tm // CHUNK):
            rows = slice(n * CHUNK, (n + 1) * CHUNK)
            au, av = z_main[rows, 0:sgu], z_main[rows, sgu:2 * sgu]
            u = _gelu(au)
            vhat, vrstd = _ln(_gelu(av))
            vn = (vhat * lng + lnb).astype(BF16)
            da = dc_main[rows, 0:sgu]
            dmixed = da * u
            dbs_scr[...] += dmixed
            dvn_parts, du_parts = [], []
            for gp in range(sgu // LANES):
                ls = slice(gp * LANES, (gp + 1) * LANES)
                vn_g = vn[:, ls]
                mixed = _head_pair_mix(wt_b[2 * gp], wt_b[2 * gp + 1], vn_g) + bs_ref[:, ls]
                du_parts.append(da[:, ls] * mixed)
                dm_g = dmixed[:, ls]
                dm_b = dm_g.astype(BF16)
                dvn_parts.append(jnp.where(lo, _dot(wtt_b[2 * gp], dm_b), _dot(wtt_b[2 * gp + 1], dm_b)))
                dws[2 * gp] = dws[2 * gp] + _dot_nt(jnp.where(lo, dm_g, 0.0).astype(BF16), vn_g)
                dws[2 * gp + 1] = dws[2 * gp + 1] + _dot_nt(jnp.where(lo, 0.0, dm_g).astype(BF16), vn_g)
            dvn = jnp.concatenate(dvn_parts, axis=1)
            du = jnp.concatenate(du_parts, axis=1)
            dv, dg_n, db_n = _ln_bwd(vhat, vrstd, lng, dvn)
            d_lng = d_lng + dg_n
            d_lnb = d_lnb + db_n
            dz_ref[rows, 0:sgu] = (du * _gelu_grad(au)).astype(BF16)
            dz_ref[rows, sgu:2 * sgu] = (dv * _gelu_grad(av)).astype(BF16)
        for h in range(n_head):
            gws_ref[h] += _tril(dws[h])
        g384_ref[_R_SGU_G:_R_SGU_G + 1, :] += d_lng
        g384_ref[_R_SGU_B:_R_SGU_B + 1, :] += d_lnb

        clg = clg_ref[...]
        bcols = slice(2 * sgu, 4 * sgu)
        zb = jnp.concatenate([z_prev[:, bcols], z_main[:, bcols], z_next[:, bcols]], axis=0)
        bval, bgate = zb[:, 0:sgu], zb[:, sgu:2 * sgu]
        sg = jax.nn.sigmoid(bgate)
        hglu = bval * sg
        n_out = tm + HALO
        hglu_shifts = _sub_tile_shifts(hglu, _shift_down)
        cw = cw_ref[...]
        hc = jnp.concatenate([hc_ref[...], jnp.where(last, 0.0, hcn_ref[...])], axis=0)
        hhat, hrstd = _ln(hc)
        bn = hhat * clg + clb_ref[...]
        db = jnp.concatenate([dc_main[:, sgu:2 * sgu], dc_next[:, sgu:2 * sgu]], axis=0)
        dbn = db * _silu_grad(bn)
        dhc_all, _, _ = _ln_bwd(hhat, hrstd, clg, dbn)
        dbn_m, hhat_m, dhc = dbn[:tm], hhat[:tm], dhc_all[:tm]
        g384_ref[_R_CLN_G:_R_CLN_G + 1, :] += jnp.sum(dbn_m * hhat_m, axis=0, keepdims=True)
        g384_ref[_R_CLN_B:_R_CLN_B + 1, :] += jnp.sum(dbn_m, axis=0, keepdims=True)
        g384_ref[_R_CONV_B:_R_CONV_B + 1, :] += jnp.sum(dhc, axis=0, keepdims=True)
        wrows = _conv_wgrad(hglu_shifts, dhc, tm)
        for k in range(CONV_WIDTH):
            g384_ref[_R_CONV_W + k:_R_CONV_W + k + 1, :] += wrows[k]
        dhglu = _depthwise(_sub_tile_shifts(dhc_all, _shift_up), cw, tm, False)
        bval_m, sg_m = bval[HALO:HALO + tm], sg[HALO:HALO + tm]
        dz_ref[:, 2 * sgu:3 * sgu] = (dhglu * sg_m).astype(BF16)
        dz_ref[:, 3 * sgu:4 * sgu] = (dhglu * bval_m * sg_m * (1.0 - sg_m)).astype(BF16)

        bd_b = bd_ref[...].astype(BF16)
        ps = ps_ref[...]
        p_main = z_main[:, 4 * sgu:]
        pext = jnp.concatenate([z_prev[:, 4 * sgu:], p_main], axis=0)
        cnt = _pool_count(i * tm, n_out, pool)
        pooled = _window_sums(pext, _shift_down)[HALO:] / cnt[:tm] - p_main
        pooled_b = pooled.astype(BF16)
        dcc = jnp.concatenate([dc_main[:, 2 * sgu:], dc_next[:, 2 * sgu:]], axis=0)
        dmix_c = dcc * ps
        mixed_c = _dot(pooled_b, bd_b)
        grp_r = lax.broadcasted_iota(jnp.int32, (pool, pool), 0) // HEAD_DIM
        grp_c = lax.broadcasted_iota(jnp.int32, (pool, pool), 1) // HEAD_DIM
        gpool_ref[0:pool, :] += jnp.where(grp_r == grp_c, _dot_tn(pooled_b, dmix_c[:tm].astype(BF16)), 0.0)
        gpool_ref[pool:pool + 1, :] += jnp.sum(dcc[:tm] * mixed_c, axis=0, keepdims=True)
        dpooled = _dot_nt(dmix_c.astype(BF16), bd_b)
        q = dpooled / cnt
        dp = _window_sums(q, _shift_up)[:tm] - dpooled[:tm]
        dz_ref[:, 4 * sgu:] = dp.astype(BF16)

        @pl.when(last)
        def _():
            r = lax.broadcasted_iota(jnp.int32, (sgu, LANES), 0)
            c = lax.broadcasted_iota(jnp.int32, (sgu, LANES), 1)
            sel = (r // HEAD_DIM == c).astype(BF16)
            gws_ref[n_head] = _split_dot(dbs_scr[...], sel)

    def vec(n):
        return pl.BlockSpec((1, n), lambda i: (0, 0))

    def prev_map(i):
        return (jnp.maximum(i * hb - 1, 0), 0)

    def next_map(i):
        return (jnp.minimum((i + 1) * hb, n_i * hb - 1), 0)

    return _call(
        body, name, (n_i,),
        [pl.BlockSpec((tm, d_in), lambda i: (i, 0)),
         pl.BlockSpec((HALO, d_in), prev_map), pl.BlockSpec((HALO, d_in), next_map),
         pl.BlockSpec((tm, d_mix), lambda i: (i, 0)), pl.BlockSpec((HALO, d_mix), next_map),
         pl.BlockSpec((tm, sgu), lambda i: (i, 0)), pl.BlockSpec((HALO, sgu), next_map),
         vec(sgu), vec(sgu),
         pl.BlockSpec(p["w_spatial"].shape, lambda i: (0, 0, 0)),
         pl.BlockSpec((CHUNK, sgu), lambda i: (0, 0)),
         pl.BlockSpec((CONV_WIDTH, sgu), lambda i: (0, 0)),
         vec(sgu), vec(sgu),
         pl.BlockSpec((pool, pool), lambda i: (0, 0)), vec(pool)],
        [pl.BlockSpec((tm, d_in), lambda i: (i, 0)),
         pl.BlockSpec((_R384_ROWS, sgu), lambda i: (0, 0)),
         pl.BlockSpec((n_head + 1, CHUNK, CHUNK), lambda i: (0, 0, 0)),
         pl.BlockSpec((pool + 8, pool), lambda i: (0, 0))],
        [jax.ShapeDtypeStruct((t, d_in), BF16),
         jax.ShapeDtypeStruct((_R384_ROWS, sgu), F32),
         jax.ShapeDtypeStruct((n_head + 1, CHUNK, CHUNK), F32),
         jax.ShapeDtypeStruct((pool + 8, pool), F32)],
        [pltpu.VMEM((CHUNK, sgu), F32)], ("arbitrary",),
        (z, z, z, dcat, dcat, hc_saved, hc_saved, p["sgu_ln_g"], p["sgu_ln_b"], p["w_spatial"], p["bs_full"],
         p["conv_w"], p["conv_ln_g"], p["conv_ln_b"], p["bd"], p["pool_scale"]), payload)


def _loss_head(x, g, target, name):
    t, d = x.shape
    tm = 512

    def body(x_ref, g_ref, tgt_ref, dx_ref, dg_ref, loss_ref):
        i = pl.program_id(0)

        @pl.when(i == 0)
        def _():
            dg_ref[...] = jnp.zeros_like(dg_ref)
            loss_ref[...] = jnp.zeros_like(loss_ref)

        gv = g_ref[...]
        xhat, rstd = _rms(x_ref[...])
        err = xhat * gv - tgt_ref[...]
        loss_ref[...] += jnp.zeros_like(loss_ref) + 0.5 * jnp.sum(jnp.mean(err * err, axis=-1, keepdims=True))
        dxn, dg = _rms_bwd(xhat, rstd, gv, err * (1.0 / d))
        dx_ref[...] = dxn
        dg_ref[...] += dg

    row = pl.BlockSpec((tm, d), lambda i: (i, 0))
    vec = pl.BlockSpec((1, d), lambda i: (0, 0))
    return pl.pallas_call(
        body, name=name, grid=(t // tm,),
        in_specs=[row, vec, row],
        out_specs=[row, vec, pl.BlockSpec((1, LANES), lambda i: (0, 0))],
        out_shape=[jax.ShapeDtypeStruct((t, d), F32), jax.ShapeDtypeStruct((1, d), F32),
                   jax.ShapeDtypeStruct((1, LANES), F32)],
        compiler_params=_cparams(("arbitrary",)),
    )(x, g, target)


def _all_gather(arrs, name, extra=None):
    gather = _GatherIci(arrs)
    n = len(arrs)
    forward = _GatherForward([jax.ShapeDtypeStruct(s.shape, s.dtype) for s in gather.out_shapes])
    x_in = len(extra.ins) if extra else 0
    x_out = len(extra.out_shapes) if extra else 0

    def body(*refs):
        ins, x_ins = refs[:n], refs[n:n + x_in]
        outs, x_outs = refs[n + x_in:2 * n + x_in], refs[2 * n + x_in:2 * n + x_in + x_out]
        sems = refs[2 * n + x_in + x_out:]
        first = gather.build(ins, outs, *sems[0:3])
        first.start()
        if extra:
            beside = extra.build(x_ins, x_outs, *sems[6:9])
            beside.start()
        first.wait()
        second = forward.build(outs, outs, *sems[3:6])
        second.start()
        second.wait()
        if extra:
            beside.wait()

    outs = pl.pallas_call(
        body, name=name,
        in_specs=[ANY] * (n + x_in), out_specs=[ANY] * (n + x_out),
        out_shape=list(gather.out_shapes) + (list(extra.out_shapes) if extra else []),
        scratch_shapes=gather.sem_shapes() + forward.sem_shapes() + (extra.sem_shapes() if extra else []),
    )(*arrs, *(extra.ins if extra else []))
    return list(outs[:n]), list(outs[n:])


def _all_gather_relayed(arrs, name):
    n = len(arrs)
    n_pairs = 8

    def body(*refs):
        ins, outs = refs[:n], refs[n:2 * n]
        send_sems, recv_sems, local_sems = refs[2 * n:]
        x, y, c = _position()
        sib, xn, yn = (x, y, 1 - c), (1 - x, y, c), (x, 1 - y, c)

        def rows(a, px, py, pc, half=None):
            r = ins[a].shape[1]
            base = (4 * px + 2 * py + pc) * r
            if half is None:
                return outs[a].at[:, pl.ds(base, r), :]
            return outs[a].at[:, pl.ds(base + half * (r // 2), r // 2), :]

        def send(a, k, src, dst, to):
            return _remote(src, dst, send_sems, recv_sems, a * n_pairs + k, to)

        def arrived(a, k, land, sender):
            _remote(land, land, send_sems, recv_sems, a * n_pairs + k, sender).wait_recv()

        own = [pltpu.make_async_copy(ins[a], rows(a, x, y, c), local_sems.at[a]) for a in range(n)]
        first = [send(a, k, ins[a], rows(a, x, y, c), to) for a in range(n) for k, to in enumerate((sib, xn, yn))]
        for cp in own + first:
            cp.start()
        for a in range(n):
            arrived(a, 1, rows(a, *xn), xn)
            arrived(a, 2, rows(a, *yn), yn)
        second = []
        for a in range(n):
            second += [send(a, 3, rows(a, *xn, half=0), rows(a, *xn, half=0), yn),
                       send(a, 4, rows(a, *yn, half=1), rows(a, *yn, half=1), xn),
                       send(a, 5, rows(a, *xn), rows(a, *xn), sib),
                       send(a, 6, rows(a, *yn), rows(a, *yn), sib)]
        for cp in second:
            cp.start()
        for a in range(n):
            arrived(a, 3, rows(a, 1 - x, 1 - y, c, half=0), yn)
            arrived(a, 4, rows(a, 1 - x, 1 - y, c, half=1), xn)
        third = [send(a, 7, rows(a, 1 - x, 1 - y, c), rows(a, 1 - x, 1 - y, c), sib) for a in range(n)]
        for cp in third:
            cp.start()
        for a in range(n):
            arrived(a, 0, rows(a, *sib), sib)
            arrived(a, 5, rows(a, 1 - x, y, 1 - c), sib)
            arrived(a, 6, rows(a, x, 1 - y, 1 - c), sib)
            arrived(a, 7, rows(a, 1 - x, 1 - y, 1 - c), sib)
        for cp in first + second + third:
            cp.wait_send()
        for cp in own:
            cp.wait()

    return list(pl.pallas_call(
        body, name=name, in_specs=[ANY] * n, out_specs=[ANY] * n,
        out_shape=[jax.ShapeDtypeStruct((a.shape[0], N_DEV * a.shape[1], a.shape[2]), a.dtype) for a in arrs],
        scratch_shapes=[pltpu.SemaphoreType.DMA((n_pairs * n,)), pltpu.SemaphoreType.DMA((n_pairs * n,)),
                        pltpu.SemaphoreType.DMA((n,))],
    )(*arrs))


def _pair_sums(grads, recvs, cidx, name):
    n = len(grads)

    def body(c_ref, *refs):
        for g_ref, r_ref, o_ref in zip(refs[:n], refs[n:2 * n], refs[2 * n:]):
            o_ref[...] = (g_ref[...].astype(F32) + r_ref[...].astype(F32)).astype(BF16)

    shapes = [(g.shape[0] // N_DEV, g.shape[1]) for g in grads]
    return list(pl.pallas_call(
        body, name=name,
        grid_spec=pltpu.PrefetchScalarGridSpec(
            num_scalar_prefetch=1, grid=(N_CHIP,),
            in_specs=[pl.BlockSpec(s, lambda q, c: (2 * q + c[0], 0)) for s in shapes]
            + [pl.BlockSpec(s, lambda q, c: (q, 0)) for s in shapes],
            out_specs=[pl.BlockSpec(s, lambda q, c: (q, 0)) for s in shapes]),
        out_shape=[jax.ShapeDtypeStruct((N_CHIP * r, cols), BF16) for r, cols in shapes],
        compiler_params=_cparams(("parallel",)),
    )(cidx, *grads, *recvs))


def _sum_blocks(parts, nblk, name):
    r = parts.shape[0] // nblk
    cols = parts.shape[1]

    def body(p_ref, o_ref):
        acc = p_ref[0:r, :].astype(F32)
        for q in range(1, nblk):
            acc = acc + p_ref[q * r:(q + 1) * r, :].astype(F32)
        o_ref[...] = acc

    return pl.pallas_call(
        body, name=name,
        out_shape=jax.ShapeDtypeStruct((r, cols), F32),
        compiler_params=_cparams(),
    )(parts)


def _adamw_math(w, g, m, v):
    m = ADAM_B1 * m + (1.0 - ADAM_B1) * g
    v = ADAM_B2 * v + (1.0 - ADAM_B2) * (g * g)
    m_hat = m / (1.0 - ADAM_B1 ** ADAM_STEP)
    v_hat = v / (1.0 - ADAM_B2 ** ADAM_STEP)
    delta = -ADAM_LR * (m_hat / (jnp.sqrt(v_hat) + ADAM_EPS) + ADAM_WD * w)
    return delta, m, v


def _finish_sharded(parts, w, m, v, name):
    depth, rr, cw = w.shape

    def body(*refs):
        p_refs = refs[:depth]
        w_ref, m_ref, v_ref, g_ref, d_ref, mo_ref, vo_ref = refs[depth:]
        l = pl.program_id(0)
        for k in range(depth):
            @pl.when(l == k)
            def _(p_ref=p_refs[k]):
                r = p_ref.shape[0] // N_CHIP
                acc = p_ref[0:r, :].astype(F32)
                for q in range(1, N_CHIP):
                    acc = acc + p_ref[q * r:(q + 1) * r, :].astype(F32)
                g_ref[...] = acc
                d_ref[...], mo_ref[...], vo_ref[...] = _adamw_math(w_ref[...], acc, m_ref[...], v_ref[...])

    blk = pl.BlockSpec((None, rr, cw), lambda l: (l, 0, 0))
    return pl.pallas_call(
        body, name=name, grid=(depth,),
        in_specs=[pl.BlockSpec(p.shape, lambda l: (0, 0)) for p in parts] + [blk] * 3, out_specs=[blk] * 4,
        out_shape=[jax.ShapeDtypeStruct(w.shape, F32)] * 4,
        compiler_params=_cparams(("arbitrary",)),
    )(*parts, w, m, v)


def _adamw_small(ws, gs, ms, vs, name):
    n = len(ws)

    def body(*refs):
        for k in range(n):
            w_ref, g_ref, m_ref, v_ref = (refs[j * n + k] for j in range(4))
            d_ref, mo_ref, vo_ref = (refs[(4 + j) * n + k] for j in range(3))
            d_ref[...], mo_ref[...], vo_ref[...] = _adamw_math(w_ref[...], g_ref[...], m_ref[...], v_ref[...])

    shapes = [jax.ShapeDtypeStruct(w.shape, F32) for w in ws]
    return pl.pallas_call(
        body, name=name, out_shape=shapes * 3, compiler_params=_cparams(),
    )(*ws, *gs, *ms, *vs)


def kernel(x, ffn1_norm, ffn1_w_gate, ffn1_w_up, ffn1_w_down, mix_norm, w_in, sgu_ln_g, sgu_ln_b, w_spatial, b_spatial, conv_w, conv_b, conv_ln_g, conv_ln_b, pool_w, pool_scale, w_out, ffn2_norm, ffn2_w_gate, ffn2_w_up, ffn2_w_down, final_norm, loss_target, m_ffn1_norm, m_ffn1_w_gate, m_ffn1_w_up, m_ffn1_w_down, m_mix_norm, m_w_in, m_sgu_ln_g, m_sgu_ln_b, m_w_spatial, m_b_spatial, m_conv_w, m_conv_b, m_conv_ln_g, m_conv_ln_b, m_pool_w, m_pool_scale, m_w_out, m_ffn2_norm, m_ffn2_w_gate, m_ffn2_w_up, m_ffn2_w_down, m_final_norm, v_ffn1_norm, v_ffn1_w_gate, v_ffn1_w_up, v_ffn1_w_down, v_mix_norm, v_w_in, v_sgu_ln_g, v_sgu_ln_b, v_w_spatial, v_b_spatial, v_conv_w, v_conv_b, v_conv_ln_g, v_conv_ln_b, v_pool_w, v_pool_scale, v_w_out, v_ffn2_norm, v_ffn2_w_gate, v_ffn2_w_up, v_ffn2_w_down, v_final_norm):
    names = ["ffn1_norm", "ffn1_w_gate", "ffn1_w_up", "ffn1_w_down", "mix_norm", "w_in", "sgu_ln_g", "sgu_ln_b",
             "w_spatial", "b_spatial", "conv_w", "conv_b", "conv_ln_g", "conv_ln_b", "pool_w", "pool_scale",
             "w_out", "ffn2_norm", "ffn2_w_gate", "ffn2_w_up", "ffn2_w_down", "final_norm"]
    W = dict(zip(names, [ffn1_norm, ffn1_w_gate, ffn1_w_up, ffn1_w_down, mix_norm, w_in, sgu_ln_g, sgu_ln_b,
                         w_spatial, b_spatial, conv_w, conv_b, conv_ln_g, conv_ln_b, pool_w, pool_scale, w_out,
                         ffn2_norm, ffn2_w_gate, ffn2_w_up, ffn2_w_down, final_norm]))
    M = dict(zip(names, [m_ffn1_norm, m_ffn1_w_gate, m_ffn1_w_up, m_ffn1_w_down, m_mix_norm, m_w_in, m_sgu_ln_g,
                         m_sgu_ln_b, m_w_spatial, m_b_spatial, m_conv_w, m_conv_b, m_conv_ln_g, m_conv_ln_b,
                         m_pool_w, m_pool_scale, m_w_out, m_ffn2_norm, m_ffn2_w_gate, m_ffn2_w_up, m_ffn2_w_down,
                         m_final_norm]))
    V = dict(zip(names, [v_ffn1_norm, v_ffn1_w_gate, v_ffn1_w_up, v_ffn1_w_down, v_mix_norm, v_w_in, v_sgu_ln_g,
                         v_sgu_ln_b, v_w_spatial, v_b_spatial, v_conv_w, v_conv_b, v_conv_ln_g, v_conv_ln_b,
                         v_pool_w, v_pool_scale, v_w_out, v_ffn2_norm, v_ffn2_w_gate, v_ffn2_w_up, v_ffn2_w_down,
                         v_final_norm]))

    depth, d = ffn1_norm.shape
    t = x.shape[1]
    sgu = sgu_ln_g.shape[1]
    pool = pool_scale.shape[1]
    n_head = sgu // HEAD_DIM
    cw_shard = conv_w.shape[2]
    xs = x.reshape(t, d)
    target = loss_target.reshape(t, d)

    def tr(w):
        return jnp.swapaxes(w, 1, 2).astype(BF16)

    ffn_shards = [[jnp.stack([tr(ffn1_w_gate)[l], tr(ffn1_w_up)[l], ffn1_w_down[l].astype(BF16)]),
                   jnp.stack([tr(ffn2_w_gate)[l], tr(ffn2_w_up)[l], ffn2_w_down[l].astype(BF16)])]
                  for l in range(depth)]
    win_shards = [tr(w_in)[l:l + 1] for l in range(depth)]
    wout_shards = [w_out[l:l + 1].astype(BF16) for l in range(depth)]
    cw_rows = depth * CONV_WIDTH
    cw_pad = -cw_rows % 8
    cw_send = jnp.pad(conv_w.reshape(cw_rows, cw_shard), ((0, cw_pad), (0, 0)))[None]
    wffn, wb, wc = {}, {}, {}
    wffn[(0, 0)], wb[0], wc[0], cwg = _all_gather_relayed(
        [ffn_shards[0][0], win_shards[0], wout_shards[0], cw_send], "ag_first")
    conv_w_full = cwg.reshape(N_DEV, cw_rows + cw_pad, cw_shard)[:, :cw_rows].reshape(
        N_DEV, depth, CONV_WIDTH, cw_shard).transpose(1, 2, 0, 3).reshape(depth, CONV_WIDTH, N_DEV * cw_shard)

    def mixer_params(l):
        return dict(
            sgu_ln_g=sgu_ln_g[l:l + 1], sgu_ln_b=sgu_ln_b[l:l + 1], w_spatial=w_spatial[l],
            bs_full=jnp.repeat(b_spatial[l].T, HEAD_DIM, axis=1),
            conv_w=conv_w_full[l], conv_b=conv_b[l:l + 1], conv_ln_g=conv_ln_g[l:l + 1],
            conv_ln_b=conv_ln_b[l:l + 1], bd=_block_diag(pool_w[l]), pool_scale=pool_scale[l:l + 1])

    saved = []
    cur = xs
    for l in range(depth):
        p = mixer_params(l)
        x0 = cur
        more = l + 1 < depth
        (x1, gate1, up1), part = _ffn_fwd(x0, ffn1_norm[l:l + 1], wffn[(l, 0)], 0, f"ffn1_fwd_{l}",
                                          _GatherIci([ffn_shards[l][1]]))
        z, (wffn[(l, 1)],) = _proj_in_fwd(x1, mix_norm[l:l + 1], wb[l], 0, f"proj_in_fwd_{l}", _GatherForward(part))
        cat, hc, part = _mix_fwd(z, p, f"mix_fwd_{l}",
                             _GatherIci([win_shards[l + 1], wout_shards[l + 1]]) if more else None)
        x2, part = _proj_out_fwd(x1, cat, wc[l], 0, f"proj_out_fwd_{l}", _GatherForward(part) if more else None)
        if more:
            wb[l + 1], wc[l + 1] = part
        (x3, gate2, up2), part = _ffn_fwd(x2, ffn2_norm[l:l + 1], wffn[(l, 1)], 0, f"ffn2_fwd_{l}",
                                          _GatherIci([ffn_shards[l + 1][0]]) if more else None)
        if more:
            (wffn[(l + 1, 0)],) = _comm(_GatherForward(part), f"ag_forward_{l + 1}")
        saved.append((p, x0, gate1, up1, x1, z, hc, cat, x2, gate2, up2))
        cur = x3

    dx, d_final, loss_part = _loss_head(cur, final_norm.reshape(1, d), target, "loss_head")

    cidx = lax.axis_index("c").astype(jnp.int32).reshape(1)
    from_chips = {}
    to_pair, to_chip = [], []
    small = []

    def pair_payload():
        return _PairExchange([g for _, g in to_pair]) if to_pair else None

    def pair_done(received):
        if to_pair:
            (nm, l), _ = to_pair[0]
            sums = _pair_sums([g for _, g in to_pair], list(received), cidx, f"rs_pair_sum_{nm}_{l}")
            to_chip.extend((key, s) for (key, _), s in zip(to_pair, sums))
        to_pair.clear()

    def take_chip():
        items = list(to_chip)
        to_chip.clear()
        return items

    def chip_payload(items):
        return _ChipExchange([s for _, s in items]) if items else None

    def chip_done(items, landed):
        for (key, _), o in zip(items, landed):
            from_chips[key] = o

    def ffn_weight_grads(prefix, l, dgate, dup, act, h, dy):
        items = take_chip()
        items, later = items[:2], items[2:]
        to_chip.extend(later)
        grads3, landed = _ffn_dw(dgate, dup, act, h, dy, f"dw_{prefix}_{l}", chip_payload(items))
        chip_done(items, landed)
        to_pair.extend(((f"{prefix}_{nm}", l), g) for nm, g in zip(("w_gate", "w_up", "w_down"), grads3))

    for l in reversed(range(depth)):
        p, x0, gate1, up1, x1, z, hc, cat, x2, gate2, up2 = saved[l]
        (dx, dgate, dup, act, h, dy, dg_ffn2), received = _ffn_bwd(
            x2, ffn2_norm[l:l + 1], dx, gate2, up2, wffn[(l, 1)], 0, f"ffn2_bwd_{l}", pair_payload())
        pair_done(received)
        ffn_weight_grads("ffn2", l, dgate, dup, act, h, dy)
        g_out, received = _tn_matmul(cat, dx, f"dw_w_out_{l}", pair_payload())
        pair_done(received)
        dcat = _proj_out_bwd(dx, wc[l], 0, f"proj_out_bwd_{l}")
        items = take_chip()
        (dz, g384, gws, gpool), landed = _mix_bwd(z, hc, dcat, p, f"mix_bwd_{l}", chip_payload(items))
        chip_done(items, landed)
        dx, hm, dg_mix = _proj_in_bwd(x1, mix_norm[l:l + 1], dx, dz, wb[l], 0, f"proj_in_bwd_{l}")
        g_in, _ = _tn_matmul(dz, hm, f"dw_w_in_{l}")
        to_pair.extend([(("w_out", l), g_out), (("w_in", l), g_in)])
        if l > 0:
            (dx, dgate, dup, act, h, dy, dg_ffn1), received = _ffn_bwd(
                x0, ffn1_norm[l:l + 1], dx, gate1, up1, wffn[(l, 0)], 0, f"ffn1_bwd_{l}", pair_payload())
            pair_done(received)
            ffn_weight_grads("ffn1", l, dgate, dup, act, h, dy)
            small.append((l, g384, gws, gpool, dg_ffn1, dg_mix, dg_ffn2))
            continue

        small.append((0, g384, gws, gpool, None, dg_mix, dg_ffn2))
        small.sort(key=lambda s: s[0])
        norm_rows = []
        for (sl, _, _, _, dg1, dgm, dg2) in small:
            norm_rows += [jnp.zeros((1, d), F32) if dg1 is None else dg1, dgm, dg2]
        norm_rows += [d_final, jnp.pad(loss_part, ((0, 0), (0, d - LANES)))]
        n_norm = len(norm_rows)
        norm_pack = jnp.concatenate(norm_rows + [jnp.zeros((8 - n_norm % 8, d), F32)] * (n_norm % 8 != 0), axis=0)
        parts = [norm_pack]
        for (_, s384, sws, spool, _, _, _) in small:
            parts += [s384, sws.reshape((n_head + 1) * CHUNK, CHUNK), spool]
        n_pair = len(to_pair)
        (dx, dgate, dup, act, h, dy, dg_ffn1), landed = _ffn_bwd(
            x0, ffn1_norm[l:l + 1], dx, gate1, up1, wffn[(l, 0)], 0, f"ffn1_bwd_{l}",
            _Merged([pair_payload(), _GatherIci([a[None] for a in parts])]))
        pair_done(landed[:n_pair])
        items = take_chip()
        g_gate, landed = _tn_matmul(dgate, h, f"dw_ffn1_w_gate_{l}",
                                    _Merged([chip_payload(items), _GatherForward(landed[n_pair:])]))
        chip_done(items, landed[:len(items)])
        gathered = landed[len(items):]
        to_pair.append((("ffn1_w_gate", l), g_gate))
        g_up, received = _tn_matmul(dup, h, f"dw_ffn1_w_up_{l}", pair_payload())
        pair_done(received)
        to_pair.append((("ffn1_w_up", l), g_up))
        items = take_chip()
        g_down, landed = _tn_matmul(act, dy, f"dw_ffn1_w_down_{l}", _Merged([chip_payload(items), pair_payload()]))
        chip_done(items, landed[:len(items)])
        pair_done(landed[len(items):])
        to_pair.append((("ffn1_w_down", l), g_down))
    grad_x = dx.reshape(x.shape)
    pair_done(_comm(pair_payload(), "rs_pair_exchange_last"))
    items = take_chip()
    (late_norm,), landed = _all_gather([jnp.pad(dg_ffn1, ((0, 7), (0, 0)))[None]], "ag_tail", chip_payload(items))
    chip_done(items, landed)

    summed = [_sum_blocks(g[0], N_DEV, f"sum_small_{k}") for k, g in enumerate(gathered)]
    late_sum = _sum_blocks(late_norm[0], N_DEV, "sum_small_late")
    norm_sum = summed[0]
    loss = norm_sum[3 * depth + 1, 0]
    cpos = lax.axis_index("x") * 4 + lax.axis_index("y") * 2 + lax.axis_index("c")
    sg = {nm: [] for nm in names}
    for l in range(depth):
        g384, gws, gpool = summed[1 + 3 * l], summed[2 + 3 * l].reshape(n_head + 1, CHUNK, CHUNK), summed[3 + 3 * l]
        sg["ffn1_norm"].append(norm_sum[3 * l] if l > 0 else late_sum[0])
        sg["mix_norm"].append(norm_sum[3 * l + 1])
        sg["ffn2_norm"].append(norm_sum[3 * l + 2])
        sg["sgu_ln_g"].append(g384[_R_SGU_G])
        sg["sgu_ln_b"].append(g384[_R_SGU_B])
        sg["conv_b"].append(g384[_R_CONV_B])
        sg["conv_ln_g"].append(g384[_R_CLN_G])
        sg["conv_ln_b"].append(g384[_R_CLN_B])
        sg["conv_w"].append(lax.dynamic_slice_in_dim(g384[_R_CONV_W:_R_CONV_W + CONV_WIDTH], cpos * cw_shard,
                                                     cw_shard, axis=1))
        sg["w_spatial"].append(gws[:n_head])
        sg["b_spatial"].append(gws[n_head][:, :n_head].T)
        sg["pool_w"].append(jnp.stack([gpool[k * HEAD_DIM:(k + 1) * HEAD_DIM, k * HEAD_DIM:(k + 1) * HEAD_DIM]
                                       for k in range(pool // HEAD_DIM)], axis=0))
        sg["pool_scale"].append(gpool[pool])
    small_names = ["ffn1_norm", "mix_norm", "sgu_ln_g", "sgu_ln_b", "w_spatial", "b_spatial", "conv_w", "conv_b",
                   "conv_ln_g", "conv_ln_b", "pool_w", "pool_scale", "ffn2_norm"]
    grads = {nm: jnp.stack(sg[nm], axis=0) for nm in small_names}
    grads["final_norm"] = norm_sum[3 * depth]

    delta, new_m, new_v = {}, {}, {}
    big_names = ["ffn1_w_gate", "ffn1_w_up", "ffn1_w_down", "w_in", "w_out", "ffn2_w_gate", "ffn2_w_up",
                 "ffn2_w_down"]
    transposed = {"ffn1_w_gate", "ffn1_w_up", "w_in", "ffn2_w_gate", "ffn2_w_up"}
    for nm in big_names:
        view = (lambda a: jnp.swapaxes(a, 1, 2)) if nm in transposed else (lambda a: a)
        outs = _finish_sharded([from_chips[(nm, l)] for l in range(depth)], view(W[nm]), view(M[nm]), view(V[nm]),
                               f"adamw_{nm}")
        grads[nm], delta[nm], new_m[nm], new_v[nm] = (view(o) for o in outs)
    snames = small_names + ["final_norm"]

    def flat2(a):
        return a.reshape(-1, a.shape[-1])

    outs = _adamw_small([flat2(W[nm]) for nm in snames], [flat2(grads[nm]) for nm in snames],
                        [flat2(M[nm]) for nm in snames], [flat2(V[nm]) for nm in snames], "adamw_small")
    ns = len(snames)
    for k, nm in enumerate(snames):
        shp = W[nm].shape
        delta[nm], new_m[nm], new_v[nm] = (outs[k].reshape(shp), outs[ns + k].reshape(shp),
                                           outs[2 * ns + k].reshape(shp))

    return (loss, grad_x, *[grads[nm] for nm in names], *[delta[nm] for nm in names],
            *[new_m[nm] for nm in names], *[new_v[nm] for nm in names])
```

```python
import functools

import jax
import jax.numpy as jnp
from jax import lax
from jax.experimental import pallas as pl
from jax.experimental.pallas import tpu as pltpu

F32 = jnp.float32
BF16 = jnp.bfloat16
EPS = 1e-6
N_DEV = 8
N_CHIP = 4
MESH = pl.DeviceIdType.MESH
ANY = pl.BlockSpec(memory_space=pl.ANY)

VMEM_LIMIT_BYTES = 56 * 1024 * 1024
LANES = 128
HALO = 32
HEAD_DIM = 64
CHUNK = 128
CONV_WIDTH = 31
POOL_WINDOWS = (2, 4, 8, 16)

ADAM_LR = 0.001
ADAM_B1 = 0.9
ADAM_B2 = 0.999
ADAM_EPS = 1e-08
ADAM_WD = 0.01
ADAM_STEP = 10


def _cparams(sem=None):
    return pltpu.CompilerParams(dimension_semantics=sem, vmem_limit_bytes=VMEM_LIMIT_BYTES)


def _position():
    return lax.axis_index("x"), lax.axis_index("y"), lax.axis_index("c")


class _Copies:
    def __init__(self):
        self.local, self.sends, self.recvs = [], [], []

    def extend(self, other):
        self.local += other.local
        self.sends += other.sends
        self.recvs += other.recvs

    def start(self):
        for cp in self.local + self.sends:
            cp.start()

    def wait(self):
        for land, send_sems, recv_sems, k, peer in self.recvs:
            _remote(land, land, send_sems, recv_sems, k, peer).wait_recv()
        for cp in self.sends:
            cp.wait_send()
        for cp in self.local:
            cp.wait()


def _remote(src, dst, send_sems, recv_sems, k, to):
    return pltpu.make_async_remote_copy(src_ref=src, dst_ref=dst, send_sem=send_sems.at[k], recv_sem=recv_sems.at[k],
                                        device_id=to, device_id_type=MESH)


class _Payload:
    ins, out_shapes, aliases, n_remote, n_local = (), (), {}, 0, 0

    def sem_shapes(self):
        return [pltpu.SemaphoreType.DMA((max(self.n_remote, 1),)), pltpu.SemaphoreType.DMA((max(self.n_remote, 1),)),
                pltpu.SemaphoreType.DMA((max(self.n_local, 1),))]


class _GatherIci(_Payload):
    def __init__(self, shards):
        self.ins = list(shards)
        self.out_shapes = [jax.ShapeDtypeStruct((s.shape[0], N_DEV * s.shape[1], s.shape[2]), s.dtype) for s in shards]
        self.n_remote, self.n_local = 4 * len(shards), len(shards)

    def build(self, ins, outs, send_sems, recv_sems, local_sems, k0=0, l0=0):
        x, y, c = _position()
        peers = [(x, y, 1 - c), (1 - x, y, c), (x, 1 - y, c), (1 - x, 1 - y, c)]
        cps = _Copies()
        for a, (src, out) in enumerate(zip(ins, outs)):
            r = src.shape[1]

            def rows(px, py, pc, out=out, r=r):
                return out.at[:, pl.ds((4 * px + 2 * py + pc) * r, r), :]

            cps.local.append(pltpu.make_async_copy(src, rows(x, y, c), local_sems.at[l0 + a]))
            for k, peer in enumerate(peers):
                cps.sends.append(_remote(src, rows(x, y, c), send_sems, recv_sems, k0 + 4 * a + k, peer))
                cps.recvs.append((rows(*peer), send_sems, recv_sems, k0 + 4 * a + k, peer))
        return cps


class _GatherForward(_Payload):
    def __init__(self, partials):
        self.ins = list(partials)
        self.out_shapes = [jax.ShapeDtypeStruct(p.shape, p.dtype) for p in partials]
        self.aliases = {a: a for a in range(len(partials))}
        self.n_remote = 3 * len(partials)

    def build(self, ins, outs, send_sems, recv_sems, local_sems, k0=0, l0=0):
        x, y, c = _position()
        chips = [(1 - x, y), (x, 1 - y), (1 - x, 1 - y)]
        cps = _Copies()
        for a, out in enumerate(outs):
            r = out.shape[1] // N_DEV
            for k, (px, py) in enumerate(chips):
                mine = out.at[:, pl.ds((4 * px + 2 * py + c) * r, r), :]
                theirs = out.at[:, pl.ds((4 * px + 2 * py + 1 - c) * r, r), :]
                cps.sends.append(_remote(mine, mine, send_sems, recv_sems, k0 + 3 * a + k, (x, y, 1 - c)))
                cps.recvs.append((theirs, send_sems, recv_sems, k0 + 3 * a + k, (x, y, 1 - c)))
        return cps


class _PairExchange(_Payload):
    def __init__(self, grads):
        self.ins = list(grads)
        self.out_shapes = [jax.ShapeDtypeStruct((g.shape[0] // 2, g.shape[1]), g.dtype) for g in grads]
        self.n_remote = N_CHIP * len(grads)

    def build(self, ins, outs, send_sems, recv_sems, local_sems, k0=0, l0=0):
        x, y, c = _position()
        cps = _Copies()
        for a, (src, out) in enumerate(zip(ins, outs)):
            r = src.shape[0] // N_DEV
            for q in range(N_CHIP):
                land = out.at[pl.ds(q * r, r), :]
                cps.sends.append(_remote(src.at[pl.ds((2 * q + 1 - c) * r, r), :], land, send_sems, recv_sems,
                                         k0 + N_CHIP * a + q, (x, y, 1 - c)))
                cps.recvs.append((land, send_sems, recv_sems, k0 + N_CHIP * a + q, (x, y, 1 - c)))
        return cps


class _ChipExchange(_Payload):
    def __init__(self, sums):
        self.ins = list(sums)
        self.out_shapes = [jax.ShapeDtypeStruct(s.shape, s.dtype) for s in sums]
        self.n_remote, self.n_local = 3 * len(sums), len(sums)

    def build(self, ins, outs, send_sems, recv_sems, local_sems, k0=0, l0=0):
        x, y, c = _position()
        my_chip = 2 * x + y
        chips = [(1 - x, y), (x, 1 - y), (1 - x, 1 - y)]
        cps = _Copies()
        for a, (src, out) in enumerate(zip(ins, outs)):
            r = src.shape[0] // N_CHIP
            mine = out.at[pl.ds(my_chip * r, r), :]
            cps.local.append(pltpu.make_async_copy(src.at[pl.ds(my_chip * r, r), :], mine, local_sems.at[l0 + a]))
            for k, (px, py) in enumerate(chips):
                land = out.at[pl.ds((2 * px + py) * r, r), :]
                cps.sends.append(_remote(src.at[pl.ds((2 * px + py) * r, r), :], mine, send_sems, recv_sems,
                                         k0 + 3 * a + k, (px, py, c)))
                cps.recvs.append((land, send_sems, recv_sems, k0 + 3 * a + k, (px, py, c)))
        return cps


class _Merged(_Payload):
    def __init__(self, parts):
        self.parts = list(parts)
        self.ins = [a for p in parts for a in p.ins]
        self.out_shapes = [s for p in parts for s in p.out_shapes]
        self.aliases, self.offsets = {}, []
        i0 = o0 = k0 = l0 = 0
        for p in parts:
            self.offsets.append((i0, o0, k0, l0))
            self.aliases.update({i0 + i: o0 + o for i, o in p.aliases.items()})
            i0, o0, k0, l0 = i0 + len(p.ins), o0 + len(p.out_shapes), k0 + p.n_remote, l0 + p.n_local
        self.n_remote, self.n_local = k0, l0

    def build(self, ins, outs, send_sems, recv_sems, local_sems):
        cps = _Copies()
        for p, (i0, o0, k0, l0) in zip(self.parts, self.offsets):
            cps.extend(p.build(ins[i0:i0 + len(p.ins)], outs[o0:o0 + len(p.out_shapes)], send_sems, recv_sems,
                               local_sems, k0, l0))
        return cps


def _call(body, name, grid, in_specs, out_specs, out_shape, scratch_shapes, semantics, args, payload=None):
    if payload is None:
        outs = pl.pallas_call(body, name=name, grid=grid, in_specs=in_specs, out_specs=out_specs,
                              out_shape=out_shape, scratch_shapes=scratch_shapes,
                              compiler_params=_cparams(semantics))(*args)
        return list(outs), []
    n_in, n_out, n_scr = len(in_specs), len(out_specs), len(scratch_shapes)
    p_in, p_out = len(payload.ins), len(payload.out_shapes)

    def carried(*refs):
        ins, p_ins = refs[:n_in], refs[n_in:n_in + p_in]
        o0 = n_in + p_in
        outs, p_outs = refs[o0:o0 + n_out], refs[o0 + n_out:o0 + n_out + p_out]
        s0 = o0 + n_out + p_out
        scr, sems = refs[s0:s0 + n_scr], refs[s0 + n_scr:]
        ids = [pl.program_id(k) for k in range(len(grid))]
        at_first = functools.reduce(jnp.logical_and, [i == 0 for i in ids])
        at_last = functools.reduce(jnp.logical_and, [i == g - 1 for i, g in zip(ids, grid)])

        @pl.when(at_first)
        def _():
            payload.build(p_ins, p_outs, *sems).start()

        body(*ins, *outs, *scr)

        @pl.when(at_last)
        def _():
            payload.build(p_ins, p_outs, *sems).wait()

    outs = pl.pallas_call(
        carried, name=name, grid=grid, in_specs=list(in_specs) + [ANY] * p_in,
        out_specs=list(out_specs) + [ANY] * p_out, out_shape=list(out_shape) + list(payload.out_shapes),
        scratch_shapes=list(scratch_shapes) + payload.sem_shapes(),
        input_output_aliases={n_in + i: n_out + o for i, o in payload.aliases.items()},
        compiler_params=_cparams(("arbitrary",) * len(grid)))(*args, *payload.ins)
    return list(outs[:n_out]), list(outs[n_out:])


def _comm(payload, name):
    def body(*refs):
        p_in, p_out = len(payload.ins), len(payload.out_shapes)
        cps = payload.build(refs[:p_in], refs[p_in:p_in + p_out], *refs[p_in + p_out:])
        cps.start()
        cps.wait()

    return list(pl.pallas_call(
        body, name=name, in_specs=[ANY] * len(payload.ins), out_specs=[ANY] * len(payload.out_shapes),
        out_shape=list(payload.out_shapes), scratch_shapes=payload.sem_shapes(),
        input_output_aliases=dict(payload.aliases))(*payload.ins))


def _dot(a, b):
    return jnp.dot(a, b, preferred_element_type=F32)


def _dot_nt(a, b):
    return lax.dot_general(a, b, (((1,), (1,)), ((), ())), preferred_element_type=F32)


def _dot_tn(a, b):
    return lax.dot_general(a, b, (((0,), (0,)), ((), ())), preferred_element_type=F32)


def _split_dot(x, e):
    hi = x.astype(BF16)
    r1 = x - hi.astype(F32)
    mid = r1.astype(BF16)
    lo = (r1 - mid.astype(F32)).astype(BF16)
    return _dot(hi, e) + _dot(mid, e) + _dot(lo, e)


def _rms(x):
    rstd = lax.rsqrt(jnp.mean(x * x, axis=-1, keepdims=True) + EPS)
    return x * rstd, rstd


def _rms_bwd(xhat, rstd, g, dh):
    dxhat = dh * g
    dx = rstd * (dxhat - xhat * jnp.mean(dxhat * xhat, axis=-1, keepdims=True))
    return dx, jnp.sum(dh * xhat, axis=0, keepdims=True)


def _ln(v):
    mu = jnp.mean(v, axis=-1, keepdims=True)
    xc = v - mu
    rstd = lax.rsqrt(jnp.mean(xc * xc, axis=-1, keepdims=True) + EPS)
    return xc * rstd, rstd


def _ln_bwd(vhat, rstd, g, dy):
    dvhat = dy * g
    dv = rstd * (dvhat - jnp.mean(dvhat, axis=-1, keepdims=True)
                 - vhat * jnp.mean(dvhat * vhat, axis=-1, keepdims=True))
    return dv, jnp.sum(dy * vhat, axis=0, keepdims=True), jnp.sum(dy, axis=0, keepdims=True)


_INV_SQRT2 = 0.7071067811865476
_INV_SQRT2PI = 0.3989422804014327


def _gelu(x):
    return 0.5 * x * (1.0 + lax.erf(x * _INV_SQRT2))


def _gelu_grad(x):
    return 0.5 * (1.0 + lax.erf(x * _INV_SQRT2)) + x * jnp.exp(-0.5 * x * x) * _INV_SQRT2PI


def _silu_grad(x):
    s = jax.nn.sigmoid(x)
    return s * (1.0 + x * (1.0 - s))


def _ffn_fwd(x, g, wa, mi, name, payload=None):
    t, d = x.shape
    f = wa.shape[1]
    tm, tf = 1024, 256
    nc = f // tf
    groups = [slice(k * (tm // 2), (k + 1) * (tm // 2)) for k in range(2)]

    def body(x_ref, g_ref, wgu_ref, wd_ref, xo_ref, gate_ref, up_ref, act_ref, h_scr, acc_scr):
        c = pl.program_id(1)

        @pl.when(c == 0)
        def _():
            xhat, _ = _rms(x_ref[...])
            h_scr[...] = (xhat * g_ref[...]).astype(BF16)
            acc_scr[...] = jnp.zeros_like(acc_scr)

        wgu, wd = wgu_ref[...].reshape(2 * tf, d), wd_ref[...]
        for rows in groups:
            gu = _dot_nt(h_scr[rows, :], wgu)
            gate, up = gu[:, :tf], gu[:, tf:]
            gate_ref[rows, :] = gate.astype(BF16)
            up_ref[rows, :] = up.astype(BF16)
            act = (gate * jax.nn.sigmoid(gate) * up).astype(BF16)
            act_ref[rows, :] = act
            acc_scr[rows, :] += _dot(act, wd)

        @pl.when(c == nc - 1)
        def _():
            xo_ref[...] = x_ref[...] + 0.5 * acc_scr[...]

    assert mi % 2 == 0
    return _call(
        body, name, (t // tm, nc),
        [pl.BlockSpec((tm, d), lambda i, c: (i, 0)), pl.BlockSpec((1, d), lambda i, c: (0, 0)),
         pl.BlockSpec((2, tf, d), lambda i, c: (mi // 2, c, 0)),
         pl.BlockSpec((None, tf, d), lambda i, c: (mi + 2, c, 0))],
        [pl.BlockSpec((tm, d), lambda i, c: (i, 0))] + [pl.BlockSpec((tm, tf), lambda i, c: (i, c))] * 3,
        [jax.ShapeDtypeStruct((t, d), F32)] + [jax.ShapeDtypeStruct((t, f), BF16)] * 3,
        [pltpu.VMEM((tm, d), BF16), pltpu.VMEM((tm, d), F32)],
        ("parallel", "arbitrary"), (x, g, wa, wa), payload)


def _ffn_bwd(x, g, dxo, gate, up, wa, mi, name, payload=None):
    t, d = x.shape
    f = wa.shape[1]
    tm, tf = 1024, 256
    nc = f // tf
    groups = [slice(k * (tm // 2), (k + 1) * (tm // 2)) for k in range(2)]

    def body(x_ref, g_ref, dxo_ref, gate_ref, up_ref, wgu_ref, wd_ref,
             dx_ref, dgate_ref, dup_ref, h_ref, dy_ref, dg_ref, acc_scr):
        i, c = pl.program_id(0), pl.program_id(1)

        @pl.when(c == 0)
        def _():
            xhat, _ = _rms(x_ref[...])
            h_ref[...] = (xhat * g_ref[...]).astype(BF16)
            dy_ref[...] = (0.5 * dxo_ref[...]).astype(BF16)
            acc_scr[...] = jnp.zeros_like(acc_scr)

        @pl.when((c == 0) & (i == 0))
        def _():
            dg_ref[...] = jnp.zeros_like(dg_ref)

        wg, wu, wd = wgu_ref[0], wgu_ref[1], wd_ref[...]
        for rows in groups:
            gt = gate_ref[rows, :].astype(F32)
            u = up_ref[rows, :].astype(F32)
            s = jax.nn.sigmoid(gt)
            silu = gt * s
            dact = _dot_nt(dy_ref[rows, :], wd)
            dgate = (dact * u * (s * (1.0 + gt * (1.0 - s)))).astype(BF16)
            dup = (dact * silu).astype(BF16)
            dgate_ref[rows, :] = dgate
            dup_ref[rows, :] = dup
            acc_scr[rows, :] += _dot(dgate, wg) + _dot(dup, wu)

        @pl.when(c == nc - 1)
        def _():
            xhat, rstd = _rms(x_ref[...])
            dxn, dg = _rms_bwd(xhat, rstd, g_ref[...], acc_scr[...])
            dx_ref[...] = dxo_ref[...] + dxn
            dg_ref[...] += dg

    assert mi % 2 == 0
    row = pl.BlockSpec((tm, d), lambda i, c: (i, 0))
    col = pl.BlockSpec((tm, tf), lambda i, c: (i, c))
    vec = pl.BlockSpec((1, d), lambda i, c: (0, 0))
    return _call(
        body, name, (t // tm, nc),
        [row, vec, row, col, col, pl.BlockSpec((2, tf, d), lambda i, c: (mi // 2, c, 0)),
         pl.BlockSpec((None, tf, d), lambda i, c: (mi + 2, c, 0))],
        [row, col, col, row, row, vec],
        [jax.ShapeDtypeStruct((t, d), F32), jax.ShapeDtypeStruct((t, f), BF16),
         jax.ShapeDtypeStruct((t, f), BF16),
         jax.ShapeDtypeStruct((t, d), BF16), jax.ShapeDtypeStruct((t, d), BF16),
         jax.ShapeDtypeStruct((1, d), F32)],
        [pltpu.VMEM((tm, d), F32)],
        ("arbitrary", "arbitrary"), (x, g, dxo, gate, up, wa, wa), payload)


def _ffn_dw(dgate, dup, act, h, dy, name, payload=None):
    t, f = dgate.shape
    d = h.shape[1]
    tk = 512
    tmm = f // 2
    nk = t // tk

    def body(dg_ref, du_ref, a_ref, h_ref, dy_ref, og_ref, ou_ref, od_ref, acc_g, acc_u, acc_d):
        k = pl.program_id(1)

        @pl.when(k == 0)
        def _():
            acc_g[...] = jnp.zeros_like(acc_g)
            acc_u[...] = jnp.zeros_like(acc_u)
            acc_d[...] = jnp.zeros_like(acc_d)

        hv = h_ref[...]
        acc_g[...] += _dot_tn(dg_ref[...], hv)
        acc_u[...] += _dot_tn(du_ref[...], hv)
        acc_d[...] += _dot_tn(a_ref[...], dy_ref[...])

        @pl.when(k == nk - 1)
        def _():
            og_ref[...] = acc_g[...].astype(BF16)
            ou_ref[...] = acc_u[...].astype(BF16)
            od_ref[...] = acc_d[...].astype(BF16)

    col = pl.BlockSpec((tk, tmm), lambda j, k: (k, j))
    row = pl.BlockSpec((tk, d), lambda j, k: (k, 0))
    out = pl.BlockSpec((tmm, d), lambda j, k: (j, 0))
    return _call(
        body, name, (f // tmm, nk), [col, col, col, row, row], [out, out, out],
        [jax.ShapeDtypeStruct((f, d), BF16)] * 3, [pltpu.VMEM((tmm, d), F32)] * 3,
        ("parallel", "arbitrary"), (dgate, dup, act, h, dy), payload)


def _tn_matmul(a, b, name, payload=None):
    t, m = a.shape
    n = b.shape[1]
    tk = 1024
    tmm = m // 2 if (m // 2) % LANES == 0 else m
    nk = t // tk

    def body(a_ref, b_ref, o_ref, acc_scr):
        k = pl.program_id(1)

        @pl.when(k == 0)
        def _():
            acc_scr[...] = jnp.zeros_like(acc_scr)

        acc_scr[...] += _dot_tn(a_ref[...].astype(BF16), b_ref[...].astype(BF16))

        @pl.when(k == nk - 1)
        def _():
            o_ref[...] = acc_scr[...].astype(BF16)

    (out,), p_outs = _call(
        body, name, (m // tmm, nk),
        [pl.BlockSpec((tk, tmm), lambda j, k: (k, j)), pl.BlockSpec((tk, n), lambda j, k: (k, 0))],
        [pl.BlockSpec((tmm, n), lambda j, k: (j, 0))],
        [jax.ShapeDtypeStruct((m, n), BF16)],
        [pltpu.VMEM((tmm, n), F32)],
        ("parallel", "arbitrary"), (a, b), payload)
    return out, p_outs


def _proj_in_fwd(x, g, wb, li, name, payload=None):
    t, d = x.shape
    n = wb.shape[1]
    tm = 512

    def body(x_ref, g_ref, w_ref, z_ref):
        xhat, _ = _rms(x_ref[...])
        z_ref[...] = _dot_nt((xhat * g_ref[...]).astype(BF16), w_ref[...])

    (z,), p_outs = _call(
        body, name, (t // tm,),
        [pl.BlockSpec((tm, d), lambda i: (i, 0)), pl.BlockSpec((1, d), lambda i: (0, 0)),
         pl.BlockSpec((None, n, d), lambda i: (li, 0, 0))],
        [pl.BlockSpec((tm, n), lambda i: (i, 0))],
        [jax.ShapeDtypeStruct((t, n), F32)], [], ("parallel",), (x, g, wb), payload)
    return z, p_outs


def _proj_out_fwd(x, cat, wc, li, name, payload=None):
    t, d = x.shape
    tm = 512

    def body(x_ref, cat_ref, w_ref, xo_ref):
        xo_ref[...] = x_ref[...] + _dot(cat_ref[...], w_ref[...])

    (xo,), p_outs = _call(
        body, name, (t // tm,),
        [pl.BlockSpec((tm, d), lambda i: (i, 0)), pl.BlockSpec((tm, d), lambda i: (i, 0)),
         pl.BlockSpec((None, d, d), lambda i: (li, 0, 0))],
        [pl.BlockSpec((tm, d), lambda i: (i, 0))],
        [jax.ShapeDtypeStruct((t, d), F32)], [], ("parallel",), (x, cat, wc), payload)
    return xo, p_outs


def _proj_out_bwd(dxo, wc, li, name):
    t, d = dxo.shape
    tm = 512

    def body(dxo_ref, w_ref, dcat_ref):
        dcat_ref[...] = _dot_nt(dxo_ref[...].astype(BF16), w_ref[...])

    return pl.pallas_call(
        body, name=name, grid=(t // tm,),
        in_specs=[pl.BlockSpec((tm, d), lambda i: (i, 0)), pl.BlockSpec((None, d, d), lambda i: (li, 0, 0))],
        out_specs=pl.BlockSpec((tm, d), lambda i: (i, 0)),
        out_shape=jax.ShapeDtypeStruct((t, d), F32),
        compiler_params=_cparams(("parallel",)),
    )(dxo, wc)


def _proj_in_bwd(x, g, dxo, dz, wb, li, name):
    t, d = x.shape
    n = wb.shape[1]
    tm = 512

    def body(x_ref, g_ref, dxo_ref, dz_ref, w_ref, dx_ref, h_ref, dg_ref):
        i = pl.program_id(0)

        @pl.when(i == 0)
        def _():
            dg_ref[...] = jnp.zeros_like(dg_ref)

        xhat, rstd = _rms(x_ref[...])
        h_ref[...] = (xhat * g_ref[...]).astype(BF16)
        dh = _dot(dz_ref[...], w_ref[...])
        dxn, dg = _rms_bwd(xhat, rstd, g_ref[...], dh)
        dx_ref[...] = dxo_ref[...] + dxn
        dg_ref[...] += dg

    row = pl.BlockSpec((tm, d), lambda i: (i, 0))
    vec = pl.BlockSpec((1, d), lambda i: (0, 0))
    return pl.pallas_call(
        body, name=name, grid=(t // tm,),
        in_specs=[row, vec, row, pl.BlockSpec((tm, n), lambda i: (i, 0)),
                  pl.BlockSpec((None, n, d), lambda i: (li, 0, 0))],
        out_specs=[row, row, vec],
        out_shape=[jax.ShapeDtypeStruct((t, d), F32), jax.ShapeDtypeStruct((t, d), BF16),
                   jax.ShapeDtypeStruct((1, d), F32)],
        compiler_params=_cparams(("arbitrary",)),
    )(x, g, dxo, dz, wb)


def _lane_ids(shape):
    return lax.broadcasted_iota(jnp.int32, shape, 1)


def _tril(w):
    r = lax.broadcasted_iota(jnp.int32, w.shape, 0)
    c = lax.broadcasted_iota(jnp.int32, w.shape, 1)
    return jnp.where(r >= c, w, 0.0)


def _shift_down(x, k):
    return x if k == 0 else pltpu.roll(x, k, 0)


def _shift_up(x, k):
    return x if k == 0 else pltpu.roll(x, x.shape[0] - k, 0)


def _sub_tile_shifts(ext, shift):
    return [shift(ext, b) for b in range(8)]


def _tap(shifted, j, n_out, down):
    a, b = divmod(j, 8)
    r0 = HALO - 8 * a if down else 8 * a
    return shifted[b][r0:r0 + n_out]


def _depthwise(shifted, w, n_out, down):
    acc = None
    for j in range(CONV_WIDTH):
        term = _tap(shifted, j, n_out, down) * w[CONV_WIDTH - 1 - j:CONV_WIDTH - j]
        acc = term if acc is None else acc + term
    return acc


def _conv_wgrad(shifted, dhc, n_out):
    return [jnp.sum(_tap(shifted, CONV_WIDTH - 1 - k, n_out, True) * dhc, axis=0, keepdims=True)
            for k in range(CONV_WIDTH)]


def _window_sums(ext, shift):
    s2 = ext + shift(ext, 1)
    s4 = s2 + shift(s2, 2)
    s8 = s4 + shift(s4, 4)
    s16 = s8 + shift(s8, 8)
    grp = _lane_ids(ext.shape) // HEAD_DIM
    return jnp.where(grp == 0, s2, jnp.where(grp == 1, s4, jnp.where(grp == 2, s8, s16)))


def _pool_count(t0, n, width):
    pos = (lax.broadcasted_iota(jnp.int32, (n, width), 0) + (t0 + 1)).astype(F32)
    grp = _lane_ids((n, width)) // HEAD_DIM
    win = jnp.where(grp == 0, 2.0, jnp.where(grp == 1, 4.0, jnp.where(grp == 2, 8.0, 16.0)))
    return jnp.minimum(pos, win)


def _block_diag(pw):
    gn, cg, _ = pw.shape
    rows = []
    for gi in range(gn):
        parts = [pw[gi] if gj == gi else jnp.zeros((cg, cg), pw.dtype) for gj in range(gn)]
        rows.append(jnp.concatenate(parts, axis=1))
    return jnp.concatenate(rows, axis=0)


def _head_pair_mix(w_even, w_odd, v):
    lo = _lane_ids((CHUNK, LANES)) < HEAD_DIM
    return jnp.where(lo, _dot(w_even, v), _dot(w_odd, v))


def _mix_fwd(z, p, name, payload=None):
    t, d_in = z.shape
    sgu = p["sgu_ln_g"].shape[1]
    pool = p["pool_scale"].shape[1]
    d_mix = 2 * sgu + pool
    tm = 512
    n_i = t // tm
    hb = tm // HALO

    def body(z_ref, zp_ref, lng_ref, lnb_ref, ws_ref, bs_ref, cw_ref, cb_ref, clg_ref, clb_ref,
             bd_ref, ps_ref, cat_ref, hc_ref):
        i = pl.program_id(0)
        first = i == 0
        z_main = z_ref[...]
        z_prev = jnp.where(first, 0.0, zp_ref[...])

        lng, lnb = lng_ref[...], lnb_ref[...]
        wt = [_tril(ws_ref[h]).astype(BF16) for h in range(sgu // HEAD_DIM)]
        for n in range(tm // CHUNK):
            rows = slice(n * CHUNK, (n + 1) * CHUNK)
            u = _gelu(z_main[rows, 0:sgu])
            vhat, _ = _ln(_gelu(z_main[rows, sgu:2 * sgu]))
            vn = (vhat * lng + lnb).astype(BF16)
            for gp in range(sgu // LANES):
                ls = slice(gp * LANES, (gp + 1) * LANES)
                mixed = _head_pair_mix(wt[2 * gp], wt[2 * gp + 1], vn[:, ls]) + bs_ref[:, ls]
                cat_ref[rows, ls] = (u[:, ls] * mixed).astype(BF16)

        def glu(zz):
            return zz[:, 2 * sgu:3 * sgu] * jax.nn.sigmoid(zz[:, 3 * sgu:4 * sgu])

        ext = jnp.concatenate([glu(z_prev), glu(z_main)], axis=0)
        hc = _depthwise(_sub_tile_shifts(ext, _shift_down), cw_ref[...], tm, True) + cb_ref[...]
        hc_ref[...] = hc
        hhat, _ = _ln(hc)
        bn = hhat * clg_ref[...] + clb_ref[...]
        cat_ref[:, sgu:2 * sgu] = (bn * jax.nn.sigmoid(bn)).astype(BF16)

        pext = jnp.concatenate([z_prev[:, 4 * sgu:], z_main[:, 4 * sgu:]], axis=0)
        sums = _window_sums(pext, _shift_down)[HALO:]
        pooled = sums / _pool_count(i * tm, tm, pool) - z_main[:, 4 * sgu:]
        mixed_c = _dot(pooled.astype(BF16), bd_ref[...].astype(BF16))
        cat_ref[:, 2 * sgu:] = (mixed_c * ps_ref[...]).astype(BF16)

    def vec(n):
        return pl.BlockSpec((1, n), lambda i: (0, 0))

    (cat, hc), p_outs = _call(
        body, name, (n_i,),
        [pl.BlockSpec((tm, d_in), lambda i: (i, 0)),
         pl.BlockSpec((HALO, d_in), lambda i: (jnp.maximum(i * hb - 1, 0), 0)),
         vec(sgu), vec(sgu),
         pl.BlockSpec(p["w_spatial"].shape, lambda i: (0, 0, 0)),
         pl.BlockSpec((CHUNK, sgu), lambda i: (0, 0)),
         pl.BlockSpec((CONV_WIDTH, sgu), lambda i: (0, 0)),
         vec(sgu), vec(sgu), vec(sgu),
         pl.BlockSpec((pool, pool), lambda i: (0, 0)), vec(pool)],
        [pl.BlockSpec((tm, d_mix), lambda i: (i, 0)), pl.BlockSpec((tm, sgu), lambda i: (i, 0))],
        [jax.ShapeDtypeStruct((t, d_mix), BF16), jax.ShapeDtypeStruct((t, sgu), F32)], [], ("parallel",),
        (z, z, p["sgu_ln_g"], p["sgu_ln_b"], p["w_spatial"], p["bs_full"], p["conv_w"], p["conv_b"],
         p["conv_ln_g"], p["conv_ln_b"], p["bd"], p["pool_scale"]), payload)
    return cat, hc, p_outs


_R_SGU_G, _R_SGU_B, _R_CONV_B, _R_CLN_G, _R_CLN_B, _R_CONV_W = 0, 1, 2, 3, 4, 8
_R384_ROWS = 40


def _mix_bwd(z, hc_saved, dxo, wc, li, p, name, payload=None):
    t, d_in = z.shape
    sgu = p["sgu_ln_g"].shape[1]
    pool = p["pool_scale"].shape[1]
    d_mix = 2 * sgu + pool
    n_head = sgu // HEAD_DIM
    tm = 512
    n_i = t // tm
    hb = tm // HALO

    def body(z_ref, zp_ref, zn_ref, dx_ref, dxn_ref, wo_ref, hc_ref, hcn_ref, lng_ref, lnb_ref, ws_ref, bs_ref,
             cw_ref, clg_ref, clb_ref, bd_ref, ps_ref, dz_ref, g384_ref, gws_ref, gpool_ref, dbs_scr):
        i = pl.program_id(0)
        first, last = i == 0, i == n_i - 1

        @pl.when(first)
        def _():
            g384_ref[...] = jnp.zeros_like(g384_ref)
            gws_ref[...] = jnp.zeros_like(gws_ref)
            gpool_ref[...] = jnp.zeros_like(gpool_ref)
            dbs_scr[...] = jnp.zeros_like(dbs_scr)

        z_main = z_ref[...]
        z_prev = jnp.where(first, 0.0, zp_ref[...])
        z_next = jnp.where(last, 0.0, zn_ref[...])
        wo = wo_ref[...]
        dc_main = _dot_nt(dx_ref[...].astype(BF16), wo)
        dc_next = jnp.where(last, 0.0, _dot_nt(dxn_ref[...].astype(BF16), wo))

        lng, lnb = lng_ref[...], lnb_ref[...]
        wt = [_tril(ws_ref[h]) for h in range(n_head)]
        wt_b = [w.astype(BF16) for w in wt]
        wtt_b = [w.T.astype(BF16) for w in wt]
        lo = _lane_ids((CHUNK, LANES)) < HEAD_DIM
        d_lng = jnp.zeros((1, sgu), F32)
        d_lnb = jnp.zeros((1, sgu), F32)
        dws = [jnp.zeros((CHUNK, CHUNK), F32) for _ in range(n_head)]
        for n in range(tm // CHUNK):
            rows = slice(n * CHUNK, (n + 1) * CHUNK)
            au, av = z_main[rows, 0:sgu], z_main[rows, sgu:2 * sgu]
            u = _gelu(au)
            vhat, vrstd = _ln(_gelu(av))
            vn = (vhat * lng + lnb).astype(BF16)
            da = dc_main[rows, 0:sgu]
            dmixed = da * u
            dbs_scr[...] += dmixed
            dvn_parts, du_parts = [], []
            for gp in range(sgu // LANES):
                ls = slice(gp * LANES, (gp + 1) * LANES)
                vn_g = vn[:, ls]
                mixed = _head_pair_mix(wt_b[2 * gp], wt_b[2 * gp + 1], vn_g) + bs_ref[:, ls]
                du_parts.append(da[:, ls] * mixed)
                dm_g = dmixed[:, ls]
                dm_b = dm_g.astype(BF16)
                dvn_parts.append(jnp.where(lo, _dot(wtt_b[2 * gp], dm_b), _dot(wtt_b[2 * gp + 1], dm_b)))
                dws[2 * gp] = dws[2 * gp] + _dot_nt(jnp.where(lo, dm_g, 0.0).astype(BF16), vn_g)
                dws[2 * gp + 1] = dws[2 * gp + 1] + _dot_nt(jnp.where(lo, 0.0, dm_g).astype(BF16), vn_g)
            dvn = jnp.concatenate(dvn_parts, axis=1)
            du = jnp.concatenate(du_parts, axis=1)
            dv, dg_n, db_n = _ln_bwd(vhat, vrstd, lng, dvn)
            d_lng = d_lng + dg_n
            d_lnb = d_lnb + db_n
            dz_ref[rows, 0:sgu] = (du * _gelu_grad(au)).astype(BF16)
            dz_ref[rows, sgu:2 * sgu] = (dv * _gelu_grad(av)).astype(BF16)
        for h in range(n_head):
            gws_ref[h] += _tril(dws[h])
        g384_ref[_R_SGU_G:_R_SGU_G + 1, :] += d_lng
        g384_ref[_R_SGU_B:_R_SGU_B + 1, :] += d_lnb

        clg = clg_ref[...]
        bcols = slice(2 * sgu, 4 * sgu)
        zb = jnp.concatenate([z_prev[:, bcols], z_main[:, bcols], z_next[:, bcols]], axis=0)
        bval, bgate = zb[:, 0:sgu], zb[:, sgu:2 * sgu]
        sg = jax.nn.sigmoid(bgate)
        hglu = bval * sg
        n_out = tm + HALO
        hglu_shifts = _sub_tile_shifts(hglu, _shift_down)
        cw = cw_ref[...]
        hc = jnp.concatenate([hc_ref[...], jnp.where(last, 0.0, hcn_ref[...])], axis=0)
        hhat, hrstd = _ln(hc)
        bn = hhat * clg + clb_ref[...]
        db = jnp.concatenate([dc_main[:, sgu:2 * sgu], dc_next[:, sgu:2 * sgu]], axis=0)
        dbn = db * _silu_grad(bn)
        dhc_all, _, _ = _ln_bwd(hhat, hrstd, clg, dbn)
        dbn_m, hhat_m, dhc = dbn[:tm], hhat[:tm], dhc_all[:tm]
        g384_ref[_R_CLN_G:_R_CLN_G + 1, :] += jnp.sum(dbn_m * hhat_m, axis=0, keepdims=True)
        g384_ref[_R_CLN_B:_R_CLN_B + 1, :] += jnp.sum(dbn_m, axis=0, keepdims=True)
        g384_ref[_R_CONV_B:_R_CONV_B + 1, :] += jnp.sum(dhc, axis=0, keepdims=True)
        wrows = _conv_wgrad(hglu_shifts, dhc, tm)
        for k in range(CONV_WIDTH):
            g384_ref[_R_CONV_W + k:_R_CONV_W + k + 1, :] += wrows[k]
        dhglu = _depthwise(_sub_tile_shifts(dhc_all, _shift_up), cw, tm, False)
        bval_m, sg_m = bval[HALO:HALO + tm], sg[HALO:HALO + tm]
        dz_ref[:, 2 * sgu:3 * sgu] = (dhglu * sg_m).astype(BF16)
        dz_ref[:, 3 * sgu:4 * sgu] = (dhglu * bval_m * sg_m * (1.0 - sg_m)).astype(BF16)

        bd_b = bd_ref[...].astype(BF16)
        ps = ps_ref[...]
        p_main = z_main[:, 4 * sgu:]
        pext = jnp.concatenate([z_prev[:, 4 * sgu:], p_main], axis=0)
        cnt = _pool_count(i * tm, n_out, pool)
        pooled = _window_sums(pext, _shift_down)[HALO:] / cnt[:tm] - p_main
        pooled_b = pooled.astype(BF16)
        dcc = jnp.concatenate([dc_main[:, 2 * sgu:], dc_next[:, 2 * sgu:]], axis=0)
        dmix_c = dcc * ps
        mixed_c = _dot(pooled_b, bd_b)
        grp_r = lax.broadcasted_iota(jnp.int32, (pool, pool), 0) // HEAD_DIM
        grp_c = lax.broadcasted_iota(jnp.int32, (pool, pool), 1) // HEAD_DIM
        gpool_ref[0:pool, :] += jnp.where(grp_r == grp_c, _dot_tn(pooled_b, dmix_c[:tm].astype(BF16)), 0.0)
        gpool_ref[pool:pool + 1, :] += jnp.sum(dcc[:tm] * mixed_c, axis=0, keepdims=True)
        dpooled = _dot_nt(dmix_c.astype(BF16), bd_b)
        q = dpooled / cnt
        dp = _window_sums(q, _shift_up)[:tm] - dpooled[:tm]
        dz_ref[:, 4 * sgu:] = dp.astype(BF16)

        @pl.when(last)
        def _():
            r = lax.broadcasted_iota(jnp.int32, (sgu, LANES), 0)
            c = lax.broadcasted_iota(jnp.int32, (sgu, LANES), 1)
            sel = (r // HEAD_DIM == c).astype(BF16)
            gws_ref[n_head] = _split_dot(dbs_scr[...], sel)

    def vec(n):
        return pl.BlockSpec((1, n), lambda i: (0, 0))

    def prev_map(i):
        return (jnp.maximum(i * hb - 1, 0), 0)

    def next_map(i):
        return (jnp.minimum((i + 1) * hb, n_i * hb - 1), 0)

    return _call(
        body, name, (n_i,),
        [pl.BlockSpec((tm, d_in), lambda i: (i, 0)),
         pl.BlockSpec((HALO, d_in), prev_map), pl.BlockSpec((HALO, d_in), next_map),
         pl.BlockSpec((tm, d_mix), lambda i: (i, 0)), pl.BlockSpec((HALO, d_mix), next_map),
         pl.BlockSpec((None, d_mix, dxo.shape[1]), lambda i: (li, 0, 0)),
         pl.BlockSpec((tm, sgu), lambda i: (i, 0)), pl.BlockSpec((HALO, sgu), next_map),
         vec(sgu), vec(sgu),
         pl.BlockSpec(p["w_spatial"].shape, lambda i: (0, 0, 0)),
         pl.BlockSpec((CHUNK, sgu), lambda i: (0, 0)),
         pl.BlockSpec((CONV_WIDTH, sgu), lambda i: (0, 0)),
         vec(sgu), vec(sgu),
         pl.BlockSpec((pool, pool), lambda i: (0, 0)), vec(pool)],
        [pl.BlockSpec((tm, d_in), lambda i: (i, 0)),
         pl.BlockSpec((_R384_ROWS, sgu), lambda i: (0, 0)),
         pl.BlockSpec((n_head + 1, CHUNK, CHUNK), lambda i: (0, 0, 0)),
         pl.BlockSpec((pool + 8, pool), lambda i: (0, 0))],
        [jax.ShapeDtypeStruct((t, d_in), BF16),
         jax.ShapeDtypeStruct((_R384_ROWS, sgu), F32),
         jax.ShapeDtypeStruct((n_head + 1, CHUNK, CHUNK), F32),
         jax.ShapeDtypeStruct((pool + 8, pool), F32)],
        [pltpu.VMEM((CHUNK, sgu), F32)], ("arbitrary",),
        (z, z, z, dxo, dxo, wc, hc_saved, hc_saved, p["sgu_ln_g"], p["sgu_ln_b"], p["w_spatial"], p["bs_full"],
         p["conv_w"], p["conv_ln_g"], p["conv_ln_b"], p["bd"], p["pool_scale"]), payload)


def _loss_head(x, g, target, name):
    t, d = x.shape
    tm = 512

    def body(x_ref, g_ref, tgt_ref, dx_ref, dg_ref, loss_ref):
        i = pl.program_id(0)

        @pl.when(i == 0)
        def _():
            dg_ref[...] = jnp.zeros_like(dg_ref)
            loss_ref[...] = jnp.zeros_like(loss_ref)

        gv = g_ref[...]
        xhat, rstd = _rms(x_ref[...])
        err = xhat * gv - tgt_ref[...]
        loss_ref[...] += jnp.zeros_like(loss_ref) + 0.5 * jnp.sum(jnp.mean(err * err, axis=-1, keepdims=True))
        dxn, dg = _rms_bwd(xhat, rstd, gv, err * (1.0 / d))
        dx_ref[...] = dxn
        dg_ref[...] += dg

    row = pl.BlockSpec((tm, d), lambda i: (i, 0))
    vec = pl.BlockSpec((1, d), lambda i: (0, 0))
    return pl.pallas_call(
        body, name=name, grid=(t // tm,),
        in_specs=[row, vec, row],
        out_specs=[row, vec, pl.BlockSpec((1, LANES), lambda i: (0, 0))],
        out_shape=[jax.ShapeDtypeStruct((t, d), F32), jax.ShapeDtypeStruct((1, d), F32),
                   jax.ShapeDtypeStruct((1, LANES), F32)],
        compiler_params=_cparams(("arbitrary",)),
    )(x, g, target)


def _all_gather(arrs, name, extra=None):
    gather = _GatherIci(arrs)
    n = len(arrs)
    forward = _GatherForward([jax.ShapeDtypeStruct(s.shape, s.dtype) for s in gather.out_shapes])
    x_in = len(extra.ins) if extra else 0
    x_out = len(extra.out_shapes) if extra else 0

    def body(*refs):
        ins, x_ins = refs[:n], refs[n:n + x_in]
        outs, x_outs = refs[n + x_in:2 * n + x_in], refs[2 * n + x_in:2 * n + x_in + x_out]
        sems = refs[2 * n + x_in + x_out:]
        first = gather.build(ins, outs, *sems[0:3])
        first.start()
        if extra:
            beside = extra.build(x_ins, x_outs, *sems[6:9])
            beside.start()
        first.wait()
        second = forward.build(outs, outs, *sems[3:6])
        second.start()
        second.wait()
        if extra:
            beside.wait()

    outs = pl.pallas_call(
        body, name=name,
        in_specs=[ANY] * (n + x_in), out_specs=[ANY] * (n + x_out),
        out_shape=list(gather.out_shapes) + (list(extra.out_shapes) if extra else []),
        scratch_shapes=gather.sem_shapes() + forward.sem_shapes() + (extra.sem_shapes() if extra else []),
    )(*arrs, *(extra.ins if extra else []))
    return list(outs[:n]), list(outs[n:])


def _all_gather_relayed(arrs, name):
    n = len(arrs)
    n_pairs = 8

    def body(*refs):
        ins, outs = refs[:n], refs[n:2 * n]
        send_sems, recv_sems, local_sems = refs[2 * n:]
        x, y, c = _position()
        sib, xn, yn = (x, y, 1 - c), (1 - x, y, c), (x, 1 - y, c)

        def rows(a, px, py, pc, half=None):
            r = ins[a].shape[1]
            base = (4 * px + 2 * py + pc) * r
            if half is None:
                return outs[a].at[:, pl.ds(base, r), :]
            return outs[a].at[:, pl.ds(base + half * (r // 2), r // 2), :]

        def send(a, k, src, dst, to):
            return _remote(src, dst, send_sems, recv_sems, a * n_pairs + k, to)

        def arrived(a, k, land, sender):
            _remote(land, land, send_sems, recv_sems, a * n_pairs + k, sender).wait_recv()

        own = [pltpu.make_async_copy(ins[a], rows(a, x, y, c), local_sems.at[a]) for a in range(n)]
        first = [send(a, k, ins[a], rows(a, x, y, c), to) for a in range(n) for k, to in enumerate((sib, xn, yn))]
        for cp in own + first:
            cp.start()
        for a in range(n):
            arrived(a, 1, rows(a, *xn), xn)
            arrived(a, 2, rows(a, *yn), yn)
        second = []
        for a in range(n):
            second += [send(a, 3, rows(a, *xn, half=0), rows(a, *xn, half=0), yn),
                       send(a, 4, rows(a, *yn, half=1), rows(a, *yn, half=1), xn),
                       send(a, 5, rows(a, *xn), rows(a, *xn), sib),
                       send(a, 6, rows(a, *yn), rows(a, *yn), sib)]
        for cp in second:
            cp.start()
        for a in range(n):
            arrived(a, 3, rows(a, 1 - x, 1 - y, c, half=0), yn)
            arrived(a, 4, rows(a, 1 - x, 1 - y, c, half=1), xn)
        third = [send(a, 7, rows(a, 1 - x, 1 - y, c), rows(a, 1 - x, 1 - y, c), sib) for a in range(n)]
        for cp in third:
            cp.start()
        for a in range(n):
            arrived(a, 0, rows(a, *sib), sib)
            arrived(a, 5, rows(a, 1 - x, y, 1 - c), sib)
            arrived(a, 6, rows(a, x, 1 - y, 1 - c), sib)
            arrived(a, 7, rows(a, 1 - x, 1 - y, 1 - c), sib)
        for cp in first + second + third:
            cp.wait_send()
        for cp in own:
            cp.wait()

    return list(pl.pallas_call(
        body, name=name, in_specs=[ANY] * n, out_specs=[ANY] * n,
        out_shape=[jax.ShapeDtypeStruct((a.shape[0], N_DEV * a.shape[1], a.shape[2]), a.dtype) for a in arrs],
        scratch_shapes=[pltpu.SemaphoreType.DMA((n_pairs * n,)), pltpu.SemaphoreType.DMA((n_pairs * n,)),
                        pltpu.SemaphoreType.DMA((n,))],
    )(*arrs))


def _pair_sums(grads, recvs, cidx, name):
    n = len(grads)

    def body(c_ref, *refs):
        for g_ref, r_ref, o_ref in zip(refs[:n], refs[n:2 * n], refs[2 * n:]):
            o_ref[...] = (g_ref[...].astype(F32) + r_ref[...].astype(F32)).astype(BF16)

    shapes = [(g.shape[0] // N_DEV, g.shape[1]) for g in grads]
    return list(pl.pallas_call(
        body, name=name,
        grid_spec=pltpu.PrefetchScalarGridSpec(
            num_scalar_prefetch=1, grid=(N_CHIP,),
            in_specs=[pl.BlockSpec(s, lambda q, c: (2 * q + c[0], 0)) for s in shapes]
            + [pl.BlockSpec(s, lambda q, c: (q, 0)) for s in shapes],
            out_specs=[pl.BlockSpec(s, lambda q, c: (q, 0)) for s in shapes]),
        out_shape=[jax.ShapeDtypeStruct((N_CHIP * r, cols), BF16) for r, cols in shapes],
        compiler_params=_cparams(("parallel",)),
    )(cidx, *grads, *recvs))


def _sum_blocks(parts, nblk, name):
    r = parts.shape[0] // nblk
    cols = parts.shape[1]

    def body(p_ref, o_ref):
        acc = p_ref[0:r, :].astype(F32)
        for q in range(1, nblk):
            acc = acc + p_ref[q * r:(q + 1) * r, :].astype(F32)
        o_ref[...] = acc

    return pl.pallas_call(
        body, name=name,
        out_shape=jax.ShapeDtypeStruct((r, cols), F32),
        compiler_params=_cparams(),
    )(parts)


def _adamw_math(w, g, m, v):
    m = ADAM_B1 * m + (1.0 - ADAM_B1) * g
    v = ADAM_B2 * v + (1.0 - ADAM_B2) * (g * g)
    m_hat = m / (1.0 - ADAM_B1 ** ADAM_STEP)
    v_hat = v / (1.0 - ADAM_B2 ** ADAM_STEP)
    delta = -ADAM_LR * (m_hat / (jnp.sqrt(v_hat) + ADAM_EPS) + ADAM_WD * w)
    return delta, m, v


def _finish_sharded(parts, w, m, v, name):
    depth, rr, cw = w.shape

    def body(*refs):
        p_refs = refs[:depth]
        w_ref, m_ref, v_ref, g_ref, d_ref, mo_ref, vo_ref = refs[depth:]
        l = pl.program_id(0)
        for k in range(depth):
            @pl.when(l == k)
            def _(p_ref=p_refs[k]):
                r = p_ref.shape[0] // N_CHIP
                acc = p_ref[0:r, :].astype(F32)
                for q in range(1, N_CHIP):
                    acc = acc + p_ref[q * r:(q + 1) * r, :].astype(F32)
                g_ref[...] = acc
                d_ref[...], mo_ref[...], vo_ref[...] = _adamw_math(w_ref[...], acc, m_ref[...], v_ref[...])

    blk = pl.BlockSpec((None, rr, cw), lambda l: (l, 0, 0))
    return pl.pallas_call(
        body, name=name, grid=(depth,),
        in_specs=[pl.BlockSpec(p.shape, lambda l: (0, 0)) for p in parts] + [blk] * 3, out_specs=[blk] * 4,
        out_shape=[jax.ShapeDtypeStruct(w.shape, F32)] * 4,
        compiler_params=_cparams(("arbitrary",)),
    )(*parts, w, m, v)


def _adamw_small(ws, gs, ms, vs, name):
    n = len(ws)

    def body(*refs):
        for k in range(n):
            w_ref, g_ref, m_ref, v_ref = (refs[j * n + k] for j in range(4))
            d_ref, mo_ref, vo_ref = (refs[(4 + j) * n + k] for j in range(3))
            d_ref[...], mo_ref[...], vo_ref[...] = _adamw_math(w_ref[...], g_ref[...], m_ref[...], v_ref[...])

    shapes = [jax.ShapeDtypeStruct(w.shape, F32) for w in ws]
    return pl.pallas_call(
        body, name=name, out_shape=shapes * 3, compiler_params=_cparams(),
    )(*ws, *gs, *ms, *vs)


def kernel(x, ffn1_norm, ffn1_w_gate, ffn1_w_up, ffn1_w_down, mix_norm, w_in, sgu_ln_g, sgu_ln_b, w_spatial, b_spatial, conv_w, conv_b, conv_ln_g, conv_ln_b, pool_w, pool_scale, w_out, ffn2_norm, ffn2_w_gate, ffn2_w_up, ffn2_w_down, final_norm, loss_target, m_ffn1_norm, m_ffn1_w_gate, m_ffn1_w_up, m_ffn1_w_down, m_mix_norm, m_w_in, m_sgu_ln_g, m_sgu_ln_b, m_w_spatial, m_b_spatial, m_conv_w, m_conv_b, m_conv_ln_g, m_conv_ln_b, m_pool_w, m_pool_scale, m_w_out, m_ffn2_norm, m_ffn2_w_gate, m_ffn2_w_up, m_ffn2_w_down, m_final_norm, v_ffn1_norm, v_ffn1_w_gate, v_ffn1_w_up, v_ffn1_w_down, v_mix_norm, v_w_in, v_sgu_ln_g, v_sgu_ln_b, v_w_spatial, v_b_spatial, v_conv_w, v_conv_b, v_conv_ln_g, v_conv_ln_b, v_pool_w, v_pool_scale, v_w_out, v_ffn2_norm, v_ffn2_w_gate, v_ffn2_w_up, v_ffn2_w_down, v_final_norm):
    names = ["ffn1_norm", "ffn1_w_gate", "ffn1_w_up", "ffn1_w_down", "mix_norm", "w_in", "sgu_ln_g", "sgu_ln_b",
             "w_spatial", "b_spatial", "conv_w", "conv_b", "conv_ln_g", "conv_ln_b", "pool_w", "pool_scale",
             "w_out", "ffn2_norm", "ffn2_w_gate", "ffn2_w_up", "ffn2_w_down", "final_norm"]
    W = dict(zip(names, [ffn1_norm, ffn1_w_gate, ffn1_w_up, ffn1_w_down, mix_norm, w_in, sgu_ln_g, sgu_ln_b,
                         w_spatial, b_spatial, conv_w, conv_b, conv_ln_g, conv_ln_b, pool_w, pool_scale, w_out,
                         ffn2_norm, ffn2_w_gate, ffn2_w_up, ffn2_w_down, final_norm]))
    M = dict(zip(names, [m_ffn1_norm, m_ffn1_w_gate, m_ffn1_w_up, m_ffn1_w_down, m_mix_norm, m_w_in, m_sgu_ln_g,
                         m_sgu_ln_b, m_w_spatial, m_b_spatial, m_conv_w, m_conv_b, m_conv_ln_g, m_conv_ln_b,
                         m_pool_w, m_pool_scale, m_w_out, m_ffn2_norm, m_ffn2_w_gate, m_ffn2_w_up, m_ffn2_w_down,
                         m_final_norm]))
    V = dict(zip(names, [v_ffn1_norm, v_ffn1_w_gate, v_ffn1_w_up, v_ffn1_w_down, v_mix_norm, v_w_in, v_sgu_ln_g,
                         v_sgu_ln_b, v_w_spatial, v_b_spatial, v_conv_w, v_conv_b, v_conv_ln_g, v_conv_ln_b,
                         v_pool_w, v_pool_scale, v_w_out, v_ffn2_norm, v_ffn2_w_gate, v_ffn2_w_up, v_ffn2_w_down,
                         v_final_norm]))

    depth, d = ffn1_norm.shape
    t = x.shape[1]
    sgu = sgu_ln_g.shape[1]
    pool = pool_scale.shape[1]
    n_head = sgu // HEAD_DIM
    cw_shard = conv_w.shape[2]
    xs = x.reshape(t, d)
    target = loss_target.reshape(t, d)

    def tr(w):
        return jnp.swapaxes(w, 1, 2).astype(BF16)

    ffn_shards = [[jnp.stack([tr(ffn1_w_gate)[l], tr(ffn1_w_up)[l], ffn1_w_down[l].astype(BF16)]),
                   jnp.stack([tr(ffn2_w_gate)[l], tr(ffn2_w_up)[l], ffn2_w_down[l].astype(BF16)])]
                  for l in range(depth)]
    win_shards = [tr(w_in)[l:l + 1] for l in range(depth)]
    wout_shards = [w_out[l:l + 1].astype(BF16) for l in range(depth)]
    cw_rows = depth * CONV_WIDTH
    cw_pad = -cw_rows % 8
    cw_send = jnp.pad(conv_w.reshape(cw_rows, cw_shard), ((0, cw_pad), (0, 0)))[None]
    wffn, wb, wc = {}, {}, {}
    wffn[(0, 0)], wb[0], wc[0], cwg = _all_gather_relayed(
        [ffn_shards[0][0], win_shards[0], wout_shards[0], cw_send], "ag_first")
    conv_w_full = cwg.reshape(N_DEV, cw_rows + cw_pad, cw_shard)[:, :cw_rows].reshape(
        N_DEV, depth, CONV_WIDTH, cw_shard).transpose(1, 2, 0, 3).reshape(depth, CONV_WIDTH, N_DEV * cw_shard)

    def mixer_params(l):
        return dict(
            sgu_ln_g=sgu_ln_g[l:l + 1], sgu_ln_b=sgu_ln_b[l:l + 1], w_spatial=w_spatial[l],
            bs_full=jnp.repeat(b_spatial[l].T, HEAD_DIM, axis=1),
            conv_w=conv_w_full[l], conv_b=conv_b[l:l + 1], conv_ln_g=conv_ln_g[l:l + 1],
            conv_ln_b=conv_ln_b[l:l + 1], bd=_block_diag(pool_w[l]), pool_scale=pool_scale[l:l + 1])

    saved = []
    cur = xs
    for l in range(depth):
        p = mixer_params(l)
        x0 = cur
        more = l + 1 < depth
        (x1, gate1, up1, act1), part = _ffn_fwd(x0, ffn1_norm[l:l + 1], wffn[(l, 0)], 0, f"ffn1_fwd_{l}",
                                          _GatherIci([ffn_shards[l][1]]))
        z, (wffn[(l, 1)],) = _proj_in_fwd(x1, mix_norm[l:l + 1], wb[l], 0, f"proj_in_fwd_{l}", _GatherForward(part))
        cat, hc, part = _mix_fwd(z, p, f"mix_fwd_{l}",
                             _GatherIci([win_shards[l + 1], wout_shards[l + 1]]) if more else None)
        x2, part = _proj_out_fwd(x1, cat, wc[l], 0, f"proj_out_fwd_{l}", _GatherForward(part) if more else None)
        if more:
            wb[l + 1], wc[l + 1] = part
        (x3, gate2, up2, act2), part = _ffn_fwd(x2, ffn2_norm[l:l + 1], wffn[(l, 1)], 0, f"ffn2_fwd_{l}",
                                          _GatherIci([ffn_shards[l + 1][0]]) if more else None)
        if more:
            (wffn[(l + 1, 0)],) = _comm(_GatherForward(part), f"ag_forward_{l + 1}")
        saved.append((p, x0, gate1, up1, act1, x1, z, hc, cat, x2, gate2, up2, act2))
        cur = x3

    dx, d_final, loss_part = _loss_head(cur, final_norm.reshape(1, d), target, "loss_head")

    cidx = lax.axis_index("c").astype(jnp.int32).reshape(1)
    from_chips = {}
    to_pair, to_chip = [], []
    small = []

    def pair_payload():
        return _PairExchange([g for _, g in to_pair]) if to_pair else None

    def pair_done(received):
        if to_pair:
            (nm, l), _ = to_pair[0]
            sums = _pair_sums([g for _, g in to_pair], list(received), cidx, f"rs_pair_sum_{nm}_{l}")
            to_chip.extend((key, s) for (key, _), s in zip(to_pair, sums))
        to_pair.clear()

    def take_chip():
        items = list(to_chip)
        to_chip.clear()
        return items

    def chip_payload(items):
        return _ChipExchange([s for _, s in items]) if items else None

    def chip_done(items, landed):
        for (key, _), o in zip(items, landed):
            from_chips[key] = o

    def ffn_weight_grads(prefix, l, dgate, dup, act, h, dy):
        items = take_chip()
        items, later = items[:2], items[2:]
        to_chip.extend(later)
        grads3, landed = _ffn_dw(dgate, dup, act, h, dy, f"dw_{prefix}_{l}", chip_payload(items))
        chip_done(items, landed)
        to_pair.extend(((f"{prefix}_{nm}", l), g) for nm, g in zip(("w_gate", "w_up", "w_down"), grads3))

    for l in reversed(range(depth)):
        p, x0, gate1, up1, act1, x1, z, hc, cat, x2, gate2, up2, act2 = saved[l]
        (dx, dgate, dup, h, dy, dg_ffn2), received = _ffn_bwd(
            x2, ffn2_norm[l:l + 1], dx, gate2, up2, wffn[(l, 1)], 0, f"ffn2_bwd_{l}", pair_payload())
        pair_done(received)
        ffn_weight_grads("ffn2", l, dgate, dup, act2, h, dy)
        g_out, received = _tn_matmul(cat, dx, f"dw_w_out_{l}", pair_payload())
        pair_done(received)
        items = take_chip()
        items, later = items[:3], items[3:]
        to_chip.extend(later)
        (dz, g384, gws, gpool), landed = _mix_bwd(z, hc, dx, wc[l], 0, p, f"mix_bwd_{l}", chip_payload(items))
        chip_done(items, landed)
        dx, hm, dg_mix = _proj_in_bwd(x1, mix_norm[l:l + 1], dx, dz, wb[l], 0, f"proj_in_bwd_{l}")
        g_in, _ = _tn_matmul(dz, hm, f"dw_w_in_{l}")
        to_pair.extend([(("w_out", l), g_out), (("w_in", l), g_in)])
        if l > 0:
            (dx, dgate, dup, h, dy, dg_ffn1), received = _ffn_bwd(
                x0, ffn1_norm[l:l + 1], dx, gate1, up1, wffn[(l, 0)], 0, f"ffn1_bwd_{l}", pair_payload())
            pair_done(received)
            ffn_weight_grads("ffn1", l, dgate, dup, act1, h, dy)
            small.append((l, g384, gws, gpool, dg_ffn1, dg_mix, dg_ffn2))
            continue

        small.append((0, g384, gws, gpool, None, dg_mix, dg_ffn2))
        small.sort(key=lambda s: s[0])
        norm_rows = []
        for (sl, _, _, _, dg1, dgm, dg2) in small:
            norm_rows += [jnp.zeros((1, d), F32) if dg1 is None else dg1, dgm, dg2]
        norm_rows += [d_final, jnp.pad(loss_part, ((0, 0), (0, d - LANES)))]
        n_norm = len(norm_rows)
        norm_pack = jnp.concatenate(norm_rows + [jnp.zeros((8 - n_norm % 8, d), F32)] * (n_norm % 8 != 0), axis=0)
        parts = [norm_pack]
        for (_, s384, sws, spool, _, _, _) in small:
            parts += [s384, sws.reshape((n_head + 1) * CHUNK, CHUNK), spool]
        n_pair = len(to_pair)
        early = take_chip()
        riding = [pair_payload(), _GatherIci([a[None] for a in parts])] + ([chip_payload(early)] if early else [])
        (dx, dgate, dup, h, dy, dg_ffn1), landed = _ffn_bwd(
            x0, ffn1_norm[l:l + 1], dx, gate1, up1, wffn[(l, 0)], 0, f"ffn1_bwd_{l}", _Merged(riding))
        pair_done(landed[:n_pair])
        chip_done(early, landed[n_pair + len(parts):])
        items = take_chip()
        g_gate, landed = _tn_matmul(
            dgate, h, f"dw_ffn1_w_gate_{l}",
            _Merged([chip_payload(items), _GatherForward(landed[n_pair:n_pair + len(parts)])]))
        chip_done(items, landed[:len(items)])
        gathered = landed[len(items):]
        to_pair.append((("ffn1_w_gate", l), g_gate))
        g_up, received = _tn_matmul(dup, h, f"dw_ffn1_w_up_{l}", pair_payload())
        pair_done(received)
        to_pair.append((("ffn1_w_up", l), g_up))
        items = take_chip()
        g_down, landed = _tn_matmul(act1, dy, f"dw_ffn1_w_down_{l}", _Merged([chip_payload(items), pair_payload()]))
        chip_done(items, landed[:len(items)])
        pair_done(landed[len(items):])
        to_pair.append((("ffn1_w_down", l), g_down))
    grad_x = dx.reshape(x.shape)
    pair_done(_comm(pair_payload(), "rs_pair_exchange_last"))
    items = take_chip()
    (late_norm,), landed = _all_gather([jnp.pad(dg_ffn1, ((0, 7), (0, 0)))[None]], "ag_tail", chip_payload(items))
    chip_done(items, landed)

    summed = [_sum_blocks(g[0], N_DEV, f"sum_small_{k}") for k, g in enumerate(gathered)]
    late_sum = _sum_blocks(late_norm[0], N_DEV, "sum_small_late")
    norm_sum = summed[0]
    loss = norm_sum[3 * depth + 1, 0]
    cpos = lax.axis_index("x") * 4 + lax.axis_index("y") * 2 + lax.axis_index("c")
    sg = {nm: [] for nm in names}
    for l in range(depth):
        g384, gws, gpool = summed[1 + 3 * l], summed[2 + 3 * l].reshape(n_head + 1, CHUNK, CHUNK), summed[3 + 3 * l]
        sg["ffn1_norm"].append(norm_sum[3 * l] if l > 0 else late_sum[0])
        sg["mix_norm"].append(norm_sum[3 * l + 1])
        sg["ffn2_norm"].append(norm_sum[3 * l + 2])
        sg["sgu_ln_g"].append(g384[_R_SGU_G])
        sg["sgu_ln_b"].append(g384[_R_SGU_B])
        sg["conv_b"].append(g384[_R_CONV_B])
        sg["conv_ln_g"].append(g384[_R_CLN_G])
        sg["conv_ln_b"].append(g384[_R_CLN_B])
        sg["conv_w"].append(lax.dynamic_slice_in_dim(g384[_R_CONV_W:_R_CONV_W + CONV_WIDTH], cpos * cw_shard,
                                                     cw_shard, axis=1))
        sg["w_spatial"].append(gws[:n_head])
        sg["b_spatial"].append(gws[n_head][:, :n_head].T)
        sg["pool_w"].append(jnp.stack([gpool[k * HEAD_DIM:(k + 1) * HEAD_DIM, k * HEAD_DIM:(k + 1) * HEAD_DIM]
                                       for k in range(pool // HEAD_DIM)], axis=0))
        sg["pool_scale"].append(gpool[pool])
    small_names = ["ffn1_norm", "mix_norm", "sgu_ln_g", "sgu_ln_b", "w_spatial", "b_spatial", "conv_w", "conv_b",
                   "conv_ln_g", "conv_ln_b", "pool_w", "pool_scale", "ffn2_norm"]
    grads = {nm: jnp.stack(sg[nm], axis=0) for nm in small_names}
    grads["final_norm"] = norm_sum[3 * depth]

    delta, new_m, new_v = {}, {}, {}
    big_names = ["ffn1_w_gate", "ffn1_w_up", "ffn1_w_down", "w_in", "w_out", "ffn2_w_gate", "ffn2_w_up",
                 "ffn2_w_down"]
    transposed = {"ffn1_w_gate", "ffn1_w_up", "w_in", "ffn2_w_gate", "ffn2_w_up"}
    for nm in big_names:
        view = (lambda a: jnp.swapaxes(a, 1, 2)) if nm in transposed else (lambda a: a)
        outs = _finish_sharded([from_chips[(nm, l)] for l in range(depth)], view(W[nm]), view(M[nm]), view(V[nm]),
                               f"adamw_{nm}")
        grads[nm], delta[nm], new_m[nm], new_v[nm] = (view(o) for o in outs)
    snames = small_names + ["final_norm"]

    def flat2(a):
        return a.reshape(-1, a.shape[-1])

    outs = _adamw_small([flat2(W[nm]) for nm in snames], [flat2(grads[nm]) for nm in snames],
                        [flat2(M[nm]) for nm in snames], [flat2(V[nm]) for nm in snames], "adamw_small")
    ns = len(snames)
    for k, nm in enumerate(snames):
        shp = W[nm].shape
        delta[nm], new_m[nm], new_v[nm] = (outs[k].reshape(shp), outs[ns + k].reshape(shp),
                                           outs[2 * ns + k].reshape(shp))

    return (loss, grad_x, *[grads[nm] for nm in names], *[delta[nm] for nm in names],
            *[new_m[nm] for nm in names], *[new_v[nm] for nm in names])
```

```python
import functools

import jax
import jax.numpy as jnp
from jax import lax
from jax.experimental import pallas as pl
from jax.experimental.pallas import tpu as pltpu

F32 = jnp.float32
BF16 = jnp.bfloat16
EPS = 1e-6
N_DEV = 8
N_CHIP = 4
MESH = pl.DeviceIdType.MESH
ANY = pl.BlockSpec(memory_space=pl.ANY)

VMEM_LIMIT_BYTES = 56 * 1024 * 1024
LANES = 128
HALO = 32
HEAD_DIM = 64
CHUNK = 128
CONV_WIDTH = 31
POOL_WINDOWS = (2, 4, 8, 16)

ADAM_LR = 0.001
ADAM_B1 = 0.9
ADAM_B2 = 0.999
ADAM_EPS = 1e-08
ADAM_WD = 0.01
ADAM_STEP = 10


def _cparams(sem=None):
    return pltpu.CompilerParams(dimension_semantics=sem, vmem_limit_bytes=VMEM_LIMIT_BYTES)


def _position():
    return lax.axis_index("x"), lax.axis_index("y"), lax.axis_index("c")


class _Copies:
    def __init__(self):
        self.local, self.sends, self.recvs = [], [], []

    def extend(self, other):
        self.local += other.local
        self.sends += other.sends
        self.recvs += other.recvs

    def start(self):
        for cp in self.local + self.sends:
            cp.start()

    def wait(self):
        for land, send_sems, recv_sems, k, peer in self.recvs:
            _remote(land, land, send_sems, recv_sems, k, peer).wait_recv()
        for cp in self.sends:
            cp.wait_send()
        for cp in self.local:
            cp.wait()


def _remote(src, dst, send_sems, recv_sems, k, to):
    return pltpu.make_async_remote_copy(src_ref=src, dst_ref=dst, send_sem=send_sems.at[k], recv_sem=recv_sems.at[k],
                                        device_id=to, device_id_type=MESH)


class _Payload:
    ins, out_shapes, aliases, n_remote, n_local = (), (), {}, 0, 0

    def sem_shapes(self):
        return [pltpu.SemaphoreType.DMA((max(self.n_remote, 1),)), pltpu.SemaphoreType.DMA((max(self.n_remote, 1),)),
                pltpu.SemaphoreType.DMA((max(self.n_local, 1),))]


class _GatherIci(_Payload):
    def __init__(self, shards):
        self.ins = list(shards)
        self.out_shapes = [jax.ShapeDtypeStruct((s.shape[0], N_DEV * s.shape[1], s.shape[2]), s.dtype) for s in shards]
        self.n_remote, self.n_local = 4 * len(shards), len(shards)

    def build(self, ins, outs, send_sems, recv_sems, local_sems, k0=0, l0=0):
        x, y, c = _position()
        peers = [(x, y, 1 - c), (1 - x, y, c), (x, 1 - y, c), (1 - x, 1 - y, c)]
        cps = _Copies()
        for a, (src, out) in enumerate(zip(ins, outs)):
            r = src.shape[1]

            def rows(px, py, pc, out=out, r=r):
                return out.at[:, pl.ds((4 * px + 2 * py + pc) * r, r), :]

            cps.local.append(pltpu.make_async_copy(src, rows(x, y, c), local_sems.at[l0 + a]))
            for k, peer in enumerate(peers):
                cps.sends.append(_remote(src, rows(x, y, c), send_sems, recv_sems, k0 + 4 * a + k, peer))
                cps.recvs.append((rows(*peer), send_sems, recv_sems, k0 + 4 * a + k, peer))
        return cps


class _GatherForward(_Payload):
    def __init__(self, partials):
        self.ins = list(partials)
        self.out_shapes = [jax.ShapeDtypeStruct(p.shape, p.dtype) for p in partials]
        self.aliases = {a: a for a in range(len(partials))}
        self.n_remote = 3 * len(partials)

    def build(self, ins, outs, send_sems, recv_sems, local_sems, k0=0, l0=0):
        x, y, c = _position()
        chips = [(1 - x, y), (x, 1 - y), (1 - x, 1 - y)]
        cps = _Copies()
        for a, out in enumerate(outs):
            r = out.shape[1] // N_DEV
            for k, (px, py) in enumerate(chips):
                mine = out.at[:, pl.ds((4 * px + 2 * py + c) * r, r), :]
                theirs = out.at[:, pl.ds((4 * px + 2 * py + 1 - c) * r, r), :]
                cps.sends.append(_remote(mine, mine, send_sems, recv_sems, k0 + 3 * a + k, (x, y, 1 - c)))
                cps.recvs.append((theirs, send_sems, recv_sems, k0 + 3 * a + k, (x, y, 1 - c)))
        return cps


class _PairExchange(_Payload):
    def __init__(self, grads):
        self.ins = list(grads)
        self.out_shapes = [jax.ShapeDtypeStruct((g.shape[0] // 2, g.shape[1]), g.dtype) for g in grads]
        self.n_remote = N_CHIP * len(grads)

    def build(self, ins, outs, send_sems, recv_sems, local_sems, k0=0, l0=0):
        x, y, c = _position()
        cps = _Copies()
        for a, (src, out) in enumerate(zip(ins, outs)):
            r = src.shape[0] // N_DEV
            for q in range(N_CHIP):
                land = out.at[pl.ds(q * r, r), :]
                cps.sends.append(_remote(src.at[pl.ds((2 * q + 1 - c) * r, r), :], land, send_sems, recv_sems,
                                         k0 + N_CHIP * a + q, (x, y, 1 - c)))
                cps.recvs.append((land, send_sems, recv_sems, k0 + N_CHIP * a + q, (x, y, 1 - c)))
        return cps


class _ChipExchange(_Payload):
    def __init__(self, sums):
        self.ins = list(sums)
        self.out_shapes = [jax.ShapeDtypeStruct(s.shape, s.dtype) for s in sums]
        self.n_remote, self.n_local = 3 * len(sums), len(sums)

    def build(self, ins, outs, send_sems, recv_sems, local_sems, k0=0, l0=0):
        x, y, c = _position()
        my_chip = 2 * x + y
        chips = [(1 - x, y), (x, 1 - y), (1 - x, 1 - y)]
        cps = _Copies()
        for a, (src, out) in enumerate(zip(ins, outs)):
            r = src.shape[0] // N_CHIP
            mine = out.at[pl.ds(my_chip * r, r), :]
            cps.local.append(pltpu.make_async_copy(src.at[pl.ds(my_chip * r, r), :], mine, local_sems.at[l0 + a]))
            for k, (px, py) in enumerate(chips):
                land = out.at[pl.ds((2 * px + py) * r, r), :]
                cps.sends.append(_remote(src.at[pl.ds((2 * px + py) * r, r), :], mine, send_sems, recv_sems,
                                         k0 + 3 * a + k, (px, py, c)))
                cps.recvs.append((land, send_sems, recv_sems, k0 + 3 * a + k, (px, py, c)))
        return cps


class _Merged(_Payload):
    def __init__(self, parts):
        self.parts = list(parts)
        self.ins = [a for p in parts for a in p.ins]
        self.out_shapes = [s for p in parts for s in p.out_shapes]
        self.aliases, self.offsets = {}, []
        i0 = o0 = k0 = l0 = 0
        for p in parts:
            self.offsets.append((i0, o0, k0, l0))
            self.aliases.update({i0 + i: o0 + o for i, o in p.aliases.items()})
            i0, o0, k0, l0 = i0 + len(p.ins), o0 + len(p.out_shapes), k0 + p.n_remote, l0 + p.n_local
        self.n_remote, self.n_local = k0, l0

    def build(self, ins, outs, send_sems, recv_sems, local_sems):
        cps = _Copies()
        for p, (i0, o0, k0, l0) in zip(self.parts, self.offsets):
            cps.extend(p.build(ins[i0:i0 + len(p.ins)], outs[o0:o0 + len(p.out_shapes)], send_sems, recv_sems,
                               local_sems, k0, l0))
        return cps


def _call(body, name, grid, in_specs, out_specs, out_shape, scratch_shapes, semantics, args, payload=None):
    if payload is None:
        outs = pl.pallas_call(body, name=name, grid=grid, in_specs=in_specs, out_specs=out_specs,
                              out_shape=out_shape, scratch_shapes=scratch_shapes,
                              compiler_params=_cparams(semantics))(*args)
        return list(outs), []
    n_in, n_out, n_scr = len(in_specs), len(out_specs), len(scratch_shapes)
    p_in, p_out = len(payload.ins), len(payload.out_shapes)

    def carried(*refs):
        ins, p_ins = refs[:n_in], refs[n_in:n_in + p_in]
        o0 = n_in + p_in
        outs, p_outs = refs[o0:o0 + n_out], refs[o0 + n_out:o0 + n_out + p_out]
        s0 = o0 + n_out + p_out
        scr, sems = refs[s0:s0 + n_scr], refs[s0 + n_scr:]
        ids = [pl.program_id(k) for k in range(len(grid))]
        at_first = functools.reduce(jnp.logical_and, [i == 0 for i in ids])
        at_last = functools.reduce(jnp.logical_and, [i == g - 1 for i, g in zip(ids, grid)])

        @pl.when(at_first)
        def _():
            payload.build(p_ins, p_outs, *sems).start()

        body(*ins, *outs, *scr)

        @pl.when(at_last)
        def _():
            payload.build(p_ins, p_outs, *sems).wait()

    outs = pl.pallas_call(
        carried, name=name, grid=grid, in_specs=list(in_specs) + [ANY] * p_in,
        out_specs=list(out_specs) + [ANY] * p_out, out_shape=list(out_shape) + list(payload.out_shapes),
        scratch_shapes=list(scratch_shapes) + payload.sem_shapes(),
        input_output_aliases={n_in + i: n_out + o for i, o in payload.aliases.items()},
        compiler_params=_cparams(("arbitrary",) * len(grid)))(*args, *payload.ins)
    return list(outs[:n_out]), list(outs[n_out:])


def _comm(payload, name):
    def body(*refs):
        p_in, p_out = len(payload.ins), len(payload.out_shapes)
        cps = payload.build(refs[:p_in], refs[p_in:p_in + p_out], *refs[p_in + p_out:])
        cps.start()
        cps.wait()

    return list(pl.pallas_call(
        body, name=name, in_specs=[ANY] * len(payload.ins), out_specs=[ANY] * len(payload.out_shapes),
        out_shape=list(payload.out_shapes), scratch_shapes=payload.sem_shapes(),
        input_output_aliases=dict(payload.aliases))(*payload.ins))


def _dot(a, b):
    return jnp.dot(a, b, preferred_element_type=F32)


def _dot_nt(a, b):
    return lax.dot_general(a, b, (((1,), (1,)), ((), ())), preferred_element_type=F32)


def _dot_tn(a, b):
    return lax.dot_general(a, b, (((0,), (0,)), ((), ())), preferred_element_type=F32)


def _split_dot(x, e):
    hi = x.astype(BF16)
    r1 = x - hi.astype(F32)
    mid = r1.astype(BF16)
    lo = (r1 - mid.astype(F32)).astype(BF16)
    return _dot(hi, e) + _dot(mid, e) + _dot(lo, e)


def _rms(x):
    rstd = lax.rsqrt(jnp.mean(x * x, axis=-1, keepdims=True) + EPS)
    return x * rstd, rstd


def _rms_bwd(xhat, rstd, g, dh):
    dxhat = dh * g
    dx = rstd * (dxhat - xhat * jnp.mean(dxhat * xhat, axis=-1, keepdims=True))
    return dx, jnp.sum(dh * xhat, axis=0, keepdims=True)


def _ln(v):
    mu = jnp.mean(v, axis=-1, keepdims=True)
    xc = v - mu
    rstd = lax.rsqrt(jnp.mean(xc * xc, axis=-1, keepdims=True) + EPS)
    return xc * rstd, rstd


def _ln_bwd(vhat, rstd, g, dy):
    dvhat = dy * g
    dv = rstd * (dvhat - jnp.mean(dvhat, axis=-1, keepdims=True)
                 - vhat * jnp.mean(dvhat * vhat, axis=-1, keepdims=True))
    return dv, jnp.sum(dy * vhat, axis=0, keepdims=True), jnp.sum(dy, axis=0, keepdims=True)


_INV_SQRT2 = 0.7071067811865476
_INV_SQRT2PI = 0.3989422804014327


def _gelu(x):
    return 0.5 * x * (1.0 + lax.erf(x * _INV_SQRT2))


def _gelu_grad(x):
    return 0.5 * (1.0 + lax.erf(x * _INV_SQRT2)) + x * jnp.exp(-0.5 * x * x) * _INV_SQRT2PI


def _silu_grad(x):
    s = jax.nn.sigmoid(x)
    return s * (1.0 + x * (1.0 - s))


def _ffn_fwd(x, g, wa, mi, name, payload=None):
    t, d = x.shape
    f = wa.shape[1]
    tm, tf = 1024, 256
    nc = f // tf
    groups = [slice(k * (tm // 2), (k + 1) * (tm // 2)) for k in range(2)]

    def body(x_ref, g_ref, wgu_ref, wd_ref, xo_ref, gate_ref, up_ref, act_ref, h_scr, acc_scr):
        c = pl.program_id(1)

        @pl.when(c == 0)
        def _():
            xhat, _ = _rms(x_ref[...])
            h_scr[...] = (xhat * g_ref[...]).astype(BF16)
            acc_scr[...] = jnp.zeros_like(acc_scr)

        wgu, wd = wgu_ref[...].reshape(2 * tf, d), wd_ref[...]
        for rows in groups:
            gu = _dot_nt(h_scr[rows, :], wgu)
            gate, up = gu[:, :tf], gu[:, tf:]
            gate_ref[rows, :] = gate.astype(BF16)
            up_ref[rows, :] = up.astype(BF16)
            act = (gate * jax.nn.sigmoid(gate) * up).astype(BF16)
            act_ref[rows, :] = act
            acc_scr[rows, :] += _dot(act, wd)

        @pl.when(c == nc - 1)
        def _():
            xo_ref[...] = x_ref[...] + 0.5 * acc_scr[...]

    assert mi % 2 == 0
    return _call(
        body, name, (t // tm, nc),
        [pl.BlockSpec((tm, d), lambda i, c: (i, 0)), pl.BlockSpec((1, d), lambda i, c: (0, 0)),
         pl.BlockSpec((2, tf, d), lambda i, c: (mi // 2, c, 0)),
         pl.BlockSpec((None, tf, d), lambda i, c: (mi + 2, c, 0))],
        [pl.BlockSpec((tm, d), lambda i, c: (i, 0))] + [pl.BlockSpec((tm, tf), lambda i, c: (i, c))] * 3,
        [jax.ShapeDtypeStruct((t, d), F32)] + [jax.ShapeDtypeStruct((t, f), BF16)] * 3,
        [pltpu.VMEM((tm, d), BF16), pltpu.VMEM((tm, d), F32)],
        ("parallel", "arbitrary"), (x, g, wa, wa), payload)


def _ffn_bwd(x, g, dxo, gate, up, wa, mi, name, payload=None):
    t, d = x.shape
    f = wa.shape[1]
    tm, tf = 1024, 256
    nc = f // tf
    groups = [slice(k * (tm // 2), (k + 1) * (tm // 2)) for k in range(2)]

    def body(x_ref, g_ref, dxo_ref, gate_ref, up_ref, wgu_ref, wd_ref,
             dx_ref, dgate_ref, dup_ref, h_ref, dy_ref, dg_ref, acc_scr):
        i, c = pl.program_id(0), pl.program_id(1)

        @pl.when(c == 0)
        def _():
            xhat, _ = _rms(x_ref[...])
            h_ref[...] = (xhat * g_ref[...]).astype(BF16)
            dy_ref[...] = (0.5 * dxo_ref[...]).astype(BF16)
            acc_scr[...] = jnp.zeros_like(acc_scr)

        @pl.when((c == 0) & (i == 0))
        def _():
            dg_ref[...] = jnp.zeros_like(dg_ref)

        wg, wu, wd = wgu_ref[0], wgu_ref[1], wd_ref[...]
        for rows in groups:
            gt = gate_ref[rows, :].astype(F32)
            u = up_ref[rows, :].astype(F32)
            s = jax.nn.sigmoid(gt)
            silu = gt * s
            dact = _dot_nt(dy_ref[rows, :], wd)
            dgate = (dact * u * (s * (1.0 + gt * (1.0 - s)))).astype(BF16)
            dup = (dact * silu).astype(BF16)
            dgate_ref[rows, :] = dgate
            dup_ref[rows, :] = dup
            acc_scr[rows, :] += _dot(dgate, wg) + _dot(dup, wu)

        @pl.when(c == nc - 1)
        def _():
            xhat, rstd = _rms(x_ref[...])
            dxn, dg = _rms_bwd(xhat, rstd, g_ref[...], acc_scr[...])
            dx_ref[...] = dxo_ref[...] + dxn
            dg_ref[...] += dg

    assert mi % 2 == 0
    row = pl.BlockSpec((tm, d), lambda i, c: (i, 0))
    col = pl.BlockSpec((tm, tf), lambda i, c: (i, c))
    vec = pl.BlockSpec((1, d), lambda i, c: (0, 0))
    return _call(
        body, name, (t // tm, nc),
        [row, vec, row, col, col, pl.BlockSpec((2, tf, d), lambda i, c: (mi // 2, c, 0)),
         pl.BlockSpec((None, tf, d), lambda i, c: (mi + 2, c, 0))],
        [row, col, col, row, row, vec],
        [jax.ShapeDtypeStruct((t, d), F32), jax.ShapeDtypeStruct((t, f), BF16),
         jax.ShapeDtypeStruct((t, f), BF16),
         jax.ShapeDtypeStruct((t, d), BF16), jax.ShapeDtypeStruct((t, d), BF16),
         jax.ShapeDtypeStruct((1, d), F32)],
        [pltpu.VMEM((tm, d), F32)],
        ("arbitrary", "arbitrary"), (x, g, dxo, gate, up, wa, wa), payload)


def _ffn_dw(dgate, dup, act, h, dy, name, payload=None):
    t, f = dgate.shape
    d = h.shape[1]
    tk = 512
    tmm = f // 2
    nk = t // tk

    def body(dg_ref, du_ref, a_ref, h_ref, dy_ref, og_ref, ou_ref, od_ref, acc_g, acc_u, acc_d):
        k = pl.program_id(1)

        @pl.when(k == 0)
        def _():
            acc_g[...] = jnp.zeros_like(acc_g)
            acc_u[...] = jnp.zeros_like(acc_u)
            acc_d[...] = jnp.zeros_like(acc_d)

        hv = h_ref[...]
        acc_g[...] += _dot_tn(dg_ref[...], hv)
        acc_u[...] += _dot_tn(du_ref[...], hv)
        acc_d[...] += _dot_tn(a_ref[...], dy_ref[...])

        @pl.when(k == nk - 1)
        def _():
            og_ref[...] = acc_g[...].astype(BF16)
            ou_ref[...] = acc_u[...].astype(BF16)
            od_ref[...] = acc_d[...].astype(BF16)

    col = pl.BlockSpec((tk, tmm), lambda j, k: (k, j))
    row = pl.BlockSpec((tk, d), lambda j, k: (k, 0))
    out = pl.BlockSpec((tmm, d), lambda j, k: (j, 0))
    return _call(
        body, name, (f // tmm, nk), [col, col, col, row, row], [out, out, out],
        [jax.ShapeDtypeStruct((f, d), BF16)] * 3, [pltpu.VMEM((tmm, d), F32)] * 3,
        ("parallel", "arbitrary"), (dgate, dup, act, h, dy), payload)


def _tn_matmul(a, b, name, payload=None):
    t, m = a.shape
    n = b.shape[1]
    tk = 1024
    tmm = m // 2 if (m // 2) % LANES == 0 else m
    nk = t // tk

    def body(a_ref, b_ref, o_ref, acc_scr):
        k = pl.program_id(1)

        @pl.when(k == 0)
        def _():
            acc_scr[...] = jnp.zeros_like(acc_scr)

        acc_scr[...] += _dot_tn(a_ref[...].astype(BF16), b_ref[...].astype(BF16))

        @pl.when(k == nk - 1)
        def _():
            o_ref[...] = acc_scr[...].astype(BF16)

    (out,), p_outs = _call(
        body, name, (m // tmm, nk),
        [pl.BlockSpec((tk, tmm), lambda j, k: (k, j)), pl.BlockSpec((tk, n), lambda j, k: (k, 0))],
        [pl.BlockSpec((tmm, n), lambda j, k: (j, 0))],
        [jax.ShapeDtypeStruct((m, n), BF16)],
        [pltpu.VMEM((tmm, n), F32)],
        ("parallel", "arbitrary"), (a, b), payload)
    return out, p_outs


def _proj_in_bwd(x, g, dxo, dz, wb, li, name):
    t, d = x.shape
    n = wb.shape[1]
    tm = 512

    def body(x_ref, g_ref, dxo_ref, dz_ref, w_ref, dx_ref, h_ref, dg_ref):
        i = pl.program_id(0)

        @pl.when(i == 0)
        def _():
            dg_ref[...] = jnp.zeros_like(dg_ref)

        xhat, rstd = _rms(x_ref[...])
        h_ref[...] = (xhat * g_ref[...]).astype(BF16)
        dh = _dot(dz_ref[...], w_ref[...])
        dxn, dg = _rms_bwd(xhat, rstd, g_ref[...], dh)
        dx_ref[...] = dxo_ref[...] + dxn
        dg_ref[...] += dg

    row = pl.BlockSpec((tm, d), lambda i: (i, 0))
    vec = pl.BlockSpec((1, d), lambda i: (0, 0))
    return pl.pallas_call(
        body, name=name, grid=(t // tm,),
        in_specs=[row, vec, row, pl.BlockSpec((tm, n), lambda i: (i, 0)),
                  pl.BlockSpec((None, n, d), lambda i: (li, 0, 0))],
        out_specs=[row, row, vec],
        out_shape=[jax.ShapeDtypeStruct((t, d), F32), jax.ShapeDtypeStruct((t, d), BF16),
                   jax.ShapeDtypeStruct((1, d), F32)],
        compiler_params=_cparams(("arbitrary",)),
    )(x, g, dxo, dz, wb)


def _lane_ids(shape):
    return lax.broadcasted_iota(jnp.int32, shape, 1)


def _tril(w):
    r = lax.broadcasted_iota(jnp.int32, w.shape, 0)
    c = lax.broadcasted_iota(jnp.int32, w.shape, 1)
    return jnp.where(r >= c, w, 0.0)


def _shift_down(x, k):
    return x if k == 0 else pltpu.roll(x, k, 0)


def _shift_up(x, k):
    return x if k == 0 else pltpu.roll(x, x.shape[0] - k, 0)


def _sub_tile_shifts(ext, shift):
    return [shift(ext, b) for b in range(8)]


def _tap(shifted, j, n_out, down):
    a, b = divmod(j, 8)
    r0 = HALO - 8 * a if down else 8 * a
    return shifted[b][r0:r0 + n_out]


def _depthwise(shifted, w, n_out, down):
    acc = None
    for j in range(CONV_WIDTH):
        term = _tap(shifted, j, n_out, down) * w[CONV_WIDTH - 1 - j:CONV_WIDTH - j]
        acc = term if acc is None else acc + term
    return acc


def _conv_wgrad(shifted, dhc, n_out):
    return [jnp.sum(_tap(shifted, CONV_WIDTH - 1 - k, n_out, True) * dhc, axis=0, keepdims=True)
            for k in range(CONV_WIDTH)]


def _window_sums(ext, shift):
    s2 = ext + shift(ext, 1)
    s4 = s2 + shift(s2, 2)
    s8 = s4 + shift(s4, 4)
    s16 = s8 + shift(s8, 8)
    grp = _lane_ids(ext.shape) // HEAD_DIM
    return jnp.where(grp == 0, s2, jnp.where(grp == 1, s4, jnp.where(grp == 2, s8, s16)))


def _pool_count(t0, n, width):
    pos = (lax.broadcasted_iota(jnp.int32, (n, width), 0) + (t0 + 1)).astype(F32)
    grp = _lane_ids((n, width)) // HEAD_DIM
    win = jnp.where(grp == 0, 2.0, jnp.where(grp == 1, 4.0, jnp.where(grp == 2, 8.0, 16.0)))
    return jnp.minimum(pos, win)


def _block_diag(pw):
    gn, cg, _ = pw.shape
    rows = []
    for gi in range(gn):
        parts = [pw[gi] if gj == gi else jnp.zeros((cg, cg), pw.dtype) for gj in range(gn)]
        rows.append(jnp.concatenate(parts, axis=1))
    return jnp.concatenate(rows, axis=0)


def _head_pair_mix(w_even, w_odd, v):
    lo = _lane_ids((CHUNK, LANES)) < HEAD_DIM
    return jnp.where(lo, _dot(w_even, v), _dot(w_odd, v))


def _mixer_fwd(x, g, wb, wc, p, name, payload=None):
    t, d = x.shape
    d_in = wb.shape[1]
    sgu = p["sgu_ln_g"].shape[1]
    pool = p["pool_scale"].shape[1]
    d_mix = 2 * sgu + pool
    tm = 512
    n_i = t // tm
    hb = tm // HALO

    def body(x_ref, xp_ref, g_ref, wi_ref, wo_ref, lng_ref, lnb_ref, ws_ref, bs_ref, cw_ref, cb_ref, clg_ref,
             clb_ref, bd_ref, ps_ref, xo_ref, z_ref, hc_ref, cat_ref):
        i = pl.program_id(0)
        first = i == 0
        gain, wi = g_ref[...], wi_ref[...]

        def project(xv):
            xhat, _ = _rms(xv)
            return _dot_nt((xhat * gain).astype(BF16), wi)

        z_main = project(x_ref[...])
        z_ref[...] = z_main
        z_prev = jnp.where(first, 0.0, project(xp_ref[...]))

        lng, lnb = lng_ref[...], lnb_ref[...]
        wt = [_tril(ws_ref[h]).astype(BF16) for h in range(sgu // HEAD_DIM)]
        for n in range(tm // CHUNK):
            rows = slice(n * CHUNK, (n + 1) * CHUNK)
            u = _gelu(z_main[rows, 0:sgu])
            vhat, _ = _ln(_gelu(z_main[rows, sgu:2 * sgu]))
            vn = (vhat * lng + lnb).astype(BF16)
            for gp in range(sgu // LANES):
                ls = slice(gp * LANES, (gp + 1) * LANES)
                mixed = _head_pair_mix(wt[2 * gp], wt[2 * gp + 1], vn[:, ls]) + bs_ref[:, ls]
                cat_ref[rows, ls] = (u[:, ls] * mixed).astype(BF16)

        def glu(zz):
            return zz[:, 2 * sgu:3 * sgu] * jax.nn.sigmoid(zz[:, 3 * sgu:4 * sgu])

        ext = jnp.concatenate([glu(z_prev), glu(z_main)], axis=0)
        hc = _depthwise(_sub_tile_shifts(ext, _shift_down), cw_ref[...], tm, True) + cb_ref[...]
        hc_ref[...] = hc
        hhat, _ = _ln(hc)
        bn = hhat * clg_ref[...] + clb_ref[...]
        cat_ref[:, sgu:2 * sgu] = (bn * jax.nn.sigmoid(bn)).astype(BF16)

        pext = jnp.concatenate([z_prev[:, 4 * sgu:], z_main[:, 4 * sgu:]], axis=0)
        sums = _window_sums(pext, _shift_down)[HALO:]
        pooled = sums / _pool_count(i * tm, tm, pool) - z_main[:, 4 * sgu:]
        mixed_c = _dot(pooled.astype(BF16), bd_ref[...].astype(BF16))
        cat_ref[:, 2 * sgu:] = (mixed_c * ps_ref[...]).astype(BF16)

        xo_ref[...] = x_ref[...] + _dot(cat_ref[...], wo_ref[...])

    def vec(n):
        return pl.BlockSpec((1, n), lambda i: (0, 0))

    return _call(
        body, name, (n_i,),
        [pl.BlockSpec((tm, d), lambda i: (i, 0)),
         pl.BlockSpec((HALO, d), lambda i: (jnp.maximum(i * hb - 1, 0), 0)),
         vec(d),
         pl.BlockSpec((None, d_in, d), lambda i: (0, 0, 0)), pl.BlockSpec((None, d_mix, d), lambda i: (0, 0, 0)),
         vec(sgu), vec(sgu),
         pl.BlockSpec(p["w_spatial"].shape, lambda i: (0, 0, 0)),
         pl.BlockSpec((CHUNK, sgu), lambda i: (0, 0)),
         pl.BlockSpec((CONV_WIDTH, sgu), lambda i: (0, 0)),
         vec(sgu), vec(sgu), vec(sgu),
         pl.BlockSpec((pool, pool), lambda i: (0, 0)), vec(pool)],
        [pl.BlockSpec((tm, d), lambda i: (i, 0)), pl.BlockSpec((tm, d_in), lambda i: (i, 0)),
         pl.BlockSpec((tm, sgu), lambda i: (i, 0)), pl.BlockSpec((tm, d_mix), lambda i: (i, 0))],
        [jax.ShapeDtypeStruct((t, d), F32), jax.ShapeDtypeStruct((t, d_in), F32),
         jax.ShapeDtypeStruct((t, sgu), F32), jax.ShapeDtypeStruct((t, d_mix), BF16)], [], ("parallel",),
        (x, x, g, wb, wc, p["sgu_ln_g"], p["sgu_ln_b"], p["w_spatial"], p["bs_full"], p["conv_w"], p["conv_b"],
         p["conv_ln_g"], p["conv_ln_b"], p["bd"], p["pool_scale"]), payload)


_R_SGU_G, _R_SGU_B, _R_CONV_B, _R_CLN_G, _R_CLN_B, _R_CONV_W = 0, 1, 2, 3, 4, 8
_R384_ROWS = 40


def _mix_bwd(z, hc_saved, dxo, wc, li, p, name, payload=None):
    t, d_in = z.shape
    sgu = p["sgu_ln_g"].shape[1]
    pool = p["pool_scale"].shape[1]
    d_mix = 2 * sgu + pool
    n_head = sgu // HEAD_DIM
    tm = 512
    n_i = t // tm
    hb = tm // HALO

    def body(z_ref, zp_ref, zn_ref, dx_ref, dxn_ref, wo_ref, hc_ref, hcn_ref, lng_ref, lnb_ref, ws_ref, bs_ref,
             cw_ref, clg_ref, clb_ref, bd_ref, ps_ref, dz_ref, g384_ref, gws_ref, gpool_ref, dbs_scr):
        i = pl.program_id(0)
        first, last = i == 0, i == n_i - 1

        @pl.when(first)
        def _():
            g384_ref[...] = jnp.zeros_like(g384_ref)
            gws_ref[...] = jnp.zeros_like(gws_ref)
            gpool_ref[...] = jnp.zeros_like(gpool_ref)
            dbs_scr[...] = jnp.zeros_like(dbs_scr)

        z_main = z_ref[...]
        z_prev = jnp.where(first, 0.0, zp_ref[...])
        z_next = jnp.where(last, 0.0, zn_ref[...])
        wo = wo_ref[...]
        dc_main = _dot_nt(dx_ref[...].astype(BF16), wo)
        dc_next = jnp.where(last, 0.0, _dot_nt(dxn_ref[...].astype(BF16), wo))

        lng, lnb = lng_ref[...], lnb_ref[...]
        wt = [_tril(ws_ref[h]) for h in range(n_head)]
        wt_b = [w.astype(BF16) for w in wt]
        wtt_b = [w.T.astype(BF16) for w in wt]
        lo = _lane_ids((CHUNK, LANES)) < HEAD_DIM
        d_lng = jnp.zeros((1, sgu), F32)
        d_lnb = jnp.zeros((1, sgu), F32)
        dws = [jnp.zeros((CHUNK, CHUNK), F32) for _ in range(n_head)]
        for n in range(tm // CHUNK):
            rows = slice(n * CHUNK, (n + 1) * CHUNK)
            au, av = z_main[rows, 0:sgu], z_main[rows, sgu:2 * sgu]
            u = _gelu(au)
            vhat, vrstd = _ln(_gelu(av))
            vn = (vhat * lng + lnb).astype(BF16)
            da = dc_main[rows, 0:sgu]
            dmixed = da * u
            dbs_scr[...] += dmixed
            dvn_parts, du_parts = [], []
            for gp in range(sgu // LANES):
                ls = slice(gp * LANES, (gp + 1) * LANES)
                vn_g = vn[:, ls]
                mixed = _head_pair_mix(wt_b[2 * gp], wt_b[2 * gp + 1], vn_g) + bs_ref[:, ls]
                du_parts.append(da[:, ls] * mixed)
                dm_g = dmixed[:, ls]
                dm_b = dm_g.astype(BF16)
                dvn_parts.append(jnp.where(lo, _dot(wtt_b[2 * gp], dm_b), _dot(wtt_b[2 * gp + 1], dm_b)))
                dws[2 * gp] = dws[2 * gp] + _dot_nt(jnp.where(lo, dm_g, 0.0).astype(BF16), vn_g)
                dws[2 * gp + 1] = dws[2 * gp + 1] + _dot_nt(jnp.where(lo, 0.0, dm_g).astype(BF16), vn_g)
            dvn = jnp.concatenate(dvn_parts, axis=1)
            du = jnp.concatenate(du_parts, axis=1)
            dv, dg_n, db_n = _ln_bwd(vhat, vrstd, lng, dvn)
            d_lng = d_lng + dg_n
            d_lnb = d_lnb + db_n
            dz_ref[rows, 0:sgu] = (du * _gelu_grad(au)).astype(BF16)
            dz_ref[rows, sgu:2 * sgu] = (dv * _gelu_grad(av)).astype(BF16)
        for h in range(n_head):
            gws_ref[h] += _tril(dws[h])
        g384_ref[_R_SGU_G:_R_SGU_G + 1, :] += d_lng
        g384_ref[_R_SGU_B:_R_SGU_B + 1, :] += d_lnb

        clg = clg_ref[...]
        bcols = slice(2 * sgu, 4 * sgu)
        zb = jnp.concatenate([z_prev[:, bcols], z_main[:, bcols], z_next[:, bcols]], axis=0)
        bval, bgate = zb[:, 0:sgu], zb[:, sgu:2 * sgu]
        sg = jax.nn.sigmoid(bgate)
        hglu = bval * sg
        n_out = tm + HALO
        hglu_shifts = _sub_tile_shifts(hglu, _shift_down)
        cw = cw_ref[...]
        hc = jnp.concatenate([hc_ref[...], jnp.where(last, 0.0, hcn_ref[...])], axis=0)
        hhat, hrstd = _ln(hc)
        bn = hhat * clg + clb_ref[...]
        db = jnp.concatenate([dc_main[:, sgu:2 * sgu], dc_next[:, sgu:2 * sgu]], axis=0)
        dbn = db * _silu_grad(bn)
        dhc_all, _, _ = _ln_bwd(hhat, hrstd, clg, dbn)
        dbn_m, hhat_m, dhc = dbn[:tm], hhat[:tm], dhc_all[:tm]
        g384_ref[_R_CLN_G:_R_CLN_G + 1, :] += jnp.sum(dbn_m * hhat_m, axis=0, keepdims=True)
        g384_ref[_R_CLN_B:_R_CLN_B + 1, :] += jnp.sum(dbn_m, axis=0, keepdims=True)
        g384_ref[_R_CONV_B:_R_CONV_B + 1, :] += jnp.sum(dhc, axis=0, keepdims=True)
        wrows = _conv_wgrad(hglu_shifts, dhc, tm)
        for k in range(CONV_WIDTH):
            g384_ref[_R_CONV_W + k:_R_CONV_W + k + 1, :] += wrows[k]
        dhglu = _depthwise(_sub_tile_shifts(dhc_all, _shift_up), cw, tm, False)
        bval_m, sg_m = bval[HALO:HALO + tm], sg[HALO:HALO + tm]
        dz_ref[:, 2 * sgu:3 * sgu] = (dhglu * sg_m).astype(BF16)
        dz_ref[:, 3 * sgu:4 * sgu] = (dhglu * bval_m * sg_m * (1.0 - sg_m)).astype(BF16)

        bd_b = bd_ref[...].astype(BF16)
        ps = ps_ref[...]
        p_main = z_main[:, 4 * sgu:]
        pext = jnp.concatenate([z_prev[:, 4 * sgu:], p_main], axis=0)
        cnt = _pool_count(i * tm, n_out, pool)
        pooled = _window_sums(pext, _shift_down)[HALO:] / cnt[:tm] - p_main
        pooled_b = pooled.astype(BF16)
        dcc = jnp.concatenate([dc_main[:, 2 * sgu:], dc_next[:, 2 * sgu:]], axis=0)
        dmix_c = dcc * ps
        mixed_c = _dot(pooled_b, bd_b)
        grp_r = lax.broadcasted_iota(jnp.int32, (pool, pool), 0) // HEAD_DIM
        grp_c = lax.broadcasted_iota(jnp.int32, (pool, pool), 1) // HEAD_DIM
        gpool_ref[0:pool, :] += jnp.where(grp_r == grp_c, _dot_tn(pooled_b, dmix_c[:tm].astype(BF16)), 0.0)
        gpool_ref[pool:pool + 1, :] += jnp.sum(dcc[:tm] * mixed_c, axis=0, keepdims=True)
        dpooled = _dot_nt(dmix_c.astype(BF16), bd_b)
        q = dpooled / cnt
        dp = _window_sums(q, _shift_up)[:tm] - dpooled[:tm]
        dz_ref[:, 4 * sgu:] = dp.astype(BF16)

        @pl.when(last)
        def _():
            r = lax.broadcasted_iota(jnp.int32, (sgu, LANES), 0)
            c = lax.broadcasted_iota(jnp.int32, (sgu, LANES), 1)
            sel = (r // HEAD_DIM == c).astype(BF16)
            gws_ref[n_head] = _split_dot(dbs_scr[...], sel)

    def vec(n):
        return pl.BlockSpec((1, n), lambda i: (0, 0))

    def prev_map(i):
        return (jnp.maximum(i * hb - 1, 0), 0)

    def next_map(i):
        return (jnp.minimum((i + 1) * hb, n_i * hb - 1), 0)

    return _call(
        body, name, (n_i,),
        [pl.BlockSpec((tm, d_in), lambda i: (i, 0)),
         pl.BlockSpec((HALO, d_in), prev_map), pl.BlockSpec((HALO, d_in), next_map),
         pl.BlockSpec((tm, d_mix), lambda i: (i, 0)), pl.BlockSpec((HALO, d_mix), next_map),
         pl.BlockSpec((None, d_mix, dxo.shape[1]), lambda i: (li, 0, 0)),
         pl.BlockSpec((tm, sgu), lambda i: (i, 0)), pl.BlockSpec((HALO, sgu), next_map),
         vec(sgu), vec(sgu),
         pl.BlockSpec(p["w_spatial"].shape, lambda i: (0, 0, 0)),
         pl.BlockSpec((CHUNK, sgu), lambda i: (0, 0)),
         pl.BlockSpec((CONV_WIDTH, sgu), lambda i: (0, 0)),
         vec(sgu), vec(sgu),
         pl.BlockSpec((pool, pool), lambda i: (0, 0)), vec(pool)],
        [pl.BlockSpec((tm, d_in), lambda i: (i, 0)),
         pl.BlockSpec((_R384_ROWS, sgu), lambda i: (0, 0)),
         pl.BlockSpec((n_head + 1, CHUNK, CHUNK), lambda i: (0, 0, 0)),
         pl.BlockSpec((pool + 8, pool), lambda i: (0, 0))],
        [jax.ShapeDtypeStruct((t, d_in), BF16),
         jax.ShapeDtypeStruct((_R384_ROWS, sgu), F32),
         jax.ShapeDtypeStruct((n_head + 1, CHUNK, CHUNK), F32),
         jax.ShapeDtypeStruct((pool + 8, pool), F32)],
        [pltpu.VMEM((CHUNK, sgu), F32)], ("arbitrary",),
        (z, z, z, dxo, dxo, wc, hc_saved, hc_saved, p["sgu_ln_g"], p["sgu_ln_b"], p["w_spatial"], p["bs_full"],
         p["conv_w"], p["conv_ln_g"], p["conv_ln_b"], p["bd"], p["pool_scale"]), payload)


def _loss_head(x, g, target, name):
    t, d = x.shape
    tm = 512

    def body(x_ref, g_ref, tgt_ref, dx_ref, dg_ref, loss_ref):
        i = pl.program_id(0)

        @pl.when(i == 0)
        def _():
            dg_ref[...] = jnp.zeros_like(dg_ref)
            loss_ref[...] = jnp.zeros_like(loss_ref)

        gv = g_ref[...]
        xhat, rstd = _rms(x_ref[...])
        err = xhat * gv - tgt_ref[...]
        loss_ref[...] += jnp.zeros_like(loss_ref) + 0.5 * jnp.sum(jnp.mean(err * err, axis=-1, keepdims=True))
        dxn, dg = _rms_bwd(xhat, rstd, gv, err * (1.0 / d))
        dx_ref[...] = dxn
        dg_ref[...] += dg

    row = pl.BlockSpec((tm, d), lambda i: (i, 0))
    vec = pl.BlockSpec((1, d), lambda i: (0, 0))
    return pl.pallas_call(
        body, name=name, grid=(t // tm,),
        in_specs=[row, vec, row],
        out_specs=[row, vec, pl.BlockSpec((1, LANES), lambda i: (0, 0))],
        out_shape=[jax.ShapeDtypeStruct((t, d), F32), jax.ShapeDtypeStruct((1, d), F32),
                   jax.ShapeDtypeStruct((1, LANES), F32)],
        compiler_params=_cparams(("arbitrary",)),
    )(x, g, target)


def _all_gather(arrs, name, extra=None):
    gather = _GatherIci(arrs)
    n = len(arrs)
    forward = _GatherForward([jax.ShapeDtypeStruct(s.shape, s.dtype) for s in gather.out_shapes])
    x_in = len(extra.ins) if extra else 0
    x_out = len(extra.out_shapes) if extra else 0

    def body(*refs):
        ins, x_ins = refs[:n], refs[n:n + x_in]
        outs, x_outs = refs[n + x_in:2 * n + x_in], refs[2 * n + x_in:2 * n + x_in + x_out]
        sems = refs[2 * n + x_in + x_out:]
        first = gather.build(ins, outs, *sems[0:3])
        first.start()
        if extra:
            beside = extra.build(x_ins, x_outs, *sems[6:9])
            beside.start()
        first.wait()
        second = forward.build(outs, outs, *sems[3:6])
        second.start()
        second.wait()
        if extra:
            beside.wait()

    outs = pl.pallas_call(
        body, name=name,
        in_specs=[ANY] * (n + x_in), out_specs=[ANY] * (n + x_out),
        out_shape=list(gather.out_shapes) + (list(extra.out_shapes) if extra else []),
        scratch_shapes=gather.sem_shapes() + forward.sem_shapes() + (extra.sem_shapes() if extra else []),
    )(*arrs, *(extra.ins if extra else []))
    return list(outs[:n]), list(outs[n:])


def _all_gather_relayed(arrs, name):
    n = len(arrs)
    n_pairs = 8

    def body(*refs):
        ins, outs = refs[:n], refs[n:2 * n]
        send_sems, recv_sems, local_sems = refs[2 * n:]
        x, y, c = _position()
        sib, xn, yn = (x, y, 1 - c), (1 - x, y, c), (x, 1 - y, c)

        def rows(a, px, py, pc, half=None):
            r = ins[a].shape[1]
            base = (4 * px + 2 * py + pc) * r
            if half is None:
                return outs[a].at[:, pl.ds(base, r), :]
            return outs[a].at[:, pl.ds(base + half * (r // 2), r // 2), :]

        def send(a, k, src, dst, to):
            return _remote(src, dst, send_sems, recv_sems, a * n_pairs + k, to)

        def arrived(a, k, land, sender):
            _remote(land, land, send_sems, recv_sems, a * n_pairs + k, sender).wait_recv()

        own = [pltpu.make_async_copy(ins[a], rows(a, x, y, c), local_sems.at[a]) for a in range(n)]
        first = [send(a, k, ins[a], rows(a, x, y, c), to) for a in range(n) for k, to in enumerate((sib, xn, yn))]
        for cp in own + first:
            cp.start()
        for a in range(n):
            arrived(a, 1, rows(a, *xn), xn)
            arrived(a, 2, rows(a, *yn), yn)
        second = []
        for a in range(n):
            second += [send(a, 3, rows(a, *xn, half=0), rows(a, *xn, half=0), yn),
                       send(a, 4, rows(a, *yn, half=1), rows(a, *yn, half=1), xn),
                       send(a, 5, rows(a, *xn), rows(a, *xn), sib),
                       send(a, 6, rows(a, *yn), rows(a, *yn), sib)]
        for cp in second:
            cp.start()
        for a in range(n):
            arrived(a, 3, rows(a, 1 - x, 1 - y, c, half=0), yn)
            arrived(a, 4, rows(a, 1 - x, 1 - y, c, half=1), xn)
        third = [send(a, 7, rows(a, 1 - x, 1 - y, c), rows(a, 1 - x, 1 - y, c), sib) for a in range(n)]
        for cp in third:
            cp.start()
        for a in range(n):
            arrived(a, 0, rows(a, *sib), sib)
            arrived(a, 5, rows(a, 1 - x, y, 1 - c), sib)
            arrived(a, 6, rows(a, x, 1 - y, 1 - c), sib)
            arrived(a, 7, rows(a, 1 - x, 1 - y, 1 - c), sib)
        for cp in first + second + third:
            cp.wait_send()
        for cp in own:
            cp.wait()

    return list(pl.pallas_call(
        body, name=name, in_specs=[ANY] * n, out_specs=[ANY] * n,
        out_shape=[jax.ShapeDtypeStruct((a.shape[0], N_DEV * a.shape[1], a.shape[2]), a.dtype) for a in arrs],
        scratch_shapes=[pltpu.SemaphoreType.DMA((n_pairs * n,)), pltpu.SemaphoreType.DMA((n_pairs * n,)),
                        pltpu.SemaphoreType.DMA((n,))],
    )(*arrs))


def _pair_sums(grads, recvs, cidx, name):
    n = len(grads)

    def body(c_ref, *refs):
        for g_ref, r_ref, o_ref in zip(refs[:n], refs[n:2 * n], refs[2 * n:]):
            o_ref[...] = (g_ref[...].astype(F32) + r_ref[...].astype(F32)).astype(BF16)

    shapes = [(g.shape[0] // N_DEV, g.shape[1]) for g in grads]
    return list(pl.pallas_call(
        body, name=name,
        grid_spec=pltpu.PrefetchScalarGridSpec(
            num_scalar_prefetch=1, grid=(N_CHIP,),
            in_specs=[pl.BlockSpec(s, lambda q, c: (2 * q + c[0], 0)) for s in shapes]
            + [pl.BlockSpec(s, lambda q, c: (q, 0)) for s in shapes],
            out_specs=[pl.BlockSpec(s, lambda q, c: (q, 0)) for s in shapes]),
        out_shape=[jax.ShapeDtypeStruct((N_CHIP * r, cols), BF16) for r, cols in shapes],
        compiler_params=_cparams(("parallel",)),
    )(cidx, *grads, *recvs))


def _sum_blocks(parts, nblk, name):
    r = parts.shape[0] // nblk
    cols = parts.shape[1]

    def body(p_ref, o_ref):
        acc = p_ref[0:r, :].astype(F32)
        for q in range(1, nblk):
            acc = acc + p_ref[q * r:(q + 1) * r, :].astype(F32)
        o_ref[...] = acc

    return pl.pallas_call(
        body, name=name,
        out_shape=jax.ShapeDtypeStruct((r, cols), F32),
        compiler_params=_cparams(),
    )(parts)


def _adamw_math(w, g, m, v):
    m = ADAM_B1 * m + (1.0 - ADAM_B1) * g
    v = ADAM_B2 * v + (1.0 - ADAM_B2) * (g * g)
    m_hat = m / (1.0 - ADAM_B1 ** ADAM_STEP)
    v_hat = v / (1.0 - ADAM_B2 ** ADAM_STEP)
    delta = -ADAM_LR * (m_hat / (jnp.sqrt(v_hat) + ADAM_EPS) + ADAM_WD * w)
    return delta, m, v


def _finish_sharded(parts, w, m, v, name):
    depth, rr, cw = w.shape

    def body(*refs):
        p_refs = refs[:depth]
        w_ref, m_ref, v_ref, g_ref, d_ref, mo_ref, vo_ref = refs[depth:]
        l = pl.program_id(0)
        for k in range(depth):
            @pl.when(l == k)
            def _(p_ref=p_refs[k]):
                r = p_ref.shape[0] // N_CHIP
                acc = p_ref[0:r, :].astype(F32)
                for q in range(1, N_CHIP):
                    acc = acc + p_ref[q * r:(q + 1) * r, :].astype(F32)
                g_ref[...] = acc
                d_ref[...], mo_ref[...], vo_ref[...] = _adamw_math(w_ref[...], acc, m_ref[...], v_ref[...])

    blk = pl.BlockSpec((None, rr, cw), lambda l: (l, 0, 0))
    return pl.pallas_call(
        body, name=name, grid=(depth,),
        in_specs=[pl.BlockSpec(p.shape, lambda l: (0, 0)) for p in parts] + [blk] * 3, out_specs=[blk] * 4,
        out_shape=[jax.ShapeDtypeStruct(w.shape, F32)] * 4,
        compiler_params=_cparams(("arbitrary",)),
    )(*parts, w, m, v)


def _adamw_small(ws, gs, ms, vs, name):
    n = len(ws)

    def body(*refs):
        for k in range(n):
            w_ref, g_ref, m_ref, v_ref = (refs[j * n + k] for j in range(4))
            d_ref, mo_ref, vo_ref = (refs[(4 + j) * n + k] for j in range(3))
            d_ref[...], mo_ref[...], vo_ref[...] = _adamw_math(w_ref[...], g_ref[...], m_ref[...], v_ref[...])

    shapes = [jax.ShapeDtypeStruct(w.shape, F32) for w in ws]
    return pl.pallas_call(
        body, name=name, out_shape=shapes * 3, compiler_params=_cparams(),
    )(*ws, *gs, *ms, *vs)


def kernel(x, ffn1_norm, ffn1_w_gate, ffn1_w_up, ffn1_w_down, mix_norm, w_in, sgu_ln_g, sgu_ln_b, w_spatial, b_spatial, conv_w, conv_b, conv_ln_g, conv_ln_b, pool_w, pool_scale, w_out, ffn2_norm, ffn2_w_gate, ffn2_w_up, ffn2_w_down, final_norm, loss_target, m_ffn1_norm, m_ffn1_w_gate, m_ffn1_w_up, m_ffn1_w_down, m_mix_norm, m_w_in, m_sgu_ln_g, m_sgu_ln_b, m_w_spatial, m_b_spatial, m_conv_w, m_conv_b, m_conv_ln_g, m_conv_ln_b, m_pool_w, m_pool_scale, m_w_out, m_ffn2_norm, m_ffn2_w_gate, m_ffn2_w_up, m_ffn2_w_down, m_final_norm, v_ffn1_norm, v_ffn1_w_gate, v_ffn1_w_up, v_ffn1_w_down, v_mix_norm, v_w_in, v_sgu_ln_g, v_sgu_ln_b, v_w_spatial, v_b_spatial, v_conv_w, v_conv_b, v_conv_ln_g, v_conv_ln_b, v_pool_w, v_pool_scale, v_w_out, v_ffn2_norm, v_ffn2_w_gate, v_ffn2_w_up, v_ffn2_w_down, v_final_norm):
    names = ["ffn1_norm", "ffn1_w_gate", "ffn1_w_up", "ffn1_w_down", "mix_norm", "w_in", "sgu_ln_g", "sgu_ln_b",
             "w_spatial", "b_spatial", "conv_w", "conv_b", "conv_ln_g", "conv_ln_b", "pool_w", "pool_scale",
             "w_out", "ffn2_norm", "ffn2_w_gate", "ffn2_w_up", "ffn2_w_down", "final_norm"]
    W = dict(zip(names, [ffn1_norm, ffn1_w_gate, ffn1_w_up, ffn1_w_down, mix_norm, w_in, sgu_ln_g, sgu_ln_b,
                         w_spatial, b_spatial, conv_w, conv_b, conv_ln_g, conv_ln_b, pool_w, pool_scale, w_out,
                         ffn2_norm, ffn2_w_gate, ffn2_w_up, ffn2_w_down, final_norm]))
    M = dict(zip(names, [m_ffn1_norm, m_ffn1_w_gate, m_ffn1_w_up, m_ffn1_w_down, m_mix_norm, m_w_in, m_sgu_ln_g,
                         m_sgu_ln_b, m_w_spatial, m_b_spatial, m_conv_w, m_conv_b, m_conv_ln_g, m_conv_ln_b,
                         m_pool_w, m_pool_scale, m_w_out, m_ffn2_norm, m_ffn2_w_gate, m_ffn2_w_up, m_ffn2_w_down,
                         m_final_norm]))
    V = dict(zip(names, [v_ffn1_norm, v_ffn1_w_gate, v_ffn1_w_up, v_ffn1_w_down, v_mix_norm, v_w_in, v_sgu_ln_g,
                         v_sgu_ln_b, v_w_spatial, v_b_spatial, v_conv_w, v_conv_b, v_conv_ln_g, v_conv_ln_b,
                         v_pool_w, v_pool_scale, v_w_out, v_ffn2_norm, v_ffn2_w_gate, v_ffn2_w_up, v_ffn2_w_down,
                         v_final_norm]))

    depth, d = ffn1_norm.shape
    t = x.shape[1]
    sgu = sgu_ln_g.shape[1]
    pool = pool_scale.shape[1]
    n_head = sgu // HEAD_DIM
    cw_shard = conv_w.shape[2]
    xs = x.reshape(t, d)
    target = loss_target.reshape(t, d)

    def tr(w):
        return jnp.swapaxes(w, 1, 2).astype(BF16)

    ffn_shards = [[jnp.stack([tr(ffn1_w_gate)[l], tr(ffn1_w_up)[l], ffn1_w_down[l].astype(BF16)]),
                   jnp.stack([tr(ffn2_w_gate)[l], tr(ffn2_w_up)[l], ffn2_w_down[l].astype(BF16)])]
                  for l in range(depth)]
    win_shards = [tr(w_in)[l:l + 1] for l in range(depth)]
    wout_shards = [w_out[l:l + 1].astype(BF16) for l in range(depth)]
    cw_rows = depth * CONV_WIDTH
    cw_pad = -cw_rows % 8
    cw_send = jnp.pad(conv_w.reshape(cw_rows, cw_shard), ((0, cw_pad), (0, 0)))[None]
    wffn, wb, wc = {}, {}, {}
    wffn[(0, 0)], wb[0], wc[0], cwg = _all_gather_relayed(
        [ffn_shards[0][0], win_shards[0], wout_shards[0], cw_send], "ag_first")
    conv_w_full = cwg.reshape(N_DEV, cw_rows + cw_pad, cw_shard)[:, :cw_rows].reshape(
        N_DEV, depth, CONV_WIDTH, cw_shard).transpose(1, 2, 0, 3).reshape(depth, CONV_WIDTH, N_DEV * cw_shard)

    def mixer_params(l):
        return dict(
            sgu_ln_g=sgu_ln_g[l:l + 1], sgu_ln_b=sgu_ln_b[l:l + 1], w_spatial=w_spatial[l],
            bs_full=jnp.repeat(b_spatial[l].T, HEAD_DIM, axis=1),
            conv_w=conv_w_full[l], conv_b=conv_b[l:l + 1], conv_ln_g=conv_ln_g[l:l + 1],
            conv_ln_b=conv_ln_b[l:l + 1], bd=_block_diag(pool_w[l]), pool_scale=pool_scale[l:l + 1])

    saved = []
    cur = xs
    for l in range(depth):
        p = mixer_params(l)
        x0 = cur
        more = l + 1 < depth
        (x1, gate1, up1, act1), part = _ffn_fwd(x0, ffn1_norm[l:l + 1], wffn[(l, 0)], 0, f"ffn1_fwd_{l}",
                                          _GatherIci([ffn_shards[l][1]]))
        riding = [_GatherForward(part)] + ([_GatherIci([win_shards[l + 1], wout_shards[l + 1]])] if more else [])
        (x2, z, hc, cat), part = _mixer_fwd(x1, mix_norm[l:l + 1], wb[l], wc[l], p, f"mixer_fwd_{l}", _Merged(riding))
        wffn[(l, 1)] = part[0]
        riding = [_GatherIci([ffn_shards[l + 1][0]]), _GatherForward(part[1:])] if more else []
        (x3, gate2, up2, act2), part = _ffn_fwd(x2, ffn2_norm[l:l + 1], wffn[(l, 1)], 0, f"ffn2_fwd_{l}",
                                          _Merged(riding) if more else None)
        if more:
            wb[l + 1], wc[l + 1] = part[1:]
            (wffn[(l + 1, 0)],) = _comm(_GatherForward(part[:1]), f"ag_forward_{l + 1}")
        saved.append((p, x0, gate1, up1, act1, x1, z, hc, cat, x2, gate2, up2, act2))
        cur = x3

    dx, d_final, loss_part = _loss_head(cur, final_norm.reshape(1, d), target, "loss_head")

    cidx = lax.axis_index("c").astype(jnp.int32).reshape(1)
    from_chips = {}
    to_pair, to_chip = [], []
    small = []

    def pair_payload():
        return _PairExchange([g for _, g in to_pair]) if to_pair else None

    def pair_done(received):
        if to_pair:
            (nm, l), _ = to_pair[0]
            sums = _pair_sums([g for _, g in to_pair], list(received), cidx, f"rs_pair_sum_{nm}_{l}")
            to_chip.extend((key, s) for (key, _), s in zip(to_pair, sums))
        to_pair.clear()

    def take_chip():
        items = list(to_chip)
        to_chip.clear()
        return items

    def chip_payload(items):
        return _ChipExchange([s for _, s in items]) if items else None

    def chip_done(items, landed):
        for (key, _), o in zip(items, landed):
            from_chips[key] = o

    def ffn_weight_grads(prefix, l, dgate, dup, act, h, dy):
        items = take_chip()
        items, later = items[:2], items[2:]
        to_chip.extend(later)
        grads3, landed = _ffn_dw(dgate, dup, act, h, dy, f"dw_{prefix}_{l}", chip_payload(items))
        chip_done(items, landed)
        to_pair.extend(((f"{prefix}_{nm}", l), g) for nm, g in zip(("w_gate", "w_up", "w_down"), grads3))

    for l in reversed(range(depth)):
        p, x0, gate1, up1, act1, x1, z, hc, cat, x2, gate2, up2, act2 = saved[l]
        (dx, dgate, dup, h, dy, dg_ffn2), received = _ffn_bwd(
            x2, ffn2_norm[l:l + 1], dx, gate2, up2, wffn[(l, 1)], 0, f"ffn2_bwd_{l}", pair_payload())
        pair_done(received)
        ffn_weight_grads("ffn2", l, dgate, dup, act2, h, dy)
        g_out, received = _tn_matmul(cat, dx, f"dw_w_out_{l}", pair_payload())
        pair_done(received)
        items = take_chip()
        items, later = items[:3], items[3:]
        to_chip.extend(later)
        (dz, g384, gws, gpool), landed = _mix_bwd(z, hc, dx, wc[l], 0, p, f"mix_bwd_{l}", chip_payload(items))
        chip_done(items, landed)
        dx, hm, dg_mix = _proj_in_bwd(x1, mix_norm[l:l + 1], dx, dz, wb[l], 0, f"proj_in_bwd_{l}")
        g_in, _ = _tn_matmul(dz, hm, f"dw_w_in_{l}")
        to_pair.extend([(("w_out", l), g_out), (("w_in", l), g_in)])
        if l > 0:
            (dx, dgate, dup, h, dy, dg_ffn1), received = _ffn_bwd(
                x0, ffn1_norm[l:l + 1], dx, gate1, up1, wffn[(l, 0)], 0, f"ffn1_bwd_{l}", pair_payload())
            pair_done(received)
            ffn_weight_grads("ffn1", l, dgate, dup, act1, h, dy)
            small.append((l, g384, gws, gpool, dg_ffn1, dg_mix, dg_ffn2))
            continue

        small.append((0, g384, gws, gpool, None, dg_mix, dg_ffn2))
        small.sort(key=lambda s: s[0])
        norm_rows = []
        for (sl, _, _, _, dg1, dgm, dg2) in small:
            norm_rows += [jnp.zeros((1, d), F32) if dg1 is None else dg1, dgm, dg2]
        norm_rows += [d_final, jnp.pad(loss_part, ((0, 0), (0, d - LANES)))]
        n_norm = len(norm_rows)
        norm_pack = jnp.concatenate(norm_rows + [jnp.zeros((8 - n_norm % 8, d), F32)] * (n_norm % 8 != 0), axis=0)
        parts = [norm_pack]
        for (_, s384, sws, spool, _, _, _) in small:
            parts += [s384, sws.reshape((n_head + 1) * CHUNK, CHUNK), spool]
        n_pair = len(to_pair)
        early = take_chip()
        riding = [pair_payload(), _GatherIci([a[None] for a in parts])] + ([chip_payload(early)] if early else [])
        (dx, dgate, dup, h, dy, dg_ffn1), landed = _ffn_bwd(
            x0, ffn1_norm[l:l + 1], dx, gate1, up1, wffn[(l, 0)], 0, f"ffn1_bwd_{l}", _Merged(riding))
        pair_done(landed[:n_pair])
        chip_done(early, landed[n_pair + len(parts):])
        items = take_chip()
        g_gate, landed = _tn_matmul(
            dgate, h, f"dw_ffn1_w_gate_{l}",
            _Merged([chip_payload(items), _GatherForward(landed[n_pair:n_pair + len(parts)])]))
        chip_done(items, landed[:len(items)])
        gathered = landed[len(items):]
        to_pair.append((("ffn1_w_gate", l), g_gate))
        g_up, received = _tn_matmul(dup, h, f"dw_ffn1_w_up_{l}", pair_payload())
        pair_done(received)
        to_pair.append((("ffn1_w_up", l), g_up))
        items = take_chip()
        g_down, landed = _tn_matmul(act1, dy, f"dw_ffn1_w_down_{l}", _Merged([chip_payload(items), pair_payload()]))
        chip_done(items, landed[:len(items)])
        pair_done(landed[len(items):])
        to_pair.append((("ffn1_w_down", l), g_down))
    grad_x = dx.reshape(x.shape)
    pair_done(_comm(pair_payload(), "rs_pair_exchange_last"))
    items = take_chip()
    (late_norm,), landed = _all_gather([jnp.pad(dg_ffn1, ((0, 7), (0, 0)))[None]], "ag_tail", chip_payload(items))
    chip_done(items, landed)

    summed = [_sum_blocks(g[0], N_DEV, f"sum_small_{k}") for k, g in enumerate(gathered)]
    late_sum = _sum_blocks(late_norm[0], N_DEV, "sum_small_late")
    norm_sum = summed[0]
    loss = norm_sum[3 * depth + 1, 0]
    cpos = lax.axis_index("x") * 4 + lax.axis_index("y") * 2 + lax.axis_index("c")
    sg = {nm: [] for nm in names}
    for l in range(depth):
        g384, gws, gpool = summed[1 + 3 * l], summed[2 + 3 * l].reshape(n_head + 1, CHUNK, CHUNK), summed[3 + 3 * l]
        sg["ffn1_norm"].append(norm_sum[3 * l] if l > 0 else late_sum[0])
        sg["mix_norm"].append(norm_sum[3 * l + 1])
        sg["ffn2_norm"].append(norm_sum[3 * l + 2])
        sg["sgu_ln_g"].append(g384[_R_SGU_G])
        sg["sgu_ln_b"].append(g384[_R_SGU_B])
        sg["conv_b"].append(g384[_R_CONV_B])
        sg["conv_ln_g"].append(g384[_R_CLN_G])
        sg["conv_ln_b"].append(g384[_R_CLN_B])
        sg["conv_w"].append(lax.dynamic_slice_in_dim(g384[_R_CONV_W:_R_CONV_W + CONV_WIDTH], cpos * cw_shard,
                                                     cw_shard, axis=1))
        sg["w_spatial"].append(gws[:n_head])
        sg["b_spatial"].append(gws[n_head][:, :n_head].T)
        sg["pool_w"].append(jnp.stack([gpool[k * HEAD_DIM:(k + 1) * HEAD_DIM, k * HEAD_DIM:(k + 1) * HEAD_DIM]
                                       for k in range(pool // HEAD_DIM)], axis=0))
        sg["pool_scale"].append(gpool[pool])
    small_names = ["ffn1_norm", "mix_norm", "sgu_ln_g", "sgu_ln_b", "w_spatial", "b_spatial", "conv_w", "conv_b",
                   "conv_ln_g", "conv_ln_b", "pool_w", "pool_scale", "ffn2_norm"]
    grads = {nm: jnp.stack(sg[nm], axis=0) for nm in small_names}
    grads["final_norm"] = norm_sum[3 * depth]

    delta, new_m, new_v = {}, {}, {}
    big_names = ["ffn1_w_gate", "ffn1_w_up", "ffn1_w_down", "w_in", "w_out", "ffn2_w_gate", "ffn2_w_up",
                 "ffn2_w_down"]
    transposed = {"ffn1_w_gate", "ffn1_w_up", "w_in", "ffn2_w_gate", "ffn2_w_up"}
    for nm in big_names:
        view = (lambda a: jnp.swapaxes(a, 1, 2)) if nm in transposed else (lambda a: a)
        outs = _finish_sharded([from_chips[(nm, l)] for l in range(depth)], view(W[nm]), view(M[nm]), view(V[nm]),
                               f"adamw_{nm}")
        grads[nm], delta[nm], new_m[nm], new_v[nm] = (view(o) for o in outs)
    snames = small_names + ["final_norm"]

    def flat2(a):
        return a.reshape(-1, a.shape[-1])

    outs = _adamw_small([flat2(W[nm]) for nm in snames], [flat2(grads[nm]) for nm in snames],
                        [flat2(M[nm]) for nm in snames], [flat2(V[nm]) for nm in snames], "adamw_small")
    ns = len(snames)
    for k, nm in enumerate(snames):
        shp = W[nm].shape
        delta[nm], new_m[nm], new_v[nm] = (outs[k].reshape(shp), outs[ns + k].reshape(shp),
                                           outs[2 * ns + k].reshape(shp))

    return (loss, grad_x, *[grads[nm] for nm in names], *[delta[nm] for nm in names],
            *[new_m[nm] for nm in names], *[new_v[nm] for nm in names])
```

```python
import functools

import jax
import jax.numpy as jnp
from jax import lax
from jax.experimental import pallas as pl
from jax.experimental.pallas import tpu as pltpu

F32 = jnp.float32
BF16 = jnp.bfloat16
EPS = 1e-6
N_DEV = 8
N_CHIP = 4
MESH = pl.DeviceIdType.MESH
ANY = pl.BlockSpec(memory_space=pl.ANY)

VMEM_LIMIT_BYTES = 56 * 1024 * 1024
LANES = 128
HALO = 32
HEAD_DIM = 64
CHUNK = 128
CONV_WIDTH = 31
POOL_WINDOWS = (2, 4, 8, 16)

ADAM_LR = 0.001
ADAM_B1 = 0.9
ADAM_B2 = 0.999
ADAM_EPS = 1e-08
ADAM_WD = 0.01
ADAM_STEP = 10


def _cparams(sem=None):
    return pltpu.CompilerParams(dimension_semantics=sem, vmem_limit_bytes=VMEM_LIMIT_BYTES)


def _position():
    return lax.axis_index("x"), lax.axis_index("y"), lax.axis_index("c")


class _Copies:
    def __init__(self):
        self.local, self.sends, self.recvs = [], [], []

    def extend(self, other):
        self.local += other.local
        self.sends += other.sends
        self.recvs += other.recvs

    def start(self):
        for cp in self.local + self.sends:
            cp.start()

    def wait(self):
        for land, send_sems, recv_sems, k, peer in self.recvs:
            _remote(land, land, send_sems, recv_sems, k, peer).wait_recv()
        for cp in self.sends:
            cp.wait_send()
        for cp in self.local:
            cp.wait()


def _remote(src, dst, send_sems, recv_sems, k, to):
    return pltpu.make_async_remote_copy(src_ref=src, dst_ref=dst, send_sem=send_sems.at[k], recv_sem=recv_sems.at[k],
                                        device_id=to, device_id_type=MESH)


class _Payload:
    ins, out_shapes, aliases, n_remote, n_local = (), (), {}, 0, 0

    def sem_shapes(self):
        return [pltpu.SemaphoreType.DMA((max(self.n_remote, 1),)), pltpu.SemaphoreType.DMA((max(self.n_remote, 1),)),
                pltpu.SemaphoreType.DMA((max(self.n_local, 1),))]


class _GatherIci(_Payload):
    def __init__(self, shards):
        self.ins = list(shards)
        self.out_shapes = [jax.ShapeDtypeStruct((s.shape[0], N_DEV * s.shape[1], s.shape[2]), s.dtype) for s in shards]
        self.n_remote, self.n_local = 4 * len(shards), len(shards)

    def build(self, ins, outs, send_sems, recv_sems, local_sems, k0=0, l0=0):
        x, y, c = _position()
        peers = [(x, y, 1 - c), (1 - x, y, c), (x, 1 - y, c), (1 - x, 1 - y, c)]
        cps = _Copies()
        for a, (src, out) in enumerate(zip(ins, outs)):
            r = src.shape[1]

            def rows(px, py, pc, out=out, r=r):
                return out.at[:, pl.ds((4 * px + 2 * py + pc) * r, r), :]

            cps.local.append(pltpu.make_async_copy(src, rows(x, y, c), local_sems.at[l0 + a]))
            for k, peer in enumerate(peers):
                cps.sends.append(_remote(src, rows(x, y, c), send_sems, recv_sems, k0 + 4 * a + k, peer))
                cps.recvs.append((rows(*peer), send_sems, recv_sems, k0 + 4 * a + k, peer))
        return cps


class _GatherForward(_Payload):
    def __init__(self, partials):
        self.ins = list(partials)
        self.out_shapes = [jax.ShapeDtypeStruct(p.shape, p.dtype) for p in partials]
        self.aliases = {a: a for a in range(len(partials))}
        self.n_remote = 3 * len(partials)

    def build(self, ins, outs, send_sems, recv_sems, local_sems, k0=0, l0=0):
        x, y, c = _position()
        chips = [(1 - x, y), (x, 1 - y), (1 - x, 1 - y)]
        cps = _Copies()
        for a, out in enumerate(outs):
            r = out.shape[1] // N_DEV
            for k, (px, py) in enumerate(chips):
                mine = out.at[:, pl.ds((4 * px + 2 * py + c) * r, r), :]
                theirs = out.at[:, pl.ds((4 * px + 2 * py + 1 - c) * r, r), :]
                cps.sends.append(_remote(mine, mine, send_sems, recv_sems, k0 + 3 * a + k, (x, y, 1 - c)))
                cps.recvs.append((theirs, send_sems, recv_sems, k0 + 3 * a + k, (x, y, 1 - c)))
        return cps


class _PairExchange(_Payload):
    def __init__(self, grads):
        self.ins = list(grads)
        self.out_shapes = [jax.ShapeDtypeStruct((g.shape[0] // 2, g.shape[1]), g.dtype) for g in grads]
        self.n_remote = N_CHIP * len(grads)

    def build(self, ins, outs, send_sems, recv_sems, local_sems, k0=0, l0=0):
        x, y, c = _position()
        cps = _Copies()
        for a, (src, out) in enumerate(zip(ins, outs)):
            r = src.shape[0] // N_DEV
            for q in range(N_CHIP):
                land = out.at[pl.ds(q * r, r), :]
                cps.sends.append(_remote(src.at[pl.ds((2 * q + 1 - c) * r, r), :], land, send_sems, recv_sems,
                                         k0 + N_CHIP * a + q, (x, y, 1 - c)))
                cps.recvs.append((land, send_sems, recv_sems, k0 + N_CHIP * a + q, (x, y, 1 - c)))
        return cps


class _ChipExchange(_Payload):
    def __init__(self, sums):
        self.ins = list(sums)
        self.out_shapes = [jax.ShapeDtypeStruct(s.shape, s.dtype) for s in sums]
        self.n_remote, self.n_local = 3 * len(sums), len(sums)

    def build(self, ins, outs, send_sems, recv_sems, local_sems, k0=0, l0=0):
        x, y, c = _position()
        my_chip = 2 * x + y
        chips = [(1 - x, y), (x, 1 - y), (1 - x, 1 - y)]
        cps = _Copies()
        for a, (src, out) in enumerate(zip(ins, outs)):
            r = src.shape[0] // N_CHIP
            mine = out.at[pl.ds(my_chip * r, r), :]
            cps.local.append(pltpu.make_async_copy(src.at[pl.ds(my_chip * r, r), :], mine, local_sems.at[l0 + a]))
            for k, (px, py) in enumerate(chips):
                land = out.at[pl.ds((2 * px + py) * r, r), :]
                cps.sends.append(_remote(src.at[pl.ds((2 * px + py) * r, r), :], mine, send_sems, recv_sems,
                                         k0 + 3 * a + k, (px, py, c)))
                cps.recvs.append((land, send_sems, recv_sems, k0 + 3 * a + k, (px, py, c)))
        return cps


class _Merged(_Payload):
    def __init__(self, parts):
        self.parts = list(parts)
        self.ins = [a for p in parts for a in p.ins]
        self.out_shapes = [s for p in parts for s in p.out_shapes]
        self.aliases, self.offsets = {}, []
        i0 = o0 = k0 = l0 = 0
        for p in parts:
            self.offsets.append((i0, o0, k0, l0))
            self.aliases.update({i0 + i: o0 + o for i, o in p.aliases.items()})
            i0, o0, k0, l0 = i0 + len(p.ins), o0 + len(p.out_shapes), k0 + p.n_remote, l0 + p.n_local
        self.n_remote, self.n_local = k0, l0

    def build(self, ins, outs, send_sems, recv_sems, local_sems):
        cps = _Copies()
        for p, (i0, o0, k0, l0) in zip(self.parts, self.offsets):
            cps.extend(p.build(ins[i0:i0 + len(p.ins)], outs[o0:o0 + len(p.out_shapes)], send_sems, recv_sems,
                               local_sems, k0, l0))
        return cps


def _call(body, name, grid, in_specs, out_specs, out_shape, scratch_shapes, semantics, args, payload=None):
    if payload is None:
        outs = pl.pallas_call(body, name=name, grid=grid, in_specs=in_specs, out_specs=out_specs,
                              out_shape=out_shape, scratch_shapes=scratch_shapes,
                              compiler_params=_cparams(semantics))(*args)
        return list(outs), []
    n_in, n_out, n_scr = len(in_specs), len(out_specs), len(scratch_shapes)
    p_in, p_out = len(payload.ins), len(payload.out_shapes)

    def carried(*refs):
        ins, p_ins = refs[:n_in], refs[n_in:n_in + p_in]
        o0 = n_in + p_in
        outs, p_outs = refs[o0:o0 + n_out], refs[o0 + n_out:o0 + n_out + p_out]
        s0 = o0 + n_out + p_out
        scr, sems = refs[s0:s0 + n_scr], refs[s0 + n_scr:]
        ids = [pl.program_id(k) for k in range(len(grid))]
        at_first = functools.reduce(jnp.logical_and, [i == 0 for i in ids])
        at_last = functools.reduce(jnp.logical_and, [i == g - 1 for i, g in zip(ids, grid)])

        @pl.when(at_first)
        def _():
            payload.build(p_ins, p_outs, *sems).start()

        body(*ins, *outs, *scr)

        @pl.when(at_last)
        def _():
            payload.build(p_ins, p_outs, *sems).wait()

    outs = pl.pallas_call(
        carried, name=name, grid=grid, in_specs=list(in_specs) + [ANY] * p_in,
        out_specs=list(out_specs) + [ANY] * p_out, out_shape=list(out_shape) + list(payload.out_shapes),
        scratch_shapes=list(scratch_shapes) + payload.sem_shapes(),
        input_output_aliases={n_in + i: n_out + o for i, o in payload.aliases.items()},
        compiler_params=_cparams(("arbitrary",) * len(grid)))(*args, *payload.ins)
    return list(outs[:n_out]), list(outs[n_out:])


def _comm(payload, name):
    def body(*refs):
        p_in, p_out = len(payload.ins), len(payload.out_shapes)
        cps = payload.build(refs[:p_in], refs[p_in:p_in + p_out], *refs[p_in + p_out:])
        cps.start()
        cps.wait()

    return list(pl.pallas_call(
        body, name=name, in_specs=[ANY] * len(payload.ins), out_specs=[ANY] * len(payload.out_shapes),
        out_shape=list(payload.out_shapes), scratch_shapes=payload.sem_shapes(),
        input_output_aliases=dict(payload.aliases))(*payload.ins))


def _dot(a, b):
    return jnp.dot(a, b, preferred_element_type=F32)


def _dot_nt(a, b):
    return lax.dot_general(a, b, (((1,), (1,)), ((), ())), preferred_element_type=F32)


def _dot_tn(a, b):
    return lax.dot_general(a, b, (((0,), (0,)), ((), ())), preferred_element_type=F32)


def _split_dot(x, e):
    hi = x.astype(BF16)
    r1 = x - hi.astype(F32)
    mid = r1.astype(BF16)
    lo = (r1 - mid.astype(F32)).astype(BF16)
    return _dot(hi, e) + _dot(mid, e) + _dot(lo, e)


def _rms(x):
    rstd = lax.rsqrt(jnp.mean(x * x, axis=-1, keepdims=True) + EPS)
    return x * rstd, rstd


def _rms_bwd(xhat, rstd, g, dh):
    dxhat = dh * g
    dx = rstd * (dxhat - xhat * jnp.mean(dxhat * xhat, axis=-1, keepdims=True))
    return dx, jnp.sum(dh * xhat, axis=0, keepdims=True)


def _ln(v):
    mu = jnp.mean(v, axis=-1, keepdims=True)
    xc = v - mu
    rstd = lax.rsqrt(jnp.mean(xc * xc, axis=-1, keepdims=True) + EPS)
    return xc * rstd, rstd


def _ln_bwd(vhat, rstd, g, dy):
    dvhat = dy * g
    dv = rstd * (dvhat - jnp.mean(dvhat, axis=-1, keepdims=True)
                 - vhat * jnp.mean(dvhat * vhat, axis=-1, keepdims=True))
    return dv, jnp.sum(dy * vhat, axis=0, keepdims=True), jnp.sum(dy, axis=0, keepdims=True)


_INV_SQRT2 = 0.7071067811865476
_INV_SQRT2PI = 0.3989422804014327


def _gelu(x):
    return 0.5 * x * (1.0 + lax.erf(x * _INV_SQRT2))


def _gelu_grad(x):
    return 0.5 * (1.0 + lax.erf(x * _INV_SQRT2)) + x * jnp.exp(-0.5 * x * x) * _INV_SQRT2PI


def _silu_grad(x):
    s = jax.nn.sigmoid(x)
    return s * (1.0 + x * (1.0 - s))


def _ffn_fwd(x, g, wa, mi, name, payload=None):
    t, d = x.shape
    f = wa.shape[1]
    tm, tf = 1024, 256
    nc = f // tf
    groups = [slice(k * (tm // 2), (k + 1) * (tm // 2)) for k in range(2)]

    def body(x_ref, g_ref, wgu_ref, wd_ref, xo_ref, gate_ref, up_ref, act_ref, h_scr, acc_scr):
        c = pl.program_id(1)

        @pl.when(c == 0)
        def _():
            xhat, _ = _rms(x_ref[...])
            h_scr[...] = (xhat * g_ref[...]).astype(BF16)
            acc_scr[...] = jnp.zeros_like(acc_scr)

        wgu, wd = wgu_ref[...].reshape(2 * tf, d), wd_ref[...]
        for rows in groups:
            gu = _dot_nt(h_scr[rows, :], wgu)
            gate, up = gu[:, :tf], gu[:, tf:]
            gate_ref[rows, :] = gate.astype(BF16)
            up_ref[rows, :] = up.astype(BF16)
            act = (gate * jax.nn.sigmoid(gate) * up).astype(BF16)
            act_ref[rows, :] = act
            acc_scr[rows, :] += _dot(act, wd)

        @pl.when(c == nc - 1)
        def _():
            xo_ref[...] = x_ref[...] + 0.5 * acc_scr[...]

    assert mi % 2 == 0
    return _call(
        body, name, (t // tm, nc),
        [pl.BlockSpec((tm, d), lambda i, c: (i, 0)), pl.BlockSpec((1, d), lambda i, c: (0, 0)),
         pl.BlockSpec((2, tf, d), lambda i, c: (mi // 2, c, 0)),
         pl.BlockSpec((None, tf, d), lambda i, c: (mi + 2, c, 0))],
        [pl.BlockSpec((tm, d), lambda i, c: (i, 0))] + [pl.BlockSpec((tm, tf), lambda i, c: (i, c))] * 3,
        [jax.ShapeDtypeStruct((t, d), F32)] + [jax.ShapeDtypeStruct((t, f), BF16)] * 3,
        [pltpu.VMEM((tm, d), BF16), pltpu.VMEM((tm, d), F32)],
        ("parallel", "arbitrary"), (x, g, wa, wa), payload)


def _ffn_bwd(x, g, dxo, gate, up, wa, mi, name, payload=None):
    t, d = x.shape
    f = wa.shape[1]
    tm, tf = 1024, 256
    nc = f // tf
    groups = [slice(k * (tm // 2), (k + 1) * (tm // 2)) for k in range(2)]

    def body(x_ref, g_ref, dxo_ref, gate_ref, up_ref, wgu_ref, wd_ref,
             dx_ref, dgate_ref, dup_ref, h_ref, dy_ref, dg_ref, acc_scr):
        i, c = pl.program_id(0), pl.program_id(1)

        @pl.when(c == 0)
        def _():
            xhat, _ = _rms(x_ref[...])
            h_ref[...] = (xhat * g_ref[...]).astype(BF16)
            dy_ref[...] = (0.5 * dxo_ref[...]).astype(BF16)
            acc_scr[...] = jnp.zeros_like(acc_scr)

        @pl.when((c == 0) & (i == 0))
        def _():
            dg_ref[...] = jnp.zeros_like(dg_ref)

        wg, wu, wd = wgu_ref[0], wgu_ref[1], wd_ref[...]
        for rows in groups:
            gt = gate_ref[rows, :].astype(F32)
            u = up_ref[rows, :].astype(F32)
            s = jax.nn.sigmoid(gt)
            silu = gt * s
            dact = _dot_nt(dy_ref[rows, :], wd)
            dgate = (dact * u * (s * (1.0 + gt * (1.0 - s)))).astype(BF16)
            dup = (dact * silu).astype(BF16)
            dgate_ref[rows, :] = dgate
            dup_ref[rows, :] = dup
            acc_scr[rows, :] += _dot(dgate, wg) + _dot(dup, wu)

        @pl.when(c == nc - 1)
        def _():
            xhat, rstd = _rms(x_ref[...])
            dxn, dg = _rms_bwd(xhat, rstd, g_ref[...], acc_scr[...])
            dx_ref[...] = dxo_ref[...] + dxn
            dg_ref[...] += dg

    assert mi % 2 == 0
    row = pl.BlockSpec((tm, d), lambda i, c: (i, 0))
    col = pl.BlockSpec((tm, tf), lambda i, c: (i, c))
    vec = pl.BlockSpec((1, d), lambda i, c: (0, 0))
    return _call(
        body, name, (t // tm, nc),
        [row, vec, row, col, col, pl.BlockSpec((2, tf, d), lambda i, c: (mi // 2, c, 0)),
         pl.BlockSpec((None, tf, d), lambda i, c: (mi + 2, c, 0))],
        [row, col, col, row, row, vec],
        [jax.ShapeDtypeStruct((t, d), F32), jax.ShapeDtypeStruct((t, f), BF16),
         jax.ShapeDtypeStruct((t, f), BF16),
         jax.ShapeDtypeStruct((t, d), BF16), jax.ShapeDtypeStruct((t, d), BF16),
         jax.ShapeDtypeStruct((1, d), F32)],
        [pltpu.VMEM((tm, d), F32)],
        ("arbitrary", "arbitrary"), (x, g, dxo, gate, up, wa, wa), payload)


def _ffn_dw(dgate, dup, act, h, dy, name, payload=None):
    t, f = dgate.shape
    d = h.shape[1]
    tk = 512
    tmm = f // 2
    nk = t // tk

    def body(dg_ref, du_ref, a_ref, h_ref, dy_ref, og_ref, ou_ref, od_ref, acc_g, acc_u, acc_d):
        k = pl.program_id(1)

        @pl.when(k == 0)
        def _():
            acc_g[...] = jnp.zeros_like(acc_g)
            acc_u[...] = jnp.zeros_like(acc_u)
            acc_d[...] = jnp.zeros_like(acc_d)

        hv = h_ref[...]
        acc_g[...] += _dot_tn(dg_ref[...], hv)
        acc_u[...] += _dot_tn(du_ref[...], hv)
        acc_d[...] += _dot_tn(a_ref[...], dy_ref[...])

        @pl.when(k == nk - 1)
        def _():
            og_ref[...] = acc_g[...].astype(BF16)
            ou_ref[...] = acc_u[...].astype(BF16)
            od_ref[...] = acc_d[...].astype(BF16)

    col = pl.BlockSpec((tk, tmm), lambda j, k: (k, j))
    row = pl.BlockSpec((tk, d), lambda j, k: (k, 0))
    out = pl.BlockSpec((tmm, d), lambda j, k: (j, 0))
    return _call(
        body, name, (f // tmm, nk), [col, col, col, row, row], [out, out, out],
        [jax.ShapeDtypeStruct((f, d), BF16)] * 3, [pltpu.VMEM((tmm, d), F32)] * 3,
        ("parallel", "arbitrary"), (dgate, dup, act, h, dy), payload)


def _tn_matmul(a, b, name, payload=None):
    t, m = a.shape
    n = b.shape[1]
    tk = 1024
    tmm = m // 2 if (m // 2) % LANES == 0 else m
    nk = t // tk

    def body(a_ref, b_ref, o_ref, acc_scr):
        k = pl.program_id(1)

        @pl.when(k == 0)
        def _():
            acc_scr[...] = jnp.zeros_like(acc_scr)

        acc_scr[...] += _dot_tn(a_ref[...].astype(BF16), b_ref[...].astype(BF16))

        @pl.when(k == nk - 1)
        def _():
            o_ref[...] = acc_scr[...].astype(BF16)

    (out,), p_outs = _call(
        body, name, (m // tmm, nk),
        [pl.BlockSpec((tk, tmm), lambda j, k: (k, j)), pl.BlockSpec((tk, n), lambda j, k: (k, 0))],
        [pl.BlockSpec((tmm, n), lambda j, k: (j, 0))],
        [jax.ShapeDtypeStruct((m, n), BF16)],
        [pltpu.VMEM((tmm, n), F32)],
        ("parallel", "arbitrary"), (a, b), payload)
    return out, p_outs


def _lane_ids(shape):
    return lax.broadcasted_iota(jnp.int32, shape, 1)


def _tril(w):
    r = lax.broadcasted_iota(jnp.int32, w.shape, 0)
    c = lax.broadcasted_iota(jnp.int32, w.shape, 1)
    return jnp.where(r >= c, w, 0.0)


def _shift_down(x, k):
    return x if k == 0 else pltpu.roll(x, k, 0)


def _shift_up(x, k):
    return x if k == 0 else pltpu.roll(x, x.shape[0] - k, 0)


def _sub_tile_shifts(ext, shift):
    return [shift(ext, b) for b in range(8)]


def _tap(shifted, j, n_out, down):
    a, b = divmod(j, 8)
    r0 = HALO - 8 * a if down else 8 * a
    return shifted[b][r0:r0 + n_out]


def _depthwise(shifted, w, n_out, down):
    acc = None
    for j in range(CONV_WIDTH):
        term = _tap(shifted, j, n_out, down) * w[CONV_WIDTH - 1 - j:CONV_WIDTH - j]
        acc = term if acc is None else acc + term
    return acc


def _conv_wgrad(shifted, dhc, n_out):
    return [jnp.sum(_tap(shifted, CONV_WIDTH - 1 - k, n_out, True) * dhc, axis=0, keepdims=True)
            for k in range(CONV_WIDTH)]


def _window_sums(ext, shift):
    s2 = ext + shift(ext, 1)
    s4 = s2 + shift(s2, 2)
    s8 = s4 + shift(s4, 4)
    s16 = s8 + shift(s8, 8)
    grp = _lane_ids(ext.shape) // HEAD_DIM
    return jnp.where(grp == 0, s2, jnp.where(grp == 1, s4, jnp.where(grp == 2, s8, s16)))


def _pool_count(t0, n, width):
    pos = (lax.broadcasted_iota(jnp.int32, (n, width), 0) + (t0 + 1)).astype(F32)
    grp = _lane_ids((n, width)) // HEAD_DIM
    win = jnp.where(grp == 0, 2.0, jnp.where(grp == 1, 4.0, jnp.where(grp == 2, 8.0, 16.0)))
    return jnp.minimum(pos, win)


def _block_diag(pw):
    gn, cg, _ = pw.shape
    rows = []
    for gi in range(gn):
        parts = [pw[gi] if gj == gi else jnp.zeros((cg, cg), pw.dtype) for gj in range(gn)]
        rows.append(jnp.concatenate(parts, axis=1))
    return jnp.concatenate(rows, axis=0)


def _head_pair_mix(w_even, w_odd, v):
    lo = _lane_ids((CHUNK, LANES)) < HEAD_DIM
    return jnp.where(lo, _dot(w_even, v), _dot(w_odd, v))


def _mixer_fwd(x, g, wb, wc, p, name, payload=None):
    t, d = x.shape
    d_in = wb.shape[1]
    sgu = p["sgu_ln_g"].shape[1]
    pool = p["pool_scale"].shape[1]
    d_mix = 2 * sgu + pool
    tm = 512
    n_i = t // tm
    hb = tm // HALO

    def body(x_ref, xp_ref, g_ref, wi_ref, wo_ref, lng_ref, lnb_ref, ws_ref, bs_ref, cw_ref, cb_ref, clg_ref,
             clb_ref, bd_ref, ps_ref, xo_ref, z_ref, hc_ref, cat_ref):
        i = pl.program_id(0)
        first = i == 0
        gain, wi = g_ref[...], wi_ref[...]

        def project(xv):
            xhat, _ = _rms(xv)
            return _dot_nt((xhat * gain).astype(BF16), wi)

        z_main = project(x_ref[...])
        z_ref[...] = z_main
        z_prev = jnp.where(first, 0.0, project(xp_ref[...]))

        lng, lnb = lng_ref[...], lnb_ref[...]
        wt = [_tril(ws_ref[h]).astype(BF16) for h in range(sgu // HEAD_DIM)]
        for n in range(tm // CHUNK):
            rows = slice(n * CHUNK, (n + 1) * CHUNK)
            u = _gelu(z_main[rows, 0:sgu])
            vhat, _ = _ln(_gelu(z_main[rows, sgu:2 * sgu]))
            vn = (vhat * lng + lnb).astype(BF16)
            for gp in range(sgu // LANES):
                ls = slice(gp * LANES, (gp + 1) * LANES)
                mixed = _head_pair_mix(wt[2 * gp], wt[2 * gp + 1], vn[:, ls]) + bs_ref[:, ls]
                cat_ref[rows, ls] = (u[:, ls] * mixed).astype(BF16)

        def glu(zz):
            return zz[:, 2 * sgu:3 * sgu] * jax.nn.sigmoid(zz[:, 3 * sgu:4 * sgu])

        ext = jnp.concatenate([glu(z_prev), glu(z_main)], axis=0)
        hc = _depthwise(_sub_tile_shifts(ext, _shift_down), cw_ref[...], tm, True) + cb_ref[...]
        hc_ref[...] = hc
        hhat, _ = _ln(hc)
        bn = hhat * clg_ref[...] + clb_ref[...]
        cat_ref[:, sgu:2 * sgu] = (bn * jax.nn.sigmoid(bn)).astype(BF16)

        pext = jnp.concatenate([z_prev[:, 4 * sgu:], z_main[:, 4 * sgu:]], axis=0)
        sums = _window_sums(pext, _shift_down)[HALO:]
        pooled = sums / _pool_count(i * tm, tm, pool) - z_main[:, 4 * sgu:]
        mixed_c = _dot(pooled.astype(BF16), bd_ref[...].astype(BF16))
        cat_ref[:, 2 * sgu:] = (mixed_c * ps_ref[...]).astype(BF16)

        xo_ref[...] = x_ref[...] + _dot(cat_ref[...], wo_ref[...])

    def vec(n):
        return pl.BlockSpec((1, n), lambda i: (0, 0))

    return _call(
        body, name, (n_i,),
        [pl.BlockSpec((tm, d), lambda i: (i, 0)),
         pl.BlockSpec((HALO, d), lambda i: (jnp.maximum(i * hb - 1, 0), 0)),
         vec(d),
         pl.BlockSpec((None, d_in, d), lambda i: (0, 0, 0)), pl.BlockSpec((None, d_mix, d), lambda i: (0, 0, 0)),
         vec(sgu), vec(sgu),
         pl.BlockSpec(p["w_spatial"].shape, lambda i: (0, 0, 0)),
         pl.BlockSpec((CHUNK, sgu), lambda i: (0, 0)),
         pl.BlockSpec((CONV_WIDTH, sgu), lambda i: (0, 0)),
         vec(sgu), vec(sgu), vec(sgu),
         pl.BlockSpec((pool, pool), lambda i: (0, 0)), vec(pool)],
        [pl.BlockSpec((tm, d), lambda i: (i, 0)), pl.BlockSpec((tm, d_in), lambda i: (i, 0)),
         pl.BlockSpec((tm, sgu), lambda i: (i, 0)), pl.BlockSpec((tm, d_mix), lambda i: (i, 0))],
        [jax.ShapeDtypeStruct((t, d), F32), jax.ShapeDtypeStruct((t, d_in), F32),
         jax.ShapeDtypeStruct((t, sgu), F32), jax.ShapeDtypeStruct((t, d_mix), BF16)], [], ("parallel",),
        (x, x, g, wb, wc, p["sgu_ln_g"], p["sgu_ln_b"], p["w_spatial"], p["bs_full"], p["conv_w"], p["conv_b"],
         p["conv_ln_g"], p["conv_ln_b"], p["bd"], p["pool_scale"]), payload)


_R_SGU_G, _R_SGU_B, _R_CONV_B, _R_CLN_G, _R_CLN_B, _R_CONV_W = 0, 1, 2, 3, 4, 8
_R384_ROWS = 40


def _mixer_bwd(x, g, z, hc_saved, dxo, wb, wc, p, name, payload=None):
    t, d_in = z.shape
    d = x.shape[1]
    sgu = p["sgu_ln_g"].shape[1]
    pool = p["pool_scale"].shape[1]
    d_mix = 2 * sgu + pool
    n_head = sgu // HEAD_DIM
    tm = 512
    n_i = t // tm
    hb = tm // HALO

    def body(x_ref, g_ref, wi_ref, z_ref, zp_ref, zn_ref, dxo_ref, dxon_ref, wo_ref, hc_ref, hcn_ref, lng_ref,
             lnb_ref, ws_ref, bs_ref, cw_ref, clg_ref, clb_ref, bd_ref, ps_ref,
             dx_ref, dz_ref, hm_ref, dgm_ref, g384_ref, gws_ref, gpool_ref, dbs_scr):
        i = pl.program_id(0)
        first, last = i == 0, i == n_i - 1

        @pl.when(first)
        def _():
            dgm_ref[...] = jnp.zeros_like(dgm_ref)
            g384_ref[...] = jnp.zeros_like(g384_ref)
            gws_ref[...] = jnp.zeros_like(gws_ref)
            gpool_ref[...] = jnp.zeros_like(gpool_ref)
            dbs_scr[...] = jnp.zeros_like(dbs_scr)

        z_main = z_ref[...]
        z_prev = jnp.where(first, 0.0, zp_ref[...])
        z_next = jnp.where(last, 0.0, zn_ref[...])
        wo = wo_ref[...]
        dc_main = _dot_nt(dxo_ref[...].astype(BF16), wo)
        dc_next = jnp.where(last, 0.0, _dot_nt(dxon_ref[...].astype(BF16), wo))

        lng, lnb = lng_ref[...], lnb_ref[...]
        wt = [_tril(ws_ref[h]) for h in range(n_head)]
        wt_b = [w.astype(BF16) for w in wt]
        wtt_b = [w.T.astype(BF16) for w in wt]
        lo = _lane_ids((CHUNK, LANES)) < HEAD_DIM
        d_lng = jnp.zeros((1, sgu), F32)
        d_lnb = jnp.zeros((1, sgu), F32)
        dws = [jnp.zeros((CHUNK, CHUNK), F32) for _ in range(n_head)]
        for n in range(tm // CHUNK):
            rows = slice(n * CHUNK, (n + 1) * CHUNK)
            au, av = z_main[rows, 0:sgu], z_main[rows, sgu:2 * sgu]
            u = _gelu(au)
            vhat, vrstd = _ln(_gelu(av))
            vn = (vhat * lng + lnb).astype(BF16)
            da = dc_main[rows, 0:sgu]
            dmixed = da * u
            dbs_scr[...] += dmixed
            dvn_parts, du_parts = [], []
            for gp in range(sgu // LANES):
                ls = slice(gp * LANES, (gp + 1) * LANES)
                vn_g = vn[:, ls]
                mixed = _head_pair_mix(wt_b[2 * gp], wt_b[2 * gp + 1], vn_g) + bs_ref[:, ls]
                du_parts.append(da[:, ls] * mixed)
                dm_g = dmixed[:, ls]
                dm_b = dm_g.astype(BF16)
                dvn_parts.append(jnp.where(lo, _dot(wtt_b[2 * gp], dm_b), _dot(wtt_b[2 * gp + 1], dm_b)))
                dws[2 * gp] = dws[2 * gp] + _dot_nt(jnp.where(lo, dm_g, 0.0).astype(BF16), vn_g)
                dws[2 * gp + 1] = dws[2 * gp + 1] + _dot_nt(jnp.where(lo, 0.0, dm_g).astype(BF16), vn_g)
            dvn = jnp.concatenate(dvn_parts, axis=1)
            du = jnp.concatenate(du_parts, axis=1)
            dv, dg_n, db_n = _ln_bwd(vhat, vrstd, lng, dvn)
            d_lng = d_lng + dg_n
            d_lnb = d_lnb + db_n
            dz_ref[rows, 0:sgu] = (du * _gelu_grad(au)).astype(BF16)
            dz_ref[rows, sgu:2 * sgu] = (dv * _gelu_grad(av)).astype(BF16)
        for h in range(n_head):
            gws_ref[h] += _tril(dws[h])
        g384_ref[_R_SGU_G:_R_SGU_G + 1, :] += d_lng
        g384_ref[_R_SGU_B:_R_SGU_B + 1, :] += d_lnb

        clg = clg_ref[...]
        bcols = slice(2 * sgu, 4 * sgu)
        zb = jnp.concatenate([z_prev[:, bcols], z_main[:, bcols], z_next[:, bcols]], axis=0)
        bval, bgate = zb[:, 0:sgu], zb[:, sgu:2 * sgu]
        sg = jax.nn.sigmoid(bgate)
        hglu = bval * sg
        n_out = tm + HALO
        hglu_shifts = _sub_tile_shifts(hglu, _shift_down)
        cw = cw_ref[...]
        hc = jnp.concatenate([hc_ref[...], jnp.where(last, 0.0, hcn_ref[...])], axis=0)
        hhat, hrstd = _ln(hc)
        bn = hhat * clg + clb_ref[...]
        db = jnp.concatenate([dc_main[:, sgu:2 * sgu], dc_next[:, sgu:2 * sgu]], axis=0)
        dbn = db * _silu_grad(bn)
        dhc_all, _, _ = _ln_bwd(hhat, hrstd, clg, dbn)
        dbn_m, hhat_m, dhc = dbn[:tm], hhat[:tm], dhc_all[:tm]
        g384_ref[_R_CLN_G:_R_CLN_G + 1, :] += jnp.sum(dbn_m * hhat_m, axis=0, keepdims=True)
        g384_ref[_R_CLN_B:_R_CLN_B + 1, :] += jnp.sum(dbn_m, axis=0, keepdims=True)
        g384_ref[_R_CONV_B:_R_CONV_B + 1, :] += jnp.sum(dhc, axis=0, keepdims=True)
        wrows = _conv_wgrad(hglu_shifts, dhc, tm)
        for k in range(CONV_WIDTH):
            g384_ref[_R_CONV_W + k:_R_CONV_W + k + 1, :] += wrows[k]
        dhglu = _depthwise(_sub_tile_shifts(dhc_all, _shift_up), cw, tm, False)
        bval_m, sg_m = bval[HALO:HALO + tm], sg[HALO:HALO + tm]
        dz_ref[:, 2 * sgu:3 * sgu] = (dhglu * sg_m).astype(BF16)
        dz_ref[:, 3 * sgu:4 * sgu] = (dhglu * bval_m * sg_m * (1.0 - sg_m)).astype(BF16)

        bd_b = bd_ref[...].astype(BF16)
        ps = ps_ref[...]
        p_main = z_main[:, 4 * sgu:]
        pext = jnp.concatenate([z_prev[:, 4 * sgu:], p_main], axis=0)
        cnt = _pool_count(i * tm, n_out, pool)
        pooled = _window_sums(pext, _shift_down)[HALO:] / cnt[:tm] - p_main
        pooled_b = pooled.astype(BF16)
        dcc = jnp.concatenate([dc_main[:, 2 * sgu:], dc_next[:, 2 * sgu:]], axis=0)
        dmix_c = dcc * ps
        mixed_c = _dot(pooled_b, bd_b)
        grp_r = lax.broadcasted_iota(jnp.int32, (pool, pool), 0) // HEAD_DIM
        grp_c = lax.broadcasted_iota(jnp.int32, (pool, pool), 1) // HEAD_DIM
        gpool_ref[0:pool, :] += jnp.where(grp_r == grp_c, _dot_tn(pooled_b, dmix_c[:tm].astype(BF16)), 0.0)
        gpool_ref[pool:pool + 1, :] += jnp.sum(dcc[:tm] * mixed_c, axis=0, keepdims=True)
        dpooled = _dot_nt(dmix_c.astype(BF16), bd_b)
        q = dpooled / cnt
        dp = _window_sums(q, _shift_up)[:tm] - dpooled[:tm]
        dz_ref[:, 4 * sgu:] = dp.astype(BF16)

        gain = g_ref[...]
        xhat, rstd = _rms(x_ref[...])
        hm_ref[...] = (xhat * gain).astype(BF16)
        dxn, dgm = _rms_bwd(xhat, rstd, gain, _dot(dz_ref[...], wi_ref[...]))
        dx_ref[...] = dxo_ref[...] + dxn
        dgm_ref[...] += dgm

        @pl.when(last)
        def _():
            r = lax.broadcasted_iota(jnp.int32, (sgu, LANES), 0)
            c = lax.broadcasted_iota(jnp.int32, (sgu, LANES), 1)
            sel = (r // HEAD_DIM == c).astype(BF16)
            gws_ref[n_head] = _split_dot(dbs_scr[...], sel)

    def vec(n):
        return pl.BlockSpec((1, n), lambda i: (0, 0))

    def prev_map(i):
        return (jnp.maximum(i * hb - 1, 0), 0)

    def next_map(i):
        return (jnp.minimum((i + 1) * hb, n_i * hb - 1), 0)

    return _call(
        body, name, (n_i,),
        [pl.BlockSpec((tm, d), lambda i: (i, 0)), pl.BlockSpec((1, d), lambda i: (0, 0)),
         pl.BlockSpec((None, d_in, d), lambda i: (0, 0, 0)),
         pl.BlockSpec((tm, d_in), lambda i: (i, 0)),
         pl.BlockSpec((HALO, d_in), prev_map), pl.BlockSpec((HALO, d_in), next_map),
         pl.BlockSpec((tm, d), lambda i: (i, 0)), pl.BlockSpec((HALO, d), next_map),
         pl.BlockSpec((None, d_mix, d), lambda i: (0, 0, 0)),
         pl.BlockSpec((tm, sgu), lambda i: (i, 0)), pl.BlockSpec((HALO, sgu), next_map),
         vec(sgu), vec(sgu),
         pl.BlockSpec(p["w_spatial"].shape, lambda i: (0, 0, 0)),
         pl.BlockSpec((CHUNK, sgu), lambda i: (0, 0)),
         pl.BlockSpec((CONV_WIDTH, sgu), lambda i: (0, 0)),
         vec(sgu), vec(sgu),
         pl.BlockSpec((pool, pool), lambda i: (0, 0)), vec(pool)],
        [pl.BlockSpec((tm, d), lambda i: (i, 0)), pl.BlockSpec((tm, d_in), lambda i: (i, 0)),
         pl.BlockSpec((tm, d), lambda i: (i, 0)), pl.BlockSpec((1, d), lambda i: (0, 0)),
         pl.BlockSpec((_R384_ROWS, sgu), lambda i: (0, 0)),
         pl.BlockSpec((n_head + 1, CHUNK, CHUNK), lambda i: (0, 0, 0)),
         pl.BlockSpec((pool + 8, pool), lambda i: (0, 0))],
        [jax.ShapeDtypeStruct((t, d), F32), jax.ShapeDtypeStruct((t, d_in), BF16),
         jax.ShapeDtypeStruct((t, d), BF16), jax.ShapeDtypeStruct((1, d), F32),
         jax.ShapeDtypeStruct((_R384_ROWS, sgu), F32),
         jax.ShapeDtypeStruct((n_head + 1, CHUNK, CHUNK), F32),
         jax.ShapeDtypeStruct((pool + 8, pool), F32)],
        [pltpu.VMEM((CHUNK, sgu), F32)], ("arbitrary",),
        (x, g, wb, z, z, z, dxo, dxo, wc, hc_saved, hc_saved, p["sgu_ln_g"], p["sgu_ln_b"], p["w_spatial"],
         p["bs_full"], p["conv_w"], p["conv_ln_g"], p["conv_ln_b"], p["bd"], p["pool_scale"]), payload)


def _loss_head(x, g, target, name):
    t, d = x.shape
    tm = 512

    def body(x_ref, g_ref, tgt_ref, dx_ref, dg_ref, loss_ref):
        i = pl.program_id(0)

        @pl.when(i == 0)
        def _():
            dg_ref[...] = jnp.zeros_like(dg_ref)
            loss_ref[...] = jnp.zeros_like(loss_ref)

        gv = g_ref[...]
        xhat, rstd = _rms(x_ref[...])
        err = xhat * gv - tgt_ref[...]
        loss_ref[...] += jnp.zeros_like(loss_ref) + 0.5 * jnp.sum(jnp.mean(err * err, axis=-1, keepdims=True))
        dxn, dg = _rms_bwd(xhat, rstd, gv, err * (1.0 / d))
        dx_ref[...] = dxn
        dg_ref[...] += dg

    row = pl.BlockSpec((tm, d), lambda i: (i, 0))
    vec = pl.BlockSpec((1, d), lambda i: (0, 0))
    return pl.pallas_call(
        body, name=name, grid=(t // tm,),
        in_specs=[row, vec, row],
        out_specs=[row, vec, pl.BlockSpec((1, LANES), lambda i: (0, 0))],
        out_shape=[jax.ShapeDtypeStruct((t, d), F32), jax.ShapeDtypeStruct((1, d), F32),
                   jax.ShapeDtypeStruct((1, LANES), F32)],
        compiler_params=_cparams(("arbitrary",)),
    )(x, g, target)


def _all_gather(arrs, name, extra=None):
    gather = _GatherIci(arrs)
    n = len(arrs)
    forward = _GatherForward([jax.ShapeDtypeStruct(s.shape, s.dtype) for s in gather.out_shapes])
    x_in = len(extra.ins) if extra else 0
    x_out = len(extra.out_shapes) if extra else 0

    def body(*refs):
        ins, x_ins = refs[:n], refs[n:n + x_in]
        outs, x_outs = refs[n + x_in:2 * n + x_in], refs[2 * n + x_in:2 * n + x_in + x_out]
        sems = refs[2 * n + x_in + x_out:]
        first = gather.build(ins, outs, *sems[0:3])
        first.start()
        if extra:
            beside = extra.build(x_ins, x_outs, *sems[6:9])
            beside.start()
        first.wait()
        second = forward.build(outs, outs, *sems[3:6])
        second.start()
        second.wait()
        if extra:
            beside.wait()

    outs = pl.pallas_call(
        body, name=name,
        in_specs=[ANY] * (n + x_in), out_specs=[ANY] * (n + x_out),
        out_shape=list(gather.out_shapes) + (list(extra.out_shapes) if extra else []),
        scratch_shapes=gather.sem_shapes() + forward.sem_shapes() + (extra.sem_shapes() if extra else []),
    )(*arrs, *(extra.ins if extra else []))
    return list(outs[:n]), list(outs[n:])


def _all_gather_relayed(arrs, name):
    n = len(arrs)
    n_pairs = 8

    def body(*refs):
        ins, outs = refs[:n], refs[n:2 * n]
        send_sems, recv_sems, local_sems = refs[2 * n:]
        x, y, c = _position()
        sib, xn, yn = (x, y, 1 - c), (1 - x, y, c), (x, 1 - y, c)

        def rows(a, px, py, pc, half=None):
            r = ins[a].shape[1]
            base = (4 * px + 2 * py + pc) * r
            if half is None:
                return outs[a].at[:, pl.ds(base, r), :]
            return outs[a].at[:, pl.ds(base + half * (r // 2), r // 2), :]

        def send(a, k, src, dst, to):
            return _remote(src, dst, send_sems, recv_sems, a * n_pairs + k, to)

        def arrived(a, k, land, sender):
            _remote(land, land, send_sems, recv_sems, a * n_pairs + k, sender).wait_recv()

        own = [pltpu.make_async_copy(ins[a], rows(a, x, y, c), local_sems.at[a]) for a in range(n)]
        first = [send(a, k, ins[a], rows(a, x, y, c), to) for a in range(n) for k, to in enumerate((sib, xn, yn))]
        for cp in own + first:
            cp.start()
        for a in range(n):
            arrived(a, 1, rows(a, *xn), xn)
            arrived(a, 2, rows(a, *yn), yn)
        second = []
        for a in range(n):
            second += [send(a, 3, rows(a, *xn, half=0), rows(a, *xn, half=0), yn),
                       send(a, 4, rows(a, *yn, half=1), rows(a, *yn, half=1), xn),
                       send(a, 5, rows(a, *xn), rows(a, *xn), sib),
                       send(a, 6, rows(a, *yn), rows(a, *yn), sib)]
        for cp in second:
            cp.start()
        for a in range(n):
            arrived(a, 3, rows(a, 1 - x, 1 - y, c, half=0), yn)
            arrived(a, 4, rows(a, 1 - x, 1 - y, c, half=1), xn)
        third = [send(a, 7, rows(a, 1 - x, 1 - y, c), rows(a, 1 - x, 1 - y, c), sib) for a in range(n)]
        for cp in third:
            cp.start()
        for a in range(n):
            arrived(a, 0, rows(a, *sib), sib)
            arrived(a, 5, rows(a, 1 - x, y, 1 - c), sib)
            arrived(a, 6, rows(a, x, 1 - y, 1 - c), sib)
            arrived(a, 7, rows(a, 1 - x, 1 - y, 1 - c), sib)
        for cp in first + second + third:
            cp.wait_send()
        for cp in own:
            cp.wait()

    return list(pl.pallas_call(
        body, name=name, in_specs=[ANY] * n, out_specs=[ANY] * n,
        out_shape=[jax.ShapeDtypeStruct((a.shape[0], N_DEV * a.shape[1], a.shape[2]), a.dtype) for a in arrs],
        scratch_shapes=[pltpu.SemaphoreType.DMA((n_pairs * n,)), pltpu.SemaphoreType.DMA((n_pairs * n,)),
                        pltpu.SemaphoreType.DMA((n,))],
    )(*arrs))


def _pair_sums(grads, recvs, cidx, name):
    n = len(grads)

    def body(c_ref, *refs):
        for g_ref, r_ref, o_ref in zip(refs[:n], refs[n:2 * n], refs[2 * n:]):
            o_ref[...] = (g_ref[...].astype(F32) + r_ref[...].astype(F32)).astype(BF16)

    shapes = [(g.shape[0] // N_DEV, g.shape[1]) for g in grads]
    return list(pl.pallas_call(
        body, name=name,
        grid_spec=pltpu.PrefetchScalarGridSpec(
            num_scalar_prefetch=1, grid=(N_CHIP,),
            in_specs=[pl.BlockSpec(s, lambda q, c: (2 * q + c[0], 0)) for s in shapes]
            + [pl.BlockSpec(s, lambda q, c: (q, 0)) for s in shapes],
            out_specs=[pl.BlockSpec(s, lambda q, c: (q, 0)) for s in shapes]),
        out_shape=[jax.ShapeDtypeStruct((N_CHIP * r, cols), BF16) for r, cols in shapes],
        compiler_params=_cparams(("parallel",)),
    )(cidx, *grads, *recvs))


def _sum_blocks(parts, nblk, name):
    r = parts.shape[0] // nblk
    cols = parts.shape[1]

    def body(p_ref, o_ref):
        acc = p_ref[0:r, :].astype(F32)
        for q in range(1, nblk):
            acc = acc + p_ref[q * r:(q + 1) * r, :].astype(F32)
        o_ref[...] = acc

    return pl.pallas_call(
        body, name=name,
        out_shape=jax.ShapeDtypeStruct((r, cols), F32),
        compiler_params=_cparams(),
    )(parts)


def _adamw_math(w, g, m, v):
    m = ADAM_B1 * m + (1.0 - ADAM_B1) * g
    v = ADAM_B2 * v + (1.0 - ADAM_B2) * (g * g)
    m_hat = m / (1.0 - ADAM_B1 ** ADAM_STEP)
    v_hat = v / (1.0 - ADAM_B2 ** ADAM_STEP)
    delta = -ADAM_LR * (m_hat / (jnp.sqrt(v_hat) + ADAM_EPS) + ADAM_WD * w)
    return delta, m, v


def _finish_sharded(parts, w, m, v, name):
    depth, rr, cw = w.shape

    def body(*refs):
        p_refs = refs[:depth]
        w_ref, m_ref, v_ref, g_ref, d_ref, mo_ref, vo_ref = refs[depth:]
        l = pl.program_id(0)
        for k in range(depth):
            @pl.when(l == k)
            def _(p_ref=p_refs[k]):
                r = p_ref.shape[0] // N_CHIP
                acc = p_ref[0:r, :].astype(F32)
                for q in range(1, N_CHIP):
                    acc = acc + p_ref[q * r:(q + 1) * r, :].astype(F32)
                g_ref[...] = acc
                d_ref[...], mo_ref[...], vo_ref[...] = _adamw_math(w_ref[...], acc, m_ref[...], v_ref[...])

    blk = pl.BlockSpec((None, rr, cw), lambda l: (l, 0, 0))
    return pl.pallas_call(
        body, name=name, grid=(depth,),
        in_specs=[pl.BlockSpec(p.shape, lambda l: (0, 0)) for p in parts] + [blk] * 3, out_specs=[blk] * 4,
        out_shape=[jax.ShapeDtypeStruct(w.shape, F32)] * 4,
        compiler_params=_cparams(("arbitrary",)),
    )(*parts, w, m, v)


def _adamw_small(ws, gs, ms, vs, name):
    n = len(ws)

    def body(*refs):
        for k in range(n):
            w_ref, g_ref, m_ref, v_ref = (refs[j * n + k] for j in range(4))
            d_ref, mo_ref, vo_ref = (refs[(4 + j) * n + k] for j in range(3))
            d_ref[...], mo_ref[...], vo_ref[...] = _adamw_math(w_ref[...], g_ref[...], m_ref[...], v_ref[...])

    shapes = [jax.ShapeDtypeStruct(w.shape, F32) for w in ws]
    return pl.pallas_call(
        body, name=name, out_shape=shapes * 3, compiler_params=_cparams(),
    )(*ws, *gs, *ms, *vs)


def kernel(x, ffn1_norm, ffn1_w_gate, ffn1_w_up, ffn1_w_down, mix_norm, w_in, sgu_ln_g, sgu_ln_b, w_spatial, b_spatial, conv_w, conv_b, conv_ln_g, conv_ln_b, pool_w, pool_scale, w_out, ffn2_norm, ffn2_w_gate, ffn2_w_up, ffn2_w_down, final_norm, loss_target, m_ffn1_norm, m_ffn1_w_gate, m_ffn1_w_up, m_ffn1_w_down, m_mix_norm, m_w_in, m_sgu_ln_g, m_sgu_ln_b, m_w_spatial, m_b_spatial, m_conv_w, m_conv_b, m_conv_ln_g, m_conv_ln_b, m_pool_w, m_pool_scale, m_w_out, m_ffn2_norm, m_ffn2_w_gate, m_ffn2_w_up, m_ffn2_w_down, m_final_norm, v_ffn1_norm, v_ffn1_w_gate, v_ffn1_w_up, v_ffn1_w_down, v_mix_norm, v_w_in, v_sgu_ln_g, v_sgu_ln_b, v_w_spatial, v_b_spatial, v_conv_w, v_conv_b, v_conv_ln_g, v_conv_ln_b, v_pool_w, v_pool_scale, v_w_out, v_ffn2_norm, v_ffn2_w_gate, v_ffn2_w_up, v_ffn2_w_down, v_final_norm):
    names = ["ffn1_norm", "ffn1_w_gate", "ffn1_w_up", "ffn1_w_down", "mix_norm", "w_in", "sgu_ln_g", "sgu_ln_b",
             "w_spatial", "b_spatial", "conv_w", "conv_b", "conv_ln_g", "conv_ln_b", "pool_w", "pool_scale",
             "w_out", "ffn2_norm", "ffn2_w_gate", "ffn2_w_up", "ffn2_w_down", "final_norm"]
    W = dict(zip(names, [ffn1_norm, ffn1_w_gate, ffn1_w_up, ffn1_w_down, mix_norm, w_in, sgu_ln_g, sgu_ln_b,
                         w_spatial, b_spatial, conv_w, conv_b, conv_ln_g, conv_ln_b, pool_w, pool_scale, w_out,
                         ffn2_norm, ffn2_w_gate, ffn2_w_up, ffn2_w_down, final_norm]))
    M = dict(zip(names, [m_ffn1_norm, m_ffn1_w_gate, m_ffn1_w_up, m_ffn1_w_down, m_mix_norm, m_w_in, m_sgu_ln_g,
                         m_sgu_ln_b, m_w_spatial, m_b_spatial, m_conv_w, m_conv_b, m_conv_ln_g, m_conv_ln_b,
                         m_pool_w, m_pool_scale, m_w_out, m_ffn2_norm, m_ffn2_w_gate, m_ffn2_w_up, m_ffn2_w_down,
                         m_final_norm]))
    V = dict(zip(names, [v_ffn1_norm, v_ffn1_w_gate, v_ffn1_w_up, v_ffn1_w_down, v_mix_norm, v_w_in, v_sgu_ln_g,
                         v_sgu_ln_b, v_w_spatial, v_b_spatial, v_conv_w, v_conv_b, v_conv_ln_g, v_conv_ln_b,
                         v_pool_w, v_pool_scale, v_w_out, v_ffn2_norm, v_ffn2_w_gate, v_ffn2_w_up, v_ffn2_w_down,
                         v_final_norm]))

    depth, d = ffn1_norm.shape
    t = x.shape[1]
    sgu = sgu_ln_g.shape[1]
    pool = pool_scale.shape[1]
    n_head = sgu // HEAD_DIM
    cw_shard = conv_w.shape[2]
    xs = x.reshape(t, d)
    target = loss_target.reshape(t, d)

    def tr(w):
        return jnp.swapaxes(w, 1, 2).astype(BF16)

    ffn_shards = [[jnp.stack([tr(ffn1_w_gate)[l], tr(ffn1_w_up)[l], ffn1_w_down[l].astype(BF16)]),
                   jnp.stack([tr(ffn2_w_gate)[l], tr(ffn2_w_up)[l], ffn2_w_down[l].astype(BF16)])]
                  for l in range(depth)]
    win_shards = [tr(w_in)[l:l + 1] for l in range(depth)]
    wout_shards = [w_out[l:l + 1].astype(BF16) for l in range(depth)]
    cw_rows = depth * CONV_WIDTH
    cw_pad = -cw_rows % 8
    cw_send = jnp.pad(conv_w.reshape(cw_rows, cw_shard), ((0, cw_pad), (0, 0)))[None]
    wffn, wb, wc = {}, {}, {}
    wffn[(0, 0)], wb[0], wc[0], cwg = _all_gather_relayed(
        [ffn_shards[0][0], win_shards[0], wout_shards[0], cw_send], "ag_first")
    conv_w_full = cwg.reshape(N_DEV, cw_rows + cw_pad, cw_shard)[:, :cw_rows].reshape(
        N_DEV, depth, CONV_WIDTH, cw_shard).transpose(1, 2, 0, 3).reshape(depth, CONV_WIDTH, N_DEV * cw_shard)

    def mixer_params(l):
        return dict(
            sgu_ln_g=sgu_ln_g[l:l + 1], sgu_ln_b=sgu_ln_b[l:l + 1], w_spatial=w_spatial[l],
            bs_full=jnp.repeat(b_spatial[l].T, HEAD_DIM, axis=1),
            conv_w=conv_w_full[l], conv_b=conv_b[l:l + 1], conv_ln_g=conv_ln_g[l:l + 1],
            conv_ln_b=conv_ln_b[l:l + 1], bd=_block_diag(pool_w[l]), pool_scale=pool_scale[l:l + 1])

    saved = []
    cur = xs
    for l in range(depth):
        p = mixer_params(l)
        x0 = cur
        more = l + 1 < depth
        (x1, gate1, up1, act1), part = _ffn_fwd(x0, ffn1_norm[l:l + 1], wffn[(l, 0)], 0, f"ffn1_fwd_{l}",
                                          _GatherIci([ffn_shards[l][1]]))
        riding = [_GatherForward(part)] + ([_GatherIci([win_shards[l + 1], wout_shards[l + 1]])] if more else [])
        (x2, z, hc, cat), part = _mixer_fwd(x1, mix_norm[l:l + 1], wb[l], wc[l], p, f"mixer_fwd_{l}", _Merged(riding))
        wffn[(l, 1)] = part[0]
        riding = [_GatherIci([ffn_shards[l + 1][0]]), _GatherForward(part[1:])] if more else []
        (x3, gate2, up2, act2), part = _ffn_fwd(x2, ffn2_norm[l:l + 1], wffn[(l, 1)], 0, f"ffn2_fwd_{l}",
                                          _Merged(riding) if more else None)
        if more:
            wb[l + 1], wc[l + 1] = part[1:]
            (wffn[(l + 1, 0)],) = _comm(_GatherForward(part[:1]), f"ag_forward_{l + 1}")
        saved.append((p, x0, gate1, up1, act1, x1, z, hc, cat, x2, gate2, up2, act2))
        cur = x3

    dx, d_final, loss_part = _loss_head(cur, final_norm.reshape(1, d), target, "loss_head")

    cidx = lax.axis_index("c").astype(jnp.int32).reshape(1)
    from_chips = {}
    to_pair, to_chip = [], []
    small = []

    def pair_payload():
        return _PairExchange([g for _, g in to_pair]) if to_pair else None

    def pair_done(received):
        if to_pair:
            (nm, l), _ = to_pair[0]
            sums = _pair_sums([g for _, g in to_pair], list(received), cidx, f"rs_pair_sum_{nm}_{l}")
            to_chip.extend((key, s) for (key, _), s in zip(to_pair, sums))
        to_pair.clear()

    def take_chip():
        items = list(to_chip)
        to_chip.clear()
        return items

    def chip_payload(items):
        return _ChipExchange([s for _, s in items]) if items else None

    def chip_done(items, landed):
        for (key, _), o in zip(items, landed):
            from_chips[key] = o

    def ffn_weight_grads(prefix, l, dgate, dup, act, h, dy):
        items = take_chip()
        items, later = items[:2], items[2:]
        to_chip.extend(later)
        grads3, landed = _ffn_dw(dgate, dup, act, h, dy, f"dw_{prefix}_{l}", chip_payload(items))
        chip_done(items, landed)
        to_pair.extend(((f"{prefix}_{nm}", l), g) for nm, g in zip(("w_gate", "w_up", "w_down"), grads3))

    for l in reversed(range(depth)):
        p, x0, gate1, up1, act1, x1, z, hc, cat, x2, gate2, up2, act2 = saved[l]
        (dx, dgate, dup, h, dy, dg_ffn2), received = _ffn_bwd(
            x2, ffn2_norm[l:l + 1], dx, gate2, up2, wffn[(l, 1)], 0, f"ffn2_bwd_{l}", pair_payload())
        pair_done(received)
        ffn_weight_grads("ffn2", l, dgate, dup, act2, h, dy)
        g_out, received = _tn_matmul(cat, dx, f"dw_w_out_{l}", pair_payload())
        pair_done(received)
        items = take_chip()
        items, later = items[:3], items[3:]
        to_chip.extend(later)
        (dx, dz, hm, dg_mix, g384, gws, gpool), landed = _mixer_bwd(
            x1, mix_norm[l:l + 1], z, hc, dx, wb[l], wc[l], p, f"mixer_bwd_{l}", chip_payload(items))
        chip_done(items, landed)
        g_in, _ = _tn_matmul(dz, hm, f"dw_w_in_{l}")
        to_pair.extend([(("w_out", l), g_out), (("w_in", l), g_in)])
        if l > 0:
            (dx, dgate, dup, h, dy, dg_ffn1), received = _ffn_bwd(
                x0, ffn1_norm[l:l + 1], dx, gate1, up1, wffn[(l, 0)], 0, f"ffn1_bwd_{l}", pair_payload())
            pair_done(received)
            ffn_weight_grads("ffn1", l, dgate, dup, act1, h, dy)
            small.append((l, g384, gws, gpool, dg_ffn1, dg_mix, dg_ffn2))
            continue

        small.append((0, g384, gws, gpool, None, dg_mix, dg_ffn2))
        small.sort(key=lambda s: s[0])
        norm_rows = []
        for (sl, _, _, _, dg1, dgm, dg2) in small:
            norm_rows += [jnp.zeros((1, d), F32) if dg1 is None else dg1, dgm, dg2]
        norm_rows += [d_final, jnp.pad(loss_part, ((0, 0), (0, d - LANES)))]
        n_norm = len(norm_rows)
        norm_pack = jnp.concatenate(norm_rows + [jnp.zeros((8 - n_norm % 8, d), F32)] * (n_norm % 8 != 0), axis=0)
        parts = [norm_pack]
        for (_, s384, sws, spool, _, _, _) in small:
            parts += [s384, sws.reshape((n_head + 1) * CHUNK, CHUNK), spool]
        n_pair = len(to_pair)
        early = take_chip()
        riding = [pair_payload(), _GatherIci([a[None] for a in parts])] + ([chip_payload(early)] if early else [])
        (dx, dgate, dup, h, dy, dg_ffn1), landed = _ffn_bwd(
            x0, ffn1_norm[l:l + 1], dx, gate1, up1, wffn[(l, 0)], 0, f"ffn1_bwd_{l}", _Merged(riding))
        pair_done(landed[:n_pair])
        chip_done(early, landed[n_pair + len(parts):])
        items = take_chip()
        g_gate, landed = _tn_matmul(
            dgate, h, f"dw_ffn1_w_gate_{l}",
            _Merged([chip_payload(items), _GatherForward(landed[n_pair:n_pair + len(parts)])]))
        chip_done(items, landed[:len(items)])
        gathered = landed[len(items):]
        to_pair.append((("ffn1_w_gate", l), g_gate))
        g_up, received = _tn_matmul(dup, h, f"dw_ffn1_w_up_{l}", pair_payload())
        pair_done(received)
        to_pair.append((("ffn1_w_up", l), g_up))
        items = take_chip()
        g_down, landed = _tn_matmul(act1, dy, f"dw_ffn1_w_down_{l}", _Merged([chip_payload(items), pair_payload()]))
        chip_done(items, landed[:len(items)])
        pair_done(landed[len(items):])
        to_pair.append((("ffn1_w_down", l), g_down))
    grad_x = dx.reshape(x.shape)
    pair_done(_comm(pair_payload(), "rs_pair_exchange_last"))
    items = take_chip()
    (late_norm,), landed = _all_gather([jnp.pad(dg_ffn1, ((0, 7), (0, 0)))[None]], "ag_tail", chip_payload(items))
    chip_done(items, landed)

    summed = [_sum_blocks(g[0], N_DEV, f"sum_small_{k}") for k, g in enumerate(gathered)]
    late_sum = _sum_blocks(late_norm[0], N_DEV, "sum_small_late")
    norm_sum = summed[0]
    loss = norm_sum[3 * depth + 1, 0]
    cpos = lax.axis_index("x") * 4 + lax.axis_index("y") * 2 + lax.axis_index("c")
    sg = {nm: [] for nm in names}
    for l in range(depth):
        g384, gws, gpool = summed[1 + 3 * l], summed[2 + 3 * l].reshape(n_head + 1, CHUNK, CHUNK), summed[3 + 3 * l]
        sg["ffn1_norm"].append(norm_sum[3 * l] if l > 0 else late_sum[0])
        sg["mix_norm"].append(norm_sum[3 * l + 1])
        sg["ffn2_norm"].append(norm_sum[3 * l + 2])
        sg["sgu_ln_g"].append(g384[_R_SGU_G])
        sg["sgu_ln_b"].append(g384[_R_SGU_B])
        sg["conv_b"].append(g384[_R_CONV_B])
        sg["conv_ln_g"].append(g384[_R_CLN_G])
        sg["conv_ln_b"].append(g384[_R_CLN_B])
        sg["conv_w"].append(lax.dynamic_slice_in_dim(g384[_R_CONV_W:_R_CONV_W + CONV_WIDTH], cpos * cw_shard,
                                                     cw_shard, axis=1))
        sg["w_spatial"].append(gws[:n_head])
        sg["b_spatial"].append(gws[n_head][:, :n_head].T)
        sg["pool_w"].append(jnp.stack([gpool[k * HEAD_DIM:(k + 1) * HEAD_DIM, k * HEAD_DIM:(k + 1) * HEAD_DIM]
                                       for k in range(pool // HEAD_DIM)], axis=0))
        sg["pool_scale"].append(gpool[pool])
    small_names = ["ffn1_norm", "mix_norm", "sgu_ln_g", "sgu_ln_b", "w_spatial", "b_spatial", "conv_w", "conv_b",
                   "conv_ln_g", "conv_ln_b", "pool_w", "pool_scale", "ffn2_norm"]
    grads = {nm: jnp.stack(sg[nm], axis=0) for nm in small_names}
    grads["final_norm"] = norm_sum[3 * depth]

    delta, new_m, new_v = {}, {}, {}
    big_names = ["ffn1_w_gate", "ffn1_w_up", "ffn1_w_down", "w_in", "w_out", "ffn2_w_gate", "ffn2_w_up",
                 "ffn2_w_down"]
    transposed = {"ffn1_w_gate", "ffn1_w_up", "w_in", "ffn2_w_gate", "ffn2_w_up"}
    for nm in big_names:
        view = (lambda a: jnp.swapaxes(a, 1, 2)) if nm in transposed else (lambda a: a)
        outs = _finish_sharded([from_chips[(nm, l)] for l in range(depth)], view(W[nm]), view(M[nm]), view(V[nm]),
                               f"adamw_{nm}")
        grads[nm], delta[nm], new_m[nm], new_v[nm] = (view(o) for o in outs)
    snames = small_names + ["final_norm"]

    def flat2(a):
        return a.reshape(-1, a.shape[-1])

    outs = _adamw_small([flat2(W[nm]) for nm in snames], [flat2(grads[nm]) for nm in snames],
                        [flat2(M[nm]) for nm in snames], [flat2(V[nm]) for nm in snames], "adamw_small")
    ns = len(snames)
    for k, nm in enumerate(snames):
        shp = W[nm].shape
        delta[nm], new_m[nm], new_v[nm] = (outs[k].reshape(shp), outs[ns + k].reshape(shp),
                                           outs[2 * ns + k].reshape(shp))

    return (loss, grad_x, *[grads[nm] for nm in names], *[delta[nm] for nm in names],
            *[new_m[nm] for nm in names], *[new_v[nm] for nm in names])
```

```python
import functools

import jax
import jax.numpy as jnp
from jax import lax
from jax.experimental import pallas as pl
from jax.experimental.pallas import tpu as pltpu

F32 = jnp.float32
BF16 = jnp.bfloat16
EPS = 1e-6
N_DEV = 8
N_CHIP = 4
MESH = pl.DeviceIdType.MESH
ANY = pl.BlockSpec(memory_space=pl.ANY)

VMEM_LIMIT_BYTES = 56 * 1024 * 1024
LANES = 128
HALO = 32
HEAD_DIM = 64
CHUNK = 128
CONV_WIDTH = 31
POOL_WINDOWS = (2, 4, 8, 16)

ADAM_LR = 0.001
ADAM_B1 = 0.9
ADAM_B2 = 0.999
ADAM_EPS = 1e-08
ADAM_WD = 0.01
ADAM_STEP = 10


def _cparams(sem=None):
    return pltpu.CompilerParams(dimension_semantics=sem, vmem_limit_bytes=VMEM_LIMIT_BYTES)


def _position():
    return lax.axis_index("x"), lax.axis_index("y"), lax.axis_index("c")


class _Copies:
    def __init__(self):
        self.local, self.sends, self.recvs = [], [], []

    def extend(self, other):
        self.local += other.local
        self.sends += other.sends
        self.recvs += other.recvs

    def start(self):
        for cp in self.local + self.sends:
            cp.start()

    def wait(self):
        for land, send_sems, recv_sems, k, peer in self.recvs:
            _remote(land, land, send_sems, recv_sems, k, peer).wait_recv()
        for cp in self.sends:
            cp.wait_send()
        for cp in self.local:
            cp.wait()


def _remote(src, dst, send_sems, recv_sems, k, to):
    return pltpu.make_async_remote_copy(src_ref=src, dst_ref=dst, send_sem=send_sems.at[k], recv_sem=recv_sems.at[k],
                                        device_id=to, device_id_type=MESH)


class _Payload:
    ins, out_shapes, aliases, n_remote, n_local = (), (), {}, 0, 0

    def sem_shapes(self):
        return [pltpu.SemaphoreType.DMA((max(self.n_remote, 1),)), pltpu.SemaphoreType.DMA((max(self.n_remote, 1),)),
                pltpu.SemaphoreType.DMA((max(self.n_local, 1),))]


class _GatherIci(_Payload):
    def __init__(self, shards):
        self.ins = list(shards)
        self.out_shapes = [jax.ShapeDtypeStruct((s.shape[0], N_DEV * s.shape[1], s.shape[2]), s.dtype) for s in shards]
        self.n_remote, self.n_local = 4 * len(shards), len(shards)

    def build(self, ins, outs, send_sems, recv_sems, local_sems, k0=0, l0=0):
        x, y, c = _position()
        peers = [(x, y, 1 - c), (1 - x, y, c), (x, 1 - y, c), (1 - x, 1 - y, c)]
        cps = _Copies()
        for a, (src, out) in enumerate(zip(ins, outs)):
            r = src.shape[1]

            def rows(px, py, pc, out=out, r=r):
                return out.at[:, pl.ds((4 * px + 2 * py + pc) * r, r), :]

            cps.local.append(pltpu.make_async_copy(src, rows(x, y, c), local_sems.at[l0 + a]))
            for k, peer in enumerate(peers):
                cps.sends.append(_remote(src, rows(x, y, c), send_sems, recv_sems, k0 + 4 * a + k, peer))
                cps.recvs.append((rows(*peer), send_sems, recv_sems, k0 + 4 * a + k, peer))
        return cps


class _GatherForward(_Payload):
    def __init__(self, partials):
        self.ins = list(partials)
        self.out_shapes = [jax.ShapeDtypeStruct(p.shape, p.dtype) for p in partials]
        self.aliases = {a: a for a in range(len(partials))}
        self.n_remote = 3 * len(partials)

    def build(self, ins, outs, send_sems, recv_sems, local_sems, k0=0, l0=0):
        x, y, c = _position()
        chips = [(1 - x, y), (x, 1 - y), (1 - x, 1 - y)]
        cps = _Copies()
        for a, out in enumerate(outs):
            r = out.shape[1] // N_DEV
            for k, (px, py) in enumerate(chips):
                mine = out.at[:, pl.ds((4 * px + 2 * py + c) * r, r), :]
                theirs = out.at[:, pl.ds((4 * px + 2 * py + 1 - c) * r, r), :]
                cps.sends.append(_remote(mine, mine, send_sems, recv_sems, k0 + 3 * a + k, (x, y, 1 - c)))
                cps.recvs.append((theirs, send_sems, recv_sems, k0 + 3 * a + k, (x, y, 1 - c)))
        return cps


class _PairExchange(_Payload):
    def __init__(self, grads):
        self.ins = list(grads)
        self.out_shapes = [jax.ShapeDtypeStruct((g.shape[0] // 2, g.shape[1]), g.dtype) for g in grads]
        self.n_remote = N_CHIP * len(grads)

    def build(self, ins, outs, send_sems, recv_sems, local_sems, k0=0, l0=0):
        x, y, c = _position()
        cps = _Copies()
        for a, (src, out) in enumerate(zip(ins, outs)):
            r = src.shape[0] // N_DEV
            for q in range(N_CHIP):
                land = out.at[pl.ds(q * r, r), :]
                cps.sends.append(_remote(src.at[pl.ds((2 * q + 1 - c) * r, r), :], land, send_sems, recv_sems,
                                         k0 + N_CHIP * a + q, (x, y, 1 - c)))
                cps.recvs.append((land, send_sems, recv_sems, k0 + N_CHIP * a + q, (x, y, 1 - c)))
        return cps


class _ChipExchange(_Payload):
    def __init__(self, sums):
        self.ins = list(sums)
        self.out_shapes = [jax.ShapeDtypeStruct(s.shape, s.dtype) for s in sums]
        self.n_remote, self.n_local = 3 * len(sums), len(sums)

    def build(self, ins, outs, send_sems, recv_sems, local_sems, k0=0, l0=0):
        x, y, c = _position()
        my_chip = 2 * x + y
        chips = [(1 - x, y), (x, 1 - y), (1 - x, 1 - y)]
        cps = _Copies()
        for a, (src, out) in enumerate(zip(ins, outs)):
            r = src.shape[0] // N_CHIP
            mine = out.at[pl.ds(my_chip * r, r), :]
            cps.local.append(pltpu.make_async_copy(src.at[pl.ds(my_chip * r, r), :], mine, local_sems.at[l0 + a]))
            for k, (px, py) in enumerate(chips):
                land = out.at[pl.ds((2 * px + py) * r, r), :]
                cps.sends.append(_remote(src.at[pl.ds((2 * px + py) * r, r), :], mine, send_sems, recv_sems,
                                         k0 + 3 * a + k, (px, py, c)))
                cps.recvs.append((land, send_sems, recv_sems, k0 + 3 * a + k, (px, py, c)))
        return cps


class _Merged(_Payload):
    def __init__(self, parts):
        self.parts = list(parts)
        self.ins = [a for p in parts for a in p.ins]
        self.out_shapes = [s for p in parts for s in p.out_shapes]
        self.aliases, self.offsets = {}, []
        i0 = o0 = k0 = l0 = 0
        for p in parts:
            self.offsets.append((i0, o0, k0, l0))
            self.aliases.update({i0 + i: o0 + o for i, o in p.aliases.items()})
            i0, o0, k0, l0 = i0 + len(p.ins), o0 + len(p.out_shapes), k0 + p.n_remote, l0 + p.n_local
        self.n_remote, self.n_local = k0, l0

    def build(self, ins, outs, send_sems, recv_sems, local_sems):
        cps = _Copies()
        for p, (i0, o0, k0, l0) in zip(self.parts, self.offsets):
            cps.extend(p.build(ins[i0:i0 + len(p.ins)], outs[o0:o0 + len(p.out_shapes)], send_sems, recv_sems,
                               local_sems, k0, l0))
        return cps


def _call(body, name, grid, in_specs, out_specs, out_shape, scratch_shapes, semantics, args, payload=None):
    if payload is None:
        outs = pl.pallas_call(body, name=name, grid=grid, in_specs=in_specs, out_specs=out_specs,
                              out_shape=out_shape, scratch_shapes=scratch_shapes,
                              compiler_params=_cparams(semantics))(*args)
        return list(outs), []
    n_in, n_out, n_scr = len(in_specs), len(out_specs), len(scratch_shapes)
    p_in, p_out = len(payload.ins), len(payload.out_shapes)

    def carried(*refs):
        ins, p_ins = refs[:n_in], refs[n_in:n_in + p_in]
        o0 = n_in + p_in
        outs, p_outs = refs[o0:o0 + n_out], refs[o0 + n_out:o0 + n_out + p_out]
        s0 = o0 + n_out + p_out
        scr, sems = refs[s0:s0 + n_scr], refs[s0 + n_scr:]
        ids = [pl.program_id(k) for k in range(len(grid))]
        at_first = functools.reduce(jnp.logical_and, [i == 0 for i in ids])
        at_last = functools.reduce(jnp.logical_and, [i == g - 1 for i, g in zip(ids, grid)])

        @pl.when(at_first)
        def _():
            payload.build(p_ins, p_outs, *sems).start()

        body(*ins, *outs, *scr)

        @pl.when(at_last)
        def _():
            payload.build(p_ins, p_outs, *sems).wait()

    outs = pl.pallas_call(
        carried, name=name, grid=grid, in_specs=list(in_specs) + [ANY] * p_in,
        out_specs=list(out_specs) + [ANY] * p_out, out_shape=list(out_shape) + list(payload.out_shapes),
        scratch_shapes=list(scratch_shapes) + payload.sem_shapes(),
        input_output_aliases={n_in + i: n_out + o for i, o in payload.aliases.items()},
        compiler_params=_cparams(("arbitrary",) * len(grid)))(*args, *payload.ins)
    return list(outs[:n_out]), list(outs[n_out:])


def _comm(payload, name):
    def body(*refs):
        p_in, p_out = len(payload.ins), len(payload.out_shapes)
        cps = payload.build(refs[:p_in], refs[p_in:p_in + p_out], *refs[p_in + p_out:])
        cps.start()
        cps.wait()

    return list(pl.pallas_call(
        body, name=name, in_specs=[ANY] * len(payload.ins), out_specs=[ANY] * len(payload.out_shapes),
        out_shape=list(payload.out_shapes), scratch_shapes=payload.sem_shapes(),
        input_output_aliases=dict(payload.aliases))(*payload.ins))


def _dot(a, b):
    return jnp.dot(a, b, preferred_element_type=F32)


def _dot_nt(a, b):
    return lax.dot_general(a, b, (((1,), (1,)), ((), ())), preferred_element_type=F32)


def _dot_tn(a, b):
    return lax.dot_general(a, b, (((0,), (0,)), ((), ())), preferred_element_type=F32)


def _split_dot(x, e):
    hi = x.astype(BF16)
    r1 = x - hi.astype(F32)
    mid = r1.astype(BF16)
    lo = (r1 - mid.astype(F32)).astype(BF16)
    return _dot(hi, e) + _dot(mid, e) + _dot(lo, e)


def _rms(x):
    rstd = lax.rsqrt(jnp.mean(x * x, axis=-1, keepdims=True) + EPS)
    return x * rstd, rstd


def _rms_bwd(xhat, rstd, g, dh):
    dxhat = dh * g
    dx = rstd * (dxhat - xhat * jnp.mean(dxhat * xhat, axis=-1, keepdims=True))
    return dx, jnp.sum(dh * xhat, axis=0, keepdims=True)


def _ln(v):
    mu = jnp.mean(v, axis=-1, keepdims=True)
    xc = v - mu
    rstd = lax.rsqrt(jnp.mean(xc * xc, axis=-1, keepdims=True) + EPS)
    return xc * rstd, rstd


def _ln_bwd(vhat, rstd, g, dy):
    dvhat = dy * g
    dv = rstd * (dvhat - jnp.mean(dvhat, axis=-1, keepdims=True)
                 - vhat * jnp.mean(dvhat * vhat, axis=-1, keepdims=True))
    return dv, jnp.sum(dy * vhat, axis=0, keepdims=True), jnp.sum(dy, axis=0, keepdims=True)


_INV_SQRT2 = 0.7071067811865476
_INV_SQRT2PI = 0.3989422804014327


def _gelu(x):
    return 0.5 * x * (1.0 + lax.erf(x * _INV_SQRT2))


def _gelu_grad(x):
    return 0.5 * (1.0 + lax.erf(x * _INV_SQRT2)) + x * jnp.exp(-0.5 * x * x) * _INV_SQRT2PI


def _silu_grad(x):
    s = jax.nn.sigmoid(x)
    return s * (1.0 + x * (1.0 - s))


def _ffn_fwd(x, g, wa, mi, name, payload=None):
    t, d = x.shape
    f = wa.shape[1]
    tm, tf = 1024, 256
    nc = f // tf
    groups = [slice(k * (tm // 2), (k + 1) * (tm // 2)) for k in range(2)]

    def body(x_ref, g_ref, wgu_ref, wd_ref, xo_ref, dgate_ref, dup_ref, act_ref, h_scr, acc_scr):
        c = pl.program_id(1)

        @pl.when(c == 0)
        def _():
            xhat, _ = _rms(x_ref[...])
            h_scr[...] = (xhat * g_ref[...]).astype(BF16)
            acc_scr[...] = jnp.zeros_like(acc_scr)

        wgu, wd = wgu_ref[...].reshape(2 * tf, d), wd_ref[...]
        for rows in groups:
            gu = _dot_nt(h_scr[rows, :], wgu)
            gate, up = gu[:, :tf], gu[:, tf:]
            s = jax.nn.sigmoid(gate)
            silu = gate * s
            dgate_ref[rows, :] = (up * (s * (1.0 + gate * (1.0 - s)))).astype(BF16)
            dup_ref[rows, :] = silu.astype(BF16)
            act = (silu * up).astype(BF16)
            act_ref[rows, :] = act
            acc_scr[rows, :] += _dot(act, wd)

        @pl.when(c == nc - 1)
        def _():
            xo_ref[...] = x_ref[...] + 0.5 * acc_scr[...]

    assert mi % 2 == 0
    return _call(
        body, name, (t // tm, nc),
        [pl.BlockSpec((tm, d), lambda i, c: (i, 0)), pl.BlockSpec((1, d), lambda i, c: (0, 0)),
         pl.BlockSpec((2, tf, d), lambda i, c: (mi // 2, c, 0)),
         pl.BlockSpec((None, tf, d), lambda i, c: (mi + 2, c, 0))],
        [pl.BlockSpec((tm, d), lambda i, c: (i, 0))] + [pl.BlockSpec((tm, tf), lambda i, c: (i, c))] * 3,
        [jax.ShapeDtypeStruct((t, d), F32)] + [jax.ShapeDtypeStruct((t, f), BF16)] * 3,
        [pltpu.VMEM((tm, d), BF16), pltpu.VMEM((tm, d), F32)],
        ("parallel", "arbitrary"), (x, g, wa, wa), payload)


def _ffn_bwd(x, g, dxo, gate, up, wa, mi, name, payload=None):
    t, d = x.shape
    f = wa.shape[1]
    tm, tf = 1024, 256
    nc = f // tf
    groups = [slice(k * (tm // 2), (k + 1) * (tm // 2)) for k in range(2)]

    def body(x_ref, g_ref, dxo_ref, gate_ref, up_ref, wgu_ref, wd_ref,
             dx_ref, dgate_ref, dup_ref, h_ref, dy_ref, dg_ref, acc_scr):
        i, c = pl.program_id(0), pl.program_id(1)

        @pl.when(c == 0)
        def _():
            xhat, _ = _rms(x_ref[...])
            h_ref[...] = (xhat * g_ref[...]).astype(BF16)
            dy_ref[...] = (0.5 * dxo_ref[...]).astype(BF16)
            acc_scr[...] = jnp.zeros_like(acc_scr)

        @pl.when((c == 0) & (i == 0))
        def _():
            dg_ref[...] = jnp.zeros_like(dg_ref)

        wg, wu, wd = wgu_ref[0], wgu_ref[1], wd_ref[...]
        for rows in groups:
            dact = _dot_nt(dy_ref[rows, :], wd)
            dgate = (dact * gate_ref[rows, :].astype(F32)).astype(BF16)
            dup = (dact * up_ref[rows, :].astype(F32)).astype(BF16)
            dgate_ref[rows, :] = dgate
            dup_ref[rows, :] = dup
            acc_scr[rows, :] += _dot(dgate, wg) + _dot(dup, wu)

        @pl.when(c == nc - 1)
        def _():
            xhat, rstd = _rms(x_ref[...])
            dxn, dg = _rms_bwd(xhat, rstd, g_ref[...], acc_scr[...])
            dx_ref[...] = dxo_ref[...] + dxn
            dg_ref[...] += dg

    assert mi % 2 == 0
    row = pl.BlockSpec((tm, d), lambda i, c: (i, 0))
    col = pl.BlockSpec((tm, tf), lambda i, c: (i, c))
    vec = pl.BlockSpec((1, d), lambda i, c: (0, 0))
    return _call(
        body, name, (t // tm, nc),
        [row, vec, row, col, col, pl.BlockSpec((2, tf, d), lambda i, c: (mi // 2, c, 0)),
         pl.BlockSpec((None, tf, d), lambda i, c: (mi + 2, c, 0))],
        [row, col, col, row, row, vec],
        [jax.ShapeDtypeStruct((t, d), F32), jax.ShapeDtypeStruct((t, f), BF16),
         jax.ShapeDtypeStruct((t, f), BF16),
         jax.ShapeDtypeStruct((t, d), BF16), jax.ShapeDtypeStruct((t, d), BF16),
         jax.ShapeDtypeStruct((1, d), F32)],
        [pltpu.VMEM((tm, d), F32)],
        ("arbitrary", "arbitrary"), (x, g, dxo, gate, up, wa, wa), payload)


def _ffn_dw(dgate, dup, act, h, dy, name, payload=None):
    t, f = dgate.shape
    d = h.shape[1]
    tk = 512
    tmm = f // 2
    nk = t // tk

    def body(dg_ref, du_ref, a_ref, h_ref, dy_ref, og_ref, ou_ref, od_ref, acc_g, acc_u, acc_d):
        k = pl.program_id(1)

        @pl.when(k == 0)
        def _():
            acc_g[...] = jnp.zeros_like(acc_g)
            acc_u[...] = jnp.zeros_like(acc_u)
            acc_d[...] = jnp.zeros_like(acc_d)

        hv = h_ref[...]
        acc_g[...] += _dot_tn(dg_ref[...], hv)
        acc_u[...] += _dot_tn(du_ref[...], hv)
        acc_d[...] += _dot_tn(a_ref[...], dy_ref[...])

        @pl.when(k == nk - 1)
        def _():
            og_ref[...] = acc_g[...].astype(BF16)
            ou_ref[...] = acc_u[...].astype(BF16)
            od_ref[...] = acc_d[...].astype(BF16)

    col = pl.BlockSpec((tk, tmm), lambda j, k: (k, j))
    row = pl.BlockSpec((tk, d), lambda j, k: (k, 0))
    out = pl.BlockSpec((tmm, d), lambda j, k: (j, 0))
    return _call(
        body, name, (f // tmm, nk), [col, col, col, row, row], [out, out, out],
        [jax.ShapeDtypeStruct((f, d), BF16)] * 3, [pltpu.VMEM((tmm, d), F32)] * 3,
        ("parallel", "arbitrary"), (dgate, dup, act, h, dy), payload)


def _tn_matmul(a, b, name, payload=None):
    t, m = a.shape
    n = b.shape[1]
    tk = 1024
    tmm = m // 2 if (m // 2) % LANES == 0 else m
    nk = t // tk

    def body(a_ref, b_ref, o_ref, acc_scr):
        k = pl.program_id(1)

        @pl.when(k == 0)
        def _():
            acc_scr[...] = jnp.zeros_like(acc_scr)

        acc_scr[...] += _dot_tn(a_ref[...].astype(BF16), b_ref[...].astype(BF16))

        @pl.when(k == nk - 1)
        def _():
            o_ref[...] = acc_scr[...].astype(BF16)

    (out,), p_outs = _call(
        body, name, (m // tmm, nk),
        [pl.BlockSpec((tk, tmm), lambda j, k: (k, j)), pl.BlockSpec((tk, n), lambda j, k: (k, 0))],
        [pl.BlockSpec((tmm, n), lambda j, k: (j, 0))],
        [jax.ShapeDtypeStruct((m, n), BF16)],
        [pltpu.VMEM((tmm, n), F32)],
        ("parallel", "arbitrary"), (a, b), payload)
    return out, p_outs


def _lane_ids(shape):
    return lax.broadcasted_iota(jnp.int32, shape, 1)


def _tril(w):
    r = lax.broadcasted_iota(jnp.int32, w.shape, 0)
    c = lax.broadcasted_iota(jnp.int32, w.shape, 1)
    return jnp.where(r >= c, w, 0.0)


def _shift_down(x, k):
    return x if k == 0 else pltpu.roll(x, k, 0)


def _shift_up(x, k):
    return x if k == 0 else pltpu.roll(x, x.shape[0] - k, 0)


def _sub_tile_shifts(ext, shift):
    return [shift(ext, b) for b in range(8)]


def _tap(shifted, j, n_out, down):
    a, b = divmod(j, 8)
    r0 = HALO - 8 * a if down else 8 * a
    return shifted[b][r0:r0 + n_out]


def _depthwise(shifted, w, n_out, down):
    acc = None
    for j in range(CONV_WIDTH):
        term = _tap(shifted, j, n_out, down) * w[CONV_WIDTH - 1 - j:CONV_WIDTH - j]
        acc = term if acc is None else acc + term
    return acc


def _conv_wgrad(shifted, dhc, n_out):
    return [jnp.sum(_tap(shifted, CONV_WIDTH - 1 - k, n_out, True) * dhc, axis=0, keepdims=True)
            for k in range(CONV_WIDTH)]


def _window_sums(ext, shift):
    s2 = ext + shift(ext, 1)
    s4 = s2 + shift(s2, 2)
    s8 = s4 + shift(s4, 4)
    s16 = s8 + shift(s8, 8)
    grp = _lane_ids(ext.shape) // HEAD_DIM
    return jnp.where(grp == 0, s2, jnp.where(grp == 1, s4, jnp.where(grp == 2, s8, s16)))


def _pool_count(t0, n, width):
    pos = (lax.broadcasted_iota(jnp.int32, (n, width), 0) + (t0 + 1)).astype(F32)
    grp = _lane_ids((n, width)) // HEAD_DIM
    win = jnp.where(grp == 0, 2.0, jnp.where(grp == 1, 4.0, jnp.where(grp == 2, 8.0, 16.0)))
    return jnp.minimum(pos, win)


def _block_diag(pw):
    gn, cg, _ = pw.shape
    rows = []
    for gi in range(gn):
        parts = [pw[gi] if gj == gi else jnp.zeros((cg, cg), pw.dtype) for gj in range(gn)]
        rows.append(jnp.concatenate(parts, axis=1))
    return jnp.concatenate(rows, axis=0)


def _head_pair_mix(w_even, w_odd, v):
    lo = _lane_ids((CHUNK, LANES)) < HEAD_DIM
    return jnp.where(lo, _dot(w_even, v), _dot(w_odd, v))


def _mixer_fwd(x, g, wb, wc, p, name, payload=None):
    t, d = x.shape
    d_in = wb.shape[1]
    sgu = p["sgu_ln_g"].shape[1]
    pool = p["pool_scale"].shape[1]
    d_mix = 2 * sgu + pool
    tm = 512
    n_i = t // tm
    hb = tm // HALO

    def body(x_ref, xp_ref, g_ref, wi_ref, wo_ref, lng_ref, lnb_ref, ws_ref, bs_ref, cw_ref, cb_ref, clg_ref,
             clb_ref, bd_ref, ps_ref, xo_ref, z_ref, hc_ref, cat_ref):
        i = pl.program_id(0)
        first = i == 0
        gain, wi = g_ref[...], wi_ref[...]

        def project(xv):
            xhat, _ = _rms(xv)
            return _dot_nt((xhat * gain).astype(BF16), wi)

        z_main = project(x_ref[...])
        z_ref[...] = z_main
        z_prev = jnp.where(first, 0.0, project(xp_ref[...]))

        lng, lnb = lng_ref[...], lnb_ref[...]
        wt = [_tril(ws_ref[h]).astype(BF16) for h in range(sgu // HEAD_DIM)]
        for n in range(tm // CHUNK):
            rows = slice(n * CHUNK, (n + 1) * CHUNK)
            u = _gelu(z_main[rows, 0:sgu])
            vhat, _ = _ln(_gelu(z_main[rows, sgu:2 * sgu]))
            vn = (vhat * lng + lnb).astype(BF16)
            for gp in range(sgu // LANES):
                ls = slice(gp * LANES, (gp + 1) * LANES)
                mixed = _head_pair_mix(wt[2 * gp], wt[2 * gp + 1], vn[:, ls]) + bs_ref[:, ls]
                cat_ref[rows, ls] = (u[:, ls] * mixed).astype(BF16)

        def glu(zz):
            return zz[:, 2 * sgu:3 * sgu] * jax.nn.sigmoid(zz[:, 3 * sgu:4 * sgu])

        ext = jnp.concatenate([glu(z_prev), glu(z_main)], axis=0)
        hc = _depthwise(_sub_tile_shifts(ext, _shift_down), cw_ref[...], tm, True) + cb_ref[...]
        hc_ref[...] = hc
        hhat, _ = _ln(hc)
        bn = hhat * clg_ref[...] + clb_ref[...]
        cat_ref[:, sgu:2 * sgu] = (bn * jax.nn.sigmoid(bn)).astype(BF16)

        pext = jnp.concatenate([z_prev[:, 4 * sgu:], z_main[:, 4 * sgu:]], axis=0)
        sums = _window_sums(pext, _shift_down)[HALO:]
        pooled = sums / _pool_count(i * tm, tm, pool) - z_main[:, 4 * sgu:]
        mixed_c = _dot(pooled.astype(BF16), bd_ref[...].astype(BF16))
        cat_ref[:, 2 * sgu:] = (mixed_c * ps_ref[...]).astype(BF16)

        xo_ref[...] = x_ref[...] + _dot(cat_ref[...], wo_ref[...])

    def vec(n):
        return pl.BlockSpec((1, n), lambda i: (0, 0))

    return _call(
        body, name, (n_i,),
        [pl.BlockSpec((tm, d), lambda i: (i, 0)),
         pl.BlockSpec((HALO, d), lambda i: (jnp.maximum(i * hb - 1, 0), 0)),
         vec(d),
         pl.BlockSpec((None, d_in, d), lambda i: (0, 0, 0)), pl.BlockSpec((None, d_mix, d), lambda i: (0, 0, 0)),
         vec(sgu), vec(sgu),
         pl.BlockSpec(p["w_spatial"].shape, lambda i: (0, 0, 0)),
         pl.BlockSpec((CHUNK, sgu), lambda i: (0, 0)),
         pl.BlockSpec((CONV_WIDTH, sgu), lambda i: (0, 0)),
         vec(sgu), vec(sgu), vec(sgu),
         pl.BlockSpec((pool, pool), lambda i: (0, 0)), vec(pool)],
        [pl.BlockSpec((tm, d), lambda i: (i, 0)), pl.BlockSpec((tm, d_in), lambda i: (i, 0)),
         pl.BlockSpec((tm, sgu), lambda i: (i, 0)), pl.BlockSpec((tm, d_mix), lambda i: (i, 0))],
        [jax.ShapeDtypeStruct((t, d), F32), jax.ShapeDtypeStruct((t, d_in), F32),
         jax.ShapeDtypeStruct((t, sgu), F32), jax.ShapeDtypeStruct((t, d_mix), BF16)], [], ("parallel",),
        (x, x, g, wb, wc, p["sgu_ln_g"], p["sgu_ln_b"], p["w_spatial"], p["bs_full"], p["conv_w"], p["conv_b"],
         p["conv_ln_g"], p["conv_ln_b"], p["bd"], p["pool_scale"]), payload)


_R_SGU_G, _R_SGU_B, _R_CONV_B, _R_CLN_G, _R_CLN_B, _R_CONV_W = 0, 1, 2, 3, 4, 8
_R384_ROWS = 40


def _mixer_bwd(x, g, z, hc_saved, dxo, wb, wc, p, name, payload=None):
    t, d_in = z.shape
    d = x.shape[1]
    sgu = p["sgu_ln_g"].shape[1]
    pool = p["pool_scale"].shape[1]
    d_mix = 2 * sgu + pool
    n_head = sgu // HEAD_DIM
    tm = 512
    n_i = t // tm
    hb = tm // HALO

    def body(x_ref, g_ref, wi_ref, z_ref, zp_ref, zn_ref, dxo_ref, dxon_ref, wo_ref, hc_ref, hcn_ref, lng_ref,
             lnb_ref, ws_ref, bs_ref, cw_ref, clg_ref, clb_ref, bd_ref, ps_ref,
             dx_ref, dz_ref, hm_ref, dgm_ref, g384_ref, gws_ref, gpool_ref, dbs_scr):
        i = pl.program_id(0)
        first, last = i == 0, i == n_i - 1

        @pl.when(first)
        def _():
            dgm_ref[...] = jnp.zeros_like(dgm_ref)
            g384_ref[...] = jnp.zeros_like(g384_ref)
            gws_ref[...] = jnp.zeros_like(gws_ref)
            gpool_ref[...] = jnp.zeros_like(gpool_ref)
            dbs_scr[...] = jnp.zeros_like(dbs_scr)

        z_main = z_ref[...]
        z_prev = jnp.where(first, 0.0, zp_ref[...])
        z_next = jnp.where(last, 0.0, zn_ref[...])
        wo = wo_ref[...]
        dc_main = _dot_nt(dxo_ref[...].astype(BF16), wo)
        dc_next = jnp.where(last, 0.0, _dot_nt(dxon_ref[...].astype(BF16), wo))

        lng, lnb = lng_ref[...], lnb_ref[...]
        wt = [_tril(ws_ref[h]) for h in range(n_head)]
        wt_b = [w.astype(BF16) for w in wt]
        wtt_b = [w.T.astype(BF16) for w in wt]
        lo = _lane_ids((CHUNK, LANES)) < HEAD_DIM
        d_lng = jnp.zeros((1, sgu), F32)
        d_lnb = jnp.zeros((1, sgu), F32)
        dws = [jnp.zeros((CHUNK, CHUNK), F32) for _ in range(n_head)]
        for n in range(tm // CHUNK):
            rows = slice(n * CHUNK, (n + 1) * CHUNK)
            au, av = z_main[rows, 0:sgu], z_main[rows, sgu:2 * sgu]
            u = _gelu(au)
            vhat, vrstd = _ln(_gelu(av))
            vn = (vhat * lng + lnb).astype(BF16)
            da = dc_main[rows, 0:sgu]
            dmixed = da * u
            dbs_scr[...] += dmixed
            dvn_parts, du_parts = [], []
            for gp in range(sgu // LANES):
                ls = slice(gp * LANES, (gp + 1) * LANES)
                vn_g = vn[:, ls]
                mixed = _head_pair_mix(wt_b[2 * gp], wt_b[2 * gp + 1], vn_g) + bs_ref[:, ls]
                du_parts.append(da[:, ls] * mixed)
                dm_g = dmixed[:, ls]
                dm_b = dm_g.astype(BF16)
                dvn_parts.append(jnp.where(lo, _dot(wtt_b[2 * gp], dm_b), _dot(wtt_b[2 * gp + 1], dm_b)))
                dws[2 * gp] = dws[2 * gp] + _dot_nt(jnp.where(lo, dm_g, 0.0).astype(BF16), vn_g)
                dws[2 * gp + 1] = dws[2 * gp + 1] + _dot_nt(jnp.where(lo, 0.0, dm_g).astype(BF16), vn_g)
            dvn = jnp.concatenate(dvn_parts, axis=1)
            du = jnp.concatenate(du_parts, axis=1)
            dv, dg_n, db_n = _ln_bwd(vhat, vrstd, lng, dvn)
            d_lng = d_lng + dg_n
            d_lnb = d_lnb + db_n
            dz_ref[rows, 0:sgu] = (du * _gelu_grad(au)).astype(BF16)
            dz_ref[rows, sgu:2 * sgu] = (dv * _gelu_grad(av)).astype(BF16)
        for h in range(n_head):
            gws_ref[h] += _tril(dws[h])
        g384_ref[_R_SGU_G:_R_SGU_G + 1, :] += d_lng
        g384_ref[_R_SGU_B:_R_SGU_B + 1, :] += d_lnb

        clg = clg_ref[...]
        bcols = slice(2 * sgu, 4 * sgu)
        zb = jnp.concatenate([z_prev[:, bcols], z_main[:, bcols], z_next[:, bcols]], axis=0)
        bval, bgate = zb[:, 0:sgu], zb[:, sgu:2 * sgu]
        sg = jax.nn.sigmoid(bgate)
        hglu = bval * sg
        n_out = tm + HALO
        hglu_shifts = _sub_tile_shifts(hglu, _shift_down)
        cw = cw_ref[...]
        hc = jnp.concatenate([hc_ref[...], jnp.where(last, 0.0, hcn_ref[...])], axis=0)
        hhat, hrstd = _ln(hc)
        bn = hhat * clg + clb_ref[...]
        db = jnp.concatenate([dc_main[:, sgu:2 * sgu], dc_next[:, sgu:2 * sgu]], axis=0)
        dbn = db * _silu_grad(bn)
        dhc_all, _, _ = _ln_bwd(hhat, hrstd, clg, dbn)
        dbn_m, hhat_m, dhc = dbn[:tm], hhat[:tm], dhc_all[:tm]
        g384_ref[_R_CLN_G:_R_CLN_G + 1, :] += jnp.sum(dbn_m * hhat_m, axis=0, keepdims=True)
        g384_ref[_R_CLN_B:_R_CLN_B + 1, :] += jnp.sum(dbn_m, axis=0, keepdims=True)
        g384_ref[_R_CONV_B:_R_CONV_B + 1, :] += jnp.sum(dhc, axis=0, keepdims=True)
        wrows = _conv_wgrad(hglu_shifts, dhc, tm)
        for k in range(CONV_WIDTH):
            g384_ref[_R_CONV_W + k:_R_CONV_W + k + 1, :] += wrows[k]
        dhglu = _depthwise(_sub_tile_shifts(dhc_all, _shift_up), cw, tm, False)
        bval_m, sg_m = bval[HALO:HALO + tm], sg[HALO:HALO + tm]
        dz_ref[:, 2 * sgu:3 * sgu] = (dhglu * sg_m).astype(BF16)
        dz_ref[:, 3 * sgu:4 * sgu] = (dhglu * bval_m * sg_m * (1.0 - sg_m)).astype(BF16)

        bd_b = bd_ref[...].astype(BF16)
        ps = ps_ref[...]
        p_main = z_main[:, 4 * sgu:]
        pext = jnp.concatenate([z_prev[:, 4 * sgu:], p_main], axis=0)
        cnt = _pool_count(i * tm, n_out, pool)
        pooled = _window_sums(pext, _shift_down)[HALO:] / cnt[:tm] - p_main
        pooled_b = pooled.astype(BF16)
        dcc = jnp.concatenate([dc_main[:, 2 * sgu:], dc_next[:, 2 * sgu:]], axis=0)
        dmix_c = dcc * ps
        mixed_c = _dot(pooled_b, bd_b)
        grp_r = lax.broadcasted_iota(jnp.int32, (pool, pool), 0) // HEAD_DIM
        grp_c = lax.broadcasted_iota(jnp.int32, (pool, pool), 1) // HEAD_DIM
        gpool_ref[0:pool, :] += jnp.where(grp_r == grp_c, _dot_tn(pooled_b, dmix_c[:tm].astype(BF16)), 0.0)
        gpool_ref[pool:pool + 1, :] += jnp.sum(dcc[:tm] * mixed_c, axis=0, keepdims=True)
        dpooled = _dot_nt(dmix_c.astype(BF16), bd_b)
        q = dpooled / cnt
        dp = _window_sums(q, _shift_up)[:tm] - dpooled[:tm]
        dz_ref[:, 4 * sgu:] = dp.astype(BF16)

        gain = g_ref[...]
        xhat, rstd = _rms(x_ref[...])
        hm_ref[...] = (xhat * gain).astype(BF16)
        dxn, dgm = _rms_bwd(xhat, rstd, gain, _dot(dz_ref[...], wi_ref[...]))
        dx_ref[...] = dxo_ref[...] + dxn
        dgm_ref[...] += dgm

        @pl.when(last)
        def _():
            r = lax.broadcasted_iota(jnp.int32, (sgu, LANES), 0)
            c = lax.broadcasted_iota(jnp.int32, (sgu, LANES), 1)
            sel = (r // HEAD_DIM == c).astype(BF16)
            gws_ref[n_head] = _split_dot(dbs_scr[...], sel)

    def vec(n):
        return pl.BlockSpec((1, n), lambda i: (0, 0))

    def prev_map(i):
        return (jnp.maximum(i * hb - 1, 0), 0)

    def next_map(i):
        return (jnp.minimum((i + 1) * hb, n_i * hb - 1), 0)

    return _call(
        body, name, (n_i,),
        [pl.BlockSpec((tm, d), lambda i: (i, 0)), pl.BlockSpec((1, d), lambda i: (0, 0)),
         pl.BlockSpec((None, d_in, d), lambda i: (0, 0, 0)),
         pl.BlockSpec((tm, d_in), lambda i: (i, 0)),
         pl.BlockSpec((HALO, d_in), prev_map), pl.BlockSpec((HALO, d_in), next_map),
         pl.BlockSpec((tm, d), lambda i: (i, 0)), pl.BlockSpec((HALO, d), next_map),
         pl.BlockSpec((None, d_mix, d), lambda i: (0, 0, 0)),
         pl.BlockSpec((tm, sgu), lambda i: (i, 0)), pl.BlockSpec((HALO, sgu), next_map),
         vec(sgu), vec(sgu),
         pl.BlockSpec(p["w_spatial"].shape, lambda i: (0, 0, 0)),
         pl.BlockSpec((CHUNK, sgu), lambda i: (0, 0)),
         pl.BlockSpec((CONV_WIDTH, sgu), lambda i: (0, 0)),
         vec(sgu), vec(sgu),
         pl.BlockSpec((pool, pool), lambda i: (0, 0)), vec(pool)],
        [pl.BlockSpec((tm, d), lambda i: (i, 0)), pl.BlockSpec((tm, d_in), lambda i: (i, 0)),
         pl.BlockSpec((tm, d), lambda i: (i, 0)), pl.BlockSpec((1, d), lambda i: (0, 0)),
         pl.BlockSpec((_R384_ROWS, sgu), lambda i: (0, 0)),
         pl.BlockSpec((n_head + 1, CHUNK, CHUNK), lambda i: (0, 0, 0)),
         pl.BlockSpec((pool + 8, pool), lambda i: (0, 0))],
        [jax.ShapeDtypeStruct((t, d), F32), jax.ShapeDtypeStruct((t, d_in), BF16),
         jax.ShapeDtypeStruct((t, d), BF16), jax.ShapeDtypeStruct((1, d), F32),
         jax.ShapeDtypeStruct((_R384_ROWS, sgu), F32),
         jax.ShapeDtypeStruct((n_head + 1, CHUNK, CHUNK), F32),
         jax.ShapeDtypeStruct((pool + 8, pool), F32)],
        [pltpu.VMEM((CHUNK, sgu), F32)], ("arbitrary",),
        (x, g, wb, z, z, z, dxo, dxo, wc, hc_saved, hc_saved, p["sgu_ln_g"], p["sgu_ln_b"], p["w_spatial"],
         p["bs_full"], p["conv_w"], p["conv_ln_g"], p["conv_ln_b"], p["bd"], p["pool_scale"]), payload)


def _loss_head(x, g, target, name):
    t, d = x.shape
    tm = 512

    def body(x_ref, g_ref, tgt_ref, dx_ref, dg_ref, loss_ref):
        i = pl.program_id(0)

        @pl.when(i == 0)
        def _():
            dg_ref[...] = jnp.zeros_like(dg_ref)
            loss_ref[...] = jnp.zeros_like(loss_ref)

        gv = g_ref[...]
        xhat, rstd = _rms(x_ref[...])
        err = xhat * gv - tgt_ref[...]
        loss_ref[...] += jnp.zeros_like(loss_ref) + 0.5 * jnp.sum(jnp.mean(err * err, axis=-1, keepdims=True))
        dxn, dg = _rms_bwd(xhat, rstd, gv, err * (1.0 / d))
        dx_ref[...] = dxn
        dg_ref[...] += dg

    row = pl.BlockSpec((tm, d), lambda i: (i, 0))
    vec = pl.BlockSpec((1, d), lambda i: (0, 0))
    return pl.pallas_call(
        body, name=name, grid=(t // tm,),
        in_specs=[row, vec, row],
        out_specs=[row, vec, pl.BlockSpec((1, LANES), lambda i: (0, 0))],
        out_shape=[jax.ShapeDtypeStruct((t, d), F32), jax.ShapeDtypeStruct((1, d), F32),
                   jax.ShapeDtypeStruct((1, LANES), F32)],
        compiler_params=_cparams(("arbitrary",)),
    )(x, g, target)


def _all_gather(arrs, name, extra=None):
    gather = _GatherIci(arrs)
    n = len(arrs)
    forward = _GatherForward([jax.ShapeDtypeStruct(s.shape, s.dtype) for s in gather.out_shapes])
    x_in = len(extra.ins) if extra else 0
    x_out = len(extra.out_shapes) if extra else 0

    def body(*refs):
        ins, x_ins = refs[:n], refs[n:n + x_in]
        outs, x_outs = refs[n + x_in:2 * n + x_in], refs[2 * n + x_in:2 * n + x_in + x_out]
        sems = refs[2 * n + x_in + x_out:]
        first = gather.build(ins, outs, *sems[0:3])
        first.start()
        if extra:
            beside = extra.build(x_ins, x_outs, *sems[6:9])
            beside.start()
        first.wait()
        second = forward.build(outs, outs, *sems[3:6])
        second.start()
        second.wait()
        if extra:
            beside.wait()

    outs = pl.pallas_call(
        body, name=name,
        in_specs=[ANY] * (n + x_in), out_specs=[ANY] * (n + x_out),
        out_shape=list(gather.out_shapes) + (list(extra.out_shapes) if extra else []),
        scratch_shapes=gather.sem_shapes() + forward.sem_shapes() + (extra.sem_shapes() if extra else []),
    )(*arrs, *(extra.ins if extra else []))
    return list(outs[:n]), list(outs[n:])


def _all_gather_relayed(arrs, name):
    n = len(arrs)
    n_pairs = 8

    def body(*refs):
        ins, outs = refs[:n], refs[n:2 * n]
        send_sems, recv_sems, local_sems = refs[2 * n:]
        x, y, c = _position()
        sib, xn, yn = (x, y, 1 - c), (1 - x, y, c), (x, 1 - y, c)

        def rows(a, px, py, pc, half=None):
            r = ins[a].shape[1]
            base = (4 * px + 2 * py + pc) * r
            if half is None:
                return outs[a].at[:, pl.ds(base, r), :]
            return outs[a].at[:, pl.ds(base + half * (r // 2), r // 2), :]

        def send(a, k, src, dst, to):
            return _remote(src, dst, send_sems, recv_sems, a * n_pairs + k, to)

        def arrived(a, k, land, sender):
            _remote(land, land, send_sems, recv_sems, a * n_pairs + k, sender).wait_recv()

        own = [pltpu.make_async_copy(ins[a], rows(a, x, y, c), local_sems.at[a]) for a in range(n)]
        first = [send(a, k, ins[a], rows(a, x, y, c), to) for a in range(n) for k, to in enumerate((sib, xn, yn))]
        for cp in own + first:
            cp.start()
        for a in range(n):
            arrived(a, 1, rows(a, *xn), xn)
            arrived(a, 2, rows(a, *yn), yn)
        second = []
        for a in range(n):
            second += [send(a, 3, rows(a, *xn, half=0), rows(a, *xn, half=0), yn),
                       send(a, 4, rows(a, *yn, half=1), rows(a, *yn, half=1), xn),
                       send(a, 5, rows(a, *xn), rows(a, *xn), sib),
                       send(a, 6, rows(a, *yn), rows(a, *yn), sib)]
        for cp in second:
            cp.start()
        for a in range(n):
            arrived(a, 3, rows(a, 1 - x, 1 - y, c, half=0), yn)
            arrived(a, 4, rows(a, 1 - x, 1 - y, c, half=1), xn)
        third = [send(a, 7, rows(a, 1 - x, 1 - y, c), rows(a, 1 - x, 1 - y, c), sib) for a in range(n)]
        for cp in third:
            cp.start()
        for a in range(n):
            arrived(a, 0, rows(a, *sib), sib)
            arrived(a, 5, rows(a, 1 - x, y, 1 - c), sib)
            arrived(a, 6, rows(a, x, 1 - y, 1 - c), sib)
            arrived(a, 7, rows(a, 1 - x, 1 - y, 1 - c), sib)
        for cp in first + second + third:
            cp.wait_send()
        for cp in own:
            cp.wait()

    return list(pl.pallas_call(
        body, name=name, in_specs=[ANY] * n, out_specs=[ANY] * n,
        out_shape=[jax.ShapeDtypeStruct((a.shape[0], N_DEV * a.shape[1], a.shape[2]), a.dtype) for a in arrs],
        scratch_shapes=[pltpu.SemaphoreType.DMA((n_pairs * n,)), pltpu.SemaphoreType.DMA((n_pairs * n,)),
                        pltpu.SemaphoreType.DMA((n,))],
    )(*arrs))


def _pair_sums(grads, recvs, cidx, name):
    n = len(grads)

    def body(c_ref, *refs):
        for g_ref, r_ref, o_ref in zip(refs[:n], refs[n:2 * n], refs[2 * n:]):
            o_ref[...] = (g_ref[...].astype(F32) + r_ref[...].astype(F32)).astype(BF16)

    shapes = [(g.shape[0] // N_DEV, g.shape[1]) for g in grads]
    return list(pl.pallas_call(
        body, name=name,
        grid_spec=pltpu.PrefetchScalarGridSpec(
            num_scalar_prefetch=1, grid=(N_CHIP,),
            in_specs=[pl.BlockSpec(s, lambda q, c: (2 * q + c[0], 0)) for s in shapes]
            + [pl.BlockSpec(s, lambda q, c: (q, 0)) for s in shapes],
            out_specs=[pl.BlockSpec(s, lambda q, c: (q, 0)) for s in shapes]),
        out_shape=[jax.ShapeDtypeStruct((N_CHIP * r, cols), BF16) for r, cols in shapes],
        compiler_params=_cparams(("parallel",)),
    )(cidx, *grads, *recvs))


def _sum_blocks(parts, nblk, name):
    r = parts.shape[0] // nblk
    cols = parts.shape[1]

    def body(p_ref, o_ref):
        acc = p_ref[0:r, :].astype(F32)
        for q in range(1, nblk):
            acc = acc + p_ref[q * r:(q + 1) * r, :].astype(F32)
        o_ref[...] = acc

    return pl.pallas_call(
        body, name=name,
        out_shape=jax.ShapeDtypeStruct((r, cols), F32),
        compiler_params=_cparams(),
    )(parts)


def _adamw_math(w, g, m, v):
    m = ADAM_B1 * m + (1.0 - ADAM_B1) * g
    v = ADAM_B2 * v + (1.0 - ADAM_B2) * (g * g)
    m_hat = m / (1.0 - ADAM_B1 ** ADAM_STEP)
    v_hat = v / (1.0 - ADAM_B2 ** ADAM_STEP)
    delta = -ADAM_LR * (m_hat / (jnp.sqrt(v_hat) + ADAM_EPS) + ADAM_WD * w)
    return delta, m, v


def _finish_sharded(parts, w, m, v, name):
    depth, rr, cw = w.shape

    def body(*refs):
        p_refs = refs[:depth]
        w_ref, m_ref, v_ref, g_ref, d_ref, mo_ref, vo_ref = refs[depth:]
        l = pl.program_id(0)
        for k in range(depth):
            @pl.when(l == k)
            def _(p_ref=p_refs[k]):
                r = p_ref.shape[0] // N_CHIP
                acc = p_ref[0:r, :].astype(F32)
                for q in range(1, N_CHIP):
                    acc = acc + p_ref[q * r:(q + 1) * r, :].astype(F32)
                g_ref[...] = acc
                d_ref[...], mo_ref[...], vo_ref[...] = _adamw_math(w_ref[...], acc, m_ref[...], v_ref[...])

    blk = pl.BlockSpec((None, rr, cw), lambda l: (l, 0, 0))
    return pl.pallas_call(
        body, name=name, grid=(depth,),
        in_specs=[pl.BlockSpec(p.shape, lambda l: (0, 0)) for p in parts] + [blk] * 3, out_specs=[blk] * 4,
        out_shape=[jax.ShapeDtypeStruct(w.shape, F32)] * 4,
        compiler_params=_cparams(("arbitrary",)),
    )(*parts, w, m, v)


def _adamw_small(ws, gs, ms, vs, name):
    n = len(ws)

    def body(*refs):
        for k in range(n):
            w_ref, g_ref, m_ref, v_ref = (refs[j * n + k] for j in range(4))
            d_ref, mo_ref, vo_ref = (refs[(4 + j) * n + k] for j in range(3))
            d_ref[...], mo_ref[...], vo_ref[...] = _adamw_math(w_ref[...], g_ref[...], m_ref[...], v_ref[...])

    shapes = [jax.ShapeDtypeStruct(w.shape, F32) for w in ws]
    return pl.pallas_call(
        body, name=name, out_shape=shapes * 3, compiler_params=_cparams(),
    )(*ws, *gs, *ms, *vs)


def kernel(x, ffn1_norm, ffn1_w_gate, ffn1_w_up, ffn1_w_down, mix_norm, w_in, sgu_ln_g, sgu_ln_b, w_spatial, b_spatial, conv_w, conv_b, conv_ln_g, conv_ln_b, pool_w, pool_scale, w_out, ffn2_norm, ffn2_w_gate, ffn2_w_up, ffn2_w_down, final_norm, loss_target, m_ffn1_norm, m_ffn1_w_gate, m_ffn1_w_up, m_ffn1_w_down, m_mix_norm, m_w_in, m_sgu_ln_g, m_sgu_ln_b, m_w_spatial, m_b_spatial, m_conv_w, m_conv_b, m_conv_ln_g, m_conv_ln_b, m_pool_w, m_pool_scale, m_w_out, m_ffn2_norm, m_ffn2_w_gate, m_ffn2_w_up, m_ffn2_w_down, m_final_norm, v_ffn1_norm, v_ffn1_w_gate, v_ffn1_w_up, v_ffn1_w_down, v_mix_norm, v_w_in, v_sgu_ln_g, v_sgu_ln_b, v_w_spatial, v_b_spatial, v_conv_w, v_conv_b, v_conv_ln_g, v_conv_ln_b, v_pool_w, v_pool_scale, v_w_out, v_ffn2_norm, v_ffn2_w_gate, v_ffn2_w_up, v_ffn2_w_down, v_final_norm):
    names = ["ffn1_norm", "ffn1_w_gate", "ffn1_w_up", "ffn1_w_down", "mix_norm", "w_in", "sgu_ln_g", "sgu_ln_b",
             "w_spatial", "b_spatial", "conv_w", "conv_b", "conv_ln_g", "conv_ln_b", "pool_w", "pool_scale",
             "w_out", "ffn2_norm", "ffn2_w_gate", "ffn2_w_up", "ffn2_w_down", "final_norm"]
    W = dict(zip(names, [ffn1_norm, ffn1_w_gate, ffn1_w_up, ffn1_w_down, mix_norm, w_in, sgu_ln_g, sgu_ln_b,
                         w_spatial, b_spatial, conv_w, conv_b, conv_ln_g, conv_ln_b, pool_w, pool_scale, w_out,
                         ffn2_norm, ffn2_w_gate, ffn2_w_up, ffn2_w_down, final_norm]))
    M = dict(zip(names, [m_ffn1_norm, m_ffn1_w_gate, m_ffn1_w_up, m_ffn1_w_down, m_mix_norm, m_w_in, m_sgu_ln_g,
                         m_sgu_ln_b, m_w_spatial, m_b_spatial, m_conv_w, m_conv_b, m_conv_ln_g, m_conv_ln_b,
                         m_pool_w, m_pool_scale, m_w_out, m_ffn2_norm, m_ffn2_w_gate, m_ffn2_w_up, m_ffn2_w_down,
                         m_final_norm]))
    V = dict(zip(names, [v_ffn1_norm, v_ffn1_w_gate, v_ffn1_w_up, v_ffn1_w_down, v_mix_norm, v_w_in, v_sgu_ln_g,
                         v_sgu_ln_b, v_w_spatial, v_b_spatial, v_conv_w, v_conv_b, v_conv_ln_g, v_conv_ln_b,
                         v_pool_w, v_pool_scale, v_w_out, v_ffn2_norm, v_ffn2_w_gate, v_ffn2_w_up, v_ffn2_w_down,
                         v_final_norm]))

    depth, d = ffn1_norm.shape
    t = x.shape[1]
    sgu = sgu_ln_g.shape[1]
    pool = pool_scale.shape[1]
    n_head = sgu // HEAD_DIM
    cw_shard = conv_w.shape[2]
    xs = x.reshape(t, d)
    target = loss_target.reshape(t, d)

    def tr(w):
        return jnp.swapaxes(w, 1, 2).astype(BF16)

    ffn_shards = [[jnp.stack([tr(ffn1_w_gate)[l], tr(ffn1_w_up)[l], ffn1_w_down[l].astype(BF16)]),
                   jnp.stack([tr(ffn2_w_gate)[l], tr(ffn2_w_up)[l], ffn2_w_down[l].astype(BF16)])]
                  for l in range(depth)]
    win_shards = [tr(w_in)[l:l + 1] for l in range(depth)]
    wout_shards = [w_out[l:l + 1].astype(BF16) for l in range(depth)]
    cw_rows = depth * CONV_WIDTH
    cw_pad = -cw_rows % 8
    cw_send = jnp.pad(conv_w.reshape(cw_rows, cw_shard), ((0, cw_pad), (0, 0)))[None]
    wffn, wb, wc = {}, {}, {}
    wffn[(0, 0)], wb[0], wc[0], cwg = _all_gather_relayed(
        [ffn_shards[0][0], win_shards[0], wout_shards[0], cw_send], "ag_first")
    conv_w_full = cwg.reshape(N_DEV, cw_rows + cw_pad, cw_shard)[:, :cw_rows].reshape(
        N_DEV, depth, CONV_WIDTH, cw_shard).transpose(1, 2, 0, 3).reshape(depth, CONV_WIDTH, N_DEV * cw_shard)

    def mixer_params(l):
        return dict(
            sgu_ln_g=sgu_ln_g[l:l + 1], sgu_ln_b=sgu_ln_b[l:l + 1], w_spatial=w_spatial[l],
            bs_full=jnp.repeat(b_spatial[l].T, HEAD_DIM, axis=1),
            conv_w=conv_w_full[l], conv_b=conv_b[l:l + 1], conv_ln_g=conv_ln_g[l:l + 1],
            conv_ln_b=conv_ln_b[l:l + 1], bd=_block_diag(pool_w[l]), pool_scale=pool_scale[l:l + 1])

    saved = []
    cur = xs
    for l in range(depth):
        p = mixer_params(l)
        x0 = cur
        more = l + 1 < depth
        (x1, gate1, up1, act1), part = _ffn_fwd(x0, ffn1_norm[l:l + 1], wffn[(l, 0)], 0, f"ffn1_fwd_{l}",
                                          _GatherIci([ffn_shards[l][1]]))
        riding = [_GatherForward(part)] + ([_GatherIci([win_shards[l + 1], wout_shards[l + 1]])] if more else [])
        (x2, z, hc, cat), part = _mixer_fwd(x1, mix_norm[l:l + 1], wb[l], wc[l], p, f"mixer_fwd_{l}", _Merged(riding))
        wffn[(l, 1)] = part[0]
        riding = [_GatherIci([ffn_shards[l + 1][0]]), _GatherForward(part[1:])] if more else []
        (x3, gate2, up2, act2), part = _ffn_fwd(x2, ffn2_norm[l:l + 1], wffn[(l, 1)], 0, f"ffn2_fwd_{l}",
                                          _Merged(riding) if more else None)
        if more:
            wb[l + 1], wc[l + 1] = part[1:]
            (wffn[(l + 1, 0)],) = _comm(_GatherForward(part[:1]), f"ag_forward_{l + 1}")
        saved.append((p, x0, gate1, up1, act1, x1, z, hc, cat, x2, gate2, up2, act2))
        cur = x3

    dx, d_final, loss_part = _loss_head(cur, final_norm.reshape(1, d), target, "loss_head")

    cidx = lax.axis_index("c").astype(jnp.int32).reshape(1)
    from_chips = {}
    to_pair, to_chip = [], []
    small = []

    def pair_payload():
        return _PairExchange([g for _, g in to_pair]) if to_pair else None

    def pair_done(received):
        if to_pair:
            (nm, l), _ = to_pair[0]
            sums = _pair_sums([g for _, g in to_pair], list(received), cidx, f"rs_pair_sum_{nm}_{l}")
            to_chip.extend((key, s) for (key, _), s in zip(to_pair, sums))
        to_pair.clear()

    def take_chip():
        items = list(to_chip)
        to_chip.clear()
        return items

    def chip_payload(items):
        return _ChipExchange([s for _, s in items]) if items else None

    def chip_done(items, landed):
        for (key, _), o in zip(items, landed):
            from_chips[key] = o

    def ffn_weight_grads(prefix, l, dgate, dup, act, h, dy):
        items = take_chip()
        items, later = items[:2], items[2:]
        to_chip.extend(later)
        grads3, landed = _ffn_dw(dgate, dup, act, h, dy, f"dw_{prefix}_{l}", chip_payload(items))
        chip_done(items, landed)
        to_pair.extend(((f"{prefix}_{nm}", l), g) for nm, g in zip(("w_gate", "w_up", "w_down"), grads3))

    for l in reversed(range(depth)):
        p, x0, gate1, up1, act1, x1, z, hc, cat, x2, gate2, up2, act2 = saved[l]
        (dx, dgate, dup, h, dy, dg_ffn2), received = _ffn_bwd(
            x2, ffn2_norm[l:l + 1], dx, gate2, up2, wffn[(l, 1)], 0, f"ffn2_bwd_{l}", pair_payload())
        pair_done(received)
        ffn_weight_grads("ffn2", l, dgate, dup, act2, h, dy)
        g_out, received = _tn_matmul(cat, dx, f"dw_w_out_{l}", pair_payload())
        pair_done(received)
        items = take_chip()
        items, later = items[:3], items[3:]
        to_chip.extend(later)
        (dx, dz, hm, dg_mix, g384, gws, gpool), landed = _mixer_bwd(
            x1, mix_norm[l:l + 1], z, hc, dx, wb[l], wc[l], p, f"mixer_bwd_{l}", chip_payload(items))
        chip_done(items, landed)
        g_in, _ = _tn_matmul(dz, hm, f"dw_w_in_{l}")
        to_pair.extend([(("w_out", l), g_out), (("w_in", l), g_in)])
        if l > 0:
            (dx, dgate, dup, h, dy, dg_ffn1), received = _ffn_bwd(
                x0, ffn1_norm[l:l + 1], dx, gate1, up1, wffn[(l, 0)], 0, f"ffn1_bwd_{l}", pair_payload())
            pair_done(received)
            ffn_weight_grads("ffn1", l, dgate, dup, act1, h, dy)
            small.append((l, g384, gws, gpool, dg_ffn1, dg_mix, dg_ffn2))
            continue

        small.append((0, g384, gws, gpool, None, dg_mix, dg_ffn2))
        small.sort(key=lambda s: s[0])
        norm_rows = []
        for (sl, _, _, _, dg1, dgm, dg2) in small:
            norm_rows += [jnp.zeros((1, d), F32) if dg1 is None else dg1, dgm, dg2]
        norm_rows += [d_final, jnp.pad(loss_part, ((0, 0), (0, d - LANES)))]
        n_norm = len(norm_rows)
        norm_pack = jnp.concatenate(norm_rows + [jnp.zeros((8 - n_norm % 8, d), F32)] * (n_norm % 8 != 0), axis=0)
        parts = [norm_pack]
        for (_, s384, sws, spool, _, _, _) in small:
            parts += [s384, sws.reshape((n_head + 1) * CHUNK, CHUNK), spool]
        n_pair = len(to_pair)
        early = take_chip()
        riding = [pair_payload(), _GatherIci([a[None] for a in parts])] + ([chip_payload(early)] if early else [])
        (dx, dgate, dup, h, dy, dg_ffn1), landed = _ffn_bwd(
            x0, ffn1_norm[l:l + 1], dx, gate1, up1, wffn[(l, 0)], 0, f"ffn1_bwd_{l}", _Merged(riding))
        pair_done(landed[:n_pair])
        chip_done(early, landed[n_pair + len(parts):])
        items = take_chip()
        g_gate, landed = _tn_matmul(
            dgate, h, f"dw_ffn1_w_gate_{l}",
            _Merged([chip_payload(items), _GatherForward(landed[n_pair:n_pair + len(parts)])]))
        chip_done(items, landed[:len(items)])
        gathered = landed[len(items):]
        to_pair.append((("ffn1_w_gate", l), g_gate))
        g_up, received = _tn_matmul(dup, h, f"dw_ffn1_w_up_{l}", pair_payload())
        pair_done(received)
        to_pair.append((("ffn1_w_up", l), g_up))
        items = take_chip()
        g_down, landed = _tn_matmul(act1, dy, f"dw_ffn1_w_down_{l}", _Merged([chip_payload(items), pair_payload()]))
        chip_done(items, landed[:len(items)])
        pair_done(landed[len(items):])
        to_pair.append((("ffn1_w_down", l), g_down))
    grad_x = dx.reshape(x.shape)
    pair_done(_comm(pair_payload(), "rs_pair_exchange_last"))
    items = take_chip()
    (late_norm,), landed = _all_gather([jnp.pad(dg_ffn1, ((0, 7), (0, 0)))[None]], "ag_tail", chip_payload(items))
    chip_done(items, landed)

    summed = [_sum_blocks(g[0], N_DEV, f"sum_small_{k}") for k, g in enumerate(gathered)]
    late_sum = _sum_blocks(late_norm[0], N_DEV, "sum_small_late")
    norm_sum = summed[0]
    loss = norm_sum[3 * depth + 1, 0]
    cpos = lax.axis_index("x") * 4 + lax.axis_index("y") * 2 + lax.axis_index("c")
    sg = {nm: [] for nm in names}
    for l in range(depth):
        g384, gws, gpool = summed[1 + 3 * l], summed[2 + 3 * l].reshape(n_head + 1, CHUNK, CHUNK), summed[3 + 3 * l]
        sg["ffn1_norm"].append(norm_sum[3 * l] if l > 0 else late_sum[0])
        sg["mix_norm"].append(norm_sum[3 * l + 1])
        sg["ffn2_norm"].append(norm_sum[3 * l + 2])
        sg["sgu_ln_g"].append(g384[_R_SGU_G])
        sg["sgu_ln_b"].append(g384[_R_SGU_B])
        sg["conv_b"].append(g384[_R_CONV_B])
        sg["conv_ln_g"].append(g384[_R_CLN_G])
        sg["conv_ln_b"].append(g384[_R_CLN_B])
        sg["conv_w"].append(lax.dynamic_slice_in_dim(g384[_R_CONV_W:_R_CONV_W + CONV_WIDTH], cpos * cw_shard,
                                                     cw_shard, axis=1))
        sg["w_spatial"].append(gws[:n_head])
        sg["b_spatial"].append(gws[n_head][:, :n_head].T)
        sg["pool_w"].append(jnp.stack([gpool[k * HEAD_DIM:(k + 1) * HEAD_DIM, k * HEAD_DIM:(k + 1) * HEAD_DIM]
                                       for k in range(pool // HEAD_DIM)], axis=0))
        sg["pool_scale"].append(gpool[pool])
    small_names = ["ffn1_norm", "mix_norm", "sgu_ln_g", "sgu_ln_b", "w_spatial", "b_spatial", "conv_w", "conv_b",
                   "conv_ln_g", "conv_ln_b", "pool_w", "pool_scale", "ffn2_norm"]
    grads = {nm: jnp.stack(sg[nm], axis=0) for nm in small_names}
    grads["final_norm"] = norm_sum[3 * depth]

    delta, new_m, new_v = {}, {}, {}
    big_names = ["ffn1_w_gate", "ffn1_w_up", "ffn1_w_down", "w_in", "w_out", "ffn2_w_gate", "ffn2_w_up",
                 "ffn2_w_down"]
    transposed = {"ffn1_w_gate", "ffn1_w_up", "w_in", "ffn2_w_gate", "ffn2_w_up"}
    for nm in big_names:
        view = (lambda a: jnp.swapaxes(a, 1, 2)) if nm in transposed else (lambda a: a)
        outs = _finish_sharded([from_chips[(nm, l)] for l in range(depth)], view(W[nm]), view(M[nm]), view(V[nm]),
                               f"adamw_{nm}")
        grads[nm], delta[nm], new_m[nm], new_v[nm] = (view(o) for o in outs)
    snames = small_names + ["final_norm"]

    def flat2(a):
        return a.reshape(-1, a.shape[-1])

    outs = _adamw_small([flat2(W[nm]) for nm in snames], [flat2(grads[nm]) for nm in snames],
                        [flat2(M[nm]) for nm in snames], [flat2(V[nm]) for nm in snames], "adamw_small")
    ns = len(snames)
    for k, nm in enumerate(snames):
        shp = W[nm].shape
        delta[nm], new_m[nm], new_v[nm] = (outs[k].reshape(shp), outs[ns + k].reshape(shp),
                                           outs[2 * ns + k].reshape(shp))

    return (loss, grad_x, *[grads[nm] for nm in names], *[delta[nm] for nm in names],
            *[new_m[nm] for nm in names], *[new_v[nm] for nm in names])
```

```python
import functools

import jax
import jax.numpy as jnp
from jax import lax
from jax.experimental import pallas as pl
from jax.experimental.pallas import tpu as pltpu

F32 = jnp.float32
BF16 = jnp.bfloat16
EPS = 1e-6
N_DEV = 8
N_CHIP = 4
MESH = pl.DeviceIdType.MESH
ANY = pl.BlockSpec(memory_space=pl.ANY)

VMEM_LIMIT_BYTES = 56 * 1024 * 1024
LANES = 128
HALO = 32
HEAD_DIM = 64
CHUNK = 128
CONV_WIDTH = 31
POOL_WINDOWS = (2, 4, 8, 16)

ADAM_LR = 0.001
ADAM_B1 = 0.9
ADAM_B2 = 0.999
ADAM_EPS = 1e-08
ADAM_WD = 0.01
ADAM_STEP = 10


def _cparams(sem=None):
    return pltpu.CompilerParams(dimension_semantics=sem, vmem_limit_bytes=VMEM_LIMIT_BYTES)


def _position():
    return lax.axis_index("x"), lax.axis_index("y"), lax.axis_index("c")


class _Copies:
    def __init__(self):
        self.local, self.sends, self.recvs = [], [], []

    def extend(self, other):
        self.local += other.local
        self.sends += other.sends
        self.recvs += other.recvs

    def start(self):
        for cp in self.local + self.sends:
            cp.start()

    def wait(self):
        for land, send_sems, recv_sems, k, peer in self.recvs:
            _remote(land, land, send_sems, recv_sems, k, peer).wait_recv()
        for cp in self.sends:
            cp.wait_send()
        for cp in self.local:
            cp.wait()


def _remote(src, dst, send_sems, recv_sems, k, to):
    return pltpu.make_async_remote_copy(src_ref=src, dst_ref=dst, send_sem=send_sems.at[k], recv_sem=recv_sems.at[k],
                                        device_id=to, device_id_type=MESH)


class _Payload:
    ins, out_shapes, aliases, n_remote, n_local = (), (), {}, 0, 0

    def sem_shapes(self):
        return [pltpu.SemaphoreType.DMA((max(self.n_remote, 1),)), pltpu.SemaphoreType.DMA((max(self.n_remote, 1),)),
                pltpu.SemaphoreType.DMA((max(self.n_local, 1),))]


class _GatherIci(_Payload):
    def __init__(self, shards):
        self.ins = list(shards)
        self.out_shapes = [jax.ShapeDtypeStruct((s.shape[0], N_DEV * s.shape[1], s.shape[2]), s.dtype) for s in shards]
        self.n_remote, self.n_local = 4 * len(shards), len(shards)

    def build(self, ins, outs, send_sems, recv_sems, local_sems, k0=0, l0=0):
        x, y, c = _position()
        peers = [(x, y, 1 - c), (1 - x, y, c), (x, 1 - y, c), (1 - x, 1 - y, c)]
        cps = _Copies()
        for a, (src, out) in enumerate(zip(ins, outs)):
            r = src.shape[1]

            def rows(px, py, pc, out=out, r=r):
                return out.at[:, pl.ds((4 * px + 2 * py + pc) * r, r), :]

            cps.local.append(pltpu.make_async_copy(src, rows(x, y, c), local_sems.at[l0 + a]))
            for k, peer in enumerate(peers):
                cps.sends.append(_remote(src, rows(x, y, c), send_sems, recv_sems, k0 + 4 * a + k, peer))
                cps.recvs.append((rows(*peer), send_sems, recv_sems, k0 + 4 * a + k, peer))
        return cps


class _GatherForward(_Payload):
    def __init__(self, partials):
        self.ins = list(partials)
        self.out_shapes = [jax.ShapeDtypeStruct(p.shape, p.dtype) for p in partials]
        self.aliases = {a: a for a in range(len(partials))}
        self.n_remote = 3 * len(partials)

    def build(self, ins, outs, send_sems, recv_sems, local_sems, k0=0, l0=0):
        x, y, c = _position()
        chips = [(1 - x, y), (x, 1 - y), (1 - x, 1 - y)]
        cps = _Copies()
        for a, out in enumerate(outs):
            r = out.shape[1] // N_DEV
            for k, (px, py) in enumerate(chips):
                mine = out.at[:, pl.ds((4 * px + 2 * py + c) * r, r), :]
                theirs = out.at[:, pl.ds((4 * px + 2 * py + 1 - c) * r, r), :]
                cps.sends.append(_remote(mine, mine, send_sems, recv_sems, k0 + 3 * a + k, (x, y, 1 - c)))
                cps.recvs.append((theirs, send_sems, recv_sems, k0 + 3 * a + k, (x, y, 1 - c)))
        return cps


class _PairExchange(_Payload):
    def __init__(self, grads):
        self.ins = list(grads)
        self.out_shapes = [jax.ShapeDtypeStruct((g.shape[0] // 2, g.shape[1]), g.dtype) for g in grads]
        self.n_remote = N_CHIP * len(grads)

    def build(self, ins, outs, send_sems, recv_sems, local_sems, k0=0, l0=0):
        x, y, c = _position()
        cps = _Copies()
        for a, (src, out) in enumerate(zip(ins, outs)):
            r = src.shape[0] // N_DEV
            for q in range(N_CHIP):
                land = out.at[pl.ds(q * r, r), :]
                cps.sends.append(_remote(src.at[pl.ds((2 * q + 1 - c) * r, r), :], land, send_sems, recv_sems,
                                         k0 + N_CHIP * a + q, (x, y, 1 - c)))
                cps.recvs.append((land, send_sems, recv_sems, k0 + N_CHIP * a + q, (x, y, 1 - c)))
        return cps


class _ChipExchange(_Payload):
    def __init__(self, sums):
        self.ins = list(sums)
        self.out_shapes = [jax.ShapeDtypeStruct(s.shape, s.dtype) for s in sums]
        self.n_remote, self.n_local = 3 * len(sums), len(sums)

    def build(self, ins, outs, send_sems, recv_sems, local_sems, k0=0, l0=0):
        x, y, c = _position()
        my_chip = 2 * x + y
        chips = [(1 - x, y), (x, 1 - y), (1 - x, 1 - y)]
        cps = _Copies()
        for a, (src, out) in enumerate(zip(ins, outs)):
            r = src.shape[0] // N_CHIP
            mine = out.at[pl.ds(my_chip * r, r), :]
            cps.local.append(pltpu.make_async_copy(src.at[pl.ds(my_chip * r, r), :], mine, local_sems.at[l0 + a]))
            for k, (px, py) in enumerate(chips):
                land = out.at[pl.ds((2 * px + py) * r, r), :]
                cps.sends.append(_remote(src.at[pl.ds((2 * px + py) * r, r), :], mine, send_sems, recv_sems,
                                         k0 + 3 * a + k, (px, py, c)))
                cps.recvs.append((land, send_sems, recv_sems, k0 + 3 * a + k, (px, py, c)))
        return cps


class _Merged(_Payload):
    def __init__(self, parts):
        self.parts = list(parts)
        self.ins = [a for p in parts for a in p.ins]
        self.out_shapes = [s for p in parts for s in p.out_shapes]
        self.aliases, self.offsets = {}, []
        i0 = o0 = k0 = l0 = 0
        for p in parts:
            self.offsets.append((i0, o0, k0, l0))
            self.aliases.update({i0 + i: o0 + o for i, o in p.aliases.items()})
            i0, o0, k0, l0 = i0 + len(p.ins), o0 + len(p.out_shapes), k0 + p.n_remote, l0 + p.n_local
        self.n_remote, self.n_local = k0, l0

    def build(self, ins, outs, send_sems, recv_sems, local_sems):
        cps = _Copies()
        for p, (i0, o0, k0, l0) in zip(self.parts, self.offsets):
            cps.extend(p.build(ins[i0:i0 + len(p.ins)], outs[o0:o0 + len(p.out_shapes)], send_sems, recv_sems,
                               local_sems, k0, l0))
        return cps


def _call(body, name, grid, in_specs, out_specs, out_shape, scratch_shapes, semantics, args, payload=None):
    if payload is None:
        outs = pl.pallas_call(body, name=name, grid=grid, in_specs=in_specs, out_specs=out_specs,
                              out_shape=out_shape, scratch_shapes=scratch_shapes,
                              compiler_params=_cparams(semantics))(*args)
        return list(outs), []
    n_in, n_out, n_scr = len(in_specs), len(out_specs), len(scratch_shapes)
    p_in, p_out = len(payload.ins), len(payload.out_shapes)

    def carried(*refs):
        ins, p_ins = refs[:n_in], refs[n_in:n_in + p_in]
        o0 = n_in + p_in
        outs, p_outs = refs[o0:o0 + n_out], refs[o0 + n_out:o0 + n_out + p_out]
        s0 = o0 + n_out + p_out
        scr, sems = refs[s0:s0 + n_scr], refs[s0 + n_scr:]
        ids = [pl.program_id(k) for k in range(len(grid))]
        at_first = functools.reduce(jnp.logical_and, [i == 0 for i in ids])
        at_last = functools.reduce(jnp.logical_and, [i == g - 1 for i, g in zip(ids, grid)])

        @pl.when(at_first)
        def _():
            payload.build(p_ins, p_outs, *sems).start()

        body(*ins, *outs, *scr)

        @pl.when(at_last)
        def _():
            payload.build(p_ins, p_outs, *sems).wait()

    outs = pl.pallas_call(
        carried, name=name, grid=grid, in_specs=list(in_specs) + [ANY] * p_in,
        out_specs=list(out_specs) + [ANY] * p_out, out_shape=list(out_shape) + list(payload.out_shapes),
        scratch_shapes=list(scratch_shapes) + payload.sem_shapes(),
        input_output_aliases={n_in + i: n_out + o for i, o in payload.aliases.items()},
        compiler_params=_cparams(("arbitrary",) * len(grid)))(*args, *payload.ins)
    return list(outs[:n_out]), list(outs[n_out:])


def _comm(payload, name):
    def body(*refs):
        p_in, p_out = len(payload.ins), len(payload.out_shapes)
        cps = payload.build(refs[:p_in], refs[p_in:p_in + p_out], *refs[p_in + p_out:])
        cps.start()
        cps.wait()

    return list(pl.pallas_call(
        body, name=name, in_specs=[ANY] * len(payload.ins), out_specs=[ANY] * len(payload.out_shapes),
        out_shape=list(payload.out_shapes), scratch_shapes=payload.sem_shapes(),
        input_output_aliases=dict(payload.aliases))(*payload.ins))


def _dot(a, b):
    return jnp.dot(a, b, preferred_element_type=F32)


def _dot_nt(a, b):
    return lax.dot_general(a, b, (((1,), (1,)), ((), ())), preferred_element_type=F32)


def _dot_tn(a, b):
    return lax.dot_general(a, b, (((0,), (0,)), ((), ())), preferred_element_type=F32)


def _split_dot(x, e):
    hi = x.astype(BF16)
    r1 = x - hi.astype(F32)
    mid = r1.astype(BF16)
    lo = (r1 - mid.astype(F32)).astype(BF16)
    return _dot(hi, e) + _dot(mid, e) + _dot(lo, e)


def _rms(x):
    rstd = lax.rsqrt(jnp.mean(x * x, axis=-1, keepdims=True) + EPS)
    return x * rstd, rstd


def _rms_bwd(xhat, rstd, g, dh):
    dxhat = dh * g
    dx = rstd * (dxhat - xhat * jnp.mean(dxhat * xhat, axis=-1, keepdims=True))
    return dx, jnp.sum(dh * xhat, axis=0, keepdims=True)


def _ln(v):
    mu = jnp.mean(v, axis=-1, keepdims=True)
    xc = v - mu
    rstd = lax.rsqrt(jnp.mean(xc * xc, axis=-1, keepdims=True) + EPS)
    return xc * rstd, rstd


def _ln_bwd(vhat, rstd, g, dy):
    dvhat = dy * g
    dv = rstd * (dvhat - jnp.mean(dvhat, axis=-1, keepdims=True)
                 - vhat * jnp.mean(dvhat * vhat, axis=-1, keepdims=True))
    return dv, jnp.sum(dy * vhat, axis=0, keepdims=True), jnp.sum(dy, axis=0, keepdims=True)


_INV_SQRT2 = 0.7071067811865476
_INV_SQRT2PI = 0.3989422804014327


def _gelu(x):
    return 0.5 * x * (1.0 + lax.erf(x * _INV_SQRT2))


def _gelu_grad(x):
    return 0.5 * (1.0 + lax.erf(x * _INV_SQRT2)) + x * jnp.exp(-0.5 * x * x) * _INV_SQRT2PI


def _silu_grad(x):
    s = jax.nn.sigmoid(x)
    return s * (1.0 + x * (1.0 - s))


def _ffn_fwd(x, g, wa, mi, name, payload=None, head=None):
    t, d = x.shape
    f = wa.shape[1]
    tm, tf = 1024, 256
    nc = f // tf
    groups = [slice(k * (tm // 2), (k + 1) * (tm // 2)) for k in range(2)]

    def body(x_ref, g_ref, wgu_ref, wd_ref, *rest):
        if head is None:
            xo_ref, gate_ref, up_ref, act_ref, h_scr, acc_scr = rest
        else:
            fg_ref, tgt_ref, xo_ref, gate_ref, up_ref, act_ref, dfg_ref, loss_ref, h_scr, acc_scr = rest
        c = pl.program_id(1)
        if head is not None:
            @pl.when((c == 0) & (pl.program_id(0) == 0))
            def _():
                dfg_ref[...] = jnp.zeros_like(dfg_ref)
                loss_ref[...] = jnp.zeros_like(loss_ref)

        @pl.when(c == 0)
        def _():
            xhat, _ = _rms(x_ref[...])
            h_scr[...] = (xhat * g_ref[...]).astype(BF16)
            acc_scr[...] = jnp.zeros_like(acc_scr)

        wgu, wd = wgu_ref[...].reshape(2 * tf, d), wd_ref[...]
        for rows in groups:
            gu = _dot_nt(h_scr[rows, :], wgu)
            gate, up = gu[:, :tf], gu[:, tf:]
            gate_ref[rows, :] = gate.astype(BF16)
            up_ref[rows, :] = up.astype(BF16)
            act = (gate * jax.nn.sigmoid(gate) * up).astype(BF16)
            act_ref[rows, :] = act
            acc_scr[rows, :] += _dot(act, wd)

        @pl.when(c == nc - 1)
        def _():
            xo = x_ref[...] + 0.5 * acc_scr[...]
            if head is None:
                xo_ref[...] = xo
            else:
                fg = fg_ref[...]
                xhat, rstd = _rms(xo)
                err = xhat * fg - tgt_ref[...]
                dxn, dfg = _rms_bwd(xhat, rstd, fg, err * (1.0 / d))
                xo_ref[...] = dxn
                dfg_ref[...] += dfg
                loss_ref[...] += jnp.zeros_like(loss_ref) + 0.5 * jnp.sum(jnp.mean(err * err, axis=-1, keepdims=True))

    assert mi % 2 == 0
    row = pl.BlockSpec((tm, d), lambda i, c: (i, 0))
    vec = pl.BlockSpec((1, d), lambda i, c: (0, 0))
    in_specs = [row, vec, pl.BlockSpec((2, tf, d), lambda i, c: (mi // 2, c, 0)),
                pl.BlockSpec((None, tf, d), lambda i, c: (mi + 2, c, 0))]
    out_specs = [row] + [pl.BlockSpec((tm, tf), lambda i, c: (i, c))] * 3
    out_shape = [jax.ShapeDtypeStruct((t, d), F32)] + [jax.ShapeDtypeStruct((t, f), BF16)] * 3
    args = (x, g, wa, wa)
    if head is not None:
        in_specs += [vec, row]
        out_specs += [vec, pl.BlockSpec((1, LANES), lambda i, c: (0, 0))]
        out_shape += [jax.ShapeDtypeStruct((1, d), F32), jax.ShapeDtypeStruct((1, LANES), F32)]
        args += tuple(head)
    return _call(
        body, name, (t // tm, nc), in_specs, out_specs, out_shape,
        [pltpu.VMEM((tm, d), BF16), pltpu.VMEM((tm, d), F32)],
        ("parallel" if head is None else "arbitrary", "arbitrary"), args, payload)


def _ffn_bwd(x, g, dxo, gate, up, wa, mi, name, payload=None):
    t, d = x.shape
    f = wa.shape[1]
    tm, tf = 1024, 256
    nc = f // tf
    groups = [slice(k * (tm // 2), (k + 1) * (tm // 2)) for k in range(2)]

    def body(x_ref, g_ref, dxo_ref, gate_ref, up_ref, wgu_ref, wd_ref,
             dx_ref, dgate_ref, dup_ref, h_ref, dy_ref, dg_ref, acc_scr):
        i, c = pl.program_id(0), pl.program_id(1)

        @pl.when(c == 0)
        def _():
            xhat, _ = _rms(x_ref[...])
            h_ref[...] = (xhat * g_ref[...]).astype(BF16)
            dy_ref[...] = (0.5 * dxo_ref[...]).astype(BF16)
            acc_scr[...] = jnp.zeros_like(acc_scr)

        @pl.when((c == 0) & (i == 0))
        def _():
            dg_ref[...] = jnp.zeros_like(dg_ref)

        wg, wu, wd = wgu_ref[0], wgu_ref[1], wd_ref[...]
        for rows in groups:
            gt = gate_ref[rows, :].astype(F32)
            u = up_ref[rows, :].astype(F32)
            s = jax.nn.sigmoid(gt)
            silu = gt * s
            dact = _dot_nt(dy_ref[rows, :], wd)
            dgate = (dact * u * (s * (1.0 + gt * (1.0 - s)))).astype(BF16)
            dup = (dact * silu).astype(BF16)
            dgate_ref[rows, :] = dgate
            dup_ref[rows, :] = dup
            acc_scr[rows, :] += _dot(dgate, wg) + _dot(dup, wu)

        @pl.when(c == nc - 1)
        def _():
            xhat, rstd = _rms(x_ref[...])
            dxn, dg = _rms_bwd(xhat, rstd, g_ref[...], acc_scr[...])
            dx_ref[...] = dxo_ref[...] + dxn
            dg_ref[...] += dg

    assert mi % 2 == 0
    row = pl.BlockSpec((tm, d), lambda i, c: (i, 0))
    col = pl.BlockSpec((tm, tf), lambda i, c: (i, c))
    vec = pl.BlockSpec((1, d), lambda i, c: (0, 0))
    return _call(
        body, name, (t // tm, nc),
        [row, vec, row, col, col, pl.BlockSpec((2, tf, d), lambda i, c: (mi // 2, c, 0)),
         pl.BlockSpec((None, tf, d), lambda i, c: (mi + 2, c, 0))],
        [row, col, col, row, row, vec],
        [jax.ShapeDtypeStruct((t, d), F32), jax.ShapeDtypeStruct((t, f), BF16),
         jax.ShapeDtypeStruct((t, f), BF16),
         jax.ShapeDtypeStruct((t, d), BF16), jax.ShapeDtypeStruct((t, d), BF16),
         jax.ShapeDtypeStruct((1, d), F32)],
        [pltpu.VMEM((tm, d), F32)],
        ("arbitrary", "arbitrary"), (x, g, dxo, gate, up, wa, wa), payload)


def _ffn_dw(dgate, dup, act, h, dy, name, payload=None):
    t, f = dgate.shape
    d = h.shape[1]
    tk = 512
    tmm = f // 2
    nk = t // tk

    def body(dg_ref, du_ref, a_ref, h_ref, dy_ref, og_ref, ou_ref, od_ref, acc_g, acc_u, acc_d):
        k = pl.program_id(1)

        @pl.when(k == 0)
        def _():
            acc_g[...] = jnp.zeros_like(acc_g)
            acc_u[...] = jnp.zeros_like(acc_u)
            acc_d[...] = jnp.zeros_like(acc_d)

        hv = h_ref[...]
        acc_g[...] += _dot_tn(dg_ref[...], hv)
        acc_u[...] += _dot_tn(du_ref[...], hv)
        acc_d[...] += _dot_tn(a_ref[...], dy_ref[...])

        @pl.when(k == nk - 1)
        def _():
            og_ref[...] = acc_g[...].astype(BF16)
            ou_ref[...] = acc_u[...].astype(BF16)
            od_ref[...] = acc_d[...].astype(BF16)

    col = pl.BlockSpec((tk, tmm), lambda j, k: (k, j))
    row = pl.BlockSpec((tk, d), lambda j, k: (k, 0))
    out = pl.BlockSpec((tmm, d), lambda j, k: (j, 0))
    return _call(
        body, name, (f // tmm, nk), [col, col, col, row, row], [out, out, out],
        [jax.ShapeDtypeStruct((f, d), BF16)] * 3, [pltpu.VMEM((tmm, d), F32)] * 3,
        ("parallel", "arbitrary"), (dgate, dup, act, h, dy), payload)


def _tn_matmul(a, b, name, payload=None):
    t, m = a.shape
    n = b.shape[1]
    tk = 1024
    tmm = m // 2 if (m // 2) % LANES == 0 else m
    nk = t // tk

    def body(a_ref, b_ref, o_ref, acc_scr):
        k = pl.program_id(1)

        @pl.when(k == 0)
        def _():
            acc_scr[...] = jnp.zeros_like(acc_scr)

        acc_scr[...] += _dot_tn(a_ref[...].astype(BF16), b_ref[...].astype(BF16))

        @pl.when(k == nk - 1)
        def _():
            o_ref[...] = acc_scr[...].astype(BF16)

    (out,), p_outs = _call(
        body, name, (m // tmm, nk),
        [pl.BlockSpec((tk, tmm), lambda j, k: (k, j)), pl.BlockSpec((tk, n), lambda j, k: (k, 0))],
        [pl.BlockSpec((tmm, n), lambda j, k: (j, 0))],
        [jax.ShapeDtypeStruct((m, n), BF16)],
        [pltpu.VMEM((tmm, n), F32)],
        ("parallel", "arbitrary"), (a, b), payload)
    return out, p_outs


def _lane_ids(shape):
    return lax.broadcasted_iota(jnp.int32, shape, 1)


def _tril(w):
    r = lax.broadcasted_iota(jnp.int32, w.shape, 0)
    c = lax.broadcasted_iota(jnp.int32, w.shape, 1)
    return jnp.where(r >= c, w, 0.0)


def _shift_down(x, k):
    return x if k == 0 else pltpu.roll(x, k, 0)


def _shift_up(x, k):
    return x if k == 0 else pltpu.roll(x, x.shape[0] - k, 0)


def _sub_tile_shifts(ext, shift):
    return [shift(ext, b) for b in range(8)]


def _tap(shifted, j, n_out, down):
    a, b = divmod(j, 8)
    r0 = HALO - 8 * a if down else 8 * a
    return shifted[b][r0:r0 + n_out]


def _depthwise(shifted, w, n_out, down):
    acc = None
    for j in range(CONV_WIDTH):
        term = _tap(shifted, j, n_out, down) * w[CONV_WIDTH - 1 - j:CONV_WIDTH - j]
        acc = term if acc is None else acc + term
    return acc


def _conv_wgrad(shifted, dhc, n_out):
    return [jnp.sum(_tap(shifted, CONV_WIDTH - 1 - k, n_out, True) * dhc, axis=0, keepdims=True)
            for k in range(CONV_WIDTH)]


def _window_sums(ext, shift):
    s2 = ext + shift(ext, 1)
    s4 = s2 + shift(s2, 2)
    s8 = s4 + shift(s4, 4)
    s16 = s8 + shift(s8, 8)
    grp = _lane_ids(ext.shape) // HEAD_DIM
    return jnp.where(grp == 0, s2, jnp.where(grp == 1, s4, jnp.where(grp == 2, s8, s16)))


def _pool_count(t0, n, width):
    pos = (lax.broadcasted_iota(jnp.int32, (n, width), 0) + (t0 + 1)).astype(F32)
    grp = _lane_ids((n, width)) // HEAD_DIM
    win = jnp.where(grp == 0, 2.0, jnp.where(grp == 1, 4.0, jnp.where(grp == 2, 8.0, 16.0)))
    return jnp.minimum(pos, win)


def _block_diag(pw):
    gn, cg, _ = pw.shape
    rows = []
    for gi in range(gn):
        parts = [pw[gi] if gj == gi else jnp.zeros((cg, cg), pw.dtype) for gj in range(gn)]
        rows.append(jnp.concatenate(parts, axis=1))
    return jnp.concatenate(rows, axis=0)


def _head_pair_mix(w_even, w_odd, v):
    lo = _lane_ids((CHUNK, LANES)) < HEAD_DIM
    return jnp.where(lo, _dot(w_even, v), _dot(w_odd, v))


def _mixer_fwd(x, g, wb, wc, p, name, payload=None):
    t, d = x.shape
    d_in = wb.shape[1]
    sgu = p["sgu_ln_g"].shape[1]
    pool = p["pool_scale"].shape[1]
    d_mix = 2 * sgu + pool
    tm = 512
    n_i = t // tm
    hb = tm // HALO

    def body(x_ref, xp_ref, g_ref, wi_ref, wo_ref, lng_ref, lnb_ref, ws_ref, bs_ref, cw_ref, cb_ref, clg_ref,
             clb_ref, bd_ref, ps_ref, xo_ref, z_ref, hc_ref, cat_ref):
        i = pl.program_id(0)
        first = i == 0
        gain, wi = g_ref[...], wi_ref[...]

        def project(xv):
            xhat, _ = _rms(xv)
            return _dot_nt((xhat * gain).astype(BF16), wi)

        z_main = project(x_ref[...])
        z_ref[...] = z_main
        z_prev = jnp.where(first, 0.0, project(xp_ref[...]))

        lng, lnb = lng_ref[...], lnb_ref[...]
        wt = [_tril(ws_ref[h]).astype(BF16) for h in range(sgu // HEAD_DIM)]
        for n in range(tm // CHUNK):
            rows = slice(n * CHUNK, (n + 1) * CHUNK)
            u = _gelu(z_main[rows, 0:sgu])
            vhat, _ = _ln(_gelu(z_main[rows, sgu:2 * sgu]))
            vn = (vhat * lng + lnb).astype(BF16)
            for gp in range(sgu // LANES):
                ls = slice(gp * LANES, (gp + 1) * LANES)
                mixed = _head_pair_mix(wt[2 * gp], wt[2 * gp + 1], vn[:, ls]) + bs_ref[:, ls]
                cat_ref[rows, ls] = (u[:, ls] * mixed).astype(BF16)

        def glu(zz):
            return zz[:, 2 * sgu:3 * sgu] * jax.nn.sigmoid(zz[:, 3 * sgu:4 * sgu])

        ext = jnp.concatenate([glu(z_prev), glu(z_main)], axis=0)
        hc = _depthwise(_sub_tile_shifts(ext, _shift_down), cw_ref[...], tm, True) + cb_ref[...]
        hc_ref[...] = hc
        hhat, _ = _ln(hc)
        bn = hhat * clg_ref[...] + clb_ref[...]
        cat_ref[:, sgu:2 * sgu] = (bn * jax.nn.sigmoid(bn)).astype(BF16)

        pext = jnp.concatenate([z_prev[:, 4 * sgu:], z_main[:, 4 * sgu:]], axis=0)
        sums = _window_sums(pext, _shift_down)[HALO:]
        pooled = sums / _pool_count(i * tm, tm, pool) - z_main[:, 4 * sgu:]
        mixed_c = _dot(pooled.astype(BF16), bd_ref[...].astype(BF16))
        cat_ref[:, 2 * sgu:] = (mixed_c * ps_ref[...]).astype(BF16)

        xo_ref[...] = x_ref[...] + _dot(cat_ref[...], wo_ref[...])

    def vec(n):
        return pl.BlockSpec((1, n), lambda i: (0, 0))

    return _call(
        body, name, (n_i,),
        [pl.BlockSpec((tm, d), lambda i: (i, 0)),
         pl.BlockSpec((HALO, d), lambda i: (jnp.maximum(i * hb - 1, 0), 0)),
         vec(d),
         pl.BlockSpec((None, d_in, d), lambda i: (0, 0, 0)), pl.BlockSpec((None, d_mix, d), lambda i: (0, 0, 0)),
         vec(sgu), vec(sgu),
         pl.BlockSpec(p["w_spatial"].shape, lambda i: (0, 0, 0)),
         pl.BlockSpec((CHUNK, sgu), lambda i: (0, 0)),
         pl.BlockSpec((CONV_WIDTH, sgu), lambda i: (0, 0)),
         vec(sgu), vec(sgu), vec(sgu),
         pl.BlockSpec((pool, pool), lambda i: (0, 0)), vec(pool)],
        [pl.BlockSpec((tm, d), lambda i: (i, 0)), pl.BlockSpec((tm, d_in), lambda i: (i, 0)),
         pl.BlockSpec((tm, sgu), lambda i: (i, 0)), pl.BlockSpec((tm, d_mix), lambda i: (i, 0))],
        [jax.ShapeDtypeStruct((t, d), F32), jax.ShapeDtypeStruct((t, d_in), F32),
         jax.ShapeDtypeStruct((t, sgu), F32), jax.ShapeDtypeStruct((t, d_mix), BF16)], [], ("parallel",),
        (x, x, g, wb, wc, p["sgu_ln_g"], p["sgu_ln_b"], p["w_spatial"], p["bs_full"], p["conv_w"], p["conv_b"],
         p["conv_ln_g"], p["conv_ln_b"], p["bd"], p["pool_scale"]), payload)


_R_SGU_G, _R_SGU_B, _R_CONV_B, _R_CLN_G, _R_CLN_B, _R_CONV_W = 0, 1, 2, 3, 4, 8
_R384_ROWS = 40


def _mixer_bwd(x, g, z, hc_saved, dxo, wb, wc, p, name, payload=None):
    t, d_in = z.shape
    d = x.shape[1]
    sgu = p["sgu_ln_g"].shape[1]
    pool = p["pool_scale"].shape[1]
    d_mix = 2 * sgu + pool
    n_head = sgu // HEAD_DIM
    tm = 512
    n_i = t // tm
    hb = tm // HALO

    def body(x_ref, g_ref, wi_ref, z_ref, zp_ref, zn_ref, dxo_ref, dxon_ref, wo_ref, hc_ref, hcn_ref, lng_ref,
             lnb_ref, ws_ref, bs_ref, cw_ref, clg_ref, clb_ref, bd_ref, ps_ref,
             dx_ref, dz_ref, hm_ref, dgm_ref, g384_ref, gws_ref, gpool_ref, dbs_scr):
        i = pl.program_id(0)
        first, last = i == 0, i == n_i - 1

        @pl.when(first)
        def _():
            dgm_ref[...] = jnp.zeros_like(dgm_ref)
            g384_ref[...] = jnp.zeros_like(g384_ref)
            gws_ref[...] = jnp.zeros_like(gws_ref)
            gpool_ref[...] = jnp.zeros_like(gpool_ref)
            dbs_scr[...] = jnp.zeros_like(dbs_scr)

        z_main = z_ref[...]
        z_prev = jnp.where(first, 0.0, zp_ref[...])
        z_next = jnp.where(last, 0.0, zn_ref[...])
        wo = wo_ref[...]
        dc_main = _dot_nt(dxo_ref[...].astype(BF16), wo)
        dc_next = jnp.where(last, 0.0, _dot_nt(dxon_ref[...].astype(BF16), wo))

        lng, lnb = lng_ref[...], lnb_ref[...]
        wt = [_tril(ws_ref[h]) for h in range(n_head)]
        wt_b = [w.astype(BF16) for w in wt]
        wtt_b = [w.T.astype(BF16) for w in wt]
        lo = _lane_ids((CHUNK, LANES)) < HEAD_DIM
        d_lng = jnp.zeros((1, sgu), F32)
        d_lnb = jnp.zeros((1, sgu), F32)
        dws = [jnp.zeros((CHUNK, CHUNK), F32) for _ in range(n_head)]
        for n in range(tm // CHUNK):
            rows = slice(n * CHUNK, (n + 1) * CHUNK)
            au, av = z_main[rows, 0:sgu], z_main[rows, sgu:2 * sgu]
            u = _gelu(au)
            vhat, vrstd = _ln(_gelu(av))
            vn = (vhat * lng + lnb).astype(BF16)
            da = dc_main[rows, 0:sgu]
            dmixed = da * u
            dbs_scr[...] += dmixed
            dvn_parts, du_parts = [], []
            for gp in range(sgu // LANES):
                ls = slice(gp * LANES, (gp + 1) * LANES)
                vn_g = vn[:, ls]
                mixed = _head_pair_mix(wt_b[2 * gp], wt_b[2 * gp + 1], vn_g) + bs_ref[:, ls]
                du_parts.append(da[:, ls] * mixed)
                dm_g = dmixed[:, ls]
                dm_b = dm_g.astype(BF16)
                dvn_parts.append(jnp.where(lo, _dot(wtt_b[2 * gp], dm_b), _dot(wtt_b[2 * gp + 1], dm_b)))
                dws[2 * gp] = dws[2 * gp] + _dot_nt(jnp.where(lo, dm_g, 0.0).astype(BF16), vn_g)
                dws[2 * gp + 1] = dws[2 * gp + 1] + _dot_nt(jnp.where(lo, 0.0, dm_g).astype(BF16), vn_g)
            dvn = jnp.concatenate(dvn_parts, axis=1)
            du = jnp.concatenate(du_parts, axis=1)
            dv, dg_n, db_n = _ln_bwd(vhat, vrstd, lng, dvn)
            d_lng = d_lng + dg_n
            d_lnb = d_lnb + db_n
            dz_ref[rows, 0:sgu] = (du * _gelu_grad(au)).astype(BF16)
            dz_ref[rows, sgu:2 * sgu] = (dv * _gelu_grad(av)).astype(BF16)
        for h in range(n_head):
            gws_ref[h] += _tril(dws[h])
        g384_ref[_R_SGU_G:_R_SGU_G + 1, :] += d_lng
        g384_ref[_R_SGU_B:_R_SGU_B + 1, :] += d_lnb

        clg = clg_ref[...]
        bcols = slice(2 * sgu, 4 * sgu)
        zb = jnp.concatenate([z_prev[:, bcols], z_main[:, bcols], z_next[:, bcols]], axis=0)
        bval, bgate = zb[:, 0:sgu], zb[:, sgu:2 * sgu]
        sg = jax.nn.sigmoid(bgate)
        hglu = bval * sg
        n_out = tm + HALO
        hglu_shifts = _sub_tile_shifts(hglu, _shift_down)
        cw = cw_ref[...]
        hc = jnp.concatenate([hc_ref[...], jnp.where(last, 0.0, hcn_ref[...])], axis=0)
        hhat, hrstd = _ln(hc)
        bn = hhat * clg + clb_ref[...]
        db = jnp.concatenate([dc_main[:, sgu:2 * sgu], dc_next[:, sgu:2 * sgu]], axis=0)
        dbn = db * _silu_grad(bn)
        dhc_all, _, _ = _ln_bwd(hhat, hrstd, clg, dbn)
        dbn_m, hhat_m, dhc = dbn[:tm], hhat[:tm], dhc_all[:tm]
        g384_ref[_R_CLN_G:_R_CLN_G + 1, :] += jnp.sum(dbn_m * hhat_m, axis=0, keepdims=True)
        g384_ref[_R_CLN_B:_R_CLN_B + 1, :] += jnp.sum(dbn_m, axis=0, keepdims=True)
        g384_ref[_R_CONV_B:_R_CONV_B + 1, :] += jnp.sum(dhc, axis=0, keepdims=True)
        wrows = _conv_wgrad(hglu_shifts, dhc, tm)
        for k in range(CONV_WIDTH):
            g384_ref[_R_CONV_W + k:_R_CONV_W + k + 1, :] += wrows[k]
        dhglu = _depthwise(_sub_tile_shifts(dhc_all, _shift_up), cw, tm, False)
        bval_m, sg_m = bval[HALO:HALO + tm], sg[HALO:HALO + tm]
        dz_ref[:, 2 * sgu:3 * sgu] = (dhglu * sg_m).astype(BF16)
        dz_ref[:, 3 * sgu:4 * sgu] = (dhglu * bval_m * sg_m * (1.0 - sg_m)).astype(BF16)

        bd_b = bd_ref[...].astype(BF16)
        ps = ps_ref[...]
        p_main = z_main[:, 4 * sgu:]
        pext = jnp.concatenate([z_prev[:, 4 * sgu:], p_main], axis=0)
        cnt = _pool_count(i * tm, n_out, pool)
        pooled = _window_sums(pext, _shift_down)[HALO:] / cnt[:tm] - p_main
        pooled_b = pooled.astype(BF16)
        dcc = jnp.concatenate([dc_main[:, 2 * sgu:], dc_next[:, 2 * sgu:]], axis=0)
        dmix_c = dcc * ps
        mixed_c = _dot(pooled_b, bd_b)
        grp_r = lax.broadcasted_iota(jnp.int32, (pool, pool), 0) // HEAD_DIM
        grp_c = lax.broadcasted_iota(jnp.int32, (pool, pool), 1) // HEAD_DIM
        gpool_ref[0:pool, :] += jnp.where(grp_r == grp_c, _dot_tn(pooled_b, dmix_c[:tm].astype(BF16)), 0.0)
        gpool_ref[pool:pool + 1, :] += jnp.sum(dcc[:tm] * mixed_c, axis=0, keepdims=True)
        dpooled = _dot_nt(dmix_c.astype(BF16), bd_b)
        q = dpooled / cnt
        dp = _window_sums(q, _shift_up)[:tm] - dpooled[:tm]
        dz_ref[:, 4 * sgu:] = dp.astype(BF16)

        gain = g_ref[...]
        xhat, rstd = _rms(x_ref[...])
        hm_ref[...] = (xhat * gain).astype(BF16)
        dxn, dgm = _rms_bwd(xhat, rstd, gain, _dot(dz_ref[...], wi_ref[...]))
        dx_ref[...] = dxo_ref[...] + dxn
        dgm_ref[...] += dgm

        @pl.when(last)
        def _():
            r = lax.broadcasted_iota(jnp.int32, (sgu, LANES), 0)
            c = lax.broadcasted_iota(jnp.int32, (sgu, LANES), 1)
            sel = (r // HEAD_DIM == c).astype(BF16)
            gws_ref[n_head] = _split_dot(dbs_scr[...], sel)

    def vec(n):
        return pl.BlockSpec((1, n), lambda i: (0, 0))

    def prev_map(i):
        return (jnp.maximum(i * hb - 1, 0), 0)

    def next_map(i):
        return (jnp.minimum((i + 1) * hb, n_i * hb - 1), 0)

    return _call(
        body, name, (n_i,),
        [pl.BlockSpec((tm, d), lambda i: (i, 0)), pl.BlockSpec((1, d), lambda i: (0, 0)),
         pl.BlockSpec((None, d_in, d), lambda i: (0, 0, 0)),
         pl.BlockSpec((tm, d_in), lambda i: (i, 0)),
         pl.BlockSpec((HALO, d_in), prev_map), pl.BlockSpec((HALO, d_in), next_map),
         pl.BlockSpec((tm, d), lambda i: (i, 0)), pl.BlockSpec((HALO, d), next_map),
         pl.BlockSpec((None, d_mix, d), lambda i: (0, 0, 0)),
         pl.BlockSpec((tm, sgu), lambda i: (i, 0)), pl.BlockSpec((HALO, sgu), next_map),
         vec(sgu), vec(sgu),
         pl.BlockSpec(p["w_spatial"].shape, lambda i: (0, 0, 0)),
         pl.BlockSpec((CHUNK, sgu), lambda i: (0, 0)),
         pl.BlockSpec((CONV_WIDTH, sgu), lambda i: (0, 0)),
         vec(sgu), vec(sgu),
         pl.BlockSpec((pool, pool), lambda i: (0, 0)), vec(pool)],
        [pl.BlockSpec((tm, d), lambda i: (i, 0)), pl.BlockSpec((tm, d_in), lambda i: (i, 0)),
         pl.BlockSpec((tm, d), lambda i: (i, 0)), pl.BlockSpec((1, d), lambda i: (0, 0)),
         pl.BlockSpec((_R384_ROWS, sgu), lambda i: (0, 0)),
         pl.BlockSpec((n_head + 1, CHUNK, CHUNK), lambda i: (0, 0, 0)),
         pl.BlockSpec((pool + 8, pool), lambda i: (0, 0))],
        [jax.ShapeDtypeStruct((t, d), F32), jax.ShapeDtypeStruct((t, d_in), BF16),
         jax.ShapeDtypeStruct((t, d), BF16), jax.ShapeDtypeStruct((1, d), F32),
         jax.ShapeDtypeStruct((_R384_ROWS, sgu), F32),
         jax.ShapeDtypeStruct((n_head + 1, CHUNK, CHUNK), F32),
         jax.ShapeDtypeStruct((pool + 8, pool), F32)],
        [pltpu.VMEM((CHUNK, sgu), F32)], ("arbitrary",),
        (x, g, wb, z, z, z, dxo, dxo, wc, hc_saved, hc_saved, p["sgu_ln_g"], p["sgu_ln_b"], p["w_spatial"],
         p["bs_full"], p["conv_w"], p["conv_ln_g"], p["conv_ln_b"], p["bd"], p["pool_scale"]), payload)


def _all_gather(arrs, name, extra=None):
    gather = _GatherIci(arrs)
    n = len(arrs)
    forward = _GatherForward([jax.ShapeDtypeStruct(s.shape, s.dtype) for s in gather.out_shapes])
    x_in = len(extra.ins) if extra else 0
    x_out = len(extra.out_shapes) if extra else 0

    def body(*refs):
        ins, x_ins = refs[:n], refs[n:n + x_in]
        outs, x_outs = refs[n + x_in:2 * n + x_in], refs[2 * n + x_in:2 * n + x_in + x_out]
        sems = refs[2 * n + x_in + x_out:]
        first = gather.build(ins, outs, *sems[0:3])
        first.start()
        if extra:
            beside = extra.build(x_ins, x_outs, *sems[6:9])
            beside.start()
        first.wait()
        second = forward.build(outs, outs, *sems[3:6])
        second.start()
        second.wait()
        if extra:
            beside.wait()

    outs = pl.pallas_call(
        body, name=name,
        in_specs=[ANY] * (n + x_in), out_specs=[ANY] * (n + x_out),
        out_shape=list(gather.out_shapes) + (list(extra.out_shapes) if extra else []),
        scratch_shapes=gather.sem_shapes() + forward.sem_shapes() + (extra.sem_shapes() if extra else []),
    )(*arrs, *(extra.ins if extra else []))
    return list(outs[:n]), list(outs[n:])


def _all_gather_relayed(arrs, name):
    n = len(arrs)
    n_pairs = 8

    def body(*refs):
        ins, outs = refs[:n], refs[n:2 * n]
        send_sems, recv_sems, local_sems = refs[2 * n:]
        x, y, c = _position()
        sib, xn, yn = (x, y, 1 - c), (1 - x, y, c), (x, 1 - y, c)

        def rows(a, px, py, pc, half=None):
            r = ins[a].shape[1]
            base = (4 * px + 2 * py + pc) * r
            if half is None:
                return outs[a].at[:, pl.ds(base, r), :]
            return outs[a].at[:, pl.ds(base + half * (r // 2), r // 2), :]

        def send(a, k, src, dst, to):
            return _remote(src, dst, send_sems, recv_sems, a * n_pairs + k, to)

        def arrived(a, k, land, sender):
            _remote(land, land, send_sems, recv_sems, a * n_pairs + k, sender).wait_recv()

        own = [pltpu.make_async_copy(ins[a], rows(a, x, y, c), local_sems.at[a]) for a in range(n)]
        first = [send(a, k, ins[a], rows(a, x, y, c), to) for a in range(n) for k, to in enumerate((sib, xn, yn))]
        for cp in own + first:
            cp.start()
        for a in range(n):
            arrived(a, 1, rows(a, *xn), xn)
            arrived(a, 2, rows(a, *yn), yn)
        second = []
        for a in range(n):
            second += [send(a, 3, rows(a, *xn, half=0), rows(a, *xn, half=0), yn),
                       send(a, 4, rows(a, *yn, half=1), rows(a, *yn, half=1), xn),
                       send(a, 5, rows(a, *xn), rows(a, *xn), sib),
                       send(a, 6, rows(a, *yn), rows(a, *yn), sib)]
        for cp in second:
            cp.start()
        for a in range(n):
            arrived(a, 3, rows(a, 1 - x, 1 - y, c, half=0), yn)
            arrived(a, 4, rows(a, 1 - x, 1 - y, c, half=1), xn)
        third = [send(a, 7, rows(a, 1 - x, 1 - y, c), rows(a, 1 - x, 1 - y, c), sib) for a in range(n)]
        for cp in third:
            cp.start()
        for a in range(n):
            arrived(a, 0, rows(a, *sib), sib)
            arrived(a, 5, rows(a, 1 - x, y, 1 - c), sib)
            arrived(a, 6, rows(a, x, 1 - y, 1 - c), sib)
            arrived(a, 7, rows(a, 1 - x, 1 - y, 1 - c), sib)
        for cp in first + second + third:
            cp.wait_send()
        for cp in own:
            cp.wait()

    return list(pl.pallas_call(
        body, name=name, in_specs=[ANY] * n, out_specs=[ANY] * n,
        out_shape=[jax.ShapeDtypeStruct((a.shape[0], N_DEV * a.shape[1], a.shape[2]), a.dtype) for a in arrs],
        scratch_shapes=[pltpu.SemaphoreType.DMA((n_pairs * n,)), pltpu.SemaphoreType.DMA((n_pairs * n,)),
                        pltpu.SemaphoreType.DMA((n,))],
    )(*arrs))


def _pair_sums(grads, recvs, cidx, name):
    n = len(grads)

    def body(c_ref, *refs):
        for g_ref, r_ref, o_ref in zip(refs[:n], refs[n:2 * n], refs[2 * n:]):
            o_ref[...] = (g_ref[...].astype(F32) + r_ref[...].astype(F32)).astype(BF16)

    shapes = [(g.shape[0] // N_DEV, g.shape[1]) for g in grads]
    return list(pl.pallas_call(
        body, name=name,
        grid_spec=pltpu.PrefetchScalarGridSpec(
            num_scalar_prefetch=1, grid=(N_CHIP,),
            in_specs=[pl.BlockSpec(s, lambda q, c: (2 * q + c[0], 0)) for s in shapes]
            + [pl.BlockSpec(s, lambda q, c: (q, 0)) for s in shapes],
            out_specs=[pl.BlockSpec(s, lambda q, c: (q, 0)) for s in shapes]),
        out_shape=[jax.ShapeDtypeStruct((N_CHIP * r, cols), BF16) for r, cols in shapes],
        compiler_params=_cparams(("parallel",)),
    )(cidx, *grads, *recvs))


def _sum_blocks(parts, nblk, name):
    r = parts.shape[0] // nblk
    cols = parts.shape[1]

    def body(p_ref, o_ref):
        acc = p_ref[0:r, :].astype(F32)
        for q in range(1, nblk):
            acc = acc + p_ref[q * r:(q + 1) * r, :].astype(F32)
        o_ref[...] = acc

    return pl.pallas_call(
        body, name=name,
        out_shape=jax.ShapeDtypeStruct((r, cols), F32),
        compiler_params=_cparams(),
    )(parts)


def _adamw_math(w, g, m, v):
    m = ADAM_B1 * m + (1.0 - ADAM_B1) * g
    v = ADAM_B2 * v + (1.0 - ADAM_B2) * (g * g)
    m_hat = m / (1.0 - ADAM_B1 ** ADAM_STEP)
    v_hat = v / (1.0 - ADAM_B2 ** ADAM_STEP)
    delta = -ADAM_LR * (m_hat / (jnp.sqrt(v_hat) + ADAM_EPS) + ADAM_WD * w)
    return delta, m, v


def _finish_sharded(parts, w, m, v, name):
    depth, rr, cw = w.shape

    def body(*refs):
        p_refs = refs[:depth]
        w_ref, m_ref, v_ref, g_ref, d_ref, mo_ref, vo_ref = refs[depth:]
        l = pl.program_id(0)
        for k in range(depth):
            @pl.when(l == k)
            def _(p_ref=p_refs[k]):
                r = p_ref.shape[0] // N_CHIP
                acc = p_ref[0:r, :].astype(F32)
                for q in range(1, N_CHIP):
                    acc = acc + p_ref[q * r:(q + 1) * r, :].astype(F32)
                g_ref[...] = acc
                d_ref[...], mo_ref[...], vo_ref[...] = _adamw_math(w_ref[...], acc, m_ref[...], v_ref[...])

    blk = pl.BlockSpec((None, rr, cw), lambda l: (l, 0, 0))
    return pl.pallas_call(
        body, name=name, grid=(depth,),
        in_specs=[pl.BlockSpec(p.shape, lambda l: (0, 0)) for p in parts] + [blk] * 3, out_specs=[blk] * 4,
        out_shape=[jax.ShapeDtypeStruct(w.shape, F32)] * 4,
        compiler_params=_cparams(("arbitrary",)),
    )(*parts, w, m, v)


def _adamw_small(ws, gs, ms, vs, name):
    n = len(ws)

    def body(*refs):
        for k in range(n):
            w_ref, g_ref, m_ref, v_ref = (refs[j * n + k] for j in range(4))
            d_ref, mo_ref, vo_ref = (refs[(4 + j) * n + k] for j in range(3))
            d_ref[...], mo_ref[...], vo_ref[...] = _adamw_math(w_ref[...], g_ref[...], m_ref[...], v_ref[...])

    shapes = [jax.ShapeDtypeStruct(w.shape, F32) for w in ws]
    return pl.pallas_call(
        body, name=name, out_shape=shapes * 3, compiler_params=_cparams(),
    )(*ws, *gs, *ms, *vs)


def kernel(x, ffn1_norm, ffn1_w_gate, ffn1_w_up, ffn1_w_down, mix_norm, w_in, sgu_ln_g, sgu_ln_b, w_spatial, b_spatial, conv_w, conv_b, conv_ln_g, conv_ln_b, pool_w, pool_scale, w_out, ffn2_norm, ffn2_w_gate, ffn2_w_up, ffn2_w_down, final_norm, loss_target, m_ffn1_norm, m_ffn1_w_gate, m_ffn1_w_up, m_ffn1_w_down, m_mix_norm, m_w_in, m_sgu_ln_g, m_sgu_ln_b, m_w_spatial, m_b_spatial, m_conv_w, m_conv_b, m_conv_ln_g, m_conv_ln_b, m_pool_w, m_pool_scale, m_w_out, m_ffn2_norm, m_ffn2_w_gate, m_ffn2_w_up, m_ffn2_w_down, m_final_norm, v_ffn1_norm, v_ffn1_w_gate, v_ffn1_w_up, v_ffn1_w_down, v_mix_norm, v_w_in, v_sgu_ln_g, v_sgu_ln_b, v_w_spatial, v_b_spatial, v_conv_w, v_conv_b, v_conv_ln_g, v_conv_ln_b, v_pool_w, v_pool_scale, v_w_out, v_ffn2_norm, v_ffn2_w_gate, v_ffn2_w_up, v_ffn2_w_down, v_final_norm):
    names = ["ffn1_norm", "ffn1_w_gate", "ffn1_w_up", "ffn1_w_down", "mix_norm", "w_in", "sgu_ln_g", "sgu_ln_b",
             "w_spatial", "b_spatial", "conv_w", "conv_b", "conv_ln_g", "conv_ln_b", "pool_w", "pool_scale",
             "w_out", "ffn2_norm", "ffn2_w_gate", "ffn2_w_up", "ffn2_w_down", "final_norm"]
    W = dict(zip(names, [ffn1_norm, ffn1_w_gate, ffn1_w_up, ffn1_w_down, mix_norm, w_in, sgu_ln_g, sgu_ln_b,
                         w_spatial, b_spatial, conv_w, conv_b, conv_ln_g, conv_ln_b, pool_w, pool_scale, w_out,
                         ffn2_norm, ffn2_w_gate, ffn2_w_up, ffn2_w_down, final_norm]))
    M = dict(zip(names, [m_ffn1_norm, m_ffn1_w_gate, m_ffn1_w_up, m_ffn1_w_down, m_mix_norm, m_w_in, m_sgu_ln_g,
                         m_sgu_ln_b, m_w_spatial, m_b_spatial, m_conv_w, m_conv_b, m_conv_ln_g, m_conv_ln_b,
                         m_pool_w, m_pool_scale, m_w_out, m_ffn2_norm, m_ffn2_w_gate, m_ffn2_w_up, m_ffn2_w_down,
                         m_final_norm]))
    V = dict(zip(names, [v_ffn1_norm, v_ffn1_w_gate, v_ffn1_w_up, v_ffn1_w_down, v_mix_norm, v_w_in, v_sgu_ln_g,
                         v_sgu_ln_b, v_w_spatial, v_b_spatial, v_conv_w, v_conv_b, v_conv_ln_g, v_conv_ln_b,
                         v_pool_w, v_pool_scale, v_w_out, v_ffn2_norm, v_ffn2_w_gate, v_ffn2_w_up, v_ffn2_w_down,
                         v_final_norm]))

    depth, d = ffn1_norm.shape
    t = x.shape[1]
    sgu = sgu_ln_g.shape[1]
    pool = pool_scale.shape[1]
    n_head = sgu // HEAD_DIM
    cw_shard = conv_w.shape[2]
    xs = x.reshape(t, d)
    target = loss_target.reshape(t, d)

    def tr(w):
        return jnp.swapaxes(w, 1, 2).astype(BF16)

    ffn_shards = [[jnp.stack([tr(ffn1_w_gate)[l], tr(ffn1_w_up)[l], ffn1_w_down[l].astype(BF16)]),
                   jnp.stack([tr(ffn2_w_gate)[l], tr(ffn2_w_up)[l], ffn2_w_down[l].astype(BF16)])]
                  for l in range(depth)]
    win_shards = [tr(w_in)[l:l + 1] for l in range(depth)]
    wout_shards = [w_out[l:l + 1].astype(BF16) for l in range(depth)]
    cw_rows = depth * CONV_WIDTH
    cw_pad = -cw_rows % 8
    cw_send = jnp.pad(conv_w.reshape(cw_rows, cw_shard), ((0, cw_pad), (0, 0)))[None]
    wffn, wb, wc = {}, {}, {}
    wffn[(0, 0)], wb[0], wc[0], cwg = _all_gather_relayed(
        [ffn_shards[0][0], win_shards[0], wout_shards[0], cw_send], "ag_first")
    conv_w_full = cwg.reshape(N_DEV, cw_rows + cw_pad, cw_shard)[:, :cw_rows].reshape(
        N_DEV, depth, CONV_WIDTH, cw_shard).transpose(1, 2, 0, 3).reshape(depth, CONV_WIDTH, N_DEV * cw_shard)

    def mixer_params(l):
        return dict(
            sgu_ln_g=sgu_ln_g[l:l + 1], sgu_ln_b=sgu_ln_b[l:l + 1], w_spatial=w_spatial[l],
            bs_full=jnp.repeat(b_spatial[l].T, HEAD_DIM, axis=1),
            conv_w=conv_w_full[l], conv_b=conv_b[l:l + 1], conv_ln_g=conv_ln_g[l:l + 1],
            conv_ln_b=conv_ln_b[l:l + 1], bd=_block_diag(pool_w[l]), pool_scale=pool_scale[l:l + 1])

    saved = []
    cur = xs
    for l in range(depth):
        p = mixer_params(l)
        x0 = cur
        more = l + 1 < depth
        (x1, gate1, up1, act1), part = _ffn_fwd(x0, ffn1_norm[l:l + 1], wffn[(l, 0)], 0, f"ffn1_fwd_{l}",
                                          _GatherIci([ffn_shards[l][1]]))
        riding = [_GatherForward(part)] + ([_GatherIci([win_shards[l + 1], wout_shards[l + 1]])] if more else [])
        (x2, z, hc, cat), part = _mixer_fwd(x1, mix_norm[l:l + 1], wb[l], wc[l], p, f"mixer_fwd_{l}", _Merged(riding))
        wffn[(l, 1)] = part[0]
        riding = [_GatherIci([ffn_shards[l + 1][0]]), _GatherForward(part[1:])] if more else []
        outs, part = _ffn_fwd(x2, ffn2_norm[l:l + 1], wffn[(l, 1)], 0, f"ffn2_fwd_{l}",
                              _Merged(riding) if more else None,
                              None if more else (final_norm.reshape(1, d), target))
        if more:
            cur, gate2, up2, act2 = outs
            wb[l + 1], wc[l + 1] = part[1:]
            (wffn[(l + 1, 0)],) = _comm(_GatherForward(part[:1]), f"ag_forward_{l + 1}")
        else:
            dx, gate2, up2, act2, d_final, loss_part = outs
        saved.append((p, x0, gate1, up1, act1, x1, z, hc, cat, x2, gate2, up2, act2))

    cidx = lax.axis_index("c").astype(jnp.int32).reshape(1)
    from_chips = {}
    to_pair, to_chip = [], []
    small = []

    def pair_payload():
        return _PairExchange([g for _, g in to_pair]) if to_pair else None

    def pair_done(received):
        if to_pair:
            (nm, l), _ = to_pair[0]
            sums = _pair_sums([g for _, g in to_pair], list(received), cidx, f"rs_pair_sum_{nm}_{l}")
            to_chip.extend((key, s) for (key, _), s in zip(to_pair, sums))
        to_pair.clear()

    def take_chip():
        items = list(to_chip)
        to_chip.clear()
        return items

    def chip_payload(items):
        return _ChipExchange([s for _, s in items]) if items else None

    def chip_done(items, landed):
        for (key, _), o in zip(items, landed):
            from_chips[key] = o

    def ffn_weight_grads(prefix, l, dgate, dup, act, h, dy):
        items = take_chip()
        items, later = items[:2], items[2:]
        to_chip.extend(later)
        grads3, landed = _ffn_dw(dgate, dup, act, h, dy, f"dw_{prefix}_{l}", chip_payload(items))
        chip_done(items, landed)
        to_pair.extend(((f"{prefix}_{nm}", l), g) for nm, g in zip(("w_gate", "w_up", "w_down"), grads3))

    for l in reversed(range(depth)):
        p, x0, gate1, up1, act1, x1, z, hc, cat, x2, gate2, up2, act2 = saved[l]
        (dx, dgate, dup, h, dy, dg_ffn2), received = _ffn_bwd(
            x2, ffn2_norm[l:l + 1], dx, gate2, up2, wffn[(l, 1)], 0, f"ffn2_bwd_{l}", pair_payload())
        pair_done(received)
        ffn_weight_grads("ffn2", l, dgate, dup, act2, h, dy)
        g_out, received = _tn_matmul(cat, dx, f"dw_w_out_{l}", pair_payload())
        pair_done(received)
        items = take_chip()
        items, later = items[:3], items[3:]
        to_chip.extend(later)
        (dx, dz, hm, dg_mix, g384, gws, gpool), landed = _mixer_bwd(
            x1, mix_norm[l:l + 1], z, hc, dx, wb[l], wc[l], p, f"mixer_bwd_{l}", chip_payload(items))
        chip_done(items, landed)
        g_in, _ = _tn_matmul(dz, hm, f"dw_w_in_{l}")
        to_pair.extend([(("w_out", l), g_out), (("w_in", l), g_in)])
        if l > 0:
            (dx, dgate, dup, h, dy, dg_ffn1), received = _ffn_bwd(
                x0, ffn1_norm[l:l + 1], dx, gate1, up1, wffn[(l, 0)], 0, f"ffn1_bwd_{l}", pair_payload())
            pair_done(received)
            ffn_weight_grads("ffn1", l, dgate, dup, act1, h, dy)
            small.append((l, g384, gws, gpool, dg_ffn1, dg_mix, dg_ffn2))
            continue

        small.append((0, g384, gws, gpool, None, dg_mix, dg_ffn2))
        small.sort(key=lambda s: s[0])
        norm_rows = []
        for (sl, _, _, _, dg1, dgm, dg2) in small:
            norm_rows += [jnp.zeros((1, d), F32) if dg1 is None else dg1, dgm, dg2]
        norm_rows += [d_final, jnp.pad(loss_part, ((0, 0), (0, d - LANES)))]
        n_norm = len(norm_rows)
        norm_pack = jnp.concatenate(norm_rows + [jnp.zeros((8 - n_norm % 8, d), F32)] * (n_norm % 8 != 0), axis=0)
        parts = [norm_pack]
        for (_, s384, sws, spool, _, _, _) in small:
            parts += [s384, sws.reshape((n_head + 1) * CHUNK, CHUNK), spool]
        n_pair = len(to_pair)
        early = take_chip()
        riding = [pair_payload(), _GatherIci([a[None] for a in parts])] + ([chip_payload(early)] if early else [])
        (dx, dgate, dup, h, dy, dg_ffn1), landed = _ffn_bwd(
            x0, ffn1_norm[l:l + 1], dx, gate1, up1, wffn[(l, 0)], 0, f"ffn1_bwd_{l}", _Merged(riding))
        pair_done(landed[:n_pair])
        chip_done(early, landed[n_pair + len(parts):])
        items = take_chip()
        g_gate, landed = _tn_matmul(
            dgate, h, f"dw_ffn1_w_gate_{l}",
            _Merged([chip_payload(items), _GatherForward(landed[n_pair:n_pair + len(parts)])]))
        chip_done(items, landed[:len(items)])
        gathered = landed[len(items):]
        to_pair.append((("ffn1_w_gate", l), g_gate))
        g_up, received = _tn_matmul(dup, h, f"dw_ffn1_w_up_{l}", pair_payload())
        pair_done(received)
        to_pair.append((("ffn1_w_up", l), g_up))
        items = take_chip()
        g_down, landed = _tn_matmul(act1, dy, f"dw_ffn1_w_down_{l}", _Merged([chip_payload(items), pair_payload()]))
        chip_done(items, landed[:len(items)])
        pair_done(landed[len(items):])
        to_pair.append((("ffn1_w_down", l), g_down))
    grad_x = dx.reshape(x.shape)
    pair_done(_comm(pair_payload(), "rs_pair_exchange_last"))
    items = take_chip()
    (late_norm,), landed = _all_gather([jnp.pad(dg_ffn1, ((0, 7), (0, 0)))[None]], "ag_tail", chip_payload(items))
    chip_done(items, landed)

    summed = [_sum_blocks(g[0], N_DEV, f"sum_small_{k}") for k, g in enumerate(gathered)]
    late_sum = _sum_blocks(late_norm[0], N_DEV, "sum_small_late")
    norm_sum = summed[0]
    loss = norm_sum[3 * depth + 1, 0]
    cpos = lax.axis_index("x") * 4 + lax.axis_index("y") * 2 + lax.axis_index("c")
    sg = {nm: [] for nm in names}
    for l in range(depth):
        g384, gws, gpool = summed[1 + 3 * l], summed[2 + 3 * l].reshape(n_head + 1, CHUNK, CHUNK), summed[3 + 3 * l]
        sg["ffn1_norm"].append(norm_sum[3 * l] if l > 0 else late_sum[0])
        sg["mix_norm"].append(norm_sum[3 * l + 1])
        sg["ffn2_norm"].append(norm_sum[3 * l + 2])
        sg["sgu_ln_g"].append(g384[_R_SGU_G])
        sg["sgu_ln_b"].append(g384[_R_SGU_B])
        sg["conv_b"].append(g384[_R_CONV_B])
        sg["conv_ln_g"].append(g384[_R_CLN_G])
        sg["conv_ln_b"].append(g384[_R_CLN_B])
        sg["conv_w"].append(lax.dynamic_slice_in_dim(g384[_R_CONV_W:_R_CONV_W + CONV_WIDTH], cpos * cw_shard,
                                                     cw_shard, axis=1))
        sg["w_spatial"].append(gws[:n_head])
        sg["b_spatial"].append(gws[n_head][:, :n_head].T)
        sg["pool_w"].append(jnp.stack([gpool[k * HEAD_DIM:(k + 1) * HEAD_DIM, k * HEAD_DIM:(k + 1) * HEAD_DIM]
                                       for k in range(pool // HEAD_DIM)], axis=0))
        sg["pool_scale"].append(gpool[pool])
    small_names = ["ffn1_norm", "mix_norm", "sgu_ln_g", "sgu_ln_b", "w_spatial", "b_spatial", "conv_w", "conv_b",
                   "conv_ln_g", "conv_ln_b", "pool_w", "pool_scale", "ffn2_norm"]
    grads = {nm: jnp.stack(sg[nm], axis=0) for nm in small_names}
    grads["final_norm"] = norm_sum[3 * depth]

    delta, new_m, new_v = {}, {}, {}
    big_names = ["ffn1_w_gate", "ffn1_w_up", "ffn1_w_down", "w_in", "w_out", "ffn2_w_gate", "ffn2_w_up",
                 "ffn2_w_down"]
    transposed = {"ffn1_w_gate", "ffn1_w_up", "w_in", "ffn2_w_gate", "ffn2_w_up"}
    for nm in big_names:
        view = (lambda a: jnp.swapaxes(a, 1, 2)) if nm in transposed else (lambda a: a)
        outs = _finish_sharded([from_chips[(nm, l)] for l in range(depth)], view(W[nm]), view(M[nm]), view(V[nm]),
                               f"adamw_{nm}")
        grads[nm], delta[nm], new_m[nm], new_v[nm] = (view(o) for o in outs)
    snames = small_names + ["final_norm"]

    def flat2(a):
        return a.reshape(-1, a.shape[-1])

    outs = _adamw_small([flat2(W[nm]) for nm in snames], [flat2(grads[nm]) for nm in snames],
                        [flat2(M[nm]) for nm in snames], [flat2(V[nm]) for nm in snames], "adamw_small")
    ns = len(snames)
    for k, nm in enumerate(snames):
        shp = W[nm].shape
        delta[nm], new_m[nm], new_v[nm] = (outs[k].reshape(shp), outs[ns + k].reshape(shp),
                                           outs[2 * ns + k].reshape(shp))

    return (loss, grad_x, *[grads[nm] for nm in names], *[delta[nm] for nm in names],
            *[new_m[nm] for nm in names], *[new_v[nm] for nm in names])
```

```python
import functools

import jax
import jax.numpy as jnp
from jax import lax
from jax.experimental import pallas as pl
from jax.experimental.pallas import tpu as pltpu

F32 = jnp.float32
BF16 = jnp.bfloat16
EPS = 1e-6
N_DEV = 8
N_CHIP = 4
MESH = pl.DeviceIdType.MESH
ANY = pl.BlockSpec(memory_space=pl.ANY)

VMEM_LIMIT_BYTES = 56 * 1024 * 1024
LANES = 128
HALO = 32
HEAD_DIM = 64
CHUNK = 128
CONV_WIDTH = 31
POOL_WINDOWS = (2, 4, 8, 16)

ADAM_LR = 0.001
ADAM_B1 = 0.9
ADAM_B2 = 0.999
ADAM_EPS = 1e-08
ADAM_WD = 0.01
ADAM_STEP = 10


def _cparams(sem=None):
    return pltpu.CompilerParams(dimension_semantics=sem, vmem_limit_bytes=VMEM_LIMIT_BYTES)


def _position():
    return lax.axis_index("x"), lax.axis_index("y"), lax.axis_index("c")


class _Copies:
    def __init__(self):
        self.local, self.sends, self.recvs = [], [], []

    def extend(self, other):
        self.local += other.local
        self.sends += other.sends
        self.recvs += other.recvs

    def start(self):
        for cp in self.local + self.sends:
            cp.start()

    def wait(self):
        for land, send_sems, recv_sems, k, peer in self.recvs:
            _remote(land, land, send_sems, recv_sems, k, peer).wait_recv()
        for cp in self.sends:
            cp.wait_send()
        for cp in self.local:
            cp.wait()


def _remote(src, dst, send_sems, recv_sems, k, to):
    return pltpu.make_async_remote_copy(src_ref=src, dst_ref=dst, send_sem=send_sems.at[k], recv_sem=recv_sems.at[k],
                                        device_id=to, device_id_type=MESH)


class _Payload:
    ins, out_shapes, aliases, n_remote, n_local = (), (), {}, 0, 0

    def sem_shapes(self):
        return [pltpu.SemaphoreType.DMA((max(self.n_remote, 1),)), pltpu.SemaphoreType.DMA((max(self.n_remote, 1),)),
                pltpu.SemaphoreType.DMA((max(self.n_local, 1),))]


class _GatherIci(_Payload):
    def __init__(self, shards):
        self.ins = list(shards)
        self.out_shapes = [jax.ShapeDtypeStruct((s.shape[0], N_DEV * s.shape[1], s.shape[2]), s.dtype) for s in shards]
        self.n_remote, self.n_local = 4 * len(shards), len(shards)

    def build(self, ins, outs, send_sems, recv_sems, local_sems, k0=0, l0=0):
        x, y, c = _position()
        peers = [(x, y, 1 - c), (1 - x, y, c), (x, 1 - y, c), (1 - x, 1 - y, c)]
        cps = _Copies()
        for a, (src, out) in enumerate(zip(ins, outs)):
            r = src.shape[1]

            def rows(px, py, pc, out=out, r=r):
                return out.at[:, pl.ds((4 * px + 2 * py + pc) * r, r), :]

            cps.local.append(pltpu.make_async_copy(src, rows(x, y, c), local_sems.at[l0 + a]))
            for k, peer in enumerate(peers):
                cps.sends.append(_remote(src, rows(x, y, c), send_sems, recv_sems, k0 + 4 * a + k, peer))
                cps.recvs.append((rows(*peer), send_sems, recv_sems, k0 + 4 * a + k, peer))
        return cps


class _GatherForward(_Payload):
    def __init__(self, partials):
        self.ins = list(partials)
        self.out_shapes = [jax.ShapeDtypeStruct(p.shape, p.dtype) for p in partials]
        self.aliases = {a: a for a in range(len(partials))}
        self.n_remote = 3 * len(partials)

    def build(self, ins, outs, send_sems, recv_sems, local_sems, k0=0, l0=0):
        x, y, c = _position()
        chips = [(1 - x, y), (x, 1 - y), (1 - x, 1 - y)]
        cps = _Copies()
        for a, out in enumerate(outs):
            r = out.shape[1] // N_DEV
            for k, (px, py) in enumerate(chips):
                mine = out.at[:, pl.ds((4 * px + 2 * py + c) * r, r), :]
                theirs = out.at[:, pl.ds((4 * px + 2 * py + 1 - c) * r, r), :]
                cps.sends.append(_remote(mine, mine, send_sems, recv_sems, k0 + 3 * a + k, (x, y, 1 - c)))
                cps.recvs.append((theirs, send_sems, recv_sems, k0 + 3 * a + k, (x, y, 1 - c)))
        return cps


class _PairExchange(_Payload):
    def __init__(self, grads):
        self.ins = list(grads)
        self.out_shapes = [jax.ShapeDtypeStruct((g.shape[0] // 2, g.shape[1]), g.dtype) for g in grads]
        self.n_remote = N_CHIP * len(grads)

    def build(self, ins, outs, send_sems, recv_sems, local_sems, k0=0, l0=0):
        x, y, c = _position()
        cps = _Copies()
        for a, (src, out) in enumerate(zip(ins, outs)):
            r = src.shape[0] // N_DEV
            for q in range(N_CHIP):
                land = out.at[pl.ds(q * r, r), :]
                cps.sends.append(_remote(src.at[pl.ds((2 * q + 1 - c) * r, r), :], land, send_sems, recv_sems,
                                         k0 + N_CHIP * a + q, (x, y, 1 - c)))
                cps.recvs.append((land, send_sems, recv_sems, k0 + N_CHIP * a + q, (x, y, 1 - c)))
        return cps


class _ChipExchange(_Payload):
    def __init__(self, sums):
        self.ins = list(sums)
        self.out_shapes = [jax.ShapeDtypeStruct(s.shape, s.dtype) for s in sums]
        self.n_remote, self.n_local = 3 * len(sums), len(sums)

    def build(self, ins, outs, send_sems, recv_sems, local_sems, k0=0, l0=0):
        x, y, c = _position()
        my_chip = 2 * x + y
        chips = [(1 - x, y), (x, 1 - y), (1 - x, 1 - y)]
        cps = _Copies()
        for a, (src, out) in enumerate(zip(ins, outs)):
            r = src.shape[0] // N_CHIP
            mine = out.at[pl.ds(my_chip * r, r), :]
            cps.local.append(pltpu.make_async_copy(src.at[pl.ds(my_chip * r, r), :], mine, local_sems.at[l0 + a]))
            for k, (px, py) in enumerate(chips):
                land = out.at[pl.ds((2 * px + py) * r, r), :]
                cps.sends.append(_remote(src.at[pl.ds((2 * px + py) * r, r), :], mine, send_sems, recv_sems,
                                         k0 + 3 * a + k, (px, py, c)))
                cps.recvs.append((land, send_sems, recv_sems, k0 + 3 * a + k, (px, py, c)))
        return cps


class _Merged(_Payload):
    def __init__(self, parts):
        self.parts = list(parts)
        self.ins = [a for p in parts for a in p.ins]
        self.out_shapes = [s for p in parts for s in p.out_shapes]
        self.aliases, self.offsets = {}, []
        i0 = o0 = k0 = l0 = 0
        for p in parts:
            self.offsets.append((i0, o0, k0, l0))
            self.aliases.update({i0 + i: o0 + o for i, o in p.aliases.items()})
            i0, o0, k0, l0 = i0 + len(p.ins), o0 + len(p.out_shapes), k0 + p.n_remote, l0 + p.n_local
        self.n_remote, self.n_local = k0, l0

    def build(self, ins, outs, send_sems, recv_sems, local_sems):
        cps = _Copies()
        for p, (i0, o0, k0, l0) in zip(self.parts, self.offsets):
            cps.extend(p.build(ins[i0:i0 + len(p.ins)], outs[o0:o0 + len(p.out_shapes)], send_sems, recv_sems,
                               local_sems, k0, l0))
        return cps


RELAY_AT = 0.84


def _call(body, name, grid, in_specs, out_specs, out_shape, scratch_shapes, semantics, args, payload=None,
          relay=None):
    if payload is None:
        outs = pl.pallas_call(body, name=name, grid=grid, in_specs=in_specs, out_specs=out_specs,
                              out_shape=out_shape, scratch_shapes=scratch_shapes,
                              compiler_params=_cparams(semantics))(*args)
        return list(outs), []
    n_in, n_out, n_scr = len(in_specs), len(out_specs), len(scratch_shapes)
    p_in, p_out = len(payload.ins), len(payload.out_shapes)

    def carried(*refs):
        ins, p_ins = refs[:n_in], refs[n_in:n_in + p_in]
        o0 = n_in + p_in
        outs, p_outs = refs[o0:o0 + n_out], refs[o0 + n_out:o0 + n_out + p_out]
        s0 = o0 + n_out + p_out
        scr, sems = refs[s0:s0 + n_scr], refs[s0 + n_scr:s0 + n_scr + 3]
        relay_sems = refs[s0 + n_scr + 3:]
        ids = [pl.program_id(k) for k in range(len(grid))]
        at_first = functools.reduce(jnp.logical_and, [i == 0 for i in ids])
        at_last = functools.reduce(jnp.logical_and, [i == g - 1 for i, g in zip(ids, grid)])

        @pl.when(at_first)
        def _():
            payload.build(p_ins, p_outs, *sems).start()

        body(*ins, *outs, *scr)

        if relay is None:
            @pl.when(at_last)
            def _():
                payload.build(p_ins, p_outs, *sems).wait()
        else:
            n_relay = len(relay.out_shapes)
            step = functools.reduce(lambda acc, ig: acc * ig[1] + ig[0], zip(ids, grid), 0)
            total = functools.reduce(lambda a, b: a * b, grid)

            @pl.when(step == int(RELAY_AT * total))
            def _():
                payload.build(p_ins, p_outs, *sems).wait()
                relay.build(p_outs[:n_relay], p_outs[:n_relay], *relay_sems).start()

            @pl.when(at_last)
            def _():
                relay.build(p_outs[:n_relay], p_outs[:n_relay], *relay_sems).wait()

    outs = pl.pallas_call(
        carried, name=name, grid=grid, in_specs=list(in_specs) + [ANY] * p_in,
        out_specs=list(out_specs) + [ANY] * p_out, out_shape=list(out_shape) + list(payload.out_shapes),
        scratch_shapes=list(scratch_shapes) + payload.sem_shapes() + (relay.sem_shapes() if relay else []),
        input_output_aliases={n_in + i: n_out + o for i, o in payload.aliases.items()},
        compiler_params=_cparams(("arbitrary",) * len(grid)))(*args, *payload.ins)
    return list(outs[:n_out]), list(outs[n_out:])


def _comm(payload, name):
    def body(*refs):
        p_in, p_out = len(payload.ins), len(payload.out_shapes)
        cps = payload.build(refs[:p_in], refs[p_in:p_in + p_out], *refs[p_in + p_out:])
        cps.start()
        cps.wait()

    return list(pl.pallas_call(
        body, name=name, in_specs=[ANY] * len(payload.ins), out_specs=[ANY] * len(payload.out_shapes),
        out_shape=list(payload.out_shapes), scratch_shapes=payload.sem_shapes(),
        input_output_aliases=dict(payload.aliases))(*payload.ins))


def _dot(a, b):
    return jnp.dot(a, b, preferred_element_type=F32)


def _dot_nt(a, b):
    return lax.dot_general(a, b, (((1,), (1,)), ((), ())), preferred_element_type=F32)


def _dot_tn(a, b):
    return lax.dot_general(a, b, (((0,), (0,)), ((), ())), preferred_element_type=F32)


def _split_dot(x, e):
    hi = x.astype(BF16)
    r1 = x - hi.astype(F32)
    mid = r1.astype(BF16)
    lo = (r1 - mid.astype(F32)).astype(BF16)
    return _dot(hi, e) + _dot(mid, e) + _dot(lo, e)


def _rms(x):
    rstd = lax.rsqrt(jnp.mean(x * x, axis=-1, keepdims=True) + EPS)
    return x * rstd, rstd


def _rms_bwd(xhat, rstd, g, dh):
    dxhat = dh * g
    dx = rstd * (dxhat - xhat * jnp.mean(dxhat * xhat, axis=-1, keepdims=True))
    return dx, jnp.sum(dh * xhat, axis=0, keepdims=True)


def _ln(v):
    mu = jnp.mean(v, axis=-1, keepdims=True)
    xc = v - mu
    rstd = lax.rsqrt(jnp.mean(xc * xc, axis=-1, keepdims=True) + EPS)
    return xc * rstd, rstd


def _ln_bwd(vhat, rstd, g, dy):
    dvhat = dy * g
    dv = rstd * (dvhat - jnp.mean(dvhat, axis=-1, keepdims=True)
                 - vhat * jnp.mean(dvhat * vhat, axis=-1, keepdims=True))
    return dv, jnp.sum(dy * vhat, axis=0, keepdims=True), jnp.sum(dy, axis=0, keepdims=True)


_INV_SQRT2 = 0.7071067811865476
_INV_SQRT2PI = 0.3989422804014327


def _gelu(x):
    return 0.5 * x * (1.0 + lax.erf(x * _INV_SQRT2))


def _gelu_grad(x):
    return 0.5 * (1.0 + lax.erf(x * _INV_SQRT2)) + x * jnp.exp(-0.5 * x * x) * _INV_SQRT2PI


def _silu_grad(x):
    s = jax.nn.sigmoid(x)
    return s * (1.0 + x * (1.0 - s))


def _ffn_fwd(x, g, wa, mi, name, payload=None, head=None, relay=None):
    t, d = x.shape
    f = wa.shape[1]
    tm, tf = 1024, 256
    nc = f // tf
    groups = [slice(k * (tm // 2), (k + 1) * (tm // 2)) for k in range(2)]

    def body(x_ref, g_ref, wgu_ref, wd_ref, *rest):
        if head is None:
            xo_ref, gate_ref, up_ref, act_ref, h_scr, acc_scr = rest
        else:
            fg_ref, tgt_ref, xo_ref, gate_ref, up_ref, act_ref, dfg_ref, loss_ref, h_scr, acc_scr = rest
        c = pl.program_id(1)
        if head is not None:
            @pl.when((c == 0) & (pl.program_id(0) == 0))
            def _():
                dfg_ref[...] = jnp.zeros_like(dfg_ref)
                loss_ref[...] = jnp.zeros_like(loss_ref)

        @pl.when(c == 0)
        def _():
            xhat, _ = _rms(x_ref[...])
            h_scr[...] = (xhat * g_ref[...]).astype(BF16)
            acc_scr[...] = jnp.zeros_like(acc_scr)

        wgu, wd = wgu_ref[...].reshape(2 * tf, d), wd_ref[...]
        for rows in groups:
            gu = _dot_nt(h_scr[rows, :], wgu)
            gate, up = gu[:, :tf], gu[:, tf:]
            gate_ref[rows, :] = gate.astype(BF16)
            up_ref[rows, :] = up.astype(BF16)
            act = (gate * jax.nn.sigmoid(gate) * up).astype(BF16)
            act_ref[rows, :] = act
            acc_scr[rows, :] += _dot(act, wd)

        @pl.when(c == nc - 1)
        def _():
            xo = x_ref[...] + 0.5 * acc_scr[...]
            if head is None:
                xo_ref[...] = xo
            else:
                fg = fg_ref[...]
                xhat, rstd = _rms(xo)
                err = xhat * fg - tgt_ref[...]
                dxn, dfg = _rms_bwd(xhat, rstd, fg, err * (1.0 / d))
                xo_ref[...] = dxn
                dfg_ref[...] += dfg
                loss_ref[...] += jnp.zeros_like(loss_ref) + 0.5 * jnp.sum(jnp.mean(err * err, axis=-1, keepdims=True))

    assert mi % 2 == 0
    row = pl.BlockSpec((tm, d), lambda i, c: (i, 0))
    vec = pl.BlockSpec((1, d), lambda i, c: (0, 0))
    in_specs = [row, vec, pl.BlockSpec((2, tf, d), lambda i, c: (mi // 2, c, 0)),
                pl.BlockSpec((None, tf, d), lambda i, c: (mi + 2, c, 0))]
    out_specs = [row] + [pl.BlockSpec((tm, tf), lambda i, c: (i, c))] * 3
    out_shape = [jax.ShapeDtypeStruct((t, d), F32)] + [jax.ShapeDtypeStruct((t, f), BF16)] * 3
    args = (x, g, wa, wa)
    if head is not None:
        in_specs += [vec, row]
        out_specs += [vec, pl.BlockSpec((1, LANES), lambda i, c: (0, 0))]
        out_shape += [jax.ShapeDtypeStruct((1, d), F32), jax.ShapeDtypeStruct((1, LANES), F32)]
        args += tuple(head)
    return _call(
        body, name, (t // tm, nc), in_specs, out_specs, out_shape,
        [pltpu.VMEM((tm, d), BF16), pltpu.VMEM((tm, d), F32)],
        ("parallel" if head is None else "arbitrary", "arbitrary"), args, payload, relay)


def _ffn_bwd(x, g, dxo, gate, up, wa, mi, name, payload=None):
    t, d = x.shape
    f = wa.shape[1]
    tm, tf = 1024, 256
    nc = f // tf
    groups = [slice(k * (tm // 2), (k + 1) * (tm // 2)) for k in range(2)]

    def body(x_ref, g_ref, dxo_ref, gate_ref, up_ref, wgu_ref, wd_ref,
             dx_ref, dgate_ref, dup_ref, h_ref, dy_ref, dg_ref, acc_scr):
        i, c = pl.program_id(0), pl.program_id(1)

        @pl.when(c == 0)
        def _():
            xhat, _ = _rms(x_ref[...])
            h_ref[...] = (xhat * g_ref[...]).astype(BF16)
            dy_ref[...] = (0.5 * dxo_ref[...]).astype(BF16)
            acc_scr[...] = jnp.zeros_like(acc_scr)

        @pl.when((c == 0) & (i == 0))
        def _():
            dg_ref[...] = jnp.zeros_like(dg_ref)

        wg, wu, wd = wgu_ref[0], wgu_ref[1], wd_ref[...]
        for rows in groups:
            gt = gate_ref[rows, :].astype(F32)
            u = up_ref[rows, :].astype(F32)
            s = jax.nn.sigmoid(gt)
            silu = gt * s
            dact = _dot_nt(dy_ref[rows, :], wd)
            dgate = (dact * u * (s * (1.0 + gt * (1.0 - s)))).astype(BF16)
            dup = (dact * silu).astype(BF16)
            dgate_ref[rows, :] = dgate
            dup_ref[rows, :] = dup
            acc_scr[rows, :] += _dot(dgate, wg) + _dot(dup, wu)

        @pl.when(c == nc - 1)
        def _():
            xhat, rstd = _rms(x_ref[...])
            dxn, dg = _rms_bwd(xhat, rstd, g_ref[...], acc_scr[...])
            dx_ref[...] = dxo_ref[...] + dxn
            dg_ref[...] += dg

    assert mi % 2 == 0
    row = pl.BlockSpec((tm, d), lambda i, c: (i, 0))
    col = pl.BlockSpec((tm, tf), lambda i, c: (i, c))
    vec = pl.BlockSpec((1, d), lambda i, c: (0, 0))
    return _call(
        body, name, (t // tm, nc),
        [row, vec, row, col, col, pl.BlockSpec((2, tf, d), lambda i, c: (mi // 2, c, 0)),
         pl.BlockSpec((None, tf, d), lambda i, c: (mi + 2, c, 0))],
        [row, col, col, row, row, vec],
        [jax.ShapeDtypeStruct((t, d), F32), jax.ShapeDtypeStruct((t, f), BF16),
         jax.ShapeDtypeStruct((t, f), BF16),
         jax.ShapeDtypeStruct((t, d), BF16), jax.ShapeDtypeStruct((t, d), BF16),
         jax.ShapeDtypeStruct((1, d), F32)],
        [pltpu.VMEM((tm, d), F32)],
        ("arbitrary", "arbitrary"), (x, g, dxo, gate, up, wa, wa), payload)


def _ffn_dw(dgate, dup, act, h, dy, name, payload=None):
    t, f = dgate.shape
    d = h.shape[1]
    tk = 512
    tmm = f // 2
    nk = t // tk

    def body(dg_ref, du_ref, a_ref, h_ref, dy_ref, og_ref, ou_ref, od_ref, acc_g, acc_u, acc_d):
        k = pl.program_id(1)

        @pl.when(k == 0)
        def _():
            acc_g[...] = jnp.zeros_like(acc_g)
            acc_u[...] = jnp.zeros_like(acc_u)
            acc_d[...] = jnp.zeros_like(acc_d)

        hv = h_ref[...]
        acc_g[...] += _dot_tn(dg_ref[...], hv)
        acc_u[...] += _dot_tn(du_ref[...], hv)
        acc_d[...] += _dot_tn(a_ref[...], dy_ref[...])

        @pl.when(k == nk - 1)
        def _():
            og_ref[...] = acc_g[...].astype(BF16)
            ou_ref[...] = acc_u[...].astype(BF16)
            od_ref[...] = acc_d[...].astype(BF16)

    col = pl.BlockSpec((tk, tmm), lambda j, k: (k, j))
    row = pl.BlockSpec((tk, d), lambda j, k: (k, 0))
    out = pl.BlockSpec((tmm, d), lambda j, k: (j, 0))
    return _call(
        body, name, (f // tmm, nk), [col, col, col, row, row], [out, out, out],
        [jax.ShapeDtypeStruct((f, d), BF16)] * 3, [pltpu.VMEM((tmm, d), F32)] * 3,
        ("parallel", "arbitrary"), (dgate, dup, act, h, dy), payload)


def _tn_matmul(a, b, name, payload=None):
    t, m = a.shape
    n = b.shape[1]
    tk = 1024
    tmm = m // 2 if (m // 2) % LANES == 0 else m
    nk = t // tk

    def body(a_ref, b_ref, o_ref, acc_scr):
        k = pl.program_id(1)

        @pl.when(k == 0)
        def _():
            acc_scr[...] = jnp.zeros_like(acc_scr)

        acc_scr[...] += _dot_tn(a_ref[...].astype(BF16), b_ref[...].astype(BF16))

        @pl.when(k == nk - 1)
        def _():
            o_ref[...] = acc_scr[...].astype(BF16)

    (out,), p_outs = _call(
        body, name, (m // tmm, nk),
        [pl.BlockSpec((tk, tmm), lambda j, k: (k, j)), pl.BlockSpec((tk, n), lambda j, k: (k, 0))],
        [pl.BlockSpec((tmm, n), lambda j, k: (j, 0))],
        [jax.ShapeDtypeStruct((m, n), BF16)],
        [pltpu.VMEM((tmm, n), F32)],
        ("parallel", "arbitrary"), (a, b), payload)
    return out, p_outs


def _lane_ids(shape):
    return lax.broadcasted_iota(jnp.int32, shape, 1)


def _tril(w):
    r = lax.broadcasted_iota(jnp.int32, w.shape, 0)
    c = lax.broadcasted_iota(jnp.int32, w.shape, 1)
    return jnp.where(r >= c, w, 0.0)


def _shift_down(x, k):
    return x if k == 0 else pltpu.roll(x, k, 0)


def _shift_up(x, k):
    return x if k == 0 else pltpu.roll(x, x.shape[0] - k, 0)


def _sub_tile_shifts(ext, shift):
    return [shift(ext, b) for b in range(8)]


def _tap(shifted, j, n_out, down):
    a, b = divmod(j, 8)
    r0 = HALO - 8 * a if down else 8 * a
    return shifted[b][r0:r0 + n_out]


def _depthwise(shifted, w, n_out, down):
    acc = None
    for j in range(CONV_WIDTH):
        term = _tap(shifted, j, n_out, down) * w[CONV_WIDTH - 1 - j:CONV_WIDTH - j]
        acc = term if acc is None else acc + term
    return acc


def _conv_wgrad(shifted, dhc, n_out):
    return [jnp.sum(_tap(shifted, CONV_WIDTH - 1 - k, n_out, True) * dhc, axis=0, keepdims=True)
            for k in range(CONV_WIDTH)]


def _window_sums(ext, shift):
    s2 = ext + shift(ext, 1)
    s4 = s2 + shift(s2, 2)
    s8 = s4 + shift(s4, 4)
    s16 = s8 + shift(s8, 8)
    grp = _lane_ids(ext.shape) // HEAD_DIM
    return jnp.where(grp == 0, s2, jnp.where(grp == 1, s4, jnp.where(grp == 2, s8, s16)))


def _pool_count(t0, n, width):
    pos = (lax.broadcasted_iota(jnp.int32, (n, width), 0) + (t0 + 1)).astype(F32)
    grp = _lane_ids((n, width)) // HEAD_DIM
    win = jnp.where(grp == 0, 2.0, jnp.where(grp == 1, 4.0, jnp.where(grp == 2, 8.0, 16.0)))
    return jnp.minimum(pos, win)


def _block_diag(pw):
    gn, cg, _ = pw.shape
    rows = []
    for gi in range(gn):
        parts = [pw[gi] if gj == gi else jnp.zeros((cg, cg), pw.dtype) for gj in range(gn)]
        rows.append(jnp.concatenate(parts, axis=1))
    return jnp.concatenate(rows, axis=0)


def _head_pair_mix(w_even, w_odd, v):
    lo = _lane_ids((CHUNK, LANES)) < HEAD_DIM
    return jnp.where(lo, _dot(w_even, v), _dot(w_odd, v))


def _mixer_fwd(x, g, wb, wc, p, name, payload=None):
    t, d = x.shape
    d_in = wb.shape[1]
    sgu = p["sgu_ln_g"].shape[1]
    pool = p["pool_scale"].shape[1]
    d_mix = 2 * sgu + pool
    tm = 512
    n_i = t // tm
    hb = tm // HALO

    def body(x_ref, xp_ref, g_ref, wi_ref, wo_ref, lng_ref, lnb_ref, ws_ref, bs_ref, cw_ref, cb_ref, clg_ref,
             clb_ref, bd_ref, ps_ref, xo_ref, z_ref, hc_ref, cat_ref):
        i = pl.program_id(0)
        first = i == 0
        gain, wi = g_ref[...], wi_ref[...]

        def project(xv):
            xhat, _ = _rms(xv)
            return _dot_nt((xhat * gain).astype(BF16), wi)

        z_main = project(x_ref[...])
        z_ref[...] = z_main
        z_prev = jnp.where(first, 0.0, project(xp_ref[...]))

        lng, lnb = lng_ref[...], lnb_ref[...]
        wt = [_tril(ws_ref[h]).astype(BF16) for h in range(sgu // HEAD_DIM)]
        for n in range(tm // CHUNK):
            rows = slice(n * CHUNK, (n + 1) * CHUNK)
            u = _gelu(z_main[rows, 0:sgu])
            vhat, _ = _ln(_gelu(z_main[rows, sgu:2 * sgu]))
            vn = (vhat * lng + lnb).astype(BF16)
            for gp in range(sgu // LANES):
                ls = slice(gp * LANES, (gp + 1) * LANES)
                mixed = _head_pair_mix(wt[2 * gp], wt[2 * gp + 1], vn[:, ls]) + bs_ref[:, ls]
                cat_ref[rows, ls] = (u[:, ls] * mixed).astype(BF16)

        def glu(zz):
            return zz[:, 2 * sgu:3 * sgu] * jax.nn.sigmoid(zz[:, 3 * sgu:4 * sgu])

        ext = jnp.concatenate([glu(z_prev), glu(z_main)], axis=0)
        hc = _depthwise(_sub_tile_shifts(ext, _shift_down), cw_ref[...], tm, True) + cb_ref[...]
        hc_ref[...] = hc
        hhat, _ = _ln(hc)
        bn = hhat * clg_ref[...] + clb_ref[...]
        cat_ref[:, sgu:2 * sgu] = (bn * jax.nn.sigmoid(bn)).astype(BF16)

        pext = jnp.concatenate([z_prev[:, 4 * sgu:], z_main[:, 4 * sgu:]], axis=0)
        sums = _window_sums(pext, _shift_down)[HALO:]
        pooled = sums / _pool_count(i * tm, tm, pool) - z_main[:, 4 * sgu:]
        mixed_c = _dot(pooled.astype(BF16), bd_ref[...].astype(BF16))
        cat_ref[:, 2 * sgu:] = (mixed_c * ps_ref[...]).astype(BF16)

        xo_ref[...] = x_ref[...] + _dot(cat_ref[...], wo_ref[...])

    def vec(n):
        return pl.BlockSpec((1, n), lambda i: (0, 0))

    return _call(
        body, name, (n_i,),
        [pl.BlockSpec((tm, d), lambda i: (i, 0)),
         pl.BlockSpec((HALO, d), lambda i: (jnp.maximum(i * hb - 1, 0), 0)),
         vec(d),
         pl.BlockSpec((None, d_in, d), lambda i: (0, 0, 0)), pl.BlockSpec((None, d_mix, d), lambda i: (0, 0, 0)),
         vec(sgu), vec(sgu),
         pl.BlockSpec(p["w_spatial"].shape, lambda i: (0, 0, 0)),
         pl.BlockSpec((CHUNK, sgu), lambda i: (0, 0)),
         pl.BlockSpec((CONV_WIDTH, sgu), lambda i: (0, 0)),
         vec(sgu), vec(sgu), vec(sgu),
         pl.BlockSpec((pool, pool), lambda i: (0, 0)), vec(pool)],
        [pl.BlockSpec((tm, d), lambda i: (i, 0)), pl.BlockSpec((tm, d_in), lambda i: (i, 0)),
         pl.BlockSpec((tm, sgu), lambda i: (i, 0)), pl.BlockSpec((tm, d_mix), lambda i: (i, 0))],
        [jax.ShapeDtypeStruct((t, d), F32), jax.ShapeDtypeStruct((t, d_in), F32),
         jax.ShapeDtypeStruct((t, sgu), F32), jax.ShapeDtypeStruct((t, d_mix), BF16)], [], ("parallel",),
        (x, x, g, wb, wc, p["sgu_ln_g"], p["sgu_ln_b"], p["w_spatial"], p["bs_full"], p["conv_w"], p["conv_b"],
         p["conv_ln_g"], p["conv_ln_b"], p["bd"], p["pool_scale"]), payload)


_R_SGU_G, _R_SGU_B, _R_CONV_B, _R_CLN_G, _R_CLN_B, _R_CONV_W = 0, 1, 2, 3, 4, 8
_R384_ROWS = 40


def _mixer_bwd(x, g, z, hc_saved, dxo, wb, wc, p, name, payload=None):
    t, d_in = z.shape
    d = x.shape[1]
    sgu = p["sgu_ln_g"].shape[1]
    pool = p["pool_scale"].shape[1]
    d_mix = 2 * sgu + pool
    n_head = sgu // HEAD_DIM
    tm = 512
    n_i = t // tm
    hb = tm // HALO

    def body(x_ref, g_ref, wi_ref, z_ref, zp_ref, zn_ref, dxo_ref, dxon_ref, wo_ref, hc_ref, hcn_ref, lng_ref,
             lnb_ref, ws_ref, bs_ref, cw_ref, clg_ref, clb_ref, bd_ref, ps_ref,
             dx_ref, dz_ref, hm_ref, dgm_ref, g384_ref, gws_ref, gpool_ref, dbs_scr):
        i = pl.program_id(0)
        first, last = i == 0, i == n_i - 1

        @pl.when(first)
        def _():
            dgm_ref[...] = jnp.zeros_like(dgm_ref)
            g384_ref[...] = jnp.zeros_like(g384_ref)
            gws_ref[...] = jnp.zeros_like(gws_ref)
            gpool_ref[...] = jnp.zeros_like(gpool_ref)
            dbs_scr[...] = jnp.zeros_like(dbs_scr)

        z_main = z_ref[...]
        z_prev = jnp.where(first, 0.0, zp_ref[...])
        z_next = jnp.where(last, 0.0, zn_ref[...])
        wo = wo_ref[...]
        dc_main = _dot_nt(dxo_ref[...].astype(BF16), wo)
        dc_next = jnp.where(last, 0.0, _dot_nt(dxon_ref[...].astype(BF16), wo))

        lng, lnb = lng_ref[...], lnb_ref[...]
        wt = [_tril(ws_ref[h]) for h in range(n_head)]
        wt_b = [w.astype(BF16) for w in wt]
        wtt_b = [w.T.astype(BF16) for w in wt]
        lo = _lane_ids((CHUNK, LANES)) < HEAD_DIM
        d_lng = jnp.zeros((1, sgu), F32)
        d_lnb = jnp.zeros((1, sgu), F32)
        dws = [jnp.zeros((CHUNK, CHUNK), F32) for _ in range(n_head)]
        for n in range(tm // CHUNK):
            rows = slice(n * CHUNK, (n + 1) * CHUNK)
            au, av = z_main[rows, 0:sgu], z_main[rows, sgu:2 * sgu]
            u = _gelu(au)
            vhat, vrstd = _ln(_gelu(av))
            vn = (vhat * lng + lnb).astype(BF16)
            da = dc_main[rows, 0:sgu]
            dmixed = da * u
            dbs_scr[...] += dmixed
            dvn_parts, du_parts = [], []
            for gp in range(sgu // LANES):
                ls = slice(gp * LANES, (gp + 1) * LANES)
                vn_g = vn[:, ls]
                mixed = _head_pair_mix(wt_b[2 * gp], wt_b[2 * gp + 1], vn_g) + bs_ref[:, ls]
                du_parts.append(da[:, ls] * mixed)
                dm_g = dmixed[:, ls]
                dm_b = dm_g.astype(BF16)
                dvn_parts.append(jnp.where(lo, _dot(wtt_b[2 * gp], dm_b), _dot(wtt_b[2 * gp + 1], dm_b)))
                dws[2 * gp] = dws[2 * gp] + _dot_nt(jnp.where(lo, dm_g, 0.0).astype(BF16), vn_g)
                dws[2 * gp + 1] = dws[2 * gp + 1] + _dot_nt(jnp.where(lo, 0.0, dm_g).astype(BF16), vn_g)
            dvn = jnp.concatenate(dvn_parts, axis=1)
            du = jnp.concatenate(du_parts, axis=1)
            dv, dg_n, db_n = _ln_bwd(vhat, vrstd, lng, dvn)
            d_lng = d_lng + dg_n
            d_lnb = d_lnb + db_n
            dz_ref[rows, 0:sgu] = (du * _gelu_grad(au)).astype(BF16)
            dz_ref[rows, sgu:2 * sgu] = (dv * _gelu_grad(av)).astype(BF16)
        for h in range(n_head):
            gws_ref[h] += _tril(dws[h])
        g384_ref[_R_SGU_G:_R_SGU_G + 1, :] += d_lng
        g384_ref[_R_SGU_B:_R_SGU_B + 1, :] += d_lnb

        clg = clg_ref[...]
        bcols = slice(2 * sgu, 4 * sgu)
        zb = jnp.concatenate([z_prev[:, bcols], z_main[:, bcols], z_next[:, bcols]], axis=0)
        bval, bgate = zb[:, 0:sgu], zb[:, sgu:2 * sgu]
        sg = jax.nn.sigmoid(bgate)
        hglu = bval * sg
        n_out = tm + HALO
        hglu_shifts = _sub_tile_shifts(hglu, _shift_down)
        cw = cw_ref[...]
        hc = jnp.concatenate([hc_ref[...], jnp.where(last, 0.0, hcn_ref[...])], axis=0)
        hhat, hrstd = _ln(hc)
        bn = hhat * clg + clb_ref[...]
        db = jnp.concatenate([dc_main[:, sgu:2 * sgu], dc_next[:, sgu:2 * sgu]], axis=0)
        dbn = db * _silu_grad(bn)
        dhc_all, _, _ = _ln_bwd(hhat, hrstd, clg, dbn)
        dbn_m, hhat_m, dhc = dbn[:tm], hhat[:tm], dhc_all[:tm]
        g384_ref[_R_CLN_G:_R_CLN_G + 1, :] += jnp.sum(dbn_m * hhat_m, axis=0, keepdims=True)
        g384_ref[_R_CLN_B:_R_CLN_B + 1, :] += jnp.sum(dbn_m, axis=0, keepdims=True)
        g384_ref[_R_CONV_B:_R_CONV_B + 1, :] += jnp.sum(dhc, axis=0, keepdims=True)
        wrows = _conv_wgrad(hglu_shifts, dhc, tm)
        for k in range(CONV_WIDTH):
            g384_ref[_R_CONV_W + k:_R_CONV_W + k + 1, :] += wrows[k]
        dhglu = _depthwise(_sub_tile_shifts(dhc_all, _shift_up), cw, tm, False)
        bval_m, sg_m = bval[HALO:HALO + tm], sg[HALO:HALO + tm]
        dz_ref[:, 2 * sgu:3 * sgu] = (dhglu * sg_m).astype(BF16)
        dz_ref[:, 3 * sgu:4 * sgu] = (dhglu * bval_m * sg_m * (1.0 - sg_m)).astype(BF16)

        bd_b = bd_ref[...].astype(BF16)
        ps = ps_ref[...]
        p_main = z_main[:, 4 * sgu:]
        pext = jnp.concatenate([z_prev[:, 4 * sgu:], p_main], axis=0)
        cnt = _pool_count(i * tm, n_out, pool)
        pooled = _window_sums(pext, _shift_down)[HALO:] / cnt[:tm] - p_main
        pooled_b = pooled.astype(BF16)
        dcc = jnp.concatenate([dc_main[:, 2 * sgu:], dc_next[:, 2 * sgu:]], axis=0)
        dmix_c = dcc * ps
        mixed_c = _dot(pooled_b, bd_b)
        grp_r = lax.broadcasted_iota(jnp.int32, (pool, pool), 0) // HEAD_DIM
        grp_c = lax.broadcasted_iota(jnp.int32, (pool, pool), 1) // HEAD_DIM
        gpool_ref[0:pool, :] += jnp.where(grp_r == grp_c, _dot_tn(pooled_b, dmix_c[:tm].astype(BF16)), 0.0)
        gpool_ref[pool:pool + 1, :] += jnp.sum(dcc[:tm] * mixed_c, axis=0, keepdims=True)
        dpooled = _dot_nt(dmix_c.astype(BF16), bd_b)
        q = dpooled / cnt
        dp = _window_sums(q, _shift_up)[:tm] - dpooled[:tm]
        dz_ref[:, 4 * sgu:] = dp.astype(BF16)

        gain = g_ref[...]
        xhat, rstd = _rms(x_ref[...])
        hm_ref[...] = (xhat * gain).astype(BF16)
        dxn, dgm = _rms_bwd(xhat, rstd, gain, _dot(dz_ref[...], wi_ref[...]))
        dx_ref[...] = dxo_ref[...] + dxn
        dgm_ref[...] += dgm

        @pl.when(last)
        def _():
            r = lax.broadcasted_iota(jnp.int32, (sgu, LANES), 0)
            c = lax.broadcasted_iota(jnp.int32, (sgu, LANES), 1)
            sel = (r // HEAD_DIM == c).astype(BF16)
            gws_ref[n_head] = _split_dot(dbs_scr[...], sel)

    def vec(n):
        return pl.BlockSpec((1, n), lambda i: (0, 0))

    def prev_map(i):
        return (jnp.maximum(i * hb - 1, 0), 0)

    def next_map(i):
        return (jnp.minimum((i + 1) * hb, n_i * hb - 1), 0)

    return _call(
        body, name, (n_i,),
        [pl.BlockSpec((tm, d), lambda i: (i, 0)), pl.BlockSpec((1, d), lambda i: (0, 0)),
         pl.BlockSpec((None, d_in, d), lambda i: (0, 0, 0)),
         pl.BlockSpec((tm, d_in), lambda i: (i, 0)),
         pl.BlockSpec((HALO, d_in), prev_map), pl.BlockSpec((HALO, d_in), next_map),
         pl.BlockSpec((tm, d), lambda i: (i, 0)), pl.BlockSpec((HALO, d), next_map),
         pl.BlockSpec((None, d_mix, d), lambda i: (0, 0, 0)),
         pl.BlockSpec((tm, sgu), lambda i: (i, 0)), pl.BlockSpec((HALO, sgu), next_map),
         vec(sgu), vec(sgu),
         pl.BlockSpec(p["w_spatial"].shape, lambda i: (0, 0, 0)),
         pl.BlockSpec((CHUNK, sgu), lambda i: (0, 0)),
         pl.BlockSpec((CONV_WIDTH, sgu), lambda i: (0, 0)),
         vec(sgu), vec(sgu),
         pl.BlockSpec((pool, pool), lambda i: (0, 0)), vec(pool)],
        [pl.BlockSpec((tm, d), lambda i: (i, 0)), pl.BlockSpec((tm, d_in), lambda i: (i, 0)),
         pl.BlockSpec((tm, d), lambda i: (i, 0)), pl.BlockSpec((1, d), lambda i: (0, 0)),
         pl.BlockSpec((_R384_ROWS, sgu), lambda i: (0, 0)),
         pl.BlockSpec((n_head + 1, CHUNK, CHUNK), lambda i: (0, 0, 0)),
         pl.BlockSpec((pool + 8, pool), lambda i: (0, 0))],
        [jax.ShapeDtypeStruct((t, d), F32), jax.ShapeDtypeStruct((t, d_in), BF16),
         jax.ShapeDtypeStruct((t, d), BF16), jax.ShapeDtypeStruct((1, d), F32),
         jax.ShapeDtypeStruct((_R384_ROWS, sgu), F32),
         jax.ShapeDtypeStruct((n_head + 1, CHUNK, CHUNK), F32),
         jax.ShapeDtypeStruct((pool + 8, pool), F32)],
        [pltpu.VMEM((CHUNK, sgu), F32)], ("arbitrary",),
        (x, g, wb, z, z, z, dxo, dxo, wc, hc_saved, hc_saved, p["sgu_ln_g"], p["sgu_ln_b"], p["w_spatial"],
         p["bs_full"], p["conv_w"], p["conv_ln_g"], p["conv_ln_b"], p["bd"], p["pool_scale"]), payload)


def _all_gather(arrs, name, extra=None, to_sum=()):
    gather = _GatherIci(arrs)
    n = len(arrs)
    forward = _GatherForward([jax.ShapeDtypeStruct(s.shape, s.dtype) for s in gather.out_shapes])
    x_in = len(extra.ins) if extra else 0
    x_out = len(extra.out_shapes) if extra else 0
    n_sum = len(to_sum)

    def body(*refs):
        ins, x_ins, s_ins = refs[:n], refs[n:n + x_in], refs[n + x_in:n + x_in + n_sum]
        o0 = n + x_in + n_sum
        outs, x_outs = refs[o0:o0 + n], refs[o0 + n:o0 + n + x_out]
        s_outs = refs[o0 + n + x_out:o0 + n + x_out + n_sum]
        sems = refs[o0 + n + x_out + n_sum:]
        first = gather.build(ins, outs, *sems[0:3])
        first.start()
        if extra:
            beside = extra.build(x_ins, x_outs, *sems[6:9])
            beside.start()
        for s_ref, o_ref in zip(s_ins, s_outs):
            r = o_ref.shape[0]
            acc = s_ref[0:r, :]
            for q in range(1, N_DEV):
                acc = acc + s_ref[q * r:(q + 1) * r, :]
            o_ref[...] = acc
        first.wait()
        second = forward.build(outs, outs, *sems[3:6])
        second.start()
        second.wait()
        if extra:
            beside.wait()

    in_vmem = pl.BlockSpec(memory_space=pltpu.VMEM)
    outs = pl.pallas_call(
        body, name=name,
        in_specs=[ANY] * (n + x_in) + [in_vmem] * n_sum, out_specs=[ANY] * (n + x_out) + [in_vmem] * n_sum,
        out_shape=list(gather.out_shapes) + (list(extra.out_shapes) if extra else [])
        + [jax.ShapeDtypeStruct((s.shape[0] // N_DEV, s.shape[1]), s.dtype) for s in to_sum],
        scratch_shapes=gather.sem_shapes() + forward.sem_shapes() + (extra.sem_shapes() if extra else []),
        compiler_params=_cparams(),
    )(*arrs, *(extra.ins if extra else []), *to_sum)
    return list(outs[:n]), list(outs[n:n + x_out]), list(outs[n + x_out:])


def _all_gather_relayed(arrs, name):
    n = len(arrs)
    n_pairs = 8

    def body(*refs):
        ins, outs = refs[:n], refs[n:2 * n]
        send_sems, recv_sems, local_sems = refs[2 * n:]
        x, y, c = _position()
        sib, xn, yn = (x, y, 1 - c), (1 - x, y, c), (x, 1 - y, c)

        def rows(a, px, py, pc, half=None):
            r = ins[a].shape[1]
            base = (4 * px + 2 * py + pc) * r
            if half is None:
                return outs[a].at[:, pl.ds(base, r), :]
            return outs[a].at[:, pl.ds(base + half * (r // 2), r // 2), :]

        def send(a, k, src, dst, to):
            return _remote(src, dst, send_sems, recv_sems, a * n_pairs + k, to)

        def arrived(a, k, land, sender):
            _remote(land, land, send_sems, recv_sems, a * n_pairs + k, sender).wait_recv()

        own = [pltpu.make_async_copy(ins[a], rows(a, x, y, c), local_sems.at[a]) for a in range(n)]
        first = [send(a, k, ins[a], rows(a, x, y, c), to) for a in range(n) for k, to in enumerate((sib, xn, yn))]
        for cp in own + first:
            cp.start()
        for a in range(n):
            arrived(a, 1, rows(a, *xn), xn)
            arrived(a, 2, rows(a, *yn), yn)
        second = []
        for a in range(n):
            second += [send(a, 3, rows(a, *xn, half=0), rows(a, *xn, half=0), yn),
                       send(a, 4, rows(a, *yn, half=1), rows(a, *yn, half=1), xn),
                       send(a, 5, rows(a, *xn), rows(a, *xn), sib),
                       send(a, 6, rows(a, *yn), rows(a, *yn), sib)]
        for cp in second:
            cp.start()
        for a in range(n):
            arrived(a, 3, rows(a, 1 - x, 1 - y, c, half=0), yn)
            arrived(a, 4, rows(a, 1 - x, 1 - y, c, half=1), xn)
        third = [send(a, 7, rows(a, 1 - x, 1 - y, c), rows(a, 1 - x, 1 - y, c), sib) for a in range(n)]
        for cp in third:
            cp.start()
        for a in range(n):
            arrived(a, 0, rows(a, *sib), sib)
            arrived(a, 5, rows(a, 1 - x, y, 1 - c), sib)
            arrived(a, 6, rows(a, x, 1 - y, 1 - c), sib)
            arrived(a, 7, rows(a, 1 - x, 1 - y, 1 - c), sib)
        for cp in first + second + third:
            cp.wait_send()
        for cp in own:
            cp.wait()

    return list(pl.pallas_call(
        body, name=name, in_specs=[ANY] * n, out_specs=[ANY] * n,
        out_shape=[jax.ShapeDtypeStruct((a.shape[0], N_DEV * a.shape[1], a.shape[2]), a.dtype) for a in arrs],
        scratch_shapes=[pltpu.SemaphoreType.DMA((n_pairs * n,)), pltpu.SemaphoreType.DMA((n_pairs * n,)),
                        pltpu.SemaphoreType.DMA((n,))],
    )(*arrs))


def _pair_sums(grads, recvs, cidx, name):
    n = len(grads)

    def body(c_ref, *refs):
        for g_ref, r_ref, o_ref in zip(refs[:n], refs[n:2 * n], refs[2 * n:]):
            o_ref[...] = (g_ref[...].astype(F32) + r_ref[...].astype(F32)).astype(BF16)

    shapes = [(g.shape[0] // N_DEV, g.shape[1]) for g in grads]
    return list(pl.pallas_call(
        body, name=name,
        grid_spec=pltpu.PrefetchScalarGridSpec(
            num_scalar_prefetch=1, grid=(N_CHIP,),
            in_specs=[pl.BlockSpec(s, lambda q, c: (2 * q + c[0], 0)) for s in shapes]
            + [pl.BlockSpec(s, lambda q, c: (q, 0)) for s in shapes],
            out_specs=[pl.BlockSpec(s, lambda q, c: (q, 0)) for s in shapes]),
        out_shape=[jax.ShapeDtypeStruct((N_CHIP * r, cols), BF16) for r, cols in shapes],
        compiler_params=_cparams(("parallel",)),
    )(cidx, *grads, *recvs))


def _sum_blocks(parts, nblk, name):
    r = parts.shape[0] // nblk
    cols = parts.shape[1]

    def body(p_ref, o_ref):
        acc = p_ref[0:r, :].astype(F32)
        for q in range(1, nblk):
            acc = acc + p_ref[q * r:(q + 1) * r, :].astype(F32)
        o_ref[...] = acc

    return pl.pallas_call(
        body, name=name,
        out_shape=jax.ShapeDtypeStruct((r, cols), F32),
        compiler_params=_cparams(),
    )(parts)


def _adamw_math(w, g, m, v):
    m = ADAM_B1 * m + (1.0 - ADAM_B1) * g
    v = ADAM_B2 * v + (1.0 - ADAM_B2) * (g * g)
    m_hat = m / (1.0 - ADAM_B1 ** ADAM_STEP)
    v_hat = v / (1.0 - ADAM_B2 ** ADAM_STEP)
    delta = -ADAM_LR * (m_hat / (jnp.sqrt(v_hat) + ADAM_EPS) + ADAM_WD * w)
    return delta, m, v


def _finish_sharded(parts, w, m, v, name):
    depth, rr, cw = w.shape

    def body(*refs):
        p_refs = refs[:depth]
        w_ref, m_ref, v_ref, g_ref, d_ref, mo_ref, vo_ref = refs[depth:]
        l = pl.program_id(0)
        for k in range(depth):
            @pl.when(l == k)
            def _(p_ref=p_refs[k]):
                r = p_ref.shape[0] // N_CHIP
                acc = p_ref[0:r, :].astype(F32)
                for q in range(1, N_CHIP):
                    acc = acc + p_ref[q * r:(q + 1) * r, :].astype(F32)
                g_ref[...] = acc
                d_ref[...], mo_ref[...], vo_ref[...] = _adamw_math(w_ref[...], acc, m_ref[...], v_ref[...])

    blk = pl.BlockSpec((None, rr, cw), lambda l: (l, 0, 0))
    return pl.pallas_call(
        body, name=name, grid=(depth,),
        in_specs=[pl.BlockSpec(p.shape, lambda l: (0, 0)) for p in parts] + [blk] * 3, out_specs=[blk] * 4,
        out_shape=[jax.ShapeDtypeStruct(w.shape, F32)] * 4,
        compiler_params=_cparams(("arbitrary",)),
    )(*parts, w, m, v)


def _adamw_small(ws, gs, ms, vs, name):
    n = len(ws)

    def body(*refs):
        for k in range(n):
            w_ref, g_ref, m_ref, v_ref = (refs[j * n + k] for j in range(4))
            d_ref, mo_ref, vo_ref = (refs[(4 + j) * n + k] for j in range(3))
            d_ref[...], mo_ref[...], vo_ref[...] = _adamw_math(w_ref[...], g_ref[...], m_ref[...], v_ref[...])

    shapes = [jax.ShapeDtypeStruct(w.shape, F32) for w in ws]
    return pl.pallas_call(
        body, name=name, out_shape=shapes * 3, compiler_params=_cparams(),
    )(*ws, *gs, *ms, *vs)


def kernel(x, ffn1_norm, ffn1_w_gate, ffn1_w_up, ffn1_w_down, mix_norm, w_in, sgu_ln_g, sgu_ln_b, w_spatial, b_spatial, conv_w, conv_b, conv_ln_g, conv_ln_b, pool_w, pool_scale, w_out, ffn2_norm, ffn2_w_gate, ffn2_w_up, ffn2_w_down, final_norm, loss_target, m_ffn1_norm, m_ffn1_w_gate, m_ffn1_w_up, m_ffn1_w_down, m_mix_norm, m_w_in, m_sgu_ln_g, m_sgu_ln_b, m_w_spatial, m_b_spatial, m_conv_w, m_conv_b, m_conv_ln_g, m_conv_ln_b, m_pool_w, m_pool_scale, m_w_out, m_ffn2_norm, m_ffn2_w_gate, m_ffn2_w_up, m_ffn2_w_down, m_final_norm, v_ffn1_norm, v_ffn1_w_gate, v_ffn1_w_up, v_ffn1_w_down, v_mix_norm, v_w_in, v_sgu_ln_g, v_sgu_ln_b, v_w_spatial, v_b_spatial, v_conv_w, v_conv_b, v_conv_ln_g, v_conv_ln_b, v_pool_w, v_pool_scale, v_w_out, v_ffn2_norm, v_ffn2_w_gate, v_ffn2_w_up, v_ffn2_w_down, v_final_norm):
    names = ["ffn1_norm", "ffn1_w_gate", "ffn1_w_up", "ffn1_w_down", "mix_norm", "w_in", "sgu_ln_g", "sgu_ln_b",
             "w_spatial", "b_spatial", "conv_w", "conv_b", "conv_ln_g", "conv_ln_b", "pool_w", "pool_scale",
             "w_out", "ffn2_norm", "ffn2_w_gate", "ffn2_w_up", "ffn2_w_down", "final_norm"]
    W = dict(zip(names, [ffn1_norm, ffn1_w_gate, ffn1_w_up, ffn1_w_down, mix_norm, w_in, sgu_ln_g, sgu_ln_b,
                         w_spatial, b_spatial, conv_w, conv_b, conv_ln_g, conv_ln_b, pool_w, pool_scale, w_out,
                         ffn2_norm, ffn2_w_gate, ffn2_w_up, ffn2_w_down, final_norm]))
    M = dict(zip(names, [m_ffn1_norm, m_ffn1_w_gate, m_ffn1_w_up, m_ffn1_w_down, m_mix_norm, m_w_in, m_sgu_ln_g,
                         m_sgu_ln_b, m_w_spatial, m_b_spatial, m_conv_w, m_conv_b, m_conv_ln_g, m_conv_ln_b,
                         m_pool_w, m_pool_scale, m_w_out, m_ffn2_norm, m_ffn2_w_gate, m_ffn2_w_up, m_ffn2_w_down,
                         m_final_norm]))
    V = dict(zip(names, [v_ffn1_norm, v_ffn1_w_gate, v_ffn1_w_up, v_ffn1_w_down, v_mix_norm, v_w_in, v_sgu_ln_g,
                         v_sgu_ln_b, v_w_spatial, v_b_spatial, v_conv_w, v_conv_b, v_conv_ln_g, v_conv_ln_b,
                         v_pool_w, v_pool_scale, v_w_out, v_ffn2_norm, v_ffn2_w_gate, v_ffn2_w_up, v_ffn2_w_down,
                         v_final_norm]))

    depth, d = ffn1_norm.shape
    t = x.shape[1]
    sgu = sgu_ln_g.shape[1]
    pool = pool_scale.shape[1]
    n_head = sgu // HEAD_DIM
    cw_shard = conv_w.shape[2]
    xs = x.reshape(t, d)
    target = loss_target.reshape(t, d)

    def tr(w):
        return jnp.swapaxes(w, 1, 2).astype(BF16)

    ffn_shards = [[jnp.stack([tr(ffn1_w_gate)[l], tr(ffn1_w_up)[l], ffn1_w_down[l].astype(BF16)]),
                   jnp.stack([tr(ffn2_w_gate)[l], tr(ffn2_w_up)[l], ffn2_w_down[l].astype(BF16)])]
                  for l in range(depth)]
    win_shards = [tr(w_in)[l:l + 1] for l in range(depth)]
    wout_shards = [w_out[l:l + 1].astype(BF16) for l in range(depth)]
    cw_rows = depth * CONV_WIDTH
    cw_pad = -cw_rows % 8
    cw_send = jnp.pad(conv_w.reshape(cw_rows, cw_shard), ((0, cw_pad), (0, 0)))[None]
    wffn, wb, wc = {}, {}, {}
    wffn[(0, 0)], wb[0], wc[0], cwg = _all_gather_relayed(
        [ffn_shards[0][0], win_shards[0], wout_shards[0], cw_send], "ag_first")
    conv_w_full = cwg.reshape(N_DEV, cw_rows + cw_pad, cw_shard)[:, :cw_rows].reshape(
        N_DEV, depth, CONV_WIDTH, cw_shard).transpose(1, 2, 0, 3).reshape(depth, CONV_WIDTH, N_DEV * cw_shard)

    def mixer_params(l):
        return dict(
            sgu_ln_g=sgu_ln_g[l:l + 1], sgu_ln_b=sgu_ln_b[l:l + 1], w_spatial=w_spatial[l],
            bs_full=jnp.repeat(b_spatial[l].T, HEAD_DIM, axis=1),
            conv_w=conv_w_full[l], conv_b=conv_b[l:l + 1], conv_ln_g=conv_ln_g[l:l + 1],
            conv_ln_b=conv_ln_b[l:l + 1], bd=_block_diag(pool_w[l]), pool_scale=pool_scale[l:l + 1])

    saved = []
    cur = xs
    for l in range(depth):
        p = mixer_params(l)
        x0 = cur
        more = l + 1 < depth
        (x1, gate1, up1, act1), part = _ffn_fwd(x0, ffn1_norm[l:l + 1], wffn[(l, 0)], 0, f"ffn1_fwd_{l}",
                                          _GatherIci([ffn_shards[l][1]]))
        riding = [_GatherForward(part)] + ([_GatherIci([win_shards[l + 1], wout_shards[l + 1]])] if more else [])
        (x2, z, hc, cat), part = _mixer_fwd(x1, mix_norm[l:l + 1], wb[l], wc[l], p, f"mixer_fwd_{l}", _Merged(riding))
        wffn[(l, 1)] = part[0]
        riding = [_GatherIci([ffn_shards[l + 1][0]]), _GatherForward(part[1:])] if more else []
        relay = _GatherForward(riding[0].out_shapes) if more else None
        outs, part = _ffn_fwd(x2, ffn2_norm[l:l + 1], wffn[(l, 1)], 0, f"ffn2_fwd_{l}",
                              _Merged(riding) if more else None,
                              None if more else (final_norm.reshape(1, d), target), relay)
        if more:
            cur, gate2, up2, act2 = outs
            wffn[(l + 1, 0)], wb[l + 1], wc[l + 1] = part
        else:
            dx, gate2, up2, act2, d_final, loss_part = outs
        saved.append((p, x0, gate1, up1, act1, x1, z, hc, cat, x2, gate2, up2, act2))

    cidx = lax.axis_index("c").astype(jnp.int32).reshape(1)
    from_chips = {}
    to_pair, to_chip = [], []
    small = []

    def pair_payload():
        return _PairExchange([g for _, g in to_pair]) if to_pair else None

    def pair_done(received):
        if to_pair:
            (nm, l), _ = to_pair[0]
            sums = _pair_sums([g for _, g in to_pair], list(received), cidx, f"rs_pair_sum_{nm}_{l}")
            to_chip.extend((key, s) for (key, _), s in zip(to_pair, sums))
        to_pair.clear()

    def take_chip():
        items = list(to_chip)
        to_chip.clear()
        return items

    def chip_payload(items):
        return _ChipExchange([s for _, s in items]) if items else None

    def chip_done(items, landed):
        for (key, _), o in zip(items, landed):
            from_chips[key] = o

    def ffn_weight_grads(prefix, l, dgate, dup, act, h, dy):
        items = take_chip()
        items, later = items[:2], items[2:]
        to_chip.extend(later)
        grads3, landed = _ffn_dw(dgate, dup, act, h, dy, f"dw_{prefix}_{l}", chip_payload(items))
        chip_done(items, landed)
        to_pair.extend(((f"{prefix}_{nm}", l), g) for nm, g in zip(("w_gate", "w_up", "w_down"), grads3))

    for l in reversed(range(depth)):
        p, x0, gate1, up1, act1, x1, z, hc, cat, x2, gate2, up2, act2 = saved[l]
        (dx, dgate, dup, h, dy, dg_ffn2), received = _ffn_bwd(
            x2, ffn2_norm[l:l + 1], dx, gate2, up2, wffn[(l, 1)], 0, f"ffn2_bwd_{l}", pair_payload())
        pair_done(received)
        ffn_weight_grads("ffn2", l, dgate, dup, act2, h, dy)
        g_out, received = _tn_matmul(cat, dx, f"dw_w_out_{l}", pair_payload())
        pair_done(received)
        items = take_chip()
        items, later = items[:3], items[3:]
        to_chip.extend(later)
        (dx, dz, hm, dg_mix, g384, gws, gpool), landed = _mixer_bwd(
            x1, mix_norm[l:l + 1], z, hc, dx, wb[l], wc[l], p, f"mixer_bwd_{l}", chip_payload(items))
        chip_done(items, landed)
        g_in, _ = _tn_matmul(dz, hm, f"dw_w_in_{l}")
        to_pair.extend([(("w_out", l), g_out), (("w_in", l), g_in)])
        if l > 0:
            (dx, dgate, dup, h, dy, dg_ffn1), received = _ffn_bwd(
                x0, ffn1_norm[l:l + 1], dx, gate1, up1, wffn[(l, 0)], 0, f"ffn1_bwd_{l}", pair_payload())
            pair_done(received)
            ffn_weight_grads("ffn1", l, dgate, dup, act1, h, dy)
            small.append((l, g384, gws, gpool, dg_ffn1, dg_mix, dg_ffn2))
            continue

        small.append((0, g384, gws, gpool, None, dg_mix, dg_ffn2))
        small.sort(key=lambda s: s[0])
        norm_rows = []
        for (sl, _, _, _, dg1, dgm, dg2) in small:
            norm_rows += [jnp.zeros((1, d), F32) if dg1 is None else dg1, dgm, dg2]
        norm_rows += [d_final, jnp.pad(loss_part, ((0, 0), (0, d - LANES)))]
        n_norm = len(norm_rows)
        norm_pack = jnp.concatenate(norm_rows + [jnp.zeros((8 - n_norm % 8, d), F32)] * (n_norm % 8 != 0), axis=0)
        parts = [norm_pack]
        for (_, s384, sws, spool, _, _, _) in small:
            parts += [s384, sws.reshape((n_head + 1) * CHUNK, CHUNK), spool]
        n_pair = len(to_pair)
        early = take_chip()
        riding = [pair_payload(), _GatherIci([a[None] for a in parts])] + ([chip_payload(early)] if early else [])
        (dx, dgate, dup, h, dy, dg_ffn1), landed = _ffn_bwd(
            x0, ffn1_norm[l:l + 1], dx, gate1, up1, wffn[(l, 0)], 0, f"ffn1_bwd_{l}", _Merged(riding))
        pair_done(landed[:n_pair])
        chip_done(early, landed[n_pair + len(parts):])
        items = take_chip()
        g_gate, landed = _tn_matmul(
            dgate, h, f"dw_ffn1_w_gate_{l}",
            _Merged([chip_payload(items), _GatherForward(landed[n_pair:n_pair + len(parts)])]))
        chip_done(items, landed[:len(items)])
        gathered = landed[len(items):]
        to_pair.append((("ffn1_w_gate", l), g_gate))
        g_up, received = _tn_matmul(dup, h, f"dw_ffn1_w_up_{l}", pair_payload())
        pair_done(received)
        to_pair.append((("ffn1_w_up", l), g_up))
        items = take_chip()
        g_down, landed = _tn_matmul(act1, dy, f"dw_ffn1_w_down_{l}", _Merged([chip_payload(items), pair_payload()]))
        chip_done(items, landed[:len(items)])
        pair_done(landed[len(items):])
        to_pair.append((("ffn1_w_down", l), g_down))
    grad_x = dx.reshape(x.shape)
    pair_done(_comm(pair_payload(), "rs_pair_exchange_last"))
    items = take_chip()
    (late_norm,), landed, summed = _all_gather([jnp.pad(dg_ffn1, ((0, 7), (0, 0)))[None]], "ag_tail",
                                               chip_payload(items), [g[0] for g in gathered])
    chip_done(items, landed)
    late_sum = _sum_blocks(late_norm[0], N_DEV, "sum_small_late")
    norm_sum = summed[0]
    loss = norm_sum[3 * depth + 1, 0]
    cpos = lax.axis_index("x") * 4 + lax.axis_index("y") * 2 + lax.axis_index("c")
    sg = {nm: [] for nm in names}
    for l in range(depth):
        g384, gws, gpool = summed[1 + 3 * l], summed[2 + 3 * l].reshape(n_head + 1, CHUNK, CHUNK), summed[3 + 3 * l]
        sg["ffn1_norm"].append(norm_sum[3 * l] if l > 0 else late_sum[0])
        sg["mix_norm"].append(norm_sum[3 * l + 1])
        sg["ffn2_norm"].append(norm_sum[3 * l + 2])
        sg["sgu_ln_g"].append(g384[_R_SGU_G])
        sg["sgu_ln_b"].append(g384[_R_SGU_B])
        sg["conv_b"].append(g384[_R_CONV_B])
        sg["conv_ln_g"].append(g384[_R_CLN_G])
        sg["conv_ln_b"].append(g384[_R_CLN_B])
        sg["conv_w"].append(lax.dynamic_slice_in_dim(g384[_R_CONV_W:_R_CONV_W + CONV_WIDTH], cpos * cw_shard,
                                                     cw_shard, axis=1))
        sg["w_spatial"].append(gws[:n_head])
        sg["b_spatial"].append(gws[n_head][:, :n_head].T)
        sg["pool_w"].append(jnp.stack([gpool[k * HEAD_DIM:(k + 1) * HEAD_DIM, k * HEAD_DIM:(k + 1) * HEAD_DIM]
                                       for k in range(pool // HEAD_DIM)], axis=0))
        sg["pool_scale"].append(gpool[pool])
    small_names = ["ffn1_norm", "mix_norm", "sgu_ln_g", "sgu_ln_b", "w_spatial", "b_spatial", "conv_w", "conv_b",
                   "conv_ln_g", "conv_ln_b", "pool_w", "pool_scale", "ffn2_norm"]
    grads = {nm: jnp.stack(sg[nm], axis=0) for nm in small_names}
    grads["final_norm"] = norm_sum[3 * depth]

    delta, new_m, new_v = {}, {}, {}
    big_names = ["ffn1_w_gate", "ffn1_w_up", "ffn1_w_down", "w_in", "w_out", "ffn2_w_gate", "ffn2_w_up",
                 "ffn2_w_down"]
    transposed = {"ffn1_w_gate", "ffn1_w_up", "w_in", "ffn2_w_gate", "ffn2_w_up"}
    for nm in big_names:
        view = (lambda a: jnp.swapaxes(a, 1, 2)) if nm in transposed else (lambda a: a)
        outs = _finish_sharded([from_chips[(nm, l)] for l in range(depth)], view(W[nm]), view(M[nm]), view(V[nm]),
                               f"adamw_{nm}")
        grads[nm], delta[nm], new_m[nm], new_v[nm] = (view(o) for o in outs)
    snames = small_names + ["final_norm"]

    def flat2(a):
        return a.reshape(-1, a.shape[-1])

    outs = _adamw_small([flat2(W[nm]) for nm in snames], [flat2(grads[nm]) for nm in snames],
                        [flat2(M[nm]) for nm in snames], [flat2(V[nm]) for nm in snames], "adamw_small")
    ns = len(snames)
    for k, nm in enumerate(snames):
        shp = W[nm].shape
        delta[nm], new_m[nm], new_v[nm] = (outs[k].reshape(shp), outs[ns + k].reshape(shp),
                                           outs[2 * ns + k].reshape(shp))

    return (loss, grad_x, *[grads[nm] for nm in names], *[delta[nm] for nm in names],
            *[new_m[nm] for nm in names], *[new_v[nm] for nm in names])
```

```python
import functools

import jax
import jax.numpy as jnp
from jax import lax
from jax.experimental import pallas as pl
from jax.experimental.pallas import tpu as pltpu

F32 = jnp.float32
BF16 = jnp.bfloat16
EPS = 1e-6
N_DEV = 8
N_CHIP = 4
MESH = pl.DeviceIdType.MESH
ANY = pl.BlockSpec(memory_space=pl.ANY)

VMEM_LIMIT_BYTES = 56 * 1024 * 1024
LANES = 128
HALO = 32
HEAD_DIM = 64
CHUNK = 128
CONV_WIDTH = 31
POOL_WINDOWS = (2, 4, 8, 16)

ADAM_LR = 0.001
ADAM_B1 = 0.9
ADAM_B2 = 0.999
ADAM_EPS = 1e-08
ADAM_WD = 0.01
ADAM_STEP = 10


def _cparams(sem=None):
    return pltpu.CompilerParams(dimension_semantics=sem, vmem_limit_bytes=VMEM_LIMIT_BYTES)


def _position():
    return lax.axis_index("x"), lax.axis_index("y"), lax.axis_index("c")


class _Copies:
    def __init__(self):
        self.local, self.sends, self.recvs = [], [], []

    def extend(self, other):
        self.local += other.local
        self.sends += other.sends
        self.recvs += other.recvs

    def start(self):
        for cp in self.local + self.sends:
            cp.start()

    def wait(self):
        for land, send_sems, recv_sems, k, peer in self.recvs:
            _remote(land, land, send_sems, recv_sems, k, peer).wait_recv()
        for cp in self.sends:
            cp.wait_send()
        for cp in self.local:
            cp.wait()


def _remote(src, dst, send_sems, recv_sems, k, to):
    return pltpu.make_async_remote_copy(src_ref=src, dst_ref=dst, send_sem=send_sems.at[k], recv_sem=recv_sems.at[k],
                                        device_id=to, device_id_type=MESH)


class _Payload:
    ins, out_shapes, aliases, n_remote, n_local = (), (), {}, 0, 0

    def sem_shapes(self):
        return [pltpu.SemaphoreType.DMA((max(self.n_remote, 1),)), pltpu.SemaphoreType.DMA((max(self.n_remote, 1),)),
                pltpu.SemaphoreType.DMA((max(self.n_local, 1),))]


class _GatherIci(_Payload):
    def __init__(self, shards):
        self.ins = list(shards)
        self.out_shapes = [jax.ShapeDtypeStruct((s.shape[0], N_DEV * s.shape[1], s.shape[2]), s.dtype) for s in shards]
        self.n_remote, self.n_local = 4 * len(shards), len(shards)

    def build(self, ins, outs, send_sems, recv_sems, local_sems, k0=0, l0=0):
        x, y, c = _position()
        peers = [(x, y, 1 - c), (1 - x, y, c), (x, 1 - y, c), (1 - x, 1 - y, c)]
        cps = _Copies()
        for a, (src, out) in enumerate(zip(ins, outs)):
            r = src.shape[1]

            def rows(px, py, pc, out=out, r=r):
                return out.at[:, pl.ds((4 * px + 2 * py + pc) * r, r), :]

            cps.local.append(pltpu.make_async_copy(src, rows(x, y, c), local_sems.at[l0 + a]))
            for k, peer in enumerate(peers):
                cps.sends.append(_remote(src, rows(x, y, c), send_sems, recv_sems, k0 + 4 * a + k, peer))
                cps.recvs.append((rows(*peer), send_sems, recv_sems, k0 + 4 * a + k, peer))
        return cps


class _GatherForward(_Payload):
    def __init__(self, partials):
        self.ins = list(partials)
        self.out_shapes = [jax.ShapeDtypeStruct(p.shape, p.dtype) for p in partials]
        self.aliases = {a: a for a in range(len(partials))}
        self.n_remote = 3 * len(partials)

    def build(self, ins, outs, send_sems, recv_sems, local_sems, k0=0, l0=0):
        x, y, c = _position()
        chips = [(1 - x, y), (x, 1 - y), (1 - x, 1 - y)]
        cps = _Copies()
        for a, out in enumerate(outs):
            r = out.shape[1] // N_DEV
            for k, (px, py) in enumerate(chips):
                mine = out.at[:, pl.ds((4 * px + 2 * py + c) * r, r), :]
                theirs = out.at[:, pl.ds((4 * px + 2 * py + 1 - c) * r, r), :]
                cps.sends.append(_remote(mine, mine, send_sems, recv_sems, k0 + 3 * a + k, (x, y, 1 - c)))
                cps.recvs.append((theirs, send_sems, recv_sems, k0 + 3 * a + k, (x, y, 1 - c)))
        return cps


class _PairExchange(_Payload):
    def __init__(self, grads):
        self.ins = list(grads)
        self.out_shapes = [jax.ShapeDtypeStruct((g.shape[0] // 2, g.shape[1]), g.dtype) for g in grads]
        self.n_remote = N_CHIP * len(grads)

    def build(self, ins, outs, send_sems, recv_sems, local_sems, k0=0, l0=0):
        x, y, c = _position()
        cps = _Copies()
        for a, (src, out) in enumerate(zip(ins, outs)):
            r = src.shape[0] // N_DEV
            for q in range(N_CHIP):
                land = out.at[pl.ds(q * r, r), :]
                cps.sends.append(_remote(src.at[pl.ds((2 * q + 1 - c) * r, r), :], land, send_sems, recv_sems,
                                         k0 + N_CHIP * a + q, (x, y, 1 - c)))
                cps.recvs.append((land, send_sems, recv_sems, k0 + N_CHIP * a + q, (x, y, 1 - c)))
        return cps


class _ChipExchange(_Payload):
    def __init__(self, sums):
        self.ins = list(sums)
        self.out_shapes = [jax.ShapeDtypeStruct(s.shape, s.dtype) for s in sums]
        self.n_remote, self.n_local = 3 * len(sums), len(sums)

    def build(self, ins, outs, send_sems, recv_sems, local_sems, k0=0, l0=0):
        x, y, c = _position()
        my_chip = 2 * x + y
        chips = [(1 - x, y), (x, 1 - y), (1 - x, 1 - y)]
        cps = _Copies()
        for a, (src, out) in enumerate(zip(ins, outs)):
            r = src.shape[0] // N_CHIP
            mine = out.at[pl.ds(my_chip * r, r), :]
            cps.local.append(pltpu.make_async_copy(src.at[pl.ds(my_chip * r, r), :], mine, local_sems.at[l0 + a]))
            for k, (px, py) in enumerate(chips):
                land = out.at[pl.ds((2 * px + py) * r, r), :]
                cps.sends.append(_remote(src.at[pl.ds((2 * px + py) * r, r), :], mine, send_sems, recv_sems,
                                         k0 + 3 * a + k, (px, py, c)))
                cps.recvs.append((land, send_sems, recv_sems, k0 + 3 * a + k, (px, py, c)))
        return cps


class _Merged(_Payload):
    def __init__(self, parts):
        self.parts = list(parts)
        self.ins = [a for p in parts for a in p.ins]
        self.out_shapes = [s for p in parts for s in p.out_shapes]
        self.aliases, self.offsets = {}, []
        i0 = o0 = k0 = l0 = 0
        for p in parts:
            self.offsets.append((i0, o0, k0, l0))
            self.aliases.update({i0 + i: o0 + o for i, o in p.aliases.items()})
            i0, o0, k0, l0 = i0 + len(p.ins), o0 + len(p.out_shapes), k0 + p.n_remote, l0 + p.n_local
        self.n_remote, self.n_local = k0, l0

    def build(self, ins, outs, send_sems, recv_sems, local_sems):
        cps = _Copies()
        for p, (i0, o0, k0, l0) in zip(self.parts, self.offsets):
            cps.extend(p.build(ins[i0:i0 + len(p.ins)], outs[o0:o0 + len(p.out_shapes)], send_sems, recv_sems,
                               local_sems, k0, l0))
        return cps


RELAY_AT = 0.84


def _call(body, name, grid, in_specs, out_specs, out_shape, scratch_shapes, semantics, args, payload=None,
          relay=None):
    if payload is None:
        outs = pl.pallas_call(body, name=name, grid=grid, in_specs=in_specs, out_specs=out_specs,
                              out_shape=out_shape, scratch_shapes=scratch_shapes,
                              compiler_params=_cparams(semantics))(*args)
        return list(outs), []
    n_in, n_out, n_scr = len(in_specs), len(out_specs), len(scratch_shapes)
    p_in, p_out = len(payload.ins), len(payload.out_shapes)

    def carried(*refs):
        ins, p_ins = refs[:n_in], refs[n_in:n_in + p_in]
        o0 = n_in + p_in
        outs, p_outs = refs[o0:o0 + n_out], refs[o0 + n_out:o0 + n_out + p_out]
        s0 = o0 + n_out + p_out
        scr, sems = refs[s0:s0 + n_scr], refs[s0 + n_scr:s0 + n_scr + 3]
        relay_sems = refs[s0 + n_scr + 3:]
        ids = [pl.program_id(k) for k in range(len(grid))]
        at_first = functools.reduce(jnp.logical_and, [i == 0 for i in ids])
        at_last = functools.reduce(jnp.logical_and, [i == g - 1 for i, g in zip(ids, grid)])

        @pl.when(at_first)
        def _():
            payload.build(p_ins, p_outs, *sems).start()

        body(*ins, *outs, *scr)

        if relay is None:
            @pl.when(at_last)
            def _():
                payload.build(p_ins, p_outs, *sems).wait()
        else:
            n_relay = len(relay.out_shapes)
            step = functools.reduce(lambda acc, ig: acc * ig[1] + ig[0], zip(ids, grid), 0)
            total = functools.reduce(lambda a, b: a * b, grid)

            @pl.when(step == int(RELAY_AT * total))
            def _():
                payload.build(p_ins, p_outs, *sems).wait()
                relay.build(p_outs[:n_relay], p_outs[:n_relay], *relay_sems).start()

            @pl.when(at_last)
            def _():
                relay.build(p_outs[:n_relay], p_outs[:n_relay], *relay_sems).wait()

    outs = pl.pallas_call(
        carried, name=name, grid=grid, in_specs=list(in_specs) + [ANY] * p_in,
        out_specs=list(out_specs) + [ANY] * p_out, out_shape=list(out_shape) + list(payload.out_shapes),
        scratch_shapes=list(scratch_shapes) + payload.sem_shapes() + (relay.sem_shapes() if relay else []),
        input_output_aliases={n_in + i: n_out + o for i, o in payload.aliases.items()},
        compiler_params=_cparams(("arbitrary",) * len(grid)))(*args, *payload.ins)
    return list(outs[:n_out]), list(outs[n_out:])


def _comm(payload, name):
    def body(*refs):
        p_in, p_out = len(payload.ins), len(payload.out_shapes)
        cps = payload.build(refs[:p_in], refs[p_in:p_in + p_out], *refs[p_in + p_out:])
        cps.start()
        cps.wait()

    return list(pl.pallas_call(
        body, name=name, in_specs=[ANY] * len(payload.ins), out_specs=[ANY] * len(payload.out_shapes),
        out_shape=list(payload.out_shapes), scratch_shapes=payload.sem_shapes(),
        input_output_aliases=dict(payload.aliases))(*payload.ins))


def _dot(a, b):
    return jnp.dot(a, b, preferred_element_type=F32)


def _dot_nt(a, b):
    return lax.dot_general(a, b, (((1,), (1,)), ((), ())), preferred_element_type=F32)


def _dot_tn(a, b):
    return lax.dot_general(a, b, (((0,), (0,)), ((), ())), preferred_element_type=F32)


def _split_dot(x, e):
    hi = x.astype(BF16)
    r1 = x - hi.astype(F32)
    mid = r1.astype(BF16)
    lo = (r1 - mid.astype(F32)).astype(BF16)
    return _dot(hi, e) + _dot(mid, e) + _dot(lo, e)


def _rms(x):
    rstd = lax.rsqrt(jnp.mean(x * x, axis=-1, keepdims=True) + EPS)
    return x * rstd, rstd


def _rms_bwd(xhat, rstd, g, dh):
    dxhat = dh * g
    dx = rstd * (dxhat - xhat * jnp.mean(dxhat * xhat, axis=-1, keepdims=True))
    return dx, jnp.sum(dh * xhat, axis=0, keepdims=True)


def _ln(v):
    mu = jnp.mean(v, axis=-1, keepdims=True)
    xc = v - mu
    rstd = lax.rsqrt(jnp.mean(xc * xc, axis=-1, keepdims=True) + EPS)
    return xc * rstd, rstd


def _ln_bwd(vhat, rstd, g, dy):
    dvhat = dy * g
    dv = rstd * (dvhat - jnp.mean(dvhat, axis=-1, keepdims=True)
                 - vhat * jnp.mean(dvhat * vhat, axis=-1, keepdims=True))
    return dv, jnp.sum(dy * vhat, axis=0, keepdims=True), jnp.sum(dy, axis=0, keepdims=True)


_INV_SQRT2 = 0.7071067811865476
_INV_SQRT2PI = 0.3989422804014327


def _gelu(x):
    return 0.5 * x * (1.0 + lax.erf(x * _INV_SQRT2))


def _gelu_grad(x):
    return 0.5 * (1.0 + lax.erf(x * _INV_SQRT2)) + x * jnp.exp(-0.5 * x * x) * _INV_SQRT2PI


def _silu_grad(x):
    s = jax.nn.sigmoid(x)
    return s * (1.0 + x * (1.0 - s))


def _ffn_fwd(x, g, wa, mi, name, payload=None, head=None, relay=None):
    t, d = x.shape
    f = wa.shape[1]
    tm, tf = 1024, 256
    nc = f // tf
    groups = [slice(k * (tm // 2), (k + 1) * (tm // 2)) for k in range(2)]

    def body(x_ref, g_ref, wgu_ref, wd_ref, *rest):
        if head is None:
            xo_ref, gate_ref, up_ref, act_ref, h_scr, acc_scr = rest
        else:
            fg_ref, tgt_ref, xo_ref, gate_ref, up_ref, act_ref, dfg_ref, loss_ref, h_scr, acc_scr = rest
        c = pl.program_id(1)
        if head is not None:
            @pl.when((c == 0) & (pl.program_id(0) == 0))
            def _():
                dfg_ref[...] = jnp.zeros_like(dfg_ref)
                loss_ref[...] = jnp.zeros_like(loss_ref)

        @pl.when(c == 0)
        def _():
            xhat, _ = _rms(x_ref[...])
            h_scr[...] = (xhat * g_ref[...]).astype(BF16)
            acc_scr[...] = jnp.zeros_like(acc_scr)

        wgu, wd = wgu_ref[...].reshape(2 * tf, d), wd_ref[...]
        for rows in groups:
            gu = _dot_nt(h_scr[rows, :], wgu)
            gate, up = gu[:, :tf], gu[:, tf:]
            gate_ref[rows, :] = gate.astype(BF16)
            up_ref[rows, :] = up.astype(BF16)
            act = (gate * jax.nn.sigmoid(gate) * up).astype(BF16)
            act_ref[rows, :] = act
            acc_scr[rows, :] += _dot(act, wd)

        @pl.when(c == nc - 1)
        def _():
            xo = x_ref[...] + 0.5 * acc_scr[...]
            if head is None:
                xo_ref[...] = xo
            else:
                fg = fg_ref[...]
                xhat, rstd = _rms(xo)
                err = xhat * fg - tgt_ref[...]
                dxn, dfg = _rms_bwd(xhat, rstd, fg, err * (1.0 / d))
                xo_ref[...] = dxn
                dfg_ref[...] += dfg
                loss_ref[...] += jnp.zeros_like(loss_ref) + 0.5 * jnp.sum(jnp.mean(err * err, axis=-1, keepdims=True))

    assert mi % 2 == 0
    row = pl.BlockSpec((tm, d), lambda i, c: (i, 0))
    vec = pl.BlockSpec((1, d), lambda i, c: (0, 0))
    in_specs = [row, vec, pl.BlockSpec((2, tf, d), lambda i, c: (mi // 2, c, 0)),
                pl.BlockSpec((None, tf, d), lambda i, c: (mi + 2, c, 0))]
    out_specs = [row] + [pl.BlockSpec((tm, tf), lambda i, c: (i, c))] * 3
    out_shape = [jax.ShapeDtypeStruct((t, d), F32)] + [jax.ShapeDtypeStruct((t, f), BF16)] * 3
    args = (x, g, wa, wa)
    if head is not None:
        in_specs += [vec, row]
        out_specs += [vec, pl.BlockSpec((1, LANES), lambda i, c: (0, 0))]
        out_shape += [jax.ShapeDtypeStruct((1, d), F32), jax.ShapeDtypeStruct((1, LANES), F32)]
        args += tuple(head)
    return _call(
        body, name, (t // tm, nc), in_specs, out_specs, out_shape,
        [pltpu.VMEM((tm, d), BF16), pltpu.VMEM((tm, d), F32)],
        ("parallel" if head is None else "arbitrary", "arbitrary"), args, payload, relay)


def _ffn_bwd(x, g, dxo, gate, up, wa, mi, name, payload=None):
    t, d = x.shape
    f = wa.shape[1]
    tm, tf = 1024, 256
    nc = f // tf
    groups = [slice(k * (tm // 2), (k + 1) * (tm // 2)) for k in range(2)]

    def body(x_ref, g_ref, dxo_ref, gate_ref, up_ref, wgu_ref, wd_ref,
             dx_ref, dgate_ref, dup_ref, h_ref, dy_ref, dg_ref, acc_scr):
        i, c = pl.program_id(0), pl.program_id(1)

        @pl.when(c == 0)
        def _():
            xhat, _ = _rms(x_ref[...])
            h_ref[...] = (xhat * g_ref[...]).astype(BF16)
            dy_ref[...] = (0.5 * dxo_ref[...]).astype(BF16)
            acc_scr[...] = jnp.zeros_like(acc_scr)

        @pl.when((c == 0) & (i == 0))
        def _():
            dg_ref[...] = jnp.zeros_like(dg_ref)

        wg, wu, wd = wgu_ref[0], wgu_ref[1], wd_ref[...]
        for rows in groups:
            gt = gate_ref[rows, :].astype(F32)
            u = up_ref[rows, :].astype(F32)
            s = jax.nn.sigmoid(gt)
            silu = gt * s
            dact = _dot_nt(dy_ref[rows, :], wd)
            dgate = (dact * u * (s * (1.0 + gt * (1.0 - s)))).astype(BF16)
            dup = (dact * silu).astype(BF16)
            dgate_ref[rows, :] = dgate
            dup_ref[rows, :] = dup
            acc_scr[rows, :] += _dot(dgate, wg) + _dot(dup, wu)

        @pl.when(c == nc - 1)
        def _():
            xhat, rstd = _rms(x_ref[...])
            dxn, dg = _rms_bwd(xhat, rstd, g_ref[...], acc_scr[...])
            dx_ref[...] = dxo_ref[...] + dxn
            dg_ref[...] += dg

    assert mi % 2 == 0
    row = pl.BlockSpec((tm, d), lambda i, c: (i, 0))
    col = pl.BlockSpec((tm, tf), lambda i, c: (i, c))
    vec = pl.BlockSpec((1, d), lambda i, c: (0, 0))
    return _call(
        body, name, (t // tm, nc),
        [row, vec, row, col, col, pl.BlockSpec((2, tf, d), lambda i, c: (mi // 2, c, 0)),
         pl.BlockSpec((None, tf, d), lambda i, c: (mi + 2, c, 0))],
        [row, col, col, row, row, vec],
        [jax.ShapeDtypeStruct((t, d), F32), jax.ShapeDtypeStruct((t, f), BF16),
         jax.ShapeDtypeStruct((t, f), BF16),
         jax.ShapeDtypeStruct((t, d), BF16), jax.ShapeDtypeStruct((t, d), BF16),
         jax.ShapeDtypeStruct((1, d), F32)],
        [pltpu.VMEM((tm, d), F32)],
        ("arbitrary", "arbitrary"), (x, g, dxo, gate, up, wa, wa), payload)


def _ffn_dw(dgate, dup, act, h, dy, name, payload=None):
    t, f = dgate.shape
    d = h.shape[1]
    tk = 512
    tmm = f // 2
    nk = t // tk

    def body(dg_ref, du_ref, a_ref, h_ref, dy_ref, og_ref, ou_ref, od_ref, acc_g, acc_u, acc_d):
        k = pl.program_id(1)

        @pl.when(k == 0)
        def _():
            acc_g[...] = jnp.zeros_like(acc_g)
            acc_u[...] = jnp.zeros_like(acc_u)
            acc_d[...] = jnp.zeros_like(acc_d)

        hv = h_ref[...]
        acc_g[...] += _dot_tn(dg_ref[...], hv)
        acc_u[...] += _dot_tn(du_ref[...], hv)
        acc_d[...] += _dot_tn(a_ref[...], dy_ref[...])

        @pl.when(k == nk - 1)
        def _():
            og_ref[...] = acc_g[...].astype(BF16)
            ou_ref[...] = acc_u[...].astype(BF16)
            od_ref[...] = acc_d[...].astype(BF16)

    col = pl.BlockSpec((tk, tmm), lambda j, k: (k, j))
    row = pl.BlockSpec((tk, d), lambda j, k: (k, 0))
    out = pl.BlockSpec((tmm, d), lambda j, k: (j, 0))
    return _call(
        body, name, (f // tmm, nk), [col, col, col, row, row], [out, out, out],
        [jax.ShapeDtypeStruct((f, d), BF16)] * 3, [pltpu.VMEM((tmm, d), F32)] * 3,
        ("parallel", "arbitrary"), (dgate, dup, act, h, dy), payload)


def _tn_matmul(a, b, name, payload=None):
    t, m = a.shape
    n = b.shape[1]
    tk = 1024
    tmm = m // 2 if (m > 2048 and (m // 2) % LANES == 0) else m
    nk = t // tk

    def body(a_ref, b_ref, o_ref, acc_scr):
        k = pl.program_id(1)

        @pl.when(k == 0)
        def _():
            acc_scr[...] = jnp.zeros_like(acc_scr)

        acc_scr[...] += _dot_tn(a_ref[...].astype(BF16), b_ref[...].astype(BF16))

        @pl.when(k == nk - 1)
        def _():
            o_ref[...] = acc_scr[...].astype(BF16)

    (out,), p_outs = _call(
        body, name, (m // tmm, nk),
        [pl.BlockSpec((tk, tmm), lambda j, k: (k, j)), pl.BlockSpec((tk, n), lambda j, k: (k, 0))],
        [pl.BlockSpec((tmm, n), lambda j, k: (j, 0))],
        [jax.ShapeDtypeStruct((m, n), BF16)],
        [pltpu.VMEM((tmm, n), F32)],
        ("parallel", "arbitrary"), (a, b), payload)
    return out, p_outs


def _lane_ids(shape):
    return lax.broadcasted_iota(jnp.int32, shape, 1)


def _tril(w):
    r = lax.broadcasted_iota(jnp.int32, w.shape, 0)
    c = lax.broadcasted_iota(jnp.int32, w.shape, 1)
    return jnp.where(r >= c, w, 0.0)


def _shift_down(x, k):
    return x if k == 0 else pltpu.roll(x, k, 0)


def _shift_up(x, k):
    return x if k == 0 else pltpu.roll(x, x.shape[0] - k, 0)


def _sub_tile_shifts(ext, shift):
    return [shift(ext, b) for b in range(8)]


def _tap(shifted, j, n_out, down):
    a, b = divmod(j, 8)
    r0 = HALO - 8 * a if down else 8 * a
    return shifted[b][r0:r0 + n_out]


def _depthwise(shifted, w, n_out, down):
    acc = None
    for j in range(CONV_WIDTH):
        term = _tap(shifted, j, n_out, down) * w[CONV_WIDTH - 1 - j:CONV_WIDTH - j]
        acc = term if acc is None else acc + term
    return acc


def _conv_wgrad(shifted, dhc, n_out):
    return [jnp.sum(_tap(shifted, CONV_WIDTH - 1 - k, n_out, True) * dhc, axis=0, keepdims=True)
            for k in range(CONV_WIDTH)]


def _window_sums(ext, shift):
    s2 = ext + shift(ext, 1)
    s4 = s2 + shift(s2, 2)
    s8 = s4 + shift(s4, 4)
    s16 = s8 + shift(s8, 8)
    grp = _lane_ids(ext.shape) // HEAD_DIM
    return jnp.where(grp == 0, s2, jnp.where(grp == 1, s4, jnp.where(grp == 2, s8, s16)))


def _pool_count(t0, n, width):
    pos = (lax.broadcasted_iota(jnp.int32, (n, width), 0) + (t0 + 1)).astype(F32)
    grp = _lane_ids((n, width)) // HEAD_DIM
    win = jnp.where(grp == 0, 2.0, jnp.where(grp == 1, 4.0, jnp.where(grp == 2, 8.0, 16.0)))
    return jnp.minimum(pos, win)


def _block_diag(pw):
    gn, cg, _ = pw.shape
    rows = []
    for gi in range(gn):
        parts = [pw[gi] if gj == gi else jnp.zeros((cg, cg), pw.dtype) for gj in range(gn)]
        rows.append(jnp.concatenate(parts, axis=1))
    return jnp.concatenate(rows, axis=0)


def _head_pair_mix(w_even, w_odd, v):
    lo = _lane_ids((CHUNK, LANES)) < HEAD_DIM
    return jnp.where(lo, _dot(w_even, v), _dot(w_odd, v))


def _mixer_fwd(x, g, wb, wc, p, name, payload=None):
    t, d = x.shape
    d_in = wb.shape[1]
    sgu = p["sgu_ln_g"].shape[1]
    pool = p["pool_scale"].shape[1]
    d_mix = 2 * sgu + pool
    tm = 512
    n_i = t // tm
    hb = tm // HALO

    def body(x_ref, xp_ref, g_ref, wi_ref, wo_ref, lng_ref, lnb_ref, ws_ref, bs_ref, cw_ref, cb_ref, clg_ref,
             clb_ref, bd_ref, ps_ref, xo_ref, z_ref, hc_ref, cat_ref):
        i = pl.program_id(0)
        first = i == 0
        gain, wi = g_ref[...], wi_ref[...]

        def project(xv):
            xhat, _ = _rms(xv)
            return _dot_nt((xhat * gain).astype(BF16), wi)

        z_main = project(x_ref[...])
        z_ref[...] = z_main
        z_prev = jnp.where(first, 0.0, project(xp_ref[...]))

        lng, lnb = lng_ref[...], lnb_ref[...]
        wt = [_tril(ws_ref[h]).astype(BF16) for h in range(sgu // HEAD_DIM)]
        for n in range(tm // CHUNK):
            rows = slice(n * CHUNK, (n + 1) * CHUNK)
            u = _gelu(z_main[rows, 0:sgu])
            vhat, _ = _ln(_gelu(z_main[rows, sgu:2 * sgu]))
            vn = (vhat * lng + lnb).astype(BF16)
            for gp in range(sgu // LANES):
                ls = slice(gp * LANES, (gp + 1) * LANES)
                mixed = _head_pair_mix(wt[2 * gp], wt[2 * gp + 1], vn[:, ls]) + bs_ref[:, ls]
                cat_ref[rows, ls] = (u[:, ls] * mixed).astype(BF16)

        def glu(zz):
            return zz[:, 2 * sgu:3 * sgu] * jax.nn.sigmoid(zz[:, 3 * sgu:4 * sgu])

        ext = jnp.concatenate([glu(z_prev), glu(z_main)], axis=0)
        hc = _depthwise(_sub_tile_shifts(ext, _shift_down), cw_ref[...], tm, True) + cb_ref[...]
        hc_ref[...] = hc
        hhat, _ = _ln(hc)
        bn = hhat * clg_ref[...] + clb_ref[...]
        cat_ref[:, sgu:2 * sgu] = (bn * jax.nn.sigmoid(bn)).astype(BF16)

        pext = jnp.concatenate([z_prev[:, 4 * sgu:], z_main[:, 4 * sgu:]], axis=0)
        sums = _window_sums(pext, _shift_down)[HALO:]
        pooled = sums / _pool_count(i * tm, tm, pool) - z_main[:, 4 * sgu:]
        mixed_c = _dot(pooled.astype(BF16), bd_ref[...].astype(BF16))
        cat_ref[:, 2 * sgu:] = (mixed_c * ps_ref[...]).astype(BF16)

        xo_ref[...] = x_ref[...] + _dot(cat_ref[...], wo_ref[...])

    def vec(n):
        return pl.BlockSpec((1, n), lambda i: (0, 0))

    return _call(
        body, name, (n_i,),
        [pl.BlockSpec((tm, d), lambda i: (i, 0)),
         pl.BlockSpec((HALO, d), lambda i: (jnp.maximum(i * hb - 1, 0), 0)),
         vec(d),
         pl.BlockSpec((None, d_in, d), lambda i: (0, 0, 0)), pl.BlockSpec((None, d_mix, d), lambda i: (0, 0, 0)),
         vec(sgu), vec(sgu),
         pl.BlockSpec(p["w_spatial"].shape, lambda i: (0, 0, 0)),
         pl.BlockSpec((CHUNK, sgu), lambda i: (0, 0)),
         pl.BlockSpec((CONV_WIDTH, sgu), lambda i: (0, 0)),
         vec(sgu), vec(sgu), vec(sgu),
         pl.BlockSpec((pool, pool), lambda i: (0, 0)), vec(pool)],
        [pl.BlockSpec((tm, d), lambda i: (i, 0)), pl.BlockSpec((tm, d_in), lambda i: (i, 0)),
         pl.BlockSpec((tm, sgu), lambda i: (i, 0)), pl.BlockSpec((tm, d_mix), lambda i: (i, 0))],
        [jax.ShapeDtypeStruct((t, d), F32), jax.ShapeDtypeStruct((t, d_in), F32),
         jax.ShapeDtypeStruct((t, sgu), F32), jax.ShapeDtypeStruct((t, d_mix), BF16)], [], ("parallel",),
        (x, x, g, wb, wc, p["sgu_ln_g"], p["sgu_ln_b"], p["w_spatial"], p["bs_full"], p["conv_w"], p["conv_b"],
         p["conv_ln_g"], p["conv_ln_b"], p["bd"], p["pool_scale"]), payload)


_R_SGU_G, _R_SGU_B, _R_CONV_B, _R_CLN_G, _R_CLN_B, _R_CONV_W = 0, 1, 2, 3, 4, 8
_R384_ROWS = 40


def _mixer_bwd(x, g, z, hc_saved, dxo, wb, wc, p, name, payload=None):
    t, d_in = z.shape
    d = x.shape[1]
    sgu = p["sgu_ln_g"].shape[1]
    pool = p["pool_scale"].shape[1]
    d_mix = 2 * sgu + pool
    n_head = sgu // HEAD_DIM
    tm = 512
    n_i = t // tm
    hb = tm // HALO

    def body(x_ref, g_ref, wi_ref, z_ref, zp_ref, zn_ref, dxo_ref, dxon_ref, wo_ref, hc_ref, hcn_ref, lng_ref,
             lnb_ref, ws_ref, bs_ref, cw_ref, clg_ref, clb_ref, bd_ref, ps_ref,
             dx_ref, dz_ref, hm_ref, dgm_ref, g384_ref, gws_ref, gpool_ref, dbs_scr):
        i = pl.program_id(0)
        first, last = i == 0, i == n_i - 1

        @pl.when(first)
        def _():
            dgm_ref[...] = jnp.zeros_like(dgm_ref)
            g384_ref[...] = jnp.zeros_like(g384_ref)
            gws_ref[...] = jnp.zeros_like(gws_ref)
            gpool_ref[...] = jnp.zeros_like(gpool_ref)
            dbs_scr[...] = jnp.zeros_like(dbs_scr)

        z_main = z_ref[...]
        z_prev = jnp.where(first, 0.0, zp_ref[...])
        z_next = jnp.where(last, 0.0, zn_ref[...])
        wo = wo_ref[...]
        dc_main = _dot_nt(dxo_ref[...].astype(BF16), wo)
        dc_next = jnp.where(last, 0.0, _dot_nt(dxon_ref[...].astype(BF16), wo))

        lng, lnb = lng_ref[...], lnb_ref[...]
        wt = [_tril(ws_ref[h]) for h in range(n_head)]
        wt_b = [w.astype(BF16) for w in wt]
        wtt_b = [w.T.astype(BF16) for w in wt]
        lo = _lane_ids((CHUNK, LANES)) < HEAD_DIM
        d_lng = jnp.zeros((1, sgu), F32)
        d_lnb = jnp.zeros((1, sgu), F32)
        dws = [jnp.zeros((CHUNK, CHUNK), F32) for _ in range(n_head)]
        for n in range(tm // CHUNK):
            rows = slice(n * CHUNK, (n + 1) * CHUNK)
            au, av = z_main[rows, 0:sgu], z_main[rows, sgu:2 * sgu]
            u = _gelu(au)
            vhat, vrstd = _ln(_gelu(av))
            vn = (vhat * lng + lnb).astype(BF16)
            da = dc_main[rows, 0:sgu]
            dmixed = da * u
            dbs_scr[...] += dmixed
            dvn_parts, du_parts = [], []
            for gp in range(sgu // LANES):
                ls = slice(gp * LANES, (gp + 1) * LANES)
                vn_g = vn[:, ls]
                mixed = _head_pair_mix(wt_b[2 * gp], wt_b[2 * gp + 1], vn_g) + bs_ref[:, ls]
                du_parts.append(da[:, ls] * mixed)
                dm_g = dmixed[:, ls]
                dm_b = dm_g.astype(BF16)
                dvn_parts.append(jnp.where(lo, _dot(wtt_b[2 * gp], dm_b), _dot(wtt_b[2 * gp + 1], dm_b)))
                dws[2 * gp] = dws[2 * gp] + _dot_nt(jnp.where(lo, dm_g, 0.0).astype(BF16), vn_g)
                dws[2 * gp + 1] = dws[2 * gp + 1] + _dot_nt(jnp.where(lo, 0.0, dm_g).astype(BF16), vn_g)
            dvn = jnp.concatenate(dvn_parts, axis=1)
            du = jnp.concatenate(du_parts, axis=1)
            dv, dg_n, db_n = _ln_bwd(vhat, vrstd, lng, dvn)
            d_lng = d_lng + dg_n
            d_lnb = d_lnb + db_n
            dz_ref[rows, 0:sgu] = (du * _gelu_grad(au)).astype(BF16)
            dz_ref[rows, sgu:2 * sgu] = (dv * _gelu_grad(av)).astype(BF16)
        for h in range(n_head):
            gws_ref[h] += _tril(dws[h])
        g384_ref[_R_SGU_G:_R_SGU_G + 1, :] += d_lng
        g384_ref[_R_SGU_B:_R_SGU_B + 1, :] += d_lnb

        clg = clg_ref[...]
        bcols = slice(2 * sgu, 4 * sgu)
        zb = jnp.concatenate([z_prev[:, bcols], z_main[:, bcols], z_next[:, bcols]], axis=0)
        bval, bgate = zb[:, 0:sgu], zb[:, sgu:2 * sgu]
        sg = jax.nn.sigmoid(bgate)
        hglu = bval * sg
        n_out = tm + HALO
        hglu_shifts = _sub_tile_shifts(hglu, _shift_down)
        cw = cw_ref[...]
        hc = jnp.concatenate([hc_ref[...], jnp.where(last, 0.0, hcn_ref[...])], axis=0)
        hhat, hrstd = _ln(hc)
        bn = hhat * clg + clb_ref[...]
        db = jnp.concatenate([dc_main[:, sgu:2 * sgu], dc_next[:, sgu:2 * sgu]], axis=0)
        dbn = db * _silu_grad(bn)
        dhc_all, _, _ = _ln_bwd(hhat, hrstd, clg, dbn)
        dbn_m, hhat_m, dhc = dbn[:tm], hhat[:tm], dhc_all[:tm]
        g384_ref[_R_CLN_G:_R_CLN_G + 1, :] += jnp.sum(dbn_m * hhat_m, axis=0, keepdims=True)
        g384_ref[_R_CLN_B:_R_CLN_B + 1, :] += jnp.sum(dbn_m, axis=0, keepdims=True)
        g384_ref[_R_CONV_B:_R_CONV_B + 1, :] += jnp.sum(dhc, axis=0, keepdims=True)
        wrows = _conv_wgrad(hglu_shifts, dhc, tm)
        for k in range(CONV_WIDTH):
            g384_ref[_R_CONV_W + k:_R_CONV_W + k + 1, :] += wrows[k]
        dhglu = _depthwise(_sub_tile_shifts(dhc_all, _shift_up), cw, tm, False)
        bval_m, sg_m = bval[HALO:HALO + tm], sg[HALO:HALO + tm]
        dz_ref[:, 2 * sgu:3 * sgu] = (dhglu * sg_m).astype(BF16)
        dz_ref[:, 3 * sgu:4 * sgu] = (dhglu * bval_m * sg_m * (1.0 - sg_m)).astype(BF16)

        bd_b = bd_ref[...].astype(BF16)
        ps = ps_ref[...]
        p_main = z_main[:, 4 * sgu:]
        pext = jnp.concatenate([z_prev[:, 4 * sgu:], p_main], axis=0)
        cnt = _pool_count(i * tm, n_out, pool)
        pooled = _window_sums(pext, _shift_down)[HALO:] / cnt[:tm] - p_main
        pooled_b = pooled.astype(BF16)
        dcc = jnp.concatenate([dc_main[:, 2 * sgu:], dc_next[:, 2 * sgu:]], axis=0)
        dmix_c = dcc * ps
        mixed_c = _dot(pooled_b, bd_b)
        grp_r = lax.broadcasted_iota(jnp.int32, (pool, pool), 0) // HEAD_DIM
        grp_c = lax.broadcasted_iota(jnp.int32, (pool, pool), 1) // HEAD_DIM
        gpool_ref[0:pool, :] += jnp.where(grp_r == grp_c, _dot_tn(pooled_b, dmix_c[:tm].astype(BF16)), 0.0)
        gpool_ref[pool:pool + 1, :] += jnp.sum(dcc[:tm] * mixed_c, axis=0, keepdims=True)
        dpooled = _dot_nt(dmix_c.astype(BF16), bd_b)
        q = dpooled / cnt
        dp = _window_sums(q, _shift_up)[:tm] - dpooled[:tm]
        dz_ref[:, 4 * sgu:] = dp.astype(BF16)

        gain = g_ref[...]
        xhat, rstd = _rms(x_ref[...])
        hm_ref[...] = (xhat * gain).astype(BF16)
        dxn, dgm = _rms_bwd(xhat, rstd, gain, _dot(dz_ref[...], wi_ref[...]))
        dx_ref[...] = dxo_ref[...] + dxn
        dgm_ref[...] += dgm

        @pl.when(last)
        def _():
            r = lax.broadcasted_iota(jnp.int32, (sgu, LANES), 0)
            c = lax.broadcasted_iota(jnp.int32, (sgu, LANES), 1)
            sel = (r // HEAD_DIM == c).astype(BF16)
            gws_ref[n_head] = _split_dot(dbs_scr[...], sel)

    def vec(n):
        return pl.BlockSpec((1, n), lambda i: (0, 0))

    def prev_map(i):
        return (jnp.maximum(i * hb - 1, 0), 0)

    def next_map(i):
        return (jnp.minimum((i + 1) * hb, n_i * hb - 1), 0)

    return _call(
        body, name, (n_i,),
        [pl.BlockSpec((tm, d), lambda i: (i, 0)), pl.BlockSpec((1, d), lambda i: (0, 0)),
         pl.BlockSpec((None, d_in, d), lambda i: (0, 0, 0)),
         pl.BlockSpec((tm, d_in), lambda i: (i, 0)),
         pl.BlockSpec((HALO, d_in), prev_map), pl.BlockSpec((HALO, d_in), next_map),
         pl.BlockSpec((tm, d), lambda i: (i, 0)), pl.BlockSpec((HALO, d), next_map),
         pl.BlockSpec((None, d_mix, d), lambda i: (0, 0, 0)),
         pl.BlockSpec((tm, sgu), lambda i: (i, 0)), pl.BlockSpec((HALO, sgu), next_map),
         vec(sgu), vec(sgu),
         pl.BlockSpec(p["w_spatial"].shape, lambda i: (0, 0, 0)),
         pl.BlockSpec((CHUNK, sgu), lambda i: (0, 0)),
         pl.BlockSpec((CONV_WIDTH, sgu), lambda i: (0, 0)),
         vec(sgu), vec(sgu),
         pl.BlockSpec((pool, pool), lambda i: (0, 0)), vec(pool)],
        [pl.BlockSpec((tm, d), lambda i: (i, 0)), pl.BlockSpec((tm, d_in), lambda i: (i, 0)),
         pl.BlockSpec((tm, d), lambda i: (i, 0)), pl.BlockSpec((1, d), lambda i: (0, 0)),
         pl.BlockSpec((_R384_ROWS, sgu), lambda i: (0, 0)),
         pl.BlockSpec((n_head + 1, CHUNK, CHUNK), lambda i: (0, 0, 0)),
         pl.BlockSpec((pool + 8, pool), lambda i: (0, 0))],
        [jax.ShapeDtypeStruct((t, d), F32), jax.ShapeDtypeStruct((t, d_in), BF16),
         jax.ShapeDtypeStruct((t, d), BF16), jax.ShapeDtypeStruct((1, d), F32),
         jax.ShapeDtypeStruct((_R384_ROWS, sgu), F32),
         jax.ShapeDtypeStruct((n_head + 1, CHUNK, CHUNK), F32),
         jax.ShapeDtypeStruct((pool + 8, pool), F32)],
        [pltpu.VMEM((CHUNK, sgu), F32)], ("arbitrary",),
        (x, g, wb, z, z, z, dxo, dxo, wc, hc_saved, hc_saved, p["sgu_ln_g"], p["sgu_ln_b"], p["w_spatial"],
         p["bs_full"], p["conv_w"], p["conv_ln_g"], p["conv_ln_b"], p["bd"], p["pool_scale"]), payload)


def _all_gather(arrs, name, extra=None, to_sum=()):
    gather = _GatherIci(arrs)
    n = len(arrs)
    forward = _GatherForward([jax.ShapeDtypeStruct(s.shape, s.dtype) for s in gather.out_shapes])
    x_in = len(extra.ins) if extra else 0
    x_out = len(extra.out_shapes) if extra else 0
    n_sum = len(to_sum)

    def body(*refs):
        ins, x_ins, s_ins = refs[:n], refs[n:n + x_in], refs[n + x_in:n + x_in + n_sum]
        o0 = n + x_in + n_sum
        outs, x_outs = refs[o0:o0 + n], refs[o0 + n:o0 + n + x_out]
        s_outs = refs[o0 + n + x_out:o0 + n + x_out + n_sum]
        sems = refs[o0 + n + x_out + n_sum:]
        first = gather.build(ins, outs, *sems[0:3])
        first.start()
        if extra:
            beside = extra.build(x_ins, x_outs, *sems[6:9])
            beside.start()
        for s_ref, o_ref in zip(s_ins, s_outs):
            r = o_ref.shape[0]
            acc = s_ref[0:r, :]
            for q in range(1, N_DEV):
                acc = acc + s_ref[q * r:(q + 1) * r, :]
            o_ref[...] = acc
        first.wait()
        second = forward.build(outs, outs, *sems[3:6])
        second.start()
        second.wait()
        if extra:
            beside.wait()

    in_vmem = pl.BlockSpec(memory_space=pltpu.VMEM)
    outs = pl.pallas_call(
        body, name=name,
        in_specs=[ANY] * (n + x_in) + [in_vmem] * n_sum, out_specs=[ANY] * (n + x_out) + [in_vmem] * n_sum,
        out_shape=list(gather.out_shapes) + (list(extra.out_shapes) if extra else [])
        + [jax.ShapeDtypeStruct((s.shape[0] // N_DEV, s.shape[1]), s.dtype) for s in to_sum],
        scratch_shapes=gather.sem_shapes() + forward.sem_shapes() + (extra.sem_shapes() if extra else []),
        compiler_params=_cparams(),
    )(*arrs, *(extra.ins if extra else []), *to_sum)
    return list(outs[:n]), list(outs[n:n + x_out]), list(outs[n + x_out:])


def _all_gather_relayed(arrs, name):
    n = len(arrs)
    n_pairs = 8

    def body(*refs):
        ins, outs = refs[:n], refs[n:2 * n]
        send_sems, recv_sems, local_sems = refs[2 * n:]
        x, y, c = _position()
        sib, xn, yn = (x, y, 1 - c), (1 - x, y, c), (x, 1 - y, c)

        def rows(a, px, py, pc, half=None):
            r = ins[a].shape[1]
            base = (4 * px + 2 * py + pc) * r
            if half is None:
                return outs[a].at[:, pl.ds(base, r), :]
            return outs[a].at[:, pl.ds(base + half * (r // 2), r // 2), :]

        def send(a, k, src, dst, to):
            return _remote(src, dst, send_sems, recv_sems, a * n_pairs + k, to)

        def arrived(a, k, land, sender):
            _remote(land, land, send_sems, recv_sems, a * n_pairs + k, sender).wait_recv()

        own = [pltpu.make_async_copy(ins[a], rows(a, x, y, c), local_sems.at[a]) for a in range(n)]
        first = [send(a, k, ins[a], rows(a, x, y, c), to) for a in range(n) for k, to in enumerate((sib, xn, yn))]
        for cp in own + first:
            cp.start()
        for a in range(n):
            arrived(a, 1, rows(a, *xn), xn)
            arrived(a, 2, rows(a, *yn), yn)
        second = []
        for a in range(n):
            second += [send(a, 3, rows(a, *xn, half=0), rows(a, *xn, half=0), yn),
                       send(a, 4, rows(a, *yn, half=1), rows(a, *yn, half=1), xn),
                       send(a, 5, rows(a, *xn), rows(a, *xn), sib),
                       send(a, 6, rows(a, *yn), rows(a, *yn), sib)]
        for cp in second:
            cp.start()
        for a in range(n):
            arrived(a, 3, rows(a, 1 - x, 1 - y, c, half=0), yn)
            arrived(a, 4, rows(a, 1 - x, 1 - y, c, half=1), xn)
        third = [send(a, 7, rows(a, 1 - x, 1 - y, c), rows(a, 1 - x, 1 - y, c), sib) for a in range(n)]
        for cp in third:
            cp.start()
        for a in range(n):
            arrived(a, 0, rows(a, *sib), sib)
            arrived(a, 5, rows(a, 1 - x, y, 1 - c), sib)
            arrived(a, 6, rows(a, x, 1 - y, 1 - c), sib)
            arrived(a, 7, rows(a, 1 - x, 1 - y, 1 - c), sib)
        for cp in first + second + third:
            cp.wait_send()
        for cp in own:
            cp.wait()

    return list(pl.pallas_call(
        body, name=name, in_specs=[ANY] * n, out_specs=[ANY] * n,
        out_shape=[jax.ShapeDtypeStruct((a.shape[0], N_DEV * a.shape[1], a.shape[2]), a.dtype) for a in arrs],
        scratch_shapes=[pltpu.SemaphoreType.DMA((n_pairs * n,)), pltpu.SemaphoreType.DMA((n_pairs * n,)),
                        pltpu.SemaphoreType.DMA((n,))],
    )(*arrs))


def _pair_sums(grads, recvs, cidx, name):
    n = len(grads)

    def body(c_ref, *refs):
        for g_ref, r_ref, o_ref in zip(refs[:n], refs[n:2 * n], refs[2 * n:]):
            o_ref[...] = (g_ref[...].astype(F32) + r_ref[...].astype(F32)).astype(BF16)

    shapes = [(g.shape[0] // N_DEV, g.shape[1]) for g in grads]
    return list(pl.pallas_call(
        body, name=name,
        grid_spec=pltpu.PrefetchScalarGridSpec(
            num_scalar_prefetch=1, grid=(N_CHIP,),
            in_specs=[pl.BlockSpec(s, lambda q, c: (2 * q + c[0], 0)) for s in shapes]
            + [pl.BlockSpec(s, lambda q, c: (q, 0)) for s in shapes],
            out_specs=[pl.BlockSpec(s, lambda q, c: (q, 0)) for s in shapes]),
        out_shape=[jax.ShapeDtypeStruct((N_CHIP * r, cols), BF16) for r, cols in shapes],
        compiler_params=_cparams(("parallel",)),
    )(cidx, *grads, *recvs))


def _sum_blocks(parts, nblk, name):
    r = parts.shape[0] // nblk
    cols = parts.shape[1]

    def body(p_ref, o_ref):
        acc = p_ref[0:r, :].astype(F32)
        for q in range(1, nblk):
            acc = acc + p_ref[q * r:(q + 1) * r, :].astype(F32)
        o_ref[...] = acc

    return pl.pallas_call(
        body, name=name,
        out_shape=jax.ShapeDtypeStruct((r, cols), F32),
        compiler_params=_cparams(),
    )(parts)


def _adamw_math(w, g, m, v):
    m = ADAM_B1 * m + (1.0 - ADAM_B1) * g
    v = ADAM_B2 * v + (1.0 - ADAM_B2) * (g * g)
    m_hat = m / (1.0 - ADAM_B1 ** ADAM_STEP)
    v_hat = v / (1.0 - ADAM_B2 ** ADAM_STEP)
    delta = -ADAM_LR * (m_hat / (jnp.sqrt(v_hat) + ADAM_EPS) + ADAM_WD * w)
    return delta, m, v


def _finish_sharded(parts, w, m, v, name):
    depth, rr, cw = w.shape

    def body(*refs):
        p_refs = refs[:depth]
        w_ref, m_ref, v_ref, g_ref, d_ref, mo_ref, vo_ref = refs[depth:]
        l = pl.program_id(0)
        for k in range(depth):
            @pl.when(l == k)
            def _(p_ref=p_refs[k]):
                r = p_ref.shape[0] // N_CHIP
                acc = p_ref[0:r, :].astype(F32)
                for q in range(1, N_CHIP):
                    acc = acc + p_ref[q * r:(q + 1) * r, :].astype(F32)
                g_ref[...] = acc
                d_ref[...], mo_ref[...], vo_ref[...] = _adamw_math(w_ref[...], acc, m_ref[...], v_ref[...])

    blk = pl.BlockSpec((None, rr, cw), lambda l: (l, 0, 0))
    return pl.pallas_call(
        body, name=name, grid=(depth,),
        in_specs=[pl.BlockSpec(p.shape, lambda l: (0, 0)) for p in parts] + [blk] * 3, out_specs=[blk] * 4,
        out_shape=[jax.ShapeDtypeStruct(w.shape, F32)] * 4,
        compiler_params=_cparams(("arbitrary",)),
    )(*parts, w, m, v)


def _adamw_small(ws, gs, ms, vs, name):
    n = len(ws)

    def body(*refs):
        for k in range(n):
            w_ref, g_ref, m_ref, v_ref = (refs[j * n + k] for j in range(4))
            d_ref, mo_ref, vo_ref = (refs[(4 + j) * n + k] for j in range(3))
            d_ref[...], mo_ref[...], vo_ref[...] = _adamw_math(w_ref[...], g_ref[...], m_ref[...], v_ref[...])

    shapes = [jax.ShapeDtypeStruct(w.shape, F32) for w in ws]
    return pl.pallas_call(
        body, name=name, out_shape=shapes * 3, compiler_params=_cparams(),
    )(*ws, *gs, *ms, *vs)


def kernel(x, ffn1_norm, ffn1_w_gate, ffn1_w_up, ffn1_w_down, mix_norm, w_in, sgu_ln_g, sgu_ln_b, w_spatial, b_spatial, conv_w, conv_b, conv_ln_g, conv_ln_b, pool_w, pool_scale, w_out, ffn2_norm, ffn2_w_gate, ffn2_w_up, ffn2_w_down, final_norm, loss_target, m_ffn1_norm, m_ffn1_w_gate, m_ffn1_w_up, m_ffn1_w_down, m_mix_norm, m_w_in, m_sgu_ln_g, m_sgu_ln_b, m_w_spatial, m_b_spatial, m_conv_w, m_conv_b, m_conv_ln_g, m_conv_ln_b, m_pool_w, m_pool_scale, m_w_out, m_ffn2_norm, m_ffn2_w_gate, m_ffn2_w_up, m_ffn2_w_down, m_final_norm, v_ffn1_norm, v_ffn1_w_gate, v_ffn1_w_up, v_ffn1_w_down, v_mix_norm, v_w_in, v_sgu_ln_g, v_sgu_ln_b, v_w_spatial, v_b_spatial, v_conv_w, v_conv_b, v_conv_ln_g, v_conv_ln_b, v_pool_w, v_pool_scale, v_w_out, v_ffn2_norm, v_ffn2_w_gate, v_ffn2_w_up, v_ffn2_w_down, v_final_norm):
    names = ["ffn1_norm", "ffn1_w_gate", "ffn1_w_up", "ffn1_w_down", "mix_norm", "w_in", "sgu_ln_g", "sgu_ln_b",
             "w_spatial", "b_spatial", "conv_w", "conv_b", "conv_ln_g", "conv_ln_b", "pool_w", "pool_scale",
             "w_out", "ffn2_norm", "ffn2_w_gate", "ffn2_w_up", "ffn2_w_down", "final_norm"]
    W = dict(zip(names, [ffn1_norm, ffn1_w_gate, ffn1_w_up, ffn1_w_down, mix_norm, w_in, sgu_ln_g, sgu_ln_b,
                         w_spatial, b_spatial, conv_w, conv_b, conv_ln_g, conv_ln_b, pool_w, pool_scale, w_out,
                         ffn2_norm, ffn2_w_gate, ffn2_w_up, ffn2_w_down, final_norm]))
    M = dict(zip(names, [m_ffn1_norm, m_ffn1_w_gate, m_ffn1_w_up, m_ffn1_w_down, m_mix_norm, m_w_in, m_sgu_ln_g,
                         m_sgu_ln_b, m_w_spatial, m_b_spatial, m_conv_w, m_conv_b, m_conv_ln_g, m_conv_ln_b,
                         m_pool_w, m_pool_scale, m_w_out, m_ffn2_norm, m_ffn2_w_gate, m_ffn2_w_up, m_ffn2_w_down,
                         m_final_norm]))
    V = dict(zip(names, [v_ffn1_norm, v_ffn1_w_gate, v_ffn1_w_up, v_ffn1_w_down, v_mix_norm, v_w_in, v_sgu_ln_g,
                         v_sgu_ln_b, v_w_spatial, v_b_spatial, v_conv_w, v_conv_b, v_conv_ln_g, v_conv_ln_b,
                         v_pool_w, v_pool_scale, v_w_out, v_ffn2_norm, v_ffn2_w_gate, v_ffn2_w_up, v_ffn2_w_down,
                         v_final_norm]))

    depth, d = ffn1_norm.shape
    t = x.shape[1]
    sgu = sgu_ln_g.shape[1]
    pool = pool_scale.shape[1]
    n_head = sgu // HEAD_DIM
    cw_shard = conv_w.shape[2]
    xs = x.reshape(t, d)
    target = loss_target.reshape(t, d)

    def tr(w):
        return jnp.swapaxes(w, 1, 2).astype(BF16)

    ffn_shards = [[jnp.stack([tr(ffn1_w_gate)[l], tr(ffn1_w_up)[l], ffn1_w_down[l].astype(BF16)]),
                   jnp.stack([tr(ffn2_w_gate)[l], tr(ffn2_w_up)[l], ffn2_w_down[l].astype(BF16)])]
                  for l in range(depth)]
    win_shards = [tr(w_in)[l:l + 1] for l in range(depth)]
    wout_shards = [w_out[l:l + 1].astype(BF16) for l in range(depth)]
    cw_rows = depth * CONV_WIDTH
    cw_pad = -cw_rows % 8
    cw_send = jnp.pad(conv_w.reshape(cw_rows, cw_shard), ((0, cw_pad), (0, 0)))[None]
    wffn, wb, wc = {}, {}, {}
    wffn[(0, 0)], wb[0], wc[0], cwg = _all_gather_relayed(
        [ffn_shards[0][0], win_shards[0], wout_shards[0], cw_send], "ag_first")
    conv_w_full = cwg.reshape(N_DEV, cw_rows + cw_pad, cw_shard)[:, :cw_rows].reshape(
        N_DEV, depth, CONV_WIDTH, cw_shard).transpose(1, 2, 0, 3).reshape(depth, CONV_WIDTH, N_DEV * cw_shard)

    def mixer_params(l):
        return dict(
            sgu_ln_g=sgu_ln_g[l:l + 1], sgu_ln_b=sgu_ln_b[l:l + 1], w_spatial=w_spatial[l],
            bs_full=jnp.repeat(b_spatial[l].T, HEAD_DIM, axis=1),
            conv_w=conv_w_full[l], conv_b=conv_b[l:l + 1], conv_ln_g=conv_ln_g[l:l + 1],
            conv_ln_b=conv_ln_b[l:l + 1], bd=_block_diag(pool_w[l]), pool_scale=pool_scale[l:l + 1])

    saved = []
    cur = xs
    for l in range(depth):
        p = mixer_params(l)
        x0 = cur
        more = l + 1 < depth
        (x1, gate1, up1, act1), part = _ffn_fwd(x0, ffn1_norm[l:l + 1], wffn[(l, 0)], 0, f"ffn1_fwd_{l}",
                                          _GatherIci([ffn_shards[l][1]]))
        riding = [_GatherForward(part)] + ([_GatherIci([win_shards[l + 1], wout_shards[l + 1]])] if more else [])
        (x2, z, hc, cat), part = _mixer_fwd(x1, mix_norm[l:l + 1], wb[l], wc[l], p, f"mixer_fwd_{l}", _Merged(riding))
        wffn[(l, 1)] = part[0]
        riding = [_GatherIci([ffn_shards[l + 1][0]]), _GatherForward(part[1:])] if more else []
        relay = _GatherForward(riding[0].out_shapes) if more else None
        outs, part = _ffn_fwd(x2, ffn2_norm[l:l + 1], wffn[(l, 1)], 0, f"ffn2_fwd_{l}",
                              _Merged(riding) if more else None,
                              None if more else (final_norm.reshape(1, d), target), relay)
        if more:
            cur, gate2, up2, act2 = outs
            wffn[(l + 1, 0)], wb[l + 1], wc[l + 1] = part
        else:
            dx, gate2, up2, act2, d_final, loss_part = outs
        saved.append((p, x0, gate1, up1, act1, x1, z, hc, cat, x2, gate2, up2, act2))

    cidx = lax.axis_index("c").astype(jnp.int32).reshape(1)
    from_chips = {}
    to_pair, to_chip = [], []
    small = []

    def pair_payload():
        return _PairExchange([g for _, g in to_pair]) if to_pair else None

    def pair_done(received):
        if to_pair:
            (nm, l), _ = to_pair[0]
            sums = _pair_sums([g for _, g in to_pair], list(received), cidx, f"rs_pair_sum_{nm}_{l}")
            to_chip.extend((key, s) for (key, _), s in zip(to_pair, sums))
        to_pair.clear()

    def take_chip():
        items = list(to_chip)
        to_chip.clear()
        return items

    def chip_payload(items):
        return _ChipExchange([s for _, s in items]) if items else None

    def chip_done(items, landed):
        for (key, _), o in zip(items, landed):
            from_chips[key] = o

    def ffn_weight_grads(prefix, l, dgate, dup, act, h, dy):
        items = take_chip()
        items, later = items[:2], items[2:]
        to_chip.extend(later)
        grads3, landed = _ffn_dw(dgate, dup, act, h, dy, f"dw_{prefix}_{l}", chip_payload(items))
        chip_done(items, landed)
        to_pair.extend(((f"{prefix}_{nm}", l), g) for nm, g in zip(("w_gate", "w_up", "w_down"), grads3))

    for l in reversed(range(depth)):
        p, x0, gate1, up1, act1, x1, z, hc, cat, x2, gate2, up2, act2 = saved[l]
        (dx, dgate, dup, h, dy, dg_ffn2), received = _ffn_bwd(
            x2, ffn2_norm[l:l + 1], dx, gate2, up2, wffn[(l, 1)], 0, f"ffn2_bwd_{l}", pair_payload())
        pair_done(received)
        ffn_weight_grads("ffn2", l, dgate, dup, act2, h, dy)
        g_out, received = _tn_matmul(cat, dx, f"dw_w_out_{l}", pair_payload())
        pair_done(received)
        items = take_chip()
        items, later = items[:3], items[3:]
        to_chip.extend(later)
        (dx, dz, hm, dg_mix, g384, gws, gpool), landed = _mixer_bwd(
            x1, mix_norm[l:l + 1], z, hc, dx, wb[l], wc[l], p, f"mixer_bwd_{l}", chip_payload(items))
        chip_done(items, landed)
        g_in, _ = _tn_matmul(dz, hm, f"dw_w_in_{l}")
        to_pair.extend([(("w_out", l), g_out), (("w_in", l), g_in)])
        if l > 0:
            (dx, dgate, dup, h, dy, dg_ffn1), received = _ffn_bwd(
                x0, ffn1_norm[l:l + 1], dx, gate1, up1, wffn[(l, 0)], 0, f"ffn1_bwd_{l}", pair_payload())
            pair_done(received)
            ffn_weight_grads("ffn1", l, dgate, dup, act1, h, dy)
            small.append((l, g384, gws, gpool, dg_ffn1, dg_mix, dg_ffn2))
            continue

        small.append((0, g384, gws, gpool, None, dg_mix, dg_ffn2))
        small.sort(key=lambda s: s[0])
        norm_rows = []
        for (sl, _, _, _, dg1, dgm, dg2) in small:
            norm_rows += [jnp.zeros((1, d), F32) if dg1 is None else dg1, dgm, dg2]
        norm_rows += [d_final, jnp.pad(loss_part, ((0, 0), (0, d - LANES)))]
        n_norm = len(norm_rows)
        norm_pack = jnp.concatenate(norm_rows + [jnp.zeros((8 - n_norm % 8, d), F32)] * (n_norm % 8 != 0), axis=0)
        parts = [norm_pack]
        for (_, s384, sws, spool, _, _, _) in small:
            parts += [s384, sws.reshape((n_head + 1) * CHUNK, CHUNK), spool]
        n_pair = len(to_pair)
        early = take_chip()
        riding = [pair_payload(), _GatherIci([a[None] for a in parts])] + ([chip_payload(early)] if early else [])
        (dx, dgate, dup, h, dy, dg_ffn1), landed = _ffn_bwd(
            x0, ffn1_norm[l:l + 1], dx, gate1, up1, wffn[(l, 0)], 0, f"ffn1_bwd_{l}", _Merged(riding))
        pair_done(landed[:n_pair])
        chip_done(early, landed[n_pair + len(parts):])
        items = take_chip()
        g_gate, landed = _tn_matmul(
            dgate, h, f"dw_ffn1_w_gate_{l}",
            _Merged([chip_payload(items), _GatherForward(landed[n_pair:n_pair + len(parts)])]))
        chip_done(items, landed[:len(items)])
        gathered = landed[len(items):]
        to_pair.append((("ffn1_w_gate", l), g_gate))
        g_up, received = _tn_matmul(dup, h, f"dw_ffn1_w_up_{l}", pair_payload())
        pair_done(received)
        to_pair.append((("ffn1_w_up", l), g_up))
        items = take_chip()
        g_down, landed = _tn_matmul(act1, dy, f"dw_ffn1_w_down_{l}", _Merged([chip_payload(items), pair_payload()]))
        chip_done(items, landed[:len(items)])
        pair_done(landed[len(items):])
        to_pair.append((("ffn1_w_down", l), g_down))
    grad_x = dx.reshape(x.shape)
    pair_done(_comm(pair_payload(), "rs_pair_exchange_last"))
    items = take_chip()
    (late_norm,), landed, summed = _all_gather([jnp.pad(dg_ffn1, ((0, 7), (0, 0)))[None]], "ag_tail",
                                               chip_payload(items), [g[0] for g in gathered])
    chip_done(items, landed)
    late_sum = _sum_blocks(late_norm[0], N_DEV, "sum_small_late")
    norm_sum = summed[0]
    loss = norm_sum[3 * depth + 1, 0]
    cpos = lax.axis_index("x") * 4 + lax.axis_index("y") * 2 + lax.axis_index("c")
    sg = {nm: [] for nm in names}
    for l in range(depth):
        g384, gws, gpool = summed[1 + 3 * l], summed[2 + 3 * l].reshape(n_head + 1, CHUNK, CHUNK), summed[3 + 3 * l]
        sg["ffn1_norm"].append(norm_sum[3 * l] if l > 0 else late_sum[0])
        sg["mix_norm"].append(norm_sum[3 * l + 1])
        sg["ffn2_norm"].append(norm_sum[3 * l + 2])
        sg["sgu_ln_g"].append(g384[_R_SGU_G])
        sg["sgu_ln_b"].append(g384[_R_SGU_B])
        sg["conv_b"].append(g384[_R_CONV_B])
        sg["conv_ln_g"].append(g384[_R_CLN_G])
        sg["conv_ln_b"].append(g384[_R_CLN_B])
        sg["conv_w"].append(lax.dynamic_slice_in_dim(g384[_R_CONV_W:_R_CONV_W + CONV_WIDTH], cpos * cw_shard,
                                                     cw_shard, axis=1))
        sg["w_spatial"].append(gws[:n_head])
        sg["b_spatial"].append(gws[n_head][:, :n_head].T)
        sg["pool_w"].append(jnp.stack([gpool[k * HEAD_DIM:(k + 1) * HEAD_DIM, k * HEAD_DIM:(k + 1) * HEAD_DIM]
                                       for k in range(pool // HEAD_DIM)], axis=0))
        sg["pool_scale"].append(gpool[pool])
    small_names = ["ffn1_norm", "mix_norm", "sgu_ln_g", "sgu_ln_b", "w_spatial", "b_spatial", "conv_w", "conv_b",
                   "conv_ln_g", "conv_ln_b", "pool_w", "pool_scale", "ffn2_norm"]
    grads = {nm: jnp.stack(sg[nm], axis=0) for nm in small_names}
    grads["final_norm"] = norm_sum[3 * depth]

    delta, new_m, new_v = {}, {}, {}
    big_names = ["ffn1_w_gate", "ffn1_w_up", "ffn1_w_down", "w_in", "w_out", "ffn2_w_gate", "ffn2_w_up",
                 "ffn2_w_down"]
    transposed = {"ffn1_w_gate", "ffn1_w_up", "w_in", "ffn2_w_gate", "ffn2_w_up"}
    for nm in big_names:
        view = (lambda a: jnp.swapaxes(a, 1, 2)) if nm in transposed else (lambda a: a)
        outs = _finish_sharded([from_chips[(nm, l)] for l in range(depth)], view(W[nm]), view(M[nm]), view(V[nm]),
                               f"adamw_{nm}")
        grads[nm], delta[nm], new_m[nm], new_v[nm] = (view(o) for o in outs)
    snames = small_names + ["final_norm"]

    def flat2(a):
        return a.reshape(-1, a.shape[-1])

    outs = _adamw_small([flat2(W[nm]) for nm in snames], [flat2(grads[nm]) for nm in snames],
                        [flat2(M[nm]) for nm in snames], [flat2(V[nm]) for nm in snames], "adamw_small")
    ns = len(snames)
    for k, nm in enumerate(snames):
        shp = W[nm].shape
        delta[nm], new_m[nm], new_v[nm] = (outs[k].reshape(shp), outs[ns + k].reshape(shp),
                                           outs[2 * ns + k].reshape(shp))

    return (loss, grad_x, *[grads[nm] for nm in names], *[delta[nm] for nm in names],
            *[new_m[nm] for nm in names], *[new_v[nm] for nm in names])
```

```python
import functools

import jax
import jax.numpy as jnp
from jax import lax
from jax.experimental import pallas as pl
from jax.experimental.pallas import tpu as pltpu

F32 = jnp.float32
BF16 = jnp.bfloat16
EPS = 1e-6
N_DEV = 8
N_CHIP = 4
MESH = pl.DeviceIdType.MESH
ANY = pl.BlockSpec(memory_space=pl.ANY)

VMEM_LIMIT_BYTES = 56 * 1024 * 1024
LANES = 128
HALO = 32
HEAD_DIM = 64
CHUNK = 128
CONV_WIDTH = 31
POOL_WINDOWS = (2, 4, 8, 16)

ADAM_LR = 0.001
ADAM_B1 = 0.9
ADAM_B2 = 0.999
ADAM_EPS = 1e-08
ADAM_WD = 0.01
ADAM_STEP = 10


def _cparams(sem=None):
    return pltpu.CompilerParams(dimension_semantics=sem, vmem_limit_bytes=VMEM_LIMIT_BYTES)


def _position():
    return lax.axis_index("x"), lax.axis_index("y"), lax.axis_index("c")


class _Copies:
    def __init__(self):
        self.local, self.sends, self.recvs = [], [], []

    def extend(self, other):
        self.local += other.local
        self.sends += other.sends
        self.recvs += other.recvs

    def start(self):
        for cp in self.local + self.sends:
            cp.start()

    def wait(self):
        for land, send_sems, recv_sems, k, peer in self.recvs:
            _remote(land, land, send_sems, recv_sems, k, peer).wait_recv()
        for cp in self.sends:
            cp.wait_send()
        for cp in self.local:
            cp.wait()


def _remote(src, dst, send_sems, recv_sems, k, to):
    return pltpu.make_async_remote_copy(src_ref=src, dst_ref=dst, send_sem=send_sems.at[k], recv_sem=recv_sems.at[k],
                                        device_id=to, device_id_type=MESH)


class _Payload:
    ins, out_shapes, aliases, n_remote, n_local = (), (), {}, 0, 0

    def sem_shapes(self):
        return [pltpu.SemaphoreType.DMA((max(self.n_remote, 1),)), pltpu.SemaphoreType.DMA((max(self.n_remote, 1),)),
                pltpu.SemaphoreType.DMA((max(self.n_local, 1),))]


class _GatherIci(_Payload):
    def __init__(self, shards):
        self.ins = list(shards)
        self.out_shapes = [jax.ShapeDtypeStruct((s.shape[0], N_DEV * s.shape[1], s.shape[2]), s.dtype) for s in shards]
        self.n_remote, self.n_local = 4 * len(shards), len(shards)

    def build(self, ins, outs, send_sems, recv_sems, local_sems, k0=0, l0=0):
        x, y, c = _position()
        peers = [(x, y, 1 - c), (1 - x, y, c), (x, 1 - y, c), (1 - x, 1 - y, c)]
        cps = _Copies()
        for a, (src, out) in enumerate(zip(ins, outs)):
            r = src.shape[1]

            def rows(px, py, pc, out=out, r=r):
                return out.at[:, pl.ds((4 * px + 2 * py + pc) * r, r), :]

            cps.local.append(pltpu.make_async_copy(src, rows(x, y, c), local_sems.at[l0 + a]))
            for k, peer in enumerate(peers):
                cps.sends.append(_remote(src, rows(x, y, c), send_sems, recv_sems, k0 + 4 * a + k, peer))
                cps.recvs.append((rows(*peer), send_sems, recv_sems, k0 + 4 * a + k, peer))
        return cps


class _GatherForward(_Payload):
    def __init__(self, partials):
        self.ins = list(partials)
        self.out_shapes = [jax.ShapeDtypeStruct(p.shape, p.dtype) for p in partials]
        self.aliases = {a: a for a in range(len(partials))}
        self.n_remote = 3 * len(partials)

    def build(self, ins, outs, send_sems, recv_sems, local_sems, k0=0, l0=0):
        x, y, c = _position()
        chips = [(1 - x, y), (x, 1 - y), (1 - x, 1 - y)]
        cps = _Copies()
        for a, out in enumerate(outs):
            r = out.shape[1] // N_DEV
            for k, (px, py) in enumerate(chips):
                mine = out.at[:, pl.ds((4 * px + 2 * py + c) * r, r), :]
                theirs = out.at[:, pl.ds((4 * px + 2 * py + 1 - c) * r, r), :]
                cps.sends.append(_remote(mine, mine, send_sems, recv_sems, k0 + 3 * a + k, (x, y, 1 - c)))
                cps.recvs.append((theirs, send_sems, recv_sems, k0 + 3 * a + k, (x, y, 1 - c)))
        return cps


class _PairExchange(_Payload):
    def __init__(self, grads):
        self.ins = list(grads)
        self.out_shapes = [jax.ShapeDtypeStruct((g.shape[0] // 2, g.shape[1]), g.dtype) for g in grads]
        self.n_remote = N_CHIP * len(grads)

    def build(self, ins, outs, send_sems, recv_sems, local_sems, k0=0, l0=0):
        x, y, c = _position()
        cps = _Copies()
        for a, (src, out) in enumerate(zip(ins, outs)):
            r = src.shape[0] // N_DEV
            for q in range(N_CHIP):
                land = out.at[pl.ds(q * r, r), :]
                cps.sends.append(_remote(src.at[pl.ds((2 * q + 1 - c) * r, r), :], land, send_sems, recv_sems,
                                         k0 + N_CHIP * a + q, (x, y, 1 - c)))
                cps.recvs.append((land, send_sems, recv_sems, k0 + N_CHIP * a + q, (x, y, 1 - c)))
        return cps


class _ChipExchange(_Payload):
    def __init__(self, sums):
        self.ins = list(sums)
        self.out_shapes = [jax.ShapeDtypeStruct(s.shape, s.dtype) for s in sums]
        self.n_remote, self.n_local = 3 * len(sums), len(sums)

    def build(self, ins, outs, send_sems, recv_sems, local_sems, k0=0, l0=0):
        x, y, c = _position()
        my_chip = 2 * x + y
        chips = [(1 - x, y), (x, 1 - y), (1 - x, 1 - y)]
        cps = _Copies()
        for a, (src, out) in enumerate(zip(ins, outs)):
            r = src.shape[0] // N_CHIP
            mine = out.at[pl.ds(my_chip * r, r), :]
            cps.local.append(pltpu.make_async_copy(src.at[pl.ds(my_chip * r, r), :], mine, local_sems.at[l0 + a]))
            for k, (px, py) in enumerate(chips):
                land = out.at[pl.ds((2 * px + py) * r, r), :]
                cps.sends.append(_remote(src.at[pl.ds((2 * px + py) * r, r), :], mine, send_sems, recv_sems,
                                         k0 + 3 * a + k, (px, py, c)))
                cps.recvs.append((land, send_sems, recv_sems, k0 + 3 * a + k, (px, py, c)))
        return cps


class _Merged(_Payload):
    def __init__(self, parts):
        self.parts = list(parts)
        self.ins = [a for p in parts for a in p.ins]
        self.out_shapes = [s for p in parts for s in p.out_shapes]
        self.aliases, self.offsets = {}, []
        i0 = o0 = k0 = l0 = 0
        for p in parts:
            self.offsets.append((i0, o0, k0, l0))
            self.aliases.update({i0 + i: o0 + o for i, o in p.aliases.items()})
            i0, o0, k0, l0 = i0 + len(p.ins), o0 + len(p.out_shapes), k0 + p.n_remote, l0 + p.n_local
        self.n_remote, self.n_local = k0, l0

    def build(self, ins, outs, send_sems, recv_sems, local_sems):
        cps = _Copies()
        for p, (i0, o0, k0, l0) in zip(self.parts, self.offsets):
            cps.extend(p.build(ins[i0:i0 + len(p.ins)], outs[o0:o0 + len(p.out_shapes)], send_sems, recv_sems,
                               local_sems, k0, l0))
        return cps


RELAY_AT = 0.88


def _call(body, name, grid, in_specs, out_specs, out_shape, scratch_shapes, semantics, args, payload=None,
          relay=None):
    if payload is None:
        outs = pl.pallas_call(body, name=name, grid=grid, in_specs=in_specs, out_specs=out_specs,
                              out_shape=out_shape, scratch_shapes=scratch_shapes,
                              compiler_params=_cparams(semantics))(*args)
        return list(outs), []
    n_in, n_out, n_scr = len(in_specs), len(out_specs), len(scratch_shapes)
    p_in, p_out = len(payload.ins), len(payload.out_shapes)

    def carried(*refs):
        ins, p_ins = refs[:n_in], refs[n_in:n_in + p_in]
        o0 = n_in + p_in
        outs, p_outs = refs[o0:o0 + n_out], refs[o0 + n_out:o0 + n_out + p_out]
        s0 = o0 + n_out + p_out
        scr, sems = refs[s0:s0 + n_scr], refs[s0 + n_scr:s0 + n_scr + 3]
        relay_sems = refs[s0 + n_scr + 3:]
        ids = [pl.program_id(k) for k in range(len(grid))]
        at_first = functools.reduce(jnp.logical_and, [i == 0 for i in ids])
        at_last = functools.reduce(jnp.logical_and, [i == g - 1 for i, g in zip(ids, grid)])

        @pl.when(at_first)
        def _():
            payload.build(p_ins, p_outs, *sems).start()

        body(*ins, *outs, *scr)

        if relay is None:
            @pl.when(at_last)
            def _():
                payload.build(p_ins, p_outs, *sems).wait()
        else:
            n_relay = len(relay.out_shapes)
            step = functools.reduce(lambda acc, ig: acc * ig[1] + ig[0], zip(ids, grid), 0)
            total = functools.reduce(lambda a, b: a * b, grid)

            @pl.when(step == int(RELAY_AT * total))
            def _():
                payload.build(p_ins, p_outs, *sems).wait()
                relay.build(p_outs[:n_relay], p_outs[:n_relay], *relay_sems).start()

            @pl.when(at_last)
            def _():
                relay.build(p_outs[:n_relay], p_outs[:n_relay], *relay_sems).wait()

    outs = pl.pallas_call(
        carried, name=name, grid=grid, in_specs=list(in_specs) + [ANY] * p_in,
        out_specs=list(out_specs) + [ANY] * p_out, out_shape=list(out_shape) + list(payload.out_shapes),
        scratch_shapes=list(scratch_shapes) + payload.sem_shapes() + (relay.sem_shapes() if relay else []),
        input_output_aliases={n_in + i: n_out + o for i, o in payload.aliases.items()},
        compiler_params=_cparams(("arbitrary",) * len(grid)))(*args, *payload.ins)
    return list(outs[:n_out]), list(outs[n_out:])


def _comm(payload, name):
    def body(*refs):
        p_in, p_out = len(payload.ins), len(payload.out_shapes)
        cps = payload.build(refs[:p_in], refs[p_in:p_in + p_out], *refs[p_in + p_out:])
        cps.start()
        cps.wait()

    return list(pl.pallas_call(
        body, name=name, in_specs=[ANY] * len(payload.ins), out_specs=[ANY] * len(payload.out_shapes),
        out_shape=list(payload.out_shapes), scratch_shapes=payload.sem_shapes(),
        input_output_aliases=dict(payload.aliases))(*payload.ins))


def _dot(a, b):
    return jnp.dot(a, b, preferred_element_type=F32)


def _dot_nt(a, b):
    return lax.dot_general(a, b, (((1,), (1,)), ((), ())), preferred_element_type=F32)


def _dot_tn(a, b):
    return lax.dot_general(a, b, (((0,), (0,)), ((), ())), preferred_element_type=F32)


def _split_dot(x, e):
    hi = x.astype(BF16)
    r1 = x - hi.astype(F32)
    mid = r1.astype(BF16)
    lo = (r1 - mid.astype(F32)).astype(BF16)
    return _dot(hi, e) + _dot(mid, e) + _dot(lo, e)


def _rms(x):
    rstd = lax.rsqrt(jnp.mean(x * x, axis=-1, keepdims=True) + EPS)
    return x * rstd, rstd


def _rms_bwd(xhat, rstd, g, dh):
    dxhat = dh * g
    dx = rstd * (dxhat - xhat * jnp.mean(dxhat * xhat, axis=-1, keepdims=True))
    return dx, jnp.sum(dh * xhat, axis=0, keepdims=True)


def _ln(v):
    mu = jnp.mean(v, axis=-1, keepdims=True)
    xc = v - mu
    rstd = lax.rsqrt(jnp.mean(xc * xc, axis=-1, keepdims=True) + EPS)
    return xc * rstd, rstd


def _ln_bwd(vhat, rstd, g, dy):
    dvhat = dy * g
    dv = rstd * (dvhat - jnp.mean(dvhat, axis=-1, keepdims=True)
                 - vhat * jnp.mean(dvhat * vhat, axis=-1, keepdims=True))
    return dv, jnp.sum(dy * vhat, axis=0, keepdims=True), jnp.sum(dy, axis=0, keepdims=True)


_INV_SQRT2 = 0.7071067811865476
_INV_SQRT2PI = 0.3989422804014327


def _gelu(x):
    return 0.5 * x * (1.0 + lax.erf(x * _INV_SQRT2))


def _gelu_grad(x):
    return 0.5 * (1.0 + lax.erf(x * _INV_SQRT2)) + x * jnp.exp(-0.5 * x * x) * _INV_SQRT2PI


def _silu_grad(x):
    s = jax.nn.sigmoid(x)
    return s * (1.0 + x * (1.0 - s))


def _ffn_fwd(x, g, wa, mi, name, payload=None, head=None, relay=None):
    t, d = x.shape
    f = wa.shape[1]
    tm, tf = 1024, 256
    nc = f // tf
    groups = [slice(k * (tm // 2), (k + 1) * (tm // 2)) for k in range(2)]

    def body(x_ref, g_ref, wgu_ref, wd_ref, *rest):
        if head is None:
            xo_ref, gate_ref, up_ref, act_ref, h_scr, acc_scr = rest
        else:
            fg_ref, tgt_ref, xo_ref, gate_ref, up_ref, act_ref, dfg_ref, loss_ref, h_scr, acc_scr = rest
        c = pl.program_id(1)
        if head is not None:
            @pl.when((c == 0) & (pl.program_id(0) == 0))
            def _():
                dfg_ref[...] = jnp.zeros_like(dfg_ref)
                loss_ref[...] = jnp.zeros_like(loss_ref)

        @pl.when(c == 0)
        def _():
            xhat, _ = _rms(x_ref[...])
            h_scr[...] = (xhat * g_ref[...]).astype(BF16)
            acc_scr[...] = jnp.zeros_like(acc_scr)

        wgu, wd = wgu_ref[...].reshape(2 * tf, d), wd_ref[...]
        for rows in groups:
            gu = _dot_nt(h_scr[rows, :], wgu)
            gate, up = gu[:, :tf], gu[:, tf:]
            gate_ref[rows, :] = gate.astype(BF16)
            up_ref[rows, :] = up.astype(BF16)
            act = (gate * jax.nn.sigmoid(gate) * up).astype(BF16)
            act_ref[rows, :] = act
            acc_scr[rows, :] += _dot(act, wd)

        @pl.when(c == nc - 1)
        def _():
            xo = x_ref[...] + 0.5 * acc_scr[...]
            if head is None:
                xo_ref[...] = xo
            else:
                fg = fg_ref[...]
                xhat, rstd = _rms(xo)
                err = xhat * fg - tgt_ref[...]
                dxn, dfg = _rms_bwd(xhat, rstd, fg, err * (1.0 / d))
                xo_ref[...] = dxn
                dfg_ref[...] += dfg
                loss_ref[...] += jnp.zeros_like(loss_ref) + 0.5 * jnp.sum(jnp.mean(err * err, axis=-1, keepdims=True))

    assert mi % 2 == 0
    row = pl.BlockSpec((tm, d), lambda i, c: (i, 0))
    vec = pl.BlockSpec((1, d), lambda i, c: (0, 0))
    in_specs = [row, vec, pl.BlockSpec((2, tf, d), lambda i, c: (mi // 2, c, 0)),
                pl.BlockSpec((None, tf, d), lambda i, c: (mi + 2, c, 0))]
    out_specs = [row] + [pl.BlockSpec((tm, tf), lambda i, c: (i, c))] * 3
    out_shape = [jax.ShapeDtypeStruct((t, d), F32)] + [jax.ShapeDtypeStruct((t, f), BF16)] * 3
    args = (x, g, wa, wa)
    if head is not None:
        in_specs += [vec, row]
        out_specs += [vec, pl.BlockSpec((1, LANES), lambda i, c: (0, 0))]
        out_shape += [jax.ShapeDtypeStruct((1, d), F32), jax.ShapeDtypeStruct((1, LANES), F32)]
        args += tuple(head)
    return _call(
        body, name, (t // tm, nc), in_specs, out_specs, out_shape,
        [pltpu.VMEM((tm, d), BF16), pltpu.VMEM((tm, d), F32)],
        ("parallel" if head is None else "arbitrary", "arbitrary"), args, payload, relay)


def _ffn_bwd(x, g, dxo, gate, up, wa, mi, name, payload=None):
    t, d = x.shape
    f = wa.shape[1]
    tm, tf = 1024, 256
    nc = f // tf
    groups = [slice(k * (tm // 2), (k + 1) * (tm // 2)) for k in range(2)]

    def body(x_ref, g_ref, dxo_ref, gate_ref, up_ref, wgu_ref, wd_ref,
             dx_ref, dgate_ref, dup_ref, h_ref, dy_ref, dg_ref, acc_scr):
        i, c = pl.program_id(0), pl.program_id(1)

        @pl.when(c == 0)
        def _():
            xhat, _ = _rms(x_ref[...])
            h_ref[...] = (xhat * g_ref[...]).astype(BF16)
            dy_ref[...] = (0.5 * dxo_ref[...]).astype(BF16)
            acc_scr[...] = jnp.zeros_like(acc_scr)

        @pl.when((c == 0) & (i == 0))
        def _():
            dg_ref[...] = jnp.zeros_like(dg_ref)

        wg, wu, wd = wgu_ref[0], wgu_ref[1], wd_ref[...]
        for rows in groups:
            gt = gate_ref[rows, :].astype(F32)
            u = up_ref[rows, :].astype(F32)
            s = jax.nn.sigmoid(gt)
            silu = gt * s
            dact = _dot_nt(dy_ref[rows, :], wd)
            dgate = (dact * u * (s * (1.0 + gt * (1.0 - s)))).astype(BF16)
            dup = (dact * silu).astype(BF16)
            dgate_ref[rows, :] = dgate
            dup_ref[rows, :] = dup
            acc_scr[rows, :] += _dot(dgate, wg) + _dot(dup, wu)

        @pl.when(c == nc - 1)
        def _():
            xhat, rstd = _rms(x_ref[...])
            dxn, dg = _rms_bwd(xhat, rstd, g_ref[...], acc_scr[...])
            dx_ref[...] = dxo_ref[...] + dxn
            dg_ref[...] += dg

    assert mi % 2 == 0
    row = pl.BlockSpec((tm, d), lambda i, c: (i, 0))
    col = pl.BlockSpec((tm, tf), lambda i, c: (i, c))
    vec = pl.BlockSpec((1, d), lambda i, c: (0, 0))
    return _call(
        body, name, (t // tm, nc),
        [row, vec, row, col, col, pl.BlockSpec((2, tf, d), lambda i, c: (mi // 2, c, 0)),
         pl.BlockSpec((None, tf, d), lambda i, c: (mi + 2, c, 0))],
        [row, col, col, row, row, vec],
        [jax.ShapeDtypeStruct((t, d), F32), jax.ShapeDtypeStruct((t, f), BF16),
         jax.ShapeDtypeStruct((t, f), BF16),
         jax.ShapeDtypeStruct((t, d), BF16), jax.ShapeDtypeStruct((t, d), BF16),
         jax.ShapeDtypeStruct((1, d), F32)],
        [pltpu.VMEM((tm, d), F32)],
        ("arbitrary", "arbitrary"), (x, g, dxo, gate, up, wa, wa), payload)


def _ffn_dw(dgate, dup, act, h, dy, name, payload=None):
    t, f = dgate.shape
    d = h.shape[1]
    tk = 512
    tmm = f // 2
    nk = t // tk

    def body(dg_ref, du_ref, a_ref, h_ref, dy_ref, og_ref, ou_ref, od_ref, acc_g, acc_u, acc_d):
        k = pl.program_id(1)

        @pl.when(k == 0)
        def _():
            acc_g[...] = jnp.zeros_like(acc_g)
            acc_u[...] = jnp.zeros_like(acc_u)
            acc_d[...] = jnp.zeros_like(acc_d)

        hv = h_ref[...]
        acc_g[...] += _dot_tn(dg_ref[...], hv)
        acc_u[...] += _dot_tn(du_ref[...], hv)
        acc_d[...] += _dot_tn(a_ref[...], dy_ref[...])

        @pl.when(k == nk - 1)
        def _():
            og_ref[...] = acc_g[...].astype(BF16)
            ou_ref[...] = acc_u[...].astype(BF16)
            od_ref[...] = acc_d[...].astype(BF16)

    col = pl.BlockSpec((tk, tmm), lambda j, k: (k, j))
    row = pl.BlockSpec((tk, d), lambda j, k: (k, 0))
    out = pl.BlockSpec((tmm, d), lambda j, k: (j, 0))
    return _call(
        body, name, (f // tmm, nk), [col, col, col, row, row], [out, out, out],
        [jax.ShapeDtypeStruct((f, d), BF16)] * 3, [pltpu.VMEM((tmm, d), F32)] * 3,
        ("parallel", "arbitrary"), (dgate, dup, act, h, dy), payload)


def _tn_matmul(a, b, name, payload=None):
    t, m = a.shape
    n = b.shape[1]
    tk = 1024
    tmm = m // 2 if (m > 2048 and (m // 2) % LANES == 0) else m
    nk = t // tk

    def body(a_ref, b_ref, o_ref, acc_scr):
        k = pl.program_id(1)

        @pl.when(k == 0)
        def _():
            acc_scr[...] = jnp.zeros_like(acc_scr)

        acc_scr[...] += _dot_tn(a_ref[...].astype(BF16), b_ref[...].astype(BF16))

        @pl.when(k == nk - 1)
        def _():
            o_ref[...] = acc_scr[...].astype(BF16)

    (out,), p_outs = _call(
        body, name, (m // tmm, nk),
        [pl.BlockSpec((tk, tmm), lambda j, k: (k, j)), pl.BlockSpec((tk, n), lambda j, k: (k, 0))],
        [pl.BlockSpec((tmm, n), lambda j, k: (j, 0))],
        [jax.ShapeDtypeStruct((m, n), BF16)],
        [pltpu.VMEM((tmm, n), F32)],
        ("parallel", "arbitrary"), (a, b), payload)
    return out, p_outs


def _lane_ids(shape):
    return lax.broadcasted_iota(jnp.int32, shape, 1)


def _tril(w):
    r = lax.broadcasted_iota(jnp.int32, w.shape, 0)
    c = lax.broadcasted_iota(jnp.int32, w.shape, 1)
    return jnp.where(r >= c, w, 0.0)


def _shift_down(x, k):
    return x if k == 0 else pltpu.roll(x, k, 0)


def _shift_up(x, k):
    return x if k == 0 else pltpu.roll(x, x.shape[0] - k, 0)


def _sub_tile_shifts(ext, shift):
    return [shift(ext, b) for b in range(8)]


def _tap(shifted, j, n_out, down):
    a, b = divmod(j, 8)
    r0 = HALO - 8 * a if down else 8 * a
    return shifted[b][r0:r0 + n_out]


def _depthwise(shifted, w, n_out, down):
    acc = None
    for j in range(CONV_WIDTH):
        term = _tap(shifted, j, n_out, down) * w[CONV_WIDTH - 1 - j:CONV_WIDTH - j]
        acc = term if acc is None else acc + term
    return acc


def _conv_wgrad(shifted, dhc, n_out):
    return [jnp.sum(_tap(shifted, CONV_WIDTH - 1 - k, n_out, True) * dhc, axis=0, keepdims=True)
            for k in range(CONV_WIDTH)]


def _window_sums(ext, shift):
    s2 = ext + shift(ext, 1)
    s4 = s2 + shift(s2, 2)
    s8 = s4 + shift(s4, 4)
    s16 = s8 + shift(s8, 8)
    grp = _lane_ids(ext.shape) // HEAD_DIM
    return jnp.where(grp == 0, s2, jnp.where(grp == 1, s4, jnp.where(grp == 2, s8, s16)))


def _pool_count(t0, n, width):
    pos = (lax.broadcasted_iota(jnp.int32, (n, width), 0) + (t0 + 1)).astype(F32)
    grp = _lane_ids((n, width)) // HEAD_DIM
    win = jnp.where(grp == 0, 2.0, jnp.where(grp == 1, 4.0, jnp.where(grp == 2, 8.0, 16.0)))
    return jnp.minimum(pos, win)


def _block_diag(pw):
    gn, cg, _ = pw.shape
    rows = []
    for gi in range(gn):
        parts = [pw[gi] if gj == gi else jnp.zeros((cg, cg), pw.dtype) for gj in range(gn)]
        rows.append(jnp.concatenate(parts, axis=1))
    return jnp.concatenate(rows, axis=0)


def _head_pair_mix(w_even, w_odd, v):
    lo = _lane_ids((CHUNK, LANES)) < HEAD_DIM
    return jnp.where(lo, _dot(w_even, v), _dot(w_odd, v))


def _mixer_fwd(x, g, wb, wc, p, name, payload=None):
    t, d = x.shape
    d_in = wb.shape[1]
    sgu = p["sgu_ln_g"].shape[1]
    pool = p["pool_scale"].shape[1]
    d_mix = 2 * sgu + pool
    tm = 512
    n_i = t // tm
    hb = tm // HALO

    def body(x_ref, xp_ref, g_ref, wi_ref, wo_ref, lng_ref, lnb_ref, ws_ref, bs_ref, cw_ref, cb_ref, clg_ref,
             clb_ref, bd_ref, ps_ref, xo_ref, z_ref, hc_ref, cat_ref):
        i = pl.program_id(0)
        first = i == 0
        gain, wi = g_ref[...], wi_ref[...]

        def project(xv):
            xhat, _ = _rms(xv)
            return _dot_nt((xhat * gain).astype(BF16), wi)

        z_main = project(x_ref[...])
        z_ref[...] = z_main
        z_prev = jnp.where(first, 0.0, project(xp_ref[...]))

        lng, lnb = lng_ref[...], lnb_ref[...]
        wt = [_tril(ws_ref[h]).astype(BF16) for h in range(sgu // HEAD_DIM)]
        for n in range(tm // CHUNK):
            rows = slice(n * CHUNK, (n + 1) * CHUNK)
            u = _gelu(z_main[rows, 0:sgu])
            vhat, _ = _ln(_gelu(z_main[rows, sgu:2 * sgu]))
            vn = (vhat * lng + lnb).astype(BF16)
            for gp in range(sgu // LANES):
                ls = slice(gp * LANES, (gp + 1) * LANES)
                mixed = _head_pair_mix(wt[2 * gp], wt[2 * gp + 1], vn[:, ls]) + bs_ref[:, ls]
                cat_ref[rows, ls] = (u[:, ls] * mixed).astype(BF16)

        def glu(zz):
            return zz[:, 2 * sgu:3 * sgu] * jax.nn.sigmoid(zz[:, 3 * sgu:4 * sgu])

        ext = jnp.concatenate([glu(z_prev), glu(z_main)], axis=0)
        hc = _depthwise(_sub_tile_shifts(ext, _shift_down), cw_ref[...], tm, True) + cb_ref[...]
        hc_ref[...] = hc
        hhat, _ = _ln(hc)
        bn = hhat * clg_ref[...] + clb_ref[...]
        cat_ref[:, sgu:2 * sgu] = (bn * jax.nn.sigmoid(bn)).astype(BF16)

        pext = jnp.concatenate([z_prev[:, 4 * sgu:], z_main[:, 4 * sgu:]], axis=0)
        sums = _window_sums(pext, _shift_down)[HALO:]
        pooled = sums / _pool_count(i * tm, tm, pool) - z_main[:, 4 * sgu:]
        mixed_c = _dot(pooled.astype(BF16), bd_ref[...].astype(BF16))
        cat_ref[:, 2 * sgu:] = (mixed_c * ps_ref[...]).astype(BF16)

        xo_ref[...] = x_ref[...] + _dot(cat_ref[...], wo_ref[...])

    def vec(n):
        return pl.BlockSpec((1, n), lambda i: (0, 0))

    return _call(
        body, name, (n_i,),
        [pl.BlockSpec((tm, d), lambda i: (i, 0)),
         pl.BlockSpec((HALO, d), lambda i: (jnp.maximum(i * hb - 1, 0), 0)),
         vec(d),
         pl.BlockSpec((None, d_in, d), lambda i: (0, 0, 0)), pl.BlockSpec((None, d_mix, d), lambda i: (0, 0, 0)),
         vec(sgu), vec(sgu),
         pl.BlockSpec(p["w_spatial"].shape, lambda i: (0, 0, 0)),
         pl.BlockSpec((CHUNK, sgu), lambda i: (0, 0)),
         pl.BlockSpec((CONV_WIDTH, sgu), lambda i: (0, 0)),
         vec(sgu), vec(sgu), vec(sgu),
         pl.BlockSpec((pool, pool), lambda i: (0, 0)), vec(pool)],
        [pl.BlockSpec((tm, d), lambda i: (i, 0)), pl.BlockSpec((tm, d_in), lambda i: (i, 0)),
         pl.BlockSpec((tm, sgu), lambda i: (i, 0)), pl.BlockSpec((tm, d_mix), lambda i: (i, 0))],
        [jax.ShapeDtypeStruct((t, d), F32), jax.ShapeDtypeStruct((t, d_in), F32),
         jax.ShapeDtypeStruct((t, sgu), F32), jax.ShapeDtypeStruct((t, d_mix), BF16)], [], ("parallel",),
        (x, x, g, wb, wc, p["sgu_ln_g"], p["sgu_ln_b"], p["w_spatial"], p["bs_full"], p["conv_w"], p["conv_b"],
         p["conv_ln_g"], p["conv_ln_b"], p["bd"], p["pool_scale"]), payload)


_R_SGU_G, _R_SGU_B, _R_CONV_B, _R_CLN_G, _R_CLN_B, _R_CONV_W = 0, 1, 2, 3, 4, 8
_R384_ROWS = 40


def _mixer_bwd(x, g, z, hc_saved, dxo, wb, wc, p, name, payload=None):
    t, d_in = z.shape
    d = x.shape[1]
    sgu = p["sgu_ln_g"].shape[1]
    pool = p["pool_scale"].shape[1]
    d_mix = 2 * sgu + pool
    n_head = sgu // HEAD_DIM
    tm = 512
    n_i = t // tm
    hb = tm // HALO

    def body(x_ref, g_ref, wi_ref, z_ref, zp_ref, zn_ref, dxo_ref, dxon_ref, wo_ref, hc_ref, hcn_ref, lng_ref,
             lnb_ref, ws_ref, bs_ref, cw_ref, clg_ref, clb_ref, bd_ref, ps_ref,
             dx_ref, dz_ref, hm_ref, dgm_ref, g384_ref, gws_ref, gpool_ref, dbs_scr):
        i = pl.program_id(0)
        first, last = i == 0, i == n_i - 1

        @pl.when(first)
        def _():
            dgm_ref[...] = jnp.zeros_like(dgm_ref)
            g384_ref[...] = jnp.zeros_like(g384_ref)
            gws_ref[...] = jnp.zeros_like(gws_ref)
            gpool_ref[...] = jnp.zeros_like(gpool_ref)
            dbs_scr[...] = jnp.zeros_like(dbs_scr)

        z_main = z_ref[...]
        z_prev = jnp.where(first, 0.0, zp_ref[...])
        z_next = jnp.where(last, 0.0, zn_ref[...])
        wo = wo_ref[...]
        dc_main = _dot_nt(dxo_ref[...].astype(BF16), wo)
        dc_next = jnp.where(last, 0.0, _dot_nt(dxon_ref[...].astype(BF16), wo))

        lng, lnb = lng_ref[...], lnb_ref[...]
        wt = [_tril(ws_ref[h]) for h in range(n_head)]
        wt_b = [w.astype(BF16) for w in wt]
        wtt_b = [w.T.astype(BF16) for w in wt]
        lo = _lane_ids((CHUNK, LANES)) < HEAD_DIM
        d_lng = jnp.zeros((1, sgu), F32)
        d_lnb = jnp.zeros((1, sgu), F32)
        dws = [jnp.zeros((CHUNK, CHUNK), F32) for _ in range(n_head)]
        for n in range(tm // CHUNK):
            rows = slice(n * CHUNK, (n + 1) * CHUNK)
            au, av = z_main[rows, 0:sgu], z_main[rows, sgu:2 * sgu]
            u = _gelu(au)
            vhat, vrstd = _ln(_gelu(av))
            vn = (vhat * lng + lnb).astype(BF16)
            da = dc_main[rows, 0:sgu]
            dmixed = da * u
            dbs_scr[...] += dmixed
            dvn_parts, du_parts = [], []
            for gp in range(sgu // LANES):
                ls = slice(gp * LANES, (gp + 1) * LANES)
                vn_g = vn[:, ls]
                mixed = _head_pair_mix(wt_b[2 * gp], wt_b[2 * gp + 1], vn_g) + bs_ref[:, ls]
                du_parts.append(da[:, ls] * mixed)
                dm_g = dmixed[:, ls]
                dm_b = dm_g.astype(BF16)
                dvn_parts.append(jnp.where(lo, _dot(wtt_b[2 * gp], dm_b), _dot(wtt_b[2 * gp + 1], dm_b)))
                dws[2 * gp] = dws[2 * gp] + _dot_nt(jnp.where(lo, dm_g, 0.0).astype(BF16), vn_g)
                dws[2 * gp + 1] = dws[2 * gp + 1] + _dot_nt(jnp.where(lo, 0.0, dm_g).astype(BF16), vn_g)
            dvn = jnp.concatenate(dvn_parts, axis=1)
            du = jnp.concatenate(du_parts, axis=1)
            dv, dg_n, db_n = _ln_bwd(vhat, vrstd, lng, dvn)
            d_lng = d_lng + dg_n
            d_lnb = d_lnb + db_n
            dz_ref[rows, 0:sgu] = (du * _gelu_grad(au)).astype(BF16)
            dz_ref[rows, sgu:2 * sgu] = (dv * _gelu_grad(av)).astype(BF16)
        for h in range(n_head):
            gws_ref[h] += _tril(dws[h])
        g384_ref[_R_SGU_G:_R_SGU_G + 1, :] += d_lng
        g384_ref[_R_SGU_B:_R_SGU_B + 1, :] += d_lnb

        clg = clg_ref[...]
        bcols = slice(2 * sgu, 4 * sgu)
        zb = jnp.concatenate([z_prev[:, bcols], z_main[:, bcols], z_next[:, bcols]], axis=0)
        bval, bgate = zb[:, 0:sgu], zb[:, sgu:2 * sgu]
        sg = jax.nn.sigmoid(bgate)
        hglu = bval * sg
        n_out = tm + HALO
        hglu_shifts = _sub_tile_shifts(hglu, _shift_down)
        cw = cw_ref[...]
        hc = jnp.concatenate([hc_ref[...], jnp.where(last, 0.0, hcn_ref[...])], axis=0)
        hhat, hrstd = _ln(hc)
        bn = hhat * clg + clb_ref[...]
        db = jnp.concatenate([dc_main[:, sgu:2 * sgu], dc_next[:, sgu:2 * sgu]], axis=0)
        dbn = db * _silu_grad(bn)
        dhc_all, _, _ = _ln_bwd(hhat, hrstd, clg, dbn)
        dbn_m, hhat_m, dhc = dbn[:tm], hhat[:tm], dhc_all[:tm]
        g384_ref[_R_CLN_G:_R_CLN_G + 1, :] += jnp.sum(dbn_m * hhat_m, axis=0, keepdims=True)
        g384_ref[_R_CLN_B:_R_CLN_B + 1, :] += jnp.sum(dbn_m, axis=0, keepdims=True)
        g384_ref[_R_CONV_B:_R_CONV_B + 1, :] += jnp.sum(dhc, axis=0, keepdims=True)
        wrows = _conv_wgrad(hglu_shifts, dhc, tm)
        for k in range(CONV_WIDTH):
            g384_ref[_R_CONV_W + k:_R_CONV_W + k + 1, :] += wrows[k]
        dhglu = _depthwise(_sub_tile_shifts(dhc_all, _shift_up), cw, tm, False)
        bval_m, sg_m = bval[HALO:HALO + tm], sg[HALO:HALO + tm]
        dz_ref[:, 2 * sgu:3 * sgu] = (dhglu * sg_m).astype(BF16)
        dz_ref[:, 3 * sgu:4 * sgu] = (dhglu * bval_m * sg_m * (1.0 - sg_m)).astype(BF16)

        bd_b = bd_ref[...].astype(BF16)
        ps = ps_ref[...]
        p_main = z_main[:, 4 * sgu:]
        pext = jnp.concatenate([z_prev[:, 4 * sgu:], p_main], axis=0)
        cnt = _pool_count(i * tm, n_out, pool)
        pooled = _window_sums(pext, _shift_down)[HALO:] / cnt[:tm] - p_main
        pooled_b = pooled.astype(BF16)
        dcc = jnp.concatenate([dc_main[:, 2 * sgu:], dc_next[:, 2 * sgu:]], axis=0)
        dmix_c = dcc * ps
        mixed_c = _dot(pooled_b, bd_b)
        grp_r = lax.broadcasted_iota(jnp.int32, (pool, pool), 0) // HEAD_DIM
        grp_c = lax.broadcasted_iota(jnp.int32, (pool, pool), 1) // HEAD_DIM
        gpool_ref[0:pool, :] += jnp.where(grp_r == grp_c, _dot_tn(pooled_b, dmix_c[:tm].astype(BF16)), 0.0)
        gpool_ref[pool:pool + 1, :] += jnp.sum(dcc[:tm] * mixed_c, axis=0, keepdims=True)
        dpooled = _dot_nt(dmix_c.astype(BF16), bd_b)
        q = dpooled / cnt
        dp = _window_sums(q, _shift_up)[:tm] - dpooled[:tm]
        dz_ref[:, 4 * sgu:] = dp.astype(BF16)

        gain = g_ref[...]
        xhat, rstd = _rms(x_ref[...])
        hm_ref[...] = (xhat * gain).astype(BF16)
        dxn, dgm = _rms_bwd(xhat, rstd, gain, _dot(dz_ref[...], wi_ref[...]))
        dx_ref[...] = dxo_ref[...] + dxn
        dgm_ref[...] += dgm

        @pl.when(last)
        def _():
            r = lax.broadcasted_iota(jnp.int32, (sgu, LANES), 0)
            c = lax.broadcasted_iota(jnp.int32, (sgu, LANES), 1)
            sel = (r // HEAD_DIM == c).astype(BF16)
            gws_ref[n_head] = _split_dot(dbs_scr[...], sel)

    def vec(n):
        return pl.BlockSpec((1, n), lambda i: (0, 0))

    def prev_map(i):
        return (jnp.maximum(i * hb - 1, 0), 0)

    def next_map(i):
        return (jnp.minimum((i + 1) * hb, n_i * hb - 1), 0)

    return _call(
        body, name, (n_i,),
        [pl.BlockSpec((tm, d), lambda i: (i, 0)), pl.BlockSpec((1, d), lambda i: (0, 0)),
         pl.BlockSpec((None, d_in, d), lambda i: (0, 0, 0)),
         pl.BlockSpec((tm, d_in), lambda i: (i, 0)),
         pl.BlockSpec((HALO, d_in), prev_map), pl.BlockSpec((HALO, d_in), next_map),
         pl.BlockSpec((tm, d), lambda i: (i, 0)), pl.BlockSpec((HALO, d), next_map),
         pl.BlockSpec((None, d_mix, d), lambda i: (0, 0, 0)),
         pl.BlockSpec((tm, sgu), lambda i: (i, 0)), pl.BlockSpec((HALO, sgu), next_map),
         vec(sgu), vec(sgu),
         pl.BlockSpec(p["w_spatial"].shape, lambda i: (0, 0, 0)),
         pl.BlockSpec((CHUNK, sgu), lambda i: (0, 0)),
         pl.BlockSpec((CONV_WIDTH, sgu), lambda i: (0, 0)),
         vec(sgu), vec(sgu),
         pl.BlockSpec((pool, pool), lambda i: (0, 0)), vec(pool)],
        [pl.BlockSpec((tm, d), lambda i: (i, 0)), pl.BlockSpec((tm, d_in), lambda i: (i, 0)),
         pl.BlockSpec((tm, d), lambda i: (i, 0)), pl.BlockSpec((1, d), lambda i: (0, 0)),
         pl.BlockSpec((_R384_ROWS, sgu), lambda i: (0, 0)),
         pl.BlockSpec((n_head + 1, CHUNK, CHUNK), lambda i: (0, 0, 0)),
         pl.BlockSpec((pool + 8, pool), lambda i: (0, 0))],
        [jax.ShapeDtypeStruct((t, d), F32), jax.ShapeDtypeStruct((t, d_in), BF16),
         jax.ShapeDtypeStruct((t, d), BF16), jax.ShapeDtypeStruct((1, d), F32),
         jax.ShapeDtypeStruct((_R384_ROWS, sgu), F32),
         jax.ShapeDtypeStruct((n_head + 1, CHUNK, CHUNK), F32),
         jax.ShapeDtypeStruct((pool + 8, pool), F32)],
        [pltpu.VMEM((CHUNK, sgu), F32)], ("arbitrary",),
        (x, g, wb, z, z, z, dxo, dxo, wc, hc_saved, hc_saved, p["sgu_ln_g"], p["sgu_ln_b"], p["w_spatial"],
         p["bs_full"], p["conv_w"], p["conv_ln_g"], p["conv_ln_b"], p["bd"], p["pool_scale"]), payload)


def _all_gather(arrs, name, extra=None, to_sum=()):
    gather = _GatherIci(arrs)
    n = len(arrs)
    forward = _GatherForward([jax.ShapeDtypeStruct(s.shape, s.dtype) for s in gather.out_shapes])
    x_in = len(extra.ins) if extra else 0
    x_out = len(extra.out_shapes) if extra else 0
    n_sum = len(to_sum)

    def body(*refs):
        ins, x_ins, s_ins = refs[:n], refs[n:n + x_in], refs[n + x_in:n + x_in + n_sum]
        o0 = n + x_in + n_sum
        outs, x_outs = refs[o0:o0 + n], refs[o0 + n:o0 + n + x_out]
        s_outs = refs[o0 + n + x_out:o0 + n + x_out + n_sum]
        sems = refs[o0 + n + x_out + n_sum:]
        first = gather.build(ins, outs, *sems[0:3])
        first.start()
        if extra:
            beside = extra.build(x_ins, x_outs, *sems[6:9])
            beside.start()
        for s_ref, o_ref in zip(s_ins, s_outs):
            r = o_ref.shape[0]
            acc = s_ref[0:r, :]
            for q in range(1, N_DEV):
                acc = acc + s_ref[q * r:(q + 1) * r, :]
            o_ref[...] = acc
        first.wait()
        second = forward.build(outs, outs, *sems[3:6])
        second.start()
        second.wait()
        if extra:
            beside.wait()

    in_vmem = pl.BlockSpec(memory_space=pltpu.VMEM)
    outs = pl.pallas_call(
        body, name=name,
        in_specs=[ANY] * (n + x_in) + [in_vmem] * n_sum, out_specs=[ANY] * (n + x_out) + [in_vmem] * n_sum,
        out_shape=list(gather.out_shapes) + (list(extra.out_shapes) if extra else [])
        + [jax.ShapeDtypeStruct((s.shape[0] // N_DEV, s.shape[1]), s.dtype) for s in to_sum],
        scratch_shapes=gather.sem_shapes() + forward.sem_shapes() + (extra.sem_shapes() if extra else []),
        compiler_params=_cparams(),
    )(*arrs, *(extra.ins if extra else []), *to_sum)
    return list(outs[:n]), list(outs[n:n + x_out]), list(outs[n + x_out:])


def _all_gather_relayed(arrs, name):
    n = len(arrs)
    n_pairs = 8
    units = [(a, l) for a in range(n) for l in range(arrs[a].shape[0])]

    def body(*refs):
        ins, outs = refs[:n], refs[n:2 * n]
        send_sems, recv_sems, local_sems = refs[2 * n:]
        x, y, c = _position()
        sib, xn, yn = (x, y, 1 - c), (1 - x, y, c), (x, 1 - y, c)
        diag = (1 - x, 1 - y, c)

        def rows(u, px, py, pc, half=None):
            a, l = units[u]
            r = ins[a].shape[1]
            base = (4 * px + 2 * py + pc) * r
            if half is None:
                return outs[a].at[pl.ds(l, 1), pl.ds(base, r), :]
            return outs[a].at[pl.ds(l, 1), pl.ds(base + half * (r // 2), r // 2), :]

        def send(u, k, src, dst, to):
            return _remote(src, dst, send_sems, recv_sems, u * n_pairs + k, to)

        def arrived(u, k, land, sender):
            _remote(land, land, send_sems, recv_sems, u * n_pairs + k, sender).wait_recv()

        own, sent = [], []
        for u, (a, l) in enumerate(units):
            src = ins[a].at[pl.ds(l, 1)]
            own.append(pltpu.make_async_copy(src, rows(u, x, y, c), local_sems.at[u]))
            sent += [send(u, k, src, rows(u, x, y, c), to) for k, to in enumerate((sib, xn, yn))]
        for cp in own + sent:
            cp.start()
        for u in range(len(units)):
            arrived(u, 1, rows(u, *xn), xn)
            arrived(u, 2, rows(u, *yn), yn)
            relay = [send(u, 3, rows(u, *xn, half=0), rows(u, *xn, half=0), yn),
                     send(u, 4, rows(u, *yn, half=1), rows(u, *yn, half=1), xn),
                     send(u, 5, rows(u, *xn), rows(u, *xn), sib),
                     send(u, 6, rows(u, *yn), rows(u, *yn), sib)]
            for cp in relay:
                cp.start()
            sent += relay
        for u in range(len(units)):
            arrived(u, 3, rows(u, *diag, half=0), yn)
            arrived(u, 4, rows(u, *diag, half=1), xn)
            last = send(u, 7, rows(u, *diag), rows(u, *diag), sib)
            last.start()
            sent.append(last)
        for u in range(len(units)):
            arrived(u, 0, rows(u, *sib), sib)
            arrived(u, 5, rows(u, 1 - x, y, 1 - c), sib)
            arrived(u, 6, rows(u, x, 1 - y, 1 - c), sib)
            arrived(u, 7, rows(u, 1 - x, 1 - y, 1 - c), sib)
        for cp in sent:
            cp.wait_send()
        for cp in own:
            cp.wait()

    n_units = len(units)
    return list(pl.pallas_call(
        body, name=name, in_specs=[ANY] * n, out_specs=[ANY] * n,
        out_shape=[jax.ShapeDtypeStruct((a.shape[0], N_DEV * a.shape[1], a.shape[2]), a.dtype) for a in arrs],
        scratch_shapes=[pltpu.SemaphoreType.DMA((n_pairs * n_units,)), pltpu.SemaphoreType.DMA((n_pairs * n_units,)),
                        pltpu.SemaphoreType.DMA((n_units,))],
    )(*arrs))


def _pair_sums(grads, recvs, cidx, name):
    n = len(grads)

    def body(c_ref, *refs):
        for g_ref, r_ref, o_ref in zip(refs[:n], refs[n:2 * n], refs[2 * n:]):
            o_ref[...] = (g_ref[...].astype(F32) + r_ref[...].astype(F32)).astype(BF16)

    shapes = [(g.shape[0] // N_DEV, g.shape[1]) for g in grads]
    return list(pl.pallas_call(
        body, name=name,
        grid_spec=pltpu.PrefetchScalarGridSpec(
            num_scalar_prefetch=1, grid=(N_CHIP,),
            in_specs=[pl.BlockSpec(s, lambda q, c: (2 * q + c[0], 0)) for s in shapes]
            + [pl.BlockSpec(s, lambda q, c: (q, 0)) for s in shapes],
            out_specs=[pl.BlockSpec(s, lambda q, c: (q, 0)) for s in shapes]),
        out_shape=[jax.ShapeDtypeStruct((N_CHIP * r, cols), BF16) for r, cols in shapes],
        compiler_params=_cparams(("parallel",)),
    )(cidx, *grads, *recvs))


def _sum_blocks(parts, nblk, name):
    r = parts.shape[0] // nblk
    cols = parts.shape[1]

    def body(p_ref, o_ref):
        acc = p_ref[0:r, :].astype(F32)
        for q in range(1, nblk):
            acc = acc + p_ref[q * r:(q + 1) * r, :].astype(F32)
        o_ref[...] = acc

    return pl.pallas_call(
        body, name=name,
        out_shape=jax.ShapeDtypeStruct((r, cols), F32),
        compiler_params=_cparams(),
    )(parts)


def _adamw_math(w, g, m, v):
    m = ADAM_B1 * m + (1.0 - ADAM_B1) * g
    v = ADAM_B2 * v + (1.0 - ADAM_B2) * (g * g)
    m_hat = m / (1.0 - ADAM_B1 ** ADAM_STEP)
    v_hat = v / (1.0 - ADAM_B2 ** ADAM_STEP)
    delta = -ADAM_LR * (m_hat / (jnp.sqrt(v_hat) + ADAM_EPS) + ADAM_WD * w)
    return delta, m, v


def _finish_sharded(parts, w, m, v, name):
    depth, rr, cw = w.shape

    def body(*refs):
        p_refs = refs[:depth]
        w_ref, m_ref, v_ref, g_ref, d_ref, mo_ref, vo_ref = refs[depth:]
        l = pl.program_id(0)
        for k in range(depth):
            @pl.when(l == k)
            def _(p_ref=p_refs[k]):
                r = p_ref.shape[0] // N_CHIP
                acc = p_ref[0:r, :].astype(F32)
                for q in range(1, N_CHIP):
                    acc = acc + p_ref[q * r:(q + 1) * r, :].astype(F32)
                g_ref[...] = acc
                d_ref[...], mo_ref[...], vo_ref[...] = _adamw_math(w_ref[...], acc, m_ref[...], v_ref[...])

    blk = pl.BlockSpec((None, rr, cw), lambda l: (l, 0, 0))
    return pl.pallas_call(
        body, name=name, grid=(depth,),
        in_specs=[pl.BlockSpec(p.shape, lambda l: (0, 0)) for p in parts] + [blk] * 3, out_specs=[blk] * 4,
        out_shape=[jax.ShapeDtypeStruct(w.shape, F32)] * 4,
        compiler_params=_cparams(("arbitrary",)),
    )(*parts, w, m, v)


def _adamw_small(ws, gs, ms, vs, name):
    n = len(ws)

    def body(*refs):
        for k in range(n):
            w_ref, g_ref, m_ref, v_ref = (refs[j * n + k] for j in range(4))
            d_ref, mo_ref, vo_ref = (refs[(4 + j) * n + k] for j in range(3))
            d_ref[...], mo_ref[...], vo_ref[...] = _adamw_math(w_ref[...], g_ref[...], m_ref[...], v_ref[...])

    shapes = [jax.ShapeDtypeStruct(w.shape, F32) for w in ws]
    return pl.pallas_call(
        body, name=name, out_shape=shapes * 3, compiler_params=_cparams(),
    )(*ws, *gs, *ms, *vs)


def kernel(x, ffn1_norm, ffn1_w_gate, ffn1_w_up, ffn1_w_down, mix_norm, w_in, sgu_ln_g, sgu_ln_b, w_spatial, b_spatial, conv_w, conv_b, conv_ln_g, conv_ln_b, pool_w, pool_scale, w_out, ffn2_norm, ffn2_w_gate, ffn2_w_up, ffn2_w_down, final_norm, loss_target, m_ffn1_norm, m_ffn1_w_gate, m_ffn1_w_up, m_ffn1_w_down, m_mix_norm, m_w_in, m_sgu_ln_g, m_sgu_ln_b, m_w_spatial, m_b_spatial, m_conv_w, m_conv_b, m_conv_ln_g, m_conv_ln_b, m_pool_w, m_pool_scale, m_w_out, m_ffn2_norm, m_ffn2_w_gate, m_ffn2_w_up, m_ffn2_w_down, m_final_norm, v_ffn1_norm, v_ffn1_w_gate, v_ffn1_w_up, v_ffn1_w_down, v_mix_norm, v_w_in, v_sgu_ln_g, v_sgu_ln_b, v_w_spatial, v_b_spatial, v_conv_w, v_conv_b, v_conv_ln_g, v_conv_ln_b, v_pool_w, v_pool_scale, v_w_out, v_ffn2_norm, v_ffn2_w_gate, v_ffn2_w_up, v_ffn2_w_down, v_final_norm):
    names = ["ffn1_norm", "ffn1_w_gate", "ffn1_w_up", "ffn1_w_down", "mix_norm", "w_in", "sgu_ln_g", "sgu_ln_b",
             "w_spatial", "b_spatial", "conv_w", "conv_b", "conv_ln_g", "conv_ln_b", "pool_w", "pool_scale",
             "w_out", "ffn2_norm", "ffn2_w_gate", "ffn2_w_up", "ffn2_w_down", "final_norm"]
    W = dict(zip(names, [ffn1_norm, ffn1_w_gate, ffn1_w_up, ffn1_w_down, mix_norm, w_in, sgu_ln_g, sgu_ln_b,
                         w_spatial, b_spatial, conv_w, conv_b, conv_ln_g, conv_ln_b, pool_w, pool_scale, w_out,
                         ffn2_norm, ffn2_w_gate, ffn2_w_up, ffn2_w_down, final_norm]))
    M = dict(zip(names, [m_ffn1_norm, m_ffn1_w_gate, m_ffn1_w_up, m_ffn1_w_down, m_mix_norm, m_w_in, m_sgu_ln_g,
                         m_sgu_ln_b, m_w_spatial, m_b_spatial, m_conv_w, m_conv_b, m_conv_ln_g, m_conv_ln_b,
                         m_pool_w, m_pool_scale, m_w_out, m_ffn2_norm, m_ffn2_w_gate, m_ffn2_w_up, m_ffn2_w_down,
                         m_final_norm]))
    V = dict(zip(names, [v_ffn1_norm, v_ffn1_w_gate, v_ffn1_w_up, v_ffn1_w_down, v_mix_norm, v_w_in, v_sgu_ln_g,
                         v_sgu_ln_b, v_w_spatial, v_b_spatial, v_conv_w, v_conv_b, v_conv_ln_g, v_conv_ln_b,
                         v_pool_w, v_pool_scale, v_w_out, v_ffn2_norm, v_ffn2_w_gate, v_ffn2_w_up, v_ffn2_w_down,
                         v_final_norm]))

    depth, d = ffn1_norm.shape
    t = x.shape[1]
    sgu = sgu_ln_g.shape[1]
    pool = pool_scale.shape[1]
    n_head = sgu // HEAD_DIM
    cw_shard = conv_w.shape[2]
    xs = x.reshape(t, d)
    target = loss_target.reshape(t, d)

    def tr(w):
        return jnp.swapaxes(w, 1, 2).astype(BF16)

    ffn_shards = [[jnp.stack([tr(ffn1_w_gate)[l], tr(ffn1_w_up)[l], ffn1_w_down[l].astype(BF16)]),
                   jnp.stack([tr(ffn2_w_gate)[l], tr(ffn2_w_up)[l], ffn2_w_down[l].astype(BF16)])]
                  for l in range(depth)]
    win_shards = [tr(w_in)[l:l + 1] for l in range(depth)]
    wout_shards = [w_out[l:l + 1].astype(BF16) for l in range(depth)]
    cw_rows = depth * CONV_WIDTH
    cw_pad = -cw_rows % 8
    cw_send = jnp.pad(conv_w.reshape(cw_rows, cw_shard), ((0, cw_pad), (0, 0)))[None]
    wffn, wb, wc = {}, {}, {}
    wffn[(0, 0)], wb[0], wc[0], cwg = _all_gather_relayed(
        [ffn_shards[0][0], win_shards[0], wout_shards[0], cw_send], "ag_first")
    conv_w_full = cwg.reshape(N_DEV, cw_rows + cw_pad, cw_shard)[:, :cw_rows].reshape(
        N_DEV, depth, CONV_WIDTH, cw_shard).transpose(1, 2, 0, 3).reshape(depth, CONV_WIDTH, N_DEV * cw_shard)

    def mixer_params(l):
        return dict(
            sgu_ln_g=sgu_ln_g[l:l + 1], sgu_ln_b=sgu_ln_b[l:l + 1], w_spatial=w_spatial[l],
            bs_full=jnp.repeat(b_spatial[l].T, HEAD_DIM, axis=1),
            conv_w=conv_w_full[l], conv_b=conv_b[l:l + 1], conv_ln_g=conv_ln_g[l:l + 1],
            conv_ln_b=conv_ln_b[l:l + 1], bd=_block_diag(pool_w[l]), pool_scale=pool_scale[l:l + 1])

    saved = []
    cur = xs
    for l in range(depth):
        p = mixer_params(l)
        x0 = cur
        more = l + 1 < depth
        (x1, gate1, up1, act1), part = _ffn_fwd(x0, ffn1_norm[l:l + 1], wffn[(l, 0)], 0, f"ffn1_fwd_{l}",
                                          _GatherIci([ffn_shards[l][1]]))
        riding = [_GatherForward(part)] + ([_GatherIci([win_shards[l + 1], wout_shards[l + 1]])] if more else [])
        (x2, z, hc, cat), part = _mixer_fwd(x1, mix_norm[l:l + 1], wb[l], wc[l], p, f"mixer_fwd_{l}", _Merged(riding))
        wffn[(l, 1)] = part[0]
        riding = [_GatherIci([ffn_shards[l + 1][0]]), _GatherForward(part[1:])] if more else []
        relay = _GatherForward(riding[0].out_shapes) if more else None
        outs, part = _ffn_fwd(x2, ffn2_norm[l:l + 1], wffn[(l, 1)], 0, f"ffn2_fwd_{l}",
                              _Merged(riding) if more else None,
                              None if more else (final_norm.reshape(1, d), target), relay)
        if more:
            cur, gate2, up2, act2 = outs
            wffn[(l + 1, 0)], wb[l + 1], wc[l + 1] = part
        else:
            dx, gate2, up2, act2, d_final, loss_part = outs
        saved.append((p, x0, gate1, up1, act1, x1, z, hc, cat, x2, gate2, up2, act2))

    cidx = lax.axis_index("c").astype(jnp.int32).reshape(1)
    from_chips = {}
    to_pair, to_chip = [], []
    small = []

    def pair_payload():
        return _PairExchange([g for _, g in to_pair]) if to_pair else None

    def pair_done(received):
        if to_pair:
            (nm, l), _ = to_pair[0]
            sums = _pair_sums([g for _, g in to_pair], list(received), cidx, f"rs_pair_sum_{nm}_{l}")
            to_chip.extend((key, s) for (key, _), s in zip(to_pair, sums))
        to_pair.clear()

    def take_chip():
        items = list(to_chip)
        to_chip.clear()
        return items

    def chip_payload(items):
        return _ChipExchange([s for _, s in items]) if items else None

    def chip_done(items, landed):
        for (key, _), o in zip(items, landed):
            from_chips[key] = o

    def ffn_weight_grads(prefix, l, dgate, dup, act, h, dy):
        items = take_chip()
        items, later = items[:2], items[2:]
        to_chip.extend(later)
        grads3, landed = _ffn_dw(dgate, dup, act, h, dy, f"dw_{prefix}_{l}", chip_payload(items))
        chip_done(items, landed)
        to_pair.extend(((f"{prefix}_{nm}", l), g) for nm, g in zip(("w_gate", "w_up", "w_down"), grads3))

    for l in reversed(range(depth)):
        p, x0, gate1, up1, act1, x1, z, hc, cat, x2, gate2, up2, act2 = saved[l]
        (dx, dgate, dup, h, dy, dg_ffn2), received = _ffn_bwd(
            x2, ffn2_norm[l:l + 1], dx, gate2, up2, wffn[(l, 1)], 0, f"ffn2_bwd_{l}", pair_payload())
        pair_done(received)
        ffn_weight_grads("ffn2", l, dgate, dup, act2, h, dy)
        g_out, received = _tn_matmul(cat, dx, f"dw_w_out_{l}", pair_payload())
        pair_done(received)
        items = take_chip()
        items, later = items[:3], items[3:]
        to_chip.extend(later)
        (dx, dz, hm, dg_mix, g384, gws, gpool), landed = _mixer_bwd(
            x1, mix_norm[l:l + 1], z, hc, dx, wb[l], wc[l], p, f"mixer_bwd_{l}", chip_payload(items))
        chip_done(items, landed)
        g_in, _ = _tn_matmul(dz, hm, f"dw_w_in_{l}")
        to_pair.extend([(("w_out", l), g_out), (("w_in", l), g_in)])
        if l > 0:
            (dx, dgate, dup, h, dy, dg_ffn1), received = _ffn_bwd(
                x0, ffn1_norm[l:l + 1], dx, gate1, up1, wffn[(l, 0)], 0, f"ffn1_bwd_{l}", pair_payload())
            pair_done(received)
            ffn_weight_grads("ffn1", l, dgate, dup, act1, h, dy)
            small.append((l, g384, gws, gpool, dg_ffn1, dg_mix, dg_ffn2))
            continue

        small.append((0, g384, gws, gpool, None, dg_mix, dg_ffn2))
        small.sort(key=lambda s: s[0])
        norm_rows = []
        for (sl, _, _, _, dg1, dgm, dg2) in small:
            norm_rows += [jnp.zeros((1, d), F32) if dg1 is None else dg1, dgm, dg2]
        norm_rows += [d_final, jnp.pad(loss_part, ((0, 0), (0, d - LANES)))]
        n_norm = len(norm_rows)
        norm_pack = jnp.concatenate(norm_rows + [jnp.zeros((8 - n_norm % 8, d), F32)] * (n_norm % 8 != 0), axis=0)
        parts = [norm_pack]
        for (_, s384, sws, spool, _, _, _) in small:
            parts += [s384, sws.reshape((n_head + 1) * CHUNK, CHUNK), spool]
        n_pair = len(to_pair)
        early = take_chip()
        riding = [pair_payload(), _GatherIci([a[None] for a in parts])] + ([chip_payload(early)] if early else [])
        (dx, dgate, dup, h, dy, dg_ffn1), landed = _ffn_bwd(
            x0, ffn1_norm[l:l + 1], dx, gate1, up1, wffn[(l, 0)], 0, f"ffn1_bwd_{l}", _Merged(riding))
        pair_done(landed[:n_pair])
        chip_done(early, landed[n_pair + len(parts):])
        items = take_chip()
        g_gate, landed = _tn_matmul(
            dgate, h, f"dw_ffn1_w_gate_{l}",
            _Merged([chip_payload(items), _GatherForward(landed[n_pair:n_pair + len(parts)])]))
        chip_done(items, landed[:len(items)])
        gathered = landed[len(items):]
        to_pair.append((("ffn1_w_gate", l), g_gate))
        g_up, received = _tn_matmul(dup, h, f"dw_ffn1_w_up_{l}", pair_payload())
        pair_done(received)
        to_pair.append((("ffn1_w_up", l), g_up))
        items = take_chip()
        g_down, landed = _tn_matmul(act1, dy, f"dw_ffn1_w_down_{l}", _Merged([chip_payload(items), pair_payload()]))
        chip_done(items, landed[:len(items)])
        pair_done(landed[len(items):])
        to_pair.append((("ffn1_w_down", l), g_down))
    grad_x = dx.reshape(x.shape)
    pair_done(_comm(pair_payload(), "rs_pair_exchange_last"))
    items = take_chip()
    (late_norm,), landed, summed = _all_gather([jnp.pad(dg_ffn1, ((0, 7), (0, 0)))[None]], "ag_tail",
                                               chip_payload(items), [g[0] for g in gathered])
    chip_done(items, landed)
    late_sum = _sum_blocks(late_norm[0], N_DEV, "sum_small_late")
    norm_sum = summed[0]
    loss = norm_sum[3 * depth + 1, 0]
    cpos = lax.axis_index("x") * 4 + lax.axis_index("y") * 2 + lax.axis_index("c")
    sg = {nm: [] for nm in names}
    for l in range(depth):
        g384, gws, gpool = summed[1 + 3 * l], summed[2 + 3 * l].reshape(n_head + 1, CHUNK, CHUNK), summed[3 + 3 * l]
        sg["ffn1_norm"].append(norm_sum[3 * l] if l > 0 else late_sum[0])
        sg["mix_norm"].append(norm_sum[3 * l + 1])
        sg["ffn2_norm"].append(norm_sum[3 * l + 2])
        sg["sgu_ln_g"].append(g384[_R_SGU_G])
        sg["sgu_ln_b"].append(g384[_R_SGU_B])
        sg["conv_b"].append(g384[_R_CONV_B])
        sg["conv_ln_g"].append(g384[_R_CLN_G])
        sg["conv_ln_b"].append(g384[_R_CLN_B])
        sg["conv_w"].append(lax.dynamic_slice_in_dim(g384[_R_CONV_W:_R_CONV_W + CONV_WIDTH], cpos * cw_shard,
                                                     cw_shard, axis=1))
        sg["w_spatial"].append(gws[:n_head])
        sg["b_spatial"].append(gws[n_head][:, :n_head].T)
        sg["pool_w"].append(jnp.stack([gpool[k * HEAD_DIM:(k + 1) * HEAD_DIM, k * HEAD_DIM:(k + 1) * HEAD_DIM]
                                       for k in range(pool // HEAD_DIM)], axis=0))
        sg["pool_scale"].append(gpool[pool])
    small_names = ["ffn1_norm", "mix_norm", "sgu_ln_g", "sgu_ln_b", "w_spatial", "b_spatial", "conv_w", "conv_b",
                   "conv_ln_g", "conv_ln_b", "pool_w", "pool_scale", "ffn2_norm"]
    grads = {nm: jnp.stack(sg[nm], axis=0) for nm in small_names}
    grads["final_norm"] = norm_sum[3 * depth]

    delta, new_m, new_v = {}, {}, {}
    big_names = ["ffn1_w_gate", "ffn1_w_up", "ffn1_w_down", "w_in", "w_out", "ffn2_w_gate", "ffn2_w_up",
                 "ffn2_w_down"]
    transposed = {"ffn1_w_gate", "ffn1_w_up", "w_in", "ffn2_w_gate", "ffn2_w_up"}
    for nm in big_names:
        view = (lambda a: jnp.swapaxes(a, 1, 2)) if nm in transposed else (lambda a: a)
        outs = _finish_sharded([from_chips[(nm, l)] for l in range(depth)], view(W[nm]), view(M[nm]), view(V[nm]),
                               f"adamw_{nm}")
        grads[nm], delta[nm], new_m[nm], new_v[nm] = (view(o) for o in outs)
    snames = small_names + ["final_norm"]

    def flat2(a):
        return a.reshape(-1, a.shape[-1])

    outs = _adamw_small([flat2(W[nm]) for nm in snames], [flat2(grads[nm]) for nm in snames],
                        [flat2(M[nm]) for nm in snames], [flat2(V[nm]) for nm in snames], "adamw_small")
    ns = len(snames)
    for k, nm in enumerate(snames):
        shp = W[nm].shape
        delta[nm], new_m[nm], new_v[nm] = (outs[k].reshape(shp), outs[ns + k].reshape(shp),
                                           outs[2 * ns + k].reshape(shp))

    return (loss, grad_x, *[grads[nm] for nm in names], *[delta[nm] for nm in names],
            *[new_m[nm] for nm in names], *[new_v[nm] for nm in names])
```

```python
import functools

import jax
import jax.numpy as jnp
from jax import lax
from jax.experimental import pallas as pl
from jax.experimental.pallas import tpu as pltpu

F32 = jnp.float32
BF16 = jnp.bfloat16
EPS = 1e-6
N_DEV = 8
N_CHIP = 4
MESH = pl.DeviceIdType.MESH
ANY = pl.BlockSpec(memory_space=pl.ANY)

VMEM_LIMIT_BYTES = 56 * 1024 * 1024
LANES = 128
HALO = 32
HEAD_DIM = 64
CHUNK = 128
CONV_WIDTH = 31
POOL_WINDOWS = (2, 4, 8, 16)

ADAM_LR = 0.001
ADAM_B1 = 0.9
ADAM_B2 = 0.999
ADAM_EPS = 1e-08
ADAM_WD = 0.01
ADAM_STEP = 10


def _cparams(sem=None):
    return pltpu.CompilerParams(dimension_semantics=sem, vmem_limit_bytes=VMEM_LIMIT_BYTES)


def _position():
    return lax.axis_index("x"), lax.axis_index("y"), lax.axis_index("c")


class _Copies:
    def __init__(self):
        self.local, self.sends, self.recvs = [], [], []

    def extend(self, other):
        self.local += other.local
        self.sends += other.sends
        self.recvs += other.recvs

    def start(self):
        for cp in self.local + self.sends:
            cp.start()

    def wait(self):
        for land, send_sems, recv_sems, k, peer in self.recvs:
            _remote(land, land, send_sems, recv_sems, k, peer).wait_recv()
        for cp in self.sends:
            cp.wait_send()
        for cp in self.local:
            cp.wait()


def _remote(src, dst, send_sems, recv_sems, k, to):
    return pltpu.make_async_remote_copy(src_ref=src, dst_ref=dst, send_sem=send_sems.at[k], recv_sem=recv_sems.at[k],
                                        device_id=to, device_id_type=MESH)


class _Payload:
    ins, out_shapes, aliases, n_remote, n_local = (), (), {}, 0, 0

    def sem_shapes(self):
        return [pltpu.SemaphoreType.DMA((max(self.n_remote, 1),)), pltpu.SemaphoreType.DMA((max(self.n_remote, 1),)),
                pltpu.SemaphoreType.DMA((max(self.n_local, 1),))]


class _GatherIci(_Payload):
    def __init__(self, shards):
        self.ins = list(shards)
        self.out_shapes = [jax.ShapeDtypeStruct((s.shape[0], N_DEV * s.shape[1], s.shape[2]), s.dtype) for s in shards]
        self.n_remote, self.n_local = 4 * len(shards), len(shards)

    def build(self, ins, outs, send_sems, recv_sems, local_sems, k0=0, l0=0):
        x, y, c = _position()
        peers = [(x, y, 1 - c), (1 - x, y, c), (x, 1 - y, c), (1 - x, 1 - y, c)]
        cps = _Copies()
        for a, (src, out) in enumerate(zip(ins, outs)):
            r = src.shape[1]

            def rows(px, py, pc, out=out, r=r):
                return out.at[:, pl.ds((4 * px + 2 * py + pc) * r, r), :]

            cps.local.append(pltpu.make_async_copy(src, rows(x, y, c), local_sems.at[l0 + a]))
            for k, peer in enumerate(peers):
                cps.sends.append(_remote(src, rows(x, y, c), send_sems, recv_sems, k0 + 4 * a + k, peer))
                cps.recvs.append((rows(*peer), send_sems, recv_sems, k0 + 4 * a + k, peer))
        return cps


class _GatherForward(_Payload):
    def __init__(self, partials):
        self.ins = list(partials)
        self.out_shapes = [jax.ShapeDtypeStruct(p.shape, p.dtype) for p in partials]
        self.aliases = {a: a for a in range(len(partials))}
        self.n_remote = 3 * len(partials)

    def build(self, ins, outs, send_sems, recv_sems, local_sems, k0=0, l0=0):
        x, y, c = _position()
        chips = [(1 - x, y), (x, 1 - y), (1 - x, 1 - y)]
        cps = _Copies()
        for a, out in enumerate(outs):
            r = out.shape[1] // N_DEV
            for k, (px, py) in enumerate(chips):
                mine = out.at[:, pl.ds((4 * px + 2 * py + c) * r, r), :]
                theirs = out.at[:, pl.ds((4 * px + 2 * py + 1 - c) * r, r), :]
                cps.sends.append(_remote(mine, mine, send_sems, recv_sems, k0 + 3 * a + k, (x, y, 1 - c)))
                cps.recvs.append((theirs, send_sems, recv_sems, k0 + 3 * a + k, (x, y, 1 - c)))
        return cps


class _PairExchange(_Payload):
    def __init__(self, grads):
        self.ins = list(grads)
        self.out_shapes = [jax.ShapeDtypeStruct((g.shape[0] // 2, g.shape[1]), g.dtype) for g in grads]
        self.n_remote = N_CHIP * len(grads)

    def build(self, ins, outs, send_sems, recv_sems, local_sems, k0=0, l0=0):
        x, y, c = _position()
        cps = _Copies()
        for a, (src, out) in enumerate(zip(ins, outs)):
            r = src.shape[0] // N_DEV
            for q in range(N_CHIP):
                land = out.at[pl.ds(q * r, r), :]
                cps.sends.append(_remote(src.at[pl.ds((2 * q + 1 - c) * r, r), :], land, send_sems, recv_sems,
                                         k0 + N_CHIP * a + q, (x, y, 1 - c)))
                cps.recvs.append((land, send_sems, recv_sems, k0 + N_CHIP * a + q, (x, y, 1 - c)))
        return cps


class _ChipExchange(_Payload):
    def __init__(self, sums):
        self.ins = list(sums)
        self.out_shapes = [jax.ShapeDtypeStruct(s.shape, s.dtype) for s in sums]
        self.n_remote, self.n_local = 3 * len(sums), len(sums)

    def build(self, ins, outs, send_sems, recv_sems, local_sems, k0=0, l0=0):
        x, y, c = _position()
        my_chip = 2 * x + y
        chips = [(1 - x, y), (x, 1 - y), (1 - x, 1 - y)]
        cps = _Copies()
        for a, (src, out) in enumerate(zip(ins, outs)):
            r = src.shape[0] // N_CHIP
            mine = out.at[pl.ds(my_chip * r, r), :]
            cps.local.append(pltpu.make_async_copy(src.at[pl.ds(my_chip * r, r), :], mine, local_sems.at[l0 + a]))
            for k, (px, py) in enumerate(chips):
                land = out.at[pl.ds((2 * px + py) * r, r), :]
                cps.sends.append(_remote(src.at[pl.ds((2 * px + py) * r, r), :], mine, send_sems, recv_sems,
                                         k0 + 3 * a + k, (px, py, c)))
                cps.recvs.append((land, send_sems, recv_sems, k0 + 3 * a + k, (px, py, c)))
        return cps


class _Merged(_Payload):
    def __init__(self, parts):
        self.parts = list(parts)
        self.ins = [a for p in parts for a in p.ins]
        self.out_shapes = [s for p in parts for s in p.out_shapes]
        self.aliases, self.offsets = {}, []
        i0 = o0 = k0 = l0 = 0
        for p in parts:
            self.offsets.append((i0, o0, k0, l0))
            self.aliases.update({i0 + i: o0 + o for i, o in p.aliases.items()})
            i0, o0, k0, l0 = i0 + len(p.ins), o0 + len(p.out_shapes), k0 + p.n_remote, l0 + p.n_local
        self.n_remote, self.n_local = k0, l0

    def build(self, ins, outs, send_sems, recv_sems, local_sems):
        cps = _Copies()
        for p, (i0, o0, k0, l0) in zip(self.parts, self.offsets):
            cps.extend(p.build(ins[i0:i0 + len(p.ins)], outs[o0:o0 + len(p.out_shapes)], send_sems, recv_sems,
                               local_sems, k0, l0))
        return cps


RELAY_AT = 0.88


def _call(body, name, grid, in_specs, out_specs, out_shape, scratch_shapes, semantics, args, payload=None,
          relay=None):
    if payload is None:
        outs = pl.pallas_call(body, name=name, grid=grid, in_specs=in_specs, out_specs=out_specs,
                              out_shape=out_shape, scratch_shapes=scratch_shapes,
                              compiler_params=_cparams(semantics))(*args)
        return list(outs), []
    n_in, n_out, n_scr = len(in_specs), len(out_specs), len(scratch_shapes)
    p_in, p_out = len(payload.ins), len(payload.out_shapes)

    def carried(*refs):
        ins, p_ins = refs[:n_in], refs[n_in:n_in + p_in]
        o0 = n_in + p_in
        outs, p_outs = refs[o0:o0 + n_out], refs[o0 + n_out:o0 + n_out + p_out]
        s0 = o0 + n_out + p_out
        scr, sems = refs[s0:s0 + n_scr], refs[s0 + n_scr:s0 + n_scr + 3]
        relay_sems = refs[s0 + n_scr + 3:]
        ids = [pl.program_id(k) for k in range(len(grid))]
        at_first = functools.reduce(jnp.logical_and, [i == 0 for i in ids])
        at_last = functools.reduce(jnp.logical_and, [i == g - 1 for i, g in zip(ids, grid)])

        @pl.when(at_first)
        def _():
            payload.build(p_ins, p_outs, *sems).start()

        body(*ins, *outs, *scr)

        if relay is None:
            @pl.when(at_last)
            def _():
                payload.build(p_ins, p_outs, *sems).wait()
        else:
            n_relay = len(relay.out_shapes)
            step = functools.reduce(lambda acc, ig: acc * ig[1] + ig[0], zip(ids, grid), 0)
            total = functools.reduce(lambda a, b: a * b, grid)

            @pl.when(step == int(RELAY_AT * total))
            def _():
                payload.build(p_ins, p_outs, *sems).wait()
                relay.build(p_outs[:n_relay], p_outs[:n_relay], *relay_sems).start()

            @pl.when(at_last)
            def _():
                relay.build(p_outs[:n_relay], p_outs[:n_relay], *relay_sems).wait()

    outs = pl.pallas_call(
        carried, name=name, grid=grid, in_specs=list(in_specs) + [ANY] * p_in,
        out_specs=list(out_specs) + [ANY] * p_out, out_shape=list(out_shape) + list(payload.out_shapes),
        scratch_shapes=list(scratch_shapes) + payload.sem_shapes() + (relay.sem_shapes() if relay else []),
        input_output_aliases={n_in + i: n_out + o for i, o in payload.aliases.items()},
        compiler_params=_cparams(("arbitrary",) * len(grid)))(*args, *payload.ins)
    return list(outs[:n_out]), list(outs[n_out:])


def _comm(payload, name):
    def body(*refs):
        p_in, p_out = len(payload.ins), len(payload.out_shapes)
        cps = payload.build(refs[:p_in], refs[p_in:p_in + p_out], *refs[p_in + p_out:])
        cps.start()
        cps.wait()

    return list(pl.pallas_call(
        body, name=name, in_specs=[ANY] * len(payload.ins), out_specs=[ANY] * len(payload.out_shapes),
        out_shape=list(payload.out_shapes), scratch_shapes=payload.sem_shapes(),
        input_output_aliases=dict(payload.aliases))(*payload.ins))


def _dot(a, b):
    return jnp.dot(a, b, preferred_element_type=F32)


def _dot_nt(a, b):
    return lax.dot_general(a, b, (((1,), (1,)), ((), ())), preferred_element_type=F32)


def _dot_tn(a, b):
    return lax.dot_general(a, b, (((0,), (0,)), ((), ())), preferred_element_type=F32)


def _split_dot(x, e):
    hi = x.astype(BF16)
    r1 = x - hi.astype(F32)
    mid = r1.astype(BF16)
    lo = (r1 - mid.astype(F32)).astype(BF16)
    return _dot(hi, e) + _dot(mid, e) + _dot(lo, e)


def _rms(x):
    rstd = lax.rsqrt(jnp.mean(x * x, axis=-1, keepdims=True) + EPS)
    return x * rstd, rstd


def _rms_bwd(xhat, rstd, g, dh):
    dxhat = dh * g
    dx = rstd * (dxhat - xhat * jnp.mean(dxhat * xhat, axis=-1, keepdims=True))
    return dx, jnp.sum(dh * xhat, axis=0, keepdims=True)


def _ln(v):
    mu = jnp.mean(v, axis=-1, keepdims=True)
    xc = v - mu
    rstd = lax.rsqrt(jnp.mean(xc * xc, axis=-1, keepdims=True) + EPS)
    return xc * rstd, rstd


def _ln_bwd(vhat, rstd, g, dy):
    dvhat = dy * g
    dv = rstd * (dvhat - jnp.mean(dvhat, axis=-1, keepdims=True)
                 - vhat * jnp.mean(dvhat * vhat, axis=-1, keepdims=True))
    return dv, jnp.sum(dy * vhat, axis=0, keepdims=True), jnp.sum(dy, axis=0, keepdims=True)


_INV_SQRT2 = 0.7071067811865476
_INV_SQRT2PI = 0.3989422804014327


def _gelu(x):
    return 0.5 * x * (1.0 + lax.erf(x * _INV_SQRT2))


def _gelu_grad(x):
    return 0.5 * (1.0 + lax.erf(x * _INV_SQRT2)) + x * jnp.exp(-0.5 * x * x) * _INV_SQRT2PI


def _silu_grad(x):
    s = jax.nn.sigmoid(x)
    return s * (1.0 + x * (1.0 - s))


def _ffn_fwd(x, g, wa, name, payload=None, head=None, relay=None):
    t, d = x.shape
    f = wa.shape[1]
    tm, tf = 1024, 256
    nc = f // tf
    groups = [slice(k * (tm // 2), (k + 1) * (tm // 2)) for k in range(2)]

    def body(x_ref, g_ref, wgu_ref, wd_ref, *rest):
        if head is None:
            xo_ref, gate_ref, up_ref, act_ref, h_scr, acc_scr = rest
        else:
            fg_ref, tgt_ref, xo_ref, gate_ref, up_ref, act_ref, dfg_ref, loss_ref, h_scr, acc_scr = rest
        c = pl.program_id(1)
        if head is not None:
            @pl.when((c == 0) & (pl.program_id(0) == 0))
            def _():
                dfg_ref[...] = jnp.zeros_like(dfg_ref)
                loss_ref[...] = jnp.zeros_like(loss_ref)

        @pl.when(c == 0)
        def _():
            xhat, _ = _rms(x_ref[...])
            h_scr[...] = (xhat * g_ref[...]).astype(BF16)
            acc_scr[...] = jnp.zeros_like(acc_scr)

        wgu, wd = wgu_ref[...].reshape(2 * tf, d), wd_ref[...]
        for rows in groups:
            gu = _dot_nt(h_scr[rows, :], wgu)
            gate, up = gu[:, :tf], gu[:, tf:]
            gate_ref[rows, :] = gate.astype(BF16)
            up_ref[rows, :] = up.astype(BF16)
            act = (gate * jax.nn.sigmoid(gate) * up).astype(BF16)
            act_ref[rows, :] = act
            acc_scr[rows, :] += _dot(act, wd)

        @pl.when(c == nc - 1)
        def _():
            xo = x_ref[...] + 0.5 * acc_scr[...]
            if head is None:
                xo_ref[...] = xo
            else:
                fg = fg_ref[...]
                xhat, rstd = _rms(xo)
                err = xhat * fg - tgt_ref[...]
                dxn, dfg = _rms_bwd(xhat, rstd, fg, err * (1.0 / d))
                xo_ref[...] = dxn
                dfg_ref[...] += dfg
                loss_ref[...] += jnp.zeros_like(loss_ref) + 0.5 * jnp.sum(jnp.mean(err * err, axis=-1, keepdims=True))

    row = pl.BlockSpec((tm, d), lambda i, c: (i, 0))
    vec = pl.BlockSpec((1, d), lambda i, c: (0, 0))
    in_specs = [row, vec, pl.BlockSpec((2, tf, d), lambda i, c: (0, c, 0)),
                pl.BlockSpec((None, tf, d), lambda i, c: (2, c, 0))]
    out_specs = [row] + [pl.BlockSpec((tm, tf), lambda i, c: (i, c))] * 3
    out_shape = [jax.ShapeDtypeStruct((t, d), F32)] + [jax.ShapeDtypeStruct((t, f), BF16)] * 3
    args = (x, g, wa, wa)
    if head is not None:
        in_specs += [vec, row]
        out_specs += [vec, pl.BlockSpec((1, LANES), lambda i, c: (0, 0))]
        out_shape += [jax.ShapeDtypeStruct((1, d), F32), jax.ShapeDtypeStruct((1, LANES), F32)]
        args += tuple(head)
    return _call(
        body, name, (t // tm, nc), in_specs, out_specs, out_shape,
        [pltpu.VMEM((tm, d), BF16), pltpu.VMEM((tm, d), F32)],
        ("parallel" if head is None else "arbitrary", "arbitrary"), args, payload, relay)


def _ffn_bwd(x, g, dxo, gate, up, wa, name, payload=None):
    t, d = x.shape
    f = wa.shape[1]
    tm, tf = 1024, 256
    nc = f // tf
    groups = [slice(k * (tm // 2), (k + 1) * (tm // 2)) for k in range(2)]

    def body(x_ref, g_ref, dxo_ref, gate_ref, up_ref, wgu_ref, wd_ref,
             dx_ref, dgate_ref, dup_ref, h_ref, dy_ref, dg_ref, acc_scr):
        i, c = pl.program_id(0), pl.program_id(1)

        @pl.when(c == 0)
        def _():
            xhat, _ = _rms(x_ref[...])
            h_ref[...] = (xhat * g_ref[...]).astype(BF16)
            dy_ref[...] = (0.5 * dxo_ref[...]).astype(BF16)
            acc_scr[...] = jnp.zeros_like(acc_scr)

        @pl.when((c == 0) & (i == 0))
        def _():
            dg_ref[...] = jnp.zeros_like(dg_ref)

        wg, wu, wd = wgu_ref[0], wgu_ref[1], wd_ref[...]
        for rows in groups:
            gt = gate_ref[rows, :].astype(F32)
            u = up_ref[rows, :].astype(F32)
            s = jax.nn.sigmoid(gt)
            silu = gt * s
            dact = _dot_nt(dy_ref[rows, :], wd)
            dgate = (dact * u * (s * (1.0 + gt * (1.0 - s)))).astype(BF16)
            dup = (dact * silu).astype(BF16)
            dgate_ref[rows, :] = dgate
            dup_ref[rows, :] = dup
            acc_scr[rows, :] += _dot(dgate, wg) + _dot(dup, wu)

        @pl.when(c == nc - 1)
        def _():
            xhat, rstd = _rms(x_ref[...])
            dxn, dg = _rms_bwd(xhat, rstd, g_ref[...], acc_scr[...])
            dx_ref[...] = dxo_ref[...] + dxn
            dg_ref[...] += dg

    row = pl.BlockSpec((tm, d), lambda i, c: (i, 0))
    col = pl.BlockSpec((tm, tf), lambda i, c: (i, c))
    vec = pl.BlockSpec((1, d), lambda i, c: (0, 0))
    return _call(
        body, name, (t // tm, nc),
        [row, vec, row, col, col, pl.BlockSpec((2, tf, d), lambda i, c: (0, c, 0)),
         pl.BlockSpec((None, tf, d), lambda i, c: (2, c, 0))],
        [row, col, col, row, row, vec],
        [jax.ShapeDtypeStruct((t, d), F32), jax.ShapeDtypeStruct((t, f), BF16),
         jax.ShapeDtypeStruct((t, f), BF16),
         jax.ShapeDtypeStruct((t, d), BF16), jax.ShapeDtypeStruct((t, d), BF16),
         jax.ShapeDtypeStruct((1, d), F32)],
        [pltpu.VMEM((tm, d), F32)],
        ("arbitrary", "arbitrary"), (x, g, dxo, gate, up, wa, wa), payload)


def _ffn_dw(dgate, dup, act, h, dy, name, payload=None):
    t, f = dgate.shape
    d = h.shape[1]
    tk = 512
    tmm = f // 2
    nk = t // tk

    def body(dg_ref, du_ref, a_ref, h_ref, dy_ref, og_ref, ou_ref, od_ref, acc_g, acc_u, acc_d):
        k = pl.program_id(1)

        @pl.when(k == 0)
        def _():
            acc_g[...] = jnp.zeros_like(acc_g)
            acc_u[...] = jnp.zeros_like(acc_u)
            acc_d[...] = jnp.zeros_like(acc_d)

        hv = h_ref[...]
        acc_g[...] += _dot_tn(dg_ref[...], hv)
        acc_u[...] += _dot_tn(du_ref[...], hv)
        acc_d[...] += _dot_tn(a_ref[...], dy_ref[...])

        @pl.when(k == nk - 1)
        def _():
            og_ref[...] = acc_g[...].astype(BF16)
            ou_ref[...] = acc_u[...].astype(BF16)
            od_ref[...] = acc_d[...].astype(BF16)

    col = pl.BlockSpec((tk, tmm), lambda j, k: (k, j))
    row = pl.BlockSpec((tk, d), lambda j, k: (k, 0))
    out = pl.BlockSpec((tmm, d), lambda j, k: (j, 0))
    return _call(
        body, name, (f // tmm, nk), [col, col, col, row, row], [out, out, out],
        [jax.ShapeDtypeStruct((f, d), BF16)] * 3, [pltpu.VMEM((tmm, d), F32)] * 3,
        ("parallel", "arbitrary"), (dgate, dup, act, h, dy), payload)


def _tn_matmul(a, b, name, payload=None):
    t, m = a.shape
    n = b.shape[1]
    tk = 1024
    tmm = m // 2 if (m > 2048 and (m // 2) % LANES == 0) else m
    nk = t // tk

    def body(a_ref, b_ref, o_ref, acc_scr):
        k = pl.program_id(1)

        @pl.when(k == 0)
        def _():
            acc_scr[...] = jnp.zeros_like(acc_scr)

        acc_scr[...] += _dot_tn(a_ref[...].astype(BF16), b_ref[...].astype(BF16))

        @pl.when(k == nk - 1)
        def _():
            o_ref[...] = acc_scr[...].astype(BF16)

    (out,), p_outs = _call(
        body, name, (m // tmm, nk),
        [pl.BlockSpec((tk, tmm), lambda j, k: (k, j)), pl.BlockSpec((tk, n), lambda j, k: (k, 0))],
        [pl.BlockSpec((tmm, n), lambda j, k: (j, 0))],
        [jax.ShapeDtypeStruct((m, n), BF16)],
        [pltpu.VMEM((tmm, n), F32)],
        ("parallel", "arbitrary"), (a, b), payload)
    return out, p_outs


def _lane_ids(shape):
    return lax.broadcasted_iota(jnp.int32, shape, 1)


def _tril(w):
    r = lax.broadcasted_iota(jnp.int32, w.shape, 0)
    c = lax.broadcasted_iota(jnp.int32, w.shape, 1)
    return jnp.where(r >= c, w, 0.0)


def _shift_down(x, k):
    return x if k == 0 else pltpu.roll(x, k, 0)


def _shift_up(x, k):
    return x if k == 0 else pltpu.roll(x, x.shape[0] - k, 0)


def _sub_tile_shifts(ext, shift):
    return [shift(ext, b) for b in range(8)]


def _tap(shifted, j, n_out, down):
    a, b = divmod(j, 8)
    r0 = HALO - 8 * a if down else 8 * a
    return shifted[b][r0:r0 + n_out]


def _depthwise(shifted, w, n_out, down):
    acc = None
    for j in range(CONV_WIDTH):
        term = _tap(shifted, j, n_out, down) * w[CONV_WIDTH - 1 - j:CONV_WIDTH - j]
        acc = term if acc is None else acc + term
    return acc


def _conv_wgrad(shifted, dhc, n_out):
    return [jnp.sum(_tap(shifted, CONV_WIDTH - 1 - k, n_out, True) * dhc, axis=0, keepdims=True)
            for k in range(CONV_WIDTH)]


def _window_sums(ext, shift):
    assert POOL_WINDOWS == (2, 4, 8, 16)
    s2 = ext + shift(ext, 1)
    s4 = s2 + shift(s2, 2)
    s8 = s4 + shift(s4, 4)
    s16 = s8 + shift(s8, 8)
    grp = _lane_ids(ext.shape) // HEAD_DIM
    return jnp.where(grp == 0, s2, jnp.where(grp == 1, s4, jnp.where(grp == 2, s8, s16)))


def _pool_count(t0, n, width):
    pos = (lax.broadcasted_iota(jnp.int32, (n, width), 0) + (t0 + 1)).astype(F32)
    grp = _lane_ids((n, width)) // HEAD_DIM
    w0, w1, w2, w3 = (float(w) for w in POOL_WINDOWS)
    win = jnp.where(grp == 0, w0, jnp.where(grp == 1, w1, jnp.where(grp == 2, w2, w3)))
    return jnp.minimum(pos, win)


def _block_diag(pw):
    gn, cg, _ = pw.shape
    rows = []
    for gi in range(gn):
        parts = [pw[gi] if gj == gi else jnp.zeros((cg, cg), pw.dtype) for gj in range(gn)]
        rows.append(jnp.concatenate(parts, axis=1))
    return jnp.concatenate(rows, axis=0)


def _head_pair_mix(w_even, w_odd, v):
    lo = _lane_ids((CHUNK, LANES)) < HEAD_DIM
    return jnp.where(lo, _dot(w_even, v), _dot(w_odd, v))


def _mixer_fwd(x, g, wb, wc, p, name, payload=None):
    t, d = x.shape
    d_in = wb.shape[1]
    sgu = p["sgu_ln_g"].shape[1]
    pool = p["pool_scale"].shape[1]
    d_mix = 2 * sgu + pool
    tm = 512
    n_i = t // tm
    hb = tm // HALO

    def body(x_ref, xp_ref, g_ref, wi_ref, wo_ref, lng_ref, lnb_ref, ws_ref, bs_ref, cw_ref, cb_ref, clg_ref,
             clb_ref, bd_ref, ps_ref, xo_ref, z_ref, hc_ref, cat_ref):
        i = pl.program_id(0)
        first = i == 0
        gain, wi = g_ref[...], wi_ref[...]

        def project(xv):
            xhat, _ = _rms(xv)
            return _dot_nt((xhat * gain).astype(BF16), wi)

        z_main = project(x_ref[...])
        z_ref[...] = z_main
        z_prev = jnp.where(first, 0.0, project(xp_ref[...]))

        lng, lnb = lng_ref[...], lnb_ref[...]
        wt = [_tril(ws_ref[h]).astype(BF16) for h in range(sgu // HEAD_DIM)]
        for n in range(tm // CHUNK):
            rows = slice(n * CHUNK, (n + 1) * CHUNK)
            u = _gelu(z_main[rows, 0:sgu])
            vhat, _ = _ln(_gelu(z_main[rows, sgu:2 * sgu]))
            vn = (vhat * lng + lnb).astype(BF16)
            for gp in range(sgu // LANES):
                ls = slice(gp * LANES, (gp + 1) * LANES)
                mixed = _head_pair_mix(wt[2 * gp], wt[2 * gp + 1], vn[:, ls]) + bs_ref[:, ls]
                cat_ref[rows, ls] = (u[:, ls] * mixed).astype(BF16)

        def glu(zz):
            return zz[:, 2 * sgu:3 * sgu] * jax.nn.sigmoid(zz[:, 3 * sgu:4 * sgu])

        ext = jnp.concatenate([glu(z_prev), glu(z_main)], axis=0)
        hc = _depthwise(_sub_tile_shifts(ext, _shift_down), cw_ref[...], tm, True) + cb_ref[...]
        hc_ref[...] = hc
        hhat, _ = _ln(hc)
        bn = hhat * clg_ref[...] + clb_ref[...]
        cat_ref[:, sgu:2 * sgu] = (bn * jax.nn.sigmoid(bn)).astype(BF16)

        pext = jnp.concatenate([z_prev[:, 4 * sgu:], z_main[:, 4 * sgu:]], axis=0)
        sums = _window_sums(pext, _shift_down)[HALO:]
        pooled = sums / _pool_count(i * tm, tm, pool) - z_main[:, 4 * sgu:]
        mixed_c = _dot(pooled.astype(BF16), bd_ref[...].astype(BF16))
        cat_ref[:, 2 * sgu:] = (mixed_c * ps_ref[...]).astype(BF16)

        xo_ref[...] = x_ref[...] + _dot(cat_ref[...], wo_ref[...])

    def vec(n):
        return pl.BlockSpec((1, n), lambda i: (0, 0))

    return _call(
        body, name, (n_i,),
        [pl.BlockSpec((tm, d), lambda i: (i, 0)),
         pl.BlockSpec((HALO, d), lambda i: (jnp.maximum(i * hb - 1, 0), 0)),
         vec(d),
         pl.BlockSpec((None, d_in, d), lambda i: (0, 0, 0)), pl.BlockSpec((None, d_mix, d), lambda i: (0, 0, 0)),
         vec(sgu), vec(sgu),
         pl.BlockSpec(p["w_spatial"].shape, lambda i: (0, 0, 0)),
         pl.BlockSpec((CHUNK, sgu), lambda i: (0, 0)),
         pl.BlockSpec((CONV_WIDTH, sgu), lambda i: (0, 0)),
         vec(sgu), vec(sgu), vec(sgu),
         pl.BlockSpec((pool, pool), lambda i: (0, 0)), vec(pool)],
        [pl.BlockSpec((tm, d), lambda i: (i, 0)), pl.BlockSpec((tm, d_in), lambda i: (i, 0)),
         pl.BlockSpec((tm, sgu), lambda i: (i, 0)), pl.BlockSpec((tm, d_mix), lambda i: (i, 0))],
        [jax.ShapeDtypeStruct((t, d), F32), jax.ShapeDtypeStruct((t, d_in), F32),
         jax.ShapeDtypeStruct((t, sgu), F32), jax.ShapeDtypeStruct((t, d_mix), BF16)], [], ("parallel",),
        (x, x, g, wb, wc, p["sgu_ln_g"], p["sgu_ln_b"], p["w_spatial"], p["bs_full"], p["conv_w"], p["conv_b"],
         p["conv_ln_g"], p["conv_ln_b"], p["bd"], p["pool_scale"]), payload)


_R_SGU_G, _R_SGU_B, _R_CONV_B, _R_CLN_G, _R_CLN_B, _R_CONV_W = 0, 1, 2, 3, 4, 8
_R384_ROWS = 40


def _mixer_bwd(x, g, z, hc_saved, dxo, wb, wc, p, name, payload=None):
    t, d_in = z.shape
    d = x.shape[1]
    sgu = p["sgu_ln_g"].shape[1]
    pool = p["pool_scale"].shape[1]
    d_mix = 2 * sgu + pool
    n_head = sgu // HEAD_DIM
    tm = 512
    n_i = t // tm
    hb = tm // HALO

    def body(x_ref, g_ref, wi_ref, z_ref, zp_ref, zn_ref, dxo_ref, dxon_ref, wo_ref, hc_ref, hcn_ref, lng_ref,
             lnb_ref, ws_ref, bs_ref, cw_ref, clg_ref, clb_ref, bd_ref, ps_ref,
             dx_ref, dz_ref, hm_ref, dgm_ref, g384_ref, gws_ref, gpool_ref, dbs_scr):
        i = pl.program_id(0)
        first, last = i == 0, i == n_i - 1

        @pl.when(first)
        def _():
            dgm_ref[...] = jnp.zeros_like(dgm_ref)
            g384_ref[...] = jnp.zeros_like(g384_ref)
            gws_ref[...] = jnp.zeros_like(gws_ref)
            gpool_ref[...] = jnp.zeros_like(gpool_ref)
            dbs_scr[...] = jnp.zeros_like(dbs_scr)

        z_main = z_ref[...]
        z_prev = jnp.where(first, 0.0, zp_ref[...])
        z_next = jnp.where(last, 0.0, zn_ref[...])
        wo = wo_ref[...]
        dc_main = _dot_nt(dxo_ref[...].astype(BF16), wo)
        dc_next = jnp.where(last, 0.0, _dot_nt(dxon_ref[...].astype(BF16), wo))

        lng, lnb = lng_ref[...], lnb_ref[...]
        wt = [_tril(ws_ref[h]) for h in range(n_head)]
        wt_b = [w.astype(BF16) for w in wt]
        wtt_b = [w.T.astype(BF16) for w in wt]
        lo = _lane_ids((CHUNK, LANES)) < HEAD_DIM
        d_lng = jnp.zeros((1, sgu), F32)
        d_lnb = jnp.zeros((1, sgu), F32)
        dws = [jnp.zeros((CHUNK, CHUNK), F32) for _ in range(n_head)]
        for n in range(tm // CHUNK):
            rows = slice(n * CHUNK, (n + 1) * CHUNK)
            au, av = z_main[rows, 0:sgu], z_main[rows, sgu:2 * sgu]
            u = _gelu(au)
            vhat, vrstd = _ln(_gelu(av))
            vn = (vhat * lng + lnb).astype(BF16)
            da = dc_main[rows, 0:sgu]
            dmixed = da * u
            dbs_scr[...] += dmixed
            dvn_parts, du_parts = [], []
            for gp in range(sgu // LANES):
                ls = slice(gp * LANES, (gp + 1) * LANES)
                vn_g = vn[:, ls]
                mixed = _head_pair_mix(wt_b[2 * gp], wt_b[2 * gp + 1], vn_g) + bs_ref[:, ls]
                du_parts.append(da[:, ls] * mixed)
                dm_g = dmixed[:, ls]
                dm_b = dm_g.astype(BF16)
                dvn_parts.append(jnp.where(lo, _dot(wtt_b[2 * gp], dm_b), _dot(wtt_b[2 * gp + 1], dm_b)))
                dws[2 * gp] = dws[2 * gp] + _dot_nt(jnp.where(lo, dm_g, 0.0).astype(BF16), vn_g)
                dws[2 * gp + 1] = dws[2 * gp + 1] + _dot_nt(jnp.where(lo, 0.0, dm_g).astype(BF16), vn_g)
            dvn = jnp.concatenate(dvn_parts, axis=1)
            du = jnp.concatenate(du_parts, axis=1)
            dv, dg_n, db_n = _ln_bwd(vhat, vrstd, lng, dvn)
            d_lng = d_lng + dg_n
            d_lnb = d_lnb + db_n
            dz_ref[rows, 0:sgu] = (du * _gelu_grad(au)).astype(BF16)
            dz_ref[rows, sgu:2 * sgu] = (dv * _gelu_grad(av)).astype(BF16)
        for h in range(n_head):
            gws_ref[h] += _tril(dws[h])
        g384_ref[_R_SGU_G:_R_SGU_G + 1, :] += d_lng
        g384_ref[_R_SGU_B:_R_SGU_B + 1, :] += d_lnb

        clg = clg_ref[...]
        bcols = slice(2 * sgu, 4 * sgu)
        zb = jnp.concatenate([z_prev[:, bcols], z_main[:, bcols], z_next[:, bcols]], axis=0)
        bval, bgate = zb[:, 0:sgu], zb[:, sgu:2 * sgu]
        sg = jax.nn.sigmoid(bgate)
        hglu = bval * sg
        n_out = tm + HALO
        hglu_shifts = _sub_tile_shifts(hglu, _shift_down)
        cw = cw_ref[...]
        hc = jnp.concatenate([hc_ref[...], jnp.where(last, 0.0, hcn_ref[...])], axis=0)
        hhat, hrstd = _ln(hc)
        bn = hhat * clg + clb_ref[...]
        db = jnp.concatenate([dc_main[:, sgu:2 * sgu], dc_next[:, sgu:2 * sgu]], axis=0)
        dbn = db * _silu_grad(bn)
        dhc_all, _, _ = _ln_bwd(hhat, hrstd, clg, dbn)
        dbn_m, hhat_m, dhc = dbn[:tm], hhat[:tm], dhc_all[:tm]
        g384_ref[_R_CLN_G:_R_CLN_G + 1, :] += jnp.sum(dbn_m * hhat_m, axis=0, keepdims=True)
        g384_ref[_R_CLN_B:_R_CLN_B + 1, :] += jnp.sum(dbn_m, axis=0, keepdims=True)
        g384_ref[_R_CONV_B:_R_CONV_B + 1, :] += jnp.sum(dhc, axis=0, keepdims=True)
        wrows = _conv_wgrad(hglu_shifts, dhc, tm)
        for k in range(CONV_WIDTH):
            g384_ref[_R_CONV_W + k:_R_CONV_W + k + 1, :] += wrows[k]
        dhglu = _depthwise(_sub_tile_shifts(dhc_all, _shift_up), cw, tm, False)
        bval_m, sg_m = bval[HALO:HALO + tm], sg[HALO:HALO + tm]
        dz_ref[:, 2 * sgu:3 * sgu] = (dhglu * sg_m).astype(BF16)
        dz_ref[:, 3 * sgu:4 * sgu] = (dhglu * bval_m * sg_m * (1.0 - sg_m)).astype(BF16)

        bd_b = bd_ref[...].astype(BF16)
        ps = ps_ref[...]
        p_main = z_main[:, 4 * sgu:]
        pext = jnp.concatenate([z_prev[:, 4 * sgu:], p_main], axis=0)
        cnt = _pool_count(i * tm, n_out, pool)
        pooled = _window_sums(pext, _shift_down)[HALO:] / cnt[:tm] - p_main
        pooled_b = pooled.astype(BF16)
        dcc = jnp.concatenate([dc_main[:, 2 * sgu:], dc_next[:, 2 * sgu:]], axis=0)
        dmix_c = dcc * ps
        mixed_c = _dot(pooled_b, bd_b)
        grp_r = lax.broadcasted_iota(jnp.int32, (pool, pool), 0) // HEAD_DIM
        grp_c = lax.broadcasted_iota(jnp.int32, (pool, pool), 1) // HEAD_DIM
        gpool_ref[0:pool, :] += jnp.where(grp_r == grp_c, _dot_tn(pooled_b, dmix_c[:tm].astype(BF16)), 0.0)
        gpool_ref[pool:pool + 1, :] += jnp.sum(dcc[:tm] * mixed_c, axis=0, keepdims=True)
        dpooled = _dot_nt(dmix_c.astype(BF16), bd_b)
        q = dpooled / cnt
        dp = _window_sums(q, _shift_up)[:tm] - dpooled[:tm]
        dz_ref[:, 4 * sgu:] = dp.astype(BF16)

        gain = g_ref[...]
        xhat, rstd = _rms(x_ref[...])
        hm_ref[...] = (xhat * gain).astype(BF16)
        dxn, dgm = _rms_bwd(xhat, rstd, gain, _dot(dz_ref[...], wi_ref[...]))
        dx_ref[...] = dxo_ref[...] + dxn
        dgm_ref[...] += dgm

        @pl.when(last)
        def _():
            r = lax.broadcasted_iota(jnp.int32, (sgu, LANES), 0)
            c = lax.broadcasted_iota(jnp.int32, (sgu, LANES), 1)
            sel = (r // HEAD_DIM == c).astype(BF16)
            gws_ref[n_head] = _split_dot(dbs_scr[...], sel)

    def vec(n):
        return pl.BlockSpec((1, n), lambda i: (0, 0))

    def prev_map(i):
        return (jnp.maximum(i * hb - 1, 0), 0)

    def next_map(i):
        return (jnp.minimum((i + 1) * hb, n_i * hb - 1), 0)

    return _call(
        body, name, (n_i,),
        [pl.BlockSpec((tm, d), lambda i: (i, 0)), pl.BlockSpec((1, d), lambda i: (0, 0)),
         pl.BlockSpec((None, d_in, d), lambda i: (0, 0, 0)),
         pl.BlockSpec((tm, d_in), lambda i: (i, 0)),
         pl.BlockSpec((HALO, d_in), prev_map), pl.BlockSpec((HALO, d_in), next_map),
         pl.BlockSpec((tm, d), lambda i: (i, 0)), pl.BlockSpec((HALO, d), next_map),
         pl.BlockSpec((None, d_mix, d), lambda i: (0, 0, 0)),
         pl.BlockSpec((tm, sgu), lambda i: (i, 0)), pl.BlockSpec((HALO, sgu), next_map),
         vec(sgu), vec(sgu),
         pl.BlockSpec(p["w_spatial"].shape, lambda i: (0, 0, 0)),
         pl.BlockSpec((CHUNK, sgu), lambda i: (0, 0)),
         pl.BlockSpec((CONV_WIDTH, sgu), lambda i: (0, 0)),
         vec(sgu), vec(sgu),
         pl.BlockSpec((pool, pool), lambda i: (0, 0)), vec(pool)],
        [pl.BlockSpec((tm, d), lambda i: (i, 0)), pl.BlockSpec((tm, d_in), lambda i: (i, 0)),
         pl.BlockSpec((tm, d), lambda i: (i, 0)), pl.BlockSpec((1, d), lambda i: (0, 0)),
         pl.BlockSpec((_R384_ROWS, sgu), lambda i: (0, 0)),
         pl.BlockSpec((n_head + 1, CHUNK, CHUNK), lambda i: (0, 0, 0)),
         pl.BlockSpec((pool + 8, pool), lambda i: (0, 0))],
        [jax.ShapeDtypeStruct((t, d), F32), jax.ShapeDtypeStruct((t, d_in), BF16),
         jax.ShapeDtypeStruct((t, d), BF16), jax.ShapeDtypeStruct((1, d), F32),
         jax.ShapeDtypeStruct((_R384_ROWS, sgu), F32),
         jax.ShapeDtypeStruct((n_head + 1, CHUNK, CHUNK), F32),
         jax.ShapeDtypeStruct((pool + 8, pool), F32)],
        [pltpu.VMEM((CHUNK, sgu), F32)], ("arbitrary",),
        (x, g, wb, z, z, z, dxo, dxo, wc, hc_saved, hc_saved, p["sgu_ln_g"], p["sgu_ln_b"], p["w_spatial"],
         p["bs_full"], p["conv_w"], p["conv_ln_g"], p["conv_ln_b"], p["bd"], p["pool_scale"]), payload)


def _all_gather(arrs, name, extra=None, to_sum=()):
    gather = _GatherIci(arrs)
    n = len(arrs)
    forward = _GatherForward([jax.ShapeDtypeStruct(s.shape, s.dtype) for s in gather.out_shapes])
    x_in = len(extra.ins) if extra else 0
    x_out = len(extra.out_shapes) if extra else 0
    n_sum = len(to_sum)

    def body(*refs):
        ins, x_ins, s_ins = refs[:n], refs[n:n + x_in], refs[n + x_in:n + x_in + n_sum]
        o0 = n + x_in + n_sum
        outs, x_outs = refs[o0:o0 + n], refs[o0 + n:o0 + n + x_out]
        s_outs = refs[o0 + n + x_out:o0 + n + x_out + n_sum]
        sems = refs[o0 + n + x_out + n_sum:]
        first = gather.build(ins, outs, *sems[0:3])
        first.start()
        if extra:
            beside = extra.build(x_ins, x_outs, *sems[6:9])
            beside.start()
        for s_ref, o_ref in zip(s_ins, s_outs):
            r = o_ref.shape[0]
            acc = s_ref[0:r, :]
            for q in range(1, N_DEV):
                acc = acc + s_ref[q * r:(q + 1) * r, :]
            o_ref[...] = acc
        first.wait()
        second = forward.build(outs, outs, *sems[3:6])
        second.start()
        second.wait()
        if extra:
            beside.wait()

    in_vmem = pl.BlockSpec(memory_space=pltpu.VMEM)
    outs = pl.pallas_call(
        body, name=name,
        in_specs=[ANY] * (n + x_in) + [in_vmem] * n_sum, out_specs=[ANY] * (n + x_out) + [in_vmem] * n_sum,
        out_shape=list(gather.out_shapes) + (list(extra.out_shapes) if extra else [])
        + [jax.ShapeDtypeStruct((s.shape[0] // N_DEV, s.shape[1]), s.dtype) for s in to_sum],
        scratch_shapes=gather.sem_shapes() + forward.sem_shapes() + (extra.sem_shapes() if extra else []),
        compiler_params=_cparams(),
    )(*arrs, *(extra.ins if extra else []), *to_sum)
    return list(outs[:n]), list(outs[n:n + x_out]), list(outs[n + x_out:])


def _all_gather_relayed(arrs, name):
    n = len(arrs)
    n_pairs = 8
    units = [(a, l) for a in range(n) for l in range(arrs[a].shape[0])]

    def body(*refs):
        ins, outs = refs[:n], refs[n:2 * n]
        send_sems, recv_sems, local_sems = refs[2 * n:]
        x, y, c = _position()
        sib, xn, yn = (x, y, 1 - c), (1 - x, y, c), (x, 1 - y, c)
        diag = (1 - x, 1 - y, c)

        def rows(u, px, py, pc, half=None):
            a, l = units[u]
            r = ins[a].shape[1]
            base = (4 * px + 2 * py + pc) * r
            if half is None:
                return outs[a].at[pl.ds(l, 1), pl.ds(base, r), :]
            return outs[a].at[pl.ds(l, 1), pl.ds(base + half * (r // 2), r // 2), :]

        def send(u, k, src, dst, to):
            return _remote(src, dst, send_sems, recv_sems, u * n_pairs + k, to)

        def arrived(u, k, land, sender):
            _remote(land, land, send_sems, recv_sems, u * n_pairs + k, sender).wait_recv()

        own, sent = [], []
        for u, (a, l) in enumerate(units):
            src = ins[a].at[pl.ds(l, 1)]
            own.append(pltpu.make_async_copy(src, rows(u, x, y, c), local_sems.at[u]))
            sent += [send(u, k, src, rows(u, x, y, c), to) for k, to in enumerate((sib, xn, yn))]
        for cp in own + sent:
            cp.start()
        for u in range(len(units)):
            arrived(u, 1, rows(u, *xn), xn)
            arrived(u, 2, rows(u, *yn), yn)
            relay = [send(u, 3, rows(u, *xn, half=0), rows(u, *xn, half=0), yn),
                     send(u, 4, rows(u, *yn, half=1), rows(u, *yn, half=1), xn),
                     send(u, 5, rows(u, *xn), rows(u, *xn), sib),
                     send(u, 6, rows(u, *yn), rows(u, *yn), sib)]
            for cp in relay:
                cp.start()
            sent += relay
        for u in range(len(units)):
            arrived(u, 3, rows(u, *diag, half=0), yn)
            arrived(u, 4, rows(u, *diag, half=1), xn)
            last = send(u, 7, rows(u, *diag), rows(u, *diag), sib)
            last.start()
            sent.append(last)
        for u in range(len(units)):
            arrived(u, 0, rows(u, *sib), sib)
            arrived(u, 5, rows(u, 1 - x, y, 1 - c), sib)
            arrived(u, 6, rows(u, x, 1 - y, 1 - c), sib)
            arrived(u, 7, rows(u, 1 - x, 1 - y, 1 - c), sib)
        for cp in sent:
            cp.wait_send()
        for cp in own:
            cp.wait()

    n_units = len(units)
    return list(pl.pallas_call(
        body, name=name, in_specs=[ANY] * n, out_specs=[ANY] * n,
        out_shape=[jax.ShapeDtypeStruct((a.shape[0], N_DEV * a.shape[1], a.shape[2]), a.dtype) for a in arrs],
        scratch_shapes=[pltpu.SemaphoreType.DMA((n_pairs * n_units,)), pltpu.SemaphoreType.DMA((n_pairs * n_units,)),
                        pltpu.SemaphoreType.DMA((n_units,))],
    )(*arrs))


def _pair_sums(grads, recvs, cidx, name):
    n = len(grads)

    def body(c_ref, *refs):
        for g_ref, r_ref, o_ref in zip(refs[:n], refs[n:2 * n], refs[2 * n:]):
            o_ref[...] = (g_ref[...].astype(F32) + r_ref[...].astype(F32)).astype(BF16)

    shapes = [(g.shape[0] // N_DEV, g.shape[1]) for g in grads]
    return list(pl.pallas_call(
        body, name=name,
        grid_spec=pltpu.PrefetchScalarGridSpec(
            num_scalar_prefetch=1, grid=(N_CHIP,),
            in_specs=[pl.BlockSpec(s, lambda q, c: (2 * q + c[0], 0)) for s in shapes]
            + [pl.BlockSpec(s, lambda q, c: (q, 0)) for s in shapes],
            out_specs=[pl.BlockSpec(s, lambda q, c: (q, 0)) for s in shapes]),
        out_shape=[jax.ShapeDtypeStruct((N_CHIP * r, cols), BF16) for r, cols in shapes],
        compiler_params=_cparams(("parallel",)),
    )(cidx, *grads, *recvs))


def _sum_blocks(parts, nblk, name):
    r = parts.shape[0] // nblk
    cols = parts.shape[1]

    def body(p_ref, o_ref):
        acc = p_ref[0:r, :].astype(F32)
        for q in range(1, nblk):
            acc = acc + p_ref[q * r:(q + 1) * r, :].astype(F32)
        o_ref[...] = acc

    return pl.pallas_call(
        body, name=name,
        out_shape=jax.ShapeDtypeStruct((r, cols), F32),
        compiler_params=_cparams(),
    )(parts)


def _adamw_math(w, g, m, v):
    m = ADAM_B1 * m + (1.0 - ADAM_B1) * g
    v = ADAM_B2 * v + (1.0 - ADAM_B2) * (g * g)
    m_hat = m / (1.0 - ADAM_B1 ** ADAM_STEP)
    v_hat = v / (1.0 - ADAM_B2 ** ADAM_STEP)
    delta = -ADAM_LR * (m_hat / (jnp.sqrt(v_hat) + ADAM_EPS) + ADAM_WD * w)
    return delta, m, v


def _finish_sharded(parts, w, m, v, name):
    depth, rr, cw = w.shape

    def body(*refs):
        p_refs = refs[:depth]
        w_ref, m_ref, v_ref, g_ref, d_ref, mo_ref, vo_ref = refs[depth:]
        l = pl.program_id(0)
        for k in range(depth):
            @pl.when(l == k)
            def _(p_ref=p_refs[k]):
                r = p_ref.shape[0] // N_CHIP
                acc = p_ref[0:r, :].astype(F32)
                for q in range(1, N_CHIP):
                    acc = acc + p_ref[q * r:(q + 1) * r, :].astype(F32)
                g_ref[...] = acc
                d_ref[...], mo_ref[...], vo_ref[...] = _adamw_math(w_ref[...], acc, m_ref[...], v_ref[...])

    blk = pl.BlockSpec((None, rr, cw), lambda l: (l, 0, 0))
    return pl.pallas_call(
        body, name=name, grid=(depth,),
        in_specs=[pl.BlockSpec(p.shape, lambda l: (0, 0)) for p in parts] + [blk] * 3, out_specs=[blk] * 4,
        out_shape=[jax.ShapeDtypeStruct(w.shape, F32)] * 4,
        compiler_params=_cparams(("arbitrary",)),
    )(*parts, w, m, v)


def _adamw_small(ws, gs, ms, vs, name):
    n = len(ws)

    def body(*refs):
        for k in range(n):
            w_ref, g_ref, m_ref, v_ref = (refs[j * n + k] for j in range(4))
            d_ref, mo_ref, vo_ref = (refs[(4 + j) * n + k] for j in range(3))
            d_ref[...], mo_ref[...], vo_ref[...] = _adamw_math(w_ref[...], g_ref[...], m_ref[...], v_ref[...])

    shapes = [jax.ShapeDtypeStruct(w.shape, F32) for w in ws]
    return pl.pallas_call(
        body, name=name, out_shape=shapes * 3, compiler_params=_cparams(),
    )(*ws, *gs, *ms, *vs)


def kernel(x, ffn1_norm, ffn1_w_gate, ffn1_w_up, ffn1_w_down, mix_norm, w_in, sgu_ln_g, sgu_ln_b, w_spatial, b_spatial, conv_w, conv_b, conv_ln_g, conv_ln_b, pool_w, pool_scale, w_out, ffn2_norm, ffn2_w_gate, ffn2_w_up, ffn2_w_down, final_norm, loss_target, m_ffn1_norm, m_ffn1_w_gate, m_ffn1_w_up, m_ffn1_w_down, m_mix_norm, m_w_in, m_sgu_ln_g, m_sgu_ln_b, m_w_spatial, m_b_spatial, m_conv_w, m_conv_b, m_conv_ln_g, m_conv_ln_b, m_pool_w, m_pool_scale, m_w_out, m_ffn2_norm, m_ffn2_w_gate, m_ffn2_w_up, m_ffn2_w_down, m_final_norm, v_ffn1_norm, v_ffn1_w_gate, v_ffn1_w_up, v_ffn1_w_down, v_mix_norm, v_w_in, v_sgu_ln_g, v_sgu_ln_b, v_w_spatial, v_b_spatial, v_conv_w, v_conv_b, v_conv_ln_g, v_conv_ln_b, v_pool_w, v_pool_scale, v_w_out, v_ffn2_norm, v_ffn2_w_gate, v_ffn2_w_up, v_ffn2_w_down, v_final_norm):
    names = ["ffn1_norm", "ffn1_w_gate", "ffn1_w_up", "ffn1_w_down", "mix_norm", "w_in", "sgu_ln_g", "sgu_ln_b",
             "w_spatial", "b_spatial", "conv_w", "conv_b", "conv_ln_g", "conv_ln_b", "pool_w", "pool_scale",
             "w_out", "ffn2_norm", "ffn2_w_gate", "ffn2_w_up", "ffn2_w_down", "final_norm"]
    W = dict(zip(names, [ffn1_norm, ffn1_w_gate, ffn1_w_up, ffn1_w_down, mix_norm, w_in, sgu_ln_g, sgu_ln_b,
                         w_spatial, b_spatial, conv_w, conv_b, conv_ln_g, conv_ln_b, pool_w, pool_scale, w_out,
                         ffn2_norm, ffn2_w_gate, ffn2_w_up, ffn2_w_down, final_norm]))
    M = dict(zip(names, [m_ffn1_norm, m_ffn1_w_gate, m_ffn1_w_up, m_ffn1_w_down, m_mix_norm, m_w_in, m_sgu_ln_g,
                         m_sgu_ln_b, m_w_spatial, m_b_spatial, m_conv_w, m_conv_b, m_conv_ln_g, m_conv_ln_b,
                         m_pool_w, m_pool_scale, m_w_out, m_ffn2_norm, m_ffn2_w_gate, m_ffn2_w_up, m_ffn2_w_down,
                         m_final_norm]))
    V = dict(zip(names, [v_ffn1_norm, v_ffn1_w_gate, v_ffn1_w_up, v_ffn1_w_down, v_mix_norm, v_w_in, v_sgu_ln_g,
                         v_sgu_ln_b, v_w_spatial, v_b_spatial, v_conv_w, v_conv_b, v_conv_ln_g, v_conv_ln_b,
                         v_pool_w, v_pool_scale, v_w_out, v_ffn2_norm, v_ffn2_w_gate, v_ffn2_w_up, v_ffn2_w_down,
                         v_final_norm]))

    depth, d = ffn1_norm.shape
    t = x.shape[1]
    sgu = sgu_ln_g.shape[1]
    pool = pool_scale.shape[1]
    n_head = sgu // HEAD_DIM
    cw_shard = conv_w.shape[2]
    xs = x.reshape(t, d)
    target = loss_target.reshape(t, d)

    def tr(w):
        return jnp.swapaxes(w, 1, 2).astype(BF16)

    ffn_shards = [[jnp.stack([tr(ffn1_w_gate)[l], tr(ffn1_w_up)[l], ffn1_w_down[l].astype(BF16)]),
                   jnp.stack([tr(ffn2_w_gate)[l], tr(ffn2_w_up)[l], ffn2_w_down[l].astype(BF16)])]
                  for l in range(depth)]
    win_shards = [tr(w_in)[l:l + 1] for l in range(depth)]
    wout_shards = [w_out[l:l + 1].astype(BF16) for l in range(depth)]
    cw_rows = depth * CONV_WIDTH
    cw_pad = -cw_rows % 8
    cw_send = jnp.pad(conv_w.reshape(cw_rows, cw_shard), ((0, cw_pad), (0, 0)))[None]
    wffn, wb, wc = {}, {}, {}
    wffn[(0, 0)], wb[0], wc[0], cwg = _all_gather_relayed(
        [ffn_shards[0][0], win_shards[0], wout_shards[0], cw_send], "ag_first")
    conv_w_full = cwg.reshape(N_DEV, cw_rows + cw_pad, cw_shard)[:, :cw_rows].reshape(
        N_DEV, depth, CONV_WIDTH, cw_shard).transpose(1, 2, 0, 3).reshape(depth, CONV_WIDTH, N_DEV * cw_shard)

    def mixer_params(l):
        return dict(
            sgu_ln_g=sgu_ln_g[l:l + 1], sgu_ln_b=sgu_ln_b[l:l + 1], w_spatial=w_spatial[l],
            bs_full=jnp.repeat(b_spatial[l].T, HEAD_DIM, axis=1),
            conv_w=conv_w_full[l], conv_b=conv_b[l:l + 1], conv_ln_g=conv_ln_g[l:l + 1],
            conv_ln_b=conv_ln_b[l:l + 1], bd=_block_diag(pool_w[l]), pool_scale=pool_scale[l:l + 1])

    saved = []
    cur = xs
    for l in range(depth):
        p = mixer_params(l)
        x0 = cur
        more = l + 1 < depth
        (x1, gate1, up1, act1), part = _ffn_fwd(x0, ffn1_norm[l:l + 1], wffn[(l, 0)], f"ffn1_fwd_{l}",
                                          _GatherIci([ffn_shards[l][1]]))
        riding = [_GatherForward(part)] + ([_GatherIci([win_shards[l + 1], wout_shards[l + 1]])] if more else [])
        (x2, z, hc, cat), part = _mixer_fwd(x1, mix_norm[l:l + 1], wb[l], wc[l], p, f"mixer_fwd_{l}", _Merged(riding))
        wffn[(l, 1)] = part[0]
        riding = [_GatherIci([ffn_shards[l + 1][0]]), _GatherForward(part[1:])] if more else []
        relay = _GatherForward(riding[0].out_shapes) if more else None
        outs, part = _ffn_fwd(x2, ffn2_norm[l:l + 1], wffn[(l, 1)], f"ffn2_fwd_{l}",
                              _Merged(riding) if more else None,
                              None if more else (final_norm.reshape(1, d), target), relay)
        if more:
            cur, gate2, up2, act2 = outs
            wffn[(l + 1, 0)], wb[l + 1], wc[l + 1] = part
        else:
            dx, gate2, up2, act2, d_final, loss_part = outs
        saved.append((p, x0, gate1, up1, act1, x1, z, hc, cat, x2, gate2, up2, act2))

    cidx = lax.axis_index("c").astype(jnp.int32).reshape(1)
    from_chips = {}
    to_pair, to_chip = [], []
    small = []

    def pair_payload():
        return _PairExchange([g for _, g in to_pair]) if to_pair else None

    def pair_done(received):
        if to_pair:
            (nm, l), _ = to_pair[0]
            sums = _pair_sums([g for _, g in to_pair], list(received), cidx, f"rs_pair_sum_{nm}_{l}")
            to_chip.extend((key, s) for (key, _), s in zip(to_pair, sums))
        to_pair.clear()

    def take_chip():
        items = list(to_chip)
        to_chip.clear()
        return items

    def chip_payload(items):
        return _ChipExchange([s for _, s in items]) if items else None

    def chip_done(items, landed):
        for (key, _), o in zip(items, landed):
            from_chips[key] = o

    def ffn_weight_grads(prefix, l, dgate, dup, act, h, dy):
        items = take_chip()
        items, later = items[:2], items[2:]
        to_chip.extend(later)
        grads3, landed = _ffn_dw(dgate, dup, act, h, dy, f"dw_{prefix}_{l}", chip_payload(items))
        chip_done(items, landed)
        to_pair.extend(((f"{prefix}_{nm}", l), g) for nm, g in zip(("w_gate", "w_up", "w_down"), grads3))

    for l in reversed(range(depth)):
        p, x0, gate1, up1, act1, x1, z, hc, cat, x2, gate2, up2, act2 = saved[l]
        (dx, dgate, dup, h, dy, dg_ffn2), received = _ffn_bwd(
            x2, ffn2_norm[l:l + 1], dx, gate2, up2, wffn[(l, 1)], f"ffn2_bwd_{l}", pair_payload())
        pair_done(received)
        ffn_weight_grads("ffn2", l, dgate, dup, act2, h, dy)
        g_out, received = _tn_matmul(cat, dx, f"dw_w_out_{l}", pair_payload())
        pair_done(received)
        items = take_chip()
        items, later = items[:3], items[3:]
        to_chip.extend(later)
        (dx, dz, hm, dg_mix, g384, gws, gpool), landed = _mixer_bwd(
            x1, mix_norm[l:l + 1], z, hc, dx, wb[l], wc[l], p, f"mixer_bwd_{l}", chip_payload(items))
        chip_done(items, landed)
        to_pair.append((("w_out", l), g_out))
        g_in, received = _tn_matmul(dz, hm, f"dw_w_in_{l}", pair_payload())
        pair_done(received)
        to_pair.append((("w_in", l), g_in))
        if l > 0:
            (dx, dgate, dup, h, dy, dg_ffn1), received = _ffn_bwd(
                x0, ffn1_norm[l:l + 1], dx, gate1, up1, wffn[(l, 0)], f"ffn1_bwd_{l}", pair_payload())
            pair_done(received)
            ffn_weight_grads("ffn1", l, dgate, dup, act1, h, dy)
            small.append((l, g384, gws, gpool, dg_ffn1, dg_mix, dg_ffn2))
            continue

        small.append((0, g384, gws, gpool, None, dg_mix, dg_ffn2))
        small.sort(key=lambda s: s[0])
        norm_rows = []
        for (sl, _, _, _, dg1, dgm, dg2) in small:
            norm_rows += [jnp.zeros((1, d), F32) if dg1 is None else dg1, dgm, dg2]
        norm_rows += [d_final, jnp.pad(loss_part, ((0, 0), (0, d - LANES)))]
        n_norm = len(norm_rows)
        norm_pack = jnp.concatenate(norm_rows + [jnp.zeros((8 - n_norm % 8, d), F32)] * (n_norm % 8 != 0), axis=0)
        parts = [norm_pack]
        for (_, s384, sws, spool, _, _, _) in small:
            parts += [s384, sws.reshape((n_head + 1) * CHUNK, CHUNK), spool]
        n_pair = len(to_pair)
        early = take_chip()
        riding = [pair_payload(), _GatherIci([a[None] for a in parts])] + ([chip_payload(early)] if early else [])
        (dx, dgate, dup, h, dy, dg_ffn1), landed = _ffn_bwd(
            x0, ffn1_norm[l:l + 1], dx, gate1, up1, wffn[(l, 0)], f"ffn1_bwd_{l}", _Merged(riding))
        pair_done(landed[:n_pair])
        chip_done(early, landed[n_pair + len(parts):])
        items = take_chip()
        g_gate, landed = _tn_matmul(
            dgate, h, f"dw_ffn1_w_gate_{l}",
            _Merged([chip_payload(items), _GatherForward(landed[n_pair:n_pair + len(parts)])]))
        chip_done(items, landed[:len(items)])
        gathered = landed[len(items):]
        to_pair.append((("ffn1_w_gate", l), g_gate))
        g_up, received = _tn_matmul(dup, h, f"dw_ffn1_w_up_{l}", pair_payload())
        pair_done(received)
        to_pair.append((("ffn1_w_up", l), g_up))
        items = take_chip()
        g_down, landed = _tn_matmul(act1, dy, f"dw_ffn1_w_down_{l}", _Merged([chip_payload(items), pair_payload()]))
        chip_done(items, landed[:len(items)])
        pair_done(landed[len(items):])
        to_pair.append((("ffn1_w_down", l), g_down))
    grad_x = dx.reshape(x.shape)
    pair_done(_comm(pair_payload(), "rs_pair_exchange_last"))
    items = take_chip()
    (late_norm,), landed, summed = _all_gather([jnp.pad(dg_ffn1, ((0, 7), (0, 0)))[None]], "ag_tail",
                                               chip_payload(items), [g[0] for g in gathered])
    chip_done(items, landed)
    late_sum = _sum_blocks(late_norm[0], N_DEV, "sum_small_late")
    norm_sum = summed[0]
    loss = norm_sum[3 * depth + 1, 0]
    cpos = lax.axis_index("x") * 4 + lax.axis_index("y") * 2 + lax.axis_index("c")
    sg = {nm: [] for nm in names}
    for l in range(depth):
        g384, gws, gpool = summed[1 + 3 * l], summed[2 + 3 * l].reshape(n_head + 1, CHUNK, CHUNK), summed[3 + 3 * l]
        sg["ffn1_norm"].append(norm_sum[3 * l] if l > 0 else late_sum[0])
        sg["mix_norm"].append(norm_sum[3 * l + 1])
        sg["ffn2_norm"].append(norm_sum[3 * l + 2])
        sg["sgu_ln_g"].append(g384[_R_SGU_G])
        sg["sgu_ln_b"].append(g384[_R_SGU_B])
        sg["conv_b"].append(g384[_R_CONV_B])
        sg["conv_ln_g"].append(g384[_R_CLN_G])
        sg["conv_ln_b"].append(g384[_R_CLN_B])
        sg["conv_w"].append(lax.dynamic_slice_in_dim(g384[_R_CONV_W:_R_CONV_W + CONV_WIDTH], cpos * cw_shard,
                                                     cw_shard, axis=1))
        sg["w_spatial"].append(gws[:n_head])
        sg["b_spatial"].append(gws[n_head][:, :n_head].T)
        sg["pool_w"].append(jnp.stack([gpool[k * HEAD_DIM:(k + 1) * HEAD_DIM, k * HEAD_DIM:(k + 1) * HEAD_DIM]
                                       for k in range(pool // HEAD_DIM)], axis=0))
        sg["pool_scale"].append(gpool[pool])
    small_names = ["ffn1_norm", "mix_norm", "sgu_ln_g", "sgu_ln_b", "w_spatial", "b_spatial", "conv_w", "conv_b",
                   "conv_ln_g", "conv_ln_b", "pool_w", "pool_scale", "ffn2_norm"]
    grads = {nm: jnp.stack(sg[nm], axis=0) for nm in small_names}
    grads["final_norm"] = norm_sum[3 * depth]

    delta, new_m, new_v = {}, {}, {}
    big_names = ["ffn1_w_gate", "ffn1_w_up", "ffn1_w_down", "w_in", "w_out", "ffn2_w_gate", "ffn2_w_up",
                 "ffn2_w_down"]
    transposed = {"ffn1_w_gate", "ffn1_w_up", "w_in", "ffn2_w_gate", "ffn2_w_up"}
    for nm in big_names:
        view = (lambda a: jnp.swapaxes(a, 1, 2)) if nm in transposed else (lambda a: a)
        outs = _finish_sharded([from_chips[(nm, l)] for l in range(depth)], view(W[nm]), view(M[nm]), view(V[nm]),
                               f"adamw_{nm}")
        grads[nm], delta[nm], new_m[nm], new_v[nm] = (view(o) for o in outs)
    snames = small_names + ["final_norm"]

    def flat2(a):
        return a.reshape(-1, a.shape[-1])

    outs = _adamw_small([flat2(W[nm]) for nm in snames], [flat2(grads[nm]) for nm in snames],
                        [flat2(M[nm]) for nm in snames], [flat2(V[nm]) for nm in snames], "adamw_small")
    ns = len(snames)
    for k, nm in enumerate(snames):
        shp = W[nm].shape
        delta[nm], new_m[nm], new_v[nm] = (outs[k].reshape(shp), outs[ns + k].reshape(shp),
                                           outs[2 * ns + k].reshape(shp))

    return (loss, grad_x, *[grads[nm] for nm in names], *[delta[nm] for nm in names],
            *[new_m[nm] for nm in names], *[new_v[nm] for nm in names])
```

```python
import functools

import jax
import jax.numpy as jnp
from jax import lax
from jax.experimental import pallas as pl
from jax.experimental.pallas import tpu as pltpu

F32 = jnp.float32
BF16 = jnp.bfloat16
EPS = 1e-6
N_DEV = 8
N_CHIP = 4
MESH = pl.DeviceIdType.MESH
ANY = pl.BlockSpec(memory_space=pl.ANY)

VMEM_LIMIT_BYTES = 56 * 1024 * 1024
LANES = 128
HALO = 32
HEAD_DIM = 64
CHUNK = 128
CONV_WIDTH = 31
POOL_WINDOWS = (2, 4, 8, 16)

ADAM_LR = 0.001
ADAM_B1 = 0.9
ADAM_B2 = 0.999
ADAM_EPS = 1e-08
ADAM_WD = 0.01
ADAM_STEP = 10


def _cparams(sem=None):
    return pltpu.CompilerParams(dimension_semantics=sem, vmem_limit_bytes=VMEM_LIMIT_BYTES)


def _position():
    return lax.axis_index("x"), lax.axis_index("y"), lax.axis_index("c")


class _Copies:
    def __init__(self):
        self.local, self.sends, self.recvs = [], [], []

    def extend(self, other):
        self.local += other.local
        self.sends += other.sends
        self.recvs += other.recvs

    def start(self):
        for cp in self.local + self.sends:
            cp.start()

    def wait(self):
        for land, send_sems, recv_sems, k, peer in self.recvs:
            _remote(land, land, send_sems, recv_sems, k, peer).wait_recv()
        for cp in self.sends:
            cp.wait_send()
        for cp in self.local:
            cp.wait()


def _remote(src, dst, send_sems, recv_sems, k, to):
    return pltpu.make_async_remote_copy(src_ref=src, dst_ref=dst, send_sem=send_sems.at[k], recv_sem=recv_sems.at[k],
                                        device_id=to, device_id_type=MESH)


class _Payload:
    ins, out_shapes, aliases, n_remote, n_local = (), (), {}, 0, 0

    def sem_shapes(self):
        return [pltpu.SemaphoreType.DMA((max(self.n_remote, 1),)), pltpu.SemaphoreType.DMA((max(self.n_remote, 1),)),
                pltpu.SemaphoreType.DMA((max(self.n_local, 1),))]


class _GatherIci(_Payload):
    def __init__(self, shards):
        self.ins = list(shards)
        self.out_shapes = [jax.ShapeDtypeStruct((s.shape[0], N_DEV * s.shape[1], s.shape[2]), s.dtype) for s in shards]
        self.n_remote, self.n_local = 4 * len(shards), len(shards)

    def build(self, ins, outs, send_sems, recv_sems, local_sems, k0=0, l0=0):
        x, y, c = _position()
        peers = [(x, y, 1 - c), (1 - x, y, c), (x, 1 - y, c), (1 - x, 1 - y, c)]
        cps = _Copies()
        for a, (src, out) in enumerate(zip(ins, outs)):
            r = src.shape[1]

            def rows(px, py, pc, out=out, r=r):
                return out.at[:, pl.ds((4 * px + 2 * py + pc) * r, r), :]

            cps.local.append(pltpu.make_async_copy(src, rows(x, y, c), local_sems.at[l0 + a]))
            for k, peer in enumerate(peers):
                cps.sends.append(_remote(src, rows(x, y, c), send_sems, recv_sems, k0 + 4 * a + k, peer))
                cps.recvs.append((rows(*peer), send_sems, recv_sems, k0 + 4 * a + k, peer))
        return cps


class _GatherForward(_Payload):
    def __init__(self, partials):
        self.ins = list(partials)
        self.out_shapes = [jax.ShapeDtypeStruct(p.shape, p.dtype) for p in partials]
        self.aliases = {a: a for a in range(len(partials))}
        self.n_remote = 3 * len(partials)

    def build(self, ins, outs, send_sems, recv_sems, local_sems, k0=0, l0=0):
        x, y, c = _position()
        chips = [(1 - x, y), (x, 1 - y), (1 - x, 1 - y)]
        cps = _Copies()
        for a, out in enumerate(outs):
            r = out.shape[1] // N_DEV
            for k, (px, py) in enumerate(chips):
                mine = out.at[:, pl.ds((4 * px + 2 * py + c) * r, r), :]
                theirs = out.at[:, pl.ds((4 * px + 2 * py + 1 - c) * r, r), :]
                cps.sends.append(_remote(mine, mine, send_sems, recv_sems, k0 + 3 * a + k, (x, y, 1 - c)))
                cps.recvs.append((theirs, send_sems, recv_sems, k0 + 3 * a + k, (x, y, 1 - c)))
        return cps


class _PairExchange(_Payload):
    def __init__(self, grads):
        self.ins = list(grads)
        self.out_shapes = [jax.ShapeDtypeStruct((g.shape[0] // 2, g.shape[1]), g.dtype) for g in grads]
        self.n_remote = N_CHIP * len(grads)

    def build(self, ins, outs, send_sems, recv_sems, local_sems, k0=0, l0=0):
        x, y, c = _position()
        cps = _Copies()
        for a, (src, out) in enumerate(zip(ins, outs)):
            r = src.shape[0] // N_DEV
            for q in range(N_CHIP):
                land = out.at[pl.ds(q * r, r), :]
                cps.sends.append(_remote(src.at[pl.ds((2 * q + 1 - c) * r, r), :], land, send_sems, recv_sems,
                                         k0 + N_CHIP * a + q, (x, y, 1 - c)))
                cps.recvs.append((land, send_sems, recv_sems, k0 + N_CHIP * a + q, (x, y, 1 - c)))
        return cps


class _ChipExchange(_Payload):
    def __init__(self, sums):
        self.ins = list(sums)
        self.out_shapes = [jax.ShapeDtypeStruct(s.shape, s.dtype) for s in sums]
        self.n_remote, self.n_local = 3 * len(sums), len(sums)

    def build(self, ins, outs, send_sems, recv_sems, local_sems, k0=0, l0=0):
        x, y, c = _position()
        my_chip = 2 * x + y
        chips = [(1 - x, y), (x, 1 - y), (1 - x, 1 - y)]
        cps = _Copies()
        for a, (src, out) in enumerate(zip(ins, outs)):
            r = src.shape[0] // N_CHIP
            mine = out.at[pl.ds(my_chip * r, r), :]
            cps.local.append(pltpu.make_async_copy(src.at[pl.ds(my_chip * r, r), :], mine, local_sems.at[l0 + a]))
            for k, (px, py) in enumerate(chips):
                land = out.at[pl.ds((2 * px + py) * r, r), :]
                cps.sends.append(_remote(src.at[pl.ds((2 * px + py) * r, r), :], mine, send_sems, recv_sems,
                                         k0 + 3 * a + k, (px, py, c)))
                cps.recvs.append((land, send_sems, recv_sems, k0 + 3 * a + k, (px, py, c)))
        return cps


class _Merged(_Payload):
    def __init__(self, parts):
        self.parts = list(parts)
        self.ins = [a for p in parts for a in p.ins]
        self.out_shapes = [s for p in parts for s in p.out_shapes]
        self.aliases, self.offsets = {}, []
        i0 = o0 = k0 = l0 = 0
        for p in parts:
            self.offsets.append((i0, o0, k0, l0))
            self.aliases.update({i0 + i: o0 + o for i, o in p.aliases.items()})
            i0, o0, k0, l0 = i0 + len(p.ins), o0 + len(p.out_shapes), k0 + p.n_remote, l0 + p.n_local
        self.n_remote, self.n_local = k0, l0

    def build(self, ins, outs, send_sems, recv_sems, local_sems):
        cps = _Copies()
        for p, (i0, o0, k0, l0) in zip(self.parts, self.offsets):
            cps.extend(p.build(ins[i0:i0 + len(p.ins)], outs[o0:o0 + len(p.out_shapes)], send_sems, recv_sems,
                               local_sems, k0, l0))
        return cps


RELAY_AT = 0.92


def _call(body, name, grid, in_specs, out_specs, out_shape, scratch_shapes, semantics, args, payload=None,
          relay=None):
    if payload is None:
        outs = pl.pallas_call(body, name=name, grid=grid, in_specs=in_specs, out_specs=out_specs,
                              out_shape=out_shape, scratch_shapes=scratch_shapes,
                              compiler_params=_cparams(semantics))(*args)
        return list(outs), []
    n_in, n_out, n_scr = len(in_specs), len(out_specs), len(scratch_shapes)
    p_in, p_out = len(payload.ins), len(payload.out_shapes)

    def carried(*refs):
        ins, p_ins = refs[:n_in], refs[n_in:n_in + p_in]
        o0 = n_in + p_in
        outs, p_outs = refs[o0:o0 + n_out], refs[o0 + n_out:o0 + n_out + p_out]
        s0 = o0 + n_out + p_out
        scr, sems = refs[s0:s0 + n_scr], refs[s0 + n_scr:s0 + n_scr + 3]
        relay_sems = refs[s0 + n_scr + 3:]
        ids = [pl.program_id(k) for k in range(len(grid))]
        at_first = functools.reduce(jnp.logical_and, [i == 0 for i in ids])
        at_last = functools.reduce(jnp.logical_and, [i == g - 1 for i, g in zip(ids, grid)])

        @pl.when(at_first)
        def _():
            payload.build(p_ins, p_outs, *sems).start()

        body(*ins, *outs, *scr)

        if relay is None:
            @pl.when(at_last)
            def _():
                payload.build(p_ins, p_outs, *sems).wait()
        else:
            n_relay = len(relay.out_shapes)
            step = functools.reduce(lambda acc, ig: acc * ig[1] + ig[0], zip(ids, grid), 0)
            total = functools.reduce(lambda a, b: a * b, grid)

            @pl.when(step == int(RELAY_AT * total))
            def _():
                payload.build(p_ins, p_outs, *sems).wait()
                relay.build(p_outs[:n_relay], p_outs[:n_relay], *relay_sems).start()

            @pl.when(at_last)
            def _():
                relay.build(p_outs[:n_relay], p_outs[:n_relay], *relay_sems).wait()

    outs = pl.pallas_call(
        carried, name=name, grid=grid, in_specs=list(in_specs) + [ANY] * p_in,
        out_specs=list(out_specs) + [ANY] * p_out, out_shape=list(out_shape) + list(payload.out_shapes),
        scratch_shapes=list(scratch_shapes) + payload.sem_shapes() + (relay.sem_shapes() if relay else []),
        input_output_aliases={n_in + i: n_out + o for i, o in payload.aliases.items()},
        compiler_params=_cparams(("arbitrary",) * len(grid)))(*args, *payload.ins)
    return list(outs[:n_out]), list(outs[n_out:])


def _comm(payload, name):
    def body(*refs):
        p_in, p_out = len(payload.ins), len(payload.out_shapes)
        cps = payload.build(refs[:p_in], refs[p_in:p_in + p_out], *refs[p_in + p_out:])
        cps.start()
        cps.wait()

    return list(pl.pallas_call(
        body, name=name, in_specs=[ANY] * len(payload.ins), out_specs=[ANY] * len(payload.out_shapes),
        out_shape=list(payload.out_shapes), scratch_shapes=payload.sem_shapes(),
        input_output_aliases=dict(payload.aliases))(*payload.ins))


def _dot(a, b):
    return jnp.dot(a, b, preferred_element_type=F32)


def _dot_nt(a, b):
    return lax.dot_general(a, b, (((1,), (1,)), ((), ())), preferred_element_type=F32)


def _dot_tn(a, b):
    return lax.dot_general(a, b, (((0,), (0,)), ((), ())), preferred_element_type=F32)


def _split_dot(x, e):
    hi = x.astype(BF16)
    r1 = x - hi.astype(F32)
    mid = r1.astype(BF16)
    lo = (r1 - mid.astype(F32)).astype(BF16)
    return _dot(hi, e) + _dot(mid, e) + _dot(lo, e)


def _rms(x):
    rstd = lax.rsqrt(jnp.mean(x * x, axis=-1, keepdims=True) + EPS)
    return x * rstd, rstd


def _rms_bwd(xhat, rstd, g, dh):
    dxhat = dh * g
    dx = rstd * (dxhat - xhat * jnp.mean(dxhat * xhat, axis=-1, keepdims=True))
    return dx, jnp.sum(dh * xhat, axis=0, keepdims=True)


def _ln(v):
    mu = jnp.mean(v, axis=-1, keepdims=True)
    xc = v - mu
    rstd = lax.rsqrt(jnp.mean(xc * xc, axis=-1, keepdims=True) + EPS)
    return xc * rstd, rstd


def _ln_bwd(vhat, rstd, g, dy):
    dvhat = dy * g
    dv = rstd * (dvhat - jnp.mean(dvhat, axis=-1, keepdims=True)
                 - vhat * jnp.mean(dvhat * vhat, axis=-1, keepdims=True))
    return dv, jnp.sum(dy * vhat, axis=0, keepdims=True), jnp.sum(dy, axis=0, keepdims=True)


_INV_SQRT2 = 0.7071067811865476
_INV_SQRT2PI = 0.3989422804014327


def _gelu(x):
    return 0.5 * x * (1.0 + lax.erf(x * _INV_SQRT2))


def _gelu_grad(x):
    return 0.5 * (1.0 + lax.erf(x * _INV_SQRT2)) + x * jnp.exp(-0.5 * x * x) * _INV_SQRT2PI


def _silu_grad(x):
    s = jax.nn.sigmoid(x)
    return s * (1.0 + x * (1.0 - s))


def _ffn_fwd(x, g, wa, name, payload=None, head=None, relay=None):
    t, d = x.shape
    f = wa.shape[1]
    tm, tf = 1024, 256
    nc = f // tf
    groups = [slice(k * (tm // 2), (k + 1) * (tm // 2)) for k in range(2)]

    def body(x_ref, g_ref, wgu_ref, wd_ref, *rest):
        if head is None:
            xo_ref, gate_ref, up_ref, act_ref, h_scr, acc_scr = rest
        else:
            fg_ref, tgt_ref, xo_ref, gate_ref, up_ref, act_ref, dfg_ref, loss_ref, h_scr, acc_scr = rest
        c = pl.program_id(1)
        if head is not None:
            @pl.when((c == 0) & (pl.program_id(0) == 0))
            def _():
                dfg_ref[...] = jnp.zeros_like(dfg_ref)
                loss_ref[...] = jnp.zeros_like(loss_ref)

        @pl.when(c == 0)
        def _():
            xhat, _ = _rms(x_ref[...])
            h_scr[...] = (xhat * g_ref[...]).astype(BF16)
            acc_scr[...] = jnp.zeros_like(acc_scr)

        wgu, wd = wgu_ref[...].reshape(2 * tf, d), wd_ref[...]
        for rows in groups:
            gu = _dot_nt(h_scr[rows, :], wgu)
            gate, up = gu[:, :tf], gu[:, tf:]
            gate_ref[rows, :] = gate.astype(BF16)
            up_ref[rows, :] = up.astype(BF16)
            act = (gate * jax.nn.sigmoid(gate) * up).astype(BF16)
            act_ref[rows, :] = act
            acc_scr[rows, :] += _dot(act, wd)

        @pl.when(c == nc - 1)
        def _():
            xo = x_ref[...] + 0.5 * acc_scr[...]
            if head is None:
                xo_ref[...] = xo
            else:
                fg = fg_ref[...]
                xhat, rstd = _rms(xo)
                err = xhat * fg - tgt_ref[...]
                dxn, dfg = _rms_bwd(xhat, rstd, fg, err * (1.0 / d))
                xo_ref[...] = dxn
                dfg_ref[...] += dfg
                loss_ref[...] += jnp.zeros_like(loss_ref) + 0.5 * jnp.sum(jnp.mean(err * err, axis=-1, keepdims=True))

    row = pl.BlockSpec((tm, d), lambda i, c: (i, 0))
    vec = pl.BlockSpec((1, d), lambda i, c: (0, 0))
    in_specs = [row, vec, pl.BlockSpec((2, tf, d), lambda i, c: (0, c, 0)),
                pl.BlockSpec((None, tf, d), lambda i, c: (2, c, 0))]
    out_specs = [row] + [pl.BlockSpec((tm, tf), lambda i, c: (i, c))] * 3
    out_shape = [jax.ShapeDtypeStruct((t, d), F32)] + [jax.ShapeDtypeStruct((t, f), BF16)] * 3
    args = (x, g, wa, wa)
    if head is not None:
        in_specs += [vec, row]
        out_specs += [vec, pl.BlockSpec((1, LANES), lambda i, c: (0, 0))]
        out_shape += [jax.ShapeDtypeStruct((1, d), F32), jax.ShapeDtypeStruct((1, LANES), F32)]
        args += tuple(head)
    return _call(
        body, name, (t // tm, nc), in_specs, out_specs, out_shape,
        [pltpu.VMEM((tm, d), BF16), pltpu.VMEM((tm, d), F32)],
        ("parallel" if head is None else "arbitrary", "arbitrary"), args, payload, relay)


def _ffn_bwd(x, g, dxo, gate, up, wa, name, payload=None):
    t, d = x.shape
    f = wa.shape[1]
    tm, tf = 1024, 256
    nc = f // tf
    groups = [slice(k * (tm // 2), (k + 1) * (tm // 2)) for k in range(2)]

    def body(x_ref, g_ref, dxo_ref, gate_ref, up_ref, wgu_ref, wd_ref,
             dx_ref, dgate_ref, dup_ref, h_ref, dy_ref, dg_ref, acc_scr):
        i, c = pl.program_id(0), pl.program_id(1)

        @pl.when(c == 0)
        def _():
            xhat, _ = _rms(x_ref[...])
            h_ref[...] = (xhat * g_ref[...]).astype(BF16)
            dy_ref[...] = (0.5 * dxo_ref[...]).astype(BF16)
            acc_scr[...] = jnp.zeros_like(acc_scr)

        @pl.when((c == 0) & (i == 0))
        def _():
            dg_ref[...] = jnp.zeros_like(dg_ref)

        wg, wu, wd = wgu_ref[0], wgu_ref[1], wd_ref[...]
        for rows in groups:
            gt = gate_ref[rows, :].astype(F32)
            u = up_ref[rows, :].astype(F32)
            s = jax.nn.sigmoid(gt)
            silu = gt * s
            dact = _dot_nt(dy_ref[rows, :], wd)
            dgate = (dact * u * (s * (1.0 + gt * (1.0 - s)))).astype(BF16)
            dup = (dact * silu).astype(BF16)
            dgate_ref[rows, :] = dgate
            dup_ref[rows, :] = dup
            acc_scr[rows, :] += _dot(dgate, wg) + _dot(dup, wu)

        @pl.when(c == nc - 1)
        def _():
            xhat, rstd = _rms(x_ref[...])
            dxn, dg = _rms_bwd(xhat, rstd, g_ref[...], acc_scr[...])
            dx_ref[...] = dxo_ref[...] + dxn
            dg_ref[...] += dg

    row = pl.BlockSpec((tm, d), lambda i, c: (i, 0))
    col = pl.BlockSpec((tm, tf), lambda i, c: (i, c))
    vec = pl.BlockSpec((1, d), lambda i, c: (0, 0))
    return _call(
        body, name, (t // tm, nc),
        [row, vec, row, col, col, pl.BlockSpec((2, tf, d), lambda i, c: (0, c, 0)),
         pl.BlockSpec((None, tf, d), lambda i, c: (2, c, 0))],
        [row, col, col, row, row, vec],
        [jax.ShapeDtypeStruct((t, d), F32), jax.ShapeDtypeStruct((t, f), BF16),
         jax.ShapeDtypeStruct((t, f), BF16),
         jax.ShapeDtypeStruct((t, d), BF16), jax.ShapeDtypeStruct((t, d), BF16),
         jax.ShapeDtypeStruct((1, d), F32)],
        [pltpu.VMEM((tm, d), F32)],
        ("arbitrary", "arbitrary"), (x, g, dxo, gate, up, wa, wa), payload)


def _ffn_dw(dgate, dup, act, h, dy, name, payload=None):
    t, f = dgate.shape
    d = h.shape[1]
    tk = 512
    tmm = f // 2
    nk = t // tk

    def body(dg_ref, du_ref, a_ref, h_ref, dy_ref, og_ref, ou_ref, od_ref, acc_g, acc_u, acc_d):
        k = pl.program_id(1)

        @pl.when(k == 0)
        def _():
            acc_g[...] = jnp.zeros_like(acc_g)
            acc_u[...] = jnp.zeros_like(acc_u)
            acc_d[...] = jnp.zeros_like(acc_d)

        hv = h_ref[...]
        acc_g[...] += _dot_tn(dg_ref[...], hv)
        acc_u[...] += _dot_tn(du_ref[...], hv)
        acc_d[...] += _dot_tn(a_ref[...], dy_ref[...])

        @pl.when(k == nk - 1)
        def _():
            og_ref[...] = acc_g[...].astype(BF16)
            ou_ref[...] = acc_u[...].astype(BF16)
            od_ref[...] = acc_d[...].astype(BF16)

    col = pl.BlockSpec((tk, tmm), lambda j, k: (k, j))
    row = pl.BlockSpec((tk, d), lambda j, k: (k, 0))
    out = pl.BlockSpec((tmm, d), lambda j, k: (j, 0))
    return _call(
        body, name, (f // tmm, nk), [col, col, col, row, row], [out, out, out],
        [jax.ShapeDtypeStruct((f, d), BF16)] * 3, [pltpu.VMEM((tmm, d), F32)] * 3,
        ("parallel", "arbitrary"), (dgate, dup, act, h, dy), payload)


def _tn_matmul(a, b, name, payload=None):
    t, m = a.shape
    n = b.shape[1]
    tk = 1024
    tmm = m // 2 if (m > 2048 and (m // 2) % LANES == 0) else m
    nk = t // tk

    def body(a_ref, b_ref, o_ref, acc_scr):
        k = pl.program_id(1)

        @pl.when(k == 0)
        def _():
            acc_scr[...] = jnp.zeros_like(acc_scr)

        acc_scr[...] += _dot_tn(a_ref[...].astype(BF16), b_ref[...].astype(BF16))

        @pl.when(k == nk - 1)
        def _():
            o_ref[...] = acc_scr[...].astype(BF16)

    (out,), p_outs = _call(
        body, name, (m // tmm, nk),
        [pl.BlockSpec((tk, tmm), lambda j, k: (k, j)), pl.BlockSpec((tk, n), lambda j, k: (k, 0))],
        [pl.BlockSpec((tmm, n), lambda j, k: (j, 0))],
        [jax.ShapeDtypeStruct((m, n), BF16)],
        [pltpu.VMEM((tmm, n), F32)],
        ("parallel", "arbitrary"), (a, b), payload)
    return out, p_outs


def _lane_ids(shape):
    return lax.broadcasted_iota(jnp.int32, shape, 1)


def _tril(w):
    r = lax.broadcasted_iota(jnp.int32, w.shape, 0)
    c = lax.broadcasted_iota(jnp.int32, w.shape, 1)
    return jnp.where(r >= c, w, 0.0)


def _shift_down(x, k):
    return x if k == 0 else pltpu.roll(x, k, 0)


def _shift_up(x, k):
    return x if k == 0 else pltpu.roll(x, x.shape[0] - k, 0)


def _sub_tile_shifts(ext, shift):
    return [shift(ext, b) for b in range(8)]


def _tap(shifted, j, n_out, down):
    a, b = divmod(j, 8)
    r0 = HALO - 8 * a if down else 8 * a
    return shifted[b][r0:r0 + n_out]


def _depthwise(shifted, w, n_out, down):
    acc = None
    for j in range(CONV_WIDTH):
        term = _tap(shifted, j, n_out, down) * w[CONV_WIDTH - 1 - j:CONV_WIDTH - j]
        acc = term if acc is None else acc + term
    return acc


def _conv_wgrad(shifted, dhc, n_out):
    return [jnp.sum(_tap(shifted, CONV_WIDTH - 1 - k, n_out, True) * dhc, axis=0, keepdims=True)
            for k in range(CONV_WIDTH)]


def _window_sums(ext, shift):
    assert POOL_WINDOWS == (2, 4, 8, 16)
    s2 = ext + shift(ext, 1)
    s4 = s2 + shift(s2, 2)
    s8 = s4 + shift(s4, 4)
    s16 = s8 + shift(s8, 8)
    grp = _lane_ids(ext.shape) // HEAD_DIM
    return jnp.where(grp == 0, s2, jnp.where(grp == 1, s4, jnp.where(grp == 2, s8, s16)))


def _pool_count(t0, n, width):
    pos = (lax.broadcasted_iota(jnp.int32, (n, width), 0) + (t0 + 1)).astype(F32)
    grp = _lane_ids((n, width)) // HEAD_DIM
    w0, w1, w2, w3 = (float(w) for w in POOL_WINDOWS)
    win = jnp.where(grp == 0, w0, jnp.where(grp == 1, w1, jnp.where(grp == 2, w2, w3)))
    return jnp.minimum(pos, win)


def _block_diag(pw):
    gn, cg, _ = pw.shape
    rows = []
    for gi in range(gn):
        parts = [pw[gi] if gj == gi else jnp.zeros((cg, cg), pw.dtype) for gj in range(gn)]
        rows.append(jnp.concatenate(parts, axis=1))
    return jnp.concatenate(rows, axis=0)


def _head_pair_mix(w_even, w_odd, v):
    lo = _lane_ids((CHUNK, LANES)) < HEAD_DIM
    return jnp.where(lo, _dot(w_even, v), _dot(w_odd, v))


def _mixer_fwd(x, g, wb, wc, p, name, payload=None):
    t, d = x.shape
    d_in = wb.shape[1]
    sgu = p["sgu_ln_g"].shape[1]
    pool = p["pool_scale"].shape[1]
    d_mix = 2 * sgu + pool
    tm = 512
    n_i = t // tm
    hb = tm // HALO

    def body(x_ref, xp_ref, g_ref, wi_ref, wo_ref, lng_ref, lnb_ref, ws_ref, bs_ref, cw_ref, cb_ref, clg_ref,
             clb_ref, bd_ref, ps_ref, xo_ref, z_ref, hc_ref, cat_ref):
        i = pl.program_id(0)
        first = i == 0
        gain, wi = g_ref[...], wi_ref[...]

        def project(xv):
            xhat, _ = _rms(xv)
            return _dot_nt((xhat * gain).astype(BF16), wi)

        z_main = project(x_ref[...])
        z_ref[...] = z_main
        z_prev = jnp.where(first, 0.0, project(xp_ref[...]))

        lng, lnb = lng_ref[...], lnb_ref[...]
        wt = [_tril(ws_ref[h]).astype(BF16) for h in range(sgu // HEAD_DIM)]
        for n in range(tm // CHUNK):
            rows = slice(n * CHUNK, (n + 1) * CHUNK)
            u = _gelu(z_main[rows, 0:sgu])
            vhat, _ = _ln(_gelu(z_main[rows, sgu:2 * sgu]))
            vn = (vhat * lng + lnb).astype(BF16)
            for gp in range(sgu // LANES):
                ls = slice(gp * LANES, (gp + 1) * LANES)
                mixed = _head_pair_mix(wt[2 * gp], wt[2 * gp + 1], vn[:, ls]) + bs_ref[:, ls]
                cat_ref[rows, ls] = (u[:, ls] * mixed).astype(BF16)

        def glu(zz):
            return zz[:, 2 * sgu:3 * sgu] * jax.nn.sigmoid(zz[:, 3 * sgu:4 * sgu])

        ext = jnp.concatenate([glu(z_prev), glu(z_main)], axis=0)
        hc = _depthwise(_sub_tile_shifts(ext, _shift_down), cw_ref[...], tm, True) + cb_ref[...]
        hc_ref[...] = hc
        hhat, _ = _ln(hc)
        bn = hhat * clg_ref[...] + clb_ref[...]
        cat_ref[:, sgu:2 * sgu] = (bn * jax.nn.sigmoid(bn)).astype(BF16)

        pext = jnp.concatenate([z_prev[:, 4 * sgu:], z_main[:, 4 * sgu:]], axis=0)
        sums = _window_sums(pext, _shift_down)[HALO:]
        pooled = sums / _pool_count(i * tm, tm, pool) - z_main[:, 4 * sgu:]
        mixed_c = _dot(pooled.astype(BF16), bd_ref[...].astype(BF16))
        cat_ref[:, 2 * sgu:] = (mixed_c * ps_ref[...]).astype(BF16)

        xo_ref[...] = x_ref[...] + _dot(cat_ref[...], wo_ref[...])

    def vec(n):
        return pl.BlockSpec((1, n), lambda i: (0, 0))

    return _call(
        body, name, (n_i,),
        [pl.BlockSpec((tm, d), lambda i: (i, 0)),
         pl.BlockSpec((HALO, d), lambda i: (jnp.maximum(i * hb - 1, 0), 0)),
         vec(d),
         pl.BlockSpec((None, d_in, d), lambda i: (0, 0, 0)), pl.BlockSpec((None, d_mix, d), lambda i: (0, 0, 0)),
         vec(sgu), vec(sgu),
         pl.BlockSpec(p["w_spatial"].shape, lambda i: (0, 0, 0)),
         pl.BlockSpec((CHUNK, sgu), lambda i: (0, 0)),
         pl.BlockSpec((CONV_WIDTH, sgu), lambda i: (0, 0)),
         vec(sgu), vec(sgu), vec(sgu),
         pl.BlockSpec((pool, pool), lambda i: (0, 0)), vec(pool)],
        [pl.BlockSpec((tm, d), lambda i: (i, 0)), pl.BlockSpec((tm, d_in), lambda i: (i, 0)),
         pl.BlockSpec((tm, sgu), lambda i: (i, 0)), pl.BlockSpec((tm, d_mix), lambda i: (i, 0))],
        [jax.ShapeDtypeStruct((t, d), F32), jax.ShapeDtypeStruct((t, d_in), F32),
         jax.ShapeDtypeStruct((t, sgu), F32), jax.ShapeDtypeStruct((t, d_mix), BF16)], [], ("parallel",),
        (x, x, g, wb, wc, p["sgu_ln_g"], p["sgu_ln_b"], p["w_spatial"], p["bs_full"], p["conv_w"], p["conv_b"],
         p["conv_ln_g"], p["conv_ln_b"], p["bd"], p["pool_scale"]), payload)


_R_SGU_G, _R_SGU_B, _R_CONV_B, _R_CLN_G, _R_CLN_B, _R_CONV_W = 0, 1, 2, 3, 4, 8
_R384_ROWS = 40


def _mixer_bwd(x, g, z, hc_saved, dxo, wb, wc, p, name, payload=None):
    t, d_in = z.shape
    d = x.shape[1]
    sgu = p["sgu_ln_g"].shape[1]
    pool = p["pool_scale"].shape[1]
    d_mix = 2 * sgu + pool
    n_head = sgu // HEAD_DIM
    tm = 512
    n_i = t // tm
    hb = tm // HALO

    def body(x_ref, g_ref, wi_ref, z_ref, zp_ref, zn_ref, dxo_ref, dxon_ref, wo_ref, hc_ref, hcn_ref, lng_ref,
             lnb_ref, ws_ref, bs_ref, cw_ref, clg_ref, clb_ref, bd_ref, ps_ref,
             dx_ref, dz_ref, hm_ref, dgm_ref, g384_ref, gws_ref, gpool_ref, dbs_scr):
        i = pl.program_id(0)
        first, last = i == 0, i == n_i - 1

        @pl.when(first)
        def _():
            dgm_ref[...] = jnp.zeros_like(dgm_ref)
            g384_ref[...] = jnp.zeros_like(g384_ref)
            gws_ref[...] = jnp.zeros_like(gws_ref)
            gpool_ref[...] = jnp.zeros_like(gpool_ref)
            dbs_scr[...] = jnp.zeros_like(dbs_scr)

        z_main = z_ref[...]
        z_prev = jnp.where(first, 0.0, zp_ref[...])
        z_next = jnp.where(last, 0.0, zn_ref[...])
        wo = wo_ref[...]
        dc_main = _dot_nt(dxo_ref[...].astype(BF16), wo)
        dc_next = jnp.where(last, 0.0, _dot_nt(dxon_ref[...].astype(BF16), wo))

        lng, lnb = lng_ref[...], lnb_ref[...]
        wt = [_tril(ws_ref[h]) for h in range(n_head)]
        wt_b = [w.astype(BF16) for w in wt]
        wtt_b = [w.T.astype(BF16) for w in wt]
        lo = _lane_ids((CHUNK, LANES)) < HEAD_DIM
        d_lng = jnp.zeros((1, sgu), F32)
        d_lnb = jnp.zeros((1, sgu), F32)
        dws = [jnp.zeros((CHUNK, CHUNK), F32) for _ in range(n_head)]
        for n in range(tm // CHUNK):
            rows = slice(n * CHUNK, (n + 1) * CHUNK)
            au, av = z_main[rows, 0:sgu], z_main[rows, sgu:2 * sgu]
            u = _gelu(au)
            vhat, vrstd = _ln(_gelu(av))
            vn = (vhat * lng + lnb).astype(BF16)
            da = dc_main[rows, 0:sgu]
            dmixed = da * u
            dbs_scr[...] += dmixed
            dvn_parts, du_parts = [], []
            for gp in range(sgu // LANES):
                ls = slice(gp * LANES, (gp + 1) * LANES)
                vn_g = vn[:, ls]
                mixed = _head_pair_mix(wt_b[2 * gp], wt_b[2 * gp + 1], vn_g) + bs_ref[:, ls]
                du_parts.append(da[:, ls] * mixed)
                dm_g = dmixed[:, ls]
                dm_b = dm_g.astype(BF16)
                dvn_parts.append(jnp.where(lo, _dot(wtt_b[2 * gp], dm_b), _dot(wtt_b[2 * gp + 1], dm_b)))
                dws[2 * gp] = dws[2 * gp] + _dot_nt(jnp.where(lo, dm_g, 0.0).astype(BF16), vn_g)
                dws[2 * gp + 1] = dws[2 * gp + 1] + _dot_nt(jnp.where(lo, 0.0, dm_g).astype(BF16), vn_g)
            dvn = jnp.concatenate(dvn_parts, axis=1)
            du = jnp.concatenate(du_parts, axis=1)
            dv, dg_n, db_n = _ln_bwd(vhat, vrstd, lng, dvn)
            d_lng = d_lng + dg_n
            d_lnb = d_lnb + db_n
            dz_ref[rows, 0:sgu] = (du * _gelu_grad(au)).astype(BF16)
            dz_ref[rows, sgu:2 * sgu] = (dv * _gelu_grad(av)).astype(BF16)
        for h in range(n_head):
            gws_ref[h] += _tril(dws[h])
        g384_ref[_R_SGU_G:_R_SGU_G + 1, :] += d_lng
        g384_ref[_R_SGU_B:_R_SGU_B + 1, :] += d_lnb

        clg = clg_ref[...]
        bcols = slice(2 * sgu, 4 * sgu)
        zb = jnp.concatenate([z_prev[:, bcols], z_main[:, bcols], z_next[:, bcols]], axis=0)
        bval, bgate = zb[:, 0:sgu], zb[:, sgu:2 * sgu]
        sg = jax.nn.sigmoid(bgate)
        hglu = bval * sg
        n_out = tm + HALO
        hglu_shifts = _sub_tile_shifts(hglu, _shift_down)
        cw = cw_ref[...]
        hc = jnp.concatenate([hc_ref[...], jnp.where(last, 0.0, hcn_ref[...])], axis=0)
        hhat, hrstd = _ln(hc)
        bn = hhat * clg + clb_ref[...]
        db = jnp.concatenate([dc_main[:, sgu:2 * sgu], dc_next[:, sgu:2 * sgu]], axis=0)
        dbn = db * _silu_grad(bn)
        dhc_all, _, _ = _ln_bwd(hhat, hrstd, clg, dbn)
        dbn_m, hhat_m, dhc = dbn[:tm], hhat[:tm], dhc_all[:tm]
        g384_ref[_R_CLN_G:_R_CLN_G + 1, :] += jnp.sum(dbn_m * hhat_m, axis=0, keepdims=True)
        g384_ref[_R_CLN_B:_R_CLN_B + 1, :] += jnp.sum(dbn_m, axis=0, keepdims=True)
        g384_ref[_R_CONV_B:_R_CONV_B + 1, :] += jnp.sum(dhc, axis=0, keepdims=True)
        wrows = _conv_wgrad(hglu_shifts, dhc, tm)
        for k in range(CONV_WIDTH):
            g384_ref[_R_CONV_W + k:_R_CONV_W + k + 1, :] += wrows[k]
        dhglu = _depthwise(_sub_tile_shifts(dhc_all, _shift_up), cw, tm, False)
        bval_m, sg_m = bval[HALO:HALO + tm], sg[HALO:HALO + tm]
        dz_ref[:, 2 * sgu:3 * sgu] = (dhglu * sg_m).astype(BF16)
        dz_ref[:, 3 * sgu:4 * sgu] = (dhglu * bval_m * sg_m * (1.0 - sg_m)).astype(BF16)

        bd_b = bd_ref[...].astype(BF16)
        ps = ps_ref[...]
        p_main = z_main[:, 4 * sgu:]
        pext = jnp.concatenate([z_prev[:, 4 * sgu:], p_main], axis=0)
        cnt = _pool_count(i * tm, n_out, pool)
        pooled = _window_sums(pext, _shift_down)[HALO:] / cnt[:tm] - p_main
        pooled_b = pooled.astype(BF16)
        dcc = jnp.concatenate([dc_main[:, 2 * sgu:], dc_next[:, 2 * sgu:]], axis=0)
        dmix_c = dcc * ps
        mixed_c = _dot(pooled_b, bd_b)
        grp_r = lax.broadcasted_iota(jnp.int32, (pool, pool), 0) // HEAD_DIM
        grp_c = lax.broadcasted_iota(jnp.int32, (pool, pool), 1) // HEAD_DIM
        gpool_ref[0:pool, :] += jnp.where(grp_r == grp_c, _dot_tn(pooled_b, dmix_c[:tm].astype(BF16)), 0.0)
        gpool_ref[pool:pool + 1, :] += jnp.sum(dcc[:tm] * mixed_c, axis=0, keepdims=True)
        dpooled = _dot_nt(dmix_c.astype(BF16), bd_b)
        q = dpooled / cnt
        dp = _window_sums(q, _shift_up)[:tm] - dpooled[:tm]
        dz_ref[:, 4 * sgu:] = dp.astype(BF16)

        gain = g_ref[...]
        xhat, rstd = _rms(x_ref[...])
        hm_ref[...] = (xhat * gain).astype(BF16)
        dxn, dgm = _rms_bwd(xhat, rstd, gain, _dot(dz_ref[...], wi_ref[...]))
        dx_ref[...] = dxo_ref[...] + dxn
        dgm_ref[...] += dgm

        @pl.when(last)
        def _():
            r = lax.broadcasted_iota(jnp.int32, (sgu, LANES), 0)
            c = lax.broadcasted_iota(jnp.int32, (sgu, LANES), 1)
            sel = (r // HEAD_DIM == c).astype(BF16)
            gws_ref[n_head] = _split_dot(dbs_scr[...], sel)

    def vec(n):
        return pl.BlockSpec((1, n), lambda i: (0, 0))

    def prev_map(i):
        return (jnp.maximum(i * hb - 1, 0), 0)

    def next_map(i):
        return (jnp.minimum((i + 1) * hb, n_i * hb - 1), 0)

    return _call(
        body, name, (n_i,),
        [pl.BlockSpec((tm, d), lambda i: (i, 0)), pl.BlockSpec((1, d), lambda i: (0, 0)),
         pl.BlockSpec((None, d_in, d), lambda i: (0, 0, 0)),
         pl.BlockSpec((tm, d_in), lambda i: (i, 0)),
         pl.BlockSpec((HALO, d_in), prev_map), pl.BlockSpec((HALO, d_in), next_map),
         pl.BlockSpec((tm, d), lambda i: (i, 0)), pl.BlockSpec((HALO, d), next_map),
         pl.BlockSpec((None, d_mix, d), lambda i: (0, 0, 0)),
         pl.BlockSpec((tm, sgu), lambda i: (i, 0)), pl.BlockSpec((HALO, sgu), next_map),
         vec(sgu), vec(sgu),
         pl.BlockSpec(p["w_spatial"].shape, lambda i: (0, 0, 0)),
         pl.BlockSpec((CHUNK, sgu), lambda i: (0, 0)),
         pl.BlockSpec((CONV_WIDTH, sgu), lambda i: (0, 0)),
         vec(sgu), vec(sgu),
         pl.BlockSpec((pool, pool), lambda i: (0, 0)), vec(pool)],
        [pl.BlockSpec((tm, d), lambda i: (i, 0)), pl.BlockSpec((tm, d_in), lambda i: (i, 0)),
         pl.BlockSpec((tm, d), lambda i: (i, 0)), pl.BlockSpec((1, d), lambda i: (0, 0)),
         pl.BlockSpec((_R384_ROWS, sgu), lambda i: (0, 0)),
         pl.BlockSpec((n_head + 1, CHUNK, CHUNK), lambda i: (0, 0, 0)),
         pl.BlockSpec((pool + 8, pool), lambda i: (0, 0))],
        [jax.ShapeDtypeStruct((t, d), F32), jax.ShapeDtypeStruct((t, d_in), BF16),
         jax.ShapeDtypeStruct((t, d), BF16), jax.ShapeDtypeStruct((1, d), F32),
         jax.ShapeDtypeStruct((_R384_ROWS, sgu), F32),
         jax.ShapeDtypeStruct((n_head + 1, CHUNK, CHUNK), F32),
         jax.ShapeDtypeStruct((pool + 8, pool), F32)],
        [pltpu.VMEM((CHUNK, sgu), F32)], ("arbitrary",),
        (x, g, wb, z, z, z, dxo, dxo, wc, hc_saved, hc_saved, p["sgu_ln_g"], p["sgu_ln_b"], p["w_spatial"],
         p["bs_full"], p["conv_w"], p["conv_ln_g"], p["conv_ln_b"], p["bd"], p["pool_scale"]), payload)


def _all_gather(arrs, name, extra=None, to_sum=()):
    gather = _GatherIci(arrs)
    n = len(arrs)
    forward = _GatherForward([jax.ShapeDtypeStruct(s.shape, s.dtype) for s in gather.out_shapes])
    x_in = len(extra.ins) if extra else 0
    x_out = len(extra.out_shapes) if extra else 0
    n_sum = len(to_sum)

    def body(*refs):
        ins, x_ins, s_ins = refs[:n], refs[n:n + x_in], refs[n + x_in:n + x_in + n_sum]
        o0 = n + x_in + n_sum
        outs, x_outs = refs[o0:o0 + n], refs[o0 + n:o0 + n + x_out]
        s_outs = refs[o0 + n + x_out:o0 + n + x_out + n_sum]
        sems = refs[o0 + n + x_out + n_sum:]
        first = gather.build(ins, outs, *sems[0:3])
        first.start()
        if extra:
            beside = extra.build(x_ins, x_outs, *sems[6:9])
            beside.start()
        for s_ref, o_ref in zip(s_ins, s_outs):
            r = o_ref.shape[0]
            acc = s_ref[0:r, :]
            for q in range(1, N_DEV):
                acc = acc + s_ref[q * r:(q + 1) * r, :]
            o_ref[...] = acc
        first.wait()
        second = forward.build(outs, outs, *sems[3:6])
        second.start()
        second.wait()
        if extra:
            beside.wait()

    in_vmem = pl.BlockSpec(memory_space=pltpu.VMEM)
    outs = pl.pallas_call(
        body, name=name,
        in_specs=[ANY] * (n + x_in) + [in_vmem] * n_sum, out_specs=[ANY] * (n + x_out) + [in_vmem] * n_sum,
        out_shape=list(gather.out_shapes) + (list(extra.out_shapes) if extra else [])
        + [jax.ShapeDtypeStruct((s.shape[0] // N_DEV, s.shape[1]), s.dtype) for s in to_sum],
        scratch_shapes=gather.sem_shapes() + forward.sem_shapes() + (extra.sem_shapes() if extra else []),
        compiler_params=_cparams(),
    )(*arrs, *(extra.ins if extra else []), *to_sum)
    return list(outs[:n]), list(outs[n:n + x_out]), list(outs[n + x_out:])


def _all_gather_relayed(arrs, name):
    n = len(arrs)
    n_pairs = 8
    units = [(a, l) for a in range(n) for l in range(arrs[a].shape[0])]

    def body(*refs):
        ins, outs = refs[:n], refs[n:2 * n]
        send_sems, recv_sems, local_sems = refs[2 * n:]
        x, y, c = _position()
        sib, xn, yn = (x, y, 1 - c), (1 - x, y, c), (x, 1 - y, c)
        diag = (1 - x, 1 - y, c)

        def rows(u, px, py, pc, half=None):
            a, l = units[u]
            r = ins[a].shape[1]
            base = (4 * px + 2 * py + pc) * r
            if half is None:
                return outs[a].at[pl.ds(l, 1), pl.ds(base, r), :]
            return outs[a].at[pl.ds(l, 1), pl.ds(base + half * (r // 2), r // 2), :]

        def send(u, k, src, dst, to):
            return _remote(src, dst, send_sems, recv_sems, u * n_pairs + k, to)

        def arrived(u, k, land, sender):
            _remote(land, land, send_sems, recv_sems, u * n_pairs + k, sender).wait_recv()

        own, sent = [], []
        for u, (a, l) in enumerate(units):
            src = ins[a].at[pl.ds(l, 1)]
            own.append(pltpu.make_async_copy(src, rows(u, x, y, c), local_sems.at[u]))
            sent += [send(u, k, src, rows(u, x, y, c), to) for k, to in enumerate((sib, xn, yn))]
        for cp in own + sent:
            cp.start()
        for u in range(len(units)):
            arrived(u, 1, rows(u, *xn), xn)
            arrived(u, 2, rows(u, *yn), yn)
            relay = [send(u, 3, rows(u, *xn, half=0), rows(u, *xn, half=0), yn),
                     send(u, 4, rows(u, *yn, half=1), rows(u, *yn, half=1), xn),
                     send(u, 5, rows(u, *xn), rows(u, *xn), sib),
                     send(u, 6, rows(u, *yn), rows(u, *yn), sib)]
            for cp in relay:
                cp.start()
            sent += relay
        for u in range(len(units)):
            arrived(u, 3, rows(u, *diag, half=0), yn)
            arrived(u, 4, rows(u, *diag, half=1), xn)
            last = send(u, 7, rows(u, *diag), rows(u, *diag), sib)
            last.start()
            sent.append(last)
        for u in range(len(units)):
            arrived(u, 0, rows(u, *sib), sib)
            arrived(u, 5, rows(u, 1 - x, y, 1 - c), sib)
            arrived(u, 6, rows(u, x, 1 - y, 1 - c), sib)
            arrived(u, 7, rows(u, 1 - x, 1 - y, 1 - c), sib)
        for cp in sent:
            cp.wait_send()
        for cp in own:
            cp.wait()

    n_units = len(units)
    return list(pl.pallas_call(
        body, name=name, in_specs=[ANY] * n, out_specs=[ANY] * n,
        out_shape=[jax.ShapeDtypeStruct((a.shape[0], N_DEV * a.shape[1], a.shape[2]), a.dtype) for a in arrs],
        scratch_shapes=[pltpu.SemaphoreType.DMA((n_pairs * n_units,)), pltpu.SemaphoreType.DMA((n_pairs * n_units,)),
                        pltpu.SemaphoreType.DMA((n_units,))],
    )(*arrs))


def _pair_sums(grads, recvs, cidx, name):
    n = len(grads)

    def body(c_ref, *refs):
        for g_ref, r_ref, o_ref in zip(refs[:n], refs[n:2 * n], refs[2 * n:]):
            o_ref[...] = (g_ref[...].astype(F32) + r_ref[...].astype(F32)).astype(BF16)

    shapes = [(g.shape[0] // N_DEV, g.shape[1]) for g in grads]
    return list(pl.pallas_call(
        body, name=name,
        grid_spec=pltpu.PrefetchScalarGridSpec(
            num_scalar_prefetch=1, grid=(N_CHIP,),
            in_specs=[pl.BlockSpec(s, lambda q, c: (2 * q + c[0], 0)) for s in shapes]
            + [pl.BlockSpec(s, lambda q, c: (q, 0)) for s in shapes],
            out_specs=[pl.BlockSpec(s, lambda q, c: (q, 0)) for s in shapes]),
        out_shape=[jax.ShapeDtypeStruct((N_CHIP * r, cols), BF16) for r, cols in shapes],
        compiler_params=_cparams(("parallel",)),
    )(cidx, *grads, *recvs))


def _sum_blocks(parts, nblk, name):
    r = parts.shape[0] // nblk
    cols = parts.shape[1]

    def body(p_ref, o_ref):
        acc = p_ref[0:r, :].astype(F32)
        for q in range(1, nblk):
            acc = acc + p_ref[q * r:(q + 1) * r, :].astype(F32)
        o_ref[...] = acc

    return pl.pallas_call(
        body, name=name,
        out_shape=jax.ShapeDtypeStruct((r, cols), F32),
        compiler_params=_cparams(),
    )(parts)


def _adamw_math(w, g, m, v):
    m = ADAM_B1 * m + (1.0 - ADAM_B1) * g
    v = ADAM_B2 * v + (1.0 - ADAM_B2) * (g * g)
    m_hat = m / (1.0 - ADAM_B1 ** ADAM_STEP)
    v_hat = v / (1.0 - ADAM_B2 ** ADAM_STEP)
    delta = -ADAM_LR * (m_hat / (jnp.sqrt(v_hat) + ADAM_EPS) + ADAM_WD * w)
    return delta, m, v


def _finish_sharded(parts, w, m, v, name):
    depth, rr, cw = w.shape

    def body(*refs):
        p_refs = refs[:depth]
        w_ref, m_ref, v_ref, g_ref, d_ref, mo_ref, vo_ref = refs[depth:]
        l = pl.program_id(0)
        for k in range(depth):
            @pl.when(l == k)
            def _(p_ref=p_refs[k]):
                r = p_ref.shape[0] // N_CHIP
                acc = p_ref[0:r, :].astype(F32)
                for q in range(1, N_CHIP):
                    acc = acc + p_ref[q * r:(q + 1) * r, :].astype(F32)
                g_ref[...] = acc
                d_ref[...], mo_ref[...], vo_ref[...] = _adamw_math(w_ref[...], acc, m_ref[...], v_ref[...])

    blk = pl.BlockSpec((None, rr, cw), lambda l: (l, 0, 0))
    return pl.pallas_call(
        body, name=name, grid=(depth,),
        in_specs=[pl.BlockSpec(p.shape, lambda l: (0, 0)) for p in parts] + [blk] * 3, out_specs=[blk] * 4,
        out_shape=[jax.ShapeDtypeStruct(w.shape, F32)] * 4,
        compiler_params=_cparams(("arbitrary",)),
    )(*parts, w, m, v)


def _adamw_small(ws, gs, ms, vs, name):
    n = len(ws)

    def body(*refs):
        for k in range(n):
            w_ref, g_ref, m_ref, v_ref = (refs[j * n + k] for j in range(4))
            d_ref, mo_ref, vo_ref = (refs[(4 + j) * n + k] for j in range(3))
            d_ref[...], mo_ref[...], vo_ref[...] = _adamw_math(w_ref[...], g_ref[...], m_ref[...], v_ref[...])

    shapes = [jax.ShapeDtypeStruct(w.shape, F32) for w in ws]
    return pl.pallas_call(
        body, name=name, out_shape=shapes * 3, compiler_params=_cparams(),
    )(*ws, *gs, *ms, *vs)


def kernel(x, ffn1_norm, ffn1_w_gate, ffn1_w_up, ffn1_w_down, mix_norm, w_in, sgu_ln_g, sgu_ln_b, w_spatial, b_spatial, conv_w, conv_b, conv_ln_g, conv_ln_b, pool_w, pool_scale, w_out, ffn2_norm, ffn2_w_gate, ffn2_w_up, ffn2_w_down, final_norm, loss_target, m_ffn1_norm, m_ffn1_w_gate, m_ffn1_w_up, m_ffn1_w_down, m_mix_norm, m_w_in, m_sgu_ln_g, m_sgu_ln_b, m_w_spatial, m_b_spatial, m_conv_w, m_conv_b, m_conv_ln_g, m_conv_ln_b, m_pool_w, m_pool_scale, m_w_out, m_ffn2_norm, m_ffn2_w_gate, m_ffn2_w_up, m_ffn2_w_down, m_final_norm, v_ffn1_norm, v_ffn1_w_gate, v_ffn1_w_up, v_ffn1_w_down, v_mix_norm, v_w_in, v_sgu_ln_g, v_sgu_ln_b, v_w_spatial, v_b_spatial, v_conv_w, v_conv_b, v_conv_ln_g, v_conv_ln_b, v_pool_w, v_pool_scale, v_w_out, v_ffn2_norm, v_ffn2_w_gate, v_ffn2_w_up, v_ffn2_w_down, v_final_norm):
    names = ["ffn1_norm", "ffn1_w_gate", "ffn1_w_up", "ffn1_w_down", "mix_norm", "w_in", "sgu_ln_g", "sgu_ln_b",
             "w_spatial", "b_spatial", "conv_w", "conv_b", "conv_ln_g", "conv_ln_b", "pool_w", "pool_scale",
             "w_out", "ffn2_norm", "ffn2_w_gate", "ffn2_w_up", "ffn2_w_down", "final_norm"]
    W = dict(zip(names, [ffn1_norm, ffn1_w_gate, ffn1_w_up, ffn1_w_down, mix_norm, w_in, sgu_ln_g, sgu_ln_b,
                         w_spatial, b_spatial, conv_w, conv_b, conv_ln_g, conv_ln_b, pool_w, pool_scale, w_out,
                         ffn2_norm, ffn2_w_gate, ffn2_w_up, ffn2_w_down, final_norm]))
    M = dict(zip(names, [m_ffn1_norm, m_ffn1_w_gate, m_ffn1_w_up, m_ffn1_w_down, m_mix_norm, m_w_in, m_sgu_ln_g,
                         m_sgu_ln_b, m_w_spatial, m_b_spatial, m_conv_w, m_conv_b, m_conv_ln_g, m_conv_ln_b,
                         m_pool_w, m_pool_scale, m_w_out, m_ffn2_norm, m_ffn2_w_gate, m_ffn2_w_up, m_ffn2_w_down,
                         m_final_norm]))
    V = dict(zip(names, [v_ffn1_norm, v_ffn1_w_gate, v_ffn1_w_up, v_ffn1_w_down, v_mix_norm, v_w_in, v_sgu_ln_g,
                         v_sgu_ln_b, v_w_spatial, v_b_spatial, v_conv_w, v_conv_b, v_conv_ln_g, v_conv_ln_b,
                         v_pool_w, v_pool_scale, v_w_out, v_ffn2_norm, v_ffn2_w_gate, v_ffn2_w_up, v_ffn2_w_down,
                         v_final_norm]))

    depth, d = ffn1_norm.shape
    t = x.shape[1]
    sgu = sgu_ln_g.shape[1]
    pool = pool_scale.shape[1]
    n_head = sgu // HEAD_DIM
    cw_shard = conv_w.shape[2]
    xs = x.reshape(t, d)
    target = loss_target.reshape(t, d)

    def tr(w):
        return jnp.swapaxes(w, 1, 2).astype(BF16)

    ffn_shards = [[jnp.stack([tr(ffn1_w_gate)[l], tr(ffn1_w_up)[l], ffn1_w_down[l].astype(BF16)]),
                   jnp.stack([tr(ffn2_w_gate)[l], tr(ffn2_w_up)[l], ffn2_w_down[l].astype(BF16)])]
                  for l in range(depth)]
    win_shards = [tr(w_in)[l:l + 1] for l in range(depth)]
    wout_shards = [w_out[l:l + 1].astype(BF16) for l in range(depth)]
    cw_rows = depth * CONV_WIDTH
    cw_pad = -cw_rows % 8
    cw_send = jnp.pad(conv_w.reshape(cw_rows, cw_shard), ((0, cw_pad), (0, 0)))[None]
    wffn, wb, wc = {}, {}, {}
    wffn[(0, 0)], wb[0], wc[0], cwg = _all_gather_relayed(
        [ffn_shards[0][0], win_shards[0], wout_shards[0], cw_send], "ag_first")
    conv_w_full = cwg.reshape(N_DEV, cw_rows + cw_pad, cw_shard)[:, :cw_rows].reshape(
        N_DEV, depth, CONV_WIDTH, cw_shard).transpose(1, 2, 0, 3).reshape(depth, CONV_WIDTH, N_DEV * cw_shard)

    def mixer_params(l):
        return dict(
            sgu_ln_g=sgu_ln_g[l:l + 1], sgu_ln_b=sgu_ln_b[l:l + 1], w_spatial=w_spatial[l],
            bs_full=jnp.repeat(b_spatial[l].T, HEAD_DIM, axis=1),
            conv_w=conv_w_full[l], conv_b=conv_b[l:l + 1], conv_ln_g=conv_ln_g[l:l + 1],
            conv_ln_b=conv_ln_b[l:l + 1], bd=_block_diag(pool_w[l]), pool_scale=pool_scale[l:l + 1])

    saved = []
    cur = xs
    for l in range(depth):
        p = mixer_params(l)
        x0 = cur
        more = l + 1 < depth
        (x1, gate1, up1, act1), part = _ffn_fwd(x0, ffn1_norm[l:l + 1], wffn[(l, 0)], f"ffn1_fwd_{l}",
                                          _GatherIci([ffn_shards[l][1]]))
        riding = [_GatherForward(part)] + ([_GatherIci([win_shards[l + 1], wout_shards[l + 1]])] if more else [])
        (x2, z, hc, cat), part = _mixer_fwd(x1, mix_norm[l:l + 1], wb[l], wc[l], p, f"mixer_fwd_{l}", _Merged(riding))
        wffn[(l, 1)] = part[0]
        riding = [_GatherIci([ffn_shards[l + 1][0]]), _GatherForward(part[1:])] if more else []
        relay = _GatherForward(riding[0].out_shapes) if more else None
        outs, part = _ffn_fwd(x2, ffn2_norm[l:l + 1], wffn[(l, 1)], f"ffn2_fwd_{l}",
                              _Merged(riding) if more else None,
                              None if more else (final_norm.reshape(1, d), target), relay)
        if more:
            cur, gate2, up2, act2 = outs
            wffn[(l + 1, 0)], wb[l + 1], wc[l + 1] = part
        else:
            dx, gate2, up2, act2, d_final, loss_part = outs
        saved.append((p, x0, gate1, up1, act1, x1, z, hc, cat, x2, gate2, up2, act2))

    cidx = lax.axis_index("c").astype(jnp.int32).reshape(1)
    from_chips = {}
    to_pair, to_chip = [], []
    small = []

    def pair_payload():
        return _PairExchange([g for _, g in to_pair]) if to_pair else None

    def pair_done(received):
        if to_pair:
            (nm, l), _ = to_pair[0]
            sums = _pair_sums([g for _, g in to_pair], list(received), cidx, f"rs_pair_sum_{nm}_{l}")
            to_chip.extend((key, s) for (key, _), s in zip(to_pair, sums))
        to_pair.clear()

    def take_chip():
        items = list(to_chip)
        to_chip.clear()
        return items

    def chip_payload(items):
        return _ChipExchange([s for _, s in items]) if items else None

    def chip_done(items, landed):
        for (key, _), o in zip(items, landed):
            from_chips[key] = o

    def ffn_weight_grads(prefix, l, dgate, dup, act, h, dy):
        items = take_chip()
        items, later = items[:2], items[2:]
        to_chip.extend(later)
        grads3, landed = _ffn_dw(dgate, dup, act, h, dy, f"dw_{prefix}_{l}", chip_payload(items))
        chip_done(items, landed)
        to_pair.extend(((f"{prefix}_{nm}", l), g) for nm, g in zip(("w_gate", "w_up", "w_down"), grads3))

    for l in reversed(range(depth)):
        p, x0, gate1, up1, act1, x1, z, hc, cat, x2, gate2, up2, act2 = saved[l]
        (dx, dgate, dup, h, dy, dg_ffn2), received = _ffn_bwd(
            x2, ffn2_norm[l:l + 1], dx, gate2, up2, wffn[(l, 1)], f"ffn2_bwd_{l}", pair_payload())
        pair_done(received)
        ffn_weight_grads("ffn2", l, dgate, dup, act2, h, dy)
        g_out, received = _tn_matmul(cat, dx, f"dw_w_out_{l}", pair_payload())
        pair_done(received)
        items = take_chip()
        items, later = items[:3], items[3:]
        to_chip.extend(later)
        (dx, dz, hm, dg_mix, g384, gws, gpool), landed = _mixer_bwd(
            x1, mix_norm[l:l + 1], z, hc, dx, wb[l], wc[l], p, f"mixer_bwd_{l}", chip_payload(items))
        chip_done(items, landed)
        g_in, _ = _tn_matmul(dz, hm, f"dw_w_in_{l}")
        to_pair.extend([(("w_out", l), g_out), (("w_in", l), g_in)])
        if l > 0:
            (dx, dgate, dup, h, dy, dg_ffn1), received = _ffn_bwd(
                x0, ffn1_norm[l:l + 1], dx, gate1, up1, wffn[(l, 0)], f"ffn1_bwd_{l}", pair_payload())
            pair_done(received)
            ffn_weight_grads("ffn1", l, dgate, dup, act1, h, dy)
            small.append((l, g384, gws, gpool, dg_ffn1, dg_mix, dg_ffn2))
            continue

        small.append((0, g384, gws, gpool, None, dg_mix, dg_ffn2))
        small.sort(key=lambda s: s[0])
        norm_rows = []
        for (sl, _, _, _, dg1, dgm, dg2) in small:
            norm_rows += [jnp.zeros((1, d), F32) if dg1 is None else dg1, dgm, dg2]
        norm_rows += [d_final, jnp.pad(loss_part, ((0, 0), (0, d - LANES)))]
        n_norm = len(norm_rows)
        norm_pack = jnp.concatenate(norm_rows + [jnp.zeros((8 - n_norm % 8, d), F32)] * (n_norm % 8 != 0), axis=0)
        parts = [norm_pack]
        for (_, s384, sws, spool, _, _, _) in small:
            parts += [s384, sws.reshape((n_head + 1) * CHUNK, CHUNK), spool]
        n_pair = len(to_pair)
        early = take_chip()
        riding = [pair_payload(), _GatherIci([a[None] for a in parts])] + ([chip_payload(early)] if early else [])
        (dx, dgate, dup, h, dy, dg_ffn1), landed = _ffn_bwd(
            x0, ffn1_norm[l:l + 1], dx, gate1, up1, wffn[(l, 0)], f"ffn1_bwd_{l}", _Merged(riding))
        pair_done(landed[:n_pair])
        chip_done(early, landed[n_pair + len(parts):])
        items = take_chip()
        g_gate, landed = _tn_matmul(
            dgate, h, f"dw_ffn1_w_gate_{l}",
            _Merged([chip_payload(items), _GatherForward(landed[n_pair:n_pair + len(parts)])]))
        chip_done(items, landed[:len(items)])
        gathered = landed[len(items):]
        to_pair.append((("ffn1_w_gate", l), g_gate))
        g_up, received = _tn_matmul(dup, h, f"dw_ffn1_w_up_{l}", pair_payload())
        pair_done(received)
        to_pair.append((("ffn1_w_up", l), g_up))
        items = take_chip()
        g_down, landed = _tn_matmul(act1, dy, f"dw_ffn1_w_down_{l}", _Merged([chip_payload(items), pair_payload()]))
        chip_done(items, landed[:len(items)])
        pair_done(landed[len(items):])
        to_pair.append((("ffn1_w_down", l), g_down))
    grad_x = dx.reshape(x.shape)
    pair_done(_comm(pair_payload(), "rs_pair_exchange_last"))
    items = take_chip()
    (late_norm,), landed, summed = _all_gather([jnp.pad(dg_ffn1, ((0, 7), (0, 0)))[None]], "ag_tail",
                                               chip_payload(items), [g[0] for g in gathered])
    chip_done(items, landed)
    late_sum = _sum_blocks(late_norm[0], N_DEV, "sum_small_late")
    norm_sum = summed[0]
    loss = norm_sum[3 * depth + 1, 0]
    cpos = lax.axis_index("x") * 4 + lax.axis_index("y") * 2 + lax.axis_index("c")
    sg = {nm: [] for nm in names}
    for l in range(depth):
        g384, gws, gpool = summed[1 + 3 * l], summed[2 + 3 * l].reshape(n_head + 1, CHUNK, CHUNK), summed[3 + 3 * l]
        sg["ffn1_norm"].append(norm_sum[3 * l] if l > 0 else late_sum[0])
        sg["mix_norm"].append(norm_sum[3 * l + 1])
        sg["ffn2_norm"].append(norm_sum[3 * l + 2])
        sg["sgu_ln_g"].append(g384[_R_SGU_G])
        sg["sgu_ln_b"].append(g384[_R_SGU_B])
        sg["conv_b"].append(g384[_R_CONV_B])
        sg["conv_ln_g"].append(g384[_R_CLN_G])
        sg["conv_ln_b"].append(g384[_R_CLN_B])
        sg["conv_w"].append(lax.dynamic_slice_in_dim(g384[_R_CONV_W:_R_CONV_W + CONV_WIDTH], cpos * cw_shard,
                                                     cw_shard, axis=1))
        sg["w_spatial"].append(gws[:n_head])
        sg["b_spatial"].append(gws[n_head][:, :n_head].T)
        sg["pool_w"].append(jnp.stack([gpool[k * HEAD_DIM:(k + 1) * HEAD_DIM, k * HEAD_DIM:(k + 1) * HEAD_DIM]
                                       for k in range(pool // HEAD_DIM)], axis=0))
        sg["pool_scale"].append(gpool[pool])
    small_names = ["ffn1_norm", "mix_norm", "sgu_ln_g", "sgu_ln_b", "w_spatial", "b_spatial", "conv_w", "conv_b",
                   "conv_ln_g", "conv_ln_b", "pool_w", "pool_scale", "ffn2_norm"]
    grads = {nm: jnp.stack(sg[nm], axis=0) for nm in small_names}
    grads["final_norm"] = norm_sum[3 * depth]

    delta, new_m, new_v = {}, {}, {}
    big_names = ["ffn1_w_gate", "ffn1_w_up", "ffn1_w_down", "w_in", "w_out", "ffn2_w_gate", "ffn2_w_up",
                 "ffn2_w_down"]
    transposed = {"ffn1_w_gate", "ffn1_w_up", "w_in", "ffn2_w_gate", "ffn2_w_up"}
    for nm in big_names:
        view = (lambda a: jnp.swapaxes(a, 1, 2)) if nm in transposed else (lambda a: a)
        outs = _finish_sharded([from_chips[(nm, l)] for l in range(depth)], view(W[nm]), view(M[nm]), view(V[nm]),
                               f"adamw_{nm}")
        grads[nm], delta[nm], new_m[nm], new_v[nm] = (view(o) for o in outs)
    snames = small_names + ["final_norm"]

    def flat2(a):
        return a.reshape(-1, a.shape[-1])

    outs = _adamw_small([flat2(W[nm]) for nm in snames], [flat2(grads[nm]) for nm in snames],
                        [flat2(M[nm]) for nm in snames], [flat2(V[nm]) for nm in snames], "adamw_small")
    ns = len(snames)
    for k, nm in enumerate(snames):
        shp = W[nm].shape
        delta[nm], new_m[nm], new_v[nm] = (outs[k].reshape(shp), outs[ns + k].reshape(shp),
                                           outs[2 * ns + k].reshape(shp))

    return (loss, grad_x, *[grads[nm] for nm in names], *[delta[nm] for nm in names],
            *[new_m[nm] for nm in names], *[new_v[nm] for nm in names])
```

```python
import functools

import jax
import jax.numpy as jnp
from jax import lax
from jax.experimental import pallas as pl
from jax.experimental.pallas import tpu as pltpu

F32 = jnp.float32
BF16 = jnp.bfloat16
EPS = 1e-6
N_DEV = 8
N_CHIP = 4
MESH = pl.DeviceIdType.MESH
ANY = pl.BlockSpec(memory_space=pl.ANY)

VMEM_LIMIT_BYTES = 56 * 1024 * 1024
LANES = 128
HALO = 32
HEAD_DIM = 64
CHUNK = 128
CONV_WIDTH = 31
POOL_WINDOWS = (2, 4, 8, 16)

ADAM_LR = 0.001
ADAM_B1 = 0.9
ADAM_B2 = 0.999
ADAM_EPS = 1e-08
ADAM_WD = 0.01
ADAM_STEP = 10


def _cparams(sem=None):
    return pltpu.CompilerParams(dimension_semantics=sem, vmem_limit_bytes=VMEM_LIMIT_BYTES)


def _position():
    return lax.axis_index("x"), lax.axis_index("y"), lax.axis_index("c")


class _Copies:
    def __init__(self):
        self.local, self.sends, self.recvs = [], [], []

    def extend(self, other):
        self.local += other.local
        self.sends += other.sends
        self.recvs += other.recvs

    def start(self):
        for cp in self.local + self.sends:
            cp.start()

    def wait(self):
        for land, send_sems, recv_sems, k, peer in self.recvs:
            _remote(land, land, send_sems, recv_sems, k, peer).wait_recv()
        for cp in self.sends:
            cp.wait_send()
        for cp in self.local:
            cp.wait()


def _remote(src, dst, send_sems, recv_sems, k, to):
    return pltpu.make_async_remote_copy(src_ref=src, dst_ref=dst, send_sem=send_sems.at[k], recv_sem=recv_sems.at[k],
                                        device_id=to, device_id_type=MESH)


class _Payload:
    ins, out_shapes, aliases, n_remote, n_local = (), (), {}, 0, 0

    def sem_shapes(self):
        return [pltpu.SemaphoreType.DMA((max(self.n_remote, 1),)), pltpu.SemaphoreType.DMA((max(self.n_remote, 1),)),
                pltpu.SemaphoreType.DMA((max(self.n_local, 1),))]


class _GatherIci(_Payload):
    def __init__(self, shards):
        self.ins = list(shards)
        self.out_shapes = [jax.ShapeDtypeStruct((s.shape[0], N_DEV * s.shape[1], s.shape[2]), s.dtype) for s in shards]
        self.n_remote, self.n_local = 4 * len(shards), len(shards)

    def build(self, ins, outs, send_sems, recv_sems, local_sems, k0=0, l0=0):
        x, y, c = _position()
        peers = [(x, y, 1 - c), (1 - x, y, c), (x, 1 - y, c), (1 - x, 1 - y, c)]
        cps = _Copies()
        for a, (src, out) in enumerate(zip(ins, outs)):
            r = src.shape[1]

            def rows(px, py, pc, out=out, r=r):
                return out.at[:, pl.ds((4 * px + 2 * py + pc) * r, r), :]

            cps.local.append(pltpu.make_async_copy(src, rows(x, y, c), local_sems.at[l0 + a]))
            for k, peer in enumerate(peers):
                cps.sends.append(_remote(src, rows(x, y, c), send_sems, recv_sems, k0 + 4 * a + k, peer))
                cps.recvs.append((rows(*peer), send_sems, recv_sems, k0 + 4 * a + k, peer))
        return cps


class _GatherForward(_Payload):
    def __init__(self, partials):
        self.ins = list(partials)
        self.out_shapes = [jax.ShapeDtypeStruct(p.shape, p.dtype) for p in partials]
        self.aliases = {a: a for a in range(len(partials))}
        self.n_remote = 3 * len(partials)

    def build(self, ins, outs, send_sems, recv_sems, local_sems, k0=0, l0=0):
        x, y, c = _position()
        chips = [(1 - x, y), (x, 1 - y), (1 - x, 1 - y)]
        cps = _Copies()
        for a, out in enumerate(outs):
            r = out.shape[1] // N_DEV
            for k, (px, py) in enumerate(chips):
                mine = out.at[:, pl.ds((4 * px + 2 * py + c) * r, r), :]
                theirs = out.at[:, pl.ds((4 * px + 2 * py + 1 - c) * r, r), :]
                cps.sends.append(_remote(mine, mine, send_sems, recv_sems, k0 + 3 * a + k, (x, y, 1 - c)))
                cps.recvs.append((theirs, send_sems, recv_sems, k0 + 3 * a + k, (x, y, 1 - c)))
        return cps


class _PairExchange(_Payload):
    def __init__(self, grads):
        self.ins = list(grads)
        self.out_shapes = [jax.ShapeDtypeStruct((g.shape[0] // 2, g.shape[1]), g.dtype) for g in grads]
        self.n_remote = N_CHIP * len(grads)

    def build(self, ins, outs, send_sems, recv_sems, local_sems, k0=0, l0=0):
        x, y, c = _position()
        cps = _Copies()
        for a, (src, out) in enumerate(zip(ins, outs)):
            r = src.shape[0] // N_DEV
            for q in range(N_CHIP):
                land = out.at[pl.ds(q * r, r), :]
                cps.sends.append(_remote(src.at[pl.ds((2 * q + 1 - c) * r, r), :], land, send_sems, recv_sems,
                                         k0 + N_CHIP * a + q, (x, y, 1 - c)))
                cps.recvs.append((land, send_sems, recv_sems, k0 + N_CHIP * a + q, (x, y, 1 - c)))
        return cps


class _ChipExchange(_Payload):
    def __init__(self, sums):
        self.ins = list(sums)
        self.out_shapes = [jax.ShapeDtypeStruct(s.shape, s.dtype) for s in sums]
        self.n_remote, self.n_local = 3 * len(sums), len(sums)

    def build(self, ins, outs, send_sems, recv_sems, local_sems, k0=0, l0=0):
        x, y, c = _position()
        my_chip = 2 * x + y
        chips = [(1 - x, y), (x, 1 - y), (1 - x, 1 - y)]
        cps = _Copies()
        for a, (src, out) in enumerate(zip(ins, outs)):
            r = src.shape[0] // N_CHIP
            mine = out.at[pl.ds(my_chip * r, r), :]
            cps.local.append(pltpu.make_async_copy(src.at[pl.ds(my_chip * r, r), :], mine, local_sems.at[l0 + a]))
            for k, (px, py) in enumerate(chips):
                land = out.at[pl.ds((2 * px + py) * r, r), :]
                cps.sends.append(_remote(src.at[pl.ds((2 * px + py) * r, r), :], mine, send_sems, recv_sems,
                                         k0 + 3 * a + k, (px, py, c)))
                cps.recvs.append((land, send_sems, recv_sems, k0 + 3 * a + k, (px, py, c)))
        return cps


class _Merged(_Payload):
    def __init__(self, parts):
        self.parts = list(parts)
        self.ins = [a for p in parts for a in p.ins]
        self.out_shapes = [s for p in parts for s in p.out_shapes]
        self.aliases, self.offsets = {}, []
        i0 = o0 = k0 = l0 = 0
        for p in parts:
            self.offsets.append((i0, o0, k0, l0))
            self.aliases.update({i0 + i: o0 + o for i, o in p.aliases.items()})
            i0, o0, k0, l0 = i0 + len(p.ins), o0 + len(p.out_shapes), k0 + p.n_remote, l0 + p.n_local
        self.n_remote, self.n_local = k0, l0

    def build(self, ins, outs, send_sems, recv_sems, local_sems):
        cps = _Copies()
        for p, (i0, o0, k0, l0) in zip(self.parts, self.offsets):
            cps.extend(p.build(ins[i0:i0 + len(p.ins)], outs[o0:o0 + len(p.out_shapes)], send_sems, recv_sems,
                               local_sems, k0, l0))
        return cps


RELAY_AT = 0.88


def _call(body, name, grid, in_specs, out_specs, out_shape, scratch_shapes, semantics, args, payload=None,
          relay=None):
    if payload is None:
        outs = pl.pallas_call(body, name=name, grid=grid, in_specs=in_specs, out_specs=out_specs,
                              out_shape=out_shape, scratch_shapes=scratch_shapes,
                              compiler_params=_cparams(semantics))(*args)
        return list(outs), []
    n_in, n_out, n_scr = len(in_specs), len(out_specs), len(scratch_shapes)
    p_in, p_out = len(payload.ins), len(payload.out_shapes)

    def carried(*refs):
        ins, p_ins = refs[:n_in], refs[n_in:n_in + p_in]
        o0 = n_in + p_in
        outs, p_outs = refs[o0:o0 + n_out], refs[o0 + n_out:o0 + n_out + p_out]
        s0 = o0 + n_out + p_out
        scr, sems = refs[s0:s0 + n_scr], refs[s0 + n_scr:s0 + n_scr + 3]
        relay_sems = refs[s0 + n_scr + 3:]
        ids = [pl.program_id(k) for k in range(len(grid))]
        at_first = functools.reduce(jnp.logical_and, [i == 0 for i in ids])
        at_last = functools.reduce(jnp.logical_and, [i == g - 1 for i, g in zip(ids, grid)])

        @pl.when(at_first)
        def _():
            payload.build(p_ins, p_outs, *sems).start()

        body(*ins, *outs, *scr)

        if relay is None:
            @pl.when(at_last)
            def _():
                payload.build(p_ins, p_outs, *sems).wait()
        else:
            n_relay = len(relay.out_shapes)
            step = functools.reduce(lambda acc, ig: acc * ig[1] + ig[0], zip(ids, grid), 0)
            total = functools.reduce(lambda a, b: a * b, grid)

            @pl.when(step == int(RELAY_AT * total))
            def _():
                payload.build(p_ins, p_outs, *sems).wait()
                relay.build(p_outs[:n_relay], p_outs[:n_relay], *relay_sems).start()

            @pl.when(at_last)
            def _():
                relay.build(p_outs[:n_relay], p_outs[:n_relay], *relay_sems).wait()

    outs = pl.pallas_call(
        carried, name=name, grid=grid, in_specs=list(in_specs) + [ANY] * p_in,
        out_specs=list(out_specs) + [ANY] * p_out, out_shape=list(out_shape) + list(payload.out_shapes),
        scratch_shapes=list(scratch_shapes) + payload.sem_shapes() + (relay.sem_shapes() if relay else []),
        input_output_aliases={n_in + i: n_out + o for i, o in payload.aliases.items()},
        compiler_params=_cparams(("arbitrary",) * len(grid)))(*args, *payload.ins)
    return list(outs[:n_out]), list(outs[n_out:])


def _comm(payload, name):
    def body(*refs):
        p_in, p_out = len(payload.ins), len(payload.out_shapes)
        cps = payload.build(refs[:p_in], refs[p_in:p_in + p_out], *refs[p_in + p_out:])
        cps.start()
        cps.wait()

    return list(pl.pallas_call(
        body, name=name, in_specs=[ANY] * len(payload.ins), out_specs=[ANY] * len(payload.out_shapes),
        out_shape=list(payload.out_shapes), scratch_shapes=payload.sem_shapes(),
        input_output_aliases=dict(payload.aliases))(*payload.ins))


def _dot(a, b):
    return jnp.dot(a, b, preferred_element_type=F32)


def _dot_nt(a, b):
    return lax.dot_general(a, b, (((1,), (1,)), ((), ())), preferred_element_type=F32)


def _dot_tn(a, b):
    return lax.dot_general(a, b, (((0,), (0,)), ((), ())), preferred_element_type=F32)


def _split_dot(x, e):
    hi = x.astype(BF16)
    r1 = x - hi.astype(F32)
    mid = r1.astype(BF16)
    lo = (r1 - mid.astype(F32)).astype(BF16)
    return _dot(hi, e) + _dot(mid, e) + _dot(lo, e)


def _rms(x):
    rstd = lax.rsqrt(jnp.mean(x * x, axis=-1, keepdims=True) + EPS)
    return x * rstd, rstd


def _rms_bwd(xhat, rstd, g, dh):
    dxhat = dh * g
    dx = rstd * (dxhat - xhat * jnp.mean(dxhat * xhat, axis=-1, keepdims=True))
    return dx, jnp.sum(dh * xhat, axis=0, keepdims=True)


def _ln(v):
    mu = jnp.mean(v, axis=-1, keepdims=True)
    xc = v - mu
    rstd = lax.rsqrt(jnp.mean(xc * xc, axis=-1, keepdims=True) + EPS)
    return xc * rstd, rstd


def _ln_bwd(vhat, rstd, g, dy):
    dvhat = dy * g
    dv = rstd * (dvhat - jnp.mean(dvhat, axis=-1, keepdims=True)
                 - vhat * jnp.mean(dvhat * vhat, axis=-1, keepdims=True))
    return dv, jnp.sum(dy * vhat, axis=0, keepdims=True), jnp.sum(dy, axis=0, keepdims=True)


_INV_SQRT2 = 0.7071067811865476
_INV_SQRT2PI = 0.3989422804014327


def _gelu(x):
    return 0.5 * x * (1.0 + lax.erf(x * _INV_SQRT2))


def _gelu_grad(x):
    return 0.5 * (1.0 + lax.erf(x * _INV_SQRT2)) + x * jnp.exp(-0.5 * x * x) * _INV_SQRT2PI


ROW_CHUNK = 128


def _for_row_chunks(n_rows, fn):
    def trip(k, carry):
        fn(pl.ds(pl.multiple_of(k * ROW_CHUNK, ROW_CHUNK), ROW_CHUNK))
        return carry

    lax.fori_loop(0, n_rows // ROW_CHUNK, trip, 0)


def _silu_grad(x):
    s = jax.nn.sigmoid(x)
    return s * (1.0 + x * (1.0 - s))


def _ffn_fwd(x, g, wa, name, payload=None, head=None, relay=None):
    t, d = x.shape
    f = wa.shape[1]
    tm, tf = 1024, 256
    nc = f // tf
    groups = [slice(k * (tm // 2), (k + 1) * (tm // 2)) for k in range(2)]

    def body(x_ref, g_ref, wgu_ref, wd_ref, *rest):
        if head is None:
            xo_ref, gate_ref, up_ref, act_ref, h_scr, acc_scr = rest
        else:
            fg_ref, tgt_ref, xo_ref, gate_ref, up_ref, act_ref, dfg_ref, loss_ref, h_scr, acc_scr = rest
        c = pl.program_id(1)
        if head is not None:
            @pl.when((c == 0) & (pl.program_id(0) == 0))
            def _():
                dfg_ref[...] = jnp.zeros_like(dfg_ref)
                loss_ref[...] = jnp.zeros_like(loss_ref)

        @pl.when(c == 0)
        def _():
            def enter(rows):
                xhat, _ = _rms(x_ref[rows, :])
                h_scr[rows, :] = (xhat * g_ref[...]).astype(BF16)
                acc_scr[rows, :] = jnp.zeros((ROW_CHUNK, d), F32)

            _for_row_chunks(tm, enter)

        wgu, wd = wgu_ref[...].reshape(2 * tf, d), wd_ref[...]
        for rows in groups:
            gu = _dot_nt(h_scr[rows, :], wgu)
            gate, up = gu[:, :tf], gu[:, tf:]
            gate_ref[rows, :] = gate.astype(BF16)
            up_ref[rows, :] = up.astype(BF16)
            act = (gate * jax.nn.sigmoid(gate) * up).astype(BF16)
            act_ref[rows, :] = act
            acc_scr[rows, :] += _dot(act, wd)

        @pl.when(c == nc - 1)
        def _():
            def leave(rows):
                xo = x_ref[rows, :] + 0.5 * acc_scr[rows, :]
                if head is None:
                    xo_ref[rows, :] = xo
                else:
                    fg = fg_ref[...]
                    xhat, rstd = _rms(xo)
                    err = xhat * fg - tgt_ref[rows, :]
                    dxn, dfg = _rms_bwd(xhat, rstd, fg, err * (1.0 / d))
                    xo_ref[rows, :] = dxn
                    dfg_ref[...] += dfg
                    loss_ref[...] += (jnp.zeros_like(loss_ref)
                                      + 0.5 * jnp.sum(jnp.mean(err * err, axis=-1, keepdims=True)))

            _for_row_chunks(tm, leave)

    row = pl.BlockSpec((tm, d), lambda i, c: (i, 0))
    vec = pl.BlockSpec((1, d), lambda i, c: (0, 0))
    in_specs = [row, vec, pl.BlockSpec((2, tf, d), lambda i, c: (0, c, 0)),
                pl.BlockSpec((None, tf, d), lambda i, c: (2, c, 0))]
    out_specs = [row] + [pl.BlockSpec((tm, tf), lambda i, c: (i, c))] * 3
    out_shape = [jax.ShapeDtypeStruct((t, d), F32)] + [jax.ShapeDtypeStruct((t, f), BF16)] * 3
    args = (x, g, wa, wa)
    if head is not None:
        in_specs += [vec, row]
        out_specs += [vec, pl.BlockSpec((1, LANES), lambda i, c: (0, 0))]
        out_shape += [jax.ShapeDtypeStruct((1, d), F32), jax.ShapeDtypeStruct((1, LANES), F32)]
        args += tuple(head)
    return _call(
        body, name, (t // tm, nc), in_specs, out_specs, out_shape,
        [pltpu.VMEM((tm, d), BF16), pltpu.VMEM((tm, d), F32)],
        ("parallel" if head is None else "arbitrary", "arbitrary"), args, payload, relay)


def _ffn_bwd(x, g, dxo, gate, up, wa, name, payload=None):
    t, d = x.shape
    f = wa.shape[1]
    tm, tf = 1024, 256
    nc = f // tf
    groups = [slice(k * (tm // 2), (k + 1) * (tm // 2)) for k in range(2)]

    def body(x_ref, g_ref, dxo_ref, gate_ref, up_ref, wgu_ref, wd_ref,
             dx_ref, dgate_ref, dup_ref, h_ref, dy_ref, dg_ref, acc_scr):
        i, c = pl.program_id(0), pl.program_id(1)

        @pl.when(c == 0)
        def _():
            def enter(rows):
                xhat, _ = _rms(x_ref[rows, :])
                h_ref[rows, :] = (xhat * g_ref[...]).astype(BF16)
                dy_ref[rows, :] = (0.5 * dxo_ref[rows, :]).astype(BF16)
                acc_scr[rows, :] = jnp.zeros((ROW_CHUNK, d), F32)

            _for_row_chunks(tm, enter)

        @pl.when((c == 0) & (i == 0))
        def _():
            dg_ref[...] = jnp.zeros_like(dg_ref)

        wg, wu, wd = wgu_ref[0], wgu_ref[1], wd_ref[...]
        for rows in groups:
            gt = gate_ref[rows, :].astype(F32)
            u = up_ref[rows, :].astype(F32)
            s = jax.nn.sigmoid(gt)
            silu = gt * s
            dact = _dot_nt(dy_ref[rows, :], wd)
            dgate = (dact * u * (s * (1.0 + gt * (1.0 - s)))).astype(BF16)
            dup = (dact * silu).astype(BF16)
            dgate_ref[rows, :] = dgate
            dup_ref[rows, :] = dup
            acc_scr[rows, :] += _dot(dgate, wg) + _dot(dup, wu)

        @pl.when(c == nc - 1)
        def _():
            def leave(rows):
                xhat, rstd = _rms(x_ref[rows, :])
                dxn, dg = _rms_bwd(xhat, rstd, g_ref[...], acc_scr[rows, :])
                dx_ref[rows, :] = dxo_ref[rows, :] + dxn
                dg_ref[...] += dg

            _for_row_chunks(tm, leave)

    row = pl.BlockSpec((tm, d), lambda i, c: (i, 0))
    col = pl.BlockSpec((tm, tf), lambda i, c: (i, c))
    vec = pl.BlockSpec((1, d), lambda i, c: (0, 0))
    return _call(
        body, name, (t // tm, nc),
        [row, vec, row, col, col, pl.BlockSpec((2, tf, d), lambda i, c: (0, c, 0)),
         pl.BlockSpec((None, tf, d), lambda i, c: (2, c, 0))],
        [row, col, col, row, row, vec],
        [jax.ShapeDtypeStruct((t, d), F32), jax.ShapeDtypeStruct((t, f), BF16),
         jax.ShapeDtypeStruct((t, f), BF16),
         jax.ShapeDtypeStruct((t, d), BF16), jax.ShapeDtypeStruct((t, d), BF16),
         jax.ShapeDtypeStruct((1, d), F32)],
        [pltpu.VMEM((tm, d), F32)],
        ("arbitrary", "arbitrary"), (x, g, dxo, gate, up, wa, wa), payload)


def _ffn_dw(dgate, dup, act, h, dy, name, payload=None):
    t, f = dgate.shape
    d = h.shape[1]
    tk = 512
    tmm = f // 2
    nk = t // tk

    def body(dg_ref, du_ref, a_ref, h_ref, dy_ref, og_ref, ou_ref, od_ref, acc_g, acc_u, acc_d):
        k = pl.program_id(1)

        @pl.when(k == 0)
        def _():
            acc_g[...] = jnp.zeros_like(acc_g)
            acc_u[...] = jnp.zeros_like(acc_u)
            acc_d[...] = jnp.zeros_like(acc_d)

        hv = h_ref[...]
        acc_g[...] += _dot_tn(dg_ref[...], hv)
        acc_u[...] += _dot_tn(du_ref[...], hv)
        acc_d[...] += _dot_tn(a_ref[...], dy_ref[...])

        @pl.when(k == nk - 1)
        def _():
            og_ref[...] = acc_g[...].astype(BF16)
            ou_ref[...] = acc_u[...].astype(BF16)
            od_ref[...] = acc_d[...].astype(BF16)

    col = pl.BlockSpec((tk, tmm), lambda j, k: (k, j))
    row = pl.BlockSpec((tk, d), lambda j, k: (k, 0))
    out = pl.BlockSpec((tmm, d), lambda j, k: (j, 0))
    return _call(
        body, name, (f // tmm, nk), [col, col, col, row, row], [out, out, out],
        [jax.ShapeDtypeStruct((f, d), BF16)] * 3, [pltpu.VMEM((tmm, d), F32)] * 3,
        ("parallel", "arbitrary"), (dgate, dup, act, h, dy), payload)


def _tn_matmul(a, b, name, payload=None):
    t, m = a.shape
    n = b.shape[1]
    tk = 1024
    tmm = m // 2 if (m > 2048 and (m // 2) % LANES == 0) else m
    nk = t // tk

    def body(a_ref, b_ref, o_ref, acc_scr):
        k = pl.program_id(1)

        @pl.when(k == 0)
        def _():
            acc_scr[...] = jnp.zeros_like(acc_scr)

        acc_scr[...] += _dot_tn(a_ref[...].astype(BF16), b_ref[...].astype(BF16))

        @pl.when(k == nk - 1)
        def _():
            o_ref[...] = acc_scr[...].astype(BF16)

    (out,), p_outs = _call(
        body, name, (m // tmm, nk),
        [pl.BlockSpec((tk, tmm), lambda j, k: (k, j)), pl.BlockSpec((tk, n), lambda j, k: (k, 0))],
        [pl.BlockSpec((tmm, n), lambda j, k: (j, 0))],
        [jax.ShapeDtypeStruct((m, n), BF16)],
        [pltpu.VMEM((tmm, n), F32)],
        ("parallel", "arbitrary"), (a, b), payload)
    return out, p_outs


def _lane_ids(shape):
    return lax.broadcasted_iota(jnp.int32, shape, 1)


def _tril(w):
    r = lax.broadcasted_iota(jnp.int32, w.shape, 0)
    c = lax.broadcasted_iota(jnp.int32, w.shape, 1)
    return jnp.where(r >= c, w, 0.0)


def _shift_down(x, k):
    return x if k == 0 else pltpu.roll(x, k, 0)


def _shift_up(x, k):
    return x if k == 0 else pltpu.roll(x, x.shape[0] - k, 0)


def _sub_tile_shifts(ext, shift):
    return [shift(ext, b) for b in range(8)]


def _tap(shifted, j, n_out, down):
    a, b = divmod(j, 8)
    r0 = HALO - 8 * a if down else 8 * a
    return shifted[b][r0:r0 + n_out]


def _depthwise(shifted, w, n_out, down):
    acc = None
    for j in range(CONV_WIDTH):
        term = _tap(shifted, j, n_out, down) * w[CONV_WIDTH - 1 - j:CONV_WIDTH - j]
        acc = term if acc is None else acc + term
    return acc


def _conv_wgrad(shifted, dhc, n_out):
    return [jnp.sum(_tap(shifted, CONV_WIDTH - 1 - k, n_out, True) * dhc, axis=0, keepdims=True)
            for k in range(CONV_WIDTH)]


def _window_sums(ext, shift):
    assert POOL_WINDOWS == (2, 4, 8, 16)
    s2 = ext + shift(ext, 1)
    s4 = s2 + shift(s2, 2)
    s8 = s4 + shift(s4, 4)
    s16 = s8 + shift(s8, 8)
    grp = _lane_ids(ext.shape) // HEAD_DIM
    return jnp.where(grp == 0, s2, jnp.where(grp == 1, s4, jnp.where(grp == 2, s8, s16)))


def _pool_count(t0, n, width):
    pos = (lax.broadcasted_iota(jnp.int32, (n, width), 0) + (t0 + 1)).astype(F32)
    grp = _lane_ids((n, width)) // HEAD_DIM
    w0, w1, w2, w3 = (float(w) for w in POOL_WINDOWS)
    win = jnp.where(grp == 0, w0, jnp.where(grp == 1, w1, jnp.where(grp == 2, w2, w3)))
    return jnp.minimum(pos, win)


def _block_diag(pw):
    gn, cg, _ = pw.shape
    rows = []
    for gi in range(gn):
        parts = [pw[gi] if gj == gi else jnp.zeros((cg, cg), pw.dtype) for gj in range(gn)]
        rows.append(jnp.concatenate(parts, axis=1))
    return jnp.concatenate(rows, axis=0)


def _head_pair_mix(w_even, w_odd, v):
    lo = _lane_ids((CHUNK, LANES)) < HEAD_DIM
    return jnp.where(lo, _dot(w_even, v), _dot(w_odd, v))


def _mixer_fwd(x, g, wb, wc, p, name, payload=None):
    t, d = x.shape
    d_in = wb.shape[1]
    sgu = p["sgu_ln_g"].shape[1]
    pool = p["pool_scale"].shape[1]
    d_mix = 2 * sgu + pool
    tm = 512
    n_i = t // tm
    hb = tm // HALO

    def body(x_ref, xp_ref, g_ref, wi_ref, wo_ref, lng_ref, lnb_ref, ws_ref, bs_ref, cw_ref, cb_ref, clg_ref,
             clb_ref, bd_ref, ps_ref, xo_ref, z_ref, hc_ref, cat_ref):
        i = pl.program_id(0)
        first = i == 0
        gain, wi = g_ref[...], wi_ref[...]

        def project(xv):
            xhat, _ = _rms(xv)
            return _dot_nt((xhat * gain).astype(BF16), wi)

        z_main = project(x_ref[...])
        z_ref[...] = z_main
        z_prev = jnp.where(first, 0.0, project(xp_ref[...]))

        lng, lnb = lng_ref[...], lnb_ref[...]
        wt = [_tril(ws_ref[h]).astype(BF16) for h in range(sgu // HEAD_DIM)]
        for n in range(tm // CHUNK):
            rows = slice(n * CHUNK, (n + 1) * CHUNK)
            u = _gelu(z_main[rows, 0:sgu])
            vhat, _ = _ln(_gelu(z_main[rows, sgu:2 * sgu]))
            vn = (vhat * lng + lnb).astype(BF16)
            for gp in range(sgu // LANES):
                ls = slice(gp * LANES, (gp + 1) * LANES)
                mixed = _head_pair_mix(wt[2 * gp], wt[2 * gp + 1], vn[:, ls]) + bs_ref[:, ls]
                cat_ref[rows, ls] = (u[:, ls] * mixed).astype(BF16)

        def glu(zz):
            return zz[:, 2 * sgu:3 * sgu] * jax.nn.sigmoid(zz[:, 3 * sgu:4 * sgu])

        ext = jnp.concatenate([glu(z_prev), glu(z_main)], axis=0)
        hc = _depthwise(_sub_tile_shifts(ext, _shift_down), cw_ref[...], tm, True) + cb_ref[...]
        hc_ref[...] = hc
        hhat, _ = _ln(hc)
        bn = hhat * clg_ref[...] + clb_ref[...]
        cat_ref[:, sgu:2 * sgu] = (bn * jax.nn.sigmoid(bn)).astype(BF16)

        pext = jnp.concatenate([z_prev[:, 4 * sgu:], z_main[:, 4 * sgu:]], axis=0)
        sums = _window_sums(pext, _shift_down)[HALO:]
        pooled = sums / _pool_count(i * tm, tm, pool) - z_main[:, 4 * sgu:]
        mixed_c = _dot(pooled.astype(BF16), bd_ref[...].astype(BF16))
        cat_ref[:, 2 * sgu:] = (mixed_c * ps_ref[...]).astype(BF16)

        xo_ref[...] = x_ref[...] + _dot(cat_ref[...], wo_ref[...])

    def vec(n):
        return pl.BlockSpec((1, n), lambda i: (0, 0))

    return _call(
        body, name, (n_i,),
        [pl.BlockSpec((tm, d), lambda i: (i, 0)),
         pl.BlockSpec((HALO, d), lambda i: (jnp.maximum(i * hb - 1, 0), 0)),
         vec(d),
         pl.BlockSpec((None, d_in, d), lambda i: (0, 0, 0)), pl.BlockSpec((None, d_mix, d), lambda i: (0, 0, 0)),
         vec(sgu), vec(sgu),
         pl.BlockSpec(p["w_spatial"].shape, lambda i: (0, 0, 0)),
         pl.BlockSpec((CHUNK, sgu), lambda i: (0, 0)),
         pl.BlockSpec((CONV_WIDTH, sgu), lambda i: (0, 0)),
         vec(sgu), vec(sgu), vec(sgu),
         pl.BlockSpec((pool, pool), lambda i: (0, 0)), vec(pool)],
        [pl.BlockSpec((tm, d), lambda i: (i, 0)), pl.BlockSpec((tm, d_in), lambda i: (i, 0)),
         pl.BlockSpec((tm, sgu), lambda i: (i, 0)), pl.BlockSpec((tm, d_mix), lambda i: (i, 0))],
        [jax.ShapeDtypeStruct((t, d), F32), jax.ShapeDtypeStruct((t, d_in), F32),
         jax.ShapeDtypeStruct((t, sgu), F32), jax.ShapeDtypeStruct((t, d_mix), BF16)], [], ("parallel",),
        (x, x, g, wb, wc, p["sgu_ln_g"], p["sgu_ln_b"], p["w_spatial"], p["bs_full"], p["conv_w"], p["conv_b"],
         p["conv_ln_g"], p["conv_ln_b"], p["bd"], p["pool_scale"]), payload)


_R_SGU_G, _R_SGU_B, _R_CONV_B, _R_CLN_G, _R_CLN_B, _R_CONV_W = 0, 1, 2, 3, 4, 8
_R384_ROWS = 40


def _mixer_bwd(x, g, z, hc_saved, dxo, wb, wc, p, name, payload=None):
    t, d_in = z.shape
    d = x.shape[1]
    sgu = p["sgu_ln_g"].shape[1]
    pool = p["pool_scale"].shape[1]
    d_mix = 2 * sgu + pool
    n_head = sgu // HEAD_DIM
    tm = 512
    n_i = t // tm
    hb = tm // HALO

    def body(x_ref, g_ref, wi_ref, z_ref, zp_ref, zn_ref, dxo_ref, dxon_ref, wo_ref, hc_ref, hcn_ref, lng_ref,
             lnb_ref, ws_ref, bs_ref, cw_ref, clg_ref, clb_ref, bd_ref, ps_ref,
             dx_ref, dz_ref, hm_ref, dgm_ref, g384_ref, gws_ref, gpool_ref, dbs_scr):
        i = pl.program_id(0)
        first, last = i == 0, i == n_i - 1

        @pl.when(first)
        def _():
            dgm_ref[...] = jnp.zeros_like(dgm_ref)
            g384_ref[...] = jnp.zeros_like(g384_ref)
            gws_ref[...] = jnp.zeros_like(gws_ref)
            gpool_ref[...] = jnp.zeros_like(gpool_ref)
            dbs_scr[...] = jnp.zeros_like(dbs_scr)

        z_main = z_ref[...]
        z_prev = jnp.where(first, 0.0, zp_ref[...])
        z_next = jnp.where(last, 0.0, zn_ref[...])
        wo = wo_ref[...]
        dc_main = _dot_nt(dxo_ref[...].astype(BF16), wo)
        dc_next = jnp.where(last, 0.0, _dot_nt(dxon_ref[...].astype(BF16), wo))

        lng, lnb = lng_ref[...], lnb_ref[...]
        wt = [_tril(ws_ref[h]) for h in range(n_head)]
        wt_b = [w.astype(BF16) for w in wt]
        wtt_b = [w.T.astype(BF16) for w in wt]
        lo = _lane_ids((CHUNK, LANES)) < HEAD_DIM
        d_lng = jnp.zeros((1, sgu), F32)
        d_lnb = jnp.zeros((1, sgu), F32)
        dws = [jnp.zeros((CHUNK, CHUNK), F32) for _ in range(n_head)]
        for n in range(tm // CHUNK):
            rows = slice(n * CHUNK, (n + 1) * CHUNK)
            au, av = z_main[rows, 0:sgu], z_main[rows, sgu:2 * sgu]
            u = _gelu(au)
            vhat, vrstd = _ln(_gelu(av))
            vn = (vhat * lng + lnb).astype(BF16)
            da = dc_main[rows, 0:sgu]
            dmixed = da * u
            dbs_scr[...] += dmixed
            dvn_parts, du_parts = [], []
            for gp in range(sgu // LANES):
                ls = slice(gp * LANES, (gp + 1) * LANES)
                vn_g = vn[:, ls]
                mixed = _head_pair_mix(wt_b[2 * gp], wt_b[2 * gp + 1], vn_g) + bs_ref[:, ls]
                du_parts.append(da[:, ls] * mixed)
                dm_g = dmixed[:, ls]
                dm_b = dm_g.astype(BF16)
                dvn_parts.append(jnp.where(lo, _dot(wtt_b[2 * gp], dm_b), _dot(wtt_b[2 * gp + 1], dm_b)))
                dws[2 * gp] = dws[2 * gp] + _dot_nt(jnp.where(lo, dm_g, 0.0).astype(BF16), vn_g)
                dws[2 * gp + 1] = dws[2 * gp + 1] + _dot_nt(jnp.where(lo, 0.0, dm_g).astype(BF16), vn_g)
            dvn = jnp.concatenate(dvn_parts, axis=1)
            du = jnp.concatenate(du_parts, axis=1)
            dv, dg_n, db_n = _ln_bwd(vhat, vrstd, lng, dvn)
            d_lng = d_lng + dg_n
            d_lnb = d_lnb + db_n
            dz_ref[rows, 0:sgu] = (du * _gelu_grad(au)).astype(BF16)
            dz_ref[rows, sgu:2 * sgu] = (dv * _gelu_grad(av)).astype(BF16)
        for h in range(n_head):
            gws_ref[h] += _tril(dws[h])
        g384_ref[_R_SGU_G:_R_SGU_G + 1, :] += d_lng
        g384_ref[_R_SGU_B:_R_SGU_B + 1, :] += d_lnb

        clg = clg_ref[...]
        bcols = slice(2 * sgu, 4 * sgu)
        zb = jnp.concatenate([z_prev[:, bcols], z_main[:, bcols], z_next[:, bcols]], axis=0)
        bval, bgate = zb[:, 0:sgu], zb[:, sgu:2 * sgu]
        sg = jax.nn.sigmoid(bgate)
        hglu = bval * sg
        n_out = tm + HALO
        hglu_shifts = _sub_tile_shifts(hglu, _shift_down)
        cw = cw_ref[...]
        hc = jnp.concatenate([hc_ref[...], jnp.where(last, 0.0, hcn_ref[...])], axis=0)
        hhat, hrstd = _ln(hc)
        bn = hhat * clg + clb_ref[...]
        db = jnp.concatenate([dc_main[:, sgu:2 * sgu], dc_next[:, sgu:2 * sgu]], axis=0)
        dbn = db * _silu_grad(bn)
        dhc_all, _, _ = _ln_bwd(hhat, hrstd, clg, dbn)
        dbn_m, hhat_m, dhc = dbn[:tm], hhat[:tm], dhc_all[:tm]
        g384_ref[_R_CLN_G:_R_CLN_G + 1, :] += jnp.sum(dbn_m * hhat_m, axis=0, keepdims=True)
        g384_ref[_R_CLN_B:_R_CLN_B + 1, :] += jnp.sum(dbn_m, axis=0, keepdims=True)
        g384_ref[_R_CONV_B:_R_CONV_B + 1, :] += jnp.sum(dhc, axis=0, keepdims=True)
        wrows = _conv_wgrad(hglu_shifts, dhc, tm)
        for k in range(CONV_WIDTH):
            g384_ref[_R_CONV_W + k:_R_CONV_W + k + 1, :] += wrows[k]
        dhglu = _depthwise(_sub_tile_shifts(dhc_all, _shift_up), cw, tm, False)
        bval_m, sg_m = bval[HALO:HALO + tm], sg[HALO:HALO + tm]
        dz_ref[:, 2 * sgu:3 * sgu] = (dhglu * sg_m).astype(BF16)
        dz_ref[:, 3 * sgu:4 * sgu] = (dhglu * bval_m * sg_m * (1.0 - sg_m)).astype(BF16)

        bd_b = bd_ref[...].astype(BF16)
        ps = ps_ref[...]
        p_main = z_main[:, 4 * sgu:]
        pext = jnp.concatenate([z_prev[:, 4 * sgu:], p_main], axis=0)
        cnt = _pool_count(i * tm, n_out, pool)
        pooled = _window_sums(pext, _shift_down)[HALO:] / cnt[:tm] - p_main
        pooled_b = pooled.astype(BF16)
        dcc = jnp.concatenate([dc_main[:, 2 * sgu:], dc_next[:, 2 * sgu:]], axis=0)
        dmix_c = dcc * ps
        mixed_c = _dot(pooled_b, bd_b)
        grp_r = lax.broadcasted_iota(jnp.int32, (pool, pool), 0) // HEAD_DIM
        grp_c = lax.broadcasted_iota(jnp.int32, (pool, pool), 1) // HEAD_DIM
        gpool_ref[0:pool, :] += jnp.where(grp_r == grp_c, _dot_tn(pooled_b, dmix_c[:tm].astype(BF16)), 0.0)
        gpool_ref[pool:pool + 1, :] += jnp.sum(dcc[:tm] * mixed_c, axis=0, keepdims=True)
        dpooled = _dot_nt(dmix_c.astype(BF16), bd_b)
        q = dpooled / cnt
        dp = _window_sums(q, _shift_up)[:tm] - dpooled[:tm]
        dz_ref[:, 4 * sgu:] = dp.astype(BF16)

        gain = g_ref[...]
        xhat, rstd = _rms(x_ref[...])
        hm_ref[...] = (xhat * gain).astype(BF16)
        dxn, dgm = _rms_bwd(xhat, rstd, gain, _dot(dz_ref[...], wi_ref[...]))
        dx_ref[...] = dxo_ref[...] + dxn
        dgm_ref[...] += dgm

        @pl.when(last)
        def _():
            r = lax.broadcasted_iota(jnp.int32, (sgu, LANES), 0)
            c = lax.broadcasted_iota(jnp.int32, (sgu, LANES), 1)
            sel = (r // HEAD_DIM == c).astype(BF16)
            gws_ref[n_head] = _split_dot(dbs_scr[...], sel)

    def vec(n):
        return pl.BlockSpec((1, n), lambda i: (0, 0))

    def prev_map(i):
        return (jnp.maximum(i * hb - 1, 0), 0)

    def next_map(i):
        return (jnp.minimum((i + 1) * hb, n_i * hb - 1), 0)

    return _call(
        body, name, (n_i,),
        [pl.BlockSpec((tm, d), lambda i: (i, 0)), pl.BlockSpec((1, d), lambda i: (0, 0)),
         pl.BlockSpec((None, d_in, d), lambda i: (0, 0, 0)),
         pl.BlockSpec((tm, d_in), lambda i: (i, 0)),
         pl.BlockSpec((HALO, d_in), prev_map), pl.BlockSpec((HALO, d_in), next_map),
         pl.BlockSpec((tm, d), lambda i: (i, 0)), pl.BlockSpec((HALO, d), next_map),
         pl.BlockSpec((None, d_mix, d), lambda i: (0, 0, 0)),
         pl.BlockSpec((tm, sgu), lambda i: (i, 0)), pl.BlockSpec((HALO, sgu), next_map),
         vec(sgu), vec(sgu),
         pl.BlockSpec(p["w_spatial"].shape, lambda i: (0, 0, 0)),
         pl.BlockSpec((CHUNK, sgu), lambda i: (0, 0)),
         pl.BlockSpec((CONV_WIDTH, sgu), lambda i: (0, 0)),
         vec(sgu), vec(sgu),
         pl.BlockSpec((pool, pool), lambda i: (0, 0)), vec(pool)],
        [pl.BlockSpec((tm, d), lambda i: (i, 0)), pl.BlockSpec((tm, d_in), lambda i: (i, 0)),
         pl.BlockSpec((tm, d), lambda i: (i, 0)), pl.BlockSpec((1, d), lambda i: (0, 0)),
         pl.BlockSpec((_R384_ROWS, sgu), lambda i: (0, 0)),
         pl.BlockSpec((n_head + 1, CHUNK, CHUNK), lambda i: (0, 0, 0)),
         pl.BlockSpec((pool + 8, pool), lambda i: (0, 0))],
        [jax.ShapeDtypeStruct((t, d), F32), jax.ShapeDtypeStruct((t, d_in), BF16),
         jax.ShapeDtypeStruct((t, d), BF16), jax.ShapeDtypeStruct((1, d), F32),
         jax.ShapeDtypeStruct((_R384_ROWS, sgu), F32),
         jax.ShapeDtypeStruct((n_head + 1, CHUNK, CHUNK), F32),
         jax.ShapeDtypeStruct((pool + 8, pool), F32)],
        [pltpu.VMEM((CHUNK, sgu), F32)], ("arbitrary",),
        (x, g, wb, z, z, z, dxo, dxo, wc, hc_saved, hc_saved, p["sgu_ln_g"], p["sgu_ln_b"], p["w_spatial"],
         p["bs_full"], p["conv_w"], p["conv_ln_g"], p["conv_ln_b"], p["bd"], p["pool_scale"]), payload)


def _all_gather(arrs, name, extra=None, to_sum=()):
    gather = _GatherIci(arrs)
    n = len(arrs)
    forward = _GatherForward([jax.ShapeDtypeStruct(s.shape, s.dtype) for s in gather.out_shapes])
    x_in = len(extra.ins) if extra else 0
    x_out = len(extra.out_shapes) if extra else 0
    n_sum = len(to_sum)

    def body(*refs):
        ins, x_ins, s_ins = refs[:n], refs[n:n + x_in], refs[n + x_in:n + x_in + n_sum]
        o0 = n + x_in + n_sum
        outs, x_outs = refs[o0:o0 + n], refs[o0 + n:o0 + n + x_out]
        s_outs = refs[o0 + n + x_out:o0 + n + x_out + n_sum]
        sems = refs[o0 + n + x_out + n_sum:]
        first = gather.build(ins, outs, *sems[0:3])
        first.start()
        if extra:
            beside = extra.build(x_ins, x_outs, *sems[6:9])
            beside.start()
        for s_ref, o_ref in zip(s_ins, s_outs):
            r = o_ref.shape[0]
            acc = s_ref[0:r, :]
            for q in range(1, N_DEV):
                acc = acc + s_ref[q * r:(q + 1) * r, :]
            o_ref[...] = acc
        first.wait()
        second = forward.build(outs, outs, *sems[3:6])
        second.start()
        second.wait()
        if extra:
            beside.wait()

    in_vmem = pl.BlockSpec(memory_space=pltpu.VMEM)
    outs = pl.pallas_call(
        body, name=name,
        in_specs=[ANY] * (n + x_in) + [in_vmem] * n_sum, out_specs=[ANY] * (n + x_out) + [in_vmem] * n_sum,
        out_shape=list(gather.out_shapes) + (list(extra.out_shapes) if extra else [])
        + [jax.ShapeDtypeStruct((s.shape[0] // N_DEV, s.shape[1]), s.dtype) for s in to_sum],
        scratch_shapes=gather.sem_shapes() + forward.sem_shapes() + (extra.sem_shapes() if extra else []),
        compiler_params=_cparams(),
    )(*arrs, *(extra.ins if extra else []), *to_sum)
    return list(outs[:n]), list(outs[n:n + x_out]), list(outs[n + x_out:])


def _all_gather_relayed(arrs, name):
    n = len(arrs)
    n_pairs = 8
    units = [(a, l) for a in range(n) for l in range(arrs[a].shape[0])]

    def body(*refs):
        ins, outs = refs[:n], refs[n:2 * n]
        send_sems, recv_sems, local_sems = refs[2 * n:]
        x, y, c = _position()
        sib, xn, yn = (x, y, 1 - c), (1 - x, y, c), (x, 1 - y, c)
        diag = (1 - x, 1 - y, c)

        def rows(u, px, py, pc, half=None):
            a, l = units[u]
            r = ins[a].shape[1]
            base = (4 * px + 2 * py + pc) * r
            if half is None:
                return outs[a].at[pl.ds(l, 1), pl.ds(base, r), :]
            return outs[a].at[pl.ds(l, 1), pl.ds(base + half * (r // 2), r // 2), :]

        def send(u, k, src, dst, to):
            return _remote(src, dst, send_sems, recv_sems, u * n_pairs + k, to)

        def arrived(u, k, land, sender):
            _remote(land, land, send_sems, recv_sems, u * n_pairs + k, sender).wait_recv()

        own, sent = [], []
        for u, (a, l) in enumerate(units):
            src = ins[a].at[pl.ds(l, 1)]
            own.append(pltpu.make_async_copy(src, rows(u, x, y, c), local_sems.at[u]))
            sent += [send(u, k, src, rows(u, x, y, c), to) for k, to in enumerate((sib, xn, yn))]
        for cp in own + sent:
            cp.start()
        for u in range(len(units)):
            arrived(u, 1, rows(u, *xn), xn)
            arrived(u, 2, rows(u, *yn), yn)
            relay = [send(u, 3, rows(u, *xn, half=0), rows(u, *xn, half=0), yn),
                     send(u, 4, rows(u, *yn, half=1), rows(u, *yn, half=1), xn),
                     send(u, 5, rows(u, *xn), rows(u, *xn), sib),
                     send(u, 6, rows(u, *yn), rows(u, *yn), sib)]
            for cp in relay:
                cp.start()
            sent += relay
        for u in range(len(units)):
            arrived(u, 3, rows(u, *diag, half=0), yn)
            arrived(u, 4, rows(u, *diag, half=1), xn)
            last = send(u, 7, rows(u, *diag), rows(u, *diag), sib)
            last.start()
            sent.append(last)
        for u in range(len(units)):
            arrived(u, 0, rows(u, *sib), sib)
            arrived(u, 5, rows(u, 1 - x, y, 1 - c), sib)
            arrived(u, 6, rows(u, x, 1 - y, 1 - c), sib)
            arrived(u, 7, rows(u, 1 - x, 1 - y, 1 - c), sib)
        for cp in sent:
            cp.wait_send()
        for cp in own:
            cp.wait()

    n_units = len(units)
    return list(pl.pallas_call(
        body, name=name, in_specs=[ANY] * n, out_specs=[ANY] * n,
        out_shape=[jax.ShapeDtypeStruct((a.shape[0], N_DEV * a.shape[1], a.shape[2]), a.dtype) for a in arrs],
        scratch_shapes=[pltpu.SemaphoreType.DMA((n_pairs * n_units,)), pltpu.SemaphoreType.DMA((n_pairs * n_units,)),
                        pltpu.SemaphoreType.DMA((n_units,))],
    )(*arrs))


def _pair_sums(grads, recvs, cidx, name):
    n = len(grads)

    def body(c_ref, *refs):
        for g_ref, r_ref, o_ref in zip(refs[:n], refs[n:2 * n], refs[2 * n:]):
            o_ref[...] = (g_ref[...].astype(F32) + r_ref[...].astype(F32)).astype(BF16)

    shapes = [(g.shape[0] // N_DEV, g.shape[1]) for g in grads]
    return list(pl.pallas_call(
        body, name=name,
        grid_spec=pltpu.PrefetchScalarGridSpec(
            num_scalar_prefetch=1, grid=(N_CHIP,),
            in_specs=[pl.BlockSpec(s, lambda q, c: (2 * q + c[0], 0)) for s in shapes]
            + [pl.BlockSpec(s, lambda q, c: (q, 0)) for s in shapes],
            out_specs=[pl.BlockSpec(s, lambda q, c: (q, 0)) for s in shapes]),
        out_shape=[jax.ShapeDtypeStruct((N_CHIP * r, cols), BF16) for r, cols in shapes],
        compiler_params=_cparams(("parallel",)),
    )(cidx, *grads, *recvs))


def _sum_blocks(parts, nblk, name):
    r = parts.shape[0] // nblk
    cols = parts.shape[1]

    def body(p_ref, o_ref):
        acc = p_ref[0:r, :].astype(F32)
        for q in range(1, nblk):
            acc = acc + p_ref[q * r:(q + 1) * r, :].astype(F32)
        o_ref[...] = acc

    return pl.pallas_call(
        body, name=name,
        out_shape=jax.ShapeDtypeStruct((r, cols), F32),
        compiler_params=_cparams(),
    )(parts)


def _adamw_math(w, g, m, v):
    m = ADAM_B1 * m + (1.0 - ADAM_B1) * g
    v = ADAM_B2 * v + (1.0 - ADAM_B2) * (g * g)
    m_hat = m / (1.0 - ADAM_B1 ** ADAM_STEP)
    v_hat = v / (1.0 - ADAM_B2 ** ADAM_STEP)
    delta = -ADAM_LR * (m_hat / (jnp.sqrt(v_hat) + ADAM_EPS) + ADAM_WD * w)
    return delta, m, v


def _finish_sharded(parts, w, m, v, name):
    depth, rr, cw = w.shape

    def body(*refs):
        p_refs = refs[:depth]
        w_ref, m_ref, v_ref, g_ref, d_ref, mo_ref, vo_ref = refs[depth:]
        l = pl.program_id(0)
        for k in range(depth):
            @pl.when(l == k)
            def _(p_ref=p_refs[k]):
                r = p_ref.shape[0] // N_CHIP
                acc = p_ref[0:r, :].astype(F32)
                for q in range(1, N_CHIP):
                    acc = acc + p_ref[q * r:(q + 1) * r, :].astype(F32)
                g_ref[...] = acc
                d_ref[...], mo_ref[...], vo_ref[...] = _adamw_math(w_ref[...], acc, m_ref[...], v_ref[...])

    blk = pl.BlockSpec((None, rr, cw), lambda l: (l, 0, 0))
    return pl.pallas_call(
        body, name=name, grid=(depth,),
        in_specs=[pl.BlockSpec(p.shape, lambda l: (0, 0)) for p in parts] + [blk] * 3, out_specs=[blk] * 4,
        out_shape=[jax.ShapeDtypeStruct(w.shape, F32)] * 4,
        compiler_params=_cparams(("arbitrary",)),
    )(*parts, w, m, v)


def _adamw_small(ws, gs, ms, vs, name):
    n = len(ws)

    def body(*refs):
        for k in range(n):
            w_ref, g_ref, m_ref, v_ref = (refs[j * n + k] for j in range(4))
            d_ref, mo_ref, vo_ref = (refs[(4 + j) * n + k] for j in range(3))
            d_ref[...], mo_ref[...], vo_ref[...] = _adamw_math(w_ref[...], g_ref[...], m_ref[...], v_ref[...])

    shapes = [jax.ShapeDtypeStruct(w.shape, F32) for w in ws]
    return pl.pallas_call(
        body, name=name, out_shape=shapes * 3, compiler_params=_cparams(),
    )(*ws, *gs, *ms, *vs)


def kernel(x, ffn1_norm, ffn1_w_gate, ffn1_w_up, ffn1_w_down, mix_norm, w_in, sgu_ln_g, sgu_ln_b, w_spatial, b_spatial, conv_w, conv_b, conv_ln_g, conv_ln_b, pool_w, pool_scale, w_out, ffn2_norm, ffn2_w_gate, ffn2_w_up, ffn2_w_down, final_norm, loss_target, m_ffn1_norm, m_ffn1_w_gate, m_ffn1_w_up, m_ffn1_w_down, m_mix_norm, m_w_in, m_sgu_ln_g, m_sgu_ln_b, m_w_spatial, m_b_spatial, m_conv_w, m_conv_b, m_conv_ln_g, m_conv_ln_b, m_pool_w, m_pool_scale, m_w_out, m_ffn2_norm, m_ffn2_w_gate, m_ffn2_w_up, m_ffn2_w_down, m_final_norm, v_ffn1_norm, v_ffn1_w_gate, v_ffn1_w_up, v_ffn1_w_down, v_mix_norm, v_w_in, v_sgu_ln_g, v_sgu_ln_b, v_w_spatial, v_b_spatial, v_conv_w, v_conv_b, v_conv_ln_g, v_conv_ln_b, v_pool_w, v_pool_scale, v_w_out, v_ffn2_norm, v_ffn2_w_gate, v_ffn2_w_up, v_ffn2_w_down, v_final_norm):
    names = ["ffn1_norm", "ffn1_w_gate", "ffn1_w_up", "ffn1_w_down", "mix_norm", "w_in", "sgu_ln_g", "sgu_ln_b",
             "w_spatial", "b_spatial", "conv_w", "conv_b", "conv_ln_g", "conv_ln_b", "pool_w", "pool_scale",
             "w_out", "ffn2_norm", "ffn2_w_gate", "ffn2_w_up", "ffn2_w_down", "final_norm"]
    W = dict(zip(names, [ffn1_norm, ffn1_w_gate, ffn1_w_up, ffn1_w_down, mix_norm, w_in, sgu_ln_g, sgu_ln_b,
                         w_spatial, b_spatial, conv_w, conv_b, conv_ln_g, conv_ln_b, pool_w, pool_scale, w_out,
                         ffn2_norm, ffn2_w_gate, ffn2_w_up, ffn2_w_down, final_norm]))
    M = dict(zip(names, [m_ffn1_norm, m_ffn1_w_gate, m_ffn1_w_up, m_ffn1_w_down, m_mix_norm, m_w_in, m_sgu_ln_g,
                         m_sgu_ln_b, m_w_spatial, m_b_spatial, m_conv_w, m_conv_b, m_conv_ln_g, m_conv_ln_b,
                         m_pool_w, m_pool_scale, m_w_out, m_ffn2_norm, m_ffn2_w_gate, m_ffn2_w_up, m_ffn2_w_down,
                         m_final_norm]))
    V = dict(zip(names, [v_ffn1_norm, v_ffn1_w_gate, v_ffn1_w_up, v_ffn1_w_down, v_mix_norm, v_w_in, v_sgu_ln_g,
                         v_sgu_ln_b, v_w_spatial, v_b_spatial, v_conv_w, v_conv_b, v_conv_ln_g, v_conv_ln_b,
                         v_pool_w, v_pool_scale, v_w_out, v_ffn2_norm, v_ffn2_w_gate, v_ffn2_w_up, v_ffn2_w_down,
                         v_final_norm]))

    depth, d = ffn1_norm.shape
    t = x.shape[1]
    sgu = sgu_ln_g.shape[1]
    pool = pool_scale.shape[1]
    n_head = sgu // HEAD_DIM
    cw_shard = conv_w.shape[2]
    xs = x.reshape(t, d)
    target = loss_target.reshape(t, d)

    def tr(w):
        return jnp.swapaxes(w, 1, 2).astype(BF16)

    ffn_shards = [[jnp.stack([tr(ffn1_w_gate)[l], tr(ffn1_w_up)[l], ffn1_w_down[l].astype(BF16)]),
                   jnp.stack([tr(ffn2_w_gate)[l], tr(ffn2_w_up)[l], ffn2_w_down[l].astype(BF16)])]
                  for l in range(depth)]
    win_shards = [tr(w_in)[l:l + 1] for l in range(depth)]
    wout_shards = [w_out[l:l + 1].astype(BF16) for l in range(depth)]
    cw_rows = depth * CONV_WIDTH
    cw_pad = -cw_rows % 8
    cw_send = jnp.pad(conv_w.reshape(cw_rows, cw_shard), ((0, cw_pad), (0, 0)))[None]
    wffn, wb, wc = {}, {}, {}
    wffn[(0, 0)], wb[0], wc[0], cwg = _all_gather_relayed(
        [ffn_shards[0][0], win_shards[0], wout_shards[0], cw_send], "ag_first")
    conv_w_full = cwg.reshape(N_DEV, cw_rows + cw_pad, cw_shard)[:, :cw_rows].reshape(
        N_DEV, depth, CONV_WIDTH, cw_shard).transpose(1, 2, 0, 3).reshape(depth, CONV_WIDTH, N_DEV * cw_shard)

    def mixer_params(l):
        return dict(
            sgu_ln_g=sgu_ln_g[l:l + 1], sgu_ln_b=sgu_ln_b[l:l + 1], w_spatial=w_spatial[l],
            bs_full=jnp.repeat(b_spatial[l].T, HEAD_DIM, axis=1),
            conv_w=conv_w_full[l], conv_b=conv_b[l:l + 1], conv_ln_g=conv_ln_g[l:l + 1],
            conv_ln_b=conv_ln_b[l:l + 1], bd=_block_diag(pool_w[l]), pool_scale=pool_scale[l:l + 1])

    saved = []
    cur = xs
    for l in range(depth):
        p = mixer_params(l)
        x0 = cur
        more = l + 1 < depth
        (x1, gate1, up1, act1), part = _ffn_fwd(x0, ffn1_norm[l:l + 1], wffn[(l, 0)], f"ffn1_fwd_{l}",
                                          _GatherIci([ffn_shards[l][1]]))
        riding = [_GatherForward(part)] + ([_GatherIci([win_shards[l + 1], wout_shards[l + 1]])] if more else [])
        (x2, z, hc, cat), part = _mixer_fwd(x1, mix_norm[l:l + 1], wb[l], wc[l], p, f"mixer_fwd_{l}", _Merged(riding))
        wffn[(l, 1)] = part[0]
        riding = [_GatherIci([ffn_shards[l + 1][0]]), _GatherForward(part[1:])] if more else []
        relay = _GatherForward(riding[0].out_shapes) if more else None
        outs, part = _ffn_fwd(x2, ffn2_norm[l:l + 1], wffn[(l, 1)], f"ffn2_fwd_{l}",
                              _Merged(riding) if more else None,
                              None if more else (final_norm.reshape(1, d), target), relay)
        if more:
            cur, gate2, up2, act2 = outs
            wffn[(l + 1, 0)], wb[l + 1], wc[l + 1] = part
        else:
            dx, gate2, up2, act2, d_final, loss_part = outs
        saved.append((p, x0, gate1, up1, act1, x1, z, hc, cat, x2, gate2, up2, act2))

    cidx = lax.axis_index("c").astype(jnp.int32).reshape(1)
    from_chips = {}
    to_pair, to_chip = [], []
    small = []

    def pair_payload():
        return _PairExchange([g for _, g in to_pair]) if to_pair else None

    def pair_done(received):
        if to_pair:
            (nm, l), _ = to_pair[0]
            sums = _pair_sums([g for _, g in to_pair], list(received), cidx, f"rs_pair_sum_{nm}_{l}")
            to_chip.extend((key, s) for (key, _), s in zip(to_pair, sums))
        to_pair.clear()

    def take_chip():
        items = list(to_chip)
        to_chip.clear()
        return items

    def chip_payload(items):
        return _ChipExchange([s for _, s in items]) if items else None

    def chip_done(items, landed):
        for (key, _), o in zip(items, landed):
            from_chips[key] = o

    def ffn_weight_grads(prefix, l, dgate, dup, act, h, dy):
        items = take_chip()
        items, later = items[:2], items[2:]
        to_chip.extend(later)
        grads3, landed = _ffn_dw(dgate, dup, act, h, dy, f"dw_{prefix}_{l}", chip_payload(items))
        chip_done(items, landed)
        to_pair.extend(((f"{prefix}_{nm}", l), g) for nm, g in zip(("w_gate", "w_up", "w_down"), grads3))

    for l in reversed(range(depth)):
        p, x0, gate1, up1, act1, x1, z, hc, cat, x2, gate2, up2, act2 = saved[l]
        (dx, dgate, dup, h, dy, dg_ffn2), received = _ffn_bwd(
            x2, ffn2_norm[l:l + 1], dx, gate2, up2, wffn[(l, 1)], f"ffn2_bwd_{l}", pair_payload())
        pair_done(received)
        ffn_weight_grads("ffn2", l, dgate, dup, act2, h, dy)
        g_out, received = _tn_matmul(cat, dx, f"dw_w_out_{l}", pair_payload())
        pair_done(received)
        items = take_chip()
        items, later = items[:3], items[3:]
        to_chip.extend(later)
        (dx, dz, hm, dg_mix, g384, gws, gpool), landed = _mixer_bwd(
            x1, mix_norm[l:l + 1], z, hc, dx, wb[l], wc[l], p, f"mixer_bwd_{l}", chip_payload(items))
        chip_done(items, landed)
        g_in, _ = _tn_matmul(dz, hm, f"dw_w_in_{l}")
        to_pair.extend([(("w_out", l), g_out), (("w_in", l), g_in)])
        if l > 0:
            (dx, dgate, dup, h, dy, dg_ffn1), received = _ffn_bwd(
                x0, ffn1_norm[l:l + 1], dx, gate1, up1, wffn[(l, 0)], f"ffn1_bwd_{l}", pair_payload())
            pair_done(received)
            ffn_weight_grads("ffn1", l, dgate, dup, act1, h, dy)
            small.append((l, g384, gws, gpool, dg_ffn1, dg_mix, dg_ffn2))
            continue

        small.append((0, g384, gws, gpool, None, dg_mix, dg_ffn2))
        small.sort(key=lambda s: s[0])
        norm_rows = []
        for (sl, _, _, _, dg1, dgm, dg2) in small:
            norm_rows += [jnp.zeros((1, d), F32) if dg1 is None else dg1, dgm, dg2]
        norm_rows += [d_final, jnp.pad(loss_part, ((0, 0), (0, d - LANES)))]
        n_norm = len(norm_rows)
        norm_pack = jnp.concatenate(norm_rows + [jnp.zeros((8 - n_norm % 8, d), F32)] * (n_norm % 8 != 0), axis=0)
        parts = [norm_pack]
        for (_, s384, sws, spool, _, _, _) in small:
            parts += [s384, sws.reshape((n_head + 1) * CHUNK, CHUNK), spool]
        n_pair = len(to_pair)
        early = take_chip()
        riding = [pair_payload(), _GatherIci([a[None] for a in parts])] + ([chip_payload(early)] if early else [])
        (dx, dgate, dup, h, dy, dg_ffn1), landed = _ffn_bwd(
            x0, ffn1_norm[l:l + 1], dx, gate1, up1, wffn[(l, 0)], f"ffn1_bwd_{l}", _Merged(riding))
        pair_done(landed[:n_pair])
        chip_done(early, landed[n_pair + len(parts):])
        items = take_chip()
        g_gate, landed = _tn_matmul(
            dgate, h, f"dw_ffn1_w_gate_{l}",
            _Merged([chip_payload(items), _GatherForward(landed[n_pair:n_pair + len(parts)])]))
        chip_done(items, landed[:len(items)])
        gathered = landed[len(items):]
        to_pair.append((("ffn1_w_gate", l), g_gate))
        g_up, received = _tn_matmul(dup, h, f"dw_ffn1_w_up_{l}", pair_payload())
        pair_done(received)
        to_pair.append((("ffn1_w_up", l), g_up))
        items = take_chip()
        g_down, landed = _tn_matmul(act1, dy, f"dw_ffn1_w_down_{l}", _Merged([chip_payload(items), pair_payload()]))
        chip_done(items, landed[:len(items)])
        pair_done(landed[len(items):])
        to_pair.append((("ffn1_w_down", l), g_down))
    grad_x = dx.reshape(x.shape)
    pair_done(_comm(pair_payload(), "rs_pair_exchange_last"))
    items = take_chip()
    (late_norm,), landed, summed = _all_gather([jnp.pad(dg_ffn1, ((0, 7), (0, 0)))[None]], "ag_tail",
                                               chip_payload(items), [g[0] for g in gathered])
    chip_done(items, landed)
    late_sum = _sum_blocks(late_norm[0], N_DEV, "sum_small_late")
    norm_sum = summed[0]
    loss = norm_sum[3 * depth + 1, 0]
    cpos = lax.axis_index("x") * 4 + lax.axis_index("y") * 2 + lax.axis_index("c")
    sg = {nm: [] for nm in names}
    for l in range(depth):
        g384, gws, gpool = summed[1 + 3 * l], summed[2 + 3 * l].reshape(n_head + 1, CHUNK, CHUNK), summed[3 + 3 * l]
        sg["ffn1_norm"].append(norm_sum[3 * l] if l > 0 else late_sum[0])
        sg["mix_norm"].append(norm_sum[3 * l + 1])
        sg["ffn2_norm"].append(norm_sum[3 * l + 2])
        sg["sgu_ln_g"].append(g384[_R_SGU_G])
        sg["sgu_ln_b"].append(g384[_R_SGU_B])
        sg["conv_b"].append(g384[_R_CONV_B])
        sg["conv_ln_g"].append(g384[_R_CLN_G])
        sg["conv_ln_b"].append(g384[_R_CLN_B])
        sg["conv_w"].append(lax.dynamic_slice_in_dim(g384[_R_CONV_W:_R_CONV_W + CONV_WIDTH], cpos * cw_shard,
                                                     cw_shard, axis=1))
        sg["w_spatial"].append(gws[:n_head])
        sg["b_spatial"].append(gws[n_head][:, :n_head].T)
        sg["pool_w"].append(jnp.stack([gpool[k * HEAD_DIM:(k + 1) * HEAD_DIM, k * HEAD_DIM:(k + 1) * HEAD_DIM]
                                       for k in range(pool // HEAD_DIM)], axis=0))
        sg["pool_scale"].append(gpool[pool])
    small_names = ["ffn1_norm", "mix_norm", "sgu_ln_g", "sgu_ln_b", "w_spatial", "b_spatial", "conv_w", "conv_b",
                   "conv_ln_g", "conv_ln_b", "pool_w", "pool_scale", "ffn2_norm"]
    grads = {nm: jnp.stack(sg[nm], axis=0) for nm in small_names}
    grads["final_norm"] = norm_sum[3 * depth]

    delta, new_m, new_v = {}, {}, {}
    big_names = ["ffn1_w_gate", "ffn1_w_up", "ffn1_w_down", "w_in", "w_out", "ffn2_w_gate", "ffn2_w_up",
                 "ffn2_w_down"]
    transposed = {"ffn1_w_gate", "ffn1_w_up", "w_in", "ffn2_w_gate", "ffn2_w_up"}
    for nm in big_names:
        view = (lambda a: jnp.swapaxes(a, 1, 2)) if nm in transposed else (lambda a: a)
        outs = _finish_sharded([from_chips[(nm, l)] for l in range(depth)], view(W[nm]), view(M[nm]), view(V[nm]),
                               f"adamw_{nm}")
        grads[nm], delta[nm], new_m[nm], new_v[nm] = (view(o) for o in outs)
    snames = small_names + ["final_norm"]

    def flat2(a):
        return a.reshape(-1, a.shape[-1])

    outs = _adamw_small([flat2(W[nm]) for nm in snames], [flat2(grads[nm]) for nm in snames],
                        [flat2(M[nm]) for nm in snames], [flat2(V[nm]) for nm in snames], "adamw_small")
    ns = len(snames)
    for k, nm in enumerate(snames):
        shp = W[nm].shape
        delta[nm], new_m[nm], new_v[nm] = (outs[k].reshape(shp), outs[ns + k].reshape(shp),
                                           outs[2 * ns + k].reshape(shp))

    return (loss, grad_x, *[grads[nm] for nm in names], *[delta[nm] for nm in names],
            *[new_m[nm] for nm in names], *[new_v[nm] for nm in names])
```

```python
import functools

import jax
import jax.numpy as jnp
from jax import lax
from jax.experimental import pallas as pl
from jax.experimental.pallas import tpu as pltpu

F32 = jnp.float32
BF16 = jnp.bfloat16
EPS = 1e-6
N_DEV = 8
N_CHIP = 4
MESH = pl.DeviceIdType.MESH
ANY = pl.BlockSpec(memory_space=pl.ANY)

VMEM_LIMIT_BYTES = 56 * 1024 * 1024
LANES = 128
HALO = 32
HEAD_DIM = 64
CHUNK = 128
CONV_WIDTH = 31
POOL_WINDOWS = (2, 4, 8, 16)

ADAM_LR = 0.001
ADAM_B1 = 0.9
ADAM_B2 = 0.999
ADAM_EPS = 1e-08
ADAM_WD = 0.01
ADAM_STEP = 10


def _cparams(sem=None):
    return pltpu.CompilerParams(dimension_semantics=sem, vmem_limit_bytes=VMEM_LIMIT_BYTES)


def _position():
    return lax.axis_index("x"), lax.axis_index("y"), lax.axis_index("c")


class _Copies:
    def __init__(self):
        self.local, self.sends, self.recvs = [], [], []

    def extend(self, other):
        self.local += other.local
        self.sends += other.sends
        self.recvs += other.recvs

    def start(self):
        for cp in self.local + self.sends:
            cp.start()

    def wait(self):
        for land, send_sems, recv_sems, k, peer in self.recvs:
            _remote(land, land, send_sems, recv_sems, k, peer).wait_recv()
        for cp in self.sends:
            cp.wait_send()
        for cp in self.local:
            cp.wait()


def _remote(src, dst, send_sems, recv_sems, k, to):
    return pltpu.make_async_remote_copy(src_ref=src, dst_ref=dst, send_sem=send_sems.at[k], recv_sem=recv_sems.at[k],
                                        device_id=to, device_id_type=MESH)


class _Payload:
    ins, out_shapes, aliases, n_remote, n_local = (), (), {}, 0, 0

    def sem_shapes(self):
        return [pltpu.SemaphoreType.DMA((max(self.n_remote, 1),)), pltpu.SemaphoreType.DMA((max(self.n_remote, 1),)),
                pltpu.SemaphoreType.DMA((max(self.n_local, 1),))]


class _GatherIci(_Payload):
    def __init__(self, shards):
        self.ins = list(shards)
        self.out_shapes = [jax.ShapeDtypeStruct((s.shape[0], N_DEV * s.shape[1], s.shape[2]), s.dtype) for s in shards]
        self.n_remote, self.n_local = 4 * len(shards), len(shards)

    def build(self, ins, outs, send_sems, recv_sems, local_sems, k0=0, l0=0):
        x, y, c = _position()
        peers = [(x, y, 1 - c), (1 - x, y, c), (x, 1 - y, c), (1 - x, 1 - y, c)]
        cps = _Copies()
        for a, (src, out) in enumerate(zip(ins, outs)):
            r = src.shape[1]

            def rows(px, py, pc, out=out, r=r):
                return out.at[:, pl.ds((4 * px + 2 * py + pc) * r, r), :]

            cps.local.append(pltpu.make_async_copy(src, rows(x, y, c), local_sems.at[l0 + a]))
            for k, peer in enumerate(peers):
                cps.sends.append(_remote(src, rows(x, y, c), send_sems, recv_sems, k0 + 4 * a + k, peer))
                cps.recvs.append((rows(*peer), send_sems, recv_sems, k0 + 4 * a + k, peer))
        return cps


class _GatherForward(_Payload):
    def __init__(self, partials):
        self.ins = list(partials)
        self.out_shapes = [jax.ShapeDtypeStruct(p.shape, p.dtype) for p in partials]
        self.aliases = {a: a for a in range(len(partials))}
        self.n_remote = 3 * len(partials)

    def build(self, ins, outs, send_sems, recv_sems, local_sems, k0=0, l0=0):
        x, y, c = _position()
        chips = [(1 - x, y), (x, 1 - y), (1 - x, 1 - y)]
        cps = _Copies()
        for a, out in enumerate(outs):
            r = out.shape[1] // N_DEV
            for k, (px, py) in enumerate(chips):
                mine = out.at[:, pl.ds((4 * px + 2 * py + c) * r, r), :]
                theirs = out.at[:, pl.ds((4 * px + 2 * py + 1 - c) * r, r), :]
                cps.sends.append(_remote(mine, mine, send_sems, recv_sems, k0 + 3 * a + k, (x, y, 1 - c)))
                cps.recvs.append((theirs, send_sems, recv_sems, k0 + 3 * a + k, (x, y, 1 - c)))
        return cps


class _PairExchange(_Payload):
    def __init__(self, grads):
        self.ins = list(grads)
        self.out_shapes = [jax.ShapeDtypeStruct((g.shape[0] // 2, g.shape[1]), g.dtype) for g in grads]
        self.n_remote = N_CHIP * len(grads)

    def build(self, ins, outs, send_sems, recv_sems, local_sems, k0=0, l0=0):
        x, y, c = _position()
        cps = _Copies()
        for a, (src, out) in enumerate(zip(ins, outs)):
            r = src.shape[0] // N_DEV
            for q in range(N_CHIP):
                land = out.at[pl.ds(q * r, r), :]
                cps.sends.append(_remote(src.at[pl.ds((2 * q + 1 - c) * r, r), :], land, send_sems, recv_sems,
                                         k0 + N_CHIP * a + q, (x, y, 1 - c)))
                cps.recvs.append((land, send_sems, recv_sems, k0 + N_CHIP * a + q, (x, y, 1 - c)))
        return cps


class _ChipExchange(_Payload):
    def __init__(self, sums):
        self.ins = list(sums)
        self.out_shapes = [jax.ShapeDtypeStruct(s.shape, s.dtype) for s in sums]
        self.n_remote, self.n_local = 3 * len(sums), len(sums)

    def build(self, ins, outs, send_sems, recv_sems, local_sems, k0=0, l0=0):
        x, y, c = _position()
        my_chip = 2 * x + y
        chips = [(1 - x, y), (x, 1 - y), (1 - x, 1 - y)]
        cps = _Copies()
        for a, (src, out) in enumerate(zip(ins, outs)):
            r = src.shape[0] // N_CHIP
            mine = out.at[pl.ds(my_chip * r, r), :]
            cps.local.append(pltpu.make_async_copy(src.at[pl.ds(my_chip * r, r), :], mine, local_sems.at[l0 + a]))
            for k, (px, py) in enumerate(chips):
                land = out.at[pl.ds((2 * px + py) * r, r), :]
                cps.sends.append(_remote(src.at[pl.ds((2 * px + py) * r, r), :], mine, send_sems, recv_sems,
                                         k0 + 3 * a + k, (px, py, c)))
                cps.recvs.append((land, send_sems, recv_sems, k0 + 3 * a + k, (px, py, c)))
        return cps


class _Merged(_Payload):
    def __init__(self, parts):
        self.parts = list(parts)
        self.ins = [a for p in parts for a in p.ins]
        self.out_shapes = [s for p in parts for s in p.out_shapes]
        self.aliases, self.offsets = {}, []
        i0 = o0 = k0 = l0 = 0
        for p in parts:
            self.offsets.append((i0, o0, k0, l0))
            self.aliases.update({i0 + i: o0 + o for i, o in p.aliases.items()})
            i0, o0, k0, l0 = i0 + len(p.ins), o0 + len(p.out_shapes), k0 + p.n_remote, l0 + p.n_local
        self.n_remote, self.n_local = k0, l0

    def build(self, ins, outs, send_sems, recv_sems, local_sems):
        cps = _Copies()
        for p, (i0, o0, k0, l0) in zip(self.parts, self.offsets):
            cps.extend(p.build(ins[i0:i0 + len(p.ins)], outs[o0:o0 + len(p.out_shapes)], send_sems, recv_sems,
                               local_sems, k0, l0))
        return cps


RELAY_AT = 0.88


def _call(body, name, grid, in_specs, out_specs, out_shape, scratch_shapes, semantics, args, payload=None,
          relay=None):
    if payload is None:
        outs = pl.pallas_call(body, name=name, grid=grid, in_specs=in_specs, out_specs=out_specs,
                              out_shape=out_shape, scratch_shapes=scratch_shapes,
                              compiler_params=_cparams(semantics))(*args)
        return list(outs), []
    n_in, n_out, n_scr = len(in_specs), len(out_specs), len(scratch_shapes)
    p_in, p_out = len(payload.ins), len(payload.out_shapes)

    def carried(*refs):
        ins, p_ins = refs[:n_in], refs[n_in:n_in + p_in]
        o0 = n_in + p_in
        outs, p_outs = refs[o0:o0 + n_out], refs[o0 + n_out:o0 + n_out + p_out]
        s0 = o0 + n_out + p_out
        scr, sems = refs[s0:s0 + n_scr], refs[s0 + n_scr:s0 + n_scr + 3]
        relay_sems = refs[s0 + n_scr + 3:]
        ids = [pl.program_id(k) for k in range(len(grid))]
        at_first = functools.reduce(jnp.logical_and, [i == 0 for i in ids])
        at_last = functools.reduce(jnp.logical_and, [i == g - 1 for i, g in zip(ids, grid)])

        @pl.when(at_first)
        def _():
            payload.build(p_ins, p_outs, *sems).start()

        body(*ins, *outs, *scr)

        if relay is None:
            @pl.when(at_last)
            def _():
                payload.build(p_ins, p_outs, *sems).wait()
        else:
            n_relay = len(relay.out_shapes)
            step = functools.reduce(lambda acc, ig: acc * ig[1] + ig[0], zip(ids, grid), 0)
            total = functools.reduce(lambda a, b: a * b, grid)

            @pl.when(step == int(RELAY_AT * total))
            def _():
                payload.build(p_ins, p_outs, *sems).wait()
                relay.build(p_outs[:n_relay], p_outs[:n_relay], *relay_sems).start()

            @pl.when(at_last)
            def _():
                relay.build(p_outs[:n_relay], p_outs[:n_relay], *relay_sems).wait()

    outs = pl.pallas_call(
        carried, name=name, grid=grid, in_specs=list(in_specs) + [ANY] * p_in,
        out_specs=list(out_specs) + [ANY] * p_out, out_shape=list(out_shape) + list(payload.out_shapes),
        scratch_shapes=list(scratch_shapes) + payload.sem_shapes() + (relay.sem_shapes() if relay else []),
        input_output_aliases={n_in + i: n_out + o for i, o in payload.aliases.items()},
        compiler_params=_cparams(("arbitrary",) * len(grid)))(*args, *payload.ins)
    return list(outs[:n_out]), list(outs[n_out:])


def _comm(payload, name):
    def body(*refs):
        p_in, p_out = len(payload.ins), len(payload.out_shapes)
        cps = payload.build(refs[:p_in], refs[p_in:p_in + p_out], *refs[p_in + p_out:])
        cps.start()
        cps.wait()

    return list(pl.pallas_call(
        body, name=name, in_specs=[ANY] * len(payload.ins), out_specs=[ANY] * len(payload.out_shapes),
        out_shape=list(payload.out_shapes), scratch_shapes=payload.sem_shapes(),
        input_output_aliases=dict(payload.aliases))(*payload.ins))


def _dot(a, b):
    return jnp.dot(a, b, preferred_element_type=F32)


def _dot_nt(a, b):
    return lax.dot_general(a, b, (((1,), (1,)), ((), ())), preferred_element_type=F32)


def _dot_tn(a, b):
    return lax.dot_general(a, b, (((0,), (0,)), ((), ())), preferred_element_type=F32)


def _split_dot(x, e):
    hi = x.astype(BF16)
    r1 = x - hi.astype(F32)
    mid = r1.astype(BF16)
    lo = (r1 - mid.astype(F32)).astype(BF16)
    return _dot(hi, e) + _dot(mid, e) + _dot(lo, e)


def _rms(x):
    rstd = lax.rsqrt(jnp.mean(x * x, axis=-1, keepdims=True) + EPS)
    return x * rstd, rstd


def _rms_bwd(xhat, rstd, g, dh):
    dxhat = dh * g
    dx = rstd * (dxhat - xhat * jnp.mean(dxhat * xhat, axis=-1, keepdims=True))
    return dx, jnp.sum(dh * xhat, axis=0, keepdims=True)


def _ln(v):
    mu = jnp.mean(v, axis=-1, keepdims=True)
    xc = v - mu
    rstd = lax.rsqrt(jnp.mean(xc * xc, axis=-1, keepdims=True) + EPS)
    return xc * rstd, rstd


def _ln_bwd(vhat, rstd, g, dy):
    dvhat = dy * g
    dv = rstd * (dvhat - jnp.mean(dvhat, axis=-1, keepdims=True)
                 - vhat * jnp.mean(dvhat * vhat, axis=-1, keepdims=True))
    return dv, jnp.sum(dy * vhat, axis=0, keepdims=True), jnp.sum(dy, axis=0, keepdims=True)


_INV_SQRT2 = 0.7071067811865476
_INV_SQRT2PI = 0.3989422804014327


def _gelu(x):
    return 0.5 * x * (1.0 + lax.erf(x * _INV_SQRT2))


def _gelu_grad(x):
    return 0.5 * (1.0 + lax.erf(x * _INV_SQRT2)) + x * jnp.exp(-0.5 * x * x) * _INV_SQRT2PI


def _silu_grad(x):
    s = jax.nn.sigmoid(x)
    return s * (1.0 + x * (1.0 - s))


def _ffn_fwd(x, g, wa, name, payload=None, head=None, relay=None):
    t, d = x.shape
    f = wa.shape[1]
    tm, tf = 1024, 256
    nc = f // tf
    groups = [slice(k * (tm // 2), (k + 1) * (tm // 2)) for k in range(2)]

    def body(x_ref, g_ref, wgu_ref, wd_ref, *rest):
        if head is None:
            xo_ref, gate_ref, up_ref, act_ref, h_scr, acc_scr = rest
        else:
            fg_ref, tgt_ref, xo_ref, gate_ref, up_ref, act_ref, dfg_ref, loss_ref, h_scr, acc_scr = rest
        c = pl.program_id(1)
        if head is not None:
            @pl.when((c == 0) & (pl.program_id(0) == 0))
            def _():
                dfg_ref[...] = jnp.zeros_like(dfg_ref)
                loss_ref[...] = jnp.zeros_like(loss_ref)

        @pl.when(c == 0)
        def _():
            xhat, _ = _rms(x_ref[...])
            h_scr[...] = (xhat * g_ref[...]).astype(BF16)
            acc_scr[...] = jnp.zeros_like(acc_scr)

        wgu, wd = wgu_ref[...].reshape(2 * tf, d), wd_ref[...]
        for rows in groups:
            gu = _dot_nt(h_scr[rows, :], wgu)
            gate, up = gu[:, :tf], gu[:, tf:]
            gate_ref[rows, :] = gate.astype(BF16)
            up_ref[rows, :] = up.astype(BF16)
            act = (gate * jax.nn.sigmoid(gate) * up).astype(BF16)
            act_ref[rows, :] = act
            acc_scr[rows, :] += _dot(act, wd)

        @pl.when(c == nc - 1)
        def _():
            xo = x_ref[...] + 0.5 * acc_scr[...]
            if head is None:
                xo_ref[...] = xo
            else:
                fg = fg_ref[...]
                xhat, rstd = _rms(xo)
                err = xhat * fg - tgt_ref[...]
                dxn, dfg = _rms_bwd(xhat, rstd, fg, err * (1.0 / d))
                xo_ref[...] = dxn
                dfg_ref[...] += dfg
                loss_ref[...] += jnp.zeros_like(loss_ref) + 0.5 * jnp.sum(jnp.mean(err * err, axis=-1, keepdims=True))

    row = pl.BlockSpec((tm, d), lambda i, c: (i, 0))
    vec = pl.BlockSpec((1, d), lambda i, c: (0, 0))
    in_specs = [row, vec, pl.BlockSpec((2, tf, d), lambda i, c: (0, c, 0)),
                pl.BlockSpec((None, tf, d), lambda i, c: (2, c, 0))]
    out_specs = [row] + [pl.BlockSpec((tm, tf), lambda i, c: (i, c))] * 3
    out_shape = [jax.ShapeDtypeStruct((t, d), F32)] + [jax.ShapeDtypeStruct((t, f), BF16)] * 3
    args = (x, g, wa, wa)
    if head is not None:
        in_specs += [vec, row]
        out_specs += [vec, pl.BlockSpec((1, LANES), lambda i, c: (0, 0))]
        out_shape += [jax.ShapeDtypeStruct((1, d), F32), jax.ShapeDtypeStruct((1, LANES), F32)]
        args += tuple(head)
    return _call(
        body, name, (t // tm, nc), in_specs, out_specs, out_shape,
        [pltpu.VMEM((tm, d), BF16), pltpu.VMEM((tm, d), F32)],
        ("parallel" if head is None else "arbitrary", "arbitrary"), args, payload, relay)


def _ffn_bwd(x, g, dxo, gate, up, wa, name, payload=None):
    t, d = x.shape
    f = wa.shape[1]
    tm, tf = 1024, 256
    nc = f // tf
    groups = [slice(k * (tm // 2), (k + 1) * (tm // 2)) for k in range(2)]

    def body(x_ref, g_ref, dxo_ref, gate_ref, up_ref, wgu_ref, wd_ref,
             dx_ref, dgate_ref, dup_ref, h_ref, dy_ref, dg_ref, acc_scr):
        i, c = pl.program_id(0), pl.program_id(1)

        @pl.when(c == 0)
        def _():
            xhat, _ = _rms(x_ref[...])
            h_ref[...] = (xhat * g_ref[...]).astype(BF16)
            dy_ref[...] = (0.5 * dxo_ref[...]).astype(BF16)
            acc_scr[...] = jnp.zeros_like(acc_scr)

        @pl.when((c == 0) & (i == 0))
        def _():
            dg_ref[...] = jnp.zeros_like(dg_ref)

        wg, wu, wd = wgu_ref[0], wgu_ref[1], wd_ref[...]
        for rows in groups:
            gt = gate_ref[rows, :].astype(F32)
            u = up_ref[rows, :].astype(F32)
            s = jax.nn.sigmoid(gt)
            silu = gt * s
            dact = _dot_nt(dy_ref[rows, :], wd)
            dgate = (dact * u * (s * (1.0 + gt * (1.0 - s)))).astype(BF16)
            dup = (dact * silu).astype(BF16)
            dgate_ref[rows, :] = dgate
            dup_ref[rows, :] = dup
            acc_scr[rows, :] += _dot(dgate, wg) + _dot(dup, wu)

        @pl.when(c == nc - 1)
        def _():
            xhat, rstd = _rms(x_ref[...])
            dxn, dg = _rms_bwd(xhat, rstd, g_ref[...], acc_scr[...])
            dx_ref[...] = dxo_ref[...] + dxn
            dg_ref[...] += dg

    row = pl.BlockSpec((tm, d), lambda i, c: (i, 0))
    col = pl.BlockSpec((tm, tf), lambda i, c: (i, c))
    vec = pl.BlockSpec((1, d), lambda i, c: (0, 0))
    return _call(
        body, name, (t // tm, nc),
        [row, vec, row, col, col, pl.BlockSpec((2, tf, d), lambda i, c: (0, c, 0)),
         pl.BlockSpec((None, tf, d), lambda i, c: (2, c, 0))],
        [row, col, col, row, row, vec],
        [jax.ShapeDtypeStruct((t, d), F32), jax.ShapeDtypeStruct((t, f), BF16),
         jax.ShapeDtypeStruct((t, f), BF16),
         jax.ShapeDtypeStruct((t, d), BF16), jax.ShapeDtypeStruct((t, d), BF16),
         jax.ShapeDtypeStruct((1, d), F32)],
        [pltpu.VMEM((tm, d), F32)],
        ("arbitrary", "arbitrary"), (x, g, dxo, gate, up, wa, wa), payload)


def _ffn_dw(dgate, dup, act, h, dy, name, payload=None):
    t, f = dgate.shape
    d = h.shape[1]
    tk = 512
    tmm = f // 2
    nk = t // tk

    def body(dg_ref, du_ref, a_ref, h_ref, dy_ref, og_ref, ou_ref, od_ref, acc_g, acc_u, acc_d):
        k = pl.program_id(1)

        @pl.when(k == 0)
        def _():
            acc_g[...] = jnp.zeros_like(acc_g)
            acc_u[...] = jnp.zeros_like(acc_u)
            acc_d[...] = jnp.zeros_like(acc_d)

        hv = h_ref[...]
        acc_g[...] += _dot_tn(dg_ref[...], hv)
        acc_u[...] += _dot_tn(du_ref[...], hv)
        acc_d[...] += _dot_tn(a_ref[...], dy_ref[...])

        @pl.when(k == nk - 1)
        def _():
            og_ref[...] = acc_g[...].astype(BF16)
            ou_ref[...] = acc_u[...].astype(BF16)
            od_ref[...] = acc_d[...].astype(BF16)

    col = pl.BlockSpec((tk, tmm), lambda j, k: (k, j))
    row = pl.BlockSpec((tk, d), lambda j, k: (k, 0))
    out = pl.BlockSpec((tmm, d), lambda j, k: (j, 0))
    return _call(
        body, name, (f // tmm, nk), [col, col, col, row, row], [out, out, out],
        [jax.ShapeDtypeStruct((f, d), BF16)] * 3, [pltpu.VMEM((tmm, d), F32)] * 3,
        ("parallel", "arbitrary"), (dgate, dup, act, h, dy), payload)


def _tn_matmul(a, b, name, payload=None):
    t, m = a.shape
    n = b.shape[1]
    tk = 1024
    tmm = m // 2 if (m > 2048 and (m // 2) % LANES == 0) else m
    nk = t // tk

    def body(a_ref, b_ref, o_ref, acc_scr):
        k = pl.program_id(1)

        @pl.when(k == 0)
        def _():
            acc_scr[...] = jnp.zeros_like(acc_scr)

        acc_scr[...] += _dot_tn(a_ref[...].astype(BF16), b_ref[...].astype(BF16))

        @pl.when(k == nk - 1)
        def _():
            o_ref[...] = acc_scr[...].astype(BF16)

    (out,), p_outs = _call(
        body, name, (m // tmm, nk),
        [pl.BlockSpec((tk, tmm), lambda j, k: (k, j)), pl.BlockSpec((tk, n), lambda j, k: (k, 0))],
        [pl.BlockSpec((tmm, n), lambda j, k: (j, 0))],
        [jax.ShapeDtypeStruct((m, n), BF16)],
        [pltpu.VMEM((tmm, n), F32)],
        ("parallel", "arbitrary"), (a, b), payload)
    return out, p_outs


def _lane_ids(shape):
    return lax.broadcasted_iota(jnp.int32, shape, 1)


def _tril(w):
    r = lax.broadcasted_iota(jnp.int32, w.shape, 0)
    c = lax.broadcasted_iota(jnp.int32, w.shape, 1)
    return jnp.where(r >= c, w, 0.0)


def _shift_down(x, k):
    return x if k == 0 else pltpu.roll(x, k, 0)


def _shift_up(x, k):
    return x if k == 0 else pltpu.roll(x, x.shape[0] - k, 0)


def _sub_tile_shifts(ext, shift):
    return [shift(ext, b) for b in range(8)]


def _tap(shifted, j, n_out, down):
    a, b = divmod(j, 8)
    r0 = HALO - 8 * a if down else 8 * a
    return shifted[b][r0:r0 + n_out]


def _depthwise(shifted, w, n_out, down):
    acc = None
    for j in range(CONV_WIDTH):
        term = _tap(shifted, j, n_out, down) * w[CONV_WIDTH - 1 - j:CONV_WIDTH - j]
        acc = term if acc is None else acc + term
    return acc


def _conv_wgrad(shifted, dhc, n_out):
    return [jnp.sum(_tap(shifted, CONV_WIDTH - 1 - k, n_out, True) * dhc, axis=0, keepdims=True)
            for k in range(CONV_WIDTH)]


def _window_sums(ext, shift):
    assert POOL_WINDOWS == (2, 4, 8, 16)
    s2 = ext + shift(ext, 1)
    s4 = s2 + shift(s2, 2)
    s8 = s4 + shift(s4, 4)
    s16 = s8 + shift(s8, 8)
    grp = _lane_ids(ext.shape) // HEAD_DIM
    return jnp.where(grp == 0, s2, jnp.where(grp == 1, s4, jnp.where(grp == 2, s8, s16)))


def _pool_count(t0, n, width):
    pos = (lax.broadcasted_iota(jnp.int32, (n, width), 0) + (t0 + 1)).astype(F32)
    grp = _lane_ids((n, width)) // HEAD_DIM
    w0, w1, w2, w3 = (float(w) for w in POOL_WINDOWS)
    win = jnp.where(grp == 0, w0, jnp.where(grp == 1, w1, jnp.where(grp == 2, w2, w3)))
    return jnp.minimum(pos, win)


def _block_diag(pw):
    gn, cg, _ = pw.shape
    rows = []
    for gi in range(gn):
        parts = [pw[gi] if gj == gi else jnp.zeros((cg, cg), pw.dtype) for gj in range(gn)]
        rows.append(jnp.concatenate(parts, axis=1))
    return jnp.concatenate(rows, axis=0)


def _head_pair_mix(w_even, w_odd, v):
    lo = _lane_ids((CHUNK, LANES)) < HEAD_DIM
    return jnp.where(lo, _dot(w_even, v), _dot(w_odd, v))


def _mixer_fwd(x, g, wb, wc, p, name, payload=None):
    t, d = x.shape
    d_in = wb.shape[1]
    sgu = p["sgu_ln_g"].shape[1]
    pool = p["pool_scale"].shape[1]
    d_mix = 2 * sgu + pool
    tm = 512
    n_i = t // tm
    hb = tm // HALO

    def body(x_ref, xp_ref, g_ref, wi_ref, wo_ref, lng_ref, lnb_ref, ws_ref, bs_ref, cw_ref, cb_ref, clg_ref,
             clb_ref, bd_ref, ps_ref, xo_ref, z_ref, hc_ref, cat_ref):
        i = pl.program_id(0)
        first = i == 0
        gain, wi = g_ref[...], wi_ref[...]

        def project(xv):
            xhat, _ = _rms(xv)
            return _dot_nt((xhat * gain).astype(BF16), wi)

        z_main = project(x_ref[...])
        z_ref[...] = z_main
        z_prev = jnp.where(first, 0.0, project(xp_ref[...]))

        lng, lnb = lng_ref[...], lnb_ref[...]
        wt = [_tril(ws_ref[h]).astype(BF16) for h in range(sgu // HEAD_DIM)]
        for n in range(tm // CHUNK):
            rows = slice(n * CHUNK, (n + 1) * CHUNK)
            u = _gelu(z_main[rows, 0:sgu])
            vhat, _ = _ln(_gelu(z_main[rows, sgu:2 * sgu]))
            vn = (vhat * lng + lnb).astype(BF16)
            for gp in range(sgu // LANES):
                ls = slice(gp * LANES, (gp + 1) * LANES)
                mixed = _head_pair_mix(wt[2 * gp], wt[2 * gp + 1], vn[:, ls]) + bs_ref[:, ls]
                cat_ref[rows, ls] = (u[:, ls] * mixed).astype(BF16)

        def glu(zz):
            return zz[:, 2 * sgu:3 * sgu] * jax.nn.sigmoid(zz[:, 3 * sgu:4 * sgu])

        ext = jnp.concatenate([glu(z_prev), glu(z_main)], axis=0)
        hc = _depthwise(_sub_tile_shifts(ext, _shift_down), cw_ref[...], tm, True) + cb_ref[...]
        hc_ref[...] = hc
        hhat, _ = _ln(hc)
        bn = hhat * clg_ref[...] + clb_ref[...]
        cat_ref[:, sgu:2 * sgu] = (bn * jax.nn.sigmoid(bn)).astype(BF16)

        pext = jnp.concatenate([z_prev[:, 4 * sgu:], z_main[:, 4 * sgu:]], axis=0)
        sums = _window_sums(pext, _shift_down)[HALO:]
        pooled = sums / _pool_count(i * tm, tm, pool) - z_main[:, 4 * sgu:]
        mixed_c = _dot(pooled.astype(BF16), bd_ref[...].astype(BF16))
        cat_ref[:, 2 * sgu:] = (mixed_c * ps_ref[...]).astype(BF16)

        xo_ref[...] = x_ref[...] + _dot(cat_ref[...], wo_ref[...])

    def vec(n):
        return pl.BlockSpec((1, n), lambda i: (0, 0))

    return _call(
        body, name, (n_i,),
        [pl.BlockSpec((tm, d), lambda i: (i, 0)),
         pl.BlockSpec((HALO, d), lambda i: (jnp.maximum(i * hb - 1, 0), 0)),
         vec(d),
         pl.BlockSpec((None, d_in, d), lambda i: (0, 0, 0)), pl.BlockSpec((None, d_mix, d), lambda i: (0, 0, 0)),
         vec(sgu), vec(sgu),
         pl.BlockSpec(p["w_spatial"].shape, lambda i: (0, 0, 0)),
         pl.BlockSpec((CHUNK, sgu), lambda i: (0, 0)),
         pl.BlockSpec((CONV_WIDTH, sgu), lambda i: (0, 0)),
         vec(sgu), vec(sgu), vec(sgu),
         pl.BlockSpec((pool, pool), lambda i: (0, 0)), vec(pool)],
        [pl.BlockSpec((tm, d), lambda i: (i, 0)), pl.BlockSpec((tm, d_in), lambda i: (i, 0)),
         pl.BlockSpec((tm, sgu), lambda i: (i, 0)), pl.BlockSpec((tm, d_mix), lambda i: (i, 0))],
        [jax.ShapeDtypeStruct((t, d), F32), jax.ShapeDtypeStruct((t, d_in), F32),
         jax.ShapeDtypeStruct((t, sgu), F32), jax.ShapeDtypeStruct((t, d_mix), BF16)], [], ("parallel",),
        (x, x, g, wb, wc, p["sgu_ln_g"], p["sgu_ln_b"], p["w_spatial"], p["bs_full"], p["conv_w"], p["conv_b"],
         p["conv_ln_g"], p["conv_ln_b"], p["bd"], p["pool_scale"]), payload)


_R_SGU_G, _R_SGU_B, _R_CONV_B, _R_CLN_G, _R_CLN_B, _R_CONV_W = 0, 1, 2, 3, 4, 8
_R384_ROWS = 40


def _mixer_bwd(x, g, z, hc_saved, dxo, wb, wc, p, name, payload=None):
    t, d_in = z.shape
    d = x.shape[1]
    sgu = p["sgu_ln_g"].shape[1]
    pool = p["pool_scale"].shape[1]
    d_mix = 2 * sgu + pool
    n_head = sgu // HEAD_DIM
    tm = 512
    n_i = t // tm
    hb = tm // HALO

    def body(x_ref, g_ref, wi_ref, z_ref, zp_ref, zn_ref, dxo_ref, dxon_ref, wo_ref, hc_ref, hcn_ref, lng_ref,
             lnb_ref, ws_ref, bs_ref, cw_ref, clg_ref, clb_ref, bd_ref, ps_ref,
             dx_ref, dz_ref, hm_ref, dgm_ref, g384_ref, gws_ref, gpool_ref, dbs_scr):
        i = pl.program_id(0)
        first, last = i == 0, i == n_i - 1

        @pl.when(first)
        def _():
            dgm_ref[...] = jnp.zeros_like(dgm_ref)
            g384_ref[...] = jnp.zeros_like(g384_ref)
            gws_ref[...] = jnp.zeros_like(gws_ref)
            gpool_ref[...] = jnp.zeros_like(gpool_ref)
            dbs_scr[...] = jnp.zeros_like(dbs_scr)

        z_main = z_ref[...]
        z_prev = jnp.where(first, 0.0, zp_ref[...])
        z_next = jnp.where(last, 0.0, zn_ref[...])
        wo = wo_ref[...]
        dc_main = _dot_nt(dxo_ref[...].astype(BF16), wo)
        dc_next = jnp.where(last, 0.0, _dot_nt(dxon_ref[...].astype(BF16), wo))

        lng, lnb = lng_ref[...], lnb_ref[...]
        wt = [_tril(ws_ref[h]) for h in range(n_head)]
        wt_b = [w.astype(BF16) for w in wt]
        wtt_b = [w.T.astype(BF16) for w in wt]
        lo = _lane_ids((CHUNK, LANES)) < HEAD_DIM
        d_lng = jnp.zeros((1, sgu), F32)
        d_lnb = jnp.zeros((1, sgu), F32)
        dws = [jnp.zeros((CHUNK, CHUNK), F32) for _ in range(n_head)]
        for n in range(tm // CHUNK):
            rows = slice(n * CHUNK, (n + 1) * CHUNK)
            au, av = z_main[rows, 0:sgu], z_main[rows, sgu:2 * sgu]
            u = _gelu(au)
            vhat, vrstd = _ln(_gelu(av))
            vn = (vhat * lng + lnb).astype(BF16)
            da = dc_main[rows, 0:sgu]
            dmixed = da * u
            dbs_scr[...] += dmixed
            dvn_parts, du_parts = [], []
            for gp in range(sgu // LANES):
                ls = slice(gp * LANES, (gp + 1) * LANES)
                vn_g = vn[:, ls]
                mixed = _head_pair_mix(wt_b[2 * gp], wt_b[2 * gp + 1], vn_g) + bs_ref[:, ls]
                du_parts.append(da[:, ls] * mixed)
                dm_g = dmixed[:, ls]
                dm_b = dm_g.astype(BF16)
                dvn_parts.append(jnp.where(lo, _dot(wtt_b[2 * gp], dm_b), _dot(wtt_b[2 * gp + 1], dm_b)))
                dws[2 * gp] = dws[2 * gp] + _dot_nt(jnp.where(lo, dm_g, 0.0).astype(BF16), vn_g)
                dws[2 * gp + 1] = dws[2 * gp + 1] + _dot_nt(jnp.where(lo, 0.0, dm_g).astype(BF16), vn_g)
            dvn = jnp.concatenate(dvn_parts, axis=1)
            du = jnp.concatenate(du_parts, axis=1)
            dv, dg_n, db_n = _ln_bwd(vhat, vrstd, lng, dvn)
            d_lng = d_lng + dg_n
            d_lnb = d_lnb + db_n
            dz_ref[rows, 0:sgu] = (du * _gelu_grad(au)).astype(BF16)
            dz_ref[rows, sgu:2 * sgu] = (dv * _gelu_grad(av)).astype(BF16)
        for h in range(n_head):
            gws_ref[h] += _tril(dws[h])
        g384_ref[_R_SGU_G:_R_SGU_G + 1, :] += d_lng
        g384_ref[_R_SGU_B:_R_SGU_B + 1, :] += d_lnb

        clg = clg_ref[...]
        bcols = slice(2 * sgu, 4 * sgu)
        zb = jnp.concatenate([z_prev[:, bcols], z_main[:, bcols], z_next[:, bcols]], axis=0)
        bval, bgate = zb[:, 0:sgu], zb[:, sgu:2 * sgu]
        sg = jax.nn.sigmoid(bgate)
        hglu = bval * sg
        n_out = tm + HALO
        hglu_shifts = _sub_tile_shifts(hglu, _shift_down)
        cw = cw_ref[...]
        hc = jnp.concatenate([hc_ref[...], jnp.where(last, 0.0, hcn_ref[...])], axis=0)
        hhat, hrstd = _ln(hc)
        bn = hhat * clg + clb_ref[...]
        db = jnp.concatenate([dc_main[:, sgu:2 * sgu], dc_next[:, sgu:2 * sgu]], axis=0)
        dbn = db * _silu_grad(bn)
        dhc_all, _, _ = _ln_bwd(hhat, hrstd, clg, dbn)
        dbn_m, hhat_m, dhc = dbn[:tm], hhat[:tm], dhc_all[:tm]
        g384_ref[_R_CLN_G:_R_CLN_G + 1, :] += jnp.sum(dbn_m * hhat_m, axis=0, keepdims=True)
        g384_ref[_R_CLN_B:_R_CLN_B + 1, :] += jnp.sum(dbn_m, axis=0, keepdims=True)
        g384_ref[_R_CONV_B:_R_CONV_B + 1, :] += jnp.sum(dhc, axis=0, keepdims=True)
        wrows = _conv_wgrad(hglu_shifts, dhc, tm)
        for k in range(CONV_WIDTH):
            g384_ref[_R_CONV_W + k:_R_CONV_W + k + 1, :] += wrows[k]
        dhglu = _depthwise(_sub_tile_shifts(dhc_all, _shift_up), cw, tm, False)
        bval_m, sg_m = bval[HALO:HALO + tm], sg[HALO:HALO + tm]
        dz_ref[:, 2 * sgu:3 * sgu] = (dhglu * sg_m).astype(BF16)
        dz_ref[:, 3 * sgu:4 * sgu] = (dhglu * bval_m * sg_m * (1.0 - sg_m)).astype(BF16)

        bd_b = bd_ref[...].astype(BF16)
        ps = ps_ref[...]
        p_main = z_main[:, 4 * sgu:]
        pext = jnp.concatenate([z_prev[:, 4 * sgu:], p_main], axis=0)
        cnt = _pool_count(i * tm, n_out, pool)
        pooled = _window_sums(pext, _shift_down)[HALO:] / cnt[:tm] - p_main
        pooled_b = pooled.astype(BF16)
        dcc = jnp.concatenate([dc_main[:, 2 * sgu:], dc_next[:, 2 * sgu:]], axis=0)
        dmix_c = dcc * ps
        mixed_c = _dot(pooled_b, bd_b)
        grp_r = lax.broadcasted_iota(jnp.int32, (pool, pool), 0) // HEAD_DIM
        grp_c = lax.broadcasted_iota(jnp.int32, (pool, pool), 1) // HEAD_DIM
        gpool_ref[0:pool, :] += jnp.where(grp_r == grp_c, _dot_tn(pooled_b, dmix_c[:tm].astype(BF16)), 0.0)
        gpool_ref[pool:pool + 1, :] += jnp.sum(dcc[:tm] * mixed_c, axis=0, keepdims=True)
        dpooled = _dot_nt(dmix_c.astype(BF16), bd_b)
        q = dpooled / cnt
        dp = _window_sums(q, _shift_up)[:tm] - dpooled[:tm]
        dz_ref[:, 4 * sgu:] = dp.astype(BF16)

        gain = g_ref[...]
        xhat, rstd = _rms(x_ref[...])
        hm_ref[...] = (xhat * gain).astype(BF16)
        dxn, dgm = _rms_bwd(xhat, rstd, gain, _dot(dz_ref[...], wi_ref[...]))
        dx_ref[...] = dxo_ref[...] + dxn
        dgm_ref[...] += dgm

        @pl.when(last)
        def _():
            r = lax.broadcasted_iota(jnp.int32, (sgu, LANES), 0)
            c = lax.broadcasted_iota(jnp.int32, (sgu, LANES), 1)
            sel = (r // HEAD_DIM == c).astype(BF16)
            gws_ref[n_head] = _split_dot(dbs_scr[...], sel)

    def vec(n):
        return pl.BlockSpec((1, n), lambda i: (0, 0))

    def prev_map(i):
        return (jnp.maximum(i * hb - 1, 0), 0)

    def next_map(i):
        return (jnp.minimum((i + 1) * hb, n_i * hb - 1), 0)

    return _call(
        body, name, (n_i,),
        [pl.BlockSpec((tm, d), lambda i: (i, 0)), pl.BlockSpec((1, d), lambda i: (0, 0)),
         pl.BlockSpec((None, d_in, d), lambda i: (0, 0, 0)),
         pl.BlockSpec((tm, d_in), lambda i: (i, 0)),
         pl.BlockSpec((HALO, d_in), prev_map), pl.BlockSpec((HALO, d_in), next_map),
         pl.BlockSpec((tm, d), lambda i: (i, 0)), pl.BlockSpec((HALO, d), next_map),
         pl.BlockSpec((None, d_mix, d), lambda i: (0, 0, 0)),
         pl.BlockSpec((tm, sgu), lambda i: (i, 0)), pl.BlockSpec((HALO, sgu), next_map),
         vec(sgu), vec(sgu),
         pl.BlockSpec(p["w_spatial"].shape, lambda i: (0, 0, 0)),
         pl.BlockSpec((CHUNK, sgu), lambda i: (0, 0)),
         pl.BlockSpec((CONV_WIDTH, sgu), lambda i: (0, 0)),
         vec(sgu), vec(sgu),
         pl.BlockSpec((pool, pool), lambda i: (0, 0)), vec(pool)],
        [pl.BlockSpec((tm, d), lambda i: (i, 0)), pl.BlockSpec((tm, d_in), lambda i: (i, 0)),
         pl.BlockSpec((tm, d), lambda i: (i, 0)), pl.BlockSpec((1, d), lambda i: (0, 0)),
         pl.BlockSpec((_R384_ROWS, sgu), lambda i: (0, 0)),
         pl.BlockSpec((n_head + 1, CHUNK, CHUNK), lambda i: (0, 0, 0)),
         pl.BlockSpec((pool + 8, pool), lambda i: (0, 0))],
        [jax.ShapeDtypeStruct((t, d), F32), jax.ShapeDtypeStruct((t, d_in), BF16),
         jax.ShapeDtypeStruct((t, d), BF16), jax.ShapeDtypeStruct((1, d), F32),
         jax.ShapeDtypeStruct((_R384_ROWS, sgu), F32),
         jax.ShapeDtypeStruct((n_head + 1, CHUNK, CHUNK), F32),
         jax.ShapeDtypeStruct((pool + 8, pool), F32)],
        [pltpu.VMEM((CHUNK, sgu), F32)], ("arbitrary",),
        (x, g, wb, z, z, z, dxo, dxo, wc, hc_saved, hc_saved, p["sgu_ln_g"], p["sgu_ln_b"], p["w_spatial"],
         p["bs_full"], p["conv_w"], p["conv_ln_g"], p["conv_ln_b"], p["bd"], p["pool_scale"]), payload)


def _all_gather(arrs, name, extra=None, to_sum=()):
    gather = _GatherIci(arrs)
    n = len(arrs)
    forward = _GatherForward([jax.ShapeDtypeStruct(s.shape, s.dtype) for s in gather.out_shapes])
    x_in = len(extra.ins) if extra else 0
    x_out = len(extra.out_shapes) if extra else 0
    n_sum = len(to_sum)

    def body(*refs):
        ins, x_ins, s_ins = refs[:n], refs[n:n + x_in], refs[n + x_in:n + x_in + n_sum]
        o0 = n + x_in + n_sum
        outs, x_outs = refs[o0:o0 + n], refs[o0 + n:o0 + n + x_out]
        s_outs = refs[o0 + n + x_out:o0 + n + x_out + n_sum]
        sems = refs[o0 + n + x_out + n_sum:]
        first = gather.build(ins, outs, *sems[0:3])
        first.start()
        if extra:
            beside = extra.build(x_ins, x_outs, *sems[6:9])
            beside.start()
        for s_ref, o_ref in zip(s_ins, s_outs):
            r = o_ref.shape[0]
            acc = s_ref[0:r, :]
            for q in range(1, N_DEV):
                acc = acc + s_ref[q * r:(q + 1) * r, :]
            o_ref[...] = acc
        first.wait()
        second = forward.build(outs, outs, *sems[3:6])
        second.start()
        second.wait()
        if extra:
            beside.wait()

    in_vmem = pl.BlockSpec(memory_space=pltpu.VMEM)
    outs = pl.pallas_call(
        body, name=name,
        in_specs=[ANY] * (n + x_in) + [in_vmem] * n_sum, out_specs=[ANY] * (n + x_out) + [in_vmem] * n_sum,
        out_shape=list(gather.out_shapes) + (list(extra.out_shapes) if extra else [])
        + [jax.ShapeDtypeStruct((s.shape[0] // N_DEV, s.shape[1]), s.dtype) for s in to_sum],
        scratch_shapes=gather.sem_shapes() + forward.sem_shapes() + (extra.sem_shapes() if extra else []),
        compiler_params=_cparams(),
    )(*arrs, *(extra.ins if extra else []), *to_sum)
    return list(outs[:n]), list(outs[n:n + x_out]), list(outs[n + x_out:])


def _all_gather_relayed(arrs, name):
    n = len(arrs)
    n_pairs = 8
    units = [(a, l) for a in range(n) for l in range(arrs[a].shape[0])]

    def body(*refs):
        ins, outs = refs[:n], refs[n:2 * n]
        send_sems, recv_sems, local_sems = refs[2 * n:]
        x, y, c = _position()
        sib, xn, yn = (x, y, 1 - c), (1 - x, y, c), (x, 1 - y, c)
        diag = (1 - x, 1 - y, c)

        def rows(u, px, py, pc, half=None):
            a, l = units[u]
            r = ins[a].shape[1]
            base = (4 * px + 2 * py + pc) * r
            if half is None:
                return outs[a].at[pl.ds(l, 1), pl.ds(base, r), :]
            return outs[a].at[pl.ds(l, 1), pl.ds(base + half * (r // 2), r // 2), :]

        def send(u, k, src, dst, to):
            return _remote(src, dst, send_sems, recv_sems, u * n_pairs + k, to)

        def arrived(u, k, land, sender):
            _remote(land, land, send_sems, recv_sems, u * n_pairs + k, sender).wait_recv()

        own, sent = [], []
        for u, (a, l) in enumerate(units):
            src = ins[a].at[pl.ds(l, 1)]
            own.append(pltpu.make_async_copy(src, rows(u, x, y, c), local_sems.at[u]))
            sent += [send(u, k, src, rows(u, x, y, c), to) for k, to in enumerate((sib, xn, yn))]
        for cp in own + sent:
            cp.start()
        for u in range(len(units)):
            arrived(u, 1, rows(u, *xn), xn)
            arrived(u, 2, rows(u, *yn), yn)
            relay = [send(u, 3, rows(u, *xn, half=0), rows(u, *xn, half=0), yn),
                     send(u, 4, rows(u, *yn, half=1), rows(u, *yn, half=1), xn),
                     send(u, 5, rows(u, *xn), rows(u, *xn), sib),
                     send(u, 6, rows(u, *yn), rows(u, *yn), sib)]
            for cp in relay:
                cp.start()
            sent += relay
        for u in range(len(units)):
            arrived(u, 3, rows(u, *diag, half=0), yn)
            arrived(u, 4, rows(u, *diag, half=1), xn)
            last = send(u, 7, rows(u, *diag), rows(u, *diag), sib)
            last.start()
            sent.append(last)
        for u in range(len(units)):
            arrived(u, 0, rows(u, *sib), sib)
            arrived(u, 5, rows(u, 1 - x, y, 1 - c), sib)
            arrived(u, 6, rows(u, x, 1 - y, 1 - c), sib)
            arrived(u, 7, rows(u, 1 - x, 1 - y, 1 - c), sib)
        for cp in sent:
            cp.wait_send()
        for cp in own:
            cp.wait()

    n_units = len(units)
    return list(pl.pallas_call(
        body, name=name, in_specs=[ANY] * n, out_specs=[ANY] * n,
        out_shape=[jax.ShapeDtypeStruct((a.shape[0], N_DEV * a.shape[1], a.shape[2]), a.dtype) for a in arrs],
        scratch_shapes=[pltpu.SemaphoreType.DMA((n_pairs * n_units,)), pltpu.SemaphoreType.DMA((n_pairs * n_units,)),
                        pltpu.SemaphoreType.DMA((n_units,))],
    )(*arrs))


def _pair_sums(grads, recvs, cidx, name):
    n = len(grads)

    def body(c_ref, *refs):
        for g_ref, r_ref, o_ref in zip(refs[:n], refs[n:2 * n], refs[2 * n:]):
            o_ref[...] = (g_ref[...].astype(F32) + r_ref[...].astype(F32)).astype(BF16)

    shapes = [(g.shape[0] // N_DEV, g.shape[1]) for g in grads]
    return list(pl.pallas_call(
        body, name=name,
        grid_spec=pltpu.PrefetchScalarGridSpec(
            num_scalar_prefetch=1, grid=(N_CHIP,),
            in_specs=[pl.BlockSpec(s, lambda q, c: (2 * q + c[0], 0)) for s in shapes]
            + [pl.BlockSpec(s, lambda q, c: (q, 0)) for s in shapes],
            out_specs=[pl.BlockSpec(s, lambda q, c: (q, 0)) for s in shapes]),
        out_shape=[jax.ShapeDtypeStruct((N_CHIP * r, cols), BF16) for r, cols in shapes],
        compiler_params=_cparams(("parallel",)),
    )(cidx, *grads, *recvs))


def _sum_blocks(parts, nblk, name):
    r = parts.shape[0] // nblk
    cols = parts.shape[1]

    def body(p_ref, o_ref):
        acc = p_ref[0:r, :].astype(F32)
        for q in range(1, nblk):
            acc = acc + p_ref[q * r:(q + 1) * r, :].astype(F32)
        o_ref[...] = acc

    return pl.pallas_call(
        body, name=name,
        out_shape=jax.ShapeDtypeStruct((r, cols), F32),
        compiler_params=_cparams(),
    )(parts)


def _adamw_math(w, g, m, v):
    m = ADAM_B1 * m + (1.0 - ADAM_B1) * g
    v = ADAM_B2 * v + (1.0 - ADAM_B2) * (g * g)
    m_hat = m / (1.0 - ADAM_B1 ** ADAM_STEP)
    v_hat = v / (1.0 - ADAM_B2 ** ADAM_STEP)
    delta = -ADAM_LR * (m_hat / (jnp.sqrt(v_hat) + ADAM_EPS) + ADAM_WD * w)
    return delta, m, v


def _finish_sharded(parts, w, m, v, name, payload=None):
    depth, rr, cw = w.shape

    def body(*refs):
        p_refs = refs[:depth]
        w_ref, m_ref, v_ref, g_ref, d_ref, mo_ref, vo_ref = refs[depth:]
        l = pl.program_id(0)
        for k in range(depth):
            @pl.when(l == k)
            def _(p_ref=p_refs[k]):
                r = p_ref.shape[0] // N_CHIP
                acc = p_ref[0:r, :].astype(F32)
                for q in range(1, N_CHIP):
                    acc = acc + p_ref[q * r:(q + 1) * r, :].astype(F32)
                g_ref[...] = acc
                d_ref[...], mo_ref[...], vo_ref[...] = _adamw_math(w_ref[...], acc, m_ref[...], v_ref[...])

    blk = pl.BlockSpec((None, rr, cw), lambda l: (l, 0, 0))
    return _call(
        body, name, (depth,),
        [pl.BlockSpec(p.shape, lambda l: (0, 0)) for p in parts] + [blk] * 3, [blk] * 4,
        [jax.ShapeDtypeStruct(w.shape, F32)] * 4, [], ("arbitrary",), (*parts, w, m, v), payload)


def _adamw_small(ws, gs, ms, vs, name):
    n = len(ws)

    def body(*refs):
        for k in range(n):
            w_ref, g_ref, m_ref, v_ref = (refs[j * n + k] for j in range(4))
            d_ref, mo_ref, vo_ref = (refs[(4 + j) * n + k] for j in range(3))
            d_ref[...], mo_ref[...], vo_ref[...] = _adamw_math(w_ref[...], g_ref[...], m_ref[...], v_ref[...])

    shapes = [jax.ShapeDtypeStruct(w.shape, F32) for w in ws]
    return pl.pallas_call(
        body, name=name, out_shape=shapes * 3, compiler_params=_cparams(),
    )(*ws, *gs, *ms, *vs)


def kernel(x, ffn1_norm, ffn1_w_gate, ffn1_w_up, ffn1_w_down, mix_norm, w_in, sgu_ln_g, sgu_ln_b, w_spatial, b_spatial, conv_w, conv_b, conv_ln_g, conv_ln_b, pool_w, pool_scale, w_out, ffn2_norm, ffn2_w_gate, ffn2_w_up, ffn2_w_down, final_norm, loss_target, m_ffn1_norm, m_ffn1_w_gate, m_ffn1_w_up, m_ffn1_w_down, m_mix_norm, m_w_in, m_sgu_ln_g, m_sgu_ln_b, m_w_spatial, m_b_spatial, m_conv_w, m_conv_b, m_conv_ln_g, m_conv_ln_b, m_pool_w, m_pool_scale, m_w_out, m_ffn2_norm, m_ffn2_w_gate, m_ffn2_w_up, m_ffn2_w_down, m_final_norm, v_ffn1_norm, v_ffn1_w_gate, v_ffn1_w_up, v_ffn1_w_down, v_mix_norm, v_w_in, v_sgu_ln_g, v_sgu_ln_b, v_w_spatial, v_b_spatial, v_conv_w, v_conv_b, v_conv_ln_g, v_conv_ln_b, v_pool_w, v_pool_scale, v_w_out, v_ffn2_norm, v_ffn2_w_gate, v_ffn2_w_up, v_ffn2_w_down, v_final_norm):
    names = ["ffn1_norm", "ffn1_w_gate", "ffn1_w_up", "ffn1_w_down", "mix_norm", "w_in", "sgu_ln_g", "sgu_ln_b",
             "w_spatial", "b_spatial", "conv_w", "conv_b", "conv_ln_g", "conv_ln_b", "pool_w", "pool_scale",
             "w_out", "ffn2_norm", "ffn2_w_gate", "ffn2_w_up", "ffn2_w_down", "final_norm"]
    W = dict(zip(names, [ffn1_norm, ffn1_w_gate, ffn1_w_up, ffn1_w_down, mix_norm, w_in, sgu_ln_g, sgu_ln_b,
                         w_spatial, b_spatial, conv_w, conv_b, conv_ln_g, conv_ln_b, pool_w, pool_scale, w_out,
                         ffn2_norm, ffn2_w_gate, ffn2_w_up, ffn2_w_down, final_norm]))
    M = dict(zip(names, [m_ffn1_norm, m_ffn1_w_gate, m_ffn1_w_up, m_ffn1_w_down, m_mix_norm, m_w_in, m_sgu_ln_g,
                         m_sgu_ln_b, m_w_spatial, m_b_spatial, m_conv_w, m_conv_b, m_conv_ln_g, m_conv_ln_b,
                         m_pool_w, m_pool_scale, m_w_out, m_ffn2_norm, m_ffn2_w_gate, m_ffn2_w_up, m_ffn2_w_down,
                         m_final_norm]))
    V = dict(zip(names, [v_ffn1_norm, v_ffn1_w_gate, v_ffn1_w_up, v_ffn1_w_down, v_mix_norm, v_w_in, v_sgu_ln_g,
                         v_sgu_ln_b, v_w_spatial, v_b_spatial, v_conv_w, v_conv_b, v_conv_ln_g, v_conv_ln_b,
                         v_pool_w, v_pool_scale, v_w_out, v_ffn2_norm, v_ffn2_w_gate, v_ffn2_w_up, v_ffn2_w_down,
                         v_final_norm]))

    depth, d = ffn1_norm.shape
    t = x.shape[1]
    sgu = sgu_ln_g.shape[1]
    pool = pool_scale.shape[1]
    n_head = sgu // HEAD_DIM
    cw_shard = conv_w.shape[2]
    xs = x.reshape(t, d)
    target = loss_target.reshape(t, d)

    def tr(w):
        return jnp.swapaxes(w, 1, 2).astype(BF16)

    ffn_shards = [[jnp.stack([tr(ffn1_w_gate)[l], tr(ffn1_w_up)[l], ffn1_w_down[l].astype(BF16)]),
                   jnp.stack([tr(ffn2_w_gate)[l], tr(ffn2_w_up)[l], ffn2_w_down[l].astype(BF16)])]
                  for l in range(depth)]
    win_shards = [tr(w_in)[l:l + 1] for l in range(depth)]
    wout_shards = [w_out[l:l + 1].astype(BF16) for l in range(depth)]
    cw_rows = depth * CONV_WIDTH
    cw_pad = -cw_rows % 8
    cw_send = jnp.pad(conv_w.reshape(cw_rows, cw_shard), ((0, cw_pad), (0, 0)))[None]
    wffn, wb, wc = {}, {}, {}
    wffn[(0, 0)], wb[0], wc[0], cwg = _all_gather_relayed(
        [ffn_shards[0][0], win_shards[0], wout_shards[0], cw_send], "ag_first")
    conv_w_full = cwg.reshape(N_DEV, cw_rows + cw_pad, cw_shard)[:, :cw_rows].reshape(
        N_DEV, depth, CONV_WIDTH, cw_shard).transpose(1, 2, 0, 3).reshape(depth, CONV_WIDTH, N_DEV * cw_shard)

    def mixer_params(l):
        return dict(
            sgu_ln_g=sgu_ln_g[l:l + 1], sgu_ln_b=sgu_ln_b[l:l + 1], w_spatial=w_spatial[l],
            bs_full=jnp.repeat(b_spatial[l].T, HEAD_DIM, axis=1),
            conv_w=conv_w_full[l], conv_b=conv_b[l:l + 1], conv_ln_g=conv_ln_g[l:l + 1],
            conv_ln_b=conv_ln_b[l:l + 1], bd=_block_diag(pool_w[l]), pool_scale=pool_scale[l:l + 1])

    saved = []
    cur = xs
    for l in range(depth):
        p = mixer_params(l)
        x0 = cur
        more = l + 1 < depth
        (x1, gate1, up1, act1), part = _ffn_fwd(x0, ffn1_norm[l:l + 1], wffn[(l, 0)], f"ffn1_fwd_{l}",
                                          _GatherIci([ffn_shards[l][1]]))
        riding = [_GatherForward(part)] + ([_GatherIci([win_shards[l + 1], wout_shards[l + 1]])] if more else [])
        (x2, z, hc, cat), part = _mixer_fwd(x1, mix_norm[l:l + 1], wb[l], wc[l], p, f"mixer_fwd_{l}", _Merged(riding))
        wffn[(l, 1)] = part[0]
        riding = [_GatherIci([ffn_shards[l + 1][0]]), _GatherForward(part[1:])] if more else []
        relay = _GatherForward(riding[0].out_shapes) if more else None
        outs, part = _ffn_fwd(x2, ffn2_norm[l:l + 1], wffn[(l, 1)], f"ffn2_fwd_{l}",
                              _Merged(riding) if more else None,
                              None if more else (final_norm.reshape(1, d), target), relay)
        if more:
            cur, gate2, up2, act2 = outs
            wffn[(l + 1, 0)], wb[l + 1], wc[l + 1] = part
        else:
            dx, gate2, up2, act2, d_final, loss_part = outs
        saved.append((p, x0, gate1, up1, act1, x1, z, hc, cat, x2, gate2, up2, act2))

    cidx = lax.axis_index("c").astype(jnp.int32).reshape(1)
    from_chips = {}
    to_pair, to_chip = [], []
    small = []

    def pair_payload():
        return _PairExchange([g for _, g in to_pair]) if to_pair else None

    def pair_done(received):
        if to_pair:
            (nm, l), _ = to_pair[0]
            sums = _pair_sums([g for _, g in to_pair], list(received), cidx, f"rs_pair_sum_{nm}_{l}")
            to_chip.extend((key, s) for (key, _), s in zip(to_pair, sums))
        to_pair.clear()

    def take_chip():
        items = list(to_chip)
        to_chip.clear()
        return items

    def chip_payload(items):
        return _ChipExchange([s for _, s in items]) if items else None

    def chip_done(items, landed):
        for (key, _), o in zip(items, landed):
            from_chips[key] = o

    def ffn_weight_grads(prefix, l, dgate, dup, act, h, dy):
        items = take_chip()
        items, later = items[:2], items[2:]
        to_chip.extend(later)
        grads3, landed = _ffn_dw(dgate, dup, act, h, dy, f"dw_{prefix}_{l}", chip_payload(items))
        chip_done(items, landed)
        to_pair.extend(((f"{prefix}_{nm}", l), g) for nm, g in zip(("w_gate", "w_up", "w_down"), grads3))

    for l in reversed(range(depth)):
        p, x0, gate1, up1, act1, x1, z, hc, cat, x2, gate2, up2, act2 = saved[l]
        (dx, dgate, dup, h, dy, dg_ffn2), received = _ffn_bwd(
            x2, ffn2_norm[l:l + 1], dx, gate2, up2, wffn[(l, 1)], f"ffn2_bwd_{l}", pair_payload())
        pair_done(received)
        ffn_weight_grads("ffn2", l, dgate, dup, act2, h, dy)
        g_out, received = _tn_matmul(cat, dx, f"dw_w_out_{l}", pair_payload())
        pair_done(received)
        items = take_chip()
        items, later = items[:3], items[3:]
        to_chip.extend(later)
        (dx, dz, hm, dg_mix, g384, gws, gpool), landed = _mixer_bwd(
            x1, mix_norm[l:l + 1], z, hc, dx, wb[l], wc[l], p, f"mixer_bwd_{l}", chip_payload(items))
        chip_done(items, landed)
        g_in, _ = _tn_matmul(dz, hm, f"dw_w_in_{l}")
        to_pair.extend([(("w_out", l), g_out), (("w_in", l), g_in)])
        if l > 0:
            (dx, dgate, dup, h, dy, dg_ffn1), received = _ffn_bwd(
                x0, ffn1_norm[l:l + 1], dx, gate1, up1, wffn[(l, 0)], f"ffn1_bwd_{l}", pair_payload())
            pair_done(received)
            ffn_weight_grads("ffn1", l, dgate, dup, act1, h, dy)
            small.append((l, g384, gws, gpool, dg_ffn1, dg_mix, dg_ffn2))
            continue

        small.append((0, g384, gws, gpool, None, dg_mix, dg_ffn2))
        small.sort(key=lambda s: s[0])
        norm_rows = []
        for (sl, _, _, _, dg1, dgm, dg2) in small:
            norm_rows += [jnp.zeros((1, d), F32) if dg1 is None else dg1, dgm, dg2]
        norm_rows += [d_final, jnp.pad(loss_part, ((0, 0), (0, d - LANES)))]
        n_norm = len(norm_rows)
        norm_pack = jnp.concatenate(norm_rows + [jnp.zeros((8 - n_norm % 8, d), F32)] * (n_norm % 8 != 0), axis=0)
        parts = [norm_pack]
        for (_, s384, sws, spool, _, _, _) in small:
            parts += [s384, sws.reshape((n_head + 1) * CHUNK, CHUNK), spool]
        n_pair = len(to_pair)
        early = take_chip()
        riding = [pair_payload(), _GatherIci([a[None] for a in parts])] + ([chip_payload(early)] if early else [])
        (dx, dgate, dup, h, dy, dg_ffn1), landed = _ffn_bwd(
            x0, ffn1_norm[l:l + 1], dx, gate1, up1, wffn[(l, 0)], f"ffn1_bwd_{l}", _Merged(riding))
        pair_done(landed[:n_pair])
        chip_done(early, landed[n_pair + len(parts):])
        items = take_chip()
        g_gate, landed = _tn_matmul(
            dgate, h, f"dw_ffn1_w_gate_{l}",
            _Merged([chip_payload(items), _GatherForward(landed[n_pair:n_pair + len(parts)])]))
        chip_done(items, landed[:len(items)])
        gathered = landed[len(items):]
        to_pair.append((("ffn1_w_gate", l), g_gate))
        g_up, received = _tn_matmul(dup, h, f"dw_ffn1_w_up_{l}", pair_payload())
        pair_done(received)
        to_pair.append((("ffn1_w_up", l), g_up))
        items = take_chip()
        g_down, landed = _tn_matmul(act1, dy, f"dw_ffn1_w_down_{l}", _Merged([chip_payload(items), pair_payload()]))
        chip_done(items, landed[:len(items)])
        pair_done(landed[len(items):])
        to_pair.append((("ffn1_w_down", l), g_down))
    grad_x = dx.reshape(x.shape)
    pair_done(_comm(pair_payload(), "rs_pair_exchange_last"))
    items = take_chip()
    big_grads, delta, new_m, new_v = {}, {}, {}, {}
    transposed = {"ffn1_w_gate", "ffn1_w_up", "w_in", "ffn2_w_gate", "ffn2_w_up"}

    def finish(nm, riding=()):
        view = (lambda a: jnp.swapaxes(a, 1, 2)) if nm in transposed else (lambda a: a)
        riding = list(riding)
        outs, landed = _finish_sharded([from_chips[(nm, l)] for l in range(depth)], view(W[nm]), view(M[nm]),
                                       view(V[nm]), f"adamw_{nm}", chip_payload(riding))
        chip_done(riding, landed)
        big_grads[nm], delta[nm], new_m[nm], new_v[nm] = (view(o) for o in outs)

    carriers = ["ffn2_w_gate", "ffn2_w_up", "ffn2_w_down", "w_in", "w_out"]
    for k, nm in enumerate(carriers):
        last = k == len(carriers) - 1
        finish(nm, items[k:] if last else items[k:k + 1])
    (late_norm,), _, summed = _all_gather([jnp.pad(dg_ffn1, ((0, 7), (0, 0)))[None]], "ag_tail",
                                          None, [g[0] for g in gathered])
    late_sum = _sum_blocks(late_norm[0], N_DEV, "sum_small_late")
    norm_sum = summed[0]
    loss = norm_sum[3 * depth + 1, 0]
    cpos = lax.axis_index("x") * 4 + lax.axis_index("y") * 2 + lax.axis_index("c")
    sg = {nm: [] for nm in names}
    for l in range(depth):
        g384, gws, gpool = summed[1 + 3 * l], summed[2 + 3 * l].reshape(n_head + 1, CHUNK, CHUNK), summed[3 + 3 * l]
        sg["ffn1_norm"].append(norm_sum[3 * l] if l > 0 else late_sum[0])
        sg["mix_norm"].append(norm_sum[3 * l + 1])
        sg["ffn2_norm"].append(norm_sum[3 * l + 2])
        sg["sgu_ln_g"].append(g384[_R_SGU_G])
        sg["sgu_ln_b"].append(g384[_R_SGU_B])
        sg["conv_b"].append(g384[_R_CONV_B])
        sg["conv_ln_g"].append(g384[_R_CLN_G])
        sg["conv_ln_b"].append(g384[_R_CLN_B])
        sg["conv_w"].append(lax.dynamic_slice_in_dim(g384[_R_CONV_W:_R_CONV_W + CONV_WIDTH], cpos * cw_shard,
                                                     cw_shard, axis=1))
        sg["w_spatial"].append(gws[:n_head])
        sg["b_spatial"].append(gws[n_head][:, :n_head].T)
        sg["pool_w"].append(jnp.stack([gpool[k * HEAD_DIM:(k + 1) * HEAD_DIM, k * HEAD_DIM:(k + 1) * HEAD_DIM]
                                       for k in range(pool // HEAD_DIM)], axis=0))
        sg["pool_scale"].append(gpool[pool])
    small_names = ["ffn1_norm", "mix_norm", "sgu_ln_g", "sgu_ln_b", "w_spatial", "b_spatial", "conv_w", "conv_b",
                   "conv_ln_g", "conv_ln_b", "pool_w", "pool_scale", "ffn2_norm"]
    grads = {nm: jnp.stack(sg[nm], axis=0) for nm in small_names}
    grads["final_norm"] = norm_sum[3 * depth]

    for nm in ["ffn1_w_gate", "ffn1_w_up", "ffn1_w_down"]:
        finish(nm)
    grads.update(big_grads)
    snames = small_names + ["final_norm"]

    def flat2(a):
        return a.reshape(-1, a.shape[-1])

    outs = _adamw_small([flat2(W[nm]) for nm in snames], [flat2(grads[nm]) for nm in snames],
                        [flat2(M[nm]) for nm in snames], [flat2(V[nm]) for nm in snames], "adamw_small")
    ns = len(snames)
    for k, nm in enumerate(snames):
        shp = W[nm].shape
        delta[nm], new_m[nm], new_v[nm] = (outs[k].reshape(shp), outs[ns + k].reshape(shp),
                                           outs[2 * ns + k].reshape(shp))

    return (loss, grad_x, *[grads[nm] for nm in names], *[delta[nm] for nm in names],
            *[new_m[nm] for nm in names], *[new_v[nm] for nm in names])
```
